```python
import jax, jax.numpy as jnp
from jax import lax
import numpy as np

D_MODEL = 1024
BATCH = 32
SEQ = 2048
DEPTH = 4

BRANCH_WIDTH = D_MODEL // 2
N_BRANCH = 3
GM_CHUNK = 128
GM_GROUPS = 4
GM_GROUP_WIDTH = BRANCH_WIDTH // GM_GROUPS
SB_HEAD_DIM = 64
SB_HEADS = BRANCH_WIDTH // SB_HEAD_DIM
SB_QBLOCK = 128
POOL_WINDOWS = (2, 4, 8, 16)
POOL_GROUPS = len(POOL_WINDOWS)
POOL_GROUP_WIDTH = BRANCH_WIDTH // POOL_GROUPS
D_FF = -(-8 * D_MODEL // (3 * 256)) * 256
N_MOD = 6
EPS = 1e-6
IN_SIZES = (BRANCH_WIDTH,) * 6 + (N_BRANCH * D_MODEL,)
IN_SPLITS = tuple(int(s) for s in np.cumsum(IN_SIZES)[:-1])
IN_COLS = int(sum(IN_SIZES))

kernel_name = "hybrid_gmlp_stickbreak_pool_adaln"


def rmsnorm(x, g):
    xf = x.astype(jnp.float32)
    xf = xf * lax.rsqrt(jnp.mean(xf * xf, axis=-1, keepdims=True) + EPS)
    return xf.astype(x.dtype) * g


def layernorm(x, g, b):
    xf = x.astype(jnp.float32)
    mu = jnp.mean(xf, axis=-1, keepdims=True)
    xc = xf - mu
    xf = xc * lax.rsqrt(jnp.mean(xc * xc, axis=-1, keepdims=True) + EPS)
    return xf.astype(x.dtype) * g + b


def gmlp_mixer(u, v, ln_g, ln_b, w_s, b_s):
    B, S, _ = v.shape
    v = layernorm(v, ln_g, ln_b)
    vc = v.reshape(B, S // GM_CHUNK, GM_CHUNK, GM_GROUPS, GM_GROUP_WIDTH)
    causal = jnp.tril(jnp.ones((GM_CHUNK, GM_CHUNK), dtype=bool))
    w = jnp.where(causal[None], w_s, 0.0)
    s = jnp.einsum('gts,bnsgc->bntgc', w, vc) + b_s.T[:, :, None]
    return u * s.reshape(B, S, BRANCH_WIDTH)


def stick_breaking_attention(q, k, v):
    B, S, _ = q.shape
    to_heads = lambda a: a.reshape(B, S, SB_HEADS, SB_HEAD_DIM).transpose(0, 2, 1, 3)
    q, k, v = to_heads(q), to_heads(k), to_heads(v)
    scale = SB_HEAD_DIM ** -0.5
    outs = []
    for i in range(S // SB_QBLOCK):
        start, end = i * SB_QBLOCK, (i + 1) * SB_QBLOCK
        qb = q[:, :, start:end].astype(jnp.float32)
        kb = k[:, :, :end].astype(jnp.float32)
        z = jnp.einsum('bhqd,bhkd->bhqk', qb, kb) * scale
        t_pos = start + jnp.arange(SB_QBLOCK)[:, None]
        s_pos = jnp.arange(end)[None, :]
        mask = s_pos < t_pos
        log_beta = jax.nn.log_sigmoid(z)
        log_one_minus = jnp.where(mask, log_beta - z, 0.0)
        suffix = lax.cumsum(log_one_minus, axis=3, reverse=True) - log_one_minus
        a = jnp.where(mask, jnp.exp(log_beta + suffix), 0.0)
        outs.append(jnp.einsum('bhqk,bhkd->bhqd', a.astype(v.dtype), v[:, :, :end]))
    o = jnp.concatenate(outs, axis=2)
    return o.transpose(0, 2, 1, 3).reshape(B, S, BRANCH_WIDTH)


def pool_mixer(xp, w_pool, pool_scale):
    B, S, _ = xp.shape
    xg = xp.astype(jnp.float32).reshape(B, S, POOL_GROUPS, POOL_GROUP_WIDTH)
    prefix = jnp.cumsum(xg, axis=1)
    pos = jnp.arange(S)
    diffs = []
    for g, w in enumerate(POOL_WINDOWS):
        pg = prefix[:, :, g]
        lagged = jnp.pad(pg[:, :S - w], ((0, 0), (w, 0), (0, 0)))
        count = jnp.minimum(pos + 1, w).astype(jnp.float32)[None, :, None]
        diffs.append((pg - lagged) / count - xg[:, :, g])
    d = jnp.stack(diffs, axis=2).astype(xp.dtype)
    y = jnp.einsum('bsgc,gcd->bsgd', d, w_pool)
    return y.reshape(B, S, BRANCH_WIDTH) * pool_scale


def _fwd_setup_inputs(seed: int = 0) -> dict:
    key = jax.random.key(seed)
    ks = jax.random.split(key, 20)
    f32 = jnp.float32
    nrm = lambda k, shape, s: jax.random.normal(k, shape, f32) * s
    L, D, W = DEPTH, D_MODEL, BRANCH_WIDTH
    return {
        "x": nrm(ks[0], (BATCH, SEQ, D), 1.0),
        "c": nrm(ks[1], (BATCH, D), 1.0),
        "rms_g1": 1.0 + nrm(ks[2], (L, D), 0.02),
        "rms_g2": 1.0 + nrm(ks[3], (L, D), 0.02),
        "w_ada": nrm(ks[4], (L, D, N_MOD * D), 0.5 * D ** -0.5),
        "b_ada": nrm(ks[5], (L, N_MOD * D), 0.02),
        "w_in": nrm(ks[6], (L, D, IN_COLS), D ** -0.5),
        "gm_ln_g": 1.0 + nrm(ks[7], (L, W), 0.02),
        "gm_ln_b": nrm(ks[8], (L, W), 0.02),
        "gm_w_spatial": nrm(ks[9], (L, GM_GROUPS, GM_CHUNK, GM_CHUNK), GM_CHUNK ** -0.5),
        "gm_b_spatial": 1.0 + nrm(ks[10], (L, GM_GROUPS, GM_CHUNK), 0.02),
        "pool_w": nrm(ks[11], (L, POOL_GROUPS, POOL_GROUP_WIDTH, POOL_GROUP_WIDTH), POOL_GROUP_WIDTH ** -0.5),
        "pool_scale": 1.0 + nrm(ks[12], (L, W), 0.02),
        "w_branch": nrm(ks[13], (L, N_BRANCH, W, D), W ** -0.5),
        "w_out": nrm(ks[14], (L, D, D), D ** -0.5),
        "w_ffn_in": nrm(ks[15], (L, D, 2 * D_FF), D ** -0.5),
        "w_ffn_out": nrm(ks[16], (L, D_FF, D), D_FF ** -0.5),
        "final_g": 1.0 + nrm(ks[17], (D,), 0.02),
    }


def _fwd_reference(x, c, rms_g1, rms_g2, w_ada, b_ada, w_in, gm_ln_g, gm_ln_b, gm_w_spatial, gm_b_spatial,
              pool_w, pool_scale, w_branch, w_out, w_ffn_in, w_ffn_out, final_g):
    B, S, D = x.shape
    c_act = jax.nn.silu(c)
    for l in range(DEPTH):
        mod = c_act @ w_ada[l] + b_ada[l]
        sh1, sc1, gt1, sh2, sc2, gt2 = [m[:, None, :] for m in jnp.split(mod, N_MOD, axis=-1)]

        h = rmsnorm(x, rms_g1[l]) * (1.0 + sc1) + sh1
        proj = h @ w_in[l]
        gm_u, gm_v, sb_q, sb_k, sb_v, pool_in, gate_logits = jnp.split(proj, IN_SPLITS, axis=-1)
        branches = (
            gmlp_mixer(jax.nn.gelu(gm_u), jax.nn.gelu(gm_v), gm_ln_g[l], gm_ln_b[l],
                       gm_w_spatial[l], gm_b_spatial[l]),
            stick_breaking_attention(sb_q, sb_k, sb_v),
            pool_mixer(pool_in, pool_w[l], pool_scale[l]),
        )
        gates = jax.nn.sigmoid(gate_logits.reshape(B, S, N_BRANCH, D))
        merged = sum(gates[:, :, n] * (branches[n] @ w_branch[l, n]) for n in range(N_BRANCH))
        x = x + gt1 * (merged @ w_out[l])

        h2 = rmsnorm(x, rms_g2[l]) * (1.0 + sc2) + sh2
        f_gate, f_up = jnp.split(h2 @ w_ffn_in[l], 2, axis=-1)
        x = x + gt2 * ((jax.nn.silu(f_gate) * f_up) @ w_ffn_out[l])
    return rmsnorm(x, final_g)


import jax as _jax
import jax.numpy as _jnp

TWIN_FORMAT = 'train_step'
FWD_PARAMS = ['x', 'c', 'rms_g1', 'rms_g2', 'w_ada', 'b_ada', 'w_in', 'gm_ln_g', 'gm_ln_b', 'gm_w_spatial', 'gm_b_spatial', 'pool_w', 'pool_scale', 'w_branch', 'w_out', 'w_ffn_in', 'w_ffn_out', 'final_g']
TWIN_WEIGHTS = ['rms_g1', 'rms_g2', 'w_ada', 'b_ada', 'w_in', 'gm_ln_g', 'gm_ln_b', 'gm_w_spatial', 'gm_b_spatial', 'pool_w', 'pool_scale', 'w_branch', 'w_out', 'w_ffn_in', 'w_ffn_out', 'final_g']
TWIN_DIFF_INPUT = 'x'
TWIN_INPUTS = ['x', 'c', 'rms_g1', 'rms_g2', 'w_ada', 'b_ada', 'w_in', 'gm_ln_g', 'gm_ln_b', 'gm_w_spatial', 'gm_b_spatial', 'pool_w', 'pool_scale', 'w_branch', 'w_out', 'w_ffn_in', 'w_ffn_out', 'final_g', 'loss_target', 'm_rms_g1', 'm_rms_g2', 'm_w_ada', 'm_b_ada', 'm_w_in', 'm_gm_ln_g', 'm_gm_ln_b', 'm_gm_w_spatial', 'm_gm_b_spatial', 'm_pool_w', 'm_pool_scale', 'm_w_branch', 'm_w_out', 'm_w_ffn_in', 'm_w_ffn_out', 'm_final_g', 'v_rms_g1', 'v_rms_g2', 'v_w_ada', 'v_b_ada', 'v_w_in', 'v_gm_ln_g', 'v_gm_ln_b', 'v_gm_w_spatial', 'v_gm_b_spatial', 'v_pool_w', 'v_pool_scale', 'v_w_branch', 'v_w_out', 'v_w_ffn_in', 'v_w_ffn_out', 'v_final_g']
TWIN_OUTPUTS = ['loss', 'grad_x', 'grad_rms_g1', 'grad_rms_g2', 'grad_w_ada', 'grad_b_ada', 'grad_w_in', 'grad_gm_ln_g', 'grad_gm_ln_b', 'grad_gm_w_spatial', 'grad_gm_b_spatial', 'grad_pool_w', 'grad_pool_scale', 'grad_w_branch', 'grad_w_out', 'grad_w_ffn_in', 'grad_w_ffn_out', 'grad_final_g', 'delta_rms_g1', 'delta_rms_g2', 'delta_w_ada', 'delta_b_ada', 'delta_w_in', 'delta_gm_ln_g', 'delta_gm_ln_b', 'delta_gm_w_spatial', 'delta_gm_b_spatial', 'delta_pool_w', 'delta_pool_scale', 'delta_w_branch', 'delta_w_out', 'delta_w_ffn_in', 'delta_w_ffn_out', 'delta_final_g', 'new_m_rms_g1', 'new_m_rms_g2', 'new_m_w_ada', 'new_m_b_ada', 'new_m_w_in', 'new_m_gm_ln_g', 'new_m_gm_ln_b', 'new_m_gm_w_spatial', 'new_m_gm_b_spatial', 'new_m_pool_w', 'new_m_pool_scale', 'new_m_w_branch', 'new_m_w_out', 'new_m_w_ffn_in', 'new_m_w_ffn_out', 'new_m_final_g', 'new_v_rms_g1', 'new_v_rms_g2', 'new_v_w_ada', 'new_v_b_ada', 'new_v_w_in', 'new_v_gm_ln_g', 'new_v_gm_ln_b', 'new_v_gm_w_spatial', 'new_v_gm_b_spatial', 'new_v_pool_w', 'new_v_pool_scale', 'new_v_w_branch', 'new_v_w_out', 'new_v_w_ffn_in', 'new_v_w_ffn_out', 'new_v_final_g']
TWIN_LEAF_KINDS = {'loss': 'loss', 'grad_x': 'grad_x', 'grad_rms_g1': 'grad_w', 'grad_rms_g2': 'grad_w', 'grad_w_ada': 'grad_w', 'grad_b_ada': 'grad_w', 'grad_w_in': 'grad_w', 'grad_gm_ln_g': 'grad_w', 'grad_gm_ln_b': 'grad_w', 'grad_gm_w_spatial': 'grad_w', 'grad_gm_b_spatial': 'grad_w', 'grad_pool_w': 'grad_w', 'grad_pool_scale': 'grad_w', 'grad_w_branch': 'grad_w', 'grad_w_out': 'grad_w', 'grad_w_ffn_in': 'grad_w', 'grad_w_ffn_out': 'grad_w', 'grad_final_g': 'grad_w', 'delta_rms_g1': 'delta_w', 'delta_rms_g2': 'delta_w', 'delta_w_ada': 'delta_w', 'delta_b_ada': 'delta_w', 'delta_w_in': 'delta_w', 'delta_gm_ln_g': 'delta_w', 'delta_gm_ln_b': 'delta_w', 'delta_gm_w_spatial': 'delta_w', 'delta_gm_b_spatial': 'delta_w', 'delta_pool_w': 'delta_w', 'delta_pool_scale': 'delta_w', 'delta_w_branch': 'delta_w', 'delta_w_out': 'delta_w', 'delta_w_ffn_in': 'delta_w', 'delta_w_ffn_out': 'delta_w', 'delta_final_g': 'delta_w', 'new_m_rms_g1': 'new_m', 'new_m_rms_g2': 'new_m', 'new_m_w_ada': 'new_m', 'new_m_b_ada': 'new_m', 'new_m_w_in': 'new_m', 'new_m_gm_ln_g': 'new_m', 'new_m_gm_ln_b': 'new_m', 'new_m_gm_w_spatial': 'new_m', 'new_m_gm_b_spatial': 'new_m', 'new_m_pool_w': 'new_m', 'new_m_pool_scale': 'new_m', 'new_m_w_branch': 'new_m', 'new_m_w_out': 'new_m', 'new_m_w_ffn_in': 'new_m', 'new_m_w_ffn_out': 'new_m', 'new_m_final_g': 'new_m', 'new_v_rms_g1': 'new_v', 'new_v_rms_g2': 'new_v', 'new_v_w_ada': 'new_v', 'new_v_b_ada': 'new_v', 'new_v_w_in': 'new_v', 'new_v_gm_ln_g': 'new_v', 'new_v_gm_ln_b': 'new_v', 'new_v_gm_w_spatial': 'new_v', 'new_v_gm_b_spatial': 'new_v', 'new_v_pool_w': 'new_v', 'new_v_pool_scale': 'new_v', 'new_v_w_branch': 'new_v', 'new_v_w_out': 'new_v', 'new_v_w_ffn_in': 'new_v', 'new_v_w_ffn_out': 'new_v', 'new_v_final_g': 'new_v'}


def _forward(args):
    return _fwd_reference(*[args[k] for k in FWD_PARAMS])


def _output_shape():
    out = _jax.eval_shape(lambda: _forward(_fwd_setup_inputs(0)))
    return out.shape, out.dtype

N_MICROBATCH = 1
ADAM_LR = 0.001
ADAM_B1 = 0.9
ADAM_B2 = 0.999
ADAM_EPS = 1e-08
ADAM_WD = 0.01
ADAM_STEP = 10
PER_EXAMPLE_BATCH_AXIS = {'x': 0, 'c': 0, 'loss_target': 0}
SHARED_INPUTS = []
_WEIGHT_DTYPES = {'rms_g1': _jnp.float32, 'rms_g2': _jnp.float32, 'w_ada': _jnp.float32, 'b_ada': _jnp.float32, 'w_in': _jnp.float32, 'gm_ln_g': _jnp.float32, 'gm_ln_b': _jnp.float32, 'gm_w_spatial': _jnp.float32, 'gm_b_spatial': _jnp.float32, 'pool_w': _jnp.float32, 'pool_scale': _jnp.float32, 'w_branch': _jnp.float32, 'w_out': _jnp.float32, 'w_ffn_in': _jnp.float32, 'w_ffn_out': _jnp.float32, 'final_g': _jnp.float32}
MOMENT_SCALE = {'rms_g1': 7.564892e-02, 'rms_g2': 7.441647e-02, 'w_ada': 7.765713e-02, 'b_ada': 1.321628e-01, 'w_in': 2.989666e-02, 'gm_ln_g': 3.033937e-02, 'gm_ln_b': 3.294646e-02, 'gm_w_spatial': 2.924563e-02, 'gm_b_spatial': 4.470132e-02, 'pool_w': 5.268824e-02, 'pool_scale': 5.576888e-02, 'w_branch': 3.505653e-02, 'w_out': 6.053295e-02, 'w_ffn_in': 3.184874e-02, 'w_ffn_out': 5.210797e-02, 'final_g': 6.414682e+01}


def _to_microbatches(a, axis):
    t = _jnp.moveaxis(a, axis, 0)
    t = t.reshape((N_MICROBATCH, t.shape[0] // N_MICROBATCH) + t.shape[1:])
    return _jnp.moveaxis(t, 1, axis + 1)


def setup_inputs(seed: int = 0) -> dict:
    inp = _fwd_setup_inputs(seed)
    key = _jax.random.fold_in(_jax.random.key(seed), 7919)
    shape, _ = _output_shape()
    out = dict(inp)
    out["loss_target"] = _jax.random.normal(_jax.random.fold_in(key, 0), shape, _jnp.float32)
    for i, name in enumerate(TWIN_WEIGHTS):
        w = inp[name].astype(_jnp.float32)
        if MOMENT_SCALE is None:
            s = _jnp.sqrt(_jnp.mean(_jnp.square(w)) + 1e-30)
        else:
            s = MOMENT_SCALE[name]
        km, kv = _jax.random.split(_jax.random.fold_in(key, i + 1))
        out[name] = w
        out["m_" + name] = s * _jax.random.normal(km, w.shape, _jnp.float32)
        out["v_" + name] = (s * s) * _jax.random.uniform(kv, w.shape, _jnp.float32, 0.5, 1.5)
    if N_MICROBATCH > 1:
        for name, axis in PER_EXAMPLE_BATCH_AXIS.items():
            out[name] = _to_microbatches(out[name], axis)
    return {'x': out['x'], 'c': out['c'], 'rms_g1': out['rms_g1'], 'rms_g2': out['rms_g2'], 'w_ada': out['w_ada'], 'b_ada': out['b_ada'], 'w_in': out['w_in'], 'gm_ln_g': out['gm_ln_g'], 'gm_ln_b': out['gm_ln_b'], 'gm_w_spatial': out['gm_w_spatial'], 'gm_b_spatial': out['gm_b_spatial'], 'pool_w': out['pool_w'], 'pool_scale': out['pool_scale'], 'w_branch': out['w_branch'], 'w_out': out['w_out'], 'w_ffn_in': out['w_ffn_in'], 'w_ffn_out': out['w_ffn_out'], 'final_g': out['final_g'], 'loss_target': out['loss_target'], 'm_rms_g1': out['m_rms_g1'], 'm_rms_g2': out['m_rms_g2'], 'm_w_ada': out['m_w_ada'], 'm_b_ada': out['m_b_ada'], 'm_w_in': out['m_w_in'], 'm_gm_ln_g': out['m_gm_ln_g'], 'm_gm_ln_b': out['m_gm_ln_b'], 'm_gm_w_spatial': out['m_gm_w_spatial'], 'm_gm_b_spatial': out['m_gm_b_spatial'], 'm_pool_w': out['m_pool_w'], 'm_pool_scale': out['m_pool_scale'], 'm_w_branch': out['m_w_branch'], 'm_w_out': out['m_w_out'], 'm_w_ffn_in': out['m_w_ffn_in'], 'm_w_ffn_out': out['m_w_ffn_out'], 'm_final_g': out['m_final_g'], 'v_rms_g1': out['v_rms_g1'], 'v_rms_g2': out['v_rms_g2'], 'v_w_ada': out['v_w_ada'], 'v_b_ada': out['v_b_ada'], 'v_w_in': out['v_w_in'], 'v_gm_ln_g': out['v_gm_ln_g'], 'v_gm_ln_b': out['v_gm_ln_b'], 'v_gm_w_spatial': out['v_gm_w_spatial'], 'v_gm_b_spatial': out['v_gm_b_spatial'], 'v_pool_w': out['v_pool_w'], 'v_pool_scale': out['v_pool_scale'], 'v_w_branch': out['v_w_branch'], 'v_w_out': out['v_w_out'], 'v_w_ffn_in': out['v_w_ffn_in'], 'v_w_ffn_out': out['v_w_ffn_out'], 'v_final_g': out['v_final_g']}


def _loss(weights, diff, rest, loss_target):
    with _jax.named_scope("forward"):
        args = {**rest, TWIN_DIFF_INPUT: diff, **{k: w.astype(_WEIGHT_DTYPES[k]) for k, w in weights.items()}}
        y = _forward(args)
    with _jax.named_scope("loss_head"):
        err = _jnp.square(y.astype(_jnp.float32) - loss_target)
        return 0.5 * _jnp.sum(_jnp.mean(err, axis=-1)) if err.ndim else 0.5 * err


def _adamw(w, g, m, v):
    m = ADAM_B1 * m + (1.0 - ADAM_B1) * g
    v = ADAM_B2 * v + (1.0 - ADAM_B2) * _jnp.square(g)
    m_hat = m / (1.0 - ADAM_B1 ** ADAM_STEP)
    v_hat = v / (1.0 - ADAM_B2 ** ADAM_STEP)
    delta = -ADAM_LR * (m_hat / (_jnp.sqrt(v_hat) + ADAM_EPS) + ADAM_WD * w)
    return delta, m, v


def reference(x, c, rms_g1, rms_g2, w_ada, b_ada, w_in, gm_ln_g, gm_ln_b, gm_w_spatial, gm_b_spatial, pool_w, pool_scale, w_branch, w_out, w_ffn_in, w_ffn_out, final_g, loss_target, m_rms_g1, m_rms_g2, m_w_ada, m_b_ada, m_w_in, m_gm_ln_g, m_gm_ln_b, m_gm_w_spatial, m_gm_b_spatial, m_pool_w, m_pool_scale, m_w_branch, m_w_out, m_w_ffn_in, m_w_ffn_out, m_final_g, v_rms_g1, v_rms_g2, v_w_ada, v_b_ada, v_w_in, v_gm_ln_g, v_gm_ln_b, v_gm_w_spatial, v_gm_b_spatial, v_pool_w, v_pool_scale, v_w_branch, v_w_out, v_w_ffn_in, v_w_ffn_out, v_final_g):
    given = dict(x=x, c=c, rms_g1=rms_g1, rms_g2=rms_g2, w_ada=w_ada, b_ada=b_ada, w_in=w_in, gm_ln_g=gm_ln_g, gm_ln_b=gm_ln_b, gm_w_spatial=gm_w_spatial, gm_b_spatial=gm_b_spatial, pool_w=pool_w, pool_scale=pool_scale, w_branch=w_branch, w_out=w_out, w_ffn_in=w_ffn_in, w_ffn_out=w_ffn_out, final_g=final_g, loss_target=loss_target, m_rms_g1=m_rms_g1, m_rms_g2=m_rms_g2, m_w_ada=m_w_ada, m_b_ada=m_b_ada, m_w_in=m_w_in, m_gm_ln_g=m_gm_ln_g, m_gm_ln_b=m_gm_ln_b, m_gm_w_spatial=m_gm_w_spatial, m_gm_b_spatial=m_gm_b_spatial, m_pool_w=m_pool_w, m_pool_scale=m_pool_scale, m_w_branch=m_w_branch, m_w_out=m_w_out, m_w_ffn_in=m_w_ffn_in, m_w_ffn_out=m_w_ffn_out, m_final_g=m_final_g, v_rms_g1=v_rms_g1, v_rms_g2=v_rms_g2, v_w_ada=v_w_ada, v_b_ada=v_b_ada, v_w_in=v_w_in, v_gm_ln_g=v_gm_ln_g, v_gm_ln_b=v_gm_ln_b, v_gm_w_spatial=v_gm_w_spatial, v_gm_b_spatial=v_gm_b_spatial, v_pool_w=v_pool_w, v_pool_scale=v_pool_scale, v_w_branch=v_w_branch, v_w_out=v_w_out, v_w_ffn_in=v_w_ffn_in, v_w_ffn_out=v_w_ffn_out, v_final_g=v_final_g)
    weights = {n: given[n] for n in TWIN_WEIGHTS}
    shared = {n: given[n] for n in SHARED_INPUTS}
    per_example = {n: given[n] for n in ['x', 'c']}
    grad_fn = _jax.value_and_grad(_loss, argnums=(0, 1))

    def one_microbatch(ex, loss_target):
        ex = dict(ex)
        diff = ex.pop(TWIN_DIFF_INPUT)
        return grad_fn(weights, diff, {**shared, **ex}, loss_target)

    if N_MICROBATCH == 1:
        loss, (grad_w, grad_x) = one_microbatch(per_example, given["loss_target"])
    else:
        def body(carry, xs):
            loss_sum, grad_sum = carry
            l_k, (gw_k, gx_k) = one_microbatch(xs[0], xs[1])
            with _jax.named_scope("update"):
                return (loss_sum + l_k, _jax.tree.map(_jnp.add, grad_sum, gw_k)), gx_k

        init = (_jnp.zeros((), _jnp.float32), _jax.tree.map(_jnp.zeros_like, weights))
        (loss, grad_w), grad_x = _jax.lax.scan(body, init, (per_example, given["loss_target"]))
    with _jax.named_scope("update"):
        delta_w, new_m, new_v = {}, {}, {}
        for n in TWIN_WEIGHTS:
            delta_w[n], new_m[n], new_v[n] = _adamw(weights[n], grad_w[n], given["m_" + n], given["v_" + n])
    return (loss, grad_x, *[grad_w[n] for n in TWIN_WEIGHTS], *[delta_w[n] for n in TWIN_WEIGHTS],
            *[new_m[n] for n in TWIN_WEIGHTS], *[new_v[n] for n in TWIN_WEIGHTS])
```

```python
import functools
import math

import jax
import jax.numpy as jnp
from jax import lax
from jax.experimental import pallas as pl
from jax.experimental.pallas import tpu as pltpu

F32 = jnp.float32
BF16 = jnp.bfloat16
MESH = pl.DeviceIdType.MESH

D = 1024
BW = 512
NB = 3
CH = 128
NG = 4
HD = 64
POOL_WINDOWS = (2, 4, 8, 16)
DFF = 2816
NMOD = 6
EPS = 1e-6
IN_COLS = 6 * D
NDEV = 8
FF_IN_SHARD = 2 * DFF // NDEV
FF_HALF = FF_IN_SHARD // 2
FF_HALF_PAD = 384
FF_IN_PAD = 2 * FF_HALF_PAD
FFP = NDEV // 2 * FF_IN_PAD

ADAM_LR = 0.001
ADAM_B1 = 0.9
ADAM_B2 = 0.999
ADAM_EPS = 1e-08
ADAM_WD = 0.01
ADAM_STEP = 10

VMEM_LIMIT = 48 * 1024 * 1024

NN = (((1,), (0,)), ((), ()))
NT = (((1,), (1,)), ((), ()))
TN = (((0,), (0,)), ((), ()))


def _cp(sem=None):
    return pltpu.CompilerParams(dimension_semantics=sem, vmem_limit_bytes=VMEM_LIMIT)


def _dot(a, b, dims=NN):
    return lax.dot_general(a, b, dims, preferred_element_type=F32)


def _my_index():
    return 4 * lax.axis_index("x") + 2 * lax.axis_index("y") + lax.axis_index("c")


def _peer(k):
    x, y, c = lax.axis_index("x"), lax.axis_index("y"), lax.axis_index("c")
    px = 1 - x if k & 4 else x
    py = 1 - y if k & 2 else y
    pc = 1 - c if k & 1 else c
    return (px, py, pc), 4 * px + 2 * py + pc


def _exchange(xs, name, all_to_all):
    n = len(xs)
    hbm = pl.BlockSpec(memory_space=pl.ANY)

    def body(*refs):
        ins, outs = refs[:n], refs[n:2 * n]
        send_sems, recv_sems, local_sems = refs[2 * n:]
        me = _my_index()
        local = []
        for a in range(n):
            src = ins[a].at[me] if all_to_all else ins[a]
            cp = pltpu.make_async_copy(src, outs[a].at[me], local_sems.at[a])
            cp.start()
            local.append(cp)
        sends = []
        for k in range(1, NDEV):
            dev, idx = _peer(k)
            for a in range(n):
                src = ins[a].at[idx] if all_to_all else ins[a]
                cp = pltpu.make_async_remote_copy(
                    src_ref=src, dst_ref=outs[a].at[me],
                    send_sem=send_sems.at[a * 7 + k - 1], recv_sem=recv_sems.at[a * 7 + k - 1],
                    device_id=dev, device_id_type=MESH)
                cp.start()
                sends.append(cp)
        for cp in sends:
            cp.wait_send()
        for k in range(1, NDEV):
            dev, idx = _peer(k)
            for a in range(n):
                src = ins[a].at[idx] if all_to_all else ins[a]
                pltpu.make_async_remote_copy(
                    src_ref=src, dst_ref=outs[a].at[idx],
                    send_sem=send_sems.at[a * 7 + k - 1], recv_sem=recv_sems.at[a * 7 + k - 1],
                    device_id=dev, device_id_type=MESH).wait_recv()
        for cp in local:
            cp.wait()

    if all_to_all:
        out_shape = [jax.ShapeDtypeStruct(x.shape, x.dtype) for x in xs]
    else:
        out_shape = [jax.ShapeDtypeStruct((NDEV,) + x.shape, x.dtype) for x in xs]
    return pl.pallas_call(
        body, name=name, out_shape=out_shape,
        in_specs=[hbm] * n, out_specs=[hbm] * n,
        scratch_shapes=[pltpu.SemaphoreType.DMA((n * 7,)), pltpu.SemaphoreType.DMA((n * 7,)),
                        pltpu.SemaphoreType.DMA((n,))],
    )(*xs)


def _mm(name, a, b, grid, a_spec, b_spec, o_spec, out_sds, dims, acc_shape):
    nk = grid[2]

    def body(a_ref, b_ref, o_ref, acc_ref):
        k = pl.program_id(2)

        @pl.when(k == 0)
        def _():
            acc_ref[...] = jnp.zeros_like(acc_ref)

        acc_ref[...] += _dot(a_ref[...].astype(BF16), b_ref[...].astype(BF16), dims)

        @pl.when(k == nk - 1)
        def _():
            o_ref[...] = acc_ref[...].astype(o_ref.dtype)

    return pl.pallas_call(
        body, name=name, grid=grid, in_specs=[a_spec, b_spec], out_specs=o_spec, out_shape=out_sds,
        scratch_shapes=[pltpu.VMEM(acc_shape, F32)],
        compiler_params=_cp(("parallel", "parallel", "arbitrary")),
    )(a, b)


def _row_tile(t, want):
    tm = min(t, want)
    assert t % tm == 0
    return tm


def _mm_colblocked(name, a, wg, out_dtype):
    t = a.shape[0]
    tm = _row_tile(t, 1024)
    return _mm(name, a, wg, (t // tm, NDEV, 1),
               pl.BlockSpec((tm, D), lambda i, j, k: (i, 0)),
               pl.BlockSpec((None, D, 768), lambda i, j, k: (j, 0, 0)),
               pl.BlockSpec((tm, 768), lambda i, j, k: (i, j)),
               jax.ShapeDtypeStruct((t, NDEV * 768), out_dtype), NN, (tm, 768))


def _mm_colblocked_nt(name, g, wg, out_dtype):
    t = g.shape[0]
    tm = _row_tile(t, 1024)
    return _mm(name, g, wg, (t // tm, 1, NDEV),
               pl.BlockSpec((tm, 768), lambda i, j, k: (i, k)),
               pl.BlockSpec((None, D, 768), lambda i, j, k: (k, 0, 0)),
               pl.BlockSpec((tm, D), lambda i, j, k: (i, 0)),
               jax.ShapeDtypeStruct((t, D), out_dtype), NT, (tm, D))


def _mm_colblocked_tn(name, a, g):
    t = a.shape[0]
    tk = _row_tile(t, 1024)
    return _mm(name, a, g, (1, NDEV, t // tk),
               pl.BlockSpec((tk, D), lambda i, j, k: (k, 0)),
               pl.BlockSpec((tk, 768), lambda i, j, k: (k, j)),
               pl.BlockSpec((None, D, 768), lambda i, j, k: (j, 0, 0)),
               jax.ShapeDtypeStruct((NDEV, D, 768), BF16), TN, (D, 768))


def _mm_nt(name, a, w, out_dtype, a_col=0, w_lead=None):
    t = a.shape[0]
    if w_lead is None:
        kdim, n = w.shape
        b_spec = pl.BlockSpec((min(kdim, 1024), n), lambda i, j, k: (j, 0))
    else:
        _, kdim, n = w.shape
        b_spec = pl.BlockSpec((None, min(kdim, 1024), n), lambda i, j, k: (w_lead, j, 0))
    tn = min(kdim, 1024)
    tm = _row_tile(t, 1024)
    return _mm(name, a, w, (t // tm, kdim // tn, 1),
               pl.BlockSpec((tm, n), lambda i, j, k: (i, a_col)),
               b_spec,
               pl.BlockSpec((tm, tn), lambda i, j, k: (i, j)),
               jax.ShapeDtypeStruct((t, kdim), out_dtype), NT, (tm, tn))


def _mm_tn(name, a, g, out_dtype=BF16):
    t, kdim = a.shape
    n = g.shape[1]
    tk = _row_tile(t, 1024)
    tm = min(kdim, 1024)
    tn = min(n, 1024)
    return _mm(name, a, g, (kdim // tm, n // tn, t // tk),
               pl.BlockSpec((tk, tm), lambda i, j, k: (k, i)),
               pl.BlockSpec((tk, tn), lambda i, j, k: (k, j)),
               pl.BlockSpec((tm, tn), lambda i, j, k: (i, j)),
               jax.ShapeDtypeStruct((kdim, n), out_dtype), TN, (tm, tn))


def _mm_residual(name, a, w, x, gt, seq):
    t, kdim = a.shape
    tm = _row_tile(seq, 512)
    tn = 512
    tk = min(kdim, 1024)
    nk = kdim // tk
    per = seq // tm

    def body(a_ref, w_ref, x_ref, gt_ref, xo_ref, y_ref, acc_ref):
        k = pl.program_id(2)

        @pl.when(k == 0)
        def _():
            acc_ref[...] = jnp.zeros_like(acc_ref)

        acc_ref[...] += _dot(a_ref[...], w_ref[...])

        @pl.when(k == nk - 1)
        def _():
            y = acc_ref[...]
            xo_ref[...] = x_ref[...] + gt_ref[0] * y
            y_ref[...] = y.astype(BF16)

    return pl.pallas_call(
        body, name=name, grid=(t // tm, D // tn, nk),
        in_specs=[pl.BlockSpec((tm, tk), lambda i, j, k: (i, k)),
                  pl.BlockSpec((tk, tn), lambda i, j, k: (k, j)),
                  pl.BlockSpec((tm, tn), lambda i, j, k: (i, j)),
                  pl.BlockSpec((1, 1, tn), lambda i, j, k: (i // per, 0, j))],
        out_specs=[pl.BlockSpec((tm, tn), lambda i, j, k: (i, j)),
                   pl.BlockSpec((tm, tn), lambda i, j, k: (i, j))],
        out_shape=[jax.ShapeDtypeStruct((t, D), F32), jax.ShapeDtypeStruct((t, D), BF16)],
        scratch_shapes=[pltpu.VMEM((tm, tn), F32)],
        compiler_params=_cp(("parallel", "parallel", "arbitrary")),
    )(a, w, x, gt)


def _ada_fwd(c_all, w_ada, b_blk):
    nl = w_ada.shape[0]
    nb = c_all.shape[0]

    def body(c_ref, w_ref, b_ref, o_ref):
        c = c_ref[...]
        ca = (c * jax.nn.sigmoid(c)).astype(BF16)
        o_ref[...] = _dot(ca, w_ref[...].astype(BF16)) + b_ref[...]

    return pl.pallas_call(
        body, name="ada_fwd", grid=(nl,),
        in_specs=[pl.BlockSpec((nb, D), lambda l: (0, 0)),
                  pl.BlockSpec((None, D, 768), lambda l: (l, 0, 0)),
                  pl.BlockSpec((None, 1, 768), lambda l: (l, 0, 0))],
        out_specs=pl.BlockSpec((None, nb, 768), lambda l: (l, 0, 0)),
        out_shape=jax.ShapeDtypeStruct((nl, nb, 768), F32),
        compiler_params=_cp(("parallel",)),
    )(c_all, w_ada, b_blk)


def _ada_bwd(c_all, dmod_blk):
    nl = dmod_blk.shape[0]
    nb = c_all.shape[0]

    def body(c_ref, d_ref, o_ref):
        c = c_ref[...]
        ca = (c * jax.nn.sigmoid(c)).astype(BF16)
        o_ref[...] = _dot(ca, d_ref[...].astype(BF16), TN)

    return pl.pallas_call(
        body, name="ada_bwd", grid=(nl,),
        in_specs=[pl.BlockSpec((nb, D), lambda l: (0, 0)),
                  pl.BlockSpec((None, nb, 768), lambda l: (l, 0, 0))],
        out_specs=pl.BlockSpec((None, D, 768), lambda l: (l, 0, 0)),
        out_shape=jax.ShapeDtypeStruct((nl, D, 768), F32),
        compiler_params=_cp(("parallel",)),
    )(c_all, dmod_blk)


def _seq_tile(seq):
    return _row_tile(seq, 512)


def _norm_mod_fwd(x, g, sc, sh):
    nb, seq, _ = x.shape
    ts = _seq_tile(seq)

    def body(x_ref, g_ref, sc_ref, sh_ref, h_ref):
        xv = x_ref[0]
        r = lax.rsqrt(jnp.mean(xv * xv, axis=-1, keepdims=True) + EPS)
        h_ref[0] = ((xv * r) * g_ref[...] * (1.0 + sc_ref[0]) + sh_ref[0]).astype(BF16)

    return pl.pallas_call(
        body, name="norm_mod_fwd", grid=(nb, seq // ts),
        in_specs=[pl.BlockSpec((1, ts, D), lambda b, s: (b, s, 0)),
                  pl.BlockSpec((1, D), lambda b, s: (0, 0)),
                  pl.BlockSpec((1, 1, D), lambda b, s: (b, 0, 0)),
                  pl.BlockSpec((1, 1, D), lambda b, s: (b, 0, 0))],
        out_specs=pl.BlockSpec((1, ts, D), lambda b, s: (b, s, 0)),
        out_shape=jax.ShapeDtypeStruct((nb, seq, D), BF16),
        compiler_params=_cp(("parallel", "parallel")),
    )(x, g, sc, sh)


def _norm_mod_bwd(x, dh, dres, g, sc):
    nb, seq, _ = x.shape
    ts = _seq_tile(seq)

    def body(x_ref, dh_ref, dres_ref, g_ref, sc_ref, dx_ref, dsh_ref, dsc_ref, dg_ref):
        @pl.when(pl.program_id(1) == 0)
        def _():
            dsh_ref[...] = jnp.zeros_like(dsh_ref)
            dsc_ref[...] = jnp.zeros_like(dsc_ref)
            dg_ref[...] = jnp.zeros_like(dg_ref)

        xv = x_ref[0]
        dh = dh_ref[0]
        gv = g_ref[...]
        onesc = 1.0 + sc_ref[0]
        r = lax.rsqrt(jnp.mean(xv * xv, axis=-1, keepdims=True) + EPS)
        xh = xv * r
        dsh_ref[0] += jnp.sum(dh, axis=0, keepdims=True)
        dsc_ref[0] += jnp.sum(dh * (xh * gv), axis=0, keepdims=True)
        dg_ref[0] += jnp.sum(dh * onesc * xh, axis=0, keepdims=True)
        dxh = dh * (gv * onesc)
        dx = r * (dxh - xh * jnp.mean(dxh * xh, axis=-1, keepdims=True))
        dx_ref[0] = dres_ref[0] + dx

    vec = jax.ShapeDtypeStruct((nb, 1, D), F32)
    vspec = pl.BlockSpec((1, 1, D), lambda b, s: (b, 0, 0))
    tile = pl.BlockSpec((1, ts, D), lambda b, s: (b, s, 0))
    return pl.pallas_call(
        body, name="norm_mod_bwd", grid=(nb, seq // ts),
        in_specs=[tile, tile, tile, pl.BlockSpec((1, D), lambda b, s: (0, 0)), vspec],
        out_specs=[tile, vspec, vspec, vspec],
        out_shape=[jax.ShapeDtypeStruct((nb, seq, D), F32), vec, vec, vec],
        compiler_params=_cp(("parallel", "arbitrary")),
    )(x, dh, dres, g, sc)


def _gate_bwd(dx, y, gt):
    nb, seq, _ = dx.shape
    ts = _seq_tile(seq)

    def body(dx_ref, y_ref, gt_ref, dy_ref, dgt_ref):
        @pl.when(pl.program_id(1) == 0)
        def _():
            dgt_ref[...] = jnp.zeros_like(dgt_ref)

        d = dx_ref[0]
        dy_ref[0] = (gt_ref[0] * d).astype(BF16)
        dgt_ref[0] += jnp.sum(d * y_ref[0].astype(F32), axis=0, keepdims=True)

    vspec = pl.BlockSpec((1, 1, D), lambda b, s: (b, 0, 0))
    tile = pl.BlockSpec((1, ts, D), lambda b, s: (b, s, 0))
    return pl.pallas_call(
        body, name="gate_bwd", grid=(nb, seq // ts),
        in_specs=[tile, tile, vspec], out_specs=[tile, vspec],
        out_shape=[jax.ShapeDtypeStruct((nb, seq, D), BF16), jax.ShapeDtypeStruct((nb, 1, D), F32)],
        compiler_params=_cp(("parallel", "arbitrary")),
    )(dx, y, gt)


def _loss_head(x, tgt, g):
    nb, seq, _ = x.shape
    ts = _seq_tile(seq)

    def body(x_ref, t_ref, g_ref, dx_ref, loss_ref, dg_ref):
        @pl.when(pl.program_id(1) == 0)
        def _():
            loss_ref[...] = jnp.zeros_like(loss_ref)
            dg_ref[...] = jnp.zeros_like(dg_ref)

        xv = x_ref[0]
        gv = g_ref[...]
        r = lax.rsqrt(jnp.mean(xv * xv, axis=-1, keepdims=True) + EPS)
        xh = xv * r
        err = xh * gv - t_ref[0]
        per_tok = jnp.mean(err * err, axis=-1, keepdims=True)
        loss_ref[0] += 0.5 * jnp.sum(per_tok, axis=0, keepdims=True)
        dy = err * (1.0 / D)
        dg_ref[0] += jnp.sum(dy * xh, axis=0, keepdims=True)
        dxh = dy * gv
        dx_ref[0] = r * (dxh - xh * jnp.mean(dxh * xh, axis=-1, keepdims=True))

    tile = pl.BlockSpec((1, ts, D), lambda b, s: (b, s, 0))
    return pl.pallas_call(
        body, name="loss_head", grid=(nb, seq // ts),
        in_specs=[tile, tile, pl.BlockSpec((1, D), lambda b, s: (0, 0))],
        out_specs=[tile, pl.BlockSpec((1, 1, 128), lambda b, s: (b, 0, 0)),
                   pl.BlockSpec((1, 1, D), lambda b, s: (b, 0, 0))],
        out_shape=[jax.ShapeDtypeStruct((nb, seq, D), F32), jax.ShapeDtypeStruct((nb, 1, 128), F32),
                   jax.ShapeDtypeStruct((nb, 1, D), F32)],
        compiler_params=_cp(("parallel", "arbitrary")),
    )(x, tgt, g)


_GELU_C = math.sqrt(2.0 / math.pi)


def _gelu(x):
    return 0.5 * x * (1.0 + jnp.tanh(_GELU_C * (x + 0.044715 * (x * x * x))))


def _gelu_and_grad(x):
    t = jnp.tanh(_GELU_C * (x + 0.044715 * (x * x * x)))
    y = 0.5 * x * (1.0 + t)
    dy = 0.5 * (1.0 + t) + 0.5 * x * (1.0 - t * t) * (_GELU_C * (1.0 + 3.0 * 0.044715 * (x * x)))
    return y, dy


def _tril_mask():
    row = lax.broadcasted_iota(jnp.int32, (CH, CH), 0)
    col = lax.broadcasted_iota(jnp.int32, (CH, CH), 1)
    return row >= col


def _gmlp_fwd(proj, ln_g, ln_b, ws, bst):
    t = proj.shape[0]
    tm = _row_tile(t, 512)

    def body(u_ref, v_ref, lg_ref, lb_ref, ws_ref, bst_ref, o_ref):
        tril = _tril_mask()
        wm = [jnp.where(tril, ws_ref[g], 0.0).astype(BF16) for g in range(NG)]
        for ch in range(tm // CH):
            rows = slice(ch * CH, (ch + 1) * CH)
            u = _gelu(u_ref[rows, :].astype(F32))
            v = _gelu(v_ref[rows, :].astype(F32))
            mu = jnp.mean(v, axis=-1, keepdims=True)
            xc = v - mu
            rstd = lax.rsqrt(jnp.mean(xc * xc, axis=-1, keepdims=True) + EPS)
            vn = ((xc * rstd) * lg_ref[...] + lb_ref[...]).astype(BF16)
            for g in range(NG):
                cols = slice(g * CH, (g + 1) * CH)
                s = _dot(wm[g], vn[:, cols]) + bst_ref[:, g:g + 1]
                o_ref[rows, cols] = (u[:, cols] * s).astype(BF16)

    return pl.pallas_call(
        body, name="gmlp_fwd", grid=(t // tm,),
        in_specs=[pl.BlockSpec((tm, BW), lambda i: (i, 0)),
                  pl.BlockSpec((tm, BW), lambda i: (i, 1)),
                  pl.BlockSpec((1, BW), lambda i: (0, 0)),
                  pl.BlockSpec((1, BW), lambda i: (0, 0)),
                  pl.BlockSpec((NG, CH, CH), lambda i: (0, 0, 0)),
                  pl.BlockSpec((CH, NG), lambda i: (0, 0))],
        out_specs=pl.BlockSpec((tm, BW), lambda i: (i, 0)),
        out_shape=jax.ShapeDtypeStruct((t, BW), BF16),
        compiler_params=_cp(("parallel",)),
    )(proj, proj, ln_g, ln_b, ws, bst)


def _gmlp_bwd(proj, dout, ln_g, ln_b, ws, bst):
    t = proj.shape[0]
    tm = _row_tile(t, 512)

    def body(u_ref, v_ref, do_ref, lg_ref, lb_ref, ws_ref, bst_ref, dp_ref, gws_ref, gbs_ref, glg_ref, glb_ref):
        @pl.when(pl.program_id(0) == 0)
        def _():
            gws_ref[...] = jnp.zeros_like(gws_ref)
            gbs_ref[...] = jnp.zeros_like(gbs_ref)
            glg_ref[...] = jnp.zeros_like(glg_ref)
            glb_ref[...] = jnp.zeros_like(glb_ref)

        tril = _tril_mask()
        wm = [jnp.where(tril, ws_ref[g], 0.0).astype(BF16) for g in range(NG)]
        ones = jnp.ones((CH, CH), BF16)
        lg = lg_ref[...]
        for ch in range(tm // CH):
            rows = slice(ch * CH, (ch + 1) * CH)
            u, du_fac = _gelu_and_grad(u_ref[rows, :].astype(F32))
            v, dv_fac = _gelu_and_grad(v_ref[rows, :].astype(F32))
            do = do_ref[rows, :].astype(F32)
            mu = jnp.mean(v, axis=-1, keepdims=True)
            xc = v - mu
            rstd = lax.rsqrt(jnp.mean(xc * xc, axis=-1, keepdims=True) + EPS)
            xh = xc * rstd
            vn = (xh * lg + lb_ref[...]).astype(BF16)
            dvn_parts = []
            for g in range(NG):
                cols = slice(g * CH, (g + 1) * CH)
                s = _dot(wm[g], vn[:, cols]) + bst_ref[:, g:g + 1]
                dp_ref[rows, cols] = (do[:, cols] * s * du_fac[:, cols]).astype(BF16)
                ds = (do[:, cols] * u[:, cols]).astype(BF16)
                gws_ref[g] += jnp.where(tril, _dot(ds, vn[:, cols], NT), 0.0)
                gbs_ref[g] += _dot(ds, ones)
                dvn_parts.append(_dot(wm[g], ds, TN))
            dvn = jnp.concatenate(dvn_parts, axis=1)
            glb_ref[...] += jnp.sum(dvn, axis=0, keepdims=True)
            glg_ref[...] += jnp.sum(dvn * xh, axis=0, keepdims=True)
            dxh = dvn * lg
            dv = rstd * (dxh - jnp.mean(dxh, axis=-1, keepdims=True)
                         - xh * jnp.mean(dxh * xh, axis=-1, keepdims=True))
            dp_ref[rows, BW:2 * BW] = (dv * dv_fac).astype(BF16)

    small = pl.BlockSpec((NG, CH, CH), lambda i: (0, 0, 0))
    vec = pl.BlockSpec((1, BW), lambda i: (0, 0))
    return pl.pallas_call(
        body, name="gmlp_bwd", grid=(t // tm,),
        in_specs=[pl.BlockSpec((tm, BW), lambda i: (i, 0)),
                  pl.BlockSpec((tm, BW), lambda i: (i, 1)),
                  pl.BlockSpec((tm, BW), lambda i: (i, 0)),
                  vec, vec, small, pl.BlockSpec((CH, NG), lambda i: (0, 0))],
        out_specs=[pl.BlockSpec((tm, 2 * BW), lambda i: (i, 0)), small, small, vec, vec],
        out_shape=[jax.ShapeDtypeStruct((t, 2 * BW), BF16),
                   jax.ShapeDtypeStruct((NG, CH, CH), F32), jax.ShapeDtypeStruct((NG, CH, CH), F32),
                   jax.ShapeDtypeStruct((1, BW), F32), jax.ShapeDtypeStruct((1, BW), F32)],
        compiler_params=_cp(("arbitrary",)),
    )(proj, proj, dout, ln_g, ln_b, ws, bst)


def _pool_bands():
    row = lax.broadcasted_iota(jnp.int32, (CH, CH), 0)
    col = lax.broadcasted_iota(jnp.int32, (CH, CH), 1)
    cur, prev = [], []
    for w in POOL_WINDOWS:
        cur.append(jnp.where((row >= col) & (row - col < w), 1.0, 0.0).astype(BF16))
        prev.append(jnp.where(row + CH - col < w, 1.0, 0.0).astype(BF16))
    return cur, prev


def _pool_inv_count(r0, w):
    pos = r0 + lax.broadcasted_iota(jnp.int32, (CH, 1), 0)
    return 1.0 / jnp.minimum(pos + 1, w).astype(F32)


def _pool_diff(x_ref, r0, rp, has_prev, cur, prev, g):
    cols = slice(g * CH, (g + 1) * CH)
    xc = x_ref[pl.ds(r0, CH), cols]
    xp = x_ref[pl.ds(rp, CH), cols]
    ws = _dot(cur[g], xc) + has_prev * _dot(prev[g], xp)
    return ws * _pool_inv_count(r0, POOL_WINDOWS[g]) - xc.astype(F32)


def _pool_fwd(proj3, pw, pscale):
    nb, seq, _ = proj3.shape
    nch = seq // CH

    def body(x_ref, pw_ref, ps_ref, o_ref):
        cur, prev = _pool_bands()
        pwb = [pw_ref[g].astype(BF16) for g in range(NG)]

        def chunk(ch, carry):
            r0 = pl.multiple_of(ch * CH, CH)
            rp = pl.multiple_of(jnp.maximum(ch - 1, 0) * CH, CH)
            has_prev = jnp.where(ch > 0, 1.0, 0.0)
            for g in range(NG):
                cols = slice(g * CH, (g + 1) * CH)
                d = _pool_diff(x_ref, r0, rp, has_prev, cur, prev, g)
                y = _dot(d.astype(BF16), pwb[g]) * ps_ref[:, cols]
                o_ref[pl.ds(r0, CH), cols] = y.astype(BF16)
            return carry

        lax.fori_loop(0, nch, chunk, 0)

    return pl.pallas_call(
        body, name="pool_fwd", grid=(nb,),
        in_specs=[pl.BlockSpec((None, seq, BW), lambda b: (b, 0, 5)),
                  pl.BlockSpec((NG, CH, CH), lambda b: (0, 0, 0)),
                  pl.BlockSpec((1, BW), lambda b: (0, 0))],
        out_specs=pl.BlockSpec((None, seq, BW), lambda b: (b, 0, 0)),
        out_shape=jax.ShapeDtypeStruct((nb, seq, BW), BF16),
        compiler_params=_cp(("parallel",)),
    )(proj3, pw, pscale)


def _pool_bwd(proj3, dout3, pw, pscale):
    nb, seq, _ = proj3.shape
    nch = seq // CH

    def body(x_ref, do_ref, pw_ref, ps_ref, dx_ref, gpw_ref, gps_ref, e_ref):
        @pl.when(pl.program_id(0) == 0)
        def _():
            gpw_ref[...] = jnp.zeros_like(gpw_ref)
            gps_ref[...] = jnp.zeros_like(gps_ref)

        cur, prev = _pool_bands()
        pwb = [pw_ref[g].astype(BF16) for g in range(NG)]

        def first(ch, carry):
            r0 = pl.multiple_of(ch * CH, CH)
            rp = pl.multiple_of(jnp.maximum(ch - 1, 0) * CH, CH)
            has_prev = jnp.where(ch > 0, 1.0, 0.0)
            for g in range(NG):
                cols = slice(g * CH, (g + 1) * CH)
                d = _pool_diff(x_ref, r0, rp, has_prev, cur, prev, g).astype(BF16)
                do = do_ref[pl.ds(r0, CH), cols].astype(F32)
                ypre = _dot(d, pwb[g])
                gps_ref[:, cols] += jnp.sum(do * ypre, axis=0, keepdims=True)
                dyp = (do * ps_ref[:, cols]).astype(BF16)
                gpw_ref[g] += _dot(d, dyp, TN)
                e_ref[pl.ds(r0, CH), cols] = _dot(dyp, pwb[g], NT)
            return carry

        lax.fori_loop(0, nch, first, 0)

        def second(ch, carry):
            r0 = pl.multiple_of(ch * CH, CH)
            rn = pl.multiple_of(jnp.minimum(ch + 1, nch - 1) * CH, CH)
            has_next = jnp.where(ch < nch - 1, 1.0, 0.0)
            for g in range(NG):
                cols = slice(g * CH, (g + 1) * CH)
                w = POOL_WINDOWS[g]
                dd = e_ref[pl.ds(r0, CH), cols]
                ec = (dd * _pool_inv_count(r0, w)).astype(BF16)
                en = (e_ref[pl.ds(rn, CH), cols] * _pool_inv_count(rn, w)).astype(BF16)
                dx = _dot(cur[g], ec, TN) + has_next * _dot(prev[g], en, TN) - dd
                dx_ref[pl.ds(r0, CH), cols] = dx.astype(BF16)
            return carry

        lax.fori_loop(0, nch, second, 0)

    small = pl.BlockSpec((NG, CH, CH), lambda b: (0, 0, 0))
    vec = pl.BlockSpec((1, BW), lambda b: (0, 0))
    return pl.pallas_call(
        body, name="pool_bwd", grid=(nb,),
        in_specs=[pl.BlockSpec((None, seq, BW), lambda b: (b, 0, 5)),
                  pl.BlockSpec((None, seq, BW), lambda b: (b, 0, 0)), small, vec],
        out_specs=[pl.BlockSpec((None, seq, BW), lambda b: (b, 0, 0)), small, vec],
        out_shape=[jax.ShapeDtypeStruct((nb, seq, BW), BF16),
                   jax.ShapeDtypeStruct((NG, CH, CH), F32), jax.ShapeDtypeStruct((1, BW), F32)],
        scratch_shapes=[pltpu.VMEM((seq, BW), F32)],
        compiler_params=_cp(("arbitrary",)),
    )(proj3, dout3, pw, pscale)


SB_BQ = 256
SB_BK = 128
SB_SCALE = HD ** -0.5


def _sb_tile(q, k, q0, k0, bq):
    z = _dot(q, k, NT) * SB_SCALE
    tpos = q0 + lax.broadcasted_iota(jnp.int32, (bq, SB_BK), 0)
    spos = k0 + lax.broadcasted_iota(jnp.int32, (bq, SB_BK), 1)
    mask = spos < tpos
    e = jnp.exp(-jnp.abs(z))
    lb = jnp.minimum(z, 0.0) - jnp.log(1.0 + e)
    lom = jnp.where(mask, lb - z, 0.0)
    return z, mask, e, lb, lom


def _dot_hilo(a, m):
    hi = a.astype(BF16)
    lo = (a - hi.astype(F32)).astype(BF16)
    return _dot(hi, m) + _dot(lo, m)


def _sb_fwd(proj3):
    nb, seq, _ = proj3.shape
    bq = min(SB_BQ, seq)
    nq = seq // bq

    def body(q_ref, k_ref, v_ref, o_ref, t_ref):
        row = lax.broadcasted_iota(jnp.int32, (SB_BK, SB_BK), 0)
        col = lax.broadcasted_iota(jnp.int32, (SB_BK, SB_BK), 1)
        upper = jnp.where(row > col, 1.0, 0.0).astype(BF16)
        for hh in range(2):
            lanes = slice(hh * HD, (hh + 1) * HD)

            def qloop(qi, carry):
                q0 = pl.multiple_of(qi * bq, bq)
                q = q_ref[pl.ds(q0, bq), lanes]
                nkb = (qi + 1) * (bq // SB_BK)

                def kloop(i, c):
                    acc, cr = c
                    k0 = pl.multiple_of((nkb - 1 - i) * SB_BK, SB_BK)
                    k = k_ref[pl.ds(k0, SB_BK), lanes]
                    v = v_ref[pl.ds(k0, SB_BK), lanes]
                    _, mask, _, lb, lom = _sb_tile(q, k, q0, k0, bq)
                    cs = _dot_hilo(lom, upper) + cr
                    a = jnp.where(mask, jnp.exp(lb + cs), 0.0)
                    acc = acc + _dot(a.astype(BF16), v)
                    cr = cr + jnp.sum(lom, axis=1, keepdims=True)
                    return acc, cr

                acc, cr = lax.fori_loop(0, nkb, kloop,
                                        (jnp.zeros((bq, HD), F32), jnp.zeros((bq, 1), F32)))
                o_ref[pl.ds(q0, bq), lanes] = acc.astype(BF16)
                t_ref[pl.ds(q0, bq), lanes] = jnp.broadcast_to(cr, (bq, HD))
                return carry

            lax.fori_loop(0, nq, qloop, 0)

    def spec(c0):
        return pl.BlockSpec((None, seq, 128), lambda b, p: (b, 0, c0 + p))

    return pl.pallas_call(
        body, name="sb_fwd", grid=(nb, BW // 128),
        in_specs=[spec(8), spec(12), spec(16)],
        out_specs=[spec(0), spec(0)],
        out_shape=[jax.ShapeDtypeStruct((nb, seq, BW), BF16), jax.ShapeDtypeStruct((nb, seq, BW), F32)],
        compiler_params=_cp(("parallel", "parallel")),
    )(proj3, proj3, proj3)


def _sb_bwd(proj3, do3, tot3):
    nb, seq, _ = proj3.shape
    bq = min(SB_BQ, seq)
    nq = seq // bq

    def body(q_ref, k_ref, v_ref, do_ref, t_ref, dq_ref, dk_ref, dv_ref, dk_acc, dv_acc):
        row = lax.broadcasted_iota(jnp.int32, (SB_BK, SB_BK), 0)
        col = lax.broadcasted_iota(jnp.int32, (SB_BK, SB_BK), 1)
        upper = jnp.where(row > col, 1.0, 0.0).astype(BF16)
        lower = jnp.where(row < col, 1.0, 0.0).astype(BF16)
        dk_acc[...] = jnp.zeros_like(dk_acc)
        dv_acc[...] = jnp.zeros_like(dv_acc)
        for hh in range(2):
            lanes = slice(hh * HD, (hh + 1) * HD)

            def qloop(qi, carry):
                q0 = pl.multiple_of(qi * bq, bq)
                q = q_ref[pl.ds(q0, bq), lanes]
                do = do_ref[pl.ds(q0, bq), lanes]
                tot = t_ref[pl.ds(q0, bq), hh * HD:hh * HD + 1]
                nkb = (qi + 1) * (bq // SB_BK)

                def kloop(kb, c):
                    dq, cpre, gpre = c
                    k0 = pl.multiple_of(kb * SB_BK, SB_BK)
                    k = k_ref[pl.ds(k0, SB_BK), lanes]
                    v = v_ref[pl.ds(k0, SB_BK), lanes]
                    z, mask, e, lb, lom = _sb_tile(q, k, q0, k0, bq)
                    rsum = jnp.sum(lom, axis=1, keepdims=True)
                    cs = _dot_hilo(lom, upper) + (tot - cpre - rsum)
                    a = jnp.where(mask, jnp.exp(lb + cs), 0.0)
                    gl = _dot(do, v, NT) * a
                    pre = _dot_hilo(gl, lower) + gpre
                    inv = 1.0 / (1.0 + e)
                    pos = z >= 0.0
                    beta = jnp.where(pos, 1.0, e) * inv
                    omb = jnp.where(pos, e, 1.0) * inv
                    dz = (jnp.where(mask, gl * omb - beta * pre, 0.0) * SB_SCALE).astype(BF16)
                    dq = dq + _dot(dz, k)
                    dk_acc[hh, pl.ds(k0, SB_BK), :] += _dot(dz, q, TN)
                    dv_acc[hh, pl.ds(k0, SB_BK), :] += _dot(a.astype(BF16), do, TN)
                    return dq, cpre + rsum, gpre + jnp.sum(gl, axis=1, keepdims=True)

                dq, _, _ = lax.fori_loop(
                    0, nkb, kloop,
                    (jnp.zeros((bq, HD), F32), jnp.zeros((bq, 1), F32), jnp.zeros((bq, 1), F32)))
                dq_ref[pl.ds(q0, bq), lanes] = dq.astype(BF16)
                return carry

            lax.fori_loop(0, nq, qloop, 0)
        for hh in range(2):
            lanes = slice(hh * HD, (hh + 1) * HD)
            dk_ref[:, lanes] = dk_acc[hh].astype(BF16)
            dv_ref[:, lanes] = dv_acc[hh].astype(BF16)

    def spec(c0):
        return pl.BlockSpec((None, seq, 128), lambda b, p: (b, 0, c0 + p))

    return pl.pallas_call(
        body, name="sb_bwd", grid=(nb, BW // 128),
        in_specs=[spec(8), spec(12), spec(16), spec(0), spec(0)],
        out_specs=[spec(0), spec(0), spec(0)],
        out_shape=[jax.ShapeDtypeStruct((nb, seq, BW), BF16)] * 3,
        scratch_shapes=[pltpu.VMEM((2, seq, HD), F32), pltpu.VMEM((2, seq, HD), F32)],
        compiler_params=_cp(("parallel", "parallel")),
    )(proj3, proj3, proj3, do3, tot3)


def _merge_fwd(brs, wb, proj):
    t = proj.shape[0]
    tm = _row_tile(t, 512)
    tn = 512
    nj = D // tn

    def body(b0, b1, b2, wb_ref, l0, l1, l2, m_ref, y0, y1, y2):
        acc = None
        for br, n, lg, y_ref in ((b0, 0, l0, y0), (b1, 1, l1, y1), (b2, 2, l2, y2)):
            y = _dot(br[...], wb_ref[n])
            y_ref[...] = y.astype(BF16)
            term = jax.nn.sigmoid(lg[...].astype(F32)) * y
            acc = term if acc is None else acc + term
        m_ref[...] = acc.astype(BF16)

    def lspec(n):
        return pl.BlockSpec((tm, tn), lambda i, j: (i, (3 * D + n * D) // tn + j))

    tile = pl.BlockSpec((tm, tn), lambda i, j: (i, j))
    bspec = pl.BlockSpec((tm, BW), lambda i, j: (i, 0))
    return pl.pallas_call(
        body, name="merge_fwd", grid=(t // tm, nj),
        in_specs=[bspec, bspec, bspec, pl.BlockSpec((NB, BW, tn), lambda i, j: (0, 0, j)),
                  lspec(0), lspec(1), lspec(2)],
        out_specs=[tile] * 4,
        out_shape=[jax.ShapeDtypeStruct((t, D), BF16)] * 4,
        compiler_params=_cp(("parallel", "parallel")),
    )(brs[0], brs[1], brs[2], wb, proj, proj, proj)


def _merge_bwd(dm, ys, proj):
    t = proj.shape[0]
    tm = _row_tile(t, 512)
    tn = 512

    def body(dm_ref, y0, y1, y2, l0, l1, l2, dl0, dl1, dl2, dy0, dy1, dy2):
        dmv = dm_ref[...].astype(F32)
        for y_ref, lg, dl_ref, dy_ref in ((y0, l0, dl0, dy0), (y1, l1, dl1, dy1), (y2, l2, dl2, dy2)):
            g = jax.nn.sigmoid(lg[...].astype(F32))
            dl_ref[...] = (dmv * y_ref[...].astype(F32) * g * (1.0 - g)).astype(BF16)
            dy_ref[...] = (dmv * g).astype(BF16)

    def lspec(n):
        return pl.BlockSpec((tm, tn), lambda i, j: (i, (3 * D + n * D) // tn + j))

    tile = pl.BlockSpec((tm, tn), lambda i, j: (i, j))
    return pl.pallas_call(
        body, name="merge_bwd", grid=(t // tm, D // tn),
        in_specs=[tile] * 4 + [lspec(0), lspec(1), lspec(2)],
        out_specs=[tile] * 6,
        out_shape=[jax.ShapeDtypeStruct((t, D), BF16)] * 6,
        compiler_params=_cp(("parallel", "parallel")),
    )(dm, ys[0], ys[1], ys[2], proj, proj, proj)


def _swiglu_fwd(f):
    t = f.shape[0]
    tm = _row_tile(t, 512)
    tn = 1024

    def body(g_ref, u_ref, a_ref):
        g = g_ref[...].astype(F32)
        a_ref[...] = (g * jax.nn.sigmoid(g) * u_ref[...].astype(F32)).astype(BF16)

    return pl.pallas_call(
        body, name="swiglu_fwd", grid=(t // tm, FFP // tn),
        in_specs=[pl.BlockSpec((tm, tn), lambda i, j: (i, j)),
                  pl.BlockSpec((tm, tn), lambda i, j: (i, FFP // tn + j))],
        out_specs=pl.BlockSpec((tm, tn), lambda i, j: (i, j)),
        out_shape=jax.ShapeDtypeStruct((t, FFP), BF16),
        compiler_params=_cp(("parallel", "parallel")),
    )(f, f)


def _swiglu_bwd(f, da):
    t = f.shape[0]
    tm = _row_tile(t, 512)
    tn = 1024

    def body(g_ref, u_ref, da_ref, dg_ref, du_ref):
        g = g_ref[...].astype(F32)
        u = u_ref[...].astype(F32)
        d = da_ref[...].astype(F32)
        s = jax.nn.sigmoid(g)
        dg_ref[...] = (d * u * (s * (1.0 + g * (1.0 - s)))).astype(BF16)
        du_ref[...] = (d * (g * s)).astype(BF16)

    return pl.pallas_call(
        body, name="swiglu_bwd", grid=(t // tm, FFP // tn),
        in_specs=[pl.BlockSpec((tm, tn), lambda i, j: (i, j)),
                  pl.BlockSpec((tm, tn), lambda i, j: (i, FFP // tn + j)),
                  pl.BlockSpec((tm, tn), lambda i, j: (i, j))],
        out_specs=[pl.BlockSpec((tm, tn), lambda i, j: (i, j)),
                   pl.BlockSpec((tm, tn), lambda i, j: (i, j))],
        out_shape=[jax.ShapeDtypeStruct((t, FFP), BF16)] * 2,
        compiler_params=_cp(("parallel", "parallel")),
    )(f, f, da)


def _adamw_reduce(name, parts, w, m, v):
    shape = w.shape
    cols = shape[-1]
    rows = int(math.prod(shape[:-1])) if len(shape) > 1 else 1
    npart = parts.shape[0]
    tr = rows if rows <= 512 else 512
    assert rows % tr == 0
    c1 = 1.0 - ADAM_B1 ** ADAM_STEP
    c2 = 1.0 - ADAM_B2 ** ADAM_STEP

    def body(p_ref, w_ref, m_ref, v_ref, g_ref, d_ref, mo_ref, vo_ref):
        g = p_ref[0].astype(F32)
        for p in range(1, npart):
            g = g + p_ref[p].astype(F32)
        mn = ADAM_B1 * m_ref[...] + (1.0 - ADAM_B1) * g
        vn = ADAM_B2 * v_ref[...] + (1.0 - ADAM_B2) * (g * g)
        m_hat = mn / c1
        v_hat = vn / c2
        g_ref[...] = g
        d_ref[...] = -ADAM_LR * (m_hat / (jnp.sqrt(v_hat) + ADAM_EPS) + ADAM_WD * w_ref[...])
        mo_ref[...] = mn
        vo_ref[...] = vn

    tile = pl.BlockSpec((tr, cols), lambda i: (i, 0))
    sds = jax.ShapeDtypeStruct((rows, cols), F32)
    outs = pl.pallas_call(
        body, name=name, grid=(rows // tr,),
        in_specs=[pl.BlockSpec((npart, tr, cols), lambda i: (0, i, 0)), tile, tile, tile],
        out_specs=[tile] * 4, out_shape=[sds] * 4,
        compiler_params=_cp(("parallel",)),
    )(parts.reshape(npart, rows, cols), w.reshape(rows, cols), m.reshape(rows, cols), v.reshape(rows, cols))
    return tuple(o.reshape(shape) for o in outs)


def _pad_ffn_in(w):
    lead = w.shape[:-1]
    w = w.reshape(lead + (2, FF_HALF))
    w = jnp.pad(w, [(0, 0)] * len(lead) + [(0, 0), (0, FF_HALF_PAD - FF_HALF)])
    return w.reshape(lead + (FF_IN_PAD,))


def _unpad_ffn_in(w):
    lead = w.shape[:-1]
    return w.reshape(lead + (2, FF_HALF_PAD))[..., :FF_HALF].reshape(lead + (FF_IN_SHARD,))


def kernel(x, c, rms_g1, rms_g2, w_ada, b_ada, w_in, gm_ln_g, gm_ln_b, gm_w_spatial, gm_b_spatial, pool_w, pool_scale, w_branch, w_out, w_ffn_in, w_ffn_out, final_g, loss_target, m_rms_g1, m_rms_g2, m_w_ada, m_b_ada, m_w_in, m_gm_ln_g, m_gm_ln_b, m_gm_w_spatial, m_gm_b_spatial, m_pool_w, m_pool_scale, m_w_branch, m_w_out, m_w_ffn_in, m_w_ffn_out, m_final_g, v_rms_g1, v_rms_g2, v_w_ada, v_b_ada, v_w_in, v_gm_ln_g, v_gm_ln_b, v_gm_w_spatial, v_gm_b_spatial, v_pool_w, v_pool_scale, v_w_branch, v_w_out, v_w_ffn_in, v_w_ffn_out, v_final_g):
    nb, seq, _ = x.shape
    nl = w_in.shape[0]
    t = nb * seq
    ntot = NDEV * nb
    me = _my_index()
    assert x.shape[2] == D and w_in.shape[1:] == (D, 768) and w_ffn_in.shape[1:] == (D, FF_IN_SHARD)
    assert seq % CH == 0

    w_ffn_in_p = _pad_ffn_in(w_ffn_in).astype(BF16)
    w_ffn_out_p = jnp.pad(w_ffn_out, ((0, 0), (0, FF_HALF_PAD - FF_HALF), (0, 0))).astype(BF16)
    w_in_b = w_in.astype(BF16)
    w_branch_b = w_branch.astype(BF16)
    w_out_b = w_out.astype(BF16)
    gathered = []
    for l in range(nl):
        g_in, g_br, g_out, g_fi, g_fo = _exchange(
            [w_in_b[l], w_branch_b[l], w_out_b[l], w_ffn_in_p[l], w_ffn_out_p[l]], "gather_weights", False)
        gathered.append(dict(
            w_in=g_in,
            w_branch=jnp.transpose(g_br, (1, 2, 0, 3)).reshape(NB, BW, D),
            w_out=g_out.reshape(D, D),
            w_ffn_in=g_fi,
            w_ffn_out=g_fo.reshape(FFP, D)))

    (c_all,) = _exchange([c], "gather_c", False)
    c_all = c_all.reshape(ntot, D)
    b_blk = lax.dynamic_slice_in_dim(b_ada, me * 768, 768, axis=1).reshape(nl, 1, 768)
    mod_blk = _ada_fwd(c_all, w_ada, b_blk)
    (mod_all,) = _exchange([mod_blk], "gather_mod", False)
    mod_all = jnp.transpose(mod_all, (1, 2, 0, 3)).reshape(nl, ntot, NMOD * D)
    mod = lax.dynamic_slice_in_dim(mod_all, me * nb, nb, axis=1).reshape(nl, nb, NMOD, 1, D)

    saved = []
    xc = x
    for l in range(nl):
        gw = gathered[l]
        sh1, sc1, gt1, sh2, sc2, gt2 = [mod[l, :, i] for i in range(NMOD)]
        h = _norm_mod_fwd(xc, rms_g1[l].reshape(1, D), sc1, sh1).reshape(t, D)
        proj = _mm_colblocked("proj_fwd", h, gw["w_in"], BF16)
        proj3 = proj.reshape(nb, seq, IN_COLS)
        br_gm = _gmlp_fwd(proj, gm_ln_g[l].reshape(1, BW), gm_ln_b[l].reshape(1, BW),
                          gm_w_spatial[l], gm_b_spatial[l].T)
        br_sb, sb_tot = _sb_fwd(proj3)
        br_pool = _pool_fwd(proj3, pool_w[l], pool_scale[l].reshape(1, BW))
        brs = [br_gm, br_sb.reshape(t, BW), br_pool.reshape(t, BW)]
        merged, y0, y1, y2 = _merge_fwd(brs, gw["w_branch"], proj)
        x_mid, mo = _mm_residual("out_fwd", merged, gw["w_out"], xc.reshape(t, D), gt1, seq)
        x_mid = x_mid.reshape(nb, seq, D)
        h2 = _norm_mod_fwd(x_mid, rms_g2[l].reshape(1, D), sc2, sh2).reshape(t, D)
        f = _mm_colblocked("ffn_in_fwd", h2, gw["w_ffn_in"], BF16)
        act = _swiglu_fwd(f)
        x_out, fo = _mm_residual("ffn_out_fwd", act, gw["w_ffn_out"], x_mid.reshape(t, D), gt2, seq)
        saved.append(dict(x_in=xc, h=h, proj=proj, brs=brs, sb_tot=sb_tot, ys=(y0, y1, y2), merged=merged,
                          mo=mo, x_mid=x_mid, h2=h2, f=f, act=act, fo=fo))
        xc = x_out.reshape(nb, seq, D)

    dx, loss_part, dfinal_part = _loss_head(xc, loss_target, final_g.reshape(1, D))
    loss = lax.psum(jnp.sum(loss_part[:, 0, 0]), ("x", "y", "c"))

    big = {}
    small_parts = {k: [None] * nl for k in ("rms_g1", "rms_g2", "gm_ln_g", "gm_ln_b", "gm_w_spatial",
                                            "gm_b_spatial", "pool_w", "pool_scale")}
    dmod = [None] * nl
    big_names = ("w_in", "w_branch", "w_out", "w_ffn_in", "w_ffn_out")
    for name in big_names:
        big[name] = [None] * nl
    for l in reversed(range(nl)):
        gw = gathered[l]
        sv = saved[l]
        sh1, sc1, gt1, sh2, sc2, gt2 = [mod[l, :, i] for i in range(NMOD)]
        dfo, dgt2 = _gate_bwd(dx, sv["fo"].reshape(nb, seq, D), gt2)
        dfo = dfo.reshape(t, D)
        dact = _mm_nt("ffn_out_dgrad", dfo, gw["w_ffn_out"], BF16)
        g_ffn_out = _mm_tn("ffn_out_wgrad", sv["act"], dfo)
        dfg, dfu = _swiglu_bwd(sv["f"], dact)
        df = jnp.concatenate([dfg, dfu], axis=1)
        dh2 = _mm_colblocked_nt("ffn_in_dgrad", df, gw["w_ffn_in"], F32)
        g_ffn_in = _mm_colblocked_tn("ffn_in_wgrad", sv["h2"], df)
        dx_mid, dsh2, dsc2, dg2 = _norm_mod_bwd(sv["x_mid"], dh2.reshape(nb, seq, D), dx,
                                                rms_g2[l].reshape(1, D), sc2)
        dmo, dgt1 = _gate_bwd(dx_mid, sv["mo"].reshape(nb, seq, D), gt1)
        dmo = dmo.reshape(t, D)
        dmerged = _mm_nt("out_dgrad", dmo, gw["w_out"], BF16)
        g_out = _mm_tn("out_wgrad", sv["merged"], dmo)
        dls_dys = _merge_bwd(dmerged, sv["ys"], sv["proj"])
        dls, dys = dls_dys[:3], dls_dys[3:]
        dbrs, g_br = [], []
        for n in range(NB):
            dbrs.append(_mm_nt("branch_dgrad", dys[n], gw["w_branch"], BF16, w_lead=n))
            g_br.append(_mm_tn("branch_wgrad", sv["brs"][n], dys[n]))
        proj3 = sv["proj"].reshape(nb, seq, IN_COLS)
        d_gm, g_ws, g_bs, g_lg, g_lb = _gmlp_bwd(sv["proj"], dbrs[0], gm_ln_g[l].reshape(1, BW),
                                                 gm_ln_b[l].reshape(1, BW), gm_w_spatial[l], gm_b_spatial[l].T)
        d_sb = _sb_bwd(proj3, dbrs[1].reshape(nb, seq, BW), sv["sb_tot"])
        d_pool, g_pw, g_ps = _pool_bwd(proj3, dbrs[2].reshape(nb, seq, BW), pool_w[l], pool_scale[l].reshape(1, BW))
        dproj = jnp.concatenate([d_gm] + [a.reshape(t, BW) for a in d_sb] + [d_pool.reshape(t, BW)] + list(dls),
                                axis=1)
        dh = _mm_colblocked_nt("proj_dgrad", dproj, gw["w_in"], F32)
        g_in = _mm_colblocked_tn("proj_wgrad", sv["h"], dproj)
        dx, dsh1, dsc1, dg1 = _norm_mod_bwd(sv["x_in"], dh.reshape(nb, seq, D), dx_mid,
                                            rms_g1[l].reshape(1, D), sc1)

        dmod[l] = jnp.concatenate([dsh1, dsc1, dgt1, dsh2, dsc2, dgt2], axis=-1)
        small_parts["rms_g1"][l] = jnp.sum(dg1, axis=0)
        small_parts["rms_g2"][l] = jnp.sum(dg2, axis=0)
        small_parts["gm_ln_g"][l] = g_lg
        small_parts["gm_ln_b"][l] = g_lb
        small_parts["gm_w_spatial"][l] = g_ws
        small_parts["gm_b_spatial"][l] = g_bs[:, :, 0]
        small_parts["pool_w"][l] = g_pw
        small_parts["pool_scale"][l] = g_ps

        g_br_dev = jnp.transpose(jnp.stack(g_br).reshape(NB, BW, NDEV, D // NDEV), (2, 0, 1, 3))
        r_in, r_br, r_out, r_fi, r_fo = _exchange(
            [g_in, g_br_dev, g_out.reshape(NDEV, D // NDEV, D), g_ffn_in,
             g_ffn_out.reshape(NDEV, FF_HALF_PAD, D)], "scatter_grads", True)
        big["w_in"][l] = _adamw_reduce("adamw_w_in", r_in, w_in[l], m_w_in[l], v_w_in[l])
        big["w_branch"][l] = _adamw_reduce("adamw_w_branch", r_br, w_branch[l], m_w_branch[l], v_w_branch[l])
        big["w_out"][l] = _adamw_reduce("adamw_w_out", r_out, w_out[l], m_w_out[l], v_w_out[l])
        big["w_ffn_in"][l] = _adamw_reduce("adamw_w_ffn_in", _unpad_ffn_in(r_fi), w_ffn_in[l],
                                           m_w_ffn_in[l], v_w_ffn_in[l])
        big["w_ffn_out"][l] = _adamw_reduce("adamw_w_ffn_out", r_fo[:, :FF_HALF], w_ffn_out[l],
                                            m_w_ffn_out[l], v_w_ffn_out[l])

    dmod_mine = jnp.stack(dmod).reshape(nl, nb, NMOD * D)
    names = list(small_parts)
    stacked = [jnp.stack(small_parts[k]) for k in names]
    gathered_small = _exchange(stacked + [dfinal_part, dmod_mine], "gather_small", False)
    dmod_all = jnp.transpose(gathered_small[-1], (1, 0, 2, 3)).reshape(nl, ntot, NMOD * D)
    dfinal_all = gathered_small[-2].reshape(ntot, D)

    results = {}
    weights = dict(rms_g1=(rms_g1, m_rms_g1, v_rms_g1), rms_g2=(rms_g2, m_rms_g2, v_rms_g2),
                   gm_ln_g=(gm_ln_g, m_gm_ln_g, v_gm_ln_g), gm_ln_b=(gm_ln_b, m_gm_ln_b, v_gm_ln_b),
                   gm_w_spatial=(gm_w_spatial, m_gm_w_spatial, v_gm_w_spatial),
                   gm_b_spatial=(gm_b_spatial, m_gm_b_spatial, v_gm_b_spatial),
                   pool_w=(pool_w, m_pool_w, v_pool_w), pool_scale=(pool_scale, m_pool_scale, v_pool_scale))
    for k, parts in zip(names, gathered_small[:len(names)]):
        w, m, v = weights[k]
        results[k] = _adamw_reduce("adamw_" + k, parts.reshape((NDEV,) + w.shape), w, m, v)
    results["final_g"] = _adamw_reduce("adamw_final_g", dfinal_all, final_g, m_final_g, v_final_g)
    results["b_ada"] = _adamw_reduce("adamw_b_ada", jnp.transpose(dmod_all, (1, 0, 2)), b_ada, m_b_ada, v_b_ada)
    dmod_blk = lax.dynamic_slice_in_dim(dmod_all, me * 768, 768, axis=2)
    g_w_ada = _ada_bwd(c_all, dmod_blk)
    results["w_ada"] = _adamw_reduce("adamw_w_ada", g_w_ada[None], w_ada, m_w_ada, v_w_ada)
    for name in big_names:
        results[name] = tuple(jnp.stack([big[name][l][i] for l in range(nl)]) for i in range(4))

    order = ["rms_g1", "rms_g2", "w_ada", "b_ada", "w_in", "gm_ln_g", "gm_ln_b", "gm_w_spatial", "gm_b_spatial",
             "pool_w", "pool_scale", "w_branch", "w_out", "w_ffn_in", "w_ffn_out", "final_g"]
    out = [loss, dx]
    for i in range(4):
        out.extend(results[k][i] for k in order)
    return tuple(out)
```

```python
import functools
import math

import jax
import jax.numpy as jnp
from jax import lax
from jax.experimental import pallas as pl
from jax.experimental.pallas import tpu as pltpu

F32 = jnp.float32
BF16 = jnp.bfloat16
MESH = pl.DeviceIdType.MESH

D = 1024
BW = 512
NB = 3
CH = 128
NG = 4
HD = 64
POOL_WINDOWS = (2, 4, 8, 16)
DFF = 2816
NMOD = 6
EPS = 1e-6
IN_COLS = 6 * D
NDEV = 8
FF_IN_SHARD = 2 * DFF // NDEV
FF_HALF = FF_IN_SHARD // 2
FF_HALF_PAD = 384
FF_IN_PAD = 2 * FF_HALF_PAD
FFP = NDEV // 2 * FF_IN_PAD

ADAM_LR = 0.001
ADAM_B1 = 0.9
ADAM_B2 = 0.999
ADAM_EPS = 1e-08
ADAM_WD = 0.01
ADAM_STEP = 10

VMEM_LIMIT = 48 * 1024 * 1024

NN = (((1,), (0,)), ((), ()))
NT = (((1,), (1,)), ((), ()))
TN = (((0,), (0,)), ((), ()))


def _cp(sem=None):
    return pltpu.CompilerParams(dimension_semantics=sem, vmem_limit_bytes=VMEM_LIMIT)


def _dot(a, b, dims=NN):
    return lax.dot_general(a, b, dims, preferred_element_type=F32)


def _my_index():
    return 4 * lax.axis_index("x") + 2 * lax.axis_index("y") + lax.axis_index("c")


def _peer(k):
    x, y, c = lax.axis_index("x"), lax.axis_index("y"), lax.axis_index("c")
    px = 1 - x if k & 4 else x
    py = 1 - y if k & 2 else y
    pc = 1 - c if k & 1 else c
    return (px, py, pc), 4 * px + 2 * py + pc


def _exchange(xs, name, all_to_all):
    n = len(xs)
    hbm = pl.BlockSpec(memory_space=pl.ANY)

    def body(*refs):
        ins, outs = refs[:n], refs[n:2 * n]
        send_sems, recv_sems, local_sems = refs[2 * n:]
        me = _my_index()
        local = []
        for a in range(n):
            src = ins[a].at[me] if all_to_all else ins[a]
            cp = pltpu.make_async_copy(src, outs[a].at[me], local_sems.at[a])
            cp.start()
            local.append(cp)
        sends = []
        for k in range(1, NDEV):
            dev, idx = _peer(k)
            for a in range(n):
                src = ins[a].at[idx] if all_to_all else ins[a]
                cp = pltpu.make_async_remote_copy(
                    src_ref=src, dst_ref=outs[a].at[me],
                    send_sem=send_sems.at[a * 7 + k - 1], recv_sem=recv_sems.at[a * 7 + k - 1],
                    device_id=dev, device_id_type=MESH)
                cp.start()
                sends.append(cp)
        for cp in sends:
            cp.wait_send()
        for k in range(1, NDEV):
            dev, idx = _peer(k)
            for a in range(n):
                src = ins[a].at[idx] if all_to_all else ins[a]
                pltpu.make_async_remote_copy(
                    src_ref=src, dst_ref=outs[a].at[idx],
                    send_sem=send_sems.at[a * 7 + k - 1], recv_sem=recv_sems.at[a * 7 + k - 1],
                    device_id=dev, device_id_type=MESH).wait_recv()
        for cp in local:
            cp.wait()

    if all_to_all:
        out_shape = [jax.ShapeDtypeStruct(x.shape, x.dtype) for x in xs]
    else:
        out_shape = [jax.ShapeDtypeStruct((NDEV,) + x.shape, x.dtype) for x in xs]
    return pl.pallas_call(
        body, name=name, out_shape=out_shape,
        in_specs=[hbm] * n, out_specs=[hbm] * n,
        scratch_shapes=[pltpu.SemaphoreType.DMA((n * 7,)), pltpu.SemaphoreType.DMA((n * 7,)),
                        pltpu.SemaphoreType.DMA((n,))],
    )(*xs)


def _mm(name, a, b, grid, a_spec, b_spec, o_spec, out_sds, dims, acc_shape):
    nk = grid[2]

    def body(a_ref, b_ref, o_ref, acc_ref):
        k = pl.program_id(2)

        @pl.when(k == 0)
        def _():
            acc_ref[...] = jnp.zeros_like(acc_ref)

        acc_ref[...] += _dot(a_ref[...].astype(BF16), b_ref[...].astype(BF16), dims)

        @pl.when(k == nk - 1)
        def _():
            o_ref[...] = acc_ref[...].astype(o_ref.dtype)

    return pl.pallas_call(
        body, name=name, grid=grid, in_specs=[a_spec, b_spec], out_specs=o_spec, out_shape=out_sds,
        scratch_shapes=[pltpu.VMEM(acc_shape, F32)],
        compiler_params=_cp(("parallel", "parallel", "arbitrary")),
    )(a, b)


def _row_tile(t, want):
    tm = min(t, want)
    assert t % tm == 0
    return tm


def _mm_colblocked(name, a, wg, out_dtype):
    t = a.shape[0]
    tm = _row_tile(t, 1024)
    return _mm(name, a, wg, (t // tm, NDEV, 1),
               pl.BlockSpec((tm, D), lambda i, j, k: (i, 0)),
               pl.BlockSpec((None, D, 768), lambda i, j, k: (j, 0, 0)),
               pl.BlockSpec((tm, 768), lambda i, j, k: (i, j)),
               jax.ShapeDtypeStruct((t, NDEV * 768), out_dtype), NN, (tm, 768))


def _mm_colblocked_nt(name, g, wg, out_dtype):
    t = g.shape[0]
    tm = _row_tile(t, 1024)
    return _mm(name, g, wg, (t // tm, 1, NDEV),
               pl.BlockSpec((tm, 768), lambda i, j, k: (i, k)),
               pl.BlockSpec((None, D, 768), lambda i, j, k: (k, 0, 0)),
               pl.BlockSpec((tm, D), lambda i, j, k: (i, 0)),
               jax.ShapeDtypeStruct((t, D), out_dtype), NT, (tm, D))


def _mm_colblocked_tn(name, a, g):
    t = a.shape[0]
    tk = _row_tile(t, 1024)
    return _mm(name, a, g, (1, NDEV, t // tk),
               pl.BlockSpec((tk, D), lambda i, j, k: (k, 0)),
               pl.BlockSpec((tk, 768), lambda i, j, k: (k, j)),
               pl.BlockSpec((None, D, 768), lambda i, j, k: (j, 0, 0)),
               jax.ShapeDtypeStruct((NDEV, D, 768), BF16), TN, (D, 768))


def _mm_nt(name, a, w, out_dtype, a_col=0, w_lead=None):
    t = a.shape[0]
    if w_lead is None:
        kdim, n = w.shape
        b_spec = pl.BlockSpec((min(kdim, 1024), n), lambda i, j, k: (j, 0))
    else:
        _, kdim, n = w.shape
        b_spec = pl.BlockSpec((None, min(kdim, 1024), n), lambda i, j, k: (w_lead, j, 0))
    tn = min(kdim, 1024)
    tm = _row_tile(t, 1024)
    return _mm(name, a, w, (t // tm, kdim // tn, 1),
               pl.BlockSpec((tm, n), lambda i, j, k: (i, a_col)),
               b_spec,
               pl.BlockSpec((tm, tn), lambda i, j, k: (i, j)),
               jax.ShapeDtypeStruct((t, kdim), out_dtype), NT, (tm, tn))


def _mm_tn(name, a, g, out_dtype=BF16):
    t, kdim = a.shape
    n = g.shape[1]
    tk = _row_tile(t, 1024)
    tm = min(kdim, 1024)
    tn = min(n, 1024)
    return _mm(name, a, g, (kdim // tm, n // tn, t // tk),
               pl.BlockSpec((tk, tm), lambda i, j, k: (k, i)),
               pl.BlockSpec((tk, tn), lambda i, j, k: (k, j)),
               pl.BlockSpec((tm, tn), lambda i, j, k: (i, j)),
               jax.ShapeDtypeStruct((kdim, n), out_dtype), TN, (tm, tn))


def _mm_residual(name, a, w, x, gt, seq):
    t, kdim = a.shape
    tm = _row_tile(seq, 512)
    tn = 512
    tk = min(kdim, 1024)
    nk = kdim // tk
    per = seq // tm

    def body(a_ref, w_ref, x_ref, gt_ref, xo_ref, y_ref, acc_ref):
        k = pl.program_id(2)

        @pl.when(k == 0)
        def _():
            acc_ref[...] = jnp.zeros_like(acc_ref)

        acc_ref[...] += _dot(a_ref[...], w_ref[...])

        @pl.when(k == nk - 1)
        def _():
            y = acc_ref[...]
            xo_ref[...] = x_ref[...] + gt_ref[0] * y
            y_ref[...] = y.astype(BF16)

    return pl.pallas_call(
        body, name=name, grid=(t // tm, D // tn, nk),
        in_specs=[pl.BlockSpec((tm, tk), lambda i, j, k: (i, k)),
                  pl.BlockSpec((tk, tn), lambda i, j, k: (k, j)),
                  pl.BlockSpec((tm, tn), lambda i, j, k: (i, j)),
                  pl.BlockSpec((1, 1, tn), lambda i, j, k: (i // per, 0, j))],
        out_specs=[pl.BlockSpec((tm, tn), lambda i, j, k: (i, j)),
                   pl.BlockSpec((tm, tn), lambda i, j, k: (i, j))],
        out_shape=[jax.ShapeDtypeStruct((t, D), F32), jax.ShapeDtypeStruct((t, D), BF16)],
        scratch_shapes=[pltpu.VMEM((tm, tn), F32)],
        compiler_params=_cp(("parallel", "parallel", "arbitrary")),
    )(a, w, x, gt)


def _ada_fwd(c_all, w_ada, b_blk):
    nl = w_ada.shape[0]
    nb = c_all.shape[0]

    def body(c_ref, w_ref, b_ref, o_ref):
        c = c_ref[...]
        ca = (c * jax.nn.sigmoid(c)).astype(BF16)
        o_ref[...] = _dot(ca, w_ref[...].astype(BF16)) + b_ref[...]

    return pl.pallas_call(
        body, name="ada_fwd", grid=(nl,),
        in_specs=[pl.BlockSpec((nb, D), lambda l: (0, 0)),
                  pl.BlockSpec((None, D, 768), lambda l: (l, 0, 0)),
                  pl.BlockSpec((None, 1, 768), lambda l: (l, 0, 0))],
        out_specs=pl.BlockSpec((None, nb, 768), lambda l: (l, 0, 0)),
        out_shape=jax.ShapeDtypeStruct((nl, nb, 768), F32),
        compiler_params=_cp(("parallel",)),
    )(c_all, w_ada, b_blk)


def _ada_bwd(c_all, dmod_blk):
    nl = dmod_blk.shape[0]
    nb = c_all.shape[0]

    def body(c_ref, d_ref, o_ref):
        c = c_ref[...]
        ca = (c * jax.nn.sigmoid(c)).astype(BF16)
        o_ref[...] = _dot(ca, d_ref[...].astype(BF16), TN)

    return pl.pallas_call(
        body, name="ada_bwd", grid=(nl,),
        in_specs=[pl.BlockSpec((nb, D), lambda l: (0, 0)),
                  pl.BlockSpec((None, nb, 768), lambda l: (l, 0, 0))],
        out_specs=pl.BlockSpec((None, D, 768), lambda l: (l, 0, 0)),
        out_shape=jax.ShapeDtypeStruct((nl, D, 768), F32),
        compiler_params=_cp(("parallel",)),
    )(c_all, dmod_blk)


def _seq_tile(seq):
    return _row_tile(seq, 512)


def _norm_mod_fwd(x, g, sc, sh):
    nb, seq, _ = x.shape
    ts = _seq_tile(seq)

    def body(x_ref, g_ref, sc_ref, sh_ref, h_ref):
        xv = x_ref[0]
        r = lax.rsqrt(jnp.mean(xv * xv, axis=-1, keepdims=True) + EPS)
        h_ref[0] = ((xv * r) * g_ref[...] * (1.0 + sc_ref[0]) + sh_ref[0]).astype(BF16)

    return pl.pallas_call(
        body, name="norm_mod_fwd", grid=(nb, seq // ts),
        in_specs=[pl.BlockSpec((1, ts, D), lambda b, s: (b, s, 0)),
                  pl.BlockSpec((1, D), lambda b, s: (0, 0)),
                  pl.BlockSpec((1, 1, D), lambda b, s: (b, 0, 0)),
                  pl.BlockSpec((1, 1, D), lambda b, s: (b, 0, 0))],
        out_specs=pl.BlockSpec((1, ts, D), lambda b, s: (b, s, 0)),
        out_shape=jax.ShapeDtypeStruct((nb, seq, D), BF16),
        compiler_params=_cp(("parallel", "parallel")),
    )(x, g, sc, sh)


def _norm_mod_bwd(x, dh, dres, g, sc):
    nb, seq, _ = x.shape
    ts = _seq_tile(seq)

    def body(x_ref, dh_ref, dres_ref, g_ref, sc_ref, dx_ref, dsh_ref, dsc_ref, dg_ref):
        @pl.when(pl.program_id(1) == 0)
        def _():
            dsh_ref[...] = jnp.zeros_like(dsh_ref)
            dsc_ref[...] = jnp.zeros_like(dsc_ref)
            dg_ref[...] = jnp.zeros_like(dg_ref)

        xv = x_ref[0]
        dh = dh_ref[0]
        gv = g_ref[...]
        onesc = 1.0 + sc_ref[0]
        r = lax.rsqrt(jnp.mean(xv * xv, axis=-1, keepdims=True) + EPS)
        xh = xv * r
        dsh_ref[0] += jnp.sum(dh, axis=0, keepdims=True)
        dsc_ref[0] += jnp.sum(dh * (xh * gv), axis=0, keepdims=True)
        dg_ref[0] += jnp.sum(dh * onesc * xh, axis=0, keepdims=True)
        dxh = dh * (gv * onesc)
        dx = r * (dxh - xh * jnp.mean(dxh * xh, axis=-1, keepdims=True))
        dx_ref[0] = dres_ref[0] + dx

    vec = jax.ShapeDtypeStruct((nb, 1, D), F32)
    vspec = pl.BlockSpec((1, 1, D), lambda b, s: (b, 0, 0))
    tile = pl.BlockSpec((1, ts, D), lambda b, s: (b, s, 0))
    return pl.pallas_call(
        body, name="norm_mod_bwd", grid=(nb, seq // ts),
        in_specs=[tile, tile, tile, pl.BlockSpec((1, D), lambda b, s: (0, 0)), vspec],
        out_specs=[tile, vspec, vspec, vspec],
        out_shape=[jax.ShapeDtypeStruct((nb, seq, D), F32), vec, vec, vec],
        compiler_params=_cp(("parallel", "arbitrary")),
    )(x, dh, dres, g, sc)


def _gate_bwd(dx, y, gt):
    nb, seq, _ = dx.shape
    ts = _seq_tile(seq)

    def body(dx_ref, y_ref, gt_ref, dy_ref, dgt_ref):
        @pl.when(pl.program_id(1) == 0)
        def _():
            dgt_ref[...] = jnp.zeros_like(dgt_ref)

        d = dx_ref[0]
        dy_ref[0] = (gt_ref[0] * d).astype(BF16)
        dgt_ref[0] += jnp.sum(d * y_ref[0].astype(F32), axis=0, keepdims=True)

    vspec = pl.BlockSpec((1, 1, D), lambda b, s: (b, 0, 0))
    tile = pl.BlockSpec((1, ts, D), lambda b, s: (b, s, 0))
    return pl.pallas_call(
        body, name="gate_bwd", grid=(nb, seq // ts),
        in_specs=[tile, tile, vspec], out_specs=[tile, vspec],
        out_shape=[jax.ShapeDtypeStruct((nb, seq, D), BF16), jax.ShapeDtypeStruct((nb, 1, D), F32)],
        compiler_params=_cp(("parallel", "arbitrary")),
    )(dx, y, gt)


def _loss_head(x, tgt, g):
    nb, seq, _ = x.shape
    ts = _seq_tile(seq)

    def body(x_ref, t_ref, g_ref, dx_ref, loss_ref, dg_ref):
        @pl.when(pl.program_id(1) == 0)
        def _():
            loss_ref[...] = jnp.zeros_like(loss_ref)
            dg_ref[...] = jnp.zeros_like(dg_ref)

        xv = x_ref[0]
        gv = g_ref[...]
        r = lax.rsqrt(jnp.mean(xv * xv, axis=-1, keepdims=True) + EPS)
        xh = xv * r
        err = xh * gv - t_ref[0]
        per_tok = jnp.mean(err * err, axis=-1, keepdims=True)
        loss_ref[0] += 0.5 * jnp.sum(per_tok, axis=0, keepdims=True)
        dy = err * (1.0 / D)
        dg_ref[0] += jnp.sum(dy * xh, axis=0, keepdims=True)
        dxh = dy * gv
        dx_ref[0] = r * (dxh - xh * jnp.mean(dxh * xh, axis=-1, keepdims=True))

    tile = pl.BlockSpec((1, ts, D), lambda b, s: (b, s, 0))
    return pl.pallas_call(
        body, name="loss_head", grid=(nb, seq // ts),
        in_specs=[tile, tile, pl.BlockSpec((1, D), lambda b, s: (0, 0))],
        out_specs=[tile, pl.BlockSpec((1, 1, 128), lambda b, s: (b, 0, 0)),
                   pl.BlockSpec((1, 1, D), lambda b, s: (b, 0, 0))],
        out_shape=[jax.ShapeDtypeStruct((nb, seq, D), F32), jax.ShapeDtypeStruct((nb, 1, 128), F32),
                   jax.ShapeDtypeStruct((nb, 1, D), F32)],
        compiler_params=_cp(("parallel", "arbitrary")),
    )(x, tgt, g)


_GELU_C = math.sqrt(2.0 / math.pi)


def _gelu(x):
    return 0.5 * x * (1.0 + jnp.tanh(_GELU_C * (x + 0.044715 * (x * x * x))))


def _gelu_and_grad(x):
    t = jnp.tanh(_GELU_C * (x + 0.044715 * (x * x * x)))
    y = 0.5 * x * (1.0 + t)
    dy = 0.5 * (1.0 + t) + 0.5 * x * (1.0 - t * t) * (_GELU_C * (1.0 + 3.0 * 0.044715 * (x * x)))
    return y, dy


def _tril_mask():
    row = lax.broadcasted_iota(jnp.int32, (CH, CH), 0)
    col = lax.broadcasted_iota(jnp.int32, (CH, CH), 1)
    return row >= col


def _gmlp_fwd(proj, ln_g, ln_b, ws, bst):
    t = proj.shape[0]
    tm = _row_tile(t, 512)

    def body(u_ref, v_ref, lg_ref, lb_ref, ws_ref, bst_ref, o_ref):
        tril = _tril_mask()
        wm = [jnp.where(tril, ws_ref[g], 0.0).astype(BF16) for g in range(NG)]
        for ch in range(tm // CH):
            rows = slice(ch * CH, (ch + 1) * CH)
            u = _gelu(u_ref[rows, :].astype(F32))
            v = _gelu(v_ref[rows, :].astype(F32))
            mu = jnp.mean(v, axis=-1, keepdims=True)
            xc = v - mu
            rstd = lax.rsqrt(jnp.mean(xc * xc, axis=-1, keepdims=True) + EPS)
            vn = ((xc * rstd) * lg_ref[...] + lb_ref[...]).astype(BF16)
            for g in range(NG):
                cols = slice(g * CH, (g + 1) * CH)
                s = _dot(wm[g], vn[:, cols]) + bst_ref[:, g:g + 1]
                o_ref[rows, cols] = (u[:, cols] * s).astype(BF16)

    return pl.pallas_call(
        body, name="gmlp_fwd", grid=(t // tm,),
        in_specs=[pl.BlockSpec((tm, BW), lambda i: (i, 0)),
                  pl.BlockSpec((tm, BW), lambda i: (i, 1)),
                  pl.BlockSpec((1, BW), lambda i: (0, 0)),
                  pl.BlockSpec((1, BW), lambda i: (0, 0)),
                  pl.BlockSpec((NG, CH, CH), lambda i: (0, 0, 0)),
                  pl.BlockSpec((CH, NG), lambda i: (0, 0))],
        out_specs=pl.BlockSpec((tm, BW), lambda i: (i, 0)),
        out_shape=jax.ShapeDtypeStruct((t, BW), BF16),
        compiler_params=_cp(("parallel",)),
    )(proj, proj, ln_g, ln_b, ws, bst)


def _gmlp_bwd(proj, dout, ln_g, ln_b, ws, bst):
    t = proj.shape[0]
    tm = _row_tile(t, 512)

    def body(u_ref, v_ref, do_ref, lg_ref, lb_ref, ws_ref, bst_ref, dp_ref, gws_ref, gbs_ref, glg_ref, glb_ref):
        @pl.when(pl.program_id(0) == 0)
        def _():
            gws_ref[...] = jnp.zeros_like(gws_ref)
            gbs_ref[...] = jnp.zeros_like(gbs_ref)
            glg_ref[...] = jnp.zeros_like(glg_ref)
            glb_ref[...] = jnp.zeros_like(glb_ref)

        tril = _tril_mask()
        wm = [jnp.where(tril, ws_ref[g], 0.0).astype(BF16) for g in range(NG)]
        ones = jnp.ones((CH, CH), BF16)
        lg = lg_ref[...]
        for ch in range(tm // CH):
            rows = slice(ch * CH, (ch + 1) * CH)
            u, du_fac = _gelu_and_grad(u_ref[rows, :].astype(F32))
            v, dv_fac = _gelu_and_grad(v_ref[rows, :].astype(F32))
            do = do_ref[rows, :].astype(F32)
            mu = jnp.mean(v, axis=-1, keepdims=True)
            xc = v - mu
            rstd = lax.rsqrt(jnp.mean(xc * xc, axis=-1, keepdims=True) + EPS)
            xh = xc * rstd
            vn = (xh * lg + lb_ref[...]).astype(BF16)
            dvn_parts = []
            for g in range(NG):
                cols = slice(g * CH, (g + 1) * CH)
                s = _dot(wm[g], vn[:, cols]) + bst_ref[:, g:g + 1]
                dp_ref[rows, cols] = (do[:, cols] * s * du_fac[:, cols]).astype(BF16)
                ds = (do[:, cols] * u[:, cols]).astype(BF16)
                gws_ref[g] += jnp.where(tril, _dot(ds, vn[:, cols], NT), 0.0)
                gbs_ref[g] += _dot(ds, ones)
                dvn_parts.append(_dot(wm[g], ds, TN))
            dvn = jnp.concatenate(dvn_parts, axis=1)
            glb_ref[...] += jnp.sum(dvn, axis=0, keepdims=True)
            glg_ref[...] += jnp.sum(dvn * xh, axis=0, keepdims=True)
            dxh = dvn * lg
            dv = rstd * (dxh - jnp.mean(dxh, axis=-1, keepdims=True)
                         - xh * jnp.mean(dxh * xh, axis=-1, keepdims=True))
            dp_ref[rows, BW:2 * BW] = (dv * dv_fac).astype(BF16)

    small = pl.BlockSpec((NG, CH, CH), lambda i: (0, 0, 0))
    vec = pl.BlockSpec((1, BW), lambda i: (0, 0))
    return pl.pallas_call(
        body, name="gmlp_bwd", grid=(t // tm,),
        in_specs=[pl.BlockSpec((tm, BW), lambda i: (i, 0)),
                  pl.BlockSpec((tm, BW), lambda i: (i, 1)),
                  pl.BlockSpec((tm, BW), lambda i: (i, 0)),
                  vec, vec, small, pl.BlockSpec((CH, NG), lambda i: (0, 0))],
        out_specs=[pl.BlockSpec((tm, 2 * BW), lambda i: (i, 0)), small, small, vec, vec],
        out_shape=[jax.ShapeDtypeStruct((t, 2 * BW), BF16),
                   jax.ShapeDtypeStruct((NG, CH, CH), F32), jax.ShapeDtypeStruct((NG, CH, CH), F32),
                   jax.ShapeDtypeStruct((1, BW), F32), jax.ShapeDtypeStruct((1, BW), F32)],
        compiler_params=_cp(("arbitrary",)),
    )(proj, proj, dout, ln_g, ln_b, ws, bst)


def _pool_bands():
    row = lax.broadcasted_iota(jnp.int32, (CH, CH), 0)
    col = lax.broadcasted_iota(jnp.int32, (CH, CH), 1)
    cur, prev = [], []
    for w in POOL_WINDOWS:
        cur.append(jnp.where((row >= col) & (row - col < w), 1.0, 0.0).astype(BF16))
        prev.append(jnp.where(row + CH - col < w, 1.0, 0.0).astype(BF16))
    return cur, prev


def _pool_inv_count(r0, w):
    pos = r0 + lax.broadcasted_iota(jnp.int32, (CH, 1), 0)
    return 1.0 / jnp.minimum(pos + 1, w).astype(F32)


def _pool_diff(x_ref, r0, rp, has_prev, cur, prev, g):
    cols = slice(g * CH, (g + 1) * CH)
    xc = x_ref[pl.ds(r0, CH), cols]
    xp = x_ref[pl.ds(rp, CH), cols]
    ws = _dot(cur[g], xc) + has_prev * _dot(prev[g], xp)
    return ws * _pool_inv_count(r0, POOL_WINDOWS[g]) - xc.astype(F32)


def _pool_fwd(proj3, pw, pscale):
    nb, seq, _ = proj3.shape
    nch = seq // CH

    def body(x_ref, pw_ref, ps_ref, o_ref):
        cur, prev = _pool_bands()
        pwb = [pw_ref[g].astype(BF16) for g in range(NG)]

        def chunk(ch, carry):
            r0 = pl.multiple_of(ch * CH, CH)
            rp = pl.multiple_of(jnp.maximum(ch - 1, 0) * CH, CH)
            has_prev = jnp.where(ch > 0, 1.0, 0.0)
            for g in range(NG):
                cols = slice(g * CH, (g + 1) * CH)
                d = _pool_diff(x_ref, r0, rp, has_prev, cur, prev, g)
                y = _dot(d.astype(BF16), pwb[g]) * ps_ref[:, cols]
                o_ref[pl.ds(r0, CH), cols] = y.astype(BF16)
            return carry

        lax.fori_loop(0, nch, chunk, 0)

    return pl.pallas_call(
        body, name="pool_fwd", grid=(nb,),
        in_specs=[pl.BlockSpec((None, seq, BW), lambda b: (b, 0, 5)),
                  pl.BlockSpec((NG, CH, CH), lambda b: (0, 0, 0)),
                  pl.BlockSpec((1, BW), lambda b: (0, 0))],
        out_specs=pl.BlockSpec((None, seq, BW), lambda b: (b, 0, 0)),
        out_shape=jax.ShapeDtypeStruct((nb, seq, BW), BF16),
        compiler_params=_cp(("parallel",)),
    )(proj3, pw, pscale)


def _pool_bwd(proj3, dout3, pw, pscale):
    nb, seq, _ = proj3.shape
    nch = seq // CH

    def body(x_ref, do_ref, pw_ref, ps_ref, dx_ref, gpw_ref, gps_ref, e_ref):
        @pl.when(pl.program_id(0) == 0)
        def _():
            gpw_ref[...] = jnp.zeros_like(gpw_ref)
            gps_ref[...] = jnp.zeros_like(gps_ref)

        cur, prev = _pool_bands()
        pwb = [pw_ref[g].astype(BF16) for g in range(NG)]

        def first(ch, carry):
            r0 = pl.multiple_of(ch * CH, CH)
            rp = pl.multiple_of(jnp.maximum(ch - 1, 0) * CH, CH)
            has_prev = jnp.where(ch > 0, 1.0, 0.0)
            for g in range(NG):
                cols = slice(g * CH, (g + 1) * CH)
                d = _pool_diff(x_ref, r0, rp, has_prev, cur, prev, g).astype(BF16)
                do = do_ref[pl.ds(r0, CH), cols].astype(F32)
                ypre = _dot(d, pwb[g])
                gps_ref[:, cols] += jnp.sum(do * ypre, axis=0, keepdims=True)
                dyp = (do * ps_ref[:, cols]).astype(BF16)
                gpw_ref[g] += _dot(d, dyp, TN)
                e_ref[pl.ds(r0, CH), cols] = _dot(dyp, pwb[g], NT)
            return carry

        lax.fori_loop(0, nch, first, 0)

        def second(ch, carry):
            r0 = pl.multiple_of(ch * CH, CH)
            rn = pl.multiple_of(jnp.minimum(ch + 1, nch - 1) * CH, CH)
            has_next = jnp.where(ch < nch - 1, 1.0, 0.0)
            for g in range(NG):
                cols = slice(g * CH, (g + 1) * CH)
                w = POOL_WINDOWS[g]
                dd = e_ref[pl.ds(r0, CH), cols]
                ec = (dd * _pool_inv_count(r0, w)).astype(BF16)
                en = (e_ref[pl.ds(rn, CH), cols] * _pool_inv_count(rn, w)).astype(BF16)
                dx = _dot(cur[g], ec, TN) + has_next * _dot(prev[g], en, TN) - dd
                dx_ref[pl.ds(r0, CH), cols] = dx.astype(BF16)
            return carry

        lax.fori_loop(0, nch, second, 0)

    small = pl.BlockSpec((NG, CH, CH), lambda b: (0, 0, 0))
    vec = pl.BlockSpec((1, BW), lambda b: (0, 0))
    return pl.pallas_call(
        body, name="pool_bwd", grid=(nb,),
        in_specs=[pl.BlockSpec((None, seq, BW), lambda b: (b, 0, 5)),
                  pl.BlockSpec((None, seq, BW), lambda b: (b, 0, 0)), small, vec],
        out_specs=[pl.BlockSpec((None, seq, BW), lambda b: (b, 0, 0)), small, vec],
        out_shape=[jax.ShapeDtypeStruct((nb, seq, BW), BF16),
                   jax.ShapeDtypeStruct((NG, CH, CH), F32), jax.ShapeDtypeStruct((1, BW), F32)],
        scratch_shapes=[pltpu.VMEM((seq, BW), F32)],
        compiler_params=_cp(("arbitrary",)),
    )(proj3, dout3, pw, pscale)


SB_BQ = 256
SB_BK = 256
SB_SCALE = HD ** -0.5


def _sb_tile(qs, k, mask):
    z = _dot(qs, k, NT)
    e = jnp.exp(-jnp.abs(z))
    lb = jnp.minimum(z, 0.0) - jnp.log(1.0 + e)
    lom = lb - z
    if mask is not None:
        lom = jnp.where(mask, lom, 0.0)
    return z, e, lb, lom


def _sb_diag_mask(bq, d):
    row = lax.broadcasted_iota(jnp.int32, (bq, SB_BK), 0)
    col = lax.broadcasted_iota(jnp.int32, (bq, SB_BK), 1)
    return col + d * SB_BK < row


def _sb_scaled(q):
    return (q.astype(F32) * SB_SCALE).astype(BF16)


def _dot_hilo(a, m):
    hi = a.astype(BF16)
    lo = (a - hi.astype(F32)).astype(BF16)
    return _dot(hi, m) + _dot(lo, m)


def _sb_fwd(proj3):
    nb, seq, _ = proj3.shape
    bq = min(SB_BQ, seq)
    nq = seq // bq
    ndiag = bq // SB_BK

    def body(q_ref, k_ref, v_ref, o_ref, t_ref):
        row = lax.broadcasted_iota(jnp.int32, (SB_BK, SB_BK), 0)
        col = lax.broadcasted_iota(jnp.int32, (SB_BK, SB_BK), 1)
        upper = jnp.where(row > col, 1.0, 0.0).astype(BF16)
        heads = [slice(hh * HD, (hh + 1) * HD) for hh in range(2)]

        def qloop(qi, carry):
            q0 = pl.multiple_of(qi * bq, bq)
            qs = [_sb_scaled(q_ref[pl.ds(q0, bq), lanes]) for lanes in heads]

            def step(k0, c, mask):
                tiles = [_sb_tile(q, k_ref[pl.ds(k0, SB_BK), lanes], mask) for lanes, q in zip(heads, qs)]
                sums = [_dot_hilo(lom, upper) for _, _, _, lom in tiles]
                out = []
                for lanes, (acc, cr), (_, _, lb, lom), cs in zip(heads, c, tiles, sums):
                    a = jnp.exp(lb + (cs + cr))
                    if mask is not None:
                        a = jnp.where(mask, a, 0.0)
                    rsum = cs[:, 0:1] + lom[:, 0:1]
                    out.append((acc + _dot(a.astype(BF16), v_ref[pl.ds(k0, SB_BK), lanes]), cr + rsum))
                return tuple(out)

            c = tuple((jnp.zeros((bq, HD), F32), jnp.zeros((bq, 1), F32)) for _ in heads)
            for d in reversed(range(ndiag)):
                c = step(pl.multiple_of(q0 + d * SB_BK, SB_BK), c, _sb_diag_mask(bq, d))
            npast = qi * ndiag
            c = lax.fori_loop(
                0, npast, lambda i, c: step(pl.multiple_of((npast - 1 - i) * SB_BK, SB_BK), c, None), c)
            for lanes, (acc, cr) in zip(heads, c):
                o_ref[pl.ds(q0, bq), lanes] = acc.astype(BF16)
                t_ref[pl.ds(q0, bq), lanes] = jnp.broadcast_to(cr, (bq, HD))
            return carry

        lax.fori_loop(0, nq, qloop, 0)

    def spec(c0):
        return pl.BlockSpec((None, seq, 128), lambda b, p: (b, 0, c0 + p))

    return pl.pallas_call(
        body, name="sb_fwd", grid=(nb, BW // 128),
        in_specs=[spec(8), spec(12), spec(16)],
        out_specs=[spec(0), spec(0)],
        out_shape=[jax.ShapeDtypeStruct((nb, seq, BW), BF16), jax.ShapeDtypeStruct((nb, seq, BW), F32)],
        compiler_params=_cp(("parallel", "parallel")),
    )(proj3, proj3, proj3)


def _sb_bwd(proj3, do3, tot3):
    nb, seq, _ = proj3.shape
    bq = min(SB_BQ, seq)
    nq = seq // bq
    ndiag = bq // SB_BK

    def body(q_ref, k_ref, v_ref, do_ref, t_ref, dq_ref, dk_ref, dv_ref, dk_acc, dv_acc):
        row = lax.broadcasted_iota(jnp.int32, (SB_BK, SB_BK), 0)
        col = lax.broadcasted_iota(jnp.int32, (SB_BK, SB_BK), 1)
        upper = jnp.where(row > col, 1.0, 0.0).astype(BF16)
        lower = jnp.where(row < col, 1.0, 0.0).astype(BF16)
        dk_acc[...] = jnp.zeros_like(dk_acc)
        dv_acc[...] = jnp.zeros_like(dv_acc)
        heads = [slice(hh * HD, (hh + 1) * HD) for hh in range(2)]

        def qloop(qi, carry):
            q0 = pl.multiple_of(qi * bq, bq)
            qs = [_sb_scaled(q_ref[pl.ds(q0, bq), lanes]) for lanes in heads]
            dos = [do_ref[pl.ds(q0, bq), lanes] for lanes in heads]
            tots = [t_ref[pl.ds(q0, bq), hh * HD:hh * HD + 1] for hh in range(2)]

            def step(k0, c, mask):
                ks = [k_ref[pl.ds(k0, SB_BK), lanes] for lanes in heads]
                tiles = [_sb_tile(q, k, mask) for q, k in zip(qs, ks)]
                sums = [_dot_hilo(lom, upper) for _, _, _, lom in tiles]
                das = [_dot(do, v_ref[pl.ds(k0, SB_BK), lanes], NT) for do, lanes in zip(dos, heads)]
                gls, rsums, avs = [], [], []
                for hh, (_, cpre, _) in enumerate(c):
                    _, _, lb, lom = tiles[hh]
                    rsum = sums[hh][:, 0:1] + lom[:, 0:1]
                    a = jnp.exp(lb + (sums[hh] + (tots[hh] - cpre - rsum)))
                    if mask is not None:
                        a = jnp.where(mask, a, 0.0)
                    gls.append(das[hh] * a)
                    rsums.append(rsum)
                    avs.append(a.astype(BF16))
                pres = [_dot_hilo(gl, lower) for gl in gls]
                out = []
                for hh, (dq, cpre, gpre) in enumerate(c):
                    z, e, _, _ = tiles[hh]
                    inv = 1.0 / (1.0 + e)
                    pos = z >= 0.0
                    beta = jnp.where(pos, 1.0, e) * inv
                    omb = jnp.where(pos, e, 1.0) * inv
                    dz = gls[hh] * omb - beta * (pres[hh] + gpre)
                    if mask is not None:
                        dz = jnp.where(mask, dz, 0.0)
                    dz = dz.astype(BF16)
                    dk_acc[hh, pl.ds(k0, SB_BK), :] += _dot(dz, qs[hh], TN)
                    dv_acc[hh, pl.ds(k0, SB_BK), :] += _dot(avs[hh], dos[hh], TN)
                    gsum = pres[hh][:, SB_BK - 1:SB_BK] + gls[hh][:, SB_BK - 1:SB_BK]
                    out.append((dq + _dot(dz, ks[hh]), cpre + rsums[hh], gpre + gsum))
                return tuple(out)

            c = tuple((jnp.zeros((bq, HD), F32), jnp.zeros((bq, 1), F32), jnp.zeros((bq, 1), F32))
                      for _ in heads)
            npast = qi * ndiag
            c = lax.fori_loop(0, npast, lambda kb, c: step(pl.multiple_of(kb * SB_BK, SB_BK), c, None), c)
            for d in range(ndiag):
                c = step(pl.multiple_of(q0 + d * SB_BK, SB_BK), c, _sb_diag_mask(bq, d))
            for lanes, (dq, _, _) in zip(heads, c):
                dq_ref[pl.ds(q0, bq), lanes] = (dq * SB_SCALE).astype(BF16)
            return carry

        lax.fori_loop(0, nq, qloop, 0)
        for hh in range(2):
            lanes = slice(hh * HD, (hh + 1) * HD)
            dk_ref[:, lanes] = dk_acc[hh].astype(BF16)
            dv_ref[:, lanes] = dv_acc[hh].astype(BF16)

    def spec(c0):
        return pl.BlockSpec((None, seq, 128), lambda b, p: (b, 0, c0 + p))

    return pl.pallas_call(
        body, name="sb_bwd", grid=(nb, BW // 128),
        in_specs=[spec(8), spec(12), spec(16), spec(0), spec(0)],
        out_specs=[spec(0), spec(0), spec(0)],
        out_shape=[jax.ShapeDtypeStruct((nb, seq, BW), BF16)] * 3,
        scratch_shapes=[pltpu.VMEM((2, seq, HD), F32), pltpu.VMEM((2, seq, HD), F32)],
        compiler_params=_cp(("parallel", "parallel")),
    )(proj3, proj3, proj3, do3, tot3)


def _merge_fwd(brs, wb, proj):
    t = proj.shape[0]
    tm = _row_tile(t, 512)
    tn = 512
    nj = D // tn

    def body(b0, b1, b2, wb_ref, l0, l1, l2, m_ref, y0, y1, y2):
        acc = None
        for br, n, lg, y_ref in ((b0, 0, l0, y0), (b1, 1, l1, y1), (b2, 2, l2, y2)):
            y = _dot(br[...], wb_ref[n])
            y_ref[...] = y.astype(BF16)
            term = jax.nn.sigmoid(lg[...].astype(F32)) * y
            acc = term if acc is None else acc + term
        m_ref[...] = acc.astype(BF16)

    def lspec(n):
        return pl.BlockSpec((tm, tn), lambda i, j: (i, (3 * D + n * D) // tn + j))

    tile = pl.BlockSpec((tm, tn), lambda i, j: (i, j))
    bspec = pl.BlockSpec((tm, BW), lambda i, j: (i, 0))
    return pl.pallas_call(
        body, name="merge_fwd", grid=(t // tm, nj),
        in_specs=[bspec, bspec, bspec, pl.BlockSpec((NB, BW, tn), lambda i, j: (0, 0, j)),
                  lspec(0), lspec(1), lspec(2)],
        out_specs=[tile] * 4,
        out_shape=[jax.ShapeDtypeStruct((t, D), BF16)] * 4,
        compiler_params=_cp(("parallel", "parallel")),
    )(brs[0], brs[1], brs[2], wb, proj, proj, proj)


def _merge_bwd(dm, ys, proj):
    t = proj.shape[0]
    tm = _row_tile(t, 512)
    tn = 512

    def body(dm_ref, y0, y1, y2, l0, l1, l2, dl0, dl1, dl2, dy0, dy1, dy2):
        dmv = dm_ref[...].astype(F32)
        for y_ref, lg, dl_ref, dy_ref in ((y0, l0, dl0, dy0), (y1, l1, dl1, dy1), (y2, l2, dl2, dy2)):
            g = jax.nn.sigmoid(lg[...].astype(F32))
            dl_ref[...] = (dmv * y_ref[...].astype(F32) * g * (1.0 - g)).astype(BF16)
            dy_ref[...] = (dmv * g).astype(BF16)

    def lspec(n):
        return pl.BlockSpec((tm, tn), lambda i, j: (i, (3 * D + n * D) // tn + j))

    tile = pl.BlockSpec((tm, tn), lambda i, j: (i, j))
    return pl.pallas_call(
        body, name="merge_bwd", grid=(t // tm, D // tn),
        in_specs=[tile] * 4 + [lspec(0), lspec(1), lspec(2)],
        out_specs=[tile] * 6,
        out_shape=[jax.ShapeDtypeStruct((t, D), BF16)] * 6,
        compiler_params=_cp(("parallel", "parallel")),
    )(dm, ys[0], ys[1], ys[2], proj, proj, proj)


def _swiglu_fwd(f):
    t = f.shape[0]
    tm = _row_tile(t, 512)
    tn = 1024

    def body(g_ref, u_ref, a_ref):
        g = g_ref[...].astype(F32)
        a_ref[...] = (g * jax.nn.sigmoid(g) * u_ref[...].astype(F32)).astype(BF16)

    return pl.pallas_call(
        body, name="swiglu_fwd", grid=(t // tm, FFP // tn),
        in_specs=[pl.BlockSpec((tm, tn), lambda i, j: (i, j)),
                  pl.BlockSpec((tm, tn), lambda i, j: (i, FFP // tn + j))],
        out_specs=pl.BlockSpec((tm, tn), lambda i, j: (i, j)),
        out_shape=jax.ShapeDtypeStruct((t, FFP), BF16),
        compiler_params=_cp(("parallel", "parallel")),
    )(f, f)


def _swiglu_bwd(f, da):
    t = f.shape[0]
    tm = _row_tile(t, 512)
    tn = 1024

    def body(g_ref, u_ref, da_ref, dg_ref, du_ref):
        g = g_ref[...].astype(F32)
        u = u_ref[...].astype(F32)
        d = da_ref[...].astype(F32)
        s = jax.nn.sigmoid(g)
        dg_ref[...] = (d * u * (s * (1.0 + g * (1.0 - s)))).astype(BF16)
        du_ref[...] = (d * (g * s)).astype(BF16)

    return pl.pallas_call(
        body, name="swiglu_bwd", grid=(t // tm, FFP // tn),
        in_specs=[pl.BlockSpec((tm, tn), lambda i, j: (i, j)),
                  pl.BlockSpec((tm, tn), lambda i, j: (i, FFP // tn + j)),
                  pl.BlockSpec((tm, tn), lambda i, j: (i, j))],
        out_specs=[pl.BlockSpec((tm, tn), lambda i, j: (i, j)),
                   pl.BlockSpec((tm, tn), lambda i, j: (i, j))],
        out_shape=[jax.ShapeDtypeStruct((t, FFP), BF16)] * 2,
        compiler_params=_cp(("parallel", "parallel")),
    )(f, f, da)


def _adamw_reduce(name, parts, w, m, v):
    shape = w.shape
    cols = shape[-1]
    rows = int(math.prod(shape[:-1])) if len(shape) > 1 else 1
    npart = parts.shape[0]
    tr = rows if rows <= 512 else 512
    assert rows % tr == 0
    c1 = 1.0 - ADAM_B1 ** ADAM_STEP
    c2 = 1.0 - ADAM_B2 ** ADAM_STEP

    def body(p_ref, w_ref, m_ref, v_ref, g_ref, d_ref, mo_ref, vo_ref):
        g = p_ref[0].astype(F32)
        for p in range(1, npart):
            g = g + p_ref[p].astype(F32)
        mn = ADAM_B1 * m_ref[...] + (1.0 - ADAM_B1) * g
        vn = ADAM_B2 * v_ref[...] + (1.0 - ADAM_B2) * (g * g)
        m_hat = mn / c1
        v_hat = vn / c2
        g_ref[...] = g
        d_ref[...] = -ADAM_LR * (m_hat / (jnp.sqrt(v_hat) + ADAM_EPS) + ADAM_WD * w_ref[...])
        mo_ref[...] = mn
        vo_ref[...] = vn

    tile = pl.BlockSpec((tr, cols), lambda i: (i, 0))
    sds = jax.ShapeDtypeStruct((rows, cols), F32)
    outs = pl.pallas_call(
        body, name=name, grid=(rows // tr,),
        in_specs=[pl.BlockSpec((npart, tr, cols), lambda i: (0, i, 0)), tile, tile, tile],
        out_specs=[tile] * 4, out_shape=[sds] * 4,
        compiler_params=_cp(("parallel",)),
    )(parts.reshape(npart, rows, cols), w.reshape(rows, cols), m.reshape(rows, cols), v.reshape(rows, cols))
    return tuple(o.reshape(shape) for o in outs)


def _pad_ffn_in(w):
    lead = w.shape[:-1]
    w = w.reshape(lead + (2, FF_HALF))
    w = jnp.pad(w, [(0, 0)] * len(lead) + [(0, 0), (0, FF_HALF_PAD - FF_HALF)])
    return w.reshape(lead + (FF_IN_PAD,))


def _unpad_ffn_in(w):
    lead = w.shape[:-1]
    return w.reshape(lead + (2, FF_HALF_PAD))[..., :FF_HALF].reshape(lead + (FF_IN_SHARD,))


def kernel(x, c, rms_g1, rms_g2, w_ada, b_ada, w_in, gm_ln_g, gm_ln_b, gm_w_spatial, gm_b_spatial, pool_w, pool_scale, w_branch, w_out, w_ffn_in, w_ffn_out, final_g, loss_target, m_rms_g1, m_rms_g2, m_w_ada, m_b_ada, m_w_in, m_gm_ln_g, m_gm_ln_b, m_gm_w_spatial, m_gm_b_spatial, m_pool_w, m_pool_scale, m_w_branch, m_w_out, m_w_ffn_in, m_w_ffn_out, m_final_g, v_rms_g1, v_rms_g2, v_w_ada, v_b_ada, v_w_in, v_gm_ln_g, v_gm_ln_b, v_gm_w_spatial, v_gm_b_spatial, v_pool_w, v_pool_scale, v_w_branch, v_w_out, v_w_ffn_in, v_w_ffn_out, v_final_g):
    nb, seq, _ = x.shape
    nl = w_in.shape[0]
    t = nb * seq
    ntot = NDEV * nb
    me = _my_index()
    assert x.shape[2] == D and w_in.shape[1:] == (D, 768) and w_ffn_in.shape[1:] == (D, FF_IN_SHARD)
    assert seq % CH == 0

    w_ffn_in_p = _pad_ffn_in(w_ffn_in).astype(BF16)
    w_ffn_out_p = jnp.pad(w_ffn_out, ((0, 0), (0, FF_HALF_PAD - FF_HALF), (0, 0))).astype(BF16)
    w_in_b = w_in.astype(BF16)
    w_branch_b = w_branch.astype(BF16)
    w_out_b = w_out.astype(BF16)
    gathered = []
    for l in range(nl):
        g_in, g_br, g_out, g_fi, g_fo = _exchange(
            [w_in_b[l], w_branch_b[l], w_out_b[l], w_ffn_in_p[l], w_ffn_out_p[l]], "gather_weights", False)
        gathered.append(dict(
            w_in=g_in,
            w_branch=jnp.transpose(g_br, (1, 2, 0, 3)).reshape(NB, BW, D),
            w_out=g_out.reshape(D, D),
            w_ffn_in=g_fi,
            w_ffn_out=g_fo.reshape(FFP, D)))

    (c_all,) = _exchange([c], "gather_c", False)
    c_all = c_all.reshape(ntot, D)
    b_blk = lax.dynamic_slice_in_dim(b_ada, me * 768, 768, axis=1).reshape(nl, 1, 768)
    mod_blk = _ada_fwd(c_all, w_ada, b_blk)
    (mod_all,) = _exchange([mod_blk], "gather_mod", False)
    mod_all = jnp.transpose(mod_all, (1, 2, 0, 3)).reshape(nl, ntot, NMOD * D)
    mod = lax.dynamic_slice_in_dim(mod_all, me * nb, nb, axis=1).reshape(nl, nb, NMOD, 1, D)

    saved = []
    xc = x
    for l in range(nl):
        gw = gathered[l]
        sh1, sc1, gt1, sh2, sc2, gt2 = [mod[l, :, i] for i in range(NMOD)]
        h = _norm_mod_fwd(xc, rms_g1[l].reshape(1, D), sc1, sh1).reshape(t, D)
        proj = _mm_colblocked("proj_fwd", h, gw["w_in"], BF16)
        proj3 = proj.reshape(nb, seq, IN_COLS)
        br_gm = _gmlp_fwd(proj, gm_ln_g[l].reshape(1, BW), gm_ln_b[l].reshape(1, BW),
                          gm_w_spatial[l], gm_b_spatial[l].T)
        br_sb, sb_tot = _sb_fwd(proj3)
        br_pool = _pool_fwd(proj3, pool_w[l], pool_scale[l].reshape(1, BW))
        brs = [br_gm, br_sb.reshape(t, BW), br_pool.reshape(t, BW)]
        merged, y0, y1, y2 = _merge_fwd(brs, gw["w_branch"], proj)
        x_mid, mo = _mm_residual("out_fwd", merged, gw["w_out"], xc.reshape(t, D), gt1, seq)
        x_mid = x_mid.reshape(nb, seq, D)
        h2 = _norm_mod_fwd(x_mid, rms_g2[l].reshape(1, D), sc2, sh2).reshape(t, D)
        f = _mm_colblocked("ffn_in_fwd", h2, gw["w_ffn_in"], BF16)
        act = _swiglu_fwd(f)
        x_out, fo = _mm_residual("ffn_out_fwd", act, gw["w_ffn_out"], x_mid.reshape(t, D), gt2, seq)
        saved.append(dict(x_in=xc, h=h, proj=proj, brs=brs, sb_tot=sb_tot, ys=(y0, y1, y2), merged=merged,
                          mo=mo, x_mid=x_mid, h2=h2, f=f, act=act, fo=fo))
        xc = x_out.reshape(nb, seq, D)

    dx, loss_part, dfinal_part = _loss_head(xc, loss_target, final_g.reshape(1, D))
    loss = lax.psum(jnp.sum(loss_part[:, 0, 0]), ("x", "y", "c"))

    big = {}
    small_parts = {k: [None] * nl for k in ("rms_g1", "rms_g2", "gm_ln_g", "gm_ln_b", "gm_w_spatial",
                                            "gm_b_spatial", "pool_w", "pool_scale")}
    dmod = [None] * nl
    big_names = ("w_in", "w_branch", "w_out", "w_ffn_in", "w_ffn_out")
    for name in big_names:
        big[name] = [None] * nl
    for l in reversed(range(nl)):
        gw = gathered[l]
        sv = saved[l]
        sh1, sc1, gt1, sh2, sc2, gt2 = [mod[l, :, i] for i in range(NMOD)]
        dfo, dgt2 = _gate_bwd(dx, sv["fo"].reshape(nb, seq, D), gt2)
        dfo = dfo.reshape(t, D)
        dact = _mm_nt("ffn_out_dgrad", dfo, gw["w_ffn_out"], BF16)
        g_ffn_out = _mm_tn("ffn_out_wgrad", sv["act"], dfo)
        dfg, dfu = _swiglu_bwd(sv["f"], dact)
        df = jnp.concatenate([dfg, dfu], axis=1)
        dh2 = _mm_colblocked_nt("ffn_in_dgrad", df, gw["w_ffn_in"], F32)
        g_ffn_in = _mm_colblocked_tn("ffn_in_wgrad", sv["h2"], df)
        dx_mid, dsh2, dsc2, dg2 = _norm_mod_bwd(sv["x_mid"], dh2.reshape(nb, seq, D), dx,
                                                rms_g2[l].reshape(1, D), sc2)
        dmo, dgt1 = _gate_bwd(dx_mid, sv["mo"].reshape(nb, seq, D), gt1)
        dmo = dmo.reshape(t, D)
        dmerged = _mm_nt("out_dgrad", dmo, gw["w_out"], BF16)
        g_out = _mm_tn("out_wgrad", sv["merged"], dmo)
        dls_dys = _merge_bwd(dmerged, sv["ys"], sv["proj"])
        dls, dys = dls_dys[:3], dls_dys[3:]
        dbrs, g_br = [], []
        for n in range(NB):
            dbrs.append(_mm_nt("branch_dgrad", dys[n], gw["w_branch"], BF16, w_lead=n))
            g_br.append(_mm_tn("branch_wgrad", sv["brs"][n], dys[n]))
        proj3 = sv["proj"].reshape(nb, seq, IN_COLS)
        d_gm, g_ws, g_bs, g_lg, g_lb = _gmlp_bwd(sv["proj"], dbrs[0], gm_ln_g[l].reshape(1, BW),
                                                 gm_ln_b[l].reshape(1, BW), gm_w_spatial[l], gm_b_spatial[l].T)
        d_sb = _sb_bwd(proj3, dbrs[1].reshape(nb, seq, BW), sv["sb_tot"])
        d_pool, g_pw, g_ps = _pool_bwd(proj3, dbrs[2].reshape(nb, seq, BW), pool_w[l], pool_scale[l].reshape(1, BW))
        dproj = jnp.concatenate([d_gm] + [a.reshape(t, BW) for a in d_sb] + [d_pool.reshape(t, BW)] + list(dls),
                                axis=1)
        dh = _mm_colblocked_nt("proj_dgrad", dproj, gw["w_in"], F32)
        g_in = _mm_colblocked_tn("proj_wgrad", sv["h"], dproj)
        dx, dsh1, dsc1, dg1 = _norm_mod_bwd(sv["x_in"], dh.reshape(nb, seq, D), dx_mid,
                                            rms_g1[l].reshape(1, D), sc1)

        dmod[l] = jnp.concatenate([dsh1, dsc1, dgt1, dsh2, dsc2, dgt2], axis=-1)
        small_parts["rms_g1"][l] = jnp.sum(dg1, axis=0)
        small_parts["rms_g2"][l] = jnp.sum(dg2, axis=0)
        small_parts["gm_ln_g"][l] = g_lg
        small_parts["gm_ln_b"][l] = g_lb
        small_parts["gm_w_spatial"][l] = g_ws
        small_parts["gm_b_spatial"][l] = g_bs[:, :, 0]
        small_parts["pool_w"][l] = g_pw
        small_parts["pool_scale"][l] = g_ps

        g_br_dev = jnp.transpose(jnp.stack(g_br).reshape(NB, BW, NDEV, D // NDEV), (2, 0, 1, 3))
        r_in, r_br, r_out, r_fi, r_fo = _exchange(
            [g_in, g_br_dev, g_out.reshape(NDEV, D // NDEV, D), g_ffn_in,
             g_ffn_out.reshape(NDEV, FF_HALF_PAD, D)], "scatter_grads", True)
        big["w_in"][l] = _adamw_reduce("adamw_w_in", r_in, w_in[l], m_w_in[l], v_w_in[l])
        big["w_branch"][l] = _adamw_reduce("adamw_w_branch", r_br, w_branch[l], m_w_branch[l], v_w_branch[l])
        big["w_out"][l] = _adamw_reduce("adamw_w_out", r_out, w_out[l], m_w_out[l], v_w_out[l])
        big["w_ffn_in"][l] = _adamw_reduce("adamw_w_ffn_in", _unpad_ffn_in(r_fi), w_ffn_in[l],
                                           m_w_ffn_in[l], v_w_ffn_in[l])
        big["w_ffn_out"][l] = _adamw_reduce("adamw_w_ffn_out", r_fo[:, :FF_HALF], w_ffn_out[l],
                                            m_w_ffn_out[l], v_w_ffn_out[l])

    dmod_mine = jnp.stack(dmod).reshape(nl, nb, NMOD * D)
    names = list(small_parts)
    stacked = [jnp.stack(small_parts[k]) for k in names]
    gathered_small = _exchange(stacked + [dfinal_part, dmod_mine], "gather_small", False)
    dmod_all = jnp.transpose(gathered_small[-1], (1, 0, 2, 3)).reshape(nl, ntot, NMOD * D)
    dfinal_all = gathered_small[-2].reshape(ntot, D)

    results = {}
    weights = dict(rms_g1=(rms_g1, m_rms_g1, v_rms_g1), rms_g2=(rms_g2, m_rms_g2, v_rms_g2),
                   gm_ln_g=(gm_ln_g, m_gm_ln_g, v_gm_ln_g), gm_ln_b=(gm_ln_b, m_gm_ln_b, v_gm_ln_b),
                   gm_w_spatial=(gm_w_spatial, m_gm_w_spatial, v_gm_w_spatial),
                   gm_b_spatial=(gm_b_spatial, m_gm_b_spatial, v_gm_b_spatial),
                   pool_w=(pool_w, m_pool_w, v_pool_w), pool_scale=(pool_scale, m_pool_scale, v_pool_scale))
    for k, parts in zip(names, gathered_small[:len(names)]):
        w, m, v = weights[k]
        results[k] = _adamw_reduce("adamw_" + k, parts.reshape((NDEV,) + w.shape), w, m, v)
    results["final_g"] = _adamw_reduce("adamw_final_g", dfinal_all, final_g, m_final_g, v_final_g)
    results["b_ada"] = _adamw_reduce("adamw_b_ada", jnp.transpose(dmod_all, (1, 0, 2)), b_ada, m_b_ada, v_b_ada)
    dmod_blk = lax.dynamic_slice_in_dim(dmod_all, me * 768, 768, axis=2)
    g_w_ada = _ada_bwd(c_all, dmod_blk)
    results["w_ada"] = _adamw_reduce("adamw_w_ada", g_w_ada[None], w_ada, m_w_ada, v_w_ada)
    for name in big_names:
        results[name] = tuple(jnp.stack([big[name][l][i] for l in range(nl)]) for i in range(4))

    order = ["rms_g1", "rms_g2", "w_ada", "b_ada", "w_in", "gm_ln_g", "gm_ln_b", "gm_w_spatial", "gm_b_spatial",
             "pool_w", "pool_scale", "w_branch", "w_out", "w_ffn_in", "w_ffn_out", "final_g"]
    out = [loss, dx]
    for i in range(4):
        out.extend(results[k][i] for k in order)
    return tuple(out)
```

```python
import functools
import math

import jax
import jax.numpy as jnp
from jax import lax
from jax.experimental import pallas as pl
from jax.experimental.pallas import tpu as pltpu

F32 = jnp.float32
BF16 = jnp.bfloat16
MESH = pl.DeviceIdType.MESH

D = 1024
BW = 512
NB = 3
CH = 128
NG = 4
HD = 64
POOL_WINDOWS = (2, 4, 8, 16)
DFF = 2816
NMOD = 6
EPS = 1e-6
IN_COLS = 6 * D
NDEV = 8
FF_IN_SHARD = 2 * DFF // NDEV
FF_HALF = FF_IN_SHARD // 2
FF_HALF_PAD = 384
FF_IN_PAD = 2 * FF_HALF_PAD
FFP = NDEV // 2 * FF_IN_PAD

ADAM_LR = 0.001
ADAM_B1 = 0.9
ADAM_B2 = 0.999
ADAM_EPS = 1e-08
ADAM_WD = 0.01
ADAM_STEP = 10

VMEM_LIMIT = 48 * 1024 * 1024

NN = (((1,), (0,)), ((), ()))
NT = (((1,), (1,)), ((), ()))
TN = (((0,), (0,)), ((), ()))


def _cp(sem=None):
    return pltpu.CompilerParams(dimension_semantics=sem, vmem_limit_bytes=VMEM_LIMIT)


def _dot(a, b, dims=NN):
    return lax.dot_general(a, b, dims, preferred_element_type=F32)


def _my_index():
    return 4 * lax.axis_index("x") + 2 * lax.axis_index("y") + lax.axis_index("c")


def _peer(k):
    x, y, c = lax.axis_index("x"), lax.axis_index("y"), lax.axis_index("c")
    px = 1 - x if k & 4 else x
    py = 1 - y if k & 2 else y
    pc = 1 - c if k & 1 else c
    return (px, py, pc), 4 * px + 2 * py + pc


def _exchange(xs, name, all_to_all):
    n = len(xs)

    def body(*refs):
        _exchange_start(refs[:n], refs[n:2 * n], refs[2 * n:], all_to_all)
        _exchange_finish(refs[:n], refs[n:2 * n], refs[2 * n:], all_to_all)

    return pl.pallas_call(
        body, name=name, out_shape=_exchange_out_shape(xs, all_to_all),
        in_specs=[_HBM] * n, out_specs=[_HBM] * n, scratch_shapes=_exchange_sems(n),
    )(*xs)


_HBM = pl.BlockSpec(memory_space=pl.ANY)


def _exchange_out_shape(xs, all_to_all):
    if all_to_all:
        return [jax.ShapeDtypeStruct(x.shape, x.dtype) for x in xs]
    return [jax.ShapeDtypeStruct((NDEV,) + x.shape, x.dtype) for x in xs]


def _exchange_sems(n):
    return [pltpu.SemaphoreType.DMA((n * 7,)), pltpu.SemaphoreType.DMA((n * 7,)), pltpu.SemaphoreType.DMA((n,))]


def _exchange_copies(ins, outs, sems, all_to_all):
    send_sems, recv_sems, local_sems = sems
    me = _my_index()
    local, sends, recvs = [], [], []
    for a in range(len(ins)):
        src = ins[a].at[me] if all_to_all else ins[a]
        local.append(pltpu.make_async_copy(src, outs[a].at[me], local_sems.at[a]))
    for k in range(1, NDEV):
        dev, idx = _peer(k)
        for a in range(len(ins)):
            src = ins[a].at[idx] if all_to_all else ins[a]
            sem = dict(send_sem=send_sems.at[a * 7 + k - 1], recv_sem=recv_sems.at[a * 7 + k - 1],
                       device_id=dev, device_id_type=MESH)
            sends.append(pltpu.make_async_remote_copy(src_ref=src, dst_ref=outs[a].at[me], **sem))
            recvs.append(pltpu.make_async_remote_copy(src_ref=src, dst_ref=outs[a].at[idx], **sem))
    return local, sends, recvs


def _exchange_start(ins, outs, sems, all_to_all):
    local, sends, _ = _exchange_copies(ins, outs, sems, all_to_all)
    for cp in local + sends:
        cp.start()


def _exchange_finish(ins, outs, sems, all_to_all):
    local, sends, recvs = _exchange_copies(ins, outs, sems, all_to_all)
    for cp in sends:
        cp.wait_send()
    for cp in recvs:
        cp.wait_recv()
    for cp in local:
        cp.wait()


def _host_exchange(body, n_in, n_out, grid, xs, all_to_all):
    n = len(xs)
    if n == 0:
        return body, [], [], [], []

    def hosted(*refs):
        ins, ex_ins = refs[:n_in], refs[n_in:n_in + n]
        outs, ex_outs = refs[n_in + n:n_in + n + n_out], refs[n_in + n + n_out:n_in + 2 * n + n_out]
        scratch = refs[n_in + 2 * n + n_out:]
        own, sems = scratch[:len(scratch) - 3], scratch[len(scratch) - 3:]
        first = functools.reduce(jnp.logical_and, [pl.program_id(a) == 0 for a in range(len(grid))])
        last = functools.reduce(jnp.logical_and, [pl.program_id(a) == grid[a] - 1 for a in range(len(grid))])

        @pl.when(first)
        def _():
            _exchange_start(ex_ins, ex_outs, sems, all_to_all)

        body(*ins, *outs, *own)

        @pl.when(last)
        def _():
            _exchange_finish(ex_ins, ex_outs, sems, all_to_all)

    return hosted, [_HBM] * n, [_HBM] * n, _exchange_out_shape(xs, all_to_all), _exchange_sems(n)


def _mm(name, a, b, grid, a_spec, b_spec, o_spec, out_sds, dims, acc_shape):
    nk = grid[2]

    def body(a_ref, b_ref, o_ref, acc_ref):
        k = pl.program_id(2)

        @pl.when(k == 0)
        def _():
            acc_ref[...] = jnp.zeros_like(acc_ref)

        acc_ref[...] += _dot(a_ref[...].astype(BF16), b_ref[...].astype(BF16), dims)

        @pl.when(k == nk - 1)
        def _():
            o_ref[...] = acc_ref[...].astype(o_ref.dtype)

    return pl.pallas_call(
        body, name=name, grid=grid, in_specs=[a_spec, b_spec], out_specs=o_spec, out_shape=out_sds,
        scratch_shapes=[pltpu.VMEM(acc_shape, F32)],
        compiler_params=_cp(("parallel", "parallel", "arbitrary")),
    )(a, b)


def _row_tile(t, want):
    tm = min(t, want)
    assert t % tm == 0
    return tm


def _mm_colblocked(name, a, wg, out_dtype):
    t = a.shape[0]
    tm = _row_tile(t, 1024)
    return _mm(name, a, wg, (t // tm, NDEV, 1),
               pl.BlockSpec((tm, D), lambda i, j, k: (i, 0)),
               pl.BlockSpec((None, D, 768), lambda i, j, k: (j, 0, 0)),
               pl.BlockSpec((tm, 768), lambda i, j, k: (i, j)),
               jax.ShapeDtypeStruct((t, NDEV * 768), out_dtype), NN, (tm, 768))


def _mm_colblocked_nt(name, g, wg, out_dtype):
    t = g.shape[0]
    tm = _row_tile(t, 1024)
    return _mm(name, g, wg, (t // tm, 1, NDEV),
               pl.BlockSpec((tm, 768), lambda i, j, k: (i, k)),
               pl.BlockSpec((None, D, 768), lambda i, j, k: (k, 0, 0)),
               pl.BlockSpec((tm, D), lambda i, j, k: (i, 0)),
               jax.ShapeDtypeStruct((t, D), out_dtype), NT, (tm, D))


def _mm_colblocked_tn(name, a, g):
    t = a.shape[0]
    tk = _row_tile(t, 1024)
    return _mm(name, a, g, (1, NDEV, t // tk),
               pl.BlockSpec((tk, D), lambda i, j, k: (k, 0)),
               pl.BlockSpec((tk, 768), lambda i, j, k: (k, j)),
               pl.BlockSpec((None, D, 768), lambda i, j, k: (j, 0, 0)),
               jax.ShapeDtypeStruct((NDEV, D, 768), BF16), TN, (D, 768))


def _mm_nt(name, a, w, out_dtype, a_col=0, w_lead=None):
    t = a.shape[0]
    if w_lead is None:
        kdim, n = w.shape
        b_spec = pl.BlockSpec((min(kdim, 1024), n), lambda i, j, k: (j, 0))
    else:
        _, kdim, n = w.shape
        b_spec = pl.BlockSpec((None, min(kdim, 1024), n), lambda i, j, k: (w_lead, j, 0))
    tn = min(kdim, 1024)
    tm = _row_tile(t, 1024)
    return _mm(name, a, w, (t // tm, kdim // tn, 1),
               pl.BlockSpec((tm, n), lambda i, j, k: (i, a_col)),
               b_spec,
               pl.BlockSpec((tm, tn), lambda i, j, k: (i, j)),
               jax.ShapeDtypeStruct((t, kdim), out_dtype), NT, (tm, tn))


def _mm_tn(name, a, g, out_dtype=BF16):
    t, kdim = a.shape
    n = g.shape[1]
    tk = _row_tile(t, 1024)
    tm = min(kdim, 1024)
    tn = min(n, 1024)
    return _mm(name, a, g, (kdim // tm, n // tn, t // tk),
               pl.BlockSpec((tk, tm), lambda i, j, k: (k, i)),
               pl.BlockSpec((tk, tn), lambda i, j, k: (k, j)),
               pl.BlockSpec((tm, tn), lambda i, j, k: (i, j)),
               jax.ShapeDtypeStruct((kdim, n), out_dtype), TN, (tm, tn))


def _mm_residual(name, a, w, x, gt, seq):
    t, kdim = a.shape
    tm = _row_tile(seq, 512)
    tn = 512
    tk = min(kdim, 1024)
    nk = kdim // tk
    per = seq // tm

    def body(a_ref, w_ref, x_ref, gt_ref, xo_ref, y_ref, acc_ref):
        k = pl.program_id(2)

        @pl.when(k == 0)
        def _():
            acc_ref[...] = jnp.zeros_like(acc_ref)

        acc_ref[...] += _dot(a_ref[...], w_ref[...])

        @pl.when(k == nk - 1)
        def _():
            y = acc_ref[...]
            xo_ref[...] = x_ref[...] + gt_ref[0] * y
            y_ref[...] = y.astype(BF16)

    return pl.pallas_call(
        body, name=name, grid=(t // tm, D // tn, nk),
        in_specs=[pl.BlockSpec((tm, tk), lambda i, j, k: (i, k)),
                  pl.BlockSpec((tk, tn), lambda i, j, k: (k, j)),
                  pl.BlockSpec((tm, tn), lambda i, j, k: (i, j)),
                  pl.BlockSpec((1, 1, tn), lambda i, j, k: (i // per, 0, j))],
        out_specs=[pl.BlockSpec((tm, tn), lambda i, j, k: (i, j)),
                   pl.BlockSpec((tm, tn), lambda i, j, k: (i, j))],
        out_shape=[jax.ShapeDtypeStruct((t, D), F32), jax.ShapeDtypeStruct((t, D), BF16)],
        scratch_shapes=[pltpu.VMEM((tm, tn), F32)],
        compiler_params=_cp(("parallel", "parallel", "arbitrary")),
    )(a, w, x, gt)


def _ada_fwd(c_all, w_ada, b_blk):
    nl = w_ada.shape[0]
    nb = c_all.shape[0]

    def body(c_ref, w_ref, b_ref, o_ref):
        c = c_ref[...]
        ca = (c * jax.nn.sigmoid(c)).astype(BF16)
        o_ref[...] = _dot(ca, w_ref[...].astype(BF16)) + b_ref[...]

    return pl.pallas_call(
        body, name="ada_fwd", grid=(nl,),
        in_specs=[pl.BlockSpec((nb, D), lambda l: (0, 0)),
                  pl.BlockSpec((None, D, 768), lambda l: (l, 0, 0)),
                  pl.BlockSpec((None, 1, 768), lambda l: (l, 0, 0))],
        out_specs=pl.BlockSpec((None, nb, 768), lambda l: (l, 0, 0)),
        out_shape=jax.ShapeDtypeStruct((nl, nb, 768), F32),
        compiler_params=_cp(("parallel",)),
    )(c_all, w_ada, b_blk)


def _ada_bwd(c_all, dmod_blk):
    nl = dmod_blk.shape[0]
    nb = c_all.shape[0]

    def body(c_ref, d_ref, o_ref):
        c = c_ref[...]
        ca = (c * jax.nn.sigmoid(c)).astype(BF16)
        o_ref[...] = _dot(ca, d_ref[...].astype(BF16), TN)

    return pl.pallas_call(
        body, name="ada_bwd", grid=(nl,),
        in_specs=[pl.BlockSpec((nb, D), lambda l: (0, 0)),
                  pl.BlockSpec((None, nb, 768), lambda l: (l, 0, 0))],
        out_specs=pl.BlockSpec((None, D, 768), lambda l: (l, 0, 0)),
        out_shape=jax.ShapeDtypeStruct((nl, D, 768), F32),
        compiler_params=_cp(("parallel",)),
    )(c_all, dmod_blk)


def _seq_tile(seq):
    return _row_tile(seq, 512)


def _norm_mod_fwd(x, g, sc, sh):
    nb, seq, _ = x.shape
    ts = _seq_tile(seq)

    def body(x_ref, g_ref, sc_ref, sh_ref, h_ref):
        xv = x_ref[0]
        r = lax.rsqrt(jnp.mean(xv * xv, axis=-1, keepdims=True) + EPS)
        h_ref[0] = ((xv * r) * g_ref[...] * (1.0 + sc_ref[0]) + sh_ref[0]).astype(BF16)

    return pl.pallas_call(
        body, name="norm_mod_fwd", grid=(nb, seq // ts),
        in_specs=[pl.BlockSpec((1, ts, D), lambda b, s: (b, s, 0)),
                  pl.BlockSpec((1, D), lambda b, s: (0, 0)),
                  pl.BlockSpec((1, 1, D), lambda b, s: (b, 0, 0)),
                  pl.BlockSpec((1, 1, D), lambda b, s: (b, 0, 0))],
        out_specs=pl.BlockSpec((1, ts, D), lambda b, s: (b, s, 0)),
        out_shape=jax.ShapeDtypeStruct((nb, seq, D), BF16),
        compiler_params=_cp(("parallel", "parallel")),
    )(x, g, sc, sh)


def _norm_mod_bwd(x, dh, dres, g, sc):
    nb, seq, _ = x.shape
    ts = _seq_tile(seq)

    def body(x_ref, dh_ref, dres_ref, g_ref, sc_ref, dx_ref, dsh_ref, dsc_ref, dg_ref):
        @pl.when(pl.program_id(1) == 0)
        def _():
            dsh_ref[...] = jnp.zeros_like(dsh_ref)
            dsc_ref[...] = jnp.zeros_like(dsc_ref)
            dg_ref[...] = jnp.zeros_like(dg_ref)

        xv = x_ref[0]
        dh = dh_ref[0]
        gv = g_ref[...]
        onesc = 1.0 + sc_ref[0]
        r = lax.rsqrt(jnp.mean(xv * xv, axis=-1, keepdims=True) + EPS)
        xh = xv * r
        dsh_ref[0] += jnp.sum(dh, axis=0, keepdims=True)
        dsc_ref[0] += jnp.sum(dh * (xh * gv), axis=0, keepdims=True)
        dg_ref[0] += jnp.sum(dh * onesc * xh, axis=0, keepdims=True)
        dxh = dh * (gv * onesc)
        dx = r * (dxh - xh * jnp.mean(dxh * xh, axis=-1, keepdims=True))
        dx_ref[0] = dres_ref[0] + dx

    vec = jax.ShapeDtypeStruct((nb, 1, D), F32)
    vspec = pl.BlockSpec((1, 1, D), lambda b, s: (b, 0, 0))
    tile = pl.BlockSpec((1, ts, D), lambda b, s: (b, s, 0))
    return pl.pallas_call(
        body, name="norm_mod_bwd", grid=(nb, seq // ts),
        in_specs=[tile, tile, tile, pl.BlockSpec((1, D), lambda b, s: (0, 0)), vspec],
        out_specs=[tile, vspec, vspec, vspec],
        out_shape=[jax.ShapeDtypeStruct((nb, seq, D), F32), vec, vec, vec],
        compiler_params=_cp(("parallel", "arbitrary")),
    )(x, dh, dres, g, sc)


def _gate_bwd(dx, y, gt):
    nb, seq, _ = dx.shape
    ts = _seq_tile(seq)

    def body(dx_ref, y_ref, gt_ref, dy_ref, dgt_ref):
        @pl.when(pl.program_id(1) == 0)
        def _():
            dgt_ref[...] = jnp.zeros_like(dgt_ref)

        d = dx_ref[0]
        dy_ref[0] = (gt_ref[0] * d).astype(BF16)
        dgt_ref[0] += jnp.sum(d * y_ref[0].astype(F32), axis=0, keepdims=True)

    vspec = pl.BlockSpec((1, 1, D), lambda b, s: (b, 0, 0))
    tile = pl.BlockSpec((1, ts, D), lambda b, s: (b, s, 0))
    return pl.pallas_call(
        body, name="gate_bwd", grid=(nb, seq // ts),
        in_specs=[tile, tile, vspec], out_specs=[tile, vspec],
        out_shape=[jax.ShapeDtypeStruct((nb, seq, D), BF16), jax.ShapeDtypeStruct((nb, 1, D), F32)],
        compiler_params=_cp(("parallel", "arbitrary")),
    )(dx, y, gt)


def _loss_head(x, tgt, g):
    nb, seq, _ = x.shape
    ts = _seq_tile(seq)

    def body(x_ref, t_ref, g_ref, dx_ref, loss_ref, dg_ref):
        @pl.when(pl.program_id(1) == 0)
        def _():
            loss_ref[...] = jnp.zeros_like(loss_ref)
            dg_ref[...] = jnp.zeros_like(dg_ref)

        xv = x_ref[0]
        gv = g_ref[...]
        r = lax.rsqrt(jnp.mean(xv * xv, axis=-1, keepdims=True) + EPS)
        xh = xv * r
        err = xh * gv - t_ref[0]
        per_tok = jnp.mean(err * err, axis=-1, keepdims=True)
        loss_ref[0] += 0.5 * jnp.sum(per_tok, axis=0, keepdims=True)
        dy = err * (1.0 / D)
        dg_ref[0] += jnp.sum(dy * xh, axis=0, keepdims=True)
        dxh = dy * gv
        dx_ref[0] = r * (dxh - xh * jnp.mean(dxh * xh, axis=-1, keepdims=True))

    tile = pl.BlockSpec((1, ts, D), lambda b, s: (b, s, 0))
    return pl.pallas_call(
        body, name="loss_head", grid=(nb, seq // ts),
        in_specs=[tile, tile, pl.BlockSpec((1, D), lambda b, s: (0, 0))],
        out_specs=[tile, pl.BlockSpec((1, 1, 128), lambda b, s: (b, 0, 0)),
                   pl.BlockSpec((1, 1, D), lambda b, s: (b, 0, 0))],
        out_shape=[jax.ShapeDtypeStruct((nb, seq, D), F32), jax.ShapeDtypeStruct((nb, 1, 128), F32),
                   jax.ShapeDtypeStruct((nb, 1, D), F32)],
        compiler_params=_cp(("parallel", "arbitrary")),
    )(x, tgt, g)


_GELU_C = math.sqrt(2.0 / math.pi)


def _gelu(x):
    return 0.5 * x * (1.0 + jnp.tanh(_GELU_C * (x + 0.044715 * (x * x * x))))


def _gelu_and_grad(x):
    t = jnp.tanh(_GELU_C * (x + 0.044715 * (x * x * x)))
    y = 0.5 * x * (1.0 + t)
    dy = 0.5 * (1.0 + t) + 0.5 * x * (1.0 - t * t) * (_GELU_C * (1.0 + 3.0 * 0.044715 * (x * x)))
    return y, dy


def _tril_mask():
    row = lax.broadcasted_iota(jnp.int32, (CH, CH), 0)
    col = lax.broadcasted_iota(jnp.int32, (CH, CH), 1)
    return row >= col


def _gmlp_fwd(proj, ln_g, ln_b, ws, bst):
    t = proj.shape[0]
    tm = _row_tile(t, 512)

    def body(u_ref, v_ref, lg_ref, lb_ref, ws_ref, bst_ref, o_ref):
        tril = _tril_mask()
        wm = [jnp.where(tril, ws_ref[g], 0.0).astype(BF16) for g in range(NG)]
        for ch in range(tm // CH):
            rows = slice(ch * CH, (ch + 1) * CH)
            u = _gelu(u_ref[rows, :].astype(F32))
            v = _gelu(v_ref[rows, :].astype(F32))
            mu = jnp.mean(v, axis=-1, keepdims=True)
            xc = v - mu
            rstd = lax.rsqrt(jnp.mean(xc * xc, axis=-1, keepdims=True) + EPS)
            vn = ((xc * rstd) * lg_ref[...] + lb_ref[...]).astype(BF16)
            for g in range(NG):
                cols = slice(g * CH, (g + 1) * CH)
                s = _dot(wm[g], vn[:, cols]) + bst_ref[:, g:g + 1]
                o_ref[rows, cols] = (u[:, cols] * s).astype(BF16)

    return pl.pallas_call(
        body, name="gmlp_fwd", grid=(t // tm,),
        in_specs=[pl.BlockSpec((tm, BW), lambda i: (i, 0)),
                  pl.BlockSpec((tm, BW), lambda i: (i, 1)),
                  pl.BlockSpec((1, BW), lambda i: (0, 0)),
                  pl.BlockSpec((1, BW), lambda i: (0, 0)),
                  pl.BlockSpec((NG, CH, CH), lambda i: (0, 0, 0)),
                  pl.BlockSpec((CH, NG), lambda i: (0, 0))],
        out_specs=pl.BlockSpec((tm, BW), lambda i: (i, 0)),
        out_shape=jax.ShapeDtypeStruct((t, BW), BF16),
        compiler_params=_cp(("parallel",)),
    )(proj, proj, ln_g, ln_b, ws, bst)


def _gmlp_bwd(proj, dout, ln_g, ln_b, ws, bst):
    t = proj.shape[0]
    tm = _row_tile(t, 512)

    def body(u_ref, v_ref, do_ref, lg_ref, lb_ref, ws_ref, bst_ref, dp_ref, gws_ref, gbs_ref, glg_ref, glb_ref):
        @pl.when(pl.program_id(0) == 0)
        def _():
            gws_ref[...] = jnp.zeros_like(gws_ref)
            gbs_ref[...] = jnp.zeros_like(gbs_ref)
            glg_ref[...] = jnp.zeros_like(glg_ref)
            glb_ref[...] = jnp.zeros_like(glb_ref)

        tril = _tril_mask()
        wm = [jnp.where(tril, ws_ref[g], 0.0).astype(BF16) for g in range(NG)]
        ones = jnp.ones((CH, CH), BF16)
        lg = lg_ref[...]
        for ch in range(tm // CH):
            rows = slice(ch * CH, (ch + 1) * CH)
            u, du_fac = _gelu_and_grad(u_ref[rows, :].astype(F32))
            v, dv_fac = _gelu_and_grad(v_ref[rows, :].astype(F32))
            do = do_ref[rows, :].astype(F32)
            mu = jnp.mean(v, axis=-1, keepdims=True)
            xc = v - mu
            rstd = lax.rsqrt(jnp.mean(xc * xc, axis=-1, keepdims=True) + EPS)
            xh = xc * rstd
            vn = (xh * lg + lb_ref[...]).astype(BF16)
            dvn_parts = []
            for g in range(NG):
                cols = slice(g * CH, (g + 1) * CH)
                s = _dot(wm[g], vn[:, cols]) + bst_ref[:, g:g + 1]
                dp_ref[rows, cols] = (do[:, cols] * s * du_fac[:, cols]).astype(BF16)
                ds = (do[:, cols] * u[:, cols]).astype(BF16)
                gws_ref[g] += jnp.where(tril, _dot(ds, vn[:, cols], NT), 0.0)
                gbs_ref[g] += _dot(ds, ones)
                dvn_parts.append(_dot(wm[g], ds, TN))
            dvn = jnp.concatenate(dvn_parts, axis=1)
            glb_ref[...] += jnp.sum(dvn, axis=0, keepdims=True)
            glg_ref[...] += jnp.sum(dvn * xh, axis=0, keepdims=True)
            dxh = dvn * lg
            dv = rstd * (dxh - jnp.mean(dxh, axis=-1, keepdims=True)
                         - xh * jnp.mean(dxh * xh, axis=-1, keepdims=True))
            dp_ref[rows, BW:2 * BW] = (dv * dv_fac).astype(BF16)

    small = pl.BlockSpec((NG, CH, CH), lambda i: (0, 0, 0))
    vec = pl.BlockSpec((1, BW), lambda i: (0, 0))
    return pl.pallas_call(
        body, name="gmlp_bwd", grid=(t // tm,),
        in_specs=[pl.BlockSpec((tm, BW), lambda i: (i, 0)),
                  pl.BlockSpec((tm, BW), lambda i: (i, 1)),
                  pl.BlockSpec((tm, BW), lambda i: (i, 0)),
                  vec, vec, small, pl.BlockSpec((CH, NG), lambda i: (0, 0))],
        out_specs=[pl.BlockSpec((tm, 2 * BW), lambda i: (i, 0)), small, small, vec, vec],
        out_shape=[jax.ShapeDtypeStruct((t, 2 * BW), BF16),
                   jax.ShapeDtypeStruct((NG, CH, CH), F32), jax.ShapeDtypeStruct((NG, CH, CH), F32),
                   jax.ShapeDtypeStruct((1, BW), F32), jax.ShapeDtypeStruct((1, BW), F32)],
        compiler_params=_cp(("arbitrary",)),
    )(proj, proj, dout, ln_g, ln_b, ws, bst)


def _pool_bands():
    row = lax.broadcasted_iota(jnp.int32, (CH, CH), 0)
    col = lax.broadcasted_iota(jnp.int32, (CH, CH), 1)
    cur, prev = [], []
    for w in POOL_WINDOWS:
        cur.append(jnp.where((row >= col) & (row - col < w), 1.0, 0.0).astype(BF16))
        prev.append(jnp.where(row + CH - col < w, 1.0, 0.0).astype(BF16))
    return cur, prev


def _pool_inv_count(r0, w):
    pos = r0 + lax.broadcasted_iota(jnp.int32, (CH, 1), 0)
    return 1.0 / jnp.minimum(pos + 1, w).astype(F32)


def _pool_diff(x_ref, r0, rp, has_prev, cur, prev, g):
    cols = slice(g * CH, (g + 1) * CH)
    xc = x_ref[pl.ds(r0, CH), cols]
    xp = x_ref[pl.ds(rp, CH), cols]
    ws = _dot(cur[g], xc) + has_prev * _dot(prev[g], xp)
    return ws * _pool_inv_count(r0, POOL_WINDOWS[g]) - xc.astype(F32)


def _pool_fwd(proj3, pw, pscale):
    nb, seq, _ = proj3.shape
    nch = seq // CH

    def body(x_ref, pw_ref, ps_ref, o_ref):
        cur, prev = _pool_bands()
        pwb = [pw_ref[g].astype(BF16) for g in range(NG)]

        def chunk(ch, carry):
            r0 = pl.multiple_of(ch * CH, CH)
            rp = pl.multiple_of(jnp.maximum(ch - 1, 0) * CH, CH)
            has_prev = jnp.where(ch > 0, 1.0, 0.0)
            for g in range(NG):
                cols = slice(g * CH, (g + 1) * CH)
                d = _pool_diff(x_ref, r0, rp, has_prev, cur, prev, g)
                y = _dot(d.astype(BF16), pwb[g]) * ps_ref[:, cols]
                o_ref[pl.ds(r0, CH), cols] = y.astype(BF16)
            return carry

        lax.fori_loop(0, nch, chunk, 0)

    return pl.pallas_call(
        body, name="pool_fwd", grid=(nb,),
        in_specs=[pl.BlockSpec((None, seq, BW), lambda b: (b, 0, 5)),
                  pl.BlockSpec((NG, CH, CH), lambda b: (0, 0, 0)),
                  pl.BlockSpec((1, BW), lambda b: (0, 0))],
        out_specs=pl.BlockSpec((None, seq, BW), lambda b: (b, 0, 0)),
        out_shape=jax.ShapeDtypeStruct((nb, seq, BW), BF16),
        compiler_params=_cp(("parallel",)),
    )(proj3, pw, pscale)


def _pool_bwd(proj3, dout3, pw, pscale):
    nb, seq, _ = proj3.shape
    nch = seq // CH

    def body(x_ref, do_ref, pw_ref, ps_ref, dx_ref, gpw_ref, gps_ref, e_ref):
        @pl.when(pl.program_id(0) == 0)
        def _():
            gpw_ref[...] = jnp.zeros_like(gpw_ref)
            gps_ref[...] = jnp.zeros_like(gps_ref)

        cur, prev = _pool_bands()
        pwb = [pw_ref[g].astype(BF16) for g in range(NG)]

        def first(ch, carry):
            r0 = pl.multiple_of(ch * CH, CH)
            rp = pl.multiple_of(jnp.maximum(ch - 1, 0) * CH, CH)
            has_prev = jnp.where(ch > 0, 1.0, 0.0)
            for g in range(NG):
                cols = slice(g * CH, (g + 1) * CH)
                d = _pool_diff(x_ref, r0, rp, has_prev, cur, prev, g).astype(BF16)
                do = do_ref[pl.ds(r0, CH), cols].astype(F32)
                ypre = _dot(d, pwb[g])
                gps_ref[:, cols] += jnp.sum(do * ypre, axis=0, keepdims=True)
                dyp = (do * ps_ref[:, cols]).astype(BF16)
                gpw_ref[g] += _dot(d, dyp, TN)
                e_ref[pl.ds(r0, CH), cols] = _dot(dyp, pwb[g], NT)
            return carry

        lax.fori_loop(0, nch, first, 0)

        def second(ch, carry):
            r0 = pl.multiple_of(ch * CH, CH)
            rn = pl.multiple_of(jnp.minimum(ch + 1, nch - 1) * CH, CH)
            has_next = jnp.where(ch < nch - 1, 1.0, 0.0)
            for g in range(NG):
                cols = slice(g * CH, (g + 1) * CH)
                w = POOL_WINDOWS[g]
                dd = e_ref[pl.ds(r0, CH), cols]
                ec = (dd * _pool_inv_count(r0, w)).astype(BF16)
                en = (e_ref[pl.ds(rn, CH), cols] * _pool_inv_count(rn, w)).astype(BF16)
                dx = _dot(cur[g], ec, TN) + has_next * _dot(prev[g], en, TN) - dd
                dx_ref[pl.ds(r0, CH), cols] = dx.astype(BF16)
            return carry

        lax.fori_loop(0, nch, second, 0)

    small = pl.BlockSpec((NG, CH, CH), lambda b: (0, 0, 0))
    vec = pl.BlockSpec((1, BW), lambda b: (0, 0))
    return pl.pallas_call(
        body, name="pool_bwd", grid=(nb,),
        in_specs=[pl.BlockSpec((None, seq, BW), lambda b: (b, 0, 5)),
                  pl.BlockSpec((None, seq, BW), lambda b: (b, 0, 0)), small, vec],
        out_specs=[pl.BlockSpec((None, seq, BW), lambda b: (b, 0, 0)), small, vec],
        out_shape=[jax.ShapeDtypeStruct((nb, seq, BW), BF16),
                   jax.ShapeDtypeStruct((NG, CH, CH), F32), jax.ShapeDtypeStruct((1, BW), F32)],
        scratch_shapes=[pltpu.VMEM((seq, BW), F32)],
        compiler_params=_cp(("arbitrary",)),
    )(proj3, dout3, pw, pscale)


SB_BQ = 256
SB_BK = 256
SB_SCALE = HD ** -0.5


def _sb_tile(qs, k, mask):
    z = _dot(qs, k, NT)
    e = jnp.exp(-jnp.abs(z))
    lb = jnp.minimum(z, 0.0) - jnp.log(1.0 + e)
    lom = lb - z
    if mask is not None:
        lom = jnp.where(mask, lom, 0.0)
    return z, e, lb, lom


def _sb_diag_mask(bq, d):
    row = lax.broadcasted_iota(jnp.int32, (bq, SB_BK), 0)
    col = lax.broadcasted_iota(jnp.int32, (bq, SB_BK), 1)
    return col + d * SB_BK < row


def _sb_scaled(q):
    return (q.astype(F32) * SB_SCALE).astype(BF16)


def _dot_hilo(a, m):
    hi = a.astype(BF16)
    lo = (a - hi.astype(F32)).astype(BF16)
    return _dot(hi, m) + _dot(lo, m)


def _sb_fwd(proj3, gather=()):
    nb, seq, _ = proj3.shape
    bq = min(SB_BQ, seq)
    nq = seq // bq
    ndiag = bq // SB_BK

    def body(q_ref, k_ref, v_ref, o_ref, t_ref):
        row = lax.broadcasted_iota(jnp.int32, (SB_BK, SB_BK), 0)
        col = lax.broadcasted_iota(jnp.int32, (SB_BK, SB_BK), 1)
        upper = jnp.where(row > col, 1.0, 0.0).astype(BF16)
        heads = [slice(hh * HD, (hh + 1) * HD) for hh in range(2)]

        def qloop(qi, carry):
            q0 = pl.multiple_of(qi * bq, bq)
            qs = [_sb_scaled(q_ref[pl.ds(q0, bq), lanes]) for lanes in heads]

            def step(k0, c, mask):
                tiles = [_sb_tile(q, k_ref[pl.ds(k0, SB_BK), lanes], mask) for lanes, q in zip(heads, qs)]
                sums = [_dot_hilo(lom, upper) for _, _, _, lom in tiles]
                out = []
                for lanes, (acc, cr), (_, _, lb, lom), cs in zip(heads, c, tiles, sums):
                    a = jnp.exp(lb + (cs + cr))
                    if mask is not None:
                        a = jnp.where(mask, a, 0.0)
                    rsum = cs[:, 0:1] + lom[:, 0:1]
                    out.append((acc + _dot(a.astype(BF16), v_ref[pl.ds(k0, SB_BK), lanes]), cr + rsum))
                return tuple(out)

            c = tuple((jnp.zeros((bq, HD), F32), jnp.zeros((bq, 1), F32)) for _ in heads)
            for d in reversed(range(ndiag)):
                c = step(pl.multiple_of(q0 + d * SB_BK, SB_BK), c, _sb_diag_mask(bq, d))
            npast = qi * ndiag
            c = lax.fori_loop(
                0, npast, lambda i, c: step(pl.multiple_of((npast - 1 - i) * SB_BK, SB_BK), c, None), c)
            for lanes, (acc, cr) in zip(heads, c):
                o_ref[pl.ds(q0, bq), lanes] = acc.astype(BF16)
                t_ref[pl.ds(q0, bq), lanes] = jnp.broadcast_to(cr, (bq, HD))
            return carry

        lax.fori_loop(0, nq, qloop, 0)

    def spec(c0):
        return pl.BlockSpec((None, seq, 128), lambda b, p: (b, 0, c0 + p))

    grid = (nb, BW // 128)
    body, ex_in, ex_out, ex_shape, ex_sems = _host_exchange(body, 3, 2, grid, gather, False)
    outs = pl.pallas_call(
        body, name="sb_fwd", grid=grid,
        in_specs=[spec(8), spec(12), spec(16)] + ex_in,
        out_specs=[spec(0), spec(0)] + ex_out,
        out_shape=[jax.ShapeDtypeStruct((nb, seq, BW), BF16), jax.ShapeDtypeStruct((nb, seq, BW), F32)] + ex_shape,
        scratch_shapes=ex_sems,
        compiler_params=_cp(("arbitrary", "arbitrary")),
    )(proj3, proj3, proj3, *gather)
    return outs[0], outs[1], outs[2:]


def _sb_bwd(proj3, do3, tot3, scatter=()):
    nb, seq, _ = proj3.shape
    bq = min(SB_BQ, seq)
    nq = seq // bq
    ndiag = bq // SB_BK

    def body(q_ref, k_ref, v_ref, do_ref, t_ref, dq_ref, dk_ref, dv_ref, dk_acc, dv_acc):
        row = lax.broadcasted_iota(jnp.int32, (SB_BK, SB_BK), 0)
        col = lax.broadcasted_iota(jnp.int32, (SB_BK, SB_BK), 1)
        upper = jnp.where(row > col, 1.0, 0.0).astype(BF16)
        lower = jnp.where(row < col, 1.0, 0.0).astype(BF16)
        dk_acc[...] = jnp.zeros_like(dk_acc)
        dv_acc[...] = jnp.zeros_like(dv_acc)
        heads = [slice(hh * HD, (hh + 1) * HD) for hh in range(2)]

        def qloop(qi, carry):
            q0 = pl.multiple_of(qi * bq, bq)
            qs = [_sb_scaled(q_ref[pl.ds(q0, bq), lanes]) for lanes in heads]
            dos = [do_ref[pl.ds(q0, bq), lanes] for lanes in heads]
            tots = [t_ref[pl.ds(q0, bq), hh * HD:hh * HD + 1] for hh in range(2)]

            def step(k0, c, mask):
                ks = [k_ref[pl.ds(k0, SB_BK), lanes] for lanes in heads]
                tiles = [_sb_tile(q, k, mask) for q, k in zip(qs, ks)]
                sums = [_dot_hilo(lom, upper) for _, _, _, lom in tiles]
                das = [_dot(do, v_ref[pl.ds(k0, SB_BK), lanes], NT) for do, lanes in zip(dos, heads)]
                gls, rsums, avs = [], [], []
                for hh, (_, cpre, _) in enumerate(c):
                    _, _, lb, lom = tiles[hh]
                    rsum = sums[hh][:, 0:1] + lom[:, 0:1]
                    a = jnp.exp(lb + (sums[hh] + (tots[hh] - cpre - rsum)))
                    if mask is not None:
                        a = jnp.where(mask, a, 0.0)
                    gls.append(das[hh] * a)
                    rsums.append(rsum)
                    avs.append(a.astype(BF16))
                pres = [_dot_hilo(gl, lower) for gl in gls]
                out = []
                for hh, (dq, cpre, gpre) in enumerate(c):
                    z, e, _, _ = tiles[hh]
                    inv = 1.0 / (1.0 + e)
                    pos = z >= 0.0
                    beta = jnp.where(pos, 1.0, e) * inv
                    omb = jnp.where(pos, e, 1.0) * inv
                    dz = gls[hh] * omb - beta * (pres[hh] + gpre)
                    if mask is not None:
                        dz = jnp.where(mask, dz, 0.0)
                    dz = dz.astype(BF16)
                    dk_acc[hh, pl.ds(k0, SB_BK), :] += _dot(dz, qs[hh], TN)
                    dv_acc[hh, pl.ds(k0, SB_BK), :] += _dot(avs[hh], dos[hh], TN)
                    gsum = pres[hh][:, SB_BK - 1:SB_BK] + gls[hh][:, SB_BK - 1:SB_BK]
                    out.append((dq + _dot(dz, ks[hh]), cpre + rsums[hh], gpre + gsum))
                return tuple(out)

            c = tuple((jnp.zeros((bq, HD), F32), jnp.zeros((bq, 1), F32), jnp.zeros((bq, 1), F32))
                      for _ in heads)
            npast = qi * ndiag
            c = lax.fori_loop(0, npast, lambda kb, c: step(pl.multiple_of(kb * SB_BK, SB_BK), c, None), c)
            for d in range(ndiag):
                c = step(pl.multiple_of(q0 + d * SB_BK, SB_BK), c, _sb_diag_mask(bq, d))
            for lanes, (dq, _, _) in zip(heads, c):
                dq_ref[pl.ds(q0, bq), lanes] = (dq * SB_SCALE).astype(BF16)
            return carry

        lax.fori_loop(0, nq, qloop, 0)
        for hh in range(2):
            lanes = slice(hh * HD, (hh + 1) * HD)
            dk_ref[:, lanes] = dk_acc[hh].astype(BF16)
            dv_ref[:, lanes] = dv_acc[hh].astype(BF16)

    def spec(c0):
        return pl.BlockSpec((None, seq, 128), lambda b, p: (b, 0, c0 + p))

    grid = (nb, BW // 128)
    body, ex_in, ex_out, ex_shape, ex_sems = _host_exchange(body, 5, 3, grid, scatter, True)
    outs = pl.pallas_call(
        body, name="sb_bwd", grid=grid,
        in_specs=[spec(8), spec(12), spec(16), spec(0), spec(0)] + ex_in,
        out_specs=[spec(0), spec(0), spec(0)] + ex_out,
        out_shape=[jax.ShapeDtypeStruct((nb, seq, BW), BF16)] * 3 + ex_shape,
        scratch_shapes=[pltpu.VMEM((2, seq, HD), F32), pltpu.VMEM((2, seq, HD), F32)] + ex_sems,
        compiler_params=_cp(("arbitrary", "arbitrary")),
    )(proj3, proj3, proj3, do3, tot3, *scatter)
    return outs[:3], outs[3:]


def _merge_fwd(brs, wb, proj):
    t = proj.shape[0]
    tm = _row_tile(t, 512)
    tn = 512
    nj = D // tn

    def body(b0, b1, b2, wb_ref, l0, l1, l2, m_ref, y0, y1, y2):
        acc = None
        for br, n, lg, y_ref in ((b0, 0, l0, y0), (b1, 1, l1, y1), (b2, 2, l2, y2)):
            y = _dot(br[...], wb_ref[n])
            y_ref[...] = y.astype(BF16)
            term = jax.nn.sigmoid(lg[...].astype(F32)) * y
            acc = term if acc is None else acc + term
        m_ref[...] = acc.astype(BF16)

    def lspec(n):
        return pl.BlockSpec((tm, tn), lambda i, j: (i, (3 * D + n * D) // tn + j))

    tile = pl.BlockSpec((tm, tn), lambda i, j: (i, j))
    bspec = pl.BlockSpec((tm, BW), lambda i, j: (i, 0))
    return pl.pallas_call(
        body, name="merge_fwd", grid=(t // tm, nj),
        in_specs=[bspec, bspec, bspec, pl.BlockSpec((NB, BW, tn), lambda i, j: (0, 0, j)),
                  lspec(0), lspec(1), lspec(2)],
        out_specs=[tile] * 4,
        out_shape=[jax.ShapeDtypeStruct((t, D), BF16)] * 4,
        compiler_params=_cp(("parallel", "parallel")),
    )(brs[0], brs[1], brs[2], wb, proj, proj, proj)


def _merge_bwd(dm, ys, proj):
    t = proj.shape[0]
    tm = _row_tile(t, 512)
    tn = 512

    def body(dm_ref, y0, y1, y2, l0, l1, l2, dl0, dl1, dl2, dy0, dy1, dy2):
        dmv = dm_ref[...].astype(F32)
        for y_ref, lg, dl_ref, dy_ref in ((y0, l0, dl0, dy0), (y1, l1, dl1, dy1), (y2, l2, dl2, dy2)):
            g = jax.nn.sigmoid(lg[...].astype(F32))
            dl_ref[...] = (dmv * y_ref[...].astype(F32) * g * (1.0 - g)).astype(BF16)
            dy_ref[...] = (dmv * g).astype(BF16)

    def lspec(n):
        return pl.BlockSpec((tm, tn), lambda i, j: (i, (3 * D + n * D) // tn + j))

    tile = pl.BlockSpec((tm, tn), lambda i, j: (i, j))
    return pl.pallas_call(
        body, name="merge_bwd", grid=(t // tm, D // tn),
        in_specs=[tile] * 4 + [lspec(0), lspec(1), lspec(2)],
        out_specs=[tile] * 6,
        out_shape=[jax.ShapeDtypeStruct((t, D), BF16)] * 6,
        compiler_params=_cp(("parallel", "parallel")),
    )(dm, ys[0], ys[1], ys[2], proj, proj, proj)


def _swiglu_fwd(f):
    t = f.shape[0]
    tm = _row_tile(t, 512)
    tn = 1024

    def body(g_ref, u_ref, a_ref):
        g = g_ref[...].astype(F32)
        a_ref[...] = (g * jax.nn.sigmoid(g) * u_ref[...].astype(F32)).astype(BF16)

    return pl.pallas_call(
        body, name="swiglu_fwd", grid=(t // tm, FFP // tn),
        in_specs=[pl.BlockSpec((tm, tn), lambda i, j: (i, j)),
                  pl.BlockSpec((tm, tn), lambda i, j: (i, FFP // tn + j))],
        out_specs=pl.BlockSpec((tm, tn), lambda i, j: (i, j)),
        out_shape=jax.ShapeDtypeStruct((t, FFP), BF16),
        compiler_params=_cp(("parallel", "parallel")),
    )(f, f)


def _swiglu_bwd(f, da):
    t = f.shape[0]
    tm = _row_tile(t, 512)
    tn = 1024

    def body(g_ref, u_ref, da_ref, dg_ref, du_ref):
        g = g_ref[...].astype(F32)
        u = u_ref[...].astype(F32)
        d = da_ref[...].astype(F32)
        s = jax.nn.sigmoid(g)
        dg_ref[...] = (d * u * (s * (1.0 + g * (1.0 - s)))).astype(BF16)
        du_ref[...] = (d * (g * s)).astype(BF16)

    return pl.pallas_call(
        body, name="swiglu_bwd", grid=(t // tm, FFP // tn),
        in_specs=[pl.BlockSpec((tm, tn), lambda i, j: (i, j)),
                  pl.BlockSpec((tm, tn), lambda i, j: (i, FFP // tn + j)),
                  pl.BlockSpec((tm, tn), lambda i, j: (i, j))],
        out_specs=[pl.BlockSpec((tm, tn), lambda i, j: (i, j)),
                   pl.BlockSpec((tm, tn), lambda i, j: (i, j))],
        out_shape=[jax.ShapeDtypeStruct((t, FFP), BF16)] * 2,
        compiler_params=_cp(("parallel", "parallel")),
    )(f, f, da)


def _adamw_math(npart, p_ref, w_ref, m_ref, v_ref, g_ref, d_ref, mo_ref, vo_ref):
    c1 = 1.0 - ADAM_B1 ** ADAM_STEP
    c2 = 1.0 - ADAM_B2 ** ADAM_STEP
    g = p_ref[0].astype(F32)
    for p in range(1, npart):
        g = g + p_ref[p].astype(F32)
    mn = ADAM_B1 * m_ref[...] + (1.0 - ADAM_B1) * g
    vn = ADAM_B2 * v_ref[...] + (1.0 - ADAM_B2) * (g * g)
    m_hat = mn / c1
    v_hat = vn / c2
    g_ref[...] = g
    d_ref[...] = -ADAM_LR * (m_hat / (jnp.sqrt(v_hat) + ADAM_EPS) + ADAM_WD * w_ref[...])
    mo_ref[...] = mn
    vo_ref[...] = vn


def _adamw_layer(name, parts, w, m, v, layer, bufs):
    nl, cols = w.shape[0], w.shape[-1]
    rows = int(math.prod(w.shape[1:-1]))
    npart = parts.shape[0]
    tr = rows if rows <= 512 else 512
    assert rows % tr == 0
    if bufs is None:
        bufs = [lax.empty((nl, rows, cols), F32) for _ in range(4)]

    def body(p_ref, w_ref, m_ref, v_ref, b0, b1, b2, b3, g_ref, d_ref, mo_ref, vo_ref):
        _adamw_math(npart, p_ref, w_ref, m_ref, v_ref, g_ref, d_ref, mo_ref, vo_ref)

    slab = pl.BlockSpec((None, tr, cols), lambda i: (layer, i, 0))
    sds = jax.ShapeDtypeStruct((nl, rows, cols), F32)
    return pl.pallas_call(
        body, name=name, grid=(rows // tr,),
        in_specs=[pl.BlockSpec((npart, tr, cols), lambda i: (0, i, 0)), slab, slab, slab] + [_HBM] * 4,
        out_specs=[slab] * 4, out_shape=[sds] * 4,
        input_output_aliases={4: 0, 5: 1, 6: 2, 7: 3},
        compiler_params=_cp(("parallel",)),
    )(parts.reshape(npart, rows, cols), w.reshape(nl, rows, cols), m.reshape(nl, rows, cols),
      v.reshape(nl, rows, cols), *bufs)


def _adamw_reduce(name, parts, w, m, v):
    shape = w.shape
    cols = shape[-1]
    rows = int(math.prod(shape[:-1])) if len(shape) > 1 else 1
    npart = parts.shape[0]
    tr = rows if rows <= 512 else 512
    assert rows % tr == 0

    def body(p_ref, w_ref, m_ref, v_ref, g_ref, d_ref, mo_ref, vo_ref):
        _adamw_math(npart, p_ref, w_ref, m_ref, v_ref, g_ref, d_ref, mo_ref, vo_ref)

    tile = pl.BlockSpec((tr, cols), lambda i: (i, 0))
    sds = jax.ShapeDtypeStruct((rows, cols), F32)
    outs = pl.pallas_call(
        body, name=name, grid=(rows // tr,),
        in_specs=[pl.BlockSpec((npart, tr, cols), lambda i: (0, i, 0)), tile, tile, tile],
        out_specs=[tile] * 4, out_shape=[sds] * 4,
        compiler_params=_cp(("parallel",)),
    )(parts.reshape(npart, rows, cols), w.reshape(rows, cols), m.reshape(rows, cols), v.reshape(rows, cols))
    return tuple(o.reshape(shape) for o in outs)


def _pad_ffn_in(w):
    lead = w.shape[:-1]
    w = w.reshape(lead + (2, FF_HALF))
    w = jnp.pad(w, [(0, 0)] * len(lead) + [(0, 0), (0, FF_HALF_PAD - FF_HALF)])
    return w.reshape(lead + (FF_IN_PAD,))


def _unpad_ffn_in(w):
    lead = w.shape[:-1]
    return w.reshape(lead + (2, FF_HALF_PAD))[..., :FF_HALF].reshape(lead + (FF_IN_SHARD,))


def kernel(x, c, rms_g1, rms_g2, w_ada, b_ada, w_in, gm_ln_g, gm_ln_b, gm_w_spatial, gm_b_spatial, pool_w, pool_scale, w_branch, w_out, w_ffn_in, w_ffn_out, final_g, loss_target, m_rms_g1, m_rms_g2, m_w_ada, m_b_ada, m_w_in, m_gm_ln_g, m_gm_ln_b, m_gm_w_spatial, m_gm_b_spatial, m_pool_w, m_pool_scale, m_w_branch, m_w_out, m_w_ffn_in, m_w_ffn_out, m_final_g, v_rms_g1, v_rms_g2, v_w_ada, v_b_ada, v_w_in, v_gm_ln_g, v_gm_ln_b, v_gm_w_spatial, v_gm_b_spatial, v_pool_w, v_pool_scale, v_w_branch, v_w_out, v_w_ffn_in, v_w_ffn_out, v_final_g):
    nb, seq, _ = x.shape
    nl = w_in.shape[0]
    t = nb * seq
    ntot = NDEV * nb
    me = _my_index()
    assert x.shape[2] == D and w_in.shape[1:] == (D, 768) and w_ffn_in.shape[1:] == (D, FF_IN_SHARD)
    assert seq % CH == 0

    w_ffn_in_p = _pad_ffn_in(w_ffn_in).astype(BF16)
    w_ffn_out_p = jnp.pad(w_ffn_out, ((0, 0), (0, FF_HALF_PAD - FF_HALF), (0, 0))).astype(BF16)
    w_in_b = w_in.astype(BF16)
    w_branch_b = w_branch.astype(BF16)
    w_out_b = w_out.astype(BF16)
    (g_in_next,) = _exchange([w_in_b[0]], "gather_w_in0", False)

    (c_all,) = _exchange([c], "gather_c", False)
    c_all = c_all.reshape(ntot, D)
    b_blk = lax.dynamic_slice_in_dim(b_ada, me * 768, 768, axis=1).reshape(nl, 1, 768)
    mod_blk = _ada_fwd(c_all, w_ada, b_blk)
    (mod_all,) = _exchange([mod_blk], "gather_mod", False)
    mod_all = jnp.transpose(mod_all, (1, 2, 0, 3)).reshape(nl, ntot, NMOD * D)
    mod = lax.dynamic_slice_in_dim(mod_all, me * nb, nb, axis=1).reshape(nl, nb, NMOD, 1, D)

    saved = []
    gathered = []
    xc = x
    for l in range(nl):
        sh1, sc1, gt1, sh2, sc2, gt2 = [mod[l, :, i] for i in range(NMOD)]
        h = _norm_mod_fwd(xc, rms_g1[l].reshape(1, D), sc1, sh1).reshape(t, D)
        proj = _mm_colblocked("proj_fwd", h, g_in_next, BF16)
        proj3 = proj.reshape(nb, seq, IN_COLS)
        br_gm = _gmlp_fwd(proj, gm_ln_g[l].reshape(1, BW), gm_ln_b[l].reshape(1, BW),
                          gm_w_spatial[l], gm_b_spatial[l].T)
        carried = [w_branch_b[l], w_out_b[l], w_ffn_in_p[l], w_ffn_out_p[l]]
        if l + 1 < nl:
            carried.append(w_in_b[l + 1])
        br_sb, sb_tot, got = _sb_fwd(proj3, carried)
        gw = dict(w_in=g_in_next,
                  w_branch=jnp.transpose(got[0], (1, 2, 0, 3)).reshape(NB, BW, D),
                  w_out=got[1].reshape(D, D),
                  w_ffn_in=got[2],
                  w_ffn_out=got[3].reshape(FFP, D))
        gathered.append(gw)
        if l + 1 < nl:
            g_in_next = got[4]
        br_pool = _pool_fwd(proj3, pool_w[l], pool_scale[l].reshape(1, BW))
        brs = [br_gm, br_sb.reshape(t, BW), br_pool.reshape(t, BW)]
        merged, y0, y1, y2 = _merge_fwd(brs, gw["w_branch"], proj)
        x_mid, mo = _mm_residual("out_fwd", merged, gw["w_out"], xc.reshape(t, D), gt1, seq)
        x_mid = x_mid.reshape(nb, seq, D)
        h2 = _norm_mod_fwd(x_mid, rms_g2[l].reshape(1, D), sc2, sh2).reshape(t, D)
        f = _mm_colblocked("ffn_in_fwd", h2, gw["w_ffn_in"], BF16)
        act = _swiglu_fwd(f)
        x_out, fo = _mm_residual("ffn_out_fwd", act, gw["w_ffn_out"], x_mid.reshape(t, D), gt2, seq)
        saved.append(dict(x_in=xc, h=h, proj=proj, brs=brs, sb_tot=sb_tot, ys=(y0, y1, y2), merged=merged,
                          mo=mo, x_mid=x_mid, h2=h2, f=f, act=act, fo=fo))
        xc = x_out.reshape(nb, seq, D)

    dx, loss_part, dfinal_part = _loss_head(xc, loss_target, final_g.reshape(1, D))
    loss = lax.psum(jnp.sum(loss_part[:, 0, 0]), ("x", "y", "c"))

    big_names = ("w_in", "w_branch", "w_out", "w_ffn_in", "w_ffn_out")
    bufs = {name: None for name in big_names}
    g_in_pending = None
    small_parts = {k: [None] * nl for k in ("rms_g1", "rms_g2", "gm_ln_g", "gm_ln_b", "gm_w_spatial",
                                            "gm_b_spatial", "pool_w", "pool_scale")}
    dmod = [None] * nl
    for l in reversed(range(nl)):
        gw = gathered[l]
        sv = saved[l]
        sh1, sc1, gt1, sh2, sc2, gt2 = [mod[l, :, i] for i in range(NMOD)]
        dfo, dgt2 = _gate_bwd(dx, sv["fo"].reshape(nb, seq, D), gt2)
        dfo = dfo.reshape(t, D)
        dact = _mm_nt("ffn_out_dgrad", dfo, gw["w_ffn_out"], BF16)
        g_ffn_out = _mm_tn("ffn_out_wgrad", sv["act"], dfo)
        dfg, dfu = _swiglu_bwd(sv["f"], dact)
        df = jnp.concatenate([dfg, dfu], axis=1)
        dh2 = _mm_colblocked_nt("ffn_in_dgrad", df, gw["w_ffn_in"], F32)
        g_ffn_in = _mm_colblocked_tn("ffn_in_wgrad", sv["h2"], df)
        dx_mid, dsh2, dsc2, dg2 = _norm_mod_bwd(sv["x_mid"], dh2.reshape(nb, seq, D), dx,
                                                rms_g2[l].reshape(1, D), sc2)
        dmo, dgt1 = _gate_bwd(dx_mid, sv["mo"].reshape(nb, seq, D), gt1)
        dmo = dmo.reshape(t, D)
        dmerged = _mm_nt("out_dgrad", dmo, gw["w_out"], BF16)
        g_out = _mm_tn("out_wgrad", sv["merged"], dmo)
        dls_dys = _merge_bwd(dmerged, sv["ys"], sv["proj"])
        dls, dys = dls_dys[:3], dls_dys[3:]
        dbrs, g_br = [], []
        for n in range(NB):
            dbrs.append(_mm_nt("branch_dgrad", dys[n], gw["w_branch"], BF16, w_lead=n))
            g_br.append(_mm_tn("branch_wgrad", sv["brs"][n], dys[n]))
        proj3 = sv["proj"].reshape(nb, seq, IN_COLS)
        d_gm, g_ws, g_bs, g_lg, g_lb = _gmlp_bwd(sv["proj"], dbrs[0], gm_ln_g[l].reshape(1, BW),
                                                 gm_ln_b[l].reshape(1, BW), gm_w_spatial[l], gm_b_spatial[l].T)
        g_br_dev = jnp.transpose(jnp.stack(g_br).reshape(NB, BW, NDEV, D // NDEV), (2, 0, 1, 3))
        carried = [g_br_dev, g_out.reshape(NDEV, D // NDEV, D), g_ffn_in, g_ffn_out.reshape(NDEV, FF_HALF_PAD, D)]
        if g_in_pending is not None:
            carried.append(g_in_pending)
        d_sb, recv = _sb_bwd(proj3, dbrs[1].reshape(nb, seq, BW), sv["sb_tot"], carried)
        bufs["w_branch"] = _adamw_layer("adamw_w_branch", recv[0], w_branch, m_w_branch, v_w_branch, l,
                                        bufs["w_branch"])
        bufs["w_out"] = _adamw_layer("adamw_w_out", recv[1], w_out, m_w_out, v_w_out, l, bufs["w_out"])
        bufs["w_ffn_in"] = _adamw_layer("adamw_w_ffn_in", _unpad_ffn_in(recv[2]), w_ffn_in, m_w_ffn_in,
                                        v_w_ffn_in, l, bufs["w_ffn_in"])
        bufs["w_ffn_out"] = _adamw_layer("adamw_w_ffn_out", recv[3][:, :FF_HALF], w_ffn_out, m_w_ffn_out,
                                         v_w_ffn_out, l, bufs["w_ffn_out"])
        if g_in_pending is not None:
            bufs["w_in"] = _adamw_layer("adamw_w_in", recv[4], w_in, m_w_in, v_w_in, l + 1, bufs["w_in"])
        d_pool, g_pw, g_ps = _pool_bwd(proj3, dbrs[2].reshape(nb, seq, BW), pool_w[l], pool_scale[l].reshape(1, BW))
        dproj = jnp.concatenate([d_gm] + [a.reshape(t, BW) for a in d_sb] + [d_pool.reshape(t, BW)] + list(dls),
                                axis=1)
        dh = _mm_colblocked_nt("proj_dgrad", dproj, gw["w_in"], F32)
        g_in_pending = _mm_colblocked_tn("proj_wgrad", sv["h"], dproj)
        dx, dsh1, dsc1, dg1 = _norm_mod_bwd(sv["x_in"], dh.reshape(nb, seq, D), dx_mid,
                                            rms_g1[l].reshape(1, D), sc1)

        dmod[l] = jnp.concatenate([dsh1, dsc1, dgt1, dsh2, dsc2, dgt2], axis=-1)
        small_parts["rms_g1"][l] = jnp.sum(dg1, axis=0)
        small_parts["rms_g2"][l] = jnp.sum(dg2, axis=0)
        small_parts["gm_ln_g"][l] = g_lg
        small_parts["gm_ln_b"][l] = g_lb
        small_parts["gm_w_spatial"][l] = g_ws
        small_parts["gm_b_spatial"][l] = g_bs[:, :, 0]
        small_parts["pool_w"][l] = g_pw
        small_parts["pool_scale"][l] = g_ps

    (r_in,) = _exchange([g_in_pending], "scatter_w_in0", True)
    bufs["w_in"] = _adamw_layer("adamw_w_in", r_in, w_in, m_w_in, v_w_in, 0, bufs["w_in"])

    dmod_mine = jnp.stack(dmod).reshape(nl, nb, NMOD * D)
    names = list(small_parts)
    stacked = [jnp.stack(small_parts[k]) for k in names]
    gathered_small = _exchange(stacked + [dfinal_part, dmod_mine], "gather_small", False)
    dmod_all = jnp.transpose(gathered_small[-1], (1, 0, 2, 3)).reshape(nl, ntot, NMOD * D)
    dfinal_all = gathered_small[-2].reshape(ntot, D)

    results = {}
    weights = dict(rms_g1=(rms_g1, m_rms_g1, v_rms_g1), rms_g2=(rms_g2, m_rms_g2, v_rms_g2),
                   gm_ln_g=(gm_ln_g, m_gm_ln_g, v_gm_ln_g), gm_ln_b=(gm_ln_b, m_gm_ln_b, v_gm_ln_b),
                   gm_w_spatial=(gm_w_spatial, m_gm_w_spatial, v_gm_w_spatial),
                   gm_b_spatial=(gm_b_spatial, m_gm_b_spatial, v_gm_b_spatial),
                   pool_w=(pool_w, m_pool_w, v_pool_w), pool_scale=(pool_scale, m_pool_scale, v_pool_scale))
    for k, parts in zip(names, gathered_small[:len(names)]):
        w, m, v = weights[k]
        results[k] = _adamw_reduce("adamw_" + k, parts.reshape((NDEV,) + w.shape), w, m, v)
    results["final_g"] = _adamw_reduce("adamw_final_g", dfinal_all, final_g, m_final_g, v_final_g)
    results["b_ada"] = _adamw_reduce("adamw_b_ada", jnp.transpose(dmod_all, (1, 0, 2)), b_ada, m_b_ada, v_b_ada)
    dmod_blk = lax.dynamic_slice_in_dim(dmod_all, me * 768, 768, axis=2)
    g_w_ada = _ada_bwd(c_all, dmod_blk)
    results["w_ada"] = _adamw_reduce("adamw_w_ada", g_w_ada[None], w_ada, m_w_ada, v_w_ada)
    stacked_w = dict(w_in=w_in, w_branch=w_branch, w_out=w_out, w_ffn_in=w_ffn_in, w_ffn_out=w_ffn_out)
    for name in big_names:
        results[name] = tuple(b.reshape(stacked_w[name].shape) for b in bufs[name])

    order = ["rms_g1", "rms_g2", "w_ada", "b_ada", "w_in", "gm_ln_g", "gm_ln_b", "gm_w_spatial", "gm_b_spatial",
             "pool_w", "pool_scale", "w_branch", "w_out", "w_ffn_in", "w_ffn_out", "final_g"]
    out = [loss, dx]
    for i in range(4):
        out.extend(results[k][i] for k in order)
    return tuple(out)
```

```python
import functools
import math

import jax
import jax.numpy as jnp
from jax import lax
from jax.experimental import pallas as pl
from jax.experimental.pallas import tpu as pltpu

F32 = jnp.float32
BF16 = jnp.bfloat16
MESH = pl.DeviceIdType.MESH

D = 1024
BW = 512
NB = 3
CH = 128
NG = 4
HD = 64
POOL_WINDOWS = (2, 4, 8, 16)
DFF = 2816
NMOD = 6
EPS = 1e-6
IN_COLS = 6 * D
NDEV = 8
FF_IN_SHARD = 2 * DFF // NDEV
FF_HALF = FF_IN_SHARD // 2
FF_HALF_PAD = 384
FF_IN_PAD = 2 * FF_HALF_PAD
FFP = NDEV // 2 * FF_IN_PAD

ADAM_LR = 0.001
ADAM_B1 = 0.9
ADAM_B2 = 0.999
ADAM_EPS = 1e-08
ADAM_WD = 0.01
ADAM_STEP = 10

VMEM_LIMIT = 48 * 1024 * 1024

NN = (((1,), (0,)), ((), ()))
NT = (((1,), (1,)), ((), ()))
TN = (((0,), (0,)), ((), ()))


def _cp(sem=None):
    return pltpu.CompilerParams(dimension_semantics=sem, vmem_limit_bytes=VMEM_LIMIT)


def _dot(a, b, dims=NN):
    return lax.dot_general(a, b, dims, preferred_element_type=F32)


def _my_index():
    return 4 * lax.axis_index("x") + 2 * lax.axis_index("y") + lax.axis_index("c")


def _peer(k):
    x, y, c = lax.axis_index("x"), lax.axis_index("y"), lax.axis_index("c")
    px = 1 - x if k & 4 else x
    py = 1 - y if k & 2 else y
    pc = 1 - c if k & 1 else c
    return (px, py, pc), 4 * px + 2 * py + pc


def _exchange(xs, name, all_to_all):
    n = len(xs)

    def body(*refs):
        _exchange_start(refs[:n], refs[n:2 * n], refs[2 * n:], all_to_all)
        _exchange_finish(refs[:n], refs[n:2 * n], refs[2 * n:], all_to_all)

    return pl.pallas_call(
        body, name=name, out_shape=_exchange_out_shape(xs, all_to_all),
        in_specs=[_HBM] * n, out_specs=[_HBM] * n, scratch_shapes=_exchange_sems(n),
    )(*xs)


_HBM = pl.BlockSpec(memory_space=pl.ANY)


def _exchange_out_shape(xs, all_to_all):
    if all_to_all:
        return [jax.ShapeDtypeStruct(x.shape, x.dtype) for x in xs]
    return [jax.ShapeDtypeStruct((NDEV,) + x.shape, x.dtype) for x in xs]


def _exchange_sems(n):
    return [pltpu.SemaphoreType.DMA((n * 7,)), pltpu.SemaphoreType.DMA((n * 7,)), pltpu.SemaphoreType.DMA((n,))]


def _exchange_copies(ins, outs, sems, all_to_all):
    send_sems, recv_sems, local_sems = sems
    me = _my_index()
    local, sends, recvs = [], [], []
    for a in range(len(ins)):
        src = ins[a].at[me] if all_to_all else ins[a]
        local.append(pltpu.make_async_copy(src, outs[a].at[me], local_sems.at[a]))
    for k in range(1, NDEV):
        dev, idx = _peer(k)
        for a in range(len(ins)):
            src = ins[a].at[idx] if all_to_all else ins[a]
            sem = dict(send_sem=send_sems.at[a * 7 + k - 1], recv_sem=recv_sems.at[a * 7 + k - 1],
                       device_id=dev, device_id_type=MESH)
            sends.append(pltpu.make_async_remote_copy(src_ref=src, dst_ref=outs[a].at[me], **sem))
            recvs.append(pltpu.make_async_remote_copy(src_ref=src, dst_ref=outs[a].at[idx], **sem))
    return local, sends, recvs


def _exchange_start(ins, outs, sems, all_to_all):
    local, sends, _ = _exchange_copies(ins, outs, sems, all_to_all)
    for cp in local + sends:
        cp.start()


def _exchange_finish(ins, outs, sems, all_to_all):
    local, sends, recvs = _exchange_copies(ins, outs, sems, all_to_all)
    for cp in sends:
        cp.wait_send()
    for cp in recvs:
        cp.wait_recv()
    for cp in local:
        cp.wait()


def _host_exchange(body, n_in, n_out, grid, xs, all_to_all):
    n = len(xs)
    if n == 0:
        return body, [], [], [], []

    def hosted(*refs):
        ins, ex_ins = refs[:n_in], refs[n_in:n_in + n]
        outs, ex_outs = refs[n_in + n:n_in + n + n_out], refs[n_in + n + n_out:n_in + 2 * n + n_out]
        scratch = refs[n_in + 2 * n + n_out:]
        own, sems = scratch[:len(scratch) - 3], scratch[len(scratch) - 3:]
        first = functools.reduce(jnp.logical_and, [pl.program_id(a) == 0 for a in range(len(grid))])
        last = functools.reduce(jnp.logical_and, [pl.program_id(a) == grid[a] - 1 for a in range(len(grid))])

        @pl.when(first)
        def _():
            _exchange_start(ex_ins, ex_outs, sems, all_to_all)

        body(*ins, *outs, *own)

        @pl.when(last)
        def _():
            _exchange_finish(ex_ins, ex_outs, sems, all_to_all)

    return hosted, [_HBM] * n, [_HBM] * n, _exchange_out_shape(xs, all_to_all), _exchange_sems(n)


def _mm(name, a, b, grid, a_spec, b_spec, o_spec, out_sds, dims, acc_shape, scatter=()):
    nk = grid[2]

    def body(a_ref, b_ref, o_ref, acc_ref):
        k = pl.program_id(2)

        @pl.when(k == 0)
        def _():
            acc_ref[...] = jnp.zeros_like(acc_ref)

        acc_ref[...] += _dot(a_ref[...].astype(BF16), b_ref[...].astype(BF16), dims)

        @pl.when(k == nk - 1)
        def _():
            o_ref[...] = acc_ref[...].astype(o_ref.dtype)

    body, ex_in, ex_out, ex_shape, ex_sems = _host_exchange(body, 2, 1, grid, scatter, True)
    outs = pl.pallas_call(
        body, name=name, grid=grid, in_specs=[a_spec, b_spec] + ex_in, out_specs=[o_spec] + ex_out,
        out_shape=[out_sds] + ex_shape,
        scratch_shapes=[pltpu.VMEM(acc_shape, F32)] + ex_sems,
        compiler_params=_cp(("arbitrary",) * 3 if scatter else ("parallel", "parallel", "arbitrary")),
    )(a, b, *scatter)
    return (outs[0], outs[1:]) if scatter else outs[0]


def _row_tile(t, want):
    tm = min(t, want)
    assert t % tm == 0
    return tm


def _mm_colblocked(name, a, wg, out_dtype):
    t = a.shape[0]
    tm = _row_tile(t, 1024)
    return _mm(name, a, wg, (t // tm, NDEV, 1),
               pl.BlockSpec((tm, D), lambda i, j, k: (i, 0)),
               pl.BlockSpec((None, D, 768), lambda i, j, k: (j, 0, 0)),
               pl.BlockSpec((tm, 768), lambda i, j, k: (i, j)),
               jax.ShapeDtypeStruct((t, NDEV * 768), out_dtype), NN, (tm, 768))


def _mm_colblocked_nt(name, g, wg, out_dtype, scatter=()):
    t = g.shape[0]
    tm = _row_tile(t, 1024)
    return _mm(name, g, wg, (t // tm, 1, NDEV),
               pl.BlockSpec((tm, 768), lambda i, j, k: (i, k)),
               pl.BlockSpec((None, D, 768), lambda i, j, k: (k, 0, 0)),
               pl.BlockSpec((tm, D), lambda i, j, k: (i, 0)),
               jax.ShapeDtypeStruct((t, D), out_dtype), NT, (tm, D), scatter)


_HALF = NDEV // 2


def _ffn_in_fwd(h2, wg):
    t = h2.shape[0]
    tm = _row_tile(t, 1024)

    def body(a_ref, wg_ref, wu_ref, g_ref, u_ref, act_ref):
        a = a_ref[...]
        g = _dot(a, wg_ref[...])
        u = _dot(a, wu_ref[...])
        g_ref[...] = g.astype(BF16)
        u_ref[...] = u.astype(BF16)
        act_ref[...] = (g * jax.nn.sigmoid(g) * u).astype(BF16)

    tile = pl.BlockSpec((tm, 768), lambda i, j: (i, j))
    return pl.pallas_call(
        body, name="ffn_in_fwd", grid=(t // tm, _HALF),
        in_specs=[pl.BlockSpec((tm, D), lambda i, j: (i, 0)),
                  pl.BlockSpec((None, D, 768), lambda i, j: (j, 0, 0)),
                  pl.BlockSpec((None, D, 768), lambda i, j: (j + _HALF, 0, 0))],
        out_specs=[tile] * 3, out_shape=[jax.ShapeDtypeStruct((t, FFP), BF16)] * 3,
        compiler_params=_cp(("parallel", "parallel")),
    )(h2, wg, wg)


def _ffn_out_dgrad(dfo, w, fg, fu):
    t = dfo.shape[0]
    tm = _row_tile(t, 1024)

    def body(a_ref, w_ref, g_ref, u_ref, dg_ref, du_ref):
        d = _dot(a_ref[...], w_ref[...], NT)
        g = g_ref[...].astype(F32)
        s = jax.nn.sigmoid(g)
        dg_ref[...] = (d * u_ref[...].astype(F32) * (s * (1.0 + g * (1.0 - s)))).astype(BF16)
        du_ref[...] = (d * (g * s)).astype(BF16)

    tile = pl.BlockSpec((tm, 768), lambda i, j: (i, j))
    return pl.pallas_call(
        body, name="ffn_out_dgrad", grid=(t // tm, _HALF),
        in_specs=[pl.BlockSpec((tm, D), lambda i, j: (i, 0)),
                  pl.BlockSpec((768, D), lambda i, j: (j, 0)), tile, tile],
        out_specs=[tile] * 2, out_shape=[jax.ShapeDtypeStruct((t, FFP), BF16)] * 2,
        compiler_params=_cp(("parallel", "parallel")),
    )(dfo, w, fg, fu)


def _ffn_in_dgrad(dg, du, wg):
    t = dg.shape[0]
    tm = _row_tile(t, 1024)

    def body(g_ref, u_ref, w_ref, o_ref, acc_ref):
        k = pl.program_id(1)

        @pl.when(k == 0)
        def _():
            acc_ref[...] = jnp.zeros_like(acc_ref)

        @pl.when(k < _HALF)
        def _():
            acc_ref[...] += _dot(g_ref[...], w_ref[...], NT)

        @pl.when(k >= _HALF)
        def _():
            acc_ref[...] += _dot(u_ref[...], w_ref[...], NT)

        @pl.when(k == NDEV - 1)
        def _():
            o_ref[...] = acc_ref[...]

    return pl.pallas_call(
        body, name="ffn_in_dgrad", grid=(t // tm, NDEV),
        in_specs=[pl.BlockSpec((tm, 768), lambda i, k: (i, jnp.minimum(k, _HALF - 1))),
                  pl.BlockSpec((tm, 768), lambda i, k: (i, jnp.maximum(k - _HALF, 0))),
                  pl.BlockSpec((None, D, 768), lambda i, k: (k, 0, 0))],
        out_specs=pl.BlockSpec((tm, D), lambda i, k: (i, 0)),
        out_shape=jax.ShapeDtypeStruct((t, D), F32),
        scratch_shapes=[pltpu.VMEM((tm, D), F32)],
        compiler_params=_cp(("parallel", "arbitrary")),
    )(dg, du, wg)


def _ffn_in_wgrad(h2, dg, du):
    t = h2.shape[0]
    tk = _row_tile(t, 1024)
    nk = t // tk

    def body(a_ref, g_ref, u_ref, o_ref, acc_ref):
        j, k = pl.program_id(0), pl.program_id(1)

        @pl.when(k == 0)
        def _():
            acc_ref[...] = jnp.zeros_like(acc_ref)

        @pl.when(j < _HALF)
        def _():
            acc_ref[...] += _dot(a_ref[...], g_ref[...], TN)

        @pl.when(j >= _HALF)
        def _():
            acc_ref[...] += _dot(a_ref[...], u_ref[...], TN)

        @pl.when(k == nk - 1)
        def _():
            o_ref[...] = acc_ref[...].astype(BF16)

    return pl.pallas_call(
        body, name="ffn_in_wgrad", grid=(NDEV, nk),
        in_specs=[pl.BlockSpec((tk, D), lambda j, k: (k, 0)),
                  pl.BlockSpec((tk, 768), lambda j, k: (jnp.where(j < _HALF, k, 0), jnp.minimum(j, _HALF - 1))),
                  pl.BlockSpec((tk, 768), lambda j, k: (jnp.where(j < _HALF, 0, k), jnp.maximum(j - _HALF, 0)))],
        out_specs=pl.BlockSpec((None, D, 768), lambda j, k: (j, 0, 0)),
        out_shape=jax.ShapeDtypeStruct((NDEV, D, 768), BF16),
        scratch_shapes=[pltpu.VMEM((D, 768), F32)],
        compiler_params=_cp(("parallel", "arbitrary")),
    )(h2, dg, du)


def _mm_colblocked_tn(name, a, g):
    t = a.shape[0]
    tk = _row_tile(t, 1024)
    return _mm(name, a, g, (1, NDEV, t // tk),
               pl.BlockSpec((tk, D), lambda i, j, k: (k, 0)),
               pl.BlockSpec((tk, 768), lambda i, j, k: (k, j)),
               pl.BlockSpec((None, D, 768), lambda i, j, k: (j, 0, 0)),
               jax.ShapeDtypeStruct((NDEV, D, 768), BF16), TN, (D, 768))


def _mm_nt(name, a, w, out_dtype, a_col=0, w_lead=None):
    t = a.shape[0]
    if w_lead is None:
        kdim, n = w.shape
        b_spec = pl.BlockSpec((min(kdim, 1024), n), lambda i, j, k: (j, 0))
    else:
        _, kdim, n = w.shape
        b_spec = pl.BlockSpec((None, min(kdim, 1024), n), lambda i, j, k: (w_lead, j, 0))
    tn = min(kdim, 1024)
    tm = _row_tile(t, 1024)
    return _mm(name, a, w, (t // tm, kdim // tn, 1),
               pl.BlockSpec((tm, n), lambda i, j, k: (i, a_col)),
               b_spec,
               pl.BlockSpec((tm, tn), lambda i, j, k: (i, j)),
               jax.ShapeDtypeStruct((t, kdim), out_dtype), NT, (tm, tn))


def _mm_tn(name, a, g, out_dtype=BF16):
    t, kdim = a.shape
    n = g.shape[1]
    tk = _row_tile(t, 1024)
    tm = min(kdim, 1024)
    tn = min(n, 1024)
    return _mm(name, a, g, (kdim // tm, n // tn, t // tk),
               pl.BlockSpec((tk, tm), lambda i, j, k: (k, i)),
               pl.BlockSpec((tk, tn), lambda i, j, k: (k, j)),
               pl.BlockSpec((tm, tn), lambda i, j, k: (i, j)),
               jax.ShapeDtypeStruct((kdim, n), out_dtype), TN, (tm, tn))


def _mm_residual(name, a, w, x, gt, seq):
    t, kdim = a.shape
    tm = _row_tile(seq, 1024)
    tn = D
    tk = min(kdim, 1024)
    nk = kdim // tk
    per = seq // tm

    def body(a_ref, w_ref, x_ref, gt_ref, xo_ref, y_ref, acc_ref):
        k = pl.program_id(2)

        @pl.when(k == 0)
        def _():
            acc_ref[...] = jnp.zeros_like(acc_ref)

        acc_ref[...] += _dot(a_ref[...], w_ref[...])

        @pl.when(k == nk - 1)
        def _():
            y = acc_ref[...]
            xo_ref[...] = x_ref[...] + gt_ref[0] * y
            y_ref[...] = y.astype(BF16)

    return pl.pallas_call(
        body, name=name, grid=(t // tm, D // tn, nk),
        in_specs=[pl.BlockSpec((tm, tk), lambda i, j, k: (i, k)),
                  pl.BlockSpec((tk, tn), lambda i, j, k: (k, j)),
                  pl.BlockSpec((tm, tn), lambda i, j, k: (i, j)),
                  pl.BlockSpec((1, 1, tn), lambda i, j, k: (i // per, 0, j))],
        out_specs=[pl.BlockSpec((tm, tn), lambda i, j, k: (i, j)),
                   pl.BlockSpec((tm, tn), lambda i, j, k: (i, j))],
        out_shape=[jax.ShapeDtypeStruct((t, D), F32), jax.ShapeDtypeStruct((t, D), BF16)],
        scratch_shapes=[pltpu.VMEM((tm, tn), F32)],
        compiler_params=_cp(("parallel", "parallel", "arbitrary")),
    )(a, w, x, gt)


def _ada_fwd(c_all, w_ada, b_blk):
    nl = w_ada.shape[0]
    nb = c_all.shape[0]

    def body(c_ref, w_ref, b_ref, o_ref):
        c = c_ref[...]
        ca = (c * jax.nn.sigmoid(c)).astype(BF16)
        o_ref[...] = _dot(ca, w_ref[...].astype(BF16)) + b_ref[...]

    return pl.pallas_call(
        body, name="ada_fwd", grid=(nl,),
        in_specs=[pl.BlockSpec((nb, D), lambda l: (0, 0)),
                  pl.BlockSpec((None, D, 768), lambda l: (l, 0, 0)),
                  pl.BlockSpec((None, 1, 768), lambda l: (l, 0, 0))],
        out_specs=pl.BlockSpec((None, nb, 768), lambda l: (l, 0, 0)),
        out_shape=jax.ShapeDtypeStruct((nl, nb, 768), F32),
        compiler_params=_cp(("parallel",)),
    )(c_all, w_ada, b_blk)


def _ada_bwd(c_all, dmod_blk):
    nl = dmod_blk.shape[0]
    nb = c_all.shape[0]

    def body(c_ref, d_ref, o_ref):
        c = c_ref[...]
        ca = (c * jax.nn.sigmoid(c)).astype(BF16)
        o_ref[...] = _dot(ca, d_ref[...].astype(BF16), TN)

    return pl.pallas_call(
        body, name="ada_bwd", grid=(nl,),
        in_specs=[pl.BlockSpec((nb, D), lambda l: (0, 0)),
                  pl.BlockSpec((None, nb, 768), lambda l: (l, 0, 0))],
        out_specs=pl.BlockSpec((None, D, 768), lambda l: (l, 0, 0)),
        out_shape=jax.ShapeDtypeStruct((nl, D, 768), F32),
        compiler_params=_cp(("parallel",)),
    )(c_all, dmod_blk)


def _seq_tile(seq):
    return _row_tile(seq, 512)


def _norm_mod_fwd(x, g, sc, sh):
    nb, seq, _ = x.shape
    ts = _seq_tile(seq)

    def body(x_ref, g_ref, sc_ref, sh_ref, h_ref):
        xv = x_ref[0]
        r = lax.rsqrt(jnp.mean(xv * xv, axis=-1, keepdims=True) + EPS)
        h_ref[0] = ((xv * r) * g_ref[...] * (1.0 + sc_ref[0]) + sh_ref[0]).astype(BF16)

    return pl.pallas_call(
        body, name="norm_mod_fwd", grid=(nb, seq // ts),
        in_specs=[pl.BlockSpec((1, ts, D), lambda b, s: (b, s, 0)),
                  pl.BlockSpec((1, D), lambda b, s: (0, 0)),
                  pl.BlockSpec((1, 1, D), lambda b, s: (b, 0, 0)),
                  pl.BlockSpec((1, 1, D), lambda b, s: (b, 0, 0))],
        out_specs=pl.BlockSpec((1, ts, D), lambda b, s: (b, s, 0)),
        out_shape=jax.ShapeDtypeStruct((nb, seq, D), BF16),
        compiler_params=_cp(("parallel", "parallel")),
    )(x, g, sc, sh)


def _norm_mod_bwd(x, dh, dres, g, sc):
    nb, seq, _ = x.shape
    ts = _seq_tile(seq)

    def body(x_ref, dh_ref, dres_ref, g_ref, sc_ref, dx_ref, dsh_ref, dsc_ref, dg_ref):
        @pl.when(pl.program_id(1) == 0)
        def _():
            dsh_ref[...] = jnp.zeros_like(dsh_ref)
            dsc_ref[...] = jnp.zeros_like(dsc_ref)
            dg_ref[...] = jnp.zeros_like(dg_ref)

        xv = x_ref[0]
        dh = dh_ref[0]
        gv = g_ref[...]
        onesc = 1.0 + sc_ref[0]
        r = lax.rsqrt(jnp.mean(xv * xv, axis=-1, keepdims=True) + EPS)
        xh = xv * r
        dsh_ref[0] += jnp.sum(dh, axis=0, keepdims=True)
        dsc_ref[0] += jnp.sum(dh * (xh * gv), axis=0, keepdims=True)
        dg_ref[0] += jnp.sum(dh * onesc * xh, axis=0, keepdims=True)
        dxh = dh * (gv * onesc)
        dx = r * (dxh - xh * jnp.mean(dxh * xh, axis=-1, keepdims=True))
        dx_ref[0] = dres_ref[0] + dx

    vec = jax.ShapeDtypeStruct((nb, 1, D), F32)
    vspec = pl.BlockSpec((1, 1, D), lambda b, s: (b, 0, 0))
    tile = pl.BlockSpec((1, ts, D), lambda b, s: (b, s, 0))
    return pl.pallas_call(
        body, name="norm_mod_bwd", grid=(nb, seq // ts),
        in_specs=[tile, tile, tile, pl.BlockSpec((1, D), lambda b, s: (0, 0)), vspec],
        out_specs=[tile, vspec, vspec, vspec],
        out_shape=[jax.ShapeDtypeStruct((nb, seq, D), F32), vec, vec, vec],
        compiler_params=_cp(("parallel", "arbitrary")),
    )(x, dh, dres, g, sc)


def _gate_bwd(dx, y, gt):
    nb, seq, _ = dx.shape
    ts = _seq_tile(seq)

    def body(dx_ref, y_ref, gt_ref, dy_ref, dgt_ref):
        @pl.when(pl.program_id(1) == 0)
        def _():
            dgt_ref[...] = jnp.zeros_like(dgt_ref)

        d = dx_ref[0]
        dy_ref[0] = (gt_ref[0] * d).astype(BF16)
        dgt_ref[0] += jnp.sum(d * y_ref[0].astype(F32), axis=0, keepdims=True)

    vspec = pl.BlockSpec((1, 1, D), lambda b, s: (b, 0, 0))
    tile = pl.BlockSpec((1, ts, D), lambda b, s: (b, s, 0))
    return pl.pallas_call(
        body, name="gate_bwd", grid=(nb, seq // ts),
        in_specs=[tile, tile, vspec], out_specs=[tile, vspec],
        out_shape=[jax.ShapeDtypeStruct((nb, seq, D), BF16), jax.ShapeDtypeStruct((nb, 1, D), F32)],
        compiler_params=_cp(("parallel", "arbitrary")),
    )(dx, y, gt)


def _loss_head(x, tgt, g):
    nb, seq, _ = x.shape
    ts = _seq_tile(seq)

    def body(x_ref, t_ref, g_ref, dx_ref, loss_ref, dg_ref):
        @pl.when(pl.program_id(1) == 0)
        def _():
            loss_ref[...] = jnp.zeros_like(loss_ref)
            dg_ref[...] = jnp.zeros_like(dg_ref)

        xv = x_ref[0]
        gv = g_ref[...]
        r = lax.rsqrt(jnp.mean(xv * xv, axis=-1, keepdims=True) + EPS)
        xh = xv * r
        err = xh * gv - t_ref[0]
        per_tok = jnp.mean(err * err, axis=-1, keepdims=True)
        loss_ref[0] += 0.5 * jnp.sum(per_tok, axis=0, keepdims=True)
        dy = err * (1.0 / D)
        dg_ref[0] += jnp.sum(dy * xh, axis=0, keepdims=True)
        dxh = dy * gv
        dx_ref[0] = r * (dxh - xh * jnp.mean(dxh * xh, axis=-1, keepdims=True))

    tile = pl.BlockSpec((1, ts, D), lambda b, s: (b, s, 0))
    return pl.pallas_call(
        body, name="loss_head", grid=(nb, seq // ts),
        in_specs=[tile, tile, pl.BlockSpec((1, D), lambda b, s: (0, 0))],
        out_specs=[tile, pl.BlockSpec((1, 1, 128), lambda b, s: (b, 0, 0)),
                   pl.BlockSpec((1, 1, D), lambda b, s: (b, 0, 0))],
        out_shape=[jax.ShapeDtypeStruct((nb, seq, D), F32), jax.ShapeDtypeStruct((nb, 1, 128), F32),
                   jax.ShapeDtypeStruct((nb, 1, D), F32)],
        compiler_params=_cp(("parallel", "arbitrary")),
    )(x, tgt, g)


_GELU_C = math.sqrt(2.0 / math.pi)


def _gelu(x):
    return 0.5 * x * (1.0 + jnp.tanh(_GELU_C * (x + 0.044715 * (x * x * x))))


def _gelu_and_grad(x):
    t = jnp.tanh(_GELU_C * (x + 0.044715 * (x * x * x)))
    y = 0.5 * x * (1.0 + t)
    dy = 0.5 * (1.0 + t) + 0.5 * x * (1.0 - t * t) * (_GELU_C * (1.0 + 3.0 * 0.044715 * (x * x)))
    return y, dy


def _tril_mask():
    row = lax.broadcasted_iota(jnp.int32, (CH, CH), 0)
    col = lax.broadcasted_iota(jnp.int32, (CH, CH), 1)
    return row >= col


def _gmlp_fwd(proj, ln_g, ln_b, ws, bst):
    t = proj.shape[0]
    tm = _row_tile(t, 512)

    def body(u_ref, v_ref, lg_ref, lb_ref, ws_ref, bst_ref, o_ref):
        tril = _tril_mask()
        wm = [jnp.where(tril, ws_ref[g], 0.0).astype(BF16) for g in range(NG)]
        for ch in range(tm // CH):
            rows = slice(ch * CH, (ch + 1) * CH)
            u = _gelu(u_ref[rows, :].astype(F32))
            v = _gelu(v_ref[rows, :].astype(F32))
            mu = jnp.mean(v, axis=-1, keepdims=True)
            xc = v - mu
            rstd = lax.rsqrt(jnp.mean(xc * xc, axis=-1, keepdims=True) + EPS)
            vn = ((xc * rstd) * lg_ref[...] + lb_ref[...]).astype(BF16)
            for g in range(NG):
                cols = slice(g * CH, (g + 1) * CH)
                s = _dot(wm[g], vn[:, cols]) + bst_ref[:, g:g + 1]
                o_ref[rows, cols] = (u[:, cols] * s).astype(BF16)

    return pl.pallas_call(
        body, name="gmlp_fwd", grid=(t // tm,),
        in_specs=[pl.BlockSpec((tm, BW), lambda i: (i, 0)),
                  pl.BlockSpec((tm, BW), lambda i: (i, 1)),
                  pl.BlockSpec((1, BW), lambda i: (0, 0)),
                  pl.BlockSpec((1, BW), lambda i: (0, 0)),
                  pl.BlockSpec((NG, CH, CH), lambda i: (0, 0, 0)),
                  pl.BlockSpec((CH, NG), lambda i: (0, 0))],
        out_specs=pl.BlockSpec((tm, BW), lambda i: (i, 0)),
        out_shape=jax.ShapeDtypeStruct((t, BW), BF16),
        compiler_params=_cp(("parallel",)),
    )(proj, proj, ln_g, ln_b, ws, bst)


def _gmlp_bwd(proj, dout, ln_g, ln_b, ws, bst):
    t = proj.shape[0]
    tm = _row_tile(t, 512)

    def body(u_ref, v_ref, do_ref, lg_ref, lb_ref, ws_ref, bst_ref, dp_ref, gws_ref, gbs_ref, glg_ref, glb_ref):
        @pl.when(pl.program_id(0) == 0)
        def _():
            gws_ref[...] = jnp.zeros_like(gws_ref)
            gbs_ref[...] = jnp.zeros_like(gbs_ref)
            glg_ref[...] = jnp.zeros_like(glg_ref)
            glb_ref[...] = jnp.zeros_like(glb_ref)

        tril = _tril_mask()
        wm = [jnp.where(tril, ws_ref[g], 0.0).astype(BF16) for g in range(NG)]
        ones = jnp.ones((CH, CH), BF16)
        lg = lg_ref[...]
        for ch in range(tm // CH):
            rows = slice(ch * CH, (ch + 1) * CH)
            u, du_fac = _gelu_and_grad(u_ref[rows, :].astype(F32))
            v, dv_fac = _gelu_and_grad(v_ref[rows, :].astype(F32))
            do = do_ref[rows, :].astype(F32)
            mu = jnp.mean(v, axis=-1, keepdims=True)
            xc = v - mu
            rstd = lax.rsqrt(jnp.mean(xc * xc, axis=-1, keepdims=True) + EPS)
            xh = xc * rstd
            vn = (xh * lg + lb_ref[...]).astype(BF16)
            dvn_parts = []
            for g in range(NG):
                cols = slice(g * CH, (g + 1) * CH)
                s = _dot(wm[g], vn[:, cols]) + bst_ref[:, g:g + 1]
                dp_ref[rows, cols] = (do[:, cols] * s * du_fac[:, cols]).astype(BF16)
                ds = (do[:, cols] * u[:, cols]).astype(BF16)
                gws_ref[g] += jnp.where(tril, _dot(ds, vn[:, cols], NT), 0.0)
                gbs_ref[g] += _dot(ds, ones)
                dvn_parts.append(_dot(wm[g], ds, TN))
            dvn = jnp.concatenate(dvn_parts, axis=1)
            glb_ref[...] += jnp.sum(dvn, axis=0, keepdims=True)
            glg_ref[...] += jnp.sum(dvn * xh, axis=0, keepdims=True)
            dxh = dvn * lg
            dv = rstd * (dxh - jnp.mean(dxh, axis=-1, keepdims=True)
                         - xh * jnp.mean(dxh * xh, axis=-1, keepdims=True))
            dp_ref[rows, BW:2 * BW] = (dv * dv_fac).astype(BF16)

    small = pl.BlockSpec((NG, CH, CH), lambda i: (0, 0, 0))
    vec = pl.BlockSpec((1, BW), lambda i: (0, 0))
    return pl.pallas_call(
        body, name="gmlp_bwd", grid=(t // tm,),
        in_specs=[pl.BlockSpec((tm, BW), lambda i: (i, 0)),
                  pl.BlockSpec((tm, BW), lambda i: (i, 1)),
                  pl.BlockSpec((tm, BW), lambda i: (i, 0)),
                  vec, vec, small, pl.BlockSpec((CH, NG), lambda i: (0, 0))],
        out_specs=[pl.BlockSpec((tm, 2 * BW), lambda i: (i, 0)), small, small, vec, vec],
        out_shape=[jax.ShapeDtypeStruct((t, 2 * BW), BF16),
                   jax.ShapeDtypeStruct((NG, CH, CH), F32), jax.ShapeDtypeStruct((NG, CH, CH), F32),
                   jax.ShapeDtypeStruct((1, BW), F32), jax.ShapeDtypeStruct((1, BW), F32)],
        compiler_params=_cp(("arbitrary",)),
    )(proj, proj, dout, ln_g, ln_b, ws, bst)


def _pool_bands():
    row = lax.broadcasted_iota(jnp.int32, (CH, CH), 0)
    col = lax.broadcasted_iota(jnp.int32, (CH, CH), 1)
    cur, prev = [], []
    for w in POOL_WINDOWS:
        cur.append(jnp.where((row >= col) & (row - col < w), 1.0, 0.0).astype(BF16))
        prev.append(jnp.where(row + CH - col < w, 1.0, 0.0).astype(BF16))
    return cur, prev


def _pool_inv_count(r0, w):
    pos = r0 + lax.broadcasted_iota(jnp.int32, (CH, 1), 0)
    return 1.0 / jnp.minimum(pos + 1, w).astype(F32)


def _pool_diff(x_ref, r0, rp, has_prev, cur, prev, g):
    cols = slice(g * CH, (g + 1) * CH)
    xc = x_ref[pl.ds(r0, CH), cols]
    xp = x_ref[pl.ds(rp, CH), cols]
    ws = _dot(cur[g], xc) + has_prev * _dot(prev[g], xp)
    return ws * _pool_inv_count(r0, POOL_WINDOWS[g]) - xc.astype(F32)


def _pool_fwd(proj3, pw, pscale):
    nb, seq, _ = proj3.shape
    nch = seq // CH

    def body(x_ref, pw_ref, ps_ref, o_ref):
        cur, prev = _pool_bands()
        pwb = [pw_ref[g].astype(BF16) for g in range(NG)]

        def chunk(ch, carry):
            r0 = pl.multiple_of(ch * CH, CH)
            rp = pl.multiple_of(jnp.maximum(ch - 1, 0) * CH, CH)
            has_prev = jnp.where(ch > 0, 1.0, 0.0)
            for g in range(NG):
                cols = slice(g * CH, (g + 1) * CH)
                d = _pool_diff(x_ref, r0, rp, has_prev, cur, prev, g)
                y = _dot(d.astype(BF16), pwb[g]) * ps_ref[:, cols]
                o_ref[pl.ds(r0, CH), cols] = y.astype(BF16)
            return carry

        lax.fori_loop(0, nch, chunk, 0)

    return pl.pallas_call(
        body, name="pool_fwd", grid=(nb,),
        in_specs=[pl.BlockSpec((None, seq, BW), lambda b: (b, 0, 5)),
                  pl.BlockSpec((NG, CH, CH), lambda b: (0, 0, 0)),
                  pl.BlockSpec((1, BW), lambda b: (0, 0))],
        out_specs=pl.BlockSpec((None, seq, BW), lambda b: (b, 0, 0)),
        out_shape=jax.ShapeDtypeStruct((nb, seq, BW), BF16),
        compiler_params=_cp(("parallel",)),
    )(proj3, pw, pscale)


def _pool_bwd(proj3, dout3, pw, pscale):
    nb, seq, _ = proj3.shape
    nch = seq // CH

    def body(x_ref, do_ref, pw_ref, ps_ref, dx_ref, gpw_ref, gps_ref, e_ref):
        @pl.when(pl.program_id(0) == 0)
        def _():
            gpw_ref[...] = jnp.zeros_like(gpw_ref)
            gps_ref[...] = jnp.zeros_like(gps_ref)

        cur, prev = _pool_bands()
        pwb = [pw_ref[g].astype(BF16) for g in range(NG)]

        def first(ch, carry):
            r0 = pl.multiple_of(ch * CH, CH)
            rp = pl.multiple_of(jnp.maximum(ch - 1, 0) * CH, CH)
            has_prev = jnp.where(ch > 0, 1.0, 0.0)
            for g in range(NG):
                cols = slice(g * CH, (g + 1) * CH)
                d = _pool_diff(x_ref, r0, rp, has_prev, cur, prev, g).astype(BF16)
                do = do_ref[pl.ds(r0, CH), cols].astype(F32)
                ypre = _dot(d, pwb[g])
                gps_ref[:, cols] += jnp.sum(do * ypre, axis=0, keepdims=True)
                dyp = (do * ps_ref[:, cols]).astype(BF16)
                gpw_ref[g] += _dot(d, dyp, TN)
                e_ref[pl.ds(r0, CH), cols] = _dot(dyp, pwb[g], NT)
            return carry

        lax.fori_loop(0, nch, first, 0)

        def second(ch, carry):
            r0 = pl.multiple_of(ch * CH, CH)
            rn = pl.multiple_of(jnp.minimum(ch + 1, nch - 1) * CH, CH)
            has_next = jnp.where(ch < nch - 1, 1.0, 0.0)
            for g in range(NG):
                cols = slice(g * CH, (g + 1) * CH)
                w = POOL_WINDOWS[g]
                dd = e_ref[pl.ds(r0, CH), cols]
                ec = (dd * _pool_inv_count(r0, w)).astype(BF16)
                en = (e_ref[pl.ds(rn, CH), cols] * _pool_inv_count(rn, w)).astype(BF16)
                dx = _dot(cur[g], ec, TN) + has_next * _dot(prev[g], en, TN) - dd
                dx_ref[pl.ds(r0, CH), cols] = dx.astype(BF16)
            return carry

        lax.fori_loop(0, nch, second, 0)

    small = pl.BlockSpec((NG, CH, CH), lambda b: (0, 0, 0))
    vec = pl.BlockSpec((1, BW), lambda b: (0, 0))
    return pl.pallas_call(
        body, name="pool_bwd", grid=(nb,),
        in_specs=[pl.BlockSpec((None, seq, BW), lambda b: (b, 0, 5)),
                  pl.BlockSpec((None, seq, BW), lambda b: (b, 0, 0)), small, vec],
        out_specs=[pl.BlockSpec((None, seq, BW), lambda b: (b, 0, 0)), small, vec],
        out_shape=[jax.ShapeDtypeStruct((nb, seq, BW), BF16),
                   jax.ShapeDtypeStruct((NG, CH, CH), F32), jax.ShapeDtypeStruct((1, BW), F32)],
        scratch_shapes=[pltpu.VMEM((seq, BW), F32)],
        compiler_params=_cp(("arbitrary",)),
    )(proj3, dout3, pw, pscale)


SB_BQ = 256
SB_BK = 256
SB_SCALE = HD ** -0.5


def _sb_tile(qs, k, mask):
    z = _dot(qs, k, NT)
    e = jnp.exp(-jnp.abs(z))
    lb = jnp.minimum(z, 0.0) - jnp.log(1.0 + e)
    lom = lb - z
    if mask is not None:
        lom = jnp.where(mask, lom, 0.0)
    return z, e, lb, lom


def _sb_diag_mask(bq, d):
    row = lax.broadcasted_iota(jnp.int32, (bq, SB_BK), 0)
    col = lax.broadcasted_iota(jnp.int32, (bq, SB_BK), 1)
    return col + d * SB_BK < row


def _sb_scaled(q):
    return (q.astype(F32) * SB_SCALE).astype(BF16)


def _dot_tri(a, m):
    return _dot(a.astype(BF16), m)


def _sb_fwd(proj3, gather=()):
    nb, seq, _ = proj3.shape
    bq = min(SB_BQ, seq)
    nq = seq // bq
    ndiag = bq // SB_BK

    def body(q_ref, k_ref, v_ref, o_ref, t_ref):
        row = lax.broadcasted_iota(jnp.int32, (SB_BK, SB_BK), 0)
        col = lax.broadcasted_iota(jnp.int32, (SB_BK, SB_BK), 1)
        upper = jnp.where(row > col, 1.0, 0.0).astype(BF16)
        heads = [slice(hh * HD, (hh + 1) * HD) for hh in range(2)]

        def qloop(qi, carry):
            q0 = pl.multiple_of(qi * bq, bq)
            qs = [_sb_scaled(q_ref[pl.ds(q0, bq), lanes]) for lanes in heads]

            def step(k0, c, mask):
                tiles = [_sb_tile(q, k_ref[pl.ds(k0, SB_BK), lanes], mask) for lanes, q in zip(heads, qs)]
                sums = [_dot_tri(lom, upper) for _, _, _, lom in tiles]
                out = []
                for lanes, (acc, cr), (_, _, lb, lom), cs in zip(heads, c, tiles, sums):
                    a = jnp.exp(lb + (cs + cr))
                    if mask is not None:
                        a = jnp.where(mask, a, 0.0)
                    rsum = cs[:, 0:1] + lom[:, 0:1]
                    out.append((acc + _dot(a.astype(BF16), v_ref[pl.ds(k0, SB_BK), lanes]), cr + rsum))
                return tuple(out)

            c = tuple((jnp.zeros((bq, HD), F32), jnp.zeros((bq, 1), F32)) for _ in heads)
            for d in reversed(range(ndiag)):
                c = step(pl.multiple_of(q0 + d * SB_BK, SB_BK), c, _sb_diag_mask(bq, d))
            npast = qi * ndiag
            c = lax.fori_loop(
                0, npast, lambda i, c: step(pl.multiple_of((npast - 1 - i) * SB_BK, SB_BK), c, None), c)
            for lanes, (acc, cr) in zip(heads, c):
                o_ref[pl.ds(q0, bq), lanes] = acc.astype(BF16)
                t_ref[pl.ds(q0, bq), lanes] = jnp.broadcast_to(cr, (bq, HD))
            return carry

        lax.fori_loop(0, nq, qloop, 0)

    def spec(c0):
        return pl.BlockSpec((None, seq, 128), lambda b, p: (b, 0, c0 + p))

    grid = (nb, BW // 128)
    body, ex_in, ex_out, ex_shape, ex_sems = _host_exchange(body, 3, 2, grid, gather, False)
    outs = pl.pallas_call(
        body, name="sb_fwd", grid=grid,
        in_specs=[spec(8), spec(12), spec(16)] + ex_in,
        out_specs=[spec(0), spec(0)] + ex_out,
        out_shape=[jax.ShapeDtypeStruct((nb, seq, BW), BF16), jax.ShapeDtypeStruct((nb, seq, BW), F32)] + ex_shape,
        scratch_shapes=ex_sems,
        compiler_params=_cp(("arbitrary", "arbitrary")),
    )(proj3, proj3, proj3, *gather)
    return outs[0], outs[1], outs[2:]


def _sb_bwd(proj3, do3, tot3, scatter=()):
    nb, seq, _ = proj3.shape
    bq = min(SB_BQ, seq)
    nq = seq // bq
    ndiag = bq // SB_BK

    def body(q_ref, k_ref, v_ref, do_ref, t_ref, dq_ref, dk_ref, dv_ref, dk_acc, dv_acc):
        row = lax.broadcasted_iota(jnp.int32, (SB_BK, SB_BK), 0)
        col = lax.broadcasted_iota(jnp.int32, (SB_BK, SB_BK), 1)
        upper = jnp.where(row > col, 1.0, 0.0).astype(BF16)
        lower = jnp.where(row < col, 1.0, 0.0).astype(BF16)
        dk_acc[...] = jnp.zeros_like(dk_acc)
        dv_acc[...] = jnp.zeros_like(dv_acc)
        heads = [slice(hh * HD, (hh + 1) * HD) for hh in range(2)]

        def qloop(qi, carry):
            q0 = pl.multiple_of(qi * bq, bq)
            qs = [_sb_scaled(q_ref[pl.ds(q0, bq), lanes]) for lanes in heads]
            dos = [do_ref[pl.ds(q0, bq), lanes] for lanes in heads]
            tots = [t_ref[pl.ds(q0, bq), hh * HD:hh * HD + 1] for hh in range(2)]

            def step(k0, c, mask):
                ks = [k_ref[pl.ds(k0, SB_BK), lanes] for lanes in heads]
                tiles = [_sb_tile(q, k, mask) for q, k in zip(qs, ks)]
                sums = [_dot_tri(lom, upper) for _, _, _, lom in tiles]
                das = [_dot(do, v_ref[pl.ds(k0, SB_BK), lanes], NT) for do, lanes in zip(dos, heads)]
                gls, rsums, avs = [], [], []
                for hh, (_, cpre, _) in enumerate(c):
                    _, _, lb, lom = tiles[hh]
                    rsum = sums[hh][:, 0:1] + lom[:, 0:1]
                    a = jnp.exp(lb + (sums[hh] + (tots[hh] - cpre - rsum)))
                    if mask is not None:
                        a = jnp.where(mask, a, 0.0)
                    gls.append(das[hh] * a)
                    rsums.append(rsum)
                    avs.append(a.astype(BF16))
                pres = [_dot_tri(gl, lower) for gl in gls]
                out = []
                for hh, (dq, cpre, gpre) in enumerate(c):
                    z, e, _, _ = tiles[hh]
                    inv = 1.0 / (1.0 + e)
                    pos = z >= 0.0
                    beta = jnp.where(pos, 1.0, e) * inv
                    omb = jnp.where(pos, e, 1.0) * inv
                    dz = gls[hh] * omb - beta * (pres[hh] + gpre)
                    if mask is not None:
                        dz = jnp.where(mask, dz, 0.0)
                    dz = dz.astype(BF16)
                    dk_acc[hh, pl.ds(k0, SB_BK), :] += _dot(dz, qs[hh], TN)
                    dv_acc[hh, pl.ds(k0, SB_BK), :] += _dot(avs[hh], dos[hh], TN)
                    gsum = pres[hh][:, SB_BK - 1:SB_BK] + gls[hh][:, SB_BK - 1:SB_BK]
                    out.append((dq + _dot(dz, ks[hh]), cpre + rsums[hh], gpre + gsum))
                return tuple(out)

            c = tuple((jnp.zeros((bq, HD), F32), jnp.zeros((bq, 1), F32), jnp.zeros((bq, 1), F32))
                      for _ in heads)
            npast = qi * ndiag
            c = lax.fori_loop(0, npast, lambda kb, c: step(pl.multiple_of(kb * SB_BK, SB_BK), c, None), c)
            for d in range(ndiag):
                c = step(pl.multiple_of(q0 + d * SB_BK, SB_BK), c, _sb_diag_mask(bq, d))
            for lanes, (dq, _, _) in zip(heads, c):
                dq_ref[pl.ds(q0, bq), lanes] = (dq * SB_SCALE).astype(BF16)
            return carry

        lax.fori_loop(0, nq, qloop, 0)
        for hh in range(2):
            lanes = slice(hh * HD, (hh + 1) * HD)
            dk_ref[:, lanes] = dk_acc[hh].astype(BF16)
            dv_ref[:, lanes] = dv_acc[hh].astype(BF16)

    def spec(c0):
        return pl.BlockSpec((None, seq, 128), lambda b, p: (b, 0, c0 + p))

    grid = (nb, BW // 128)
    body, ex_in, ex_out, ex_shape, ex_sems = _host_exchange(body, 5, 3, grid, scatter, True)
    outs = pl.pallas_call(
        body, name="sb_bwd", grid=grid,
        in_specs=[spec(8), spec(12), spec(16), spec(0), spec(0)] + ex_in,
        out_specs=[spec(0), spec(0), spec(0)] + ex_out,
        out_shape=[jax.ShapeDtypeStruct((nb, seq, BW), BF16)] * 3 + ex_shape,
        scratch_shapes=[pltpu.VMEM((2, seq, HD), F32), pltpu.VMEM((2, seq, HD), F32)] + ex_sems,
        compiler_params=_cp(("arbitrary", "arbitrary")),
    )(proj3, proj3, proj3, do3, tot3, *scatter)
    return outs[:3], outs[3:]


def _merge_fwd(brs, wb, proj):
    t = proj.shape[0]
    tm = _row_tile(t, 512)
    tn = 512
    nj = D // tn

    def body(b0, b1, b2, wb_ref, l0, l1, l2, m_ref, y0, y1, y2):
        acc = None
        for br, n, lg, y_ref in ((b0, 0, l0, y0), (b1, 1, l1, y1), (b2, 2, l2, y2)):
            y = _dot(br[...], wb_ref[n])
            y_ref[...] = y.astype(BF16)
            term = jax.nn.sigmoid(lg[...].astype(F32)) * y
            acc = term if acc is None else acc + term
        m_ref[...] = acc.astype(BF16)

    def lspec(n):
        return pl.BlockSpec((tm, tn), lambda i, j: (i, (3 * D + n * D) // tn + j))

    tile = pl.BlockSpec((tm, tn), lambda i, j: (i, j))
    bspec = pl.BlockSpec((tm, BW), lambda i, j: (i, 0))
    return pl.pallas_call(
        body, name="merge_fwd", grid=(t // tm, nj),
        in_specs=[bspec, bspec, bspec, pl.BlockSpec((NB, BW, tn), lambda i, j: (0, 0, j)),
                  lspec(0), lspec(1), lspec(2)],
        out_specs=[tile] * 4,
        out_shape=[jax.ShapeDtypeStruct((t, D), BF16)] * 4,
        compiler_params=_cp(("parallel", "parallel")),
    )(brs[0], brs[1], brs[2], wb, proj, proj, proj)


def _merge_bwd(dm, ys, proj):
    t = proj.shape[0]
    tm = _row_tile(t, 512)
    tn = 512

    def body(dm_ref, y0, y1, y2, l0, l1, l2, dl0, dl1, dl2, dy0, dy1, dy2):
        dmv = dm_ref[...].astype(F32)
        for y_ref, lg, dl_ref, dy_ref in ((y0, l0, dl0, dy0), (y1, l1, dl1, dy1), (y2, l2, dl2, dy2)):
            g = jax.nn.sigmoid(lg[...].astype(F32))
            dl_ref[...] = (dmv * y_ref[...].astype(F32) * g * (1.0 - g)).astype(BF16)
            dy_ref[...] = (dmv * g).astype(BF16)

    def lspec(n):
        return pl.BlockSpec((tm, tn), lambda i, j: (i, (3 * D + n * D) // tn + j))

    tile = pl.BlockSpec((tm, tn), lambda i, j: (i, j))
    return pl.pallas_call(
        body, name="merge_bwd", grid=(t // tm, D // tn),
        in_specs=[tile] * 4 + [lspec(0), lspec(1), lspec(2)],
        out_specs=[tile] * 6,
        out_shape=[jax.ShapeDtypeStruct((t, D), BF16)] * 6,
        compiler_params=_cp(("parallel", "parallel")),
    )(dm, ys[0], ys[1], ys[2], proj, proj, proj)


def _adamw_math(npart, p_ref, w_ref, m_ref, v_ref, g_ref, d_ref, mo_ref, vo_ref):
    c1 = 1.0 - ADAM_B1 ** ADAM_STEP
    c2 = 1.0 - ADAM_B2 ** ADAM_STEP
    g = p_ref[0].astype(F32)
    for p in range(1, npart):
        g = g + p_ref[p].astype(F32)
    mn = ADAM_B1 * m_ref[...] + (1.0 - ADAM_B1) * g
    vn = ADAM_B2 * v_ref[...] + (1.0 - ADAM_B2) * (g * g)
    m_hat = mn / c1
    v_hat = vn / c2
    g_ref[...] = g
    d_ref[...] = -ADAM_LR * (m_hat / (jnp.sqrt(v_hat) + ADAM_EPS) + ADAM_WD * w_ref[...])
    mo_ref[...] = mn
    vo_ref[...] = vn


def _adamw_layer(name, parts, w, m, v, layer, bufs):
    nl, cols = w.shape[0], w.shape[-1]
    rows = int(math.prod(w.shape[1:-1]))
    npart = parts.shape[0]
    tr = rows if rows <= 512 else 512
    assert rows % tr == 0
    if bufs is None:
        bufs = [lax.empty((nl, rows, cols), F32) for _ in range(4)]

    def body(p_ref, w_ref, m_ref, v_ref, b0, b1, b2, b3, g_ref, d_ref, mo_ref, vo_ref):
        _adamw_math(npart, p_ref, w_ref, m_ref, v_ref, g_ref, d_ref, mo_ref, vo_ref)

    slab = pl.BlockSpec((None, tr, cols), lambda i: (layer, i, 0))
    sds = jax.ShapeDtypeStruct((nl, rows, cols), F32)
    return pl.pallas_call(
        body, name=name, grid=(rows // tr,),
        in_specs=[pl.BlockSpec((npart, tr, cols), lambda i: (0, i, 0)), slab, slab, slab] + [_HBM] * 4,
        out_specs=[slab] * 4, out_shape=[sds] * 4,
        input_output_aliases={4: 0, 5: 1, 6: 2, 7: 3},
        compiler_params=_cp(("parallel",)),
    )(parts.reshape(npart, rows, cols), w.reshape(nl, rows, cols), m.reshape(nl, rows, cols),
      v.reshape(nl, rows, cols), *bufs)


def _adamw_reduce(name, parts, w, m, v):
    shape = w.shape
    cols = shape[-1]
    rows = int(math.prod(shape[:-1])) if len(shape) > 1 else 1
    npart = parts.shape[0]
    tr = rows if rows <= 512 else 512
    assert rows % tr == 0

    def body(p_ref, w_ref, m_ref, v_ref, g_ref, d_ref, mo_ref, vo_ref):
        _adamw_math(npart, p_ref, w_ref, m_ref, v_ref, g_ref, d_ref, mo_ref, vo_ref)

    tile = pl.BlockSpec((tr, cols), lambda i: (i, 0))
    sds = jax.ShapeDtypeStruct((rows, cols), F32)
    outs = pl.pallas_call(
        body, name=name, grid=(rows // tr,),
        in_specs=[pl.BlockSpec((npart, tr, cols), lambda i: (0, i, 0)), tile, tile, tile],
        out_specs=[tile] * 4, out_shape=[sds] * 4,
        compiler_params=_cp(("parallel",)),
    )(parts.reshape(npart, rows, cols), w.reshape(rows, cols), m.reshape(rows, cols), v.reshape(rows, cols))
    return tuple(o.reshape(shape) for o in outs)


def _pad_ffn_in(w):
    lead = w.shape[:-1]
    w = w.reshape(lead + (2, FF_HALF))
    w = jnp.pad(w, [(0, 0)] * len(lead) + [(0, 0), (0, FF_HALF_PAD - FF_HALF)])
    return w.reshape(lead + (FF_IN_PAD,))


def _unpad_ffn_in(w):
    lead = w.shape[:-1]
    return w.reshape(lead + (2, FF_HALF_PAD))[..., :FF_HALF].reshape(lead + (FF_IN_SHARD,))


def kernel(x, c, rms_g1, rms_g2, w_ada, b_ada, w_in, gm_ln_g, gm_ln_b, gm_w_spatial, gm_b_spatial, pool_w, pool_scale, w_branch, w_out, w_ffn_in, w_ffn_out, final_g, loss_target, m_rms_g1, m_rms_g2, m_w_ada, m_b_ada, m_w_in, m_gm_ln_g, m_gm_ln_b, m_gm_w_spatial, m_gm_b_spatial, m_pool_w, m_pool_scale, m_w_branch, m_w_out, m_w_ffn_in, m_w_ffn_out, m_final_g, v_rms_g1, v_rms_g2, v_w_ada, v_b_ada, v_w_in, v_gm_ln_g, v_gm_ln_b, v_gm_w_spatial, v_gm_b_spatial, v_pool_w, v_pool_scale, v_w_branch, v_w_out, v_w_ffn_in, v_w_ffn_out, v_final_g):
    nb, seq, _ = x.shape
    nl = w_in.shape[0]
    t = nb * seq
    ntot = NDEV * nb
    me = _my_index()
    assert x.shape[2] == D and w_in.shape[1:] == (D, 768) and w_ffn_in.shape[1:] == (D, FF_IN_SHARD)
    assert seq % CH == 0

    w_ffn_in_p = _pad_ffn_in(w_ffn_in).astype(BF16)
    w_ffn_out_p = jnp.pad(w_ffn_out, ((0, 0), (0, FF_HALF_PAD - FF_HALF), (0, 0))).astype(BF16)
    w_in_b = w_in.astype(BF16)
    w_branch_b = w_branch.astype(BF16)
    w_out_b = w_out.astype(BF16)
    (g_in_next,) = _exchange([w_in_b[0]], "gather_w_in0", False)

    (c_all,) = _exchange([c], "gather_c", False)
    c_all = c_all.reshape(ntot, D)
    b_blk = lax.dynamic_slice_in_dim(b_ada, me * 768, 768, axis=1).reshape(nl, 1, 768)
    mod_blk = _ada_fwd(c_all, w_ada, b_blk)
    (mod_all,) = _exchange([mod_blk], "gather_mod", False)
    mod_all = jnp.transpose(mod_all, (1, 2, 0, 3)).reshape(nl, ntot, NMOD * D)
    mod = lax.dynamic_slice_in_dim(mod_all, me * nb, nb, axis=1).reshape(nl, nb, NMOD, 1, D)

    saved = []
    gathered = []
    xc = x
    for l in range(nl):
        sh1, sc1, gt1, sh2, sc2, gt2 = [mod[l, :, i] for i in range(NMOD)]
        h = _norm_mod_fwd(xc, rms_g1[l].reshape(1, D), sc1, sh1).reshape(t, D)
        proj = _mm_colblocked("proj_fwd", h, g_in_next, BF16)
        proj3 = proj.reshape(nb, seq, IN_COLS)
        br_gm = _gmlp_fwd(proj, gm_ln_g[l].reshape(1, BW), gm_ln_b[l].reshape(1, BW),
                          gm_w_spatial[l], gm_b_spatial[l].T)
        carried = [w_branch_b[l], w_out_b[l], w_ffn_in_p[l], w_ffn_out_p[l]]
        if l + 1 < nl:
            carried.append(w_in_b[l + 1])
        br_sb, sb_tot, got = _sb_fwd(proj3, carried)
        gw = dict(w_in=g_in_next,
                  w_branch=jnp.transpose(got[0], (1, 2, 0, 3)).reshape(NB, BW, D),
                  w_out=got[1].reshape(D, D),
                  w_ffn_in=got[2],
                  w_ffn_out=got[3].reshape(FFP, D))
        gathered.append(gw)
        if l + 1 < nl:
            g_in_next = got[4]
        br_pool = _pool_fwd(proj3, pool_w[l], pool_scale[l].reshape(1, BW))
        brs = [br_gm, br_sb.reshape(t, BW), br_pool.reshape(t, BW)]
        merged, y0, y1, y2 = _merge_fwd(brs, gw["w_branch"], proj)
        x_mid, mo = _mm_residual("out_fwd", merged, gw["w_out"], xc.reshape(t, D), gt1, seq)
        x_mid = x_mid.reshape(nb, seq, D)
        h2 = _norm_mod_fwd(x_mid, rms_g2[l].reshape(1, D), sc2, sh2).reshape(t, D)
        fg, fu, act = _ffn_in_fwd(h2, gw["w_ffn_in"])
        x_out, fo = _mm_residual("ffn_out_fwd", act, gw["w_ffn_out"], x_mid.reshape(t, D), gt2, seq)
        saved.append(dict(x_in=xc, h=h, proj=proj, brs=brs, sb_tot=sb_tot, ys=(y0, y1, y2), merged=merged,
                          mo=mo, x_mid=x_mid, h2=h2, fg=fg, fu=fu, act=act, fo=fo))
        xc = x_out.reshape(nb, seq, D)

    dx, loss_part, dfinal_part = _loss_head(xc, loss_target, final_g.reshape(1, D))
    loss = lax.psum(jnp.sum(loss_part[:, 0, 0]), ("x", "y", "c"))

    big_names = ("w_in", "w_branch", "w_out", "w_ffn_in", "w_ffn_out")
    bufs = {name: None for name in big_names}
    g_in_pending = None
    small_parts = {k: [None] * nl for k in ("rms_g1", "rms_g2", "gm_ln_g", "gm_ln_b", "gm_w_spatial",
                                            "gm_b_spatial", "pool_w", "pool_scale")}
    dmod = [None] * nl
    for l in reversed(range(nl)):
        gw = gathered[l]
        sv = saved[l]
        sh1, sc1, gt1, sh2, sc2, gt2 = [mod[l, :, i] for i in range(NMOD)]
        dfo, dgt2 = _gate_bwd(dx, sv["fo"].reshape(nb, seq, D), gt2)
        dfo = dfo.reshape(t, D)
        g_ffn_out = _mm_tn("ffn_out_wgrad", sv["act"], dfo)
        dfg, dfu = _ffn_out_dgrad(dfo, gw["w_ffn_out"], sv["fg"], sv["fu"])
        dh2 = _ffn_in_dgrad(dfg, dfu, gw["w_ffn_in"])
        g_ffn_in = _ffn_in_wgrad(sv["h2"], dfg, dfu)
        dx_mid, dsh2, dsc2, dg2 = _norm_mod_bwd(sv["x_mid"], dh2.reshape(nb, seq, D), dx,
                                                rms_g2[l].reshape(1, D), sc2)
        dmo, dgt1 = _gate_bwd(dx_mid, sv["mo"].reshape(nb, seq, D), gt1)
        dmo = dmo.reshape(t, D)
        dmerged = _mm_nt("out_dgrad", dmo, gw["w_out"], BF16)
        g_out = _mm_tn("out_wgrad", sv["merged"], dmo)
        dls_dys = _merge_bwd(dmerged, sv["ys"], sv["proj"])
        dls, dys = dls_dys[:3], dls_dys[3:]
        dbrs, g_br = [], []
        for n in range(NB):
            dbrs.append(_mm_nt("branch_dgrad", dys[n], gw["w_branch"], BF16, w_lead=n))
            g_br.append(_mm_tn("branch_wgrad", sv["brs"][n], dys[n]))
        proj3 = sv["proj"].reshape(nb, seq, IN_COLS)
        d_gm, g_ws, g_bs, g_lg, g_lb = _gmlp_bwd(sv["proj"], dbrs[0], gm_ln_g[l].reshape(1, BW),
                                                 gm_ln_b[l].reshape(1, BW), gm_w_spatial[l], gm_b_spatial[l].T)
        g_br_dev = jnp.transpose(jnp.stack(g_br).reshape(NB, BW, NDEV, D // NDEV), (2, 0, 1, 3))
        carried = [g_br_dev, g_out.reshape(NDEV, D // NDEV, D), g_ffn_in, g_ffn_out.reshape(NDEV, FF_HALF_PAD, D)]
        if g_in_pending is not None:
            carried.append(g_in_pending)
        d_sb, recv = _sb_bwd(proj3, dbrs[1].reshape(nb, seq, BW), sv["sb_tot"], carried)
        bufs["w_branch"] = _adamw_layer("adamw_w_branch", recv[0], w_branch, m_w_branch, v_w_branch, l,
                                        bufs["w_branch"])
        bufs["w_out"] = _adamw_layer("adamw_w_out", recv[1], w_out, m_w_out, v_w_out, l, bufs["w_out"])
        bufs["w_ffn_in"] = _adamw_layer("adamw_w_ffn_in", _unpad_ffn_in(recv[2]), w_ffn_in, m_w_ffn_in,
                                        v_w_ffn_in, l, bufs["w_ffn_in"])
        bufs["w_ffn_out"] = _adamw_layer("adamw_w_ffn_out", recv[3][:, :FF_HALF], w_ffn_out, m_w_ffn_out,
                                         v_w_ffn_out, l, bufs["w_ffn_out"])
        if g_in_pending is not None:
            bufs["w_in"] = _adamw_layer("adamw_w_in", recv[4], w_in, m_w_in, v_w_in, l + 1, bufs["w_in"])
        d_pool, g_pw, g_ps = _pool_bwd(proj3, dbrs[2].reshape(nb, seq, BW), pool_w[l], pool_scale[l].reshape(1, BW))
        dproj = jnp.concatenate([d_gm] + [a.reshape(t, BW) for a in d_sb] + [d_pool.reshape(t, BW)] + list(dls),
                                axis=1)
        g_in_pending = _mm_colblocked_tn("proj_wgrad", sv["h"], dproj)
        if l > 0:
            dh = _mm_colblocked_nt("proj_dgrad", dproj, gw["w_in"], F32)
        else:
            dh, (r_in,) = _mm_colblocked_nt("proj_dgrad", dproj, gw["w_in"], F32, [g_in_pending])
        dx, dsh1, dsc1, dg1 = _norm_mod_bwd(sv["x_in"], dh.reshape(nb, seq, D), dx_mid,
                                            rms_g1[l].reshape(1, D), sc1)

        dmod[l] = jnp.concatenate([dsh1, dsc1, dgt1, dsh2, dsc2, dgt2], axis=-1)
        small_parts["rms_g1"][l] = jnp.sum(dg1, axis=0)
        small_parts["rms_g2"][l] = jnp.sum(dg2, axis=0)
        small_parts["gm_ln_g"][l] = g_lg
        small_parts["gm_ln_b"][l] = g_lb
        small_parts["gm_w_spatial"][l] = g_ws
        small_parts["gm_b_spatial"][l] = g_bs[:, :, 0]
        small_parts["pool_w"][l] = g_pw
        small_parts["pool_scale"][l] = g_ps

    bufs["w_in"] = _adamw_layer("adamw_w_in", r_in, w_in, m_w_in, v_w_in, 0, bufs["w_in"])

    dmod_mine = jnp.stack(dmod).reshape(nl, nb, NMOD * D)
    names = list(small_parts)
    stacked = [jnp.stack(small_parts[k]) for k in names]
    gathered_small = _exchange(stacked + [dfinal_part, dmod_mine], "gather_small", False)
    dmod_all = jnp.transpose(gathered_small[-1], (1, 0, 2, 3)).reshape(nl, ntot, NMOD * D)
    dfinal_all = gathered_small[-2].reshape(ntot, D)

    results = {}
    weights = dict(rms_g1=(rms_g1, m_rms_g1, v_rms_g1), rms_g2=(rms_g2, m_rms_g2, v_rms_g2),
                   gm_ln_g=(gm_ln_g, m_gm_ln_g, v_gm_ln_g), gm_ln_b=(gm_ln_b, m_gm_ln_b, v_gm_ln_b),
                   gm_w_spatial=(gm_w_spatial, m_gm_w_spatial, v_gm_w_spatial),
                   gm_b_spatial=(gm_b_spatial, m_gm_b_spatial, v_gm_b_spatial),
                   pool_w=(pool_w, m_pool_w, v_pool_w), pool_scale=(pool_scale, m_pool_scale, v_pool_scale))
    for k, parts in zip(names, gathered_small[:len(names)]):
        w, m, v = weights[k]
        results[k] = _adamw_reduce("adamw_" + k, parts.reshape((NDEV,) + w.shape), w, m, v)
    results["final_g"] = _adamw_reduce("adamw_final_g", dfinal_all, final_g, m_final_g, v_final_g)
    results["b_ada"] = _adamw_reduce("adamw_b_ada", jnp.transpose(dmod_all, (1, 0, 2)), b_ada, m_b_ada, v_b_ada)
    dmod_blk = lax.dynamic_slice_in_dim(dmod_all, me * 768, 768, axis=2)
    g_w_ada = _ada_bwd(c_all, dmod_blk)
    results["w_ada"] = _adamw_reduce("adamw_w_ada", g_w_ada[None], w_ada, m_w_ada, v_w_ada)
    stacked_w = dict(w_in=w_in, w_branch=w_branch, w_out=w_out, w_ffn_in=w_ffn_in, w_ffn_out=w_ffn_out)
    for name in big_names:
        results[name] = tuple(b.reshape(stacked_w[name].shape) for b in bufs[name])

    order = ["rms_g1", "rms_g2", "w_ada", "b_ada", "w_in", "gm_ln_g", "gm_ln_b", "gm_w_spatial", "gm_b_spatial",
             "pool_w", "pool_scale", "w_branch", "w_out", "w_ffn_in", "w_ffn_out", "final_g"]
    out = [loss, dx]
    for i in range(4):
        out.extend(results[k][i] for k in order)
    return tuple(out)
```

```python
import functools
import math

import jax
import jax.numpy as jnp
from jax import lax
from jax.experimental import pallas as pl
from jax.experimental.pallas import tpu as pltpu

F32 = jnp.float32
BF16 = jnp.bfloat16
MESH = pl.DeviceIdType.MESH

D = 1024
BW = 512
NB = 3
CH = 128
NG = 4
HD = 64
POOL_WINDOWS = (2, 4, 8, 16)
DFF = 2816
NMOD = 6
EPS = 1e-6
IN_COLS = 6 * D
NDEV = 8
FF_IN_SHARD = 2 * DFF // NDEV
FF_HALF = FF_IN_SHARD // 2
FF_HALF_PAD = 384
FF_IN_PAD = 2 * FF_HALF_PAD
FFP = NDEV // 2 * FF_IN_PAD

ADAM_LR = 0.001
ADAM_B1 = 0.9
ADAM_B2 = 0.999
ADAM_EPS = 1e-08
ADAM_WD = 0.01
ADAM_STEP = 10

VMEM_LIMIT = 48 * 1024 * 1024

NN = (((1,), (0,)), ((), ()))
NT = (((1,), (1,)), ((), ()))
TN = (((0,), (0,)), ((), ()))


def _cp(sem=None):
    return pltpu.CompilerParams(dimension_semantics=sem, vmem_limit_bytes=VMEM_LIMIT)


def _dot(a, b, dims=NN):
    return lax.dot_general(a, b, dims, preferred_element_type=F32)


def _my_index():
    return 4 * lax.axis_index("x") + 2 * lax.axis_index("y") + lax.axis_index("c")


def _peer(k):
    x, y, c = lax.axis_index("x"), lax.axis_index("y"), lax.axis_index("c")
    px = 1 - x if k & 4 else x
    py = 1 - y if k & 2 else y
    pc = 1 - c if k & 1 else c
    return (px, py, pc), 4 * px + 2 * py + pc


def _exchange(xs, name, all_to_all):
    n = len(xs)

    def body(*refs):
        _exchange_start(refs[:n], refs[n:2 * n], refs[2 * n:], all_to_all)
        _exchange_finish(refs[:n], refs[n:2 * n], refs[2 * n:], all_to_all)

    return pl.pallas_call(
        body, name=name, out_shape=_exchange_out_shape(xs, all_to_all),
        in_specs=[_HBM] * n, out_specs=[_HBM] * n, scratch_shapes=_exchange_sems(n),
    )(*xs)


_HBM = pl.BlockSpec(memory_space=pl.ANY)


def _exchange_out_shape(xs, all_to_all):
    if all_to_all:
        return [jax.ShapeDtypeStruct(x.shape, x.dtype) for x in xs]
    return [jax.ShapeDtypeStruct((NDEV,) + x.shape, x.dtype) for x in xs]


def _exchange_sems(n):
    return [pltpu.SemaphoreType.DMA((n * 7,)), pltpu.SemaphoreType.DMA((n * 7,)), pltpu.SemaphoreType.DMA((n,))]


def _exchange_copies(ins, outs, sems, all_to_all):
    send_sems, recv_sems, local_sems = sems
    me = _my_index()
    local, sends, recvs = [], [], []
    for a in range(len(ins)):
        src = ins[a].at[me] if all_to_all else ins[a]
        local.append(pltpu.make_async_copy(src, outs[a].at[me], local_sems.at[a]))
    for k in range(1, NDEV):
        dev, idx = _peer(k)
        for a in range(len(ins)):
            src = ins[a].at[idx] if all_to_all else ins[a]
            sem = dict(send_sem=send_sems.at[a * 7 + k - 1], recv_sem=recv_sems.at[a * 7 + k - 1],
                       device_id=dev, device_id_type=MESH)
            sends.append(pltpu.make_async_remote_copy(src_ref=src, dst_ref=outs[a].at[me], **sem))
            recvs.append(pltpu.make_async_remote_copy(src_ref=src, dst_ref=outs[a].at[idx], **sem))
    return local, sends, recvs


def _exchange_start(ins, outs, sems, all_to_all):
    local, sends, _ = _exchange_copies(ins, outs, sems, all_to_all)
    for cp in local + sends:
        cp.start()


def _exchange_finish(ins, outs, sems, all_to_all):
    local, sends, recvs = _exchange_copies(ins, outs, sems, all_to_all)
    for cp in sends:
        cp.wait_send()
    for cp in recvs:
        cp.wait_recv()
    for cp in local:
        cp.wait()


def _host_exchange(body, n_in, n_out, grid, xs, all_to_all):
    n = len(xs)
    if n == 0:
        return body, [], [], [], []

    def hosted(*refs):
        ins, ex_ins = refs[:n_in], refs[n_in:n_in + n]
        outs, ex_outs = refs[n_in + n:n_in + n + n_out], refs[n_in + n + n_out:n_in + 2 * n + n_out]
        scratch = refs[n_in + 2 * n + n_out:]
        own, sems = scratch[:len(scratch) - 3], scratch[len(scratch) - 3:]
        first = functools.reduce(jnp.logical_and, [pl.program_id(a) == 0 for a in range(len(grid))])
        last = functools.reduce(jnp.logical_and, [pl.program_id(a) == grid[a] - 1 for a in range(len(grid))])

        @pl.when(first)
        def _():
            _exchange_start(ex_ins, ex_outs, sems, all_to_all)

        body(*ins, *outs, *own)

        @pl.when(last)
        def _():
            _exchange_finish(ex_ins, ex_outs, sems, all_to_all)

    return hosted, [_HBM] * n, [_HBM] * n, _exchange_out_shape(xs, all_to_all), _exchange_sems(n)


def _mm(name, a, b, grid, a_spec, b_spec, o_spec, out_sds, dims, acc_shape, carried=(), all_to_all=True):
    nk = grid[2]

    if nk == 1:
        def body(a_ref, b_ref, o_ref):
            o_ref[...] = _dot(a_ref[...].astype(BF16), b_ref[...].astype(BF16), dims).astype(o_ref.dtype)
        scratch = []
    else:
        def body(a_ref, b_ref, o_ref, acc_ref):
            k = pl.program_id(2)

            @pl.when(k == 0)
            def _():
                acc_ref[...] = jnp.zeros_like(acc_ref)

            acc_ref[...] += _dot(a_ref[...].astype(BF16), b_ref[...].astype(BF16), dims)

            @pl.when(k == nk - 1)
            def _():
                o_ref[...] = acc_ref[...].astype(o_ref.dtype)
        scratch = [pltpu.VMEM(acc_shape, F32)]

    body, ex_in, ex_out, ex_shape, ex_sems = _host_exchange(body, 2, 1, grid, carried, all_to_all)
    outs = pl.pallas_call(
        body, name=name, grid=grid, in_specs=[a_spec, b_spec] + ex_in, out_specs=[o_spec] + ex_out,
        out_shape=[out_sds] + ex_shape,
        scratch_shapes=scratch + ex_sems,
        compiler_params=_cp(("arbitrary",) * 3 if carried else ("parallel", "parallel", "arbitrary")),
    )(a, b, *carried)
    return (outs[0], outs[1:]) if carried else outs[0]


def _row_tile(t, want):
    tm = min(t, want)
    assert t % tm == 0
    return tm


def _mm_colblocked(name, a, wg, out_dtype, gather=()):
    t = a.shape[0]
    tm = _row_tile(t, 1024)
    return _mm(name, a, wg, (t // tm, NDEV, 1),
               pl.BlockSpec((tm, D), lambda i, j, k: (i, 0)),
               pl.BlockSpec((None, D, 768), lambda i, j, k: (j, 0, 0)),
               pl.BlockSpec((tm, 768), lambda i, j, k: (i, j)),
               jax.ShapeDtypeStruct((t, NDEV * 768), out_dtype), NN, (tm, 768), gather, False)


def _mm_colblocked_nt(name, g, wg, out_dtype, scatter=()):
    t = g.shape[0]
    tm = _row_tile(t, 1024)
    return _mm(name, g, wg, (t // tm, 1, NDEV),
               pl.BlockSpec((tm, 768), lambda i, j, k: (i, k)),
               pl.BlockSpec((None, D, 768), lambda i, j, k: (k, 0, 0)),
               pl.BlockSpec((tm, D), lambda i, j, k: (i, 0)),
               jax.ShapeDtypeStruct((t, D), out_dtype), NT, (tm, D), scatter)


_HALF = NDEV // 2


def _ffn_in_fwd(h2, wg, gather=()):
    t = h2.shape[0]
    tm = _row_tile(t, 1024)

    def body(a_ref, wg_ref, wu_ref, g_ref, u_ref, act_ref):
        a = a_ref[...]
        g = _dot(a, wg_ref[...])
        u = _dot(a, wu_ref[...])
        g_ref[...] = g.astype(BF16)
        u_ref[...] = u.astype(BF16)
        act_ref[...] = (g * jax.nn.sigmoid(g) * u).astype(BF16)

    tile = pl.BlockSpec((tm, 768), lambda i, j: (i, j))
    grid = (t // tm, _HALF)
    body, ex_in, ex_out, ex_shape, ex_sems = _host_exchange(body, 3, 3, grid, gather, False)
    outs = pl.pallas_call(
        body, name="ffn_in_fwd", grid=grid,
        in_specs=[pl.BlockSpec((tm, D), lambda i, j: (i, 0)),
                  pl.BlockSpec((None, D, 768), lambda i, j: (j, 0, 0)),
                  pl.BlockSpec((None, D, 768), lambda i, j: (j + _HALF, 0, 0))] + ex_in,
        out_specs=[tile] * 3 + ex_out, out_shape=[jax.ShapeDtypeStruct((t, FFP), BF16)] * 3 + ex_shape,
        scratch_shapes=ex_sems,
        compiler_params=_cp(("arbitrary", "arbitrary") if gather else ("parallel", "parallel")),
    )(h2, wg, wg, *gather)
    return outs[0], outs[1], outs[2], outs[3:]


def _ffn_out_dgrad(dfo, w, fg, fu):
    t = dfo.shape[0]
    tm = _row_tile(t, 1024)

    def body(a_ref, w_ref, g_ref, u_ref, dg_ref, du_ref):
        d = _dot(a_ref[...], w_ref[...], NT)
        g = g_ref[...].astype(F32)
        s = jax.nn.sigmoid(g)
        dg_ref[...] = (d * u_ref[...].astype(F32) * (s * (1.0 + g * (1.0 - s)))).astype(BF16)
        du_ref[...] = (d * (g * s)).astype(BF16)

    tile = pl.BlockSpec((tm, 768), lambda i, j: (i, j))
    return pl.pallas_call(
        body, name="ffn_out_dgrad", grid=(t // tm, _HALF),
        in_specs=[pl.BlockSpec((tm, D), lambda i, j: (i, 0)),
                  pl.BlockSpec((768, D), lambda i, j: (j, 0)), tile, tile],
        out_specs=[tile] * 2, out_shape=[jax.ShapeDtypeStruct((t, FFP), BF16)] * 2,
        compiler_params=_cp(("parallel", "parallel")),
    )(dfo, w, fg, fu)


def _ffn_in_dgrad(dg, du, wg):
    t = dg.shape[0]
    tm = _row_tile(t, 1024)

    def body(g_ref, u_ref, w_ref, o_ref, acc_ref):
        k = pl.program_id(1)

        @pl.when(k == 0)
        def _():
            acc_ref[...] = jnp.zeros_like(acc_ref)

        @pl.when(k < _HALF)
        def _():
            acc_ref[...] += _dot(g_ref[...], w_ref[...], NT)

        @pl.when(k >= _HALF)
        def _():
            acc_ref[...] += _dot(u_ref[...], w_ref[...], NT)

        @pl.when(k == NDEV - 1)
        def _():
            o_ref[...] = acc_ref[...]

    return pl.pallas_call(
        body, name="ffn_in_dgrad", grid=(t // tm, NDEV),
        in_specs=[pl.BlockSpec((tm, 768), lambda i, k: (i, jnp.minimum(k, _HALF - 1))),
                  pl.BlockSpec((tm, 768), lambda i, k: (i, jnp.maximum(k - _HALF, 0))),
                  pl.BlockSpec((None, D, 768), lambda i, k: (k, 0, 0))],
        out_specs=pl.BlockSpec((tm, D), lambda i, k: (i, 0)),
        out_shape=jax.ShapeDtypeStruct((t, D), F32),
        scratch_shapes=[pltpu.VMEM((tm, D), F32)],
        compiler_params=_cp(("parallel", "arbitrary")),
    )(dg, du, wg)


def _ffn_in_wgrad(h2, dg, du):
    t = h2.shape[0]
    tk = _row_tile(t, 1024)
    nk = t // tk

    def body(a_ref, g_ref, u_ref, o_ref, acc_ref):
        j, k = pl.program_id(0), pl.program_id(1)

        @pl.when(k == 0)
        def _():
            acc_ref[...] = jnp.zeros_like(acc_ref)

        @pl.when(j < _HALF)
        def _():
            acc_ref[...] += _dot(a_ref[...], g_ref[...], TN)

        @pl.when(j >= _HALF)
        def _():
            acc_ref[...] += _dot(a_ref[...], u_ref[...], TN)

        @pl.when(k == nk - 1)
        def _():
            o_ref[...] = acc_ref[...].astype(BF16)

    return pl.pallas_call(
        body, name="ffn_in_wgrad", grid=(NDEV, nk),
        in_specs=[pl.BlockSpec((tk, D), lambda j, k: (k, 0)),
                  pl.BlockSpec((tk, 768), lambda j, k: (jnp.where(j < _HALF, k, 0), jnp.minimum(j, _HALF - 1))),
                  pl.BlockSpec((tk, 768), lambda j, k: (jnp.where(j < _HALF, 0, k), jnp.maximum(j - _HALF, 0)))],
        out_specs=pl.BlockSpec((None, D, 768), lambda j, k: (j, 0, 0)),
        out_shape=jax.ShapeDtypeStruct((NDEV, D, 768), BF16),
        scratch_shapes=[pltpu.VMEM((D, 768), F32)],
        compiler_params=_cp(("parallel", "arbitrary")),
    )(h2, dg, du)


def _mm_colblocked_tn(name, a, g):
    t = a.shape[0]
    tk = _row_tile(t, 1024)
    return _mm(name, a, g, (1, NDEV, t // tk),
               pl.BlockSpec((tk, D), lambda i, j, k: (k, 0)),
               pl.BlockSpec((tk, 768), lambda i, j, k: (k, j)),
               pl.BlockSpec((None, D, 768), lambda i, j, k: (j, 0, 0)),
               jax.ShapeDtypeStruct((NDEV, D, 768), BF16), TN, (D, 768))


def _mm_nt(name, a, w, out_dtype, a_col=0, w_lead=None):
    t = a.shape[0]
    if w_lead is None:
        kdim, n = w.shape
        b_spec = pl.BlockSpec((min(kdim, 1024), n), lambda i, j, k: (j, 0))
    else:
        _, kdim, n = w.shape
        b_spec = pl.BlockSpec((None, min(kdim, 1024), n), lambda i, j, k: (w_lead, j, 0))
    tn = min(kdim, 1024)
    tm = _row_tile(t, 1024)
    return _mm(name, a, w, (t // tm, kdim // tn, 1),
               pl.BlockSpec((tm, n), lambda i, j, k: (i, a_col)),
               b_spec,
               pl.BlockSpec((tm, tn), lambda i, j, k: (i, j)),
               jax.ShapeDtypeStruct((t, kdim), out_dtype), NT, (tm, tn))


def _mm_tn(name, a, g, out_dtype=BF16):
    t, kdim = a.shape
    n = g.shape[1]
    tk = _row_tile(t, 1024)
    tm = min(kdim, 1024)
    tn = min(n, 1024)
    return _mm(name, a, g, (kdim // tm, n // tn, t // tk),
               pl.BlockSpec((tk, tm), lambda i, j, k: (k, i)),
               pl.BlockSpec((tk, tn), lambda i, j, k: (k, j)),
               pl.BlockSpec((tm, tn), lambda i, j, k: (i, j)),
               jax.ShapeDtypeStruct((kdim, n), out_dtype), TN, (tm, tn))


def _mm_residual(name, a, w, x, gt, seq):
    t, kdim = a.shape
    tm = _row_tile(seq, 1024)
    tn = D
    tk = min(kdim, 1024)
    nk = kdim // tk
    per = seq // tm

    def body(a_ref, w_ref, x_ref, gt_ref, xo_ref, y_ref, acc_ref):
        k = pl.program_id(2)

        @pl.when(k == 0)
        def _():
            acc_ref[...] = jnp.zeros_like(acc_ref)

        acc_ref[...] += _dot(a_ref[...], w_ref[...])

        @pl.when(k == nk - 1)
        def _():
            y = acc_ref[...]
            xo_ref[...] = x_ref[...] + gt_ref[0] * y
            y_ref[...] = y.astype(BF16)

    return pl.pallas_call(
        body, name=name, grid=(t // tm, D // tn, nk),
        in_specs=[pl.BlockSpec((tm, tk), lambda i, j, k: (i, k)),
                  pl.BlockSpec((tk, tn), lambda i, j, k: (k, j)),
                  pl.BlockSpec((tm, tn), lambda i, j, k: (i, j)),
                  pl.BlockSpec((1, 1, tn), lambda i, j, k: (i // per, 0, j))],
        out_specs=[pl.BlockSpec((tm, tn), lambda i, j, k: (i, j)),
                   pl.BlockSpec((tm, tn), lambda i, j, k: (i, j))],
        out_shape=[jax.ShapeDtypeStruct((t, D), F32), jax.ShapeDtypeStruct((t, D), BF16)],
        scratch_shapes=[pltpu.VMEM((tm, tn), F32)],
        compiler_params=_cp(("parallel", "parallel", "arbitrary")),
    )(a, w, x, gt)


def _ada_fwd(c_all, w_ada, b_blk):
    nl = w_ada.shape[0]
    nb = c_all.shape[0]

    def body(c_ref, w_ref, b_ref, o_ref):
        c = c_ref[...]
        ca = (c * jax.nn.sigmoid(c)).astype(BF16)
        o_ref[...] = _dot(ca, w_ref[...].astype(BF16)) + b_ref[...]

    return pl.pallas_call(
        body, name="ada_fwd", grid=(nl,),
        in_specs=[pl.BlockSpec((nb, D), lambda l: (0, 0)),
                  pl.BlockSpec((None, D, 768), lambda l: (l, 0, 0)),
                  pl.BlockSpec((None, 1, 768), lambda l: (l, 0, 0))],
        out_specs=pl.BlockSpec((None, nb, 768), lambda l: (l, 0, 0)),
        out_shape=jax.ShapeDtypeStruct((nl, nb, 768), F32),
        compiler_params=_cp(("parallel",)),
    )(c_all, w_ada, b_blk)


def _ada_bwd(c_all, dmod_blk):
    nl = dmod_blk.shape[0]
    nb = c_all.shape[0]

    def body(c_ref, d_ref, o_ref):
        c = c_ref[...]
        ca = (c * jax.nn.sigmoid(c)).astype(BF16)
        o_ref[...] = _dot(ca, d_ref[...].astype(BF16), TN)

    return pl.pallas_call(
        body, name="ada_bwd", grid=(nl,),
        in_specs=[pl.BlockSpec((nb, D), lambda l: (0, 0)),
                  pl.BlockSpec((None, nb, 768), lambda l: (l, 0, 0))],
        out_specs=pl.BlockSpec((None, D, 768), lambda l: (l, 0, 0)),
        out_shape=jax.ShapeDtypeStruct((nl, D, 768), F32),
        compiler_params=_cp(("parallel",)),
    )(c_all, dmod_blk)


def _seq_tile(seq):
    return _row_tile(seq, 512)


def _norm_mod_fwd(x, g, sc, sh):
    nb, seq, _ = x.shape
    ts = _seq_tile(seq)

    def body(x_ref, g_ref, sc_ref, sh_ref, h_ref):
        xv = x_ref[0]
        r = lax.rsqrt(jnp.mean(xv * xv, axis=-1, keepdims=True) + EPS)
        h_ref[0] = ((xv * r) * g_ref[...] * (1.0 + sc_ref[0]) + sh_ref[0]).astype(BF16)

    return pl.pallas_call(
        body, name="norm_mod_fwd", grid=(nb, seq // ts),
        in_specs=[pl.BlockSpec((1, ts, D), lambda b, s: (b, s, 0)),
                  pl.BlockSpec((1, D), lambda b, s: (0, 0)),
                  pl.BlockSpec((1, 1, D), lambda b, s: (b, 0, 0)),
                  pl.BlockSpec((1, 1, D), lambda b, s: (b, 0, 0))],
        out_specs=pl.BlockSpec((1, ts, D), lambda b, s: (b, s, 0)),
        out_shape=jax.ShapeDtypeStruct((nb, seq, D), BF16),
        compiler_params=_cp(("parallel", "parallel")),
    )(x, g, sc, sh)


def _norm_mod_bwd(x, dh, dres, g, sc):
    nb, seq, _ = x.shape
    ts = _seq_tile(seq)

    def body(x_ref, dh_ref, dres_ref, g_ref, sc_ref, dx_ref, dsh_ref, dsc_ref, dg_ref):
        @pl.when(pl.program_id(1) == 0)
        def _():
            dsh_ref[...] = jnp.zeros_like(dsh_ref)
            dsc_ref[...] = jnp.zeros_like(dsc_ref)
            dg_ref[...] = jnp.zeros_like(dg_ref)

        xv = x_ref[0]
        dh = dh_ref[0]
        gv = g_ref[...]
        onesc = 1.0 + sc_ref[0]
        r = lax.rsqrt(jnp.mean(xv * xv, axis=-1, keepdims=True) + EPS)
        xh = xv * r
        dsh_ref[0] += jnp.sum(dh, axis=0, keepdims=True)
        dsc_ref[0] += jnp.sum(dh * (xh * gv), axis=0, keepdims=True)
        dg_ref[0] += jnp.sum(dh * onesc * xh, axis=0, keepdims=True)
        dxh = dh * (gv * onesc)
        dx = r * (dxh - xh * jnp.mean(dxh * xh, axis=-1, keepdims=True))
        dx_ref[0] = dres_ref[0] + dx

    vec = jax.ShapeDtypeStruct((nb, 1, D), F32)
    vspec = pl.BlockSpec((1, 1, D), lambda b, s: (b, 0, 0))
    tile = pl.BlockSpec((1, ts, D), lambda b, s: (b, s, 0))
    return pl.pallas_call(
        body, name="norm_mod_bwd", grid=(nb, seq // ts),
        in_specs=[tile, tile, tile, pl.BlockSpec((1, D), lambda b, s: (0, 0)), vspec],
        out_specs=[tile, vspec, vspec, vspec],
        out_shape=[jax.ShapeDtypeStruct((nb, seq, D), F32), vec, vec, vec],
        compiler_params=_cp(("parallel", "arbitrary")),
    )(x, dh, dres, g, sc)


def _gate_bwd(dx, y, gt):
    nb, seq, _ = dx.shape
    ts = _seq_tile(seq)

    def body(dx_ref, y_ref, gt_ref, dy_ref, dgt_ref):
        @pl.when(pl.program_id(1) == 0)
        def _():
            dgt_ref[...] = jnp.zeros_like(dgt_ref)

        d = dx_ref[0]
        dy_ref[0] = (gt_ref[0] * d).astype(BF16)
        dgt_ref[0] += jnp.sum(d * y_ref[0].astype(F32), axis=0, keepdims=True)

    vspec = pl.BlockSpec((1, 1, D), lambda b, s: (b, 0, 0))
    tile = pl.BlockSpec((1, ts, D), lambda b, s: (b, s, 0))
    return pl.pallas_call(
        body, name="gate_bwd", grid=(nb, seq // ts),
        in_specs=[tile, tile, vspec], out_specs=[tile, vspec],
        out_shape=[jax.ShapeDtypeStruct((nb, seq, D), BF16), jax.ShapeDtypeStruct((nb, 1, D), F32)],
        compiler_params=_cp(("parallel", "arbitrary")),
    )(dx, y, gt)


def _loss_head(x, tgt, g):
    nb, seq, _ = x.shape
    ts = _seq_tile(seq)

    def body(x_ref, t_ref, g_ref, dx_ref, loss_ref, dg_ref):
        @pl.when(pl.program_id(1) == 0)
        def _():
            loss_ref[...] = jnp.zeros_like(loss_ref)
            dg_ref[...] = jnp.zeros_like(dg_ref)

        xv = x_ref[0]
        gv = g_ref[...]
        r = lax.rsqrt(jnp.mean(xv * xv, axis=-1, keepdims=True) + EPS)
        xh = xv * r
        err = xh * gv - t_ref[0]
        per_tok = jnp.mean(err * err, axis=-1, keepdims=True)
        loss_ref[0] += 0.5 * jnp.sum(per_tok, axis=0, keepdims=True)
        dy = err * (1.0 / D)
        dg_ref[0] += jnp.sum(dy * xh, axis=0, keepdims=True)
        dxh = dy * gv
        dx_ref[0] = r * (dxh - xh * jnp.mean(dxh * xh, axis=-1, keepdims=True))

    tile = pl.BlockSpec((1, ts, D), lambda b, s: (b, s, 0))
    return pl.pallas_call(
        body, name="loss_head", grid=(nb, seq // ts),
        in_specs=[tile, tile, pl.BlockSpec((1, D), lambda b, s: (0, 0))],
        out_specs=[tile, pl.BlockSpec((1, 1, 128), lambda b, s: (b, 0, 0)),
                   pl.BlockSpec((1, 1, D), lambda b, s: (b, 0, 0))],
        out_shape=[jax.ShapeDtypeStruct((nb, seq, D), F32), jax.ShapeDtypeStruct((nb, 1, 128), F32),
                   jax.ShapeDtypeStruct((nb, 1, D), F32)],
        compiler_params=_cp(("parallel", "arbitrary")),
    )(x, tgt, g)


_GELU_C = math.sqrt(2.0 / math.pi)


def _gelu(x):
    return 0.5 * x * (1.0 + jnp.tanh(_GELU_C * (x + 0.044715 * (x * x * x))))


def _gelu_and_grad(x):
    t = jnp.tanh(_GELU_C * (x + 0.044715 * (x * x * x)))
    y = 0.5 * x * (1.0 + t)
    dy = 0.5 * (1.0 + t) + 0.5 * x * (1.0 - t * t) * (_GELU_C * (1.0 + 3.0 * 0.044715 * (x * x)))
    return y, dy


def _tril_mask():
    row = lax.broadcasted_iota(jnp.int32, (CH, CH), 0)
    col = lax.broadcasted_iota(jnp.int32, (CH, CH), 1)
    return row >= col


def _gmlp_fwd(proj, ln_g, ln_b, ws, bst):
    t = proj.shape[0]
    tm = _row_tile(t, 512)

    def body(u_ref, v_ref, lg_ref, lb_ref, ws_ref, bst_ref, o_ref):
        tril = _tril_mask()
        wm = [jnp.where(tril, ws_ref[g], 0.0).astype(BF16) for g in range(NG)]
        for ch in range(tm // CH):
            rows = slice(ch * CH, (ch + 1) * CH)
            u = _gelu(u_ref[rows, :].astype(F32))
            v = _gelu(v_ref[rows, :].astype(F32))
            mu = jnp.mean(v, axis=-1, keepdims=True)
            xc = v - mu
            rstd = lax.rsqrt(jnp.mean(xc * xc, axis=-1, keepdims=True) + EPS)
            vn = ((xc * rstd) * lg_ref[...] + lb_ref[...]).astype(BF16)
            for g in range(NG):
                cols = slice(g * CH, (g + 1) * CH)
                s = _dot(wm[g], vn[:, cols]) + bst_ref[:, g:g + 1]
                o_ref[rows, cols] = (u[:, cols] * s).astype(BF16)

    return pl.pallas_call(
        body, name="gmlp_fwd", grid=(t // tm,),
        in_specs=[pl.BlockSpec((tm, BW), lambda i: (i, 0)),
                  pl.BlockSpec((tm, BW), lambda i: (i, 1)),
                  pl.BlockSpec((1, BW), lambda i: (0, 0)),
                  pl.BlockSpec((1, BW), lambda i: (0, 0)),
                  pl.BlockSpec((NG, CH, CH), lambda i: (0, 0, 0)),
                  pl.BlockSpec((CH, NG), lambda i: (0, 0))],
        out_specs=pl.BlockSpec((tm, BW), lambda i: (i, 0)),
        out_shape=jax.ShapeDtypeStruct((t, BW), BF16),
        compiler_params=_cp(("parallel",)),
    )(proj, proj, ln_g, ln_b, ws, bst)


def _gmlp_bwd(proj, dout, ln_g, ln_b, ws, bst):
    t = proj.shape[0]
    tm = _row_tile(t, 512)

    def body(u_ref, v_ref, do_ref, lg_ref, lb_ref, ws_ref, bst_ref, dp_ref, gws_ref, gbs_ref, glg_ref, glb_ref):
        @pl.when(pl.program_id(0) == 0)
        def _():
            gws_ref[...] = jnp.zeros_like(gws_ref)
            gbs_ref[...] = jnp.zeros_like(gbs_ref)
            glg_ref[...] = jnp.zeros_like(glg_ref)
            glb_ref[...] = jnp.zeros_like(glb_ref)

        tril = _tril_mask()
        wm = [jnp.where(tril, ws_ref[g], 0.0).astype(BF16) for g in range(NG)]
        ones = jnp.ones((CH, CH), BF16)
        lg = lg_ref[...]
        for ch in range(tm // CH):
            rows = slice(ch * CH, (ch + 1) * CH)
            u, du_fac = _gelu_and_grad(u_ref[rows, :].astype(F32))
            v, dv_fac = _gelu_and_grad(v_ref[rows, :].astype(F32))
            do = do_ref[rows, :].astype(F32)
            mu = jnp.mean(v, axis=-1, keepdims=True)
            xc = v - mu
            rstd = lax.rsqrt(jnp.mean(xc * xc, axis=-1, keepdims=True) + EPS)
            xh = xc * rstd
            vn = (xh * lg + lb_ref[...]).astype(BF16)
            dvn_parts = []
            for g in range(NG):
                cols = slice(g * CH, (g + 1) * CH)
                s = _dot(wm[g], vn[:, cols]) + bst_ref[:, g:g + 1]
                dp_ref[rows, cols] = (do[:, cols] * s * du_fac[:, cols]).astype(BF16)
                ds = (do[:, cols] * u[:, cols]).astype(BF16)
                gws_ref[g] += jnp.where(tril, _dot(ds, vn[:, cols], NT), 0.0)
                gbs_ref[g] += _dot(ds, ones)
                dvn_parts.append(_dot(wm[g], ds, TN))
            dvn = jnp.concatenate(dvn_parts, axis=1)
            glb_ref[...] += jnp.sum(dvn, axis=0, keepdims=True)
            glg_ref[...] += jnp.sum(dvn * xh, axis=0, keepdims=True)
            dxh = dvn * lg
            dv = rstd * (dxh - jnp.mean(dxh, axis=-1, keepdims=True)
                         - xh * jnp.mean(dxh * xh, axis=-1, keepdims=True))
            dp_ref[rows, BW:2 * BW] = (dv * dv_fac).astype(BF16)

    small = pl.BlockSpec((NG, CH, CH), lambda i: (0, 0, 0))
    vec = pl.BlockSpec((1, BW), lambda i: (0, 0))
    return pl.pallas_call(
        body, name="gmlp_bwd", grid=(t // tm,),
        in_specs=[pl.BlockSpec((tm, BW), lambda i: (i, 0)),
                  pl.BlockSpec((tm, BW), lambda i: (i, 1)),
                  pl.BlockSpec((tm, BW), lambda i: (i, 0)),
                  vec, vec, small, pl.BlockSpec((CH, NG), lambda i: (0, 0))],
        out_specs=[pl.BlockSpec((tm, 2 * BW), lambda i: (i, 0)), small, small, vec, vec],
        out_shape=[jax.ShapeDtypeStruct((t, 2 * BW), BF16),
                   jax.ShapeDtypeStruct((NG, CH, CH), F32), jax.ShapeDtypeStruct((NG, CH, CH), F32),
                   jax.ShapeDtypeStruct((1, BW), F32), jax.ShapeDtypeStruct((1, BW), F32)],
        compiler_params=_cp(("arbitrary",)),
    )(proj, proj, dout, ln_g, ln_b, ws, bst)


def _pool_bands():
    row = lax.broadcasted_iota(jnp.int32, (CH, CH), 0)
    col = lax.broadcasted_iota(jnp.int32, (CH, CH), 1)
    cur, prev = [], []
    for w in POOL_WINDOWS:
        cur.append(jnp.where((row >= col) & (row - col < w), 1.0, 0.0).astype(BF16))
        prev.append(jnp.where(row + CH - col < w, 1.0, 0.0).astype(BF16))
    return cur, prev


def _pool_inv_count(r0, w):
    pos = r0 + lax.broadcasted_iota(jnp.int32, (CH, 1), 0)
    return 1.0 / jnp.minimum(pos + 1, w).astype(F32)


def _pool_diff(x_ref, r0, rp, has_prev, cur, prev, g):
    cols = slice(g * CH, (g + 1) * CH)
    xc = x_ref[pl.ds(r0, CH), cols]
    xp = x_ref[pl.ds(rp, CH), cols]
    ws = _dot(cur[g], xc) + has_prev * _dot(prev[g], xp)
    return ws * _pool_inv_count(r0, POOL_WINDOWS[g]) - xc.astype(F32)


def _pool_fwd(proj3, pw, pscale):
    nb, seq, _ = proj3.shape
    nch = seq // CH

    def body(x_ref, pw_ref, ps_ref, o_ref):
        cur, prev = _pool_bands()
        pwb = [pw_ref[g].astype(BF16) for g in range(NG)]

        def chunk(ch, carry):
            r0 = pl.multiple_of(ch * CH, CH)
            rp = pl.multiple_of(jnp.maximum(ch - 1, 0) * CH, CH)
            has_prev = jnp.where(ch > 0, 1.0, 0.0)
            for g in range(NG):
                cols = slice(g * CH, (g + 1) * CH)
                d = _pool_diff(x_ref, r0, rp, has_prev, cur, prev, g)
                y = _dot(d.astype(BF16), pwb[g]) * ps_ref[:, cols]
                o_ref[pl.ds(r0, CH), cols] = y.astype(BF16)
            return carry

        lax.fori_loop(0, nch, chunk, 0)

    return pl.pallas_call(
        body, name="pool_fwd", grid=(nb,),
        in_specs=[pl.BlockSpec((None, seq, BW), lambda b: (b, 0, 5)),
                  pl.BlockSpec((NG, CH, CH), lambda b: (0, 0, 0)),
                  pl.BlockSpec((1, BW), lambda b: (0, 0))],
        out_specs=pl.BlockSpec((None, seq, BW), lambda b: (b, 0, 0)),
        out_shape=jax.ShapeDtypeStruct((nb, seq, BW), BF16),
        compiler_params=_cp(("parallel",)),
    )(proj3, pw, pscale)


def _pool_bwd(proj3, dout3, pw, pscale):
    nb, seq, _ = proj3.shape
    nch = seq // CH

    def body(x_ref, do_ref, pw_ref, ps_ref, dx_ref, gpw_ref, gps_ref, e_ref):
        @pl.when(pl.program_id(0) == 0)
        def _():
            gpw_ref[...] = jnp.zeros_like(gpw_ref)
            gps_ref[...] = jnp.zeros_like(gps_ref)

        cur, prev = _pool_bands()
        pwb = [pw_ref[g].astype(BF16) for g in range(NG)]

        def first(ch, carry):
            r0 = pl.multiple_of(ch * CH, CH)
            rp = pl.multiple_of(jnp.maximum(ch - 1, 0) * CH, CH)
            has_prev = jnp.where(ch > 0, 1.0, 0.0)
            for g in range(NG):
                cols = slice(g * CH, (g + 1) * CH)
                d = _pool_diff(x_ref, r0, rp, has_prev, cur, prev, g).astype(BF16)
                do = do_ref[pl.ds(r0, CH), cols].astype(F32)
                ypre = _dot(d, pwb[g])
                gps_ref[:, cols] += jnp.sum(do * ypre, axis=0, keepdims=True)
                dyp = (do * ps_ref[:, cols]).astype(BF16)
                gpw_ref[g] += _dot(d, dyp, TN)
                e_ref[pl.ds(r0, CH), cols] = _dot(dyp, pwb[g], NT)
            return carry

        lax.fori_loop(0, nch, first, 0)

        def second(ch, carry):
            r0 = pl.multiple_of(ch * CH, CH)
            rn = pl.multiple_of(jnp.minimum(ch + 1, nch - 1) * CH, CH)
            has_next = jnp.where(ch < nch - 1, 1.0, 0.0)
            for g in range(NG):
                cols = slice(g * CH, (g + 1) * CH)
                w = POOL_WINDOWS[g]
                dd = e_ref[pl.ds(r0, CH), cols]
                ec = (dd * _pool_inv_count(r0, w)).astype(BF16)
                en = (e_ref[pl.ds(rn, CH), cols] * _pool_inv_count(rn, w)).astype(BF16)
                dx = _dot(cur[g], ec, TN) + has_next * _dot(prev[g], en, TN) - dd
                dx_ref[pl.ds(r0, CH), cols] = dx.astype(BF16)
            return carry

        lax.fori_loop(0, nch, second, 0)

    small = pl.BlockSpec((NG, CH, CH), lambda b: (0, 0, 0))
    vec = pl.BlockSpec((1, BW), lambda b: (0, 0))
    return pl.pallas_call(
        body, name="pool_bwd", grid=(nb,),
        in_specs=[pl.BlockSpec((None, seq, BW), lambda b: (b, 0, 5)),
                  pl.BlockSpec((None, seq, BW), lambda b: (b, 0, 0)), small, vec],
        out_specs=[pl.BlockSpec((None, seq, BW), lambda b: (b, 0, 0)), small, vec],
        out_shape=[jax.ShapeDtypeStruct((nb, seq, BW), BF16),
                   jax.ShapeDtypeStruct((NG, CH, CH), F32), jax.ShapeDtypeStruct((1, BW), F32)],
        scratch_shapes=[pltpu.VMEM((seq, BW), F32)],
        compiler_params=_cp(("arbitrary",)),
    )(proj3, dout3, pw, pscale)


SB_BQ = 256
SB_BK = 256
SB_SCALE = HD ** -0.5


SB_EXIT = -110.0


def _sb_tile(qs, k, mask):
    z = _dot(qs, k, NT)
    lb = jnp.minimum(z, 0.0) - jnp.log(1.0 + jnp.exp(-jnp.abs(z)))
    lom = lb - z
    if mask is not None:
        lom = jnp.where(mask, lom, 0.0)
    return lb, lom


def _sb_alive(c):
    top = functools.reduce(jnp.maximum, [jnp.max(state[1]) for state in c])
    return (top > SB_EXIT).astype(jnp.int32)


def _sb_past_blocks(step, c, npast):
    def cond(s):
        return jnp.logical_and(s[0] < npast, s[1] > 0)

    def body(s):
        i, _, c = s
        c = step(pl.multiple_of((npast - 1 - i) * SB_BK, SB_BK), c, None)
        return i + 1, _sb_alive(c), c

    return lax.while_loop(cond, body, (jnp.int32(0), _sb_alive(c), c))[2]


def _sb_diag_mask(bq, d):
    row = lax.broadcasted_iota(jnp.int32, (bq, SB_BK), 0)
    col = lax.broadcasted_iota(jnp.int32, (bq, SB_BK), 1)
    return col + d * SB_BK < row


def _sb_scaled(q):
    return (q.astype(F32) * SB_SCALE).astype(BF16)


def _dot_tri(a, m):
    return _dot(a.astype(BF16), m)


def _dot_tri2(a, m):
    hi = a.astype(BF16)
    lo = (a - hi.astype(F32)).astype(BF16)
    return _dot(hi, m) + _dot(lo, m)


def _sb_fwd(proj3, gather=()):
    nb, seq, _ = proj3.shape
    bq = min(SB_BQ, seq)
    nq = seq // bq
    ndiag = bq // SB_BK

    def body(q_ref, k_ref, v_ref, o_ref):
        row = lax.broadcasted_iota(jnp.int32, (SB_BK, SB_BK), 0)
        col = lax.broadcasted_iota(jnp.int32, (SB_BK, SB_BK), 1)
        upper = jnp.where(row > col, 1.0, 0.0).astype(BF16)
        heads = [slice(hh * HD, (hh + 1) * HD) for hh in range(2)]

        def qloop(qi, carry):
            q0 = pl.multiple_of(qi * bq, bq)
            qs = [_sb_scaled(q_ref[pl.ds(q0, bq), lanes]) for lanes in heads]

            def step(k0, c, mask):
                tiles = [_sb_tile(q, k_ref[pl.ds(k0, SB_BK), lanes], mask) for lanes, q in zip(heads, qs)]
                sums = [_dot_tri(lom, upper) for _, lom in tiles]
                out = []
                for lanes, (acc, cr), (lb, lom), cs in zip(heads, c, tiles, sums):
                    a = jnp.exp(lb + (cs + cr))
                    if mask is not None:
                        a = jnp.where(mask, a, 0.0)
                    rsum = cs[:, 0:1] + lom[:, 0:1]
                    out.append((acc + _dot(a.astype(BF16), v_ref[pl.ds(k0, SB_BK), lanes]), cr + rsum))
                return tuple(out)

            c = tuple((jnp.zeros((bq, HD), F32), jnp.zeros((bq, 1), F32)) for _ in heads)
            for d in reversed(range(ndiag)):
                c = step(pl.multiple_of(q0 + d * SB_BK, SB_BK), c, _sb_diag_mask(bq, d))
            c = _sb_past_blocks(step, c, qi * ndiag)
            for lanes, (acc, _) in zip(heads, c):
                o_ref[pl.ds(q0, bq), lanes] = acc
            return carry

        lax.fori_loop(0, nq, qloop, 0)

    def spec(c0):
        return pl.BlockSpec((None, seq, 128), lambda b, p: (b, 0, c0 + p))

    grid = (nb, BW // 128)
    body, ex_in, ex_out, ex_shape, ex_sems = _host_exchange(body, 3, 1, grid, gather, False)
    outs = pl.pallas_call(
        body, name="sb_fwd", grid=grid,
        in_specs=[spec(8), spec(12), spec(16)] + ex_in,
        out_specs=[spec(0)] + ex_out,
        out_shape=[jax.ShapeDtypeStruct((nb, seq, BW), F32)] + ex_shape,
        scratch_shapes=ex_sems,
        compiler_params=_cp(("arbitrary", "arbitrary")),
    )(proj3, proj3, proj3, *gather)
    return outs[0], outs[1:]


def _sb_bwd(proj3, do3, o3, scatter=()):
    nb, seq, _ = proj3.shape
    bq = min(SB_BQ, seq)
    nq = seq // bq
    ndiag = bq // SB_BK

    def body(q_ref, k_ref, v_ref, do_ref, o_ref, dq_ref, dk_ref, dv_ref, dk_acc, dv_acc):
        row = lax.broadcasted_iota(jnp.int32, (SB_BK, SB_BK), 0)
        col = lax.broadcasted_iota(jnp.int32, (SB_BK, SB_BK), 1)
        upper = jnp.where(row > col, 1.0, 0.0).astype(BF16)
        later = jnp.where(row >= col, 1.0, 0.0).astype(BF16)
        dk_acc[...] = jnp.zeros_like(dk_acc)
        dv_acc[...] = jnp.zeros_like(dv_acc)
        heads = [slice(hh * HD, (hh + 1) * HD) for hh in range(2)]

        def qloop(qi, carry):
            q0 = pl.multiple_of(qi * bq, bq)
            qs = [_sb_scaled(q_ref[pl.ds(q0, bq), lanes]) for lanes in heads]
            dos = [do_ref[pl.ds(q0, bq), lanes] for lanes in heads]
            gtot = [jnp.sum(do.astype(F32) * o_ref[pl.ds(q0, bq), lanes], axis=1, keepdims=True)
                    for do, lanes in zip(dos, heads)]

            def step(k0, c, mask):
                ks = [k_ref[pl.ds(k0, SB_BK), lanes] for lanes in heads]
                tiles = [_sb_tile(q, k, mask) for q, k in zip(qs, ks)]
                sums = [_dot_tri(lom, upper) for _, lom in tiles]
                das = [_dot(do, v_ref[pl.ds(k0, SB_BK), lanes], NT) for do, lanes in zip(dos, heads)]
                gls, avs = [], []
                for hh, (_, cr, _) in enumerate(c):
                    a = jnp.exp(tiles[hh][0] + (sums[hh] + cr))
                    if mask is not None:
                        a = jnp.where(mask, a, 0.0)
                    ab = a.astype(BF16)
                    avs.append(ab)
                    gls.append(das[hh] * ab.astype(F32))
                tails = [_dot_tri2(gl, later) for gl in gls]
                out = []
                for hh, (dq, cr, gdone) in enumerate(c):
                    lb, lom = tiles[hh]
                    pre = gtot[hh] - gdone - tails[hh]
                    dz = gls[hh] - jnp.exp(lb) * (gls[hh] + pre)
                    if mask is not None:
                        dz = jnp.where(mask, dz, 0.0)
                    dz = dz.astype(BF16)
                    dk_acc[hh, pl.ds(k0, SB_BK), :] += _dot(dz, qs[hh], TN)
                    dv_acc[hh, pl.ds(k0, SB_BK), :] += _dot(avs[hh], dos[hh], TN)
                    rsum = sums[hh][:, 0:1] + lom[:, 0:1]
                    out.append((dq + _dot(dz, ks[hh]), cr + rsum, gdone + tails[hh][:, 0:1]))
                return tuple(out)

            c = tuple((jnp.zeros((bq, HD), F32), jnp.zeros((bq, 1), F32), jnp.zeros((bq, 1), F32))
                      for _ in heads)
            for d in reversed(range(ndiag)):
                c = step(pl.multiple_of(q0 + d * SB_BK, SB_BK), c, _sb_diag_mask(bq, d))
            c = _sb_past_blocks(step, c, qi * ndiag)
            for lanes, (dq, _, _) in zip(heads, c):
                dq_ref[pl.ds(q0, bq), lanes] = (dq * SB_SCALE).astype(BF16)
            return carry

        lax.fori_loop(0, nq, qloop, 0)
        for hh in range(2):
            lanes = slice(hh * HD, (hh + 1) * HD)
            dk_ref[:, lanes] = dk_acc[hh].astype(BF16)
            dv_ref[:, lanes] = dv_acc[hh].astype(BF16)

    def spec(c0):
        return pl.BlockSpec((None, seq, 128), lambda b, p: (b, 0, c0 + p))

    grid = (nb, BW // 128)
    body, ex_in, ex_out, ex_shape, ex_sems = _host_exchange(body, 5, 3, grid, scatter, True)
    outs = pl.pallas_call(
        body, name="sb_bwd", grid=grid,
        in_specs=[spec(8), spec(12), spec(16), spec(0), spec(0)] + ex_in,
        out_specs=[spec(0), spec(0), spec(0)] + ex_out,
        out_shape=[jax.ShapeDtypeStruct((nb, seq, BW), BF16)] * 3 + ex_shape,
        scratch_shapes=[pltpu.VMEM((2, seq, HD), F32), pltpu.VMEM((2, seq, HD), F32)] + ex_sems,
        compiler_params=_cp(("arbitrary", "arbitrary")),
    )(proj3, proj3, proj3, do3, o3, *scatter)
    return outs[:3], outs[3:]


def _merge_fwd(brs, wb, proj):
    t = proj.shape[0]
    tm = _row_tile(t, 512)
    tn = 512
    nj = D // tn

    def body(b0, b1, b2, wb_ref, l0, l1, l2, m_ref, y0, y1, y2):
        acc = None
        for br, n, lg, y_ref in ((b0, 0, l0, y0), (b1, 1, l1, y1), (b2, 2, l2, y2)):
            y = _dot(br[...].astype(BF16), wb_ref[n])
            y_ref[...] = y.astype(BF16)
            term = jax.nn.sigmoid(lg[...].astype(F32)) * y
            acc = term if acc is None else acc + term
        m_ref[...] = acc.astype(BF16)

    def lspec(n):
        return pl.BlockSpec((tm, tn), lambda i, j: (i, (3 * D + n * D) // tn + j))

    tile = pl.BlockSpec((tm, tn), lambda i, j: (i, j))
    bspec = pl.BlockSpec((tm, BW), lambda i, j: (i, 0))
    return pl.pallas_call(
        body, name="merge_fwd", grid=(t // tm, nj),
        in_specs=[bspec, bspec, bspec, pl.BlockSpec((NB, BW, tn), lambda i, j: (0, 0, j)),
                  lspec(0), lspec(1), lspec(2)],
        out_specs=[tile] * 4,
        out_shape=[jax.ShapeDtypeStruct((t, D), BF16)] * 4,
        compiler_params=_cp(("parallel", "parallel")),
    )(brs[0], brs[1], brs[2], wb, proj, proj, proj)


def _merge_bwd(dm, ys, proj):
    t = proj.shape[0]
    tm = _row_tile(t, 512)
    tn = 512

    def body(dm_ref, y0, y1, y2, l0, l1, l2, dl0, dl1, dl2, dy0, dy1, dy2):
        dmv = dm_ref[...].astype(F32)
        for y_ref, lg, dl_ref, dy_ref in ((y0, l0, dl0, dy0), (y1, l1, dl1, dy1), (y2, l2, dl2, dy2)):
            g = jax.nn.sigmoid(lg[...].astype(F32))
            dl_ref[...] = (dmv * y_ref[...].astype(F32) * g * (1.0 - g)).astype(BF16)
            dy_ref[...] = (dmv * g).astype(BF16)

    def lspec(n):
        return pl.BlockSpec((tm, tn), lambda i, j: (i, (3 * D + n * D) // tn + j))

    tile = pl.BlockSpec((tm, tn), lambda i, j: (i, j))
    return pl.pallas_call(
        body, name="merge_bwd", grid=(t // tm, D // tn),
        in_specs=[tile] * 4 + [lspec(0), lspec(1), lspec(2)],
        out_specs=[tile] * 6,
        out_shape=[jax.ShapeDtypeStruct((t, D), BF16)] * 6,
        compiler_params=_cp(("parallel", "parallel")),
    )(dm, ys[0], ys[1], ys[2], proj, proj, proj)


def _adamw_math(npart, p_ref, w_ref, m_ref, v_ref, g_ref, d_ref, mo_ref, vo_ref):
    c1 = 1.0 - ADAM_B1 ** ADAM_STEP
    c2 = 1.0 - ADAM_B2 ** ADAM_STEP
    g = p_ref[0].astype(F32)
    for p in range(1, npart):
        g = g + p_ref[p].astype(F32)
    mn = ADAM_B1 * m_ref[...] + (1.0 - ADAM_B1) * g
    vn = ADAM_B2 * v_ref[...] + (1.0 - ADAM_B2) * (g * g)
    m_hat = mn / c1
    v_hat = vn / c2
    g_ref[...] = g
    d_ref[...] = -ADAM_LR * (m_hat / (jnp.sqrt(v_hat) + ADAM_EPS) + ADAM_WD * w_ref[...])
    mo_ref[...] = mn
    vo_ref[...] = vn


def _adamw_layer(name, parts, w, m, v, layer, bufs):
    nl, cols = w.shape[0], w.shape[-1]
    rows = int(math.prod(w.shape[1:-1]))
    npart = parts.shape[0]
    tr = rows if rows <= 512 else 512
    assert rows % tr == 0
    if bufs is None:
        bufs = [lax.empty((nl, rows, cols), F32) for _ in range(4)]

    def body(p_ref, w_ref, m_ref, v_ref, b0, b1, b2, b3, g_ref, d_ref, mo_ref, vo_ref):
        _adamw_math(npart, p_ref, w_ref, m_ref, v_ref, g_ref, d_ref, mo_ref, vo_ref)

    slab = pl.BlockSpec((None, tr, cols), lambda i: (layer, i, 0))
    sds = jax.ShapeDtypeStruct((nl, rows, cols), F32)
    return pl.pallas_call(
        body, name=name, grid=(rows // tr,),
        in_specs=[pl.BlockSpec((npart, tr, cols), lambda i: (0, i, 0)), slab, slab, slab] + [_HBM] * 4,
        out_specs=[slab] * 4, out_shape=[sds] * 4,
        input_output_aliases={4: 0, 5: 1, 6: 2, 7: 3},
        compiler_params=_cp(("parallel",)),
    )(parts.reshape(npart, rows, cols), w.reshape(nl, rows, cols), m.reshape(nl, rows, cols),
      v.reshape(nl, rows, cols), *bufs)


def _adamw_reduce(name, parts, w, m, v):
    shape = w.shape
    cols = shape[-1]
    rows = int(math.prod(shape[:-1])) if len(shape) > 1 else 1
    npart = parts.shape[0]
    tr = rows if rows <= 512 else 512
    assert rows % tr == 0

    def body(p_ref, w_ref, m_ref, v_ref, g_ref, d_ref, mo_ref, vo_ref):
        _adamw_math(npart, p_ref, w_ref, m_ref, v_ref, g_ref, d_ref, mo_ref, vo_ref)

    tile = pl.BlockSpec((tr, cols), lambda i: (i, 0))
    sds = jax.ShapeDtypeStruct((rows, cols), F32)
    outs = pl.pallas_call(
        body, name=name, grid=(rows // tr,),
        in_specs=[pl.BlockSpec((npart, tr, cols), lambda i: (0, i, 0)), tile, tile, tile],
        out_specs=[tile] * 4, out_shape=[sds] * 4,
        compiler_params=_cp(("parallel",)),
    )(parts.reshape(npart, rows, cols), w.reshape(rows, cols), m.reshape(rows, cols), v.reshape(rows, cols))
    return tuple(o.reshape(shape) for o in outs)


def _pad_ffn_in(w):
    lead = w.shape[:-1]
    w = w.reshape(lead + (2, FF_HALF))
    w = jnp.pad(w, [(0, 0)] * len(lead) + [(0, 0), (0, FF_HALF_PAD - FF_HALF)])
    return w.reshape(lead + (FF_IN_PAD,))


def _unpad_ffn_in(w):
    lead = w.shape[:-1]
    return w.reshape(lead + (2, FF_HALF_PAD))[..., :FF_HALF].reshape(lead + (FF_IN_SHARD,))


def kernel(x, c, rms_g1, rms_g2, w_ada, b_ada, w_in, gm_ln_g, gm_ln_b, gm_w_spatial, gm_b_spatial, pool_w, pool_scale, w_branch, w_out, w_ffn_in, w_ffn_out, final_g, loss_target, m_rms_g1, m_rms_g2, m_w_ada, m_b_ada, m_w_in, m_gm_ln_g, m_gm_ln_b, m_gm_w_spatial, m_gm_b_spatial, m_pool_w, m_pool_scale, m_w_branch, m_w_out, m_w_ffn_in, m_w_ffn_out, m_final_g, v_rms_g1, v_rms_g2, v_w_ada, v_b_ada, v_w_in, v_gm_ln_g, v_gm_ln_b, v_gm_w_spatial, v_gm_b_spatial, v_pool_w, v_pool_scale, v_w_branch, v_w_out, v_w_ffn_in, v_w_ffn_out, v_final_g):
    nb, seq, _ = x.shape
    nl = w_in.shape[0]
    t = nb * seq
    ntot = NDEV * nb
    me = _my_index()
    assert x.shape[2] == D and w_in.shape[1:] == (D, 768) and w_ffn_in.shape[1:] == (D, FF_IN_SHARD)
    assert seq % CH == 0

    w_ffn_in_p = _pad_ffn_in(w_ffn_in).astype(BF16)
    w_ffn_out_p = jnp.pad(w_ffn_out, ((0, 0), (0, FF_HALF_PAD - FF_HALF), (0, 0))).astype(BF16)
    w_in_b = w_in.astype(BF16)
    w_branch_b = w_branch.astype(BF16)
    w_out_b = w_out.astype(BF16)
    (g_in_next,) = _exchange([w_in_b[0]], "gather_w_in0", False)

    (c_all,) = _exchange([c], "gather_c", False)
    c_all = c_all.reshape(ntot, D)
    b_blk = lax.dynamic_slice_in_dim(b_ada, me * 768, 768, axis=1).reshape(nl, 1, 768)
    mod_blk = _ada_fwd(c_all, w_ada, b_blk)
    (mod_all,) = _exchange([mod_blk], "gather_mod", False)
    mod_all = jnp.transpose(mod_all, (1, 2, 0, 3)).reshape(nl, ntot, NMOD * D)
    mod = lax.dynamic_slice_in_dim(mod_all, me * nb, nb, axis=1).reshape(nl, nb, NMOD, 1, D)

    saved = []
    gathered = []
    xc = x
    for l in range(nl):
        sh1, sc1, gt1, sh2, sc2, gt2 = [mod[l, :, i] for i in range(NMOD)]
        h = _norm_mod_fwd(xc, rms_g1[l].reshape(1, D), sc1, sh1).reshape(t, D)
        proj, (g_ffn_in_w,) = _mm_colblocked("proj_fwd", h, g_in_next, BF16, [w_ffn_in_p[l]])
        proj3 = proj.reshape(nb, seq, IN_COLS)
        br_gm = _gmlp_fwd(proj, gm_ln_g[l].reshape(1, BW), gm_ln_b[l].reshape(1, BW),
                          gm_w_spatial[l], gm_b_spatial[l].T)
        sb_o, got = _sb_fwd(proj3, [w_branch_b[l], w_out_b[l], w_ffn_out_p[l]])
        gw = dict(w_in=g_in_next,
                  w_branch=jnp.transpose(got[0], (1, 2, 0, 3)).reshape(NB, BW, D),
                  w_out=got[1].reshape(D, D),
                  w_ffn_in=g_ffn_in_w,
                  w_ffn_out=got[2].reshape(FFP, D))
        gathered.append(gw)
        br_pool = _pool_fwd(proj3, pool_w[l], pool_scale[l].reshape(1, BW))
        brs = [br_gm, sb_o.reshape(t, BW), br_pool.reshape(t, BW)]
        merged, y0, y1, y2 = _merge_fwd(brs, gw["w_branch"], proj)
        x_mid, mo = _mm_residual("out_fwd", merged, gw["w_out"], xc.reshape(t, D), gt1, seq)
        x_mid = x_mid.reshape(nb, seq, D)
        h2 = _norm_mod_fwd(x_mid, rms_g2[l].reshape(1, D), sc2, sh2).reshape(t, D)
        fg, fu, act, got = _ffn_in_fwd(h2, gw["w_ffn_in"], [w_in_b[l + 1]] if l + 1 < nl else [])
        if l + 1 < nl:
            g_in_next = got[0]
        x_out, fo = _mm_residual("ffn_out_fwd", act, gw["w_ffn_out"], x_mid.reshape(t, D), gt2, seq)
        saved.append(dict(x_in=xc, h=h, proj=proj, brs=brs, sb_o=sb_o, ys=(y0, y1, y2), merged=merged,
                          mo=mo, x_mid=x_mid, h2=h2, fg=fg, fu=fu, act=act, fo=fo))
        xc = x_out.reshape(nb, seq, D)

    dx, loss_part, dfinal_part = _loss_head(xc, loss_target, final_g.reshape(1, D))
    loss = lax.psum(jnp.sum(loss_part[:, 0, 0]), ("x", "y", "c"))

    big_names = ("w_in", "w_branch", "w_out", "w_ffn_in", "w_ffn_out")
    bufs = {name: None for name in big_names}
    small_parts = {k: [None] * nl for k in ("rms_g1", "rms_g2", "gm_ln_g", "gm_ln_b", "gm_w_spatial",
                                            "gm_b_spatial", "pool_w", "pool_scale")}
    dmod = [None] * nl
    for l in reversed(range(nl)):
        gw = gathered[l]
        sv = saved[l]
        sh1, sc1, gt1, sh2, sc2, gt2 = [mod[l, :, i] for i in range(NMOD)]
        dfo, dgt2 = _gate_bwd(dx, sv["fo"].reshape(nb, seq, D), gt2)
        dfo = dfo.reshape(t, D)
        g_ffn_out = _mm_tn("ffn_out_wgrad", sv["act"], dfo)
        dfg, dfu = _ffn_out_dgrad(dfo, gw["w_ffn_out"], sv["fg"], sv["fu"])
        dh2 = _ffn_in_dgrad(dfg, dfu, gw["w_ffn_in"])
        g_ffn_in = _ffn_in_wgrad(sv["h2"], dfg, dfu)
        dx_mid, dsh2, dsc2, dg2 = _norm_mod_bwd(sv["x_mid"], dh2.reshape(nb, seq, D), dx,
                                                rms_g2[l].reshape(1, D), sc2)
        dmo, dgt1 = _gate_bwd(dx_mid, sv["mo"].reshape(nb, seq, D), gt1)
        dmo = dmo.reshape(t, D)
        dmerged = _mm_nt("out_dgrad", dmo, gw["w_out"], BF16)
        g_out = _mm_tn("out_wgrad", sv["merged"], dmo)
        dls_dys = _merge_bwd(dmerged, sv["ys"], sv["proj"])
        dls, dys = dls_dys[:3], dls_dys[3:]
        dbrs, g_br = [], []
        for n in range(NB):
            dbrs.append(_mm_nt("branch_dgrad", dys[n], gw["w_branch"], BF16, w_lead=n))
            g_br.append(_mm_tn("branch_wgrad", sv["brs"][n], dys[n]))
        proj3 = sv["proj"].reshape(nb, seq, IN_COLS)
        d_gm, g_ws, g_bs, g_lg, g_lb = _gmlp_bwd(sv["proj"], dbrs[0], gm_ln_g[l].reshape(1, BW),
                                                 gm_ln_b[l].reshape(1, BW), gm_w_spatial[l], gm_b_spatial[l].T)
        g_br_dev = jnp.transpose(jnp.stack(g_br).reshape(NB, BW, NDEV, D // NDEV), (2, 0, 1, 3))
        carried = [g_br_dev, g_out.reshape(NDEV, D // NDEV, D), g_ffn_in, g_ffn_out.reshape(NDEV, FF_HALF_PAD, D)]
        d_sb, recv = _sb_bwd(proj3, dbrs[1].reshape(nb, seq, BW), sv["sb_o"], carried)
        bufs["w_branch"] = _adamw_layer("adamw_w_branch", recv[0], w_branch, m_w_branch, v_w_branch, l,
                                        bufs["w_branch"])
        bufs["w_out"] = _adamw_layer("adamw_w_out", recv[1], w_out, m_w_out, v_w_out, l, bufs["w_out"])
        bufs["w_ffn_in"] = _adamw_layer("adamw_w_ffn_in", _unpad_ffn_in(recv[2]), w_ffn_in, m_w_ffn_in,
                                        v_w_ffn_in, l, bufs["w_ffn_in"])
        bufs["w_ffn_out"] = _adamw_layer("adamw_w_ffn_out", recv[3][:, :FF_HALF], w_ffn_out, m_w_ffn_out,
                                         v_w_ffn_out, l, bufs["w_ffn_out"])
        d_pool, g_pw, g_ps = _pool_bwd(proj3, dbrs[2].reshape(nb, seq, BW), pool_w[l], pool_scale[l].reshape(1, BW))
        dproj = jnp.concatenate([d_gm] + [a.reshape(t, BW) for a in d_sb] + [d_pool.reshape(t, BW)] + list(dls),
                                axis=1)
        g_in = _mm_colblocked_tn("proj_wgrad", sv["h"], dproj)
        dh, (r_in,) = _mm_colblocked_nt("proj_dgrad", dproj, gw["w_in"], F32, [g_in])
        bufs["w_in"] = _adamw_layer("adamw_w_in", r_in, w_in, m_w_in, v_w_in, l, bufs["w_in"])
        dx, dsh1, dsc1, dg1 = _norm_mod_bwd(sv["x_in"], dh.reshape(nb, seq, D), dx_mid,
                                            rms_g1[l].reshape(1, D), sc1)

        dmod[l] = jnp.concatenate([dsh1, dsc1, dgt1, dsh2, dsc2, dgt2], axis=-1)
        small_parts["rms_g1"][l] = jnp.sum(dg1, axis=0)
        small_parts["rms_g2"][l] = jnp.sum(dg2, axis=0)
        small_parts["gm_ln_g"][l] = g_lg
        small_parts["gm_ln_b"][l] = g_lb
        small_parts["gm_w_spatial"][l] = g_ws
        small_parts["gm_b_spatial"][l] = g_bs[:, :, 0]
        small_parts["pool_w"][l] = g_pw
        small_parts["pool_scale"][l] = g_ps

    dmod_mine = jnp.stack(dmod).reshape(nl, nb, NMOD * D)
    names = list(small_parts)
    stacked = [jnp.stack(small_parts[k]) for k in names]
    gathered_small = _exchange(stacked + [dfinal_part, dmod_mine], "gather_small", False)
    dmod_all = jnp.transpose(gathered_small[-1], (1, 0, 2, 3)).reshape(nl, ntot, NMOD * D)
    dfinal_all = gathered_small[-2].reshape(ntot, D)

    results = {}
    weights = dict(rms_g1=(rms_g1, m_rms_g1, v_rms_g1), rms_g2=(rms_g2, m_rms_g2, v_rms_g2),
                   gm_ln_g=(gm_ln_g, m_gm_ln_g, v_gm_ln_g), gm_ln_b=(gm_ln_b, m_gm_ln_b, v_gm_ln_b),
                   gm_w_spatial=(gm_w_spatial, m_gm_w_spatial, v_gm_w_spatial),
                   gm_b_spatial=(gm_b_spatial, m_gm_b_spatial, v_gm_b_spatial),
                   pool_w=(pool_w, m_pool_w, v_pool_w), pool_scale=(pool_scale, m_pool_scale, v_pool_scale))
    for k, parts in zip(names, gathered_small[:len(names)]):
        w, m, v = weights[k]
        results[k] = _adamw_reduce("adamw_" + k, parts.reshape((NDEV,) + w.shape), w, m, v)
    results["final_g"] = _adamw_reduce("adamw_final_g", dfinal_all, final_g, m_final_g, v_final_g)
    results["b_ada"] = _adamw_reduce("adamw_b_ada", jnp.transpose(dmod_all, (1, 0, 2)), b_ada, m_b_ada, v_b_ada)
    dmod_blk = lax.dynamic_slice_in_dim(dmod_all, me * 768, 768, axis=2)
    g_w_ada = _ada_bwd(c_all, dmod_blk)
    results["w_ada"] = _adamw_reduce("adamw_w_ada", g_w_ada[None], w_ada, m_w_ada, v_w_ada)
    stacked_w = dict(w_in=w_in, w_branch=w_branch, w_out=w_out, w_ffn_in=w_ffn_in, w_ffn_out=w_ffn_out)
    for name in big_names:
        results[name] = tuple(b.reshape(stacked_w[name].shape) for b in bufs[name])

    order = ["rms_g1", "rms_g2", "w_ada", "b_ada", "w_in", "gm_ln_g", "gm_ln_b", "gm_w_spatial", "gm_b_spatial",
             "pool_w", "pool_scale", "w_branch", "w_out", "w_ffn_in", "w_ffn_out", "final_g"]
    out = [loss, dx]
    for i in range(4):
        out.extend(results[k][i] for k in order)
    return tuple(out)
```

```python
import functools
import math

import jax
import jax.numpy as jnp
from jax import lax
from jax.experimental import pallas as pl
from jax.experimental.pallas import tpu as pltpu

F32 = jnp.float32
BF16 = jnp.bfloat16
MESH = pl.DeviceIdType.MESH

D = 1024
BW = 512
NB = 3
CH = 128
NG = 4
HD = 64
POOL_WINDOWS = (2, 4, 8, 16)
DFF = 2816
NMOD = 6
EPS = 1e-6
IN_COLS = 6 * D
NDEV = 8
FF_IN_SHARD = 2 * DFF // NDEV
FF_HALF = FF_IN_SHARD // 2
FF_HALF_PAD = 384
FF_IN_PAD = 2 * FF_HALF_PAD
FFP = NDEV // 2 * FF_IN_PAD

ADAM_LR = 0.001
ADAM_B1 = 0.9
ADAM_B2 = 0.999
ADAM_EPS = 1e-08
ADAM_WD = 0.01
ADAM_STEP = 10

VMEM_LIMIT = 48 * 1024 * 1024

NN = (((1,), (0,)), ((), ()))
NT = (((1,), (1,)), ((), ()))
TN = (((0,), (0,)), ((), ()))


def _cp(sem=None):
    return pltpu.CompilerParams(dimension_semantics=sem, vmem_limit_bytes=VMEM_LIMIT)


def _dot(a, b, dims=NN):
    return lax.dot_general(a, b, dims, preferred_element_type=F32)


def _my_index():
    return 4 * lax.axis_index("x") + 2 * lax.axis_index("y") + lax.axis_index("c")


def _peer(k):
    x, y, c = lax.axis_index("x"), lax.axis_index("y"), lax.axis_index("c")
    px = 1 - x if k & 4 else x
    py = 1 - y if k & 2 else y
    pc = 1 - c if k & 1 else c
    return (px, py, pc), 4 * px + 2 * py + pc


def _exchange(xs, name, all_to_all):
    n = len(xs)

    def body(*refs):
        _exchange_start(refs[:n], refs[n:2 * n], refs[2 * n:], all_to_all)
        _exchange_finish(refs[:n], refs[n:2 * n], refs[2 * n:], all_to_all)

    return pl.pallas_call(
        body, name=name, out_shape=_exchange_out_shape(xs, all_to_all),
        in_specs=[_HBM] * n, out_specs=[_HBM] * n, scratch_shapes=_exchange_sems(n),
    )(*xs)


_HBM = pl.BlockSpec(memory_space=pl.ANY)


def _exchange_out_shape(xs, all_to_all):
    if all_to_all:
        return [jax.ShapeDtypeStruct(x.shape, x.dtype) for x in xs]
    return [jax.ShapeDtypeStruct((NDEV,) + x.shape, x.dtype) for x in xs]


def _exchange_sems(n):
    return [pltpu.SemaphoreType.DMA((n * 7,)), pltpu.SemaphoreType.DMA((n * 7,)), pltpu.SemaphoreType.DMA((n,))]


def _exchange_copies(ins, outs, sems, all_to_all):
    send_sems, recv_sems, local_sems = sems
    me = _my_index()
    local, sends, recvs = [], [], []
    for a in range(len(ins)):
        src = ins[a].at[me] if all_to_all else ins[a]
        local.append(pltpu.make_async_copy(src, outs[a].at[me], local_sems.at[a]))
    for k in range(1, NDEV):
        dev, idx = _peer(k)
        for a in range(len(ins)):
            src = ins[a].at[idx] if all_to_all else ins[a]
            sem = dict(send_sem=send_sems.at[a * 7 + k - 1], recv_sem=recv_sems.at[a * 7 + k - 1],
                       device_id=dev, device_id_type=MESH)
            sends.append(pltpu.make_async_remote_copy(src_ref=src, dst_ref=outs[a].at[me], **sem))
            recvs.append(pltpu.make_async_remote_copy(src_ref=src, dst_ref=outs[a].at[idx], **sem))
    return local, sends, recvs


def _exchange_start(ins, outs, sems, all_to_all):
    local, sends, _ = _exchange_copies(ins, outs, sems, all_to_all)
    for cp in local + sends:
        cp.start()


def _exchange_finish(ins, outs, sems, all_to_all):
    local, sends, recvs = _exchange_copies(ins, outs, sems, all_to_all)
    for cp in sends:
        cp.wait_send()
    for cp in recvs:
        cp.wait_recv()
    for cp in local:
        cp.wait()


def _host_exchange(body, n_in, n_out, grid, xs, all_to_all):
    n = len(xs)
    if n == 0:
        return body, [], [], [], []

    def hosted(*refs):
        ins, ex_ins = refs[:n_in], refs[n_in:n_in + n]
        outs, ex_outs = refs[n_in + n:n_in + n + n_out], refs[n_in + n + n_out:n_in + 2 * n + n_out]
        scratch = refs[n_in + 2 * n + n_out:]
        own, sems = scratch[:len(scratch) - 3], scratch[len(scratch) - 3:]
        first = functools.reduce(jnp.logical_and, [pl.program_id(a) == 0 for a in range(len(grid))])
        last = functools.reduce(jnp.logical_and, [pl.program_id(a) == grid[a] - 1 for a in range(len(grid))])

        @pl.when(first)
        def _():
            _exchange_start(ex_ins, ex_outs, sems, all_to_all)

        body(*ins, *outs, *own)

        @pl.when(last)
        def _():
            _exchange_finish(ex_ins, ex_outs, sems, all_to_all)

    return hosted, [_HBM] * n, [_HBM] * n, _exchange_out_shape(xs, all_to_all), _exchange_sems(n)


def _mm(name, a, b, grid, a_spec, b_spec, o_spec, out_sds, dims, acc_shape, carried=(), all_to_all=True):
    nk = grid[2]

    if nk == 1:
        def body(a_ref, b_ref, o_ref):
            o_ref[...] = _dot(a_ref[...].astype(BF16), b_ref[...].astype(BF16), dims).astype(o_ref.dtype)
        scratch = []
    else:
        def body(a_ref, b_ref, o_ref, acc_ref):
            k = pl.program_id(2)

            @pl.when(k == 0)
            def _():
                acc_ref[...] = jnp.zeros_like(acc_ref)

            acc_ref[...] += _dot(a_ref[...].astype(BF16), b_ref[...].astype(BF16), dims)

            @pl.when(k == nk - 1)
            def _():
                o_ref[...] = acc_ref[...].astype(o_ref.dtype)
        scratch = [pltpu.VMEM(acc_shape, F32)]

    body, ex_in, ex_out, ex_shape, ex_sems = _host_exchange(body, 2, 1, grid, carried, all_to_all)
    outs = pl.pallas_call(
        body, name=name, grid=grid, in_specs=[a_spec, b_spec] + ex_in, out_specs=[o_spec] + ex_out,
        out_shape=[out_sds] + ex_shape,
        scratch_shapes=scratch + ex_sems,
        compiler_params=_cp(("arbitrary",) * 3 if carried else ("parallel", "parallel", "arbitrary")),
    )(a, b, *carried)
    return (outs[0], outs[1:]) if carried else outs[0]


def _row_tile(t, want):
    tm = min(t, want)
    assert t % tm == 0
    return tm


def _mm_colblocked(name, a, wg, out_dtype, gather=()):
    t = a.shape[0]
    tm = _row_tile(t, 1024)
    return _mm(name, a, wg, (t // tm, NDEV, 1),
               pl.BlockSpec((tm, D), lambda i, j, k: (i, 0)),
               pl.BlockSpec((None, D, 768), lambda i, j, k: (j, 0, 0)),
               pl.BlockSpec((tm, 768), lambda i, j, k: (i, j)),
               jax.ShapeDtypeStruct((t, NDEV * 768), out_dtype), NN, (tm, 768), gather, False)


def _mm_colblocked_nt(name, g, wg, out_dtype, scatter=()):
    t = g.shape[0]
    tm = _row_tile(t, 1024)
    return _mm(name, g, wg, (t // tm, 1, NDEV),
               pl.BlockSpec((tm, 768), lambda i, j, k: (i, k)),
               pl.BlockSpec((None, D, 768), lambda i, j, k: (k, 0, 0)),
               pl.BlockSpec((tm, D), lambda i, j, k: (i, 0)),
               jax.ShapeDtypeStruct((t, D), out_dtype), NT, (tm, D), scatter)


_HALF = NDEV // 2


def _ffn_in_fwd(h2, wg, gather=()):
    t = h2.shape[0]
    tm = _row_tile(t, 1024)

    def body(a_ref, wg_ref, wu_ref, g_ref, u_ref, act_ref):
        a = a_ref[...]
        g = _dot(a, wg_ref[...])
        u = _dot(a, wu_ref[...])
        g_ref[...] = g.astype(BF16)
        u_ref[...] = u.astype(BF16)
        act_ref[...] = (g * jax.nn.sigmoid(g) * u).astype(BF16)

    tile = pl.BlockSpec((tm, 768), lambda i, j: (i, j))
    grid = (t // tm, _HALF)
    body, ex_in, ex_out, ex_shape, ex_sems = _host_exchange(body, 3, 3, grid, gather, False)
    outs = pl.pallas_call(
        body, name="ffn_in_fwd", grid=grid,
        in_specs=[pl.BlockSpec((tm, D), lambda i, j: (i, 0)),
                  pl.BlockSpec((None, D, 768), lambda i, j: (j, 0, 0)),
                  pl.BlockSpec((None, D, 768), lambda i, j: (j + _HALF, 0, 0))] + ex_in,
        out_specs=[tile] * 3 + ex_out, out_shape=[jax.ShapeDtypeStruct((t, FFP), BF16)] * 3 + ex_shape,
        scratch_shapes=ex_sems,
        compiler_params=_cp(("arbitrary", "arbitrary") if gather else ("parallel", "parallel")),
    )(h2, wg, wg, *gather)
    return outs[0], outs[1], outs[2], outs[3:]


def _ffn_out_dgrad(dfo, w, fg, fu):
    t = dfo.shape[0]
    tm = _row_tile(t, 1024)

    def body(a_ref, w_ref, g_ref, u_ref, dg_ref, du_ref):
        d = _dot(a_ref[...], w_ref[...], NT)
        g = g_ref[...].astype(F32)
        s = jax.nn.sigmoid(g)
        dg_ref[...] = (d * u_ref[...].astype(F32) * (s * (1.0 + g * (1.0 - s)))).astype(BF16)
        du_ref[...] = (d * (g * s)).astype(BF16)

    tile = pl.BlockSpec((tm, 768), lambda i, j: (i, j))
    return pl.pallas_call(
        body, name="ffn_out_dgrad", grid=(t // tm, _HALF),
        in_specs=[pl.BlockSpec((tm, D), lambda i, j: (i, 0)),
                  pl.BlockSpec((768, D), lambda i, j: (j, 0)), tile, tile],
        out_specs=[tile] * 2, out_shape=[jax.ShapeDtypeStruct((t, FFP), BF16)] * 2,
        compiler_params=_cp(("parallel", "parallel")),
    )(dfo, w, fg, fu)


def _ffn_in_dgrad(dg, du, wg):
    t = dg.shape[0]
    tm = _row_tile(t, 1024)

    def body(g_ref, u_ref, w_ref, o_ref, acc_ref):
        k = pl.program_id(1)

        @pl.when(k == 0)
        def _():
            acc_ref[...] = jnp.zeros_like(acc_ref)

        @pl.when(k < _HALF)
        def _():
            acc_ref[...] += _dot(g_ref[...], w_ref[...], NT)

        @pl.when(k >= _HALF)
        def _():
            acc_ref[...] += _dot(u_ref[...], w_ref[...], NT)

        @pl.when(k == NDEV - 1)
        def _():
            o_ref[...] = acc_ref[...]

    return pl.pallas_call(
        body, name="ffn_in_dgrad", grid=(t // tm, NDEV),
        in_specs=[pl.BlockSpec((tm, 768), lambda i, k: (i, jnp.minimum(k, _HALF - 1))),
                  pl.BlockSpec((tm, 768), lambda i, k: (i, jnp.maximum(k - _HALF, 0))),
                  pl.BlockSpec((None, D, 768), lambda i, k: (k, 0, 0))],
        out_specs=pl.BlockSpec((tm, D), lambda i, k: (i, 0)),
        out_shape=jax.ShapeDtypeStruct((t, D), F32),
        scratch_shapes=[pltpu.VMEM((tm, D), F32)],
        compiler_params=_cp(("parallel", "arbitrary")),
    )(dg, du, wg)


def _ffn_in_wgrad(h2, dg, du):
    t = h2.shape[0]
    tk = _row_tile(t, 1024)
    nk = t // tk

    def body(a_ref, g_ref, u_ref, o_ref, acc_ref):
        j, k = pl.program_id(0), pl.program_id(1)

        @pl.when(k == 0)
        def _():
            acc_ref[...] = jnp.zeros_like(acc_ref)

        @pl.when(j < _HALF)
        def _():
            acc_ref[...] += _dot(g_ref[...], a_ref[...], TN)

        @pl.when(j >= _HALF)
        def _():
            acc_ref[...] += _dot(u_ref[...], a_ref[...], TN)

        @pl.when(k == nk - 1)
        def _():
            o_ref[...] = acc_ref[...].astype(BF16)

    return pl.pallas_call(
        body, name="ffn_in_wgrad", grid=(NDEV, nk),
        in_specs=[pl.BlockSpec((tk, D), lambda j, k: (k, 0)),
                  pl.BlockSpec((tk, 768), lambda j, k: (jnp.where(j < _HALF, k, 0), jnp.minimum(j, _HALF - 1))),
                  pl.BlockSpec((tk, 768), lambda j, k: (jnp.where(j < _HALF, 0, k), jnp.maximum(j - _HALF, 0)))],
        out_specs=pl.BlockSpec((None, 768, D), lambda j, k: (j, 0, 0)),
        out_shape=jax.ShapeDtypeStruct((NDEV, 768, D), BF16),
        scratch_shapes=[pltpu.VMEM((768, D), F32)],
        compiler_params=_cp(("parallel", "arbitrary")),
    )(h2, dg, du)


def _mm_colblocked_tn(name, a, g):
    t = a.shape[0]
    tk = _row_tile(t, 1024)
    return _mm(name, a, g, (1, NDEV, t // tk),
               pl.BlockSpec((tk, D), lambda i, j, k: (k, 0)),
               pl.BlockSpec((tk, 768), lambda i, j, k: (k, j)),
               pl.BlockSpec((None, D, 768), lambda i, j, k: (j, 0, 0)),
               jax.ShapeDtypeStruct((NDEV, D, 768), BF16), TN, (D, 768))


def _mm_nt(name, a, w, out_dtype, a_col=0, w_lead=None):
    t = a.shape[0]
    if w_lead is None:
        kdim, n = w.shape
        b_spec = pl.BlockSpec((min(kdim, 1024), n), lambda i, j, k: (j, 0))
    else:
        _, kdim, n = w.shape
        b_spec = pl.BlockSpec((None, min(kdim, 1024), n), lambda i, j, k: (w_lead, j, 0))
    tn = min(kdim, 1024)
    tm = _row_tile(t, 1024)
    return _mm(name, a, w, (t // tm, kdim // tn, 1),
               pl.BlockSpec((tm, n), lambda i, j, k: (i, a_col)),
               b_spec,
               pl.BlockSpec((tm, tn), lambda i, j, k: (i, j)),
               jax.ShapeDtypeStruct((t, kdim), out_dtype), NT, (tm, tn))


def _mm_tn(name, a, g, out_dtype=BF16):
    t, kdim = a.shape
    n = g.shape[1]
    tk = _row_tile(t, 1024)
    tm = min(kdim, 1024)
    tn = min(n, 1024)
    return _mm(name, a, g, (kdim // tm, n // tn, t // tk),
               pl.BlockSpec((tk, tm), lambda i, j, k: (k, i)),
               pl.BlockSpec((tk, tn), lambda i, j, k: (k, j)),
               pl.BlockSpec((tm, tn), lambda i, j, k: (i, j)),
               jax.ShapeDtypeStruct((kdim, n), out_dtype), TN, (tm, tn))


def _mm_residual(name, a, w, x, gt, seq):
    t, kdim = a.shape
    tm = _row_tile(seq, 1024)
    tn = D
    tk = min(kdim, 1024)
    nk = kdim // tk
    per = seq // tm

    def body(a_ref, w_ref, x_ref, gt_ref, xo_ref, y_ref, acc_ref):
        k = pl.program_id(2)

        @pl.when(k == 0)
        def _():
            acc_ref[...] = jnp.zeros_like(acc_ref)

        acc_ref[...] += _dot(a_ref[...], w_ref[...])

        @pl.when(k == nk - 1)
        def _():
            y = acc_ref[...]
            xo_ref[...] = x_ref[...] + gt_ref[0] * y
            y_ref[...] = y.astype(BF16)

    return pl.pallas_call(
        body, name=name, grid=(t // tm, D // tn, nk),
        in_specs=[pl.BlockSpec((tm, tk), lambda i, j, k: (i, k)),
                  pl.BlockSpec((tk, tn), lambda i, j, k: (k, j)),
                  pl.BlockSpec((tm, tn), lambda i, j, k: (i, j)),
                  pl.BlockSpec((1, 1, tn), lambda i, j, k: (i // per, 0, j))],
        out_specs=[pl.BlockSpec((tm, tn), lambda i, j, k: (i, j)),
                   pl.BlockSpec((tm, tn), lambda i, j, k: (i, j))],
        out_shape=[jax.ShapeDtypeStruct((t, D), F32), jax.ShapeDtypeStruct((t, D), BF16)],
        scratch_shapes=[pltpu.VMEM((tm, tn), F32)],
        compiler_params=_cp(("parallel", "parallel", "arbitrary")),
    )(a, w, x, gt)


def _ada_fwd(c_all, w_ada, b_blk):
    nl = w_ada.shape[0]
    nb = c_all.shape[0]

    def body(c_ref, w_ref, b_ref, o_ref):
        c = c_ref[...]
        ca = (c * jax.nn.sigmoid(c)).astype(BF16)
        o_ref[...] = _dot(ca, w_ref[...].astype(BF16)) + b_ref[...]

    return pl.pallas_call(
        body, name="ada_fwd", grid=(nl,),
        in_specs=[pl.BlockSpec((nb, D), lambda l: (0, 0)),
                  pl.BlockSpec((None, D, 768), lambda l: (l, 0, 0)),
                  pl.BlockSpec((None, 1, 768), lambda l: (l, 0, 0))],
        out_specs=pl.BlockSpec((None, nb, 768), lambda l: (l, 0, 0)),
        out_shape=jax.ShapeDtypeStruct((nl, nb, 768), F32),
        compiler_params=_cp(("parallel",)),
    )(c_all, w_ada, b_blk)


def _ada_bwd(c_all, dmod_blk):
    nl = dmod_blk.shape[0]
    nb = c_all.shape[0]

    def body(c_ref, d_ref, o_ref):
        c = c_ref[...]
        ca = (c * jax.nn.sigmoid(c)).astype(BF16)
        o_ref[...] = _dot(ca, d_ref[...].astype(BF16), TN)

    return pl.pallas_call(
        body, name="ada_bwd", grid=(nl,),
        in_specs=[pl.BlockSpec((nb, D), lambda l: (0, 0)),
                  pl.BlockSpec((None, nb, 768), lambda l: (l, 0, 0))],
        out_specs=pl.BlockSpec((None, D, 768), lambda l: (l, 0, 0)),
        out_shape=jax.ShapeDtypeStruct((nl, D, 768), F32),
        compiler_params=_cp(("parallel",)),
    )(c_all, dmod_blk)


def _seq_tile(seq):
    return _row_tile(seq, 512)


def _norm_mod_fwd(x, g, sc, sh):
    nb, seq, _ = x.shape
    ts = _seq_tile(seq)

    def body(x_ref, g_ref, sc_ref, sh_ref, h_ref):
        xv = x_ref[0]
        r = lax.rsqrt(jnp.mean(xv * xv, axis=-1, keepdims=True) + EPS)
        h_ref[0] = ((xv * r) * g_ref[...] * (1.0 + sc_ref[0]) + sh_ref[0]).astype(BF16)

    return pl.pallas_call(
        body, name="norm_mod_fwd", grid=(nb, seq // ts),
        in_specs=[pl.BlockSpec((1, ts, D), lambda b, s: (b, s, 0)),
                  pl.BlockSpec((1, D), lambda b, s: (0, 0)),
                  pl.BlockSpec((1, 1, D), lambda b, s: (b, 0, 0)),
                  pl.BlockSpec((1, 1, D), lambda b, s: (b, 0, 0))],
        out_specs=pl.BlockSpec((1, ts, D), lambda b, s: (b, s, 0)),
        out_shape=jax.ShapeDtypeStruct((nb, seq, D), BF16),
        compiler_params=_cp(("parallel", "parallel")),
    )(x, g, sc, sh)


def _norm_mod_bwd(x, dh, dres, g, sc):
    nb, seq, _ = x.shape
    ts = _seq_tile(seq)

    def body(x_ref, dh_ref, dres_ref, g_ref, sc_ref, dx_ref, dsh_ref, dsc_ref, dg_ref):
        @pl.when(pl.program_id(1) == 0)
        def _():
            dsh_ref[...] = jnp.zeros_like(dsh_ref)
            dsc_ref[...] = jnp.zeros_like(dsc_ref)
            dg_ref[...] = jnp.zeros_like(dg_ref)

        xv = x_ref[0]
        dh = dh_ref[0]
        gv = g_ref[...]
        onesc = 1.0 + sc_ref[0]
        r = lax.rsqrt(jnp.mean(xv * xv, axis=-1, keepdims=True) + EPS)
        xh = xv * r
        dsh_ref[0] += jnp.sum(dh, axis=0, keepdims=True)
        dsc_ref[0] += jnp.sum(dh * (xh * gv), axis=0, keepdims=True)
        dg_ref[0] += jnp.sum(dh * onesc * xh, axis=0, keepdims=True)
        dxh = dh * (gv * onesc)
        dx = r * (dxh - xh * jnp.mean(dxh * xh, axis=-1, keepdims=True))
        dx_ref[0] = dres_ref[0] + dx

    vec = jax.ShapeDtypeStruct((nb, 1, D), F32)
    vspec = pl.BlockSpec((1, 1, D), lambda b, s: (b, 0, 0))
    tile = pl.BlockSpec((1, ts, D), lambda b, s: (b, s, 0))
    return pl.pallas_call(
        body, name="norm_mod_bwd", grid=(nb, seq // ts),
        in_specs=[tile, tile, tile, pl.BlockSpec((1, D), lambda b, s: (0, 0)), vspec],
        out_specs=[tile, vspec, vspec, vspec],
        out_shape=[jax.ShapeDtypeStruct((nb, seq, D), F32), vec, vec, vec],
        compiler_params=_cp(("parallel", "arbitrary")),
    )(x, dh, dres, g, sc)


def _gate_bwd(dx, y, gt):
    nb, seq, _ = dx.shape
    ts = _seq_tile(seq)

    def body(dx_ref, y_ref, gt_ref, dy_ref, dgt_ref):
        @pl.when(pl.program_id(1) == 0)
        def _():
            dgt_ref[...] = jnp.zeros_like(dgt_ref)

        d = dx_ref[0]
        dy_ref[0] = (gt_ref[0] * d).astype(BF16)
        dgt_ref[0] += jnp.sum(d * y_ref[0].astype(F32), axis=0, keepdims=True)

    vspec = pl.BlockSpec((1, 1, D), lambda b, s: (b, 0, 0))
    tile = pl.BlockSpec((1, ts, D), lambda b, s: (b, s, 0))
    return pl.pallas_call(
        body, name="gate_bwd", grid=(nb, seq // ts),
        in_specs=[tile, tile, vspec], out_specs=[tile, vspec],
        out_shape=[jax.ShapeDtypeStruct((nb, seq, D), BF16), jax.ShapeDtypeStruct((nb, 1, D), F32)],
        compiler_params=_cp(("parallel", "arbitrary")),
    )(dx, y, gt)


def _loss_head(x, tgt, g):
    nb, seq, _ = x.shape
    ts = _seq_tile(seq)

    def body(x_ref, t_ref, g_ref, dx_ref, loss_ref, dg_ref):
        @pl.when(pl.program_id(1) == 0)
        def _():
            loss_ref[...] = jnp.zeros_like(loss_ref)
            dg_ref[...] = jnp.zeros_like(dg_ref)

        xv = x_ref[0]
        gv = g_ref[...]
        r = lax.rsqrt(jnp.mean(xv * xv, axis=-1, keepdims=True) + EPS)
        xh = xv * r
        err = xh * gv - t_ref[0]
        per_tok = jnp.mean(err * err, axis=-1, keepdims=True)
        loss_ref[0] += 0.5 * jnp.sum(per_tok, axis=0, keepdims=True)
        dy = err * (1.0 / D)
        dg_ref[0] += jnp.sum(dy * xh, axis=0, keepdims=True)
        dxh = dy * gv
        dx_ref[0] = r * (dxh - xh * jnp.mean(dxh * xh, axis=-1, keepdims=True))

    tile = pl.BlockSpec((1, ts, D), lambda b, s: (b, s, 0))
    return pl.pallas_call(
        body, name="loss_head", grid=(nb, seq // ts),
        in_specs=[tile, tile, pl.BlockSpec((1, D), lambda b, s: (0, 0))],
        out_specs=[tile, pl.BlockSpec((1, 1, 128), lambda b, s: (b, 0, 0)),
                   pl.BlockSpec((1, 1, D), lambda b, s: (b, 0, 0))],
        out_shape=[jax.ShapeDtypeStruct((nb, seq, D), F32), jax.ShapeDtypeStruct((nb, 1, 128), F32),
                   jax.ShapeDtypeStruct((nb, 1, D), F32)],
        compiler_params=_cp(("parallel", "arbitrary")),
    )(x, tgt, g)


_GELU_C = math.sqrt(2.0 / math.pi)


def _gelu(x):
    return 0.5 * x * (1.0 + jnp.tanh(_GELU_C * (x + 0.044715 * (x * x * x))))


def _gelu_and_grad(x):
    t = jnp.tanh(_GELU_C * (x + 0.044715 * (x * x * x)))
    y = 0.5 * x * (1.0 + t)
    dy = 0.5 * (1.0 + t) + 0.5 * x * (1.0 - t * t) * (_GELU_C * (1.0 + 3.0 * 0.044715 * (x * x)))
    return y, dy


def _tril_mask():
    row = lax.broadcasted_iota(jnp.int32, (CH, CH), 0)
    col = lax.broadcasted_iota(jnp.int32, (CH, CH), 1)
    return row >= col


def _gmlp_fwd(proj, ln_g, ln_b, ws, bst):
    t = proj.shape[0]
    tm = _row_tile(t, 512)

    def body(u_ref, v_ref, lg_ref, lb_ref, ws_ref, bst_ref, o_ref):
        tril = _tril_mask()
        wm = [jnp.where(tril, ws_ref[g], 0.0).astype(BF16) for g in range(NG)]
        for ch in range(tm // CH):
            rows = slice(ch * CH, (ch + 1) * CH)
            u = _gelu(u_ref[rows, :].astype(F32))
            v = _gelu(v_ref[rows, :].astype(F32))
            mu = jnp.mean(v, axis=-1, keepdims=True)
            xc = v - mu
            rstd = lax.rsqrt(jnp.mean(xc * xc, axis=-1, keepdims=True) + EPS)
            vn = ((xc * rstd) * lg_ref[...] + lb_ref[...]).astype(BF16)
            for g in range(NG):
                cols = slice(g * CH, (g + 1) * CH)
                s = _dot(wm[g], vn[:, cols]) + bst_ref[:, g:g + 1]
                o_ref[rows, cols] = (u[:, cols] * s).astype(BF16)

    return pl.pallas_call(
        body, name="gmlp_fwd", grid=(t // tm,),
        in_specs=[pl.BlockSpec((tm, BW), lambda i: (i, 0)),
                  pl.BlockSpec((tm, BW), lambda i: (i, 1)),
                  pl.BlockSpec((1, BW), lambda i: (0, 0)),
                  pl.BlockSpec((1, BW), lambda i: (0, 0)),
                  pl.BlockSpec((NG, CH, CH), lambda i: (0, 0, 0)),
                  pl.BlockSpec((CH, NG), lambda i: (0, 0))],
        out_specs=pl.BlockSpec((tm, BW), lambda i: (i, 0)),
        out_shape=jax.ShapeDtypeStruct((t, BW), BF16),
        compiler_params=_cp(("parallel",)),
    )(proj, proj, ln_g, ln_b, ws, bst)


def _gmlp_bwd(proj, dout, ln_g, ln_b, ws, bst):
    t = proj.shape[0]
    tm = _row_tile(t, 512)

    def body(u_ref, v_ref, do_ref, lg_ref, lb_ref, ws_ref, bst_ref, dp_ref, gws_ref, gbs_ref, glg_ref, glb_ref):
        @pl.when(pl.program_id(0) == 0)
        def _():
            gws_ref[...] = jnp.zeros_like(gws_ref)
            gbs_ref[...] = jnp.zeros_like(gbs_ref)
            glg_ref[...] = jnp.zeros_like(glg_ref)
            glb_ref[...] = jnp.zeros_like(glb_ref)

        tril = _tril_mask()
        wm = [jnp.where(tril, ws_ref[g], 0.0).astype(BF16) for g in range(NG)]
        ones = jnp.ones((CH, CH), BF16)
        lg = lg_ref[...]
        for ch in range(tm // CH):
            rows = slice(ch * CH, (ch + 1) * CH)
            u, du_fac = _gelu_and_grad(u_ref[rows, :].astype(F32))
            v, dv_fac = _gelu_and_grad(v_ref[rows, :].astype(F32))
            do = do_ref[rows, :].astype(F32)
            mu = jnp.mean(v, axis=-1, keepdims=True)
            xc = v - mu
            rstd = lax.rsqrt(jnp.mean(xc * xc, axis=-1, keepdims=True) + EPS)
            xh = xc * rstd
            vn = (xh * lg + lb_ref[...]).astype(BF16)
            dvn_parts = []
            for g in range(NG):
                cols = slice(g * CH, (g + 1) * CH)
                s = _dot(wm[g], vn[:, cols]) + bst_ref[:, g:g + 1]
                dp_ref[rows, cols] = (do[:, cols] * s * du_fac[:, cols]).astype(BF16)
                ds = (do[:, cols] * u[:, cols]).astype(BF16)
                gws_ref[g] += jnp.where(tril, _dot(ds, vn[:, cols], NT), 0.0)
                gbs_ref[g] += _dot(ds, ones)
                dvn_parts.append(_dot(wm[g], ds, TN))
            dvn = jnp.concatenate(dvn_parts, axis=1)
            glb_ref[...] += jnp.sum(dvn, axis=0, keepdims=True)
            glg_ref[...] += jnp.sum(dvn * xh, axis=0, keepdims=True)
            dxh = dvn * lg
            dv = rstd * (dxh - jnp.mean(dxh, axis=-1, keepdims=True)
                         - xh * jnp.mean(dxh * xh, axis=-1, keepdims=True))
            dp_ref[rows, BW:2 * BW] = (dv * dv_fac).astype(BF16)

    small = pl.BlockSpec((NG, CH, CH), lambda i: (0, 0, 0))
    vec = pl.BlockSpec((1, BW), lambda i: (0, 0))
    return pl.pallas_call(
        body, name="gmlp_bwd", grid=(t // tm,),
        in_specs=[pl.BlockSpec((tm, BW), lambda i: (i, 0)),
                  pl.BlockSpec((tm, BW), lambda i: (i, 1)),
                  pl.BlockSpec((tm, BW), lambda i: (i, 0)),
                  vec, vec, small, pl.BlockSpec((CH, NG), lambda i: (0, 0))],
        out_specs=[pl.BlockSpec((tm, 2 * BW), lambda i: (i, 0)), small, small, vec, vec],
        out_shape=[jax.ShapeDtypeStruct((t, 2 * BW), BF16),
                   jax.ShapeDtypeStruct((NG, CH, CH), F32), jax.ShapeDtypeStruct((NG, CH, CH), F32),
                   jax.ShapeDtypeStruct((1, BW), F32), jax.ShapeDtypeStruct((1, BW), F32)],
        compiler_params=_cp(("arbitrary",)),
    )(proj, proj, dout, ln_g, ln_b, ws, bst)


def _pool_bands():
    row = lax.broadcasted_iota(jnp.int32, (CH, CH), 0)
    col = lax.broadcasted_iota(jnp.int32, (CH, CH), 1)
    cur, prev = [], []
    for w in POOL_WINDOWS:
        cur.append(jnp.where((row >= col) & (row - col < w), 1.0, 0.0).astype(BF16))
        prev.append(jnp.where(row + CH - col < w, 1.0, 0.0).astype(BF16))
    return cur, prev


def _pool_inv_count(r0, w):
    pos = r0 + lax.broadcasted_iota(jnp.int32, (CH, 1), 0)
    return 1.0 / jnp.minimum(pos + 1, w).astype(F32)


def _pool_diff(x_ref, r0, rp, has_prev, cur, prev, g):
    cols = slice(g * CH, (g + 1) * CH)
    xc = x_ref[pl.ds(r0, CH), cols]
    xp = x_ref[pl.ds(rp, CH), cols]
    ws = _dot(cur[g], xc) + has_prev * _dot(prev[g], xp)
    return ws * _pool_inv_count(r0, POOL_WINDOWS[g]) - xc.astype(F32)


def _pool_fwd(proj3, pw, pscale):
    nb, seq, _ = proj3.shape
    nch = seq // CH

    def body(x_ref, pw_ref, ps_ref, o_ref):
        cur, prev = _pool_bands()
        pwb = [pw_ref[g].astype(BF16) for g in range(NG)]

        def chunk(ch, carry):
            r0 = pl.multiple_of(ch * CH, CH)
            rp = pl.multiple_of(jnp.maximum(ch - 1, 0) * CH, CH)
            has_prev = jnp.where(ch > 0, 1.0, 0.0)
            for g in range(NG):
                cols = slice(g * CH, (g + 1) * CH)
                d = _pool_diff(x_ref, r0, rp, has_prev, cur, prev, g)
                y = _dot(d.astype(BF16), pwb[g]) * ps_ref[:, cols]
                o_ref[pl.ds(r0, CH), cols] = y.astype(BF16)
            return carry

        lax.fori_loop(0, nch, chunk, 0, unroll=2)

    return pl.pallas_call(
        body, name="pool_fwd", grid=(nb,),
        in_specs=[pl.BlockSpec((None, seq, BW), lambda b: (b, 0, 5)),
                  pl.BlockSpec((NG, CH, CH), lambda b: (0, 0, 0)),
                  pl.BlockSpec((1, BW), lambda b: (0, 0))],
        out_specs=pl.BlockSpec((None, seq, BW), lambda b: (b, 0, 0)),
        out_shape=jax.ShapeDtypeStruct((nb, seq, BW), BF16),
        compiler_params=_cp(("parallel",)),
    )(proj3, pw, pscale)


def _pool_bwd(proj3, dout3, pw, pscale):
    nb, seq, _ = proj3.shape
    nch = seq // CH

    def body(x_ref, do_ref, pw_ref, ps_ref, dx_ref, gpw_ref, gps_ref, e_ref):
        @pl.when(pl.program_id(0) == 0)
        def _():
            gpw_ref[...] = jnp.zeros_like(gpw_ref)
            gps_ref[...] = jnp.zeros_like(gps_ref)

        cur, prev = _pool_bands()
        pwb = [pw_ref[g].astype(BF16) for g in range(NG)]

        def first(ch, carry):
            r0 = pl.multiple_of(ch * CH, CH)
            rp = pl.multiple_of(jnp.maximum(ch - 1, 0) * CH, CH)
            has_prev = jnp.where(ch > 0, 1.0, 0.0)
            for g in range(NG):
                cols = slice(g * CH, (g + 1) * CH)
                d = _pool_diff(x_ref, r0, rp, has_prev, cur, prev, g).astype(BF16)
                do = do_ref[pl.ds(r0, CH), cols].astype(F32)
                ypre = _dot(d, pwb[g])
                gps_ref[:, cols] += jnp.sum(do * ypre, axis=0, keepdims=True)
                dyp = (do * ps_ref[:, cols]).astype(BF16)
                gpw_ref[g] += _dot(d, dyp, TN)
                e_ref[pl.ds(r0, CH), cols] = _dot(dyp, pwb[g], NT)
            return carry

        lax.fori_loop(0, nch, first, 0, unroll=2)

        def second(ch, carry):
            r0 = pl.multiple_of(ch * CH, CH)
            rn = pl.multiple_of(jnp.minimum(ch + 1, nch - 1) * CH, CH)
            has_next = jnp.where(ch < nch - 1, 1.0, 0.0)
            for g in range(NG):
                cols = slice(g * CH, (g + 1) * CH)
                w = POOL_WINDOWS[g]
                dd = e_ref[pl.ds(r0, CH), cols]
                ec = (dd * _pool_inv_count(r0, w)).astype(BF16)
                en = (e_ref[pl.ds(rn, CH), cols] * _pool_inv_count(rn, w)).astype(BF16)
                dx = _dot(cur[g], ec, TN) + has_next * _dot(prev[g], en, TN) - dd
                dx_ref[pl.ds(r0, CH), cols] = dx.astype(BF16)
            return carry

        lax.fori_loop(0, nch, second, 0, unroll=2)

    small = pl.BlockSpec((NG, CH, CH), lambda b: (0, 0, 0))
    vec = pl.BlockSpec((1, BW), lambda b: (0, 0))
    return pl.pallas_call(
        body, name="pool_bwd", grid=(nb,),
        in_specs=[pl.BlockSpec((None, seq, BW), lambda b: (b, 0, 5)),
                  pl.BlockSpec((None, seq, BW), lambda b: (b, 0, 0)), small, vec],
        out_specs=[pl.BlockSpec((None, seq, BW), lambda b: (b, 0, 0)), small, vec],
        out_shape=[jax.ShapeDtypeStruct((nb, seq, BW), BF16),
                   jax.ShapeDtypeStruct((NG, CH, CH), F32), jax.ShapeDtypeStruct((1, BW), F32)],
        scratch_shapes=[pltpu.VMEM((seq, BW), F32)],
        compiler_params=_cp(("arbitrary",)),
    )(proj3, dout3, pw, pscale)


SB_BQ = 256
SB_BK = 256
SB_SCALE = HD ** -0.5


SB_EXIT = -110.0


def _sb_tile(qs, k, mask):
    z = _dot(qs, k, NT)
    lb = jnp.minimum(z, 0.0) - jnp.log(1.0 + jnp.exp(-jnp.abs(z)))
    lom = lb - z
    if mask is not None:
        lom = jnp.where(mask, lom, 0.0)
    return lb, lom


def _sb_alive(c):
    top = functools.reduce(jnp.maximum, [jnp.max(state[1]) for state in c])
    return (top > SB_EXIT).astype(jnp.int32)


def _sb_past_blocks(step, c, npast):
    def cond(s):
        return jnp.logical_and(s[0] < npast, s[1] > 0)

    def body(s):
        i, _, c = s
        c = step(pl.multiple_of((npast - 1 - i) * SB_BK, SB_BK), c, None)
        return i + 1, _sb_alive(c), c

    return lax.while_loop(cond, body, (jnp.int32(0), _sb_alive(c), c))[2]


def _sb_diag_mask(bq, d):
    row = lax.broadcasted_iota(jnp.int32, (bq, SB_BK), 0)
    col = lax.broadcasted_iota(jnp.int32, (bq, SB_BK), 1)
    return col + d * SB_BK < row


def _sb_scaled(q):
    return (q.astype(F32) * SB_SCALE).astype(BF16)


def _dot_tri(a, m):
    return _dot(a.astype(BF16), m)


def _dot_tri2(a, m):
    hi = a.astype(BF16)
    lo = (a - hi.astype(F32)).astype(BF16)
    return _dot(hi, m) + _dot(lo, m)


def _sb_fwd(proj3, gather=()):
    nb, seq, _ = proj3.shape
    bq = min(SB_BQ, seq)
    nq = seq // bq
    ndiag = bq // SB_BK

    def body(q_ref, k_ref, v_ref, o_ref):
        row = lax.broadcasted_iota(jnp.int32, (SB_BK, SB_BK), 0)
        col = lax.broadcasted_iota(jnp.int32, (SB_BK, SB_BK), 1)
        upper = jnp.where(row > col, 1.0, 0.0).astype(BF16)
        heads = [slice(hh * HD, (hh + 1) * HD) for hh in range(2)]

        def qloop(qi, carry):
            q0 = pl.multiple_of(qi * bq, bq)
            qs = [_sb_scaled(q_ref[pl.ds(q0, bq), lanes]) for lanes in heads]

            def step(k0, c, mask):
                tiles = [_sb_tile(q, k_ref[pl.ds(k0, SB_BK), lanes], mask) for lanes, q in zip(heads, qs)]
                sums = [_dot_tri(lom, upper) for _, lom in tiles]
                out = []
                for lanes, (acc, cr), (lb, lom), cs in zip(heads, c, tiles, sums):
                    a = jnp.exp(lb + (cs + cr))
                    if mask is not None:
                        a = jnp.where(mask, a, 0.0)
                    rsum = cs[:, 0:1] + lom[:, 0:1]
                    out.append((acc + _dot(a.astype(BF16), v_ref[pl.ds(k0, SB_BK), lanes]), cr + rsum))
                return tuple(out)

            c = tuple((jnp.zeros((bq, HD), F32), jnp.zeros((bq, 1), F32)) for _ in heads)
            for d in reversed(range(ndiag)):
                c = step(pl.multiple_of(q0 + d * SB_BK, SB_BK), c, _sb_diag_mask(bq, d))
            c = _sb_past_blocks(step, c, qi * ndiag)
            for lanes, (acc, _) in zip(heads, c):
                o_ref[pl.ds(q0, bq), lanes] = acc
            return carry

        lax.fori_loop(0, nq, qloop, 0)

    def spec(c0):
        return pl.BlockSpec((None, seq, 128), lambda b, p: (b, 0, c0 + p))

    grid = (nb, BW // 128)
    body, ex_in, ex_out, ex_shape, ex_sems = _host_exchange(body, 3, 1, grid, gather, False)
    outs = pl.pallas_call(
        body, name="sb_fwd", grid=grid,
        in_specs=[spec(8), spec(12), spec(16)] + ex_in,
        out_specs=[spec(0)] + ex_out,
        out_shape=[jax.ShapeDtypeStruct((nb, seq, BW), F32)] + ex_shape,
        scratch_shapes=ex_sems,
        compiler_params=_cp(("arbitrary", "arbitrary")),
    )(proj3, proj3, proj3, *gather)
    return outs[0], outs[1:]


def _sb_bwd(proj3, do3, o3, scatter=()):
    nb, seq, _ = proj3.shape
    bq = min(SB_BQ, seq)
    nq = seq // bq
    ndiag = bq // SB_BK

    def body(q_ref, k_ref, v_ref, do_ref, o_ref, dq_ref, dk_ref, dv_ref, dk_acc, dv_acc):
        row = lax.broadcasted_iota(jnp.int32, (SB_BK, SB_BK), 0)
        col = lax.broadcasted_iota(jnp.int32, (SB_BK, SB_BK), 1)
        upper = jnp.where(row > col, 1.0, 0.0).astype(BF16)
        later = jnp.where(row >= col, 1.0, 0.0).astype(BF16)
        dk_acc[...] = jnp.zeros_like(dk_acc)
        dv_acc[...] = jnp.zeros_like(dv_acc)
        heads = [slice(hh * HD, (hh + 1) * HD) for hh in range(2)]

        def qloop(qi, carry):
            q0 = pl.multiple_of(qi * bq, bq)
            qs = [_sb_scaled(q_ref[pl.ds(q0, bq), lanes]) for lanes in heads]
            dos = [do_ref[pl.ds(q0, bq), lanes] for lanes in heads]
            gtot = [jnp.sum(do.astype(F32) * o_ref[pl.ds(q0, bq), lanes], axis=1, keepdims=True)
                    for do, lanes in zip(dos, heads)]

            def step(k0, c, mask):
                ks = [k_ref[pl.ds(k0, SB_BK), lanes] for lanes in heads]
                tiles = [_sb_tile(q, k, mask) for q, k in zip(qs, ks)]
                sums = [_dot_tri(lom, upper) for _, lom in tiles]
                das = [_dot(do, v_ref[pl.ds(k0, SB_BK), lanes], NT) for do, lanes in zip(dos, heads)]
                gls, avs = [], []
                for hh, (_, cr, _) in enumerate(c):
                    a = jnp.exp(tiles[hh][0] + (sums[hh] + cr))
                    if mask is not None:
                        a = jnp.where(mask, a, 0.0)
                    ab = a.astype(BF16)
                    avs.append(ab)
                    gls.append(das[hh] * ab.astype(F32))
                tails = [_dot_tri2(gl, later) for gl in gls]
                out = []
                for hh, (dq, cr, gdone) in enumerate(c):
                    lb, lom = tiles[hh]
                    pre = gtot[hh] - gdone - tails[hh]
                    dz = gls[hh] - jnp.exp(lb) * (gls[hh] + pre)
                    if mask is not None:
                        dz = jnp.where(mask, dz, 0.0)
                    dz = dz.astype(BF16)
                    dk_acc[hh, pl.ds(k0, SB_BK), :] += _dot(dz, qs[hh], TN)
                    dv_acc[hh, pl.ds(k0, SB_BK), :] += _dot(avs[hh], dos[hh], TN)
                    rsum = sums[hh][:, 0:1] + lom[:, 0:1]
                    out.append((dq + _dot(dz, ks[hh]), cr + rsum, gdone + tails[hh][:, 0:1]))
                return tuple(out)

            c = tuple((jnp.zeros((bq, HD), F32), jnp.zeros((bq, 1), F32), jnp.zeros((bq, 1), F32))
                      for _ in heads)
            for d in reversed(range(ndiag)):
                c = step(pl.multiple_of(q0 + d * SB_BK, SB_BK), c, _sb_diag_mask(bq, d))
            c = _sb_past_blocks(step, c, qi * ndiag)
            for lanes, (dq, _, _) in zip(heads, c):
                dq_ref[pl.ds(q0, bq), lanes] = (dq * SB_SCALE).astype(BF16)
            return carry

        lax.fori_loop(0, nq, qloop, 0)
        for hh in range(2):
            lanes = slice(hh * HD, (hh + 1) * HD)
            dk_ref[:, lanes] = dk_acc[hh].astype(BF16)
            dv_ref[:, lanes] = dv_acc[hh].astype(BF16)

    def spec(c0):
        return pl.BlockSpec((None, seq, 128), lambda b, p: (b, 0, c0 + p))

    grid = (nb, BW // 128)
    body, ex_in, ex_out, ex_shape, ex_sems = _host_exchange(body, 5, 3, grid, scatter, True)
    outs = pl.pallas_call(
        body, name="sb_bwd", grid=grid,
        in_specs=[spec(8), spec(12), spec(16), spec(0), spec(0)] + ex_in,
        out_specs=[spec(0), spec(0), spec(0)] + ex_out,
        out_shape=[jax.ShapeDtypeStruct((nb, seq, BW), BF16)] * 3 + ex_shape,
        scratch_shapes=[pltpu.VMEM((2, seq, HD), F32), pltpu.VMEM((2, seq, HD), F32)] + ex_sems,
        compiler_params=_cp(("arbitrary", "arbitrary")),
    )(proj3, proj3, proj3, do3, o3, *scatter)
    return outs[:3], outs[3:]


def _merge_fwd(brs, wb, proj):
    t = proj.shape[0]
    tm = _row_tile(t, 512)
    tn = 512
    nj = D // tn

    def body(b0, b1, b2, wb_ref, l0, l1, l2, m_ref, y0, y1, y2):
        acc = None
        for br, n, lg, y_ref in ((b0, 0, l0, y0), (b1, 1, l1, y1), (b2, 2, l2, y2)):
            y = _dot(br[...].astype(BF16), wb_ref[n])
            y_ref[...] = y.astype(BF16)
            term = jax.nn.sigmoid(lg[...].astype(F32)) * y
            acc = term if acc is None else acc + term
        m_ref[...] = acc.astype(BF16)

    def lspec(n):
        return pl.BlockSpec((tm, tn), lambda i, j: (i, (3 * D + n * D) // tn + j))

    tile = pl.BlockSpec((tm, tn), lambda i, j: (i, j))
    bspec = pl.BlockSpec((tm, BW), lambda i, j: (i, 0))
    return pl.pallas_call(
        body, name="merge_fwd", grid=(t // tm, nj),
        in_specs=[bspec, bspec, bspec, pl.BlockSpec((NB, BW, tn), lambda i, j: (0, 0, j)),
                  lspec(0), lspec(1), lspec(2)],
        out_specs=[tile] * 4,
        out_shape=[jax.ShapeDtypeStruct((t, D), BF16)] * 4,
        compiler_params=_cp(("parallel", "parallel")),
    )(brs[0], brs[1], brs[2], wb, proj, proj, proj)


def _merge_bwd(dm, ys, proj):
    t = proj.shape[0]
    tm = _row_tile(t, 512)
    tn = 512

    def body(dm_ref, y0, y1, y2, l0, l1, l2, dl0, dl1, dl2, dy0, dy1, dy2):
        dmv = dm_ref[...].astype(F32)
        for y_ref, lg, dl_ref, dy_ref in ((y0, l0, dl0, dy0), (y1, l1, dl1, dy1), (y2, l2, dl2, dy2)):
            g = jax.nn.sigmoid(lg[...].astype(F32))
            dl_ref[...] = (dmv * y_ref[...].astype(F32) * g * (1.0 - g)).astype(BF16)
            dy_ref[...] = (dmv * g).astype(BF16)

    def lspec(n):
        return pl.BlockSpec((tm, tn), lambda i, j: (i, (3 * D + n * D) // tn + j))

    tile = pl.BlockSpec((tm, tn), lambda i, j: (i, j))
    return pl.pallas_call(
        body, name="merge_bwd", grid=(t // tm, D // tn),
        in_specs=[tile] * 4 + [lspec(0), lspec(1), lspec(2)],
        out_specs=[tile] * 6,
        out_shape=[jax.ShapeDtypeStruct((t, D), BF16)] * 6,
        compiler_params=_cp(("parallel", "parallel")),
    )(dm, ys[0], ys[1], ys[2], proj, proj, proj)


def _adamw_rows(rows):
    if rows <= 512:
        return rows
    return next(tr for tr in (512, 384, 352, 256, 128, 64, 32, 16, 8) if rows % tr == 0)


def _adamw_math(npart, p_ref, w_ref, m_ref, v_ref, g_ref, d_ref, mo_ref, vo_ref):
    c1 = 1.0 - ADAM_B1 ** ADAM_STEP
    c2 = 1.0 - ADAM_B2 ** ADAM_STEP
    g = p_ref[0].astype(F32)
    for p in range(1, npart):
        g = g + p_ref[p].astype(F32)
    mn = ADAM_B1 * m_ref[...] + (1.0 - ADAM_B1) * g
    vn = ADAM_B2 * v_ref[...] + (1.0 - ADAM_B2) * (g * g)
    m_hat = mn / c1
    v_hat = vn / c2
    g_ref[...] = g
    d_ref[...] = -ADAM_LR * (m_hat / (jnp.sqrt(v_hat) + ADAM_EPS) + ADAM_WD * w_ref[...])
    mo_ref[...] = mn
    vo_ref[...] = vn


def _adamw_layer(name, parts, w, m, v, layer, bufs):
    nl, cols = w.shape[0], w.shape[-1]
    rows = int(math.prod(w.shape[1:-1]))
    npart = parts.shape[0]
    tr = _adamw_rows(rows)
    if bufs is None:
        bufs = [lax.empty((nl, rows, cols), F32) for _ in range(4)]

    def body(p_ref, w_ref, m_ref, v_ref, b0, b1, b2, b3, g_ref, d_ref, mo_ref, vo_ref):
        _adamw_math(npart, p_ref, w_ref, m_ref, v_ref, g_ref, d_ref, mo_ref, vo_ref)

    slab = pl.BlockSpec((None, tr, cols), lambda i: (layer, i, 0))
    sds = jax.ShapeDtypeStruct((nl, rows, cols), F32)
    return pl.pallas_call(
        body, name=name, grid=(rows // tr,),
        in_specs=[pl.BlockSpec((npart, tr, cols), lambda i: (0, i, 0)), slab, slab, slab] + [_HBM] * 4,
        out_specs=[slab] * 4, out_shape=[sds] * 4,
        input_output_aliases={4: 0, 5: 1, 6: 2, 7: 3},
        compiler_params=_cp(("parallel",)),
    )(parts.reshape(npart, rows, cols), w.reshape(nl, rows, cols), m.reshape(nl, rows, cols),
      v.reshape(nl, rows, cols), *bufs)


def _adamw_reduce(name, parts, w, m, v):
    shape = w.shape
    cols = shape[-1]
    rows = int(math.prod(shape[:-1])) if len(shape) > 1 else 1
    npart = parts.shape[0]
    tr = _adamw_rows(rows)

    def body(p_ref, w_ref, m_ref, v_ref, g_ref, d_ref, mo_ref, vo_ref):
        _adamw_math(npart, p_ref, w_ref, m_ref, v_ref, g_ref, d_ref, mo_ref, vo_ref)

    tile = pl.BlockSpec((tr, cols), lambda i: (i, 0))
    sds = jax.ShapeDtypeStruct((rows, cols), F32)
    outs = pl.pallas_call(
        body, name=name, grid=(rows // tr,),
        in_specs=[pl.BlockSpec((npart, tr, cols), lambda i: (0, i, 0)), tile, tile, tile],
        out_specs=[tile] * 4, out_shape=[sds] * 4,
        compiler_params=_cp(("parallel",)),
    )(parts.reshape(npart, rows, cols), w.reshape(rows, cols), m.reshape(rows, cols), v.reshape(rows, cols))
    return tuple(o.reshape(shape) for o in outs)


def _pad_ffn_in(w):
    lead = w.shape[:-1]
    w = w.reshape(lead + (2, FF_HALF))
    w = jnp.pad(w, [(0, 0)] * len(lead) + [(0, 0), (0, FF_HALF_PAD - FF_HALF)])
    return w.reshape(lead + (FF_IN_PAD,))


def kernel(x, c, rms_g1, rms_g2, w_ada, b_ada, w_in, gm_ln_g, gm_ln_b, gm_w_spatial, gm_b_spatial, pool_w, pool_scale, w_branch, w_out, w_ffn_in, w_ffn_out, final_g, loss_target, m_rms_g1, m_rms_g2, m_w_ada, m_b_ada, m_w_in, m_gm_ln_g, m_gm_ln_b, m_gm_w_spatial, m_gm_b_spatial, m_pool_w, m_pool_scale, m_w_branch, m_w_out, m_w_ffn_in, m_w_ffn_out, m_final_g, v_rms_g1, v_rms_g2, v_w_ada, v_b_ada, v_w_in, v_gm_ln_g, v_gm_ln_b, v_gm_w_spatial, v_gm_b_spatial, v_pool_w, v_pool_scale, v_w_branch, v_w_out, v_w_ffn_in, v_w_ffn_out, v_final_g):
    nb, seq, _ = x.shape
    nl = w_in.shape[0]
    t = nb * seq
    ntot = NDEV * nb
    me = _my_index()
    assert x.shape[2] == D and w_in.shape[1:] == (D, 768) and w_ffn_in.shape[1:] == (D, FF_IN_SHARD)
    assert seq % CH == 0

    w_ffn_in_p = _pad_ffn_in(w_ffn_in).astype(BF16)
    w_ffn_out_p = jnp.pad(w_ffn_out, ((0, 0), (0, FF_HALF_PAD - FF_HALF), (0, 0))).astype(BF16)
    w_in_b = w_in.astype(BF16)
    w_branch_b = w_branch.astype(BF16)
    w_out_b = w_out.astype(BF16)
    (g_in_next,) = _exchange([w_in_b[0]], "gather_w_in0", False)

    (c_all,) = _exchange([c], "gather_c", False)
    c_all = c_all.reshape(ntot, D)
    b_blk = lax.dynamic_slice_in_dim(b_ada, me * 768, 768, axis=1).reshape(nl, 1, 768)
    mod_blk = _ada_fwd(c_all, w_ada, b_blk)
    (mod_all,) = _exchange([mod_blk], "gather_mod", False)
    mod_all = jnp.transpose(mod_all, (1, 2, 0, 3)).reshape(nl, ntot, NMOD * D)
    mod = lax.dynamic_slice_in_dim(mod_all, me * nb, nb, axis=1).reshape(nl, nb, NMOD, 1, D)

    saved = []
    gathered = []
    xc = x
    for l in range(nl):
        sh1, sc1, gt1, sh2, sc2, gt2 = [mod[l, :, i] for i in range(NMOD)]
        h = _norm_mod_fwd(xc, rms_g1[l].reshape(1, D), sc1, sh1).reshape(t, D)
        proj, (g_ffn_in_w,) = _mm_colblocked("proj_fwd", h, g_in_next, BF16, [w_ffn_in_p[l]])
        proj3 = proj.reshape(nb, seq, IN_COLS)
        br_gm = _gmlp_fwd(proj, gm_ln_g[l].reshape(1, BW), gm_ln_b[l].reshape(1, BW),
                          gm_w_spatial[l], gm_b_spatial[l].T)
        sb_o, got = _sb_fwd(proj3, [w_branch_b[l], w_out_b[l], w_ffn_out_p[l]])
        gw = dict(w_in=g_in_next,
                  w_branch=jnp.transpose(got[0], (1, 2, 0, 3)).reshape(NB, BW, D),
                  w_out=got[1].reshape(D, D),
                  w_ffn_in=g_ffn_in_w,
                  w_ffn_out=got[2].reshape(FFP, D))
        gathered.append(gw)
        br_pool = _pool_fwd(proj3, pool_w[l], pool_scale[l].reshape(1, BW))
        brs = [br_gm, sb_o.reshape(t, BW), br_pool.reshape(t, BW)]
        merged, y0, y1, y2 = _merge_fwd(brs, gw["w_branch"], proj)
        x_mid, mo = _mm_residual("out_fwd", merged, gw["w_out"], xc.reshape(t, D), gt1, seq)
        x_mid = x_mid.reshape(nb, seq, D)
        h2 = _norm_mod_fwd(x_mid, rms_g2[l].reshape(1, D), sc2, sh2).reshape(t, D)
        fg, fu, act, got = _ffn_in_fwd(h2, gw["w_ffn_in"], [w_in_b[l + 1]] if l + 1 < nl else [])
        if l + 1 < nl:
            g_in_next = got[0]
        x_out, fo = _mm_residual("ffn_out_fwd", act, gw["w_ffn_out"], x_mid.reshape(t, D), gt2, seq)
        saved.append(dict(x_in=xc, h=h, proj=proj, brs=brs, sb_o=sb_o, ys=(y0, y1, y2), merged=merged,
                          mo=mo, x_mid=x_mid, h2=h2, fg=fg, fu=fu, act=act, fo=fo))
        xc = x_out.reshape(nb, seq, D)

    dx, loss_part, dfinal_part = _loss_head(xc, loss_target, final_g.reshape(1, D))
    loss = lax.psum(jnp.sum(loss_part[:, 0, 0]), ("x", "y", "c"))

    big_names = ("w_in", "w_branch", "w_out", "w_ffn_in", "w_ffn_out")
    bufs = {name: None for name in big_names}
    w_ffn_in_t, m_w_ffn_in_t, v_w_ffn_in_t = [jnp.swapaxes(a, 1, 2) for a in (w_ffn_in, m_w_ffn_in, v_w_ffn_in)]
    small_parts = {k: [None] * nl for k in ("rms_g1", "rms_g2", "gm_ln_g", "gm_ln_b", "gm_w_spatial",
                                            "gm_b_spatial", "pool_w", "pool_scale")}
    dmod = [None] * nl
    for l in reversed(range(nl)):
        gw = gathered[l]
        sv = saved[l]
        sh1, sc1, gt1, sh2, sc2, gt2 = [mod[l, :, i] for i in range(NMOD)]
        dfo, dgt2 = _gate_bwd(dx, sv["fo"].reshape(nb, seq, D), gt2)
        dfo = dfo.reshape(t, D)
        g_ffn_out = _mm_tn("ffn_out_wgrad", sv["act"], dfo)
        dfg, dfu = _ffn_out_dgrad(dfo, gw["w_ffn_out"], sv["fg"], sv["fu"])
        dh2 = _ffn_in_dgrad(dfg, dfu, gw["w_ffn_in"])
        g_ffn_in = _ffn_in_wgrad(sv["h2"], dfg, dfu)
        dx_mid, dsh2, dsc2, dg2 = _norm_mod_bwd(sv["x_mid"], dh2.reshape(nb, seq, D), dx,
                                                rms_g2[l].reshape(1, D), sc2)
        dmo, dgt1 = _gate_bwd(dx_mid, sv["mo"].reshape(nb, seq, D), gt1)
        dmo = dmo.reshape(t, D)
        dmerged = _mm_nt("out_dgrad", dmo, gw["w_out"], BF16)
        g_out = _mm_tn("out_wgrad", sv["merged"], dmo)
        dls_dys = _merge_bwd(dmerged, sv["ys"], sv["proj"])
        dls, dys = dls_dys[:3], dls_dys[3:]
        dbrs, g_br = [], []
        for n in range(NB):
            dbrs.append(_mm_nt("branch_dgrad", dys[n], gw["w_branch"], BF16, w_lead=n))
            g_br.append(_mm_tn("branch_wgrad", sv["brs"][n], dys[n]))
        proj3 = sv["proj"].reshape(nb, seq, IN_COLS)
        d_gm, g_ws, g_bs, g_lg, g_lb = _gmlp_bwd(sv["proj"], dbrs[0], gm_ln_g[l].reshape(1, BW),
                                                 gm_ln_b[l].reshape(1, BW), gm_w_spatial[l], gm_b_spatial[l].T)
        g_br_dev = jnp.transpose(jnp.stack(g_br).reshape(NB, BW, NDEV, D // NDEV), (2, 0, 1, 3))
        carried = [g_br_dev, g_out.reshape(NDEV, D // NDEV, D), g_ffn_in, g_ffn_out.reshape(NDEV, FF_HALF_PAD, D)]
        d_sb, recv = _sb_bwd(proj3, dbrs[1].reshape(nb, seq, BW), sv["sb_o"], carried)
        bufs["w_branch"] = _adamw_layer("adamw_w_branch", recv[0], w_branch, m_w_branch, v_w_branch, l,
                                        bufs["w_branch"])
        bufs["w_out"] = _adamw_layer("adamw_w_out", recv[1], w_out, m_w_out, v_w_out, l, bufs["w_out"])
        r_fi = recv[2].reshape(NDEV, 2, FF_HALF_PAD, D)[:, :, :FF_HALF].reshape(NDEV, FF_IN_SHARD, D)
        bufs["w_ffn_in"] = _adamw_layer("adamw_w_ffn_in", r_fi, w_ffn_in_t, m_w_ffn_in_t, v_w_ffn_in_t, l,
                                        bufs["w_ffn_in"])
        bufs["w_ffn_out"] = _adamw_layer("adamw_w_ffn_out", recv[3][:, :FF_HALF], w_ffn_out, m_w_ffn_out,
                                         v_w_ffn_out, l, bufs["w_ffn_out"])
        d_pool, g_pw, g_ps = _pool_bwd(proj3, dbrs[2].reshape(nb, seq, BW), pool_w[l], pool_scale[l].reshape(1, BW))
        dproj = jnp.concatenate([d_gm] + [a.reshape(t, BW) for a in d_sb] + [d_pool.reshape(t, BW)] + list(dls),
                                axis=1)
        g_in = _mm_colblocked_tn("proj_wgrad", sv["h"], dproj)
        dh, (r_in,) = _mm_colblocked_nt("proj_dgrad", dproj, gw["w_in"], F32, [g_in])
        bufs["w_in"] = _adamw_layer("adamw_w_in", r_in, w_in, m_w_in, v_w_in, l, bufs["w_in"])
        dx, dsh1, dsc1, dg1 = _norm_mod_bwd(sv["x_in"], dh.reshape(nb, seq, D), dx_mid,
                                            rms_g1[l].reshape(1, D), sc1)

        dmod[l] = jnp.concatenate([dsh1, dsc1, dgt1, dsh2, dsc2, dgt2], axis=-1)
        small_parts["rms_g1"][l] = jnp.sum(dg1, axis=0)
        small_parts["rms_g2"][l] = jnp.sum(dg2, axis=0)
        small_parts["gm_ln_g"][l] = g_lg
        small_parts["gm_ln_b"][l] = g_lb
        small_parts["gm_w_spatial"][l] = g_ws
        small_parts["gm_b_spatial"][l] = g_bs[:, :, 0]
        small_parts["pool_w"][l] = g_pw
        small_parts["pool_scale"][l] = g_ps

    dmod_mine = jnp.stack(dmod).reshape(nl, nb, NMOD * D)
    names = list(small_parts)
    stacked = [jnp.stack(small_parts[k]).astype(BF16 if k in ("gm_w_spatial", "pool_w") else F32) for k in names]
    gathered_small = _exchange(stacked + [dfinal_part, dmod_mine], "gather_small", False)
    dmod_all = jnp.transpose(gathered_small[-1], (1, 0, 2, 3)).reshape(nl, ntot, NMOD * D)
    dfinal_all = gathered_small[-2].reshape(ntot, D)

    results = {}
    weights = dict(rms_g1=(rms_g1, m_rms_g1, v_rms_g1), rms_g2=(rms_g2, m_rms_g2, v_rms_g2),
                   gm_ln_g=(gm_ln_g, m_gm_ln_g, v_gm_ln_g), gm_ln_b=(gm_ln_b, m_gm_ln_b, v_gm_ln_b),
                   gm_w_spatial=(gm_w_spatial, m_gm_w_spatial, v_gm_w_spatial),
                   gm_b_spatial=(gm_b_spatial, m_gm_b_spatial, v_gm_b_spatial),
                   pool_w=(pool_w, m_pool_w, v_pool_w), pool_scale=(pool_scale, m_pool_scale, v_pool_scale))
    for k, parts in zip(names, gathered_small[:len(names)]):
        w, m, v = weights[k]
        results[k] = _adamw_reduce("adamw_" + k, parts.reshape((NDEV,) + w.shape), w, m, v)
    results["final_g"] = _adamw_reduce("adamw_final_g", dfinal_all, final_g, m_final_g, v_final_g)
    results["b_ada"] = _adamw_reduce("adamw_b_ada", jnp.transpose(dmod_all, (1, 0, 2)), b_ada, m_b_ada, v_b_ada)
    dmod_blk = lax.dynamic_slice_in_dim(dmod_all, me * 768, 768, axis=2)
    g_w_ada = _ada_bwd(c_all, dmod_blk)
    results["w_ada"] = _adamw_reduce("adamw_w_ada", g_w_ada[None], w_ada, m_w_ada, v_w_ada)
    stacked_w = dict(w_in=w_in, w_branch=w_branch, w_out=w_out, w_ffn_in=w_ffn_in_t, w_ffn_out=w_ffn_out)
    for name in big_names:
        results[name] = tuple(b.reshape(stacked_w[name].shape) for b in bufs[name])
    results["w_ffn_in"] = tuple(jnp.swapaxes(b, 1, 2) for b in results["w_ffn_in"])

    order = ["rms_g1", "rms_g2", "w_ada", "b_ada", "w_in", "gm_ln_g", "gm_ln_b", "gm_w_spatial", "gm_b_spatial",
             "pool_w", "pool_scale", "w_branch", "w_out", "w_ffn_in", "w_ffn_out", "final_g"]
    out = [loss, dx]
    for i in range(4):
        out.extend(results[k][i] for k in order)
    return tuple(out)
```

```python
import functools
import math

import jax
import jax.numpy as jnp
from jax import lax
from jax.experimental import pallas as pl
from jax.experimental.pallas import tpu as pltpu

F32 = jnp.float32
BF16 = jnp.bfloat16
MESH = pl.DeviceIdType.MESH

D = 1024
BW = 512
NB = 3
CH = 128
NG = 4
HD = 64
POOL_WINDOWS = (2, 4, 8, 16)
DFF = 2816
NMOD = 6
EPS = 1e-6
IN_COLS = 6 * D
NDEV = 8
FF_IN_SHARD = 2 * DFF // NDEV
FF_HALF = FF_IN_SHARD // 2
FF_HALF_PAD = 384
FF_IN_PAD = 2 * FF_HALF_PAD
FFP = NDEV // 2 * FF_IN_PAD

ADAM_LR = 0.001
ADAM_B1 = 0.9
ADAM_B2 = 0.999
ADAM_EPS = 1e-08
ADAM_WD = 0.01
ADAM_STEP = 10

VMEM_LIMIT = 48 * 1024 * 1024
BIG_ROWS = 2048

NN = (((1,), (0,)), ((), ()))
NT = (((1,), (1,)), ((), ()))
TN = (((0,), (0,)), ((), ()))


def _cp(sem=None):
    return pltpu.CompilerParams(dimension_semantics=sem, vmem_limit_bytes=VMEM_LIMIT)


def _dot(a, b, dims=NN):
    return lax.dot_general(a, b, dims, preferred_element_type=F32)


def _my_index():
    return 4 * lax.axis_index("x") + 2 * lax.axis_index("y") + lax.axis_index("c")


def _peer(k):
    x, y, c = lax.axis_index("x"), lax.axis_index("y"), lax.axis_index("c")
    px = 1 - x if k & 4 else x
    py = 1 - y if k & 2 else y
    pc = 1 - c if k & 1 else c
    return (px, py, pc), 4 * px + 2 * py + pc


def _exchange(xs, name, all_to_all):
    n = len(xs)

    def body(*refs):
        _exchange_start(refs[:n], refs[n:2 * n], refs[2 * n:], all_to_all)
        _exchange_finish(refs[:n], refs[n:2 * n], refs[2 * n:], all_to_all)

    return pl.pallas_call(
        body, name=name, out_shape=_exchange_out_shape(xs, all_to_all),
        in_specs=[_HBM] * n, out_specs=[_HBM] * n, scratch_shapes=_exchange_sems(n),
    )(*xs)


_HBM = pl.BlockSpec(memory_space=pl.ANY)


def _exchange_out_shape(xs, all_to_all):
    if all_to_all:
        return [jax.ShapeDtypeStruct(x.shape, x.dtype) for x in xs]
    return [jax.ShapeDtypeStruct((NDEV,) + x.shape, x.dtype) for x in xs]


def _exchange_sems(n):
    return [pltpu.SemaphoreType.DMA((n * 7,)), pltpu.SemaphoreType.DMA((n * 7,)), pltpu.SemaphoreType.DMA((n,))]


def _exchange_copies(ins, outs, sems, all_to_all):
    send_sems, recv_sems, local_sems = sems
    me = _my_index()
    local, sends, recvs = [], [], []
    for a in range(len(ins)):
        src = ins[a].at[me] if all_to_all else ins[a]
        local.append(pltpu.make_async_copy(src, outs[a].at[me], local_sems.at[a]))
    for k in range(1, NDEV):
        dev, idx = _peer(k)
        for a in range(len(ins)):
            src = ins[a].at[idx] if all_to_all else ins[a]
            sem = dict(send_sem=send_sems.at[a * 7 + k - 1], recv_sem=recv_sems.at[a * 7 + k - 1],
                       device_id=dev, device_id_type=MESH)
            sends.append(pltpu.make_async_remote_copy(src_ref=src, dst_ref=outs[a].at[me], **sem))
            recvs.append(pltpu.make_async_remote_copy(src_ref=src, dst_ref=outs[a].at[idx], **sem))
    return local, sends, recvs


def _exchange_start(ins, outs, sems, all_to_all):
    local, sends, _ = _exchange_copies(ins, outs, sems, all_to_all)
    for cp in local + sends:
        cp.start()


def _exchange_finish(ins, outs, sems, all_to_all):
    local, sends, recvs = _exchange_copies(ins, outs, sems, all_to_all)
    for cp in sends:
        cp.wait_send()
    for cp in recvs:
        cp.wait_recv()
    for cp in local:
        cp.wait()


def _host_exchange(body, n_in, n_out, grid, xs, all_to_all):
    n = len(xs)
    if n == 0:
        return body, [], [], [], []

    def hosted(*refs):
        ins, ex_ins = refs[:n_in], refs[n_in:n_in + n]
        outs, ex_outs = refs[n_in + n:n_in + n + n_out], refs[n_in + n + n_out:n_in + 2 * n + n_out]
        scratch = refs[n_in + 2 * n + n_out:]
        own, sems = scratch[:len(scratch) - 3], scratch[len(scratch) - 3:]
        first = functools.reduce(jnp.logical_and, [pl.program_id(a) == 0 for a in range(len(grid))])
        last = functools.reduce(jnp.logical_and, [pl.program_id(a) == grid[a] - 1 for a in range(len(grid))])

        @pl.when(first)
        def _():
            _exchange_start(ex_ins, ex_outs, sems, all_to_all)

        body(*ins, *outs, *own)

        @pl.when(last)
        def _():
            _exchange_finish(ex_ins, ex_outs, sems, all_to_all)

    return hosted, [_HBM] * n, [_HBM] * n, _exchange_out_shape(xs, all_to_all), _exchange_sems(n)


def _mm(name, a, b, grid, a_spec, b_spec, o_spec, out_sds, dims, acc_shape, carried=(), all_to_all=True):
    nk = grid[2]

    if nk == 1:
        def body(a_ref, b_ref, o_ref):
            o_ref[...] = _dot(a_ref[...].astype(BF16), b_ref[...].astype(BF16), dims).astype(o_ref.dtype)
        scratch = []
    else:
        def body(a_ref, b_ref, o_ref, acc_ref):
            k = pl.program_id(2)

            @pl.when(k == 0)
            def _():
                acc_ref[...] = jnp.zeros_like(acc_ref)

            acc_ref[...] += _dot(a_ref[...].astype(BF16), b_ref[...].astype(BF16), dims)

            @pl.when(k == nk - 1)
            def _():
                o_ref[...] = acc_ref[...].astype(o_ref.dtype)
        scratch = [pltpu.VMEM(acc_shape, F32)]

    body, ex_in, ex_out, ex_shape, ex_sems = _host_exchange(body, 2, 1, grid, carried, all_to_all)
    outs = pl.pallas_call(
        body, name=name, grid=grid, in_specs=[a_spec, b_spec] + ex_in, out_specs=[o_spec] + ex_out,
        out_shape=[out_sds] + ex_shape,
        scratch_shapes=scratch + ex_sems,
        compiler_params=_cp(("arbitrary",) * 3 if carried else ("parallel", "parallel", "arbitrary")),
    )(a, b, *carried)
    return (outs[0], outs[1:]) if carried else outs[0]


def _row_tile(t, want):
    tm = min(t, want)
    assert t % tm == 0
    return tm


def _mm_colblocked(name, a, wg, out_dtype, gather=()):
    t = a.shape[0]
    tm = _row_tile(t, BIG_ROWS)
    return _mm(name, a, wg, (t // tm, NDEV, 1),
               pl.BlockSpec((tm, D), lambda i, j, k: (i, 0)),
               pl.BlockSpec((None, D, 768), lambda i, j, k: (j, 0, 0)),
               pl.BlockSpec((tm, 768), lambda i, j, k: (i, j)),
               jax.ShapeDtypeStruct((t, NDEV * 768), out_dtype), NN, (tm, 768), gather, False)


def _mm_colblocked_nt(name, g, wg, out_dtype, scatter=()):
    t = g.shape[0]
    tm = _row_tile(t, BIG_ROWS)
    return _mm(name, g, wg, (t // tm, 1, NDEV),
               pl.BlockSpec((tm, 768), lambda i, j, k: (i, k)),
               pl.BlockSpec((None, D, 768), lambda i, j, k: (k, 0, 0)),
               pl.BlockSpec((tm, D), lambda i, j, k: (i, 0)),
               jax.ShapeDtypeStruct((t, D), out_dtype), NT, (tm, D), scatter)


_HALF = NDEV // 2


def _ffn_in_fwd(h2, wg, gather=()):
    t = h2.shape[0]
    tm = _row_tile(t, 1024)

    def body(a_ref, wg_ref, wu_ref, g_ref, u_ref, act_ref):
        a = a_ref[...]
        g = _dot(a, wg_ref[...])
        u = _dot(a, wu_ref[...])
        g_ref[...] = g.astype(BF16)
        u_ref[...] = u.astype(BF16)
        act_ref[...] = (g * jax.nn.sigmoid(g) * u).astype(BF16)

    tile = pl.BlockSpec((tm, 768), lambda i, j: (i, j))
    grid = (t // tm, _HALF)
    body, ex_in, ex_out, ex_shape, ex_sems = _host_exchange(body, 3, 3, grid, gather, False)
    outs = pl.pallas_call(
        body, name="ffn_in_fwd", grid=grid,
        in_specs=[pl.BlockSpec((tm, D), lambda i, j: (i, 0)),
                  pl.BlockSpec((None, D, 768), lambda i, j: (j, 0, 0)),
                  pl.BlockSpec((None, D, 768), lambda i, j: (j + _HALF, 0, 0))] + ex_in,
        out_specs=[tile] * 3 + ex_out, out_shape=[jax.ShapeDtypeStruct((t, FFP), BF16)] * 3 + ex_shape,
        scratch_shapes=ex_sems,
        compiler_params=_cp(("arbitrary", "arbitrary") if gather else ("parallel", "parallel")),
    )(h2, wg, wg, *gather)
    return outs[0], outs[1], outs[2], outs[3:]


def _ffn_out_dgrad(dfo, w, fg, fu):
    t = dfo.shape[0]
    tm = _row_tile(t, 1024)

    def body(a_ref, w_ref, g_ref, u_ref, dg_ref, du_ref):
        d = _dot(a_ref[...], w_ref[...], NT)
        g = g_ref[...].astype(F32)
        s = jax.nn.sigmoid(g)
        dg_ref[...] = (d * u_ref[...].astype(F32) * (s * (1.0 + g * (1.0 - s)))).astype(BF16)
        du_ref[...] = (d * (g * s)).astype(BF16)

    tile = pl.BlockSpec((tm, 768), lambda i, j: (i, j))
    return pl.pallas_call(
        body, name="ffn_out_dgrad", grid=(t // tm, _HALF),
        in_specs=[pl.BlockSpec((tm, D), lambda i, j: (i, 0)),
                  pl.BlockSpec((768, D), lambda i, j: (j, 0)), tile, tile],
        out_specs=[tile] * 2, out_shape=[jax.ShapeDtypeStruct((t, FFP), BF16)] * 2,
        compiler_params=_cp(("parallel", "parallel")),
    )(dfo, w, fg, fu)


def _ffn_in_dgrad(dg, du, wg):
    t = dg.shape[0]
    tm = _row_tile(t, BIG_ROWS)

    def body(g_ref, u_ref, w_ref, o_ref, acc_ref):
        k = pl.program_id(1)

        @pl.when(k == 0)
        def _():
            acc_ref[...] = jnp.zeros_like(acc_ref)

        @pl.when(k < _HALF)
        def _():
            acc_ref[...] += _dot(g_ref[...], w_ref[...], NT)

        @pl.when(k >= _HALF)
        def _():
            acc_ref[...] += _dot(u_ref[...], w_ref[...], NT)

        @pl.when(k == NDEV - 1)
        def _():
            o_ref[...] = acc_ref[...]

    return pl.pallas_call(
        body, name="ffn_in_dgrad", grid=(t // tm, NDEV),
        in_specs=[pl.BlockSpec((tm, 768), lambda i, k: (i, jnp.minimum(k, _HALF - 1))),
                  pl.BlockSpec((tm, 768), lambda i, k: (i, jnp.maximum(k - _HALF, 0))),
                  pl.BlockSpec((None, D, 768), lambda i, k: (k, 0, 0))],
        out_specs=pl.BlockSpec((tm, D), lambda i, k: (i, 0)),
        out_shape=jax.ShapeDtypeStruct((t, D), F32),
        scratch_shapes=[pltpu.VMEM((tm, D), F32)],
        compiler_params=_cp(("parallel", "arbitrary")),
    )(dg, du, wg)


def _ffn_in_wgrad(h2, dg, du):
    t = h2.shape[0]
    tk = _row_tile(t, BIG_ROWS)
    nk = t // tk

    def body(a_ref, g_ref, u_ref, o_ref, acc_ref):
        j, k = pl.program_id(0), pl.program_id(1)

        @pl.when(k == 0)
        def _():
            acc_ref[...] = jnp.zeros_like(acc_ref)

        @pl.when(j < _HALF)
        def _():
            acc_ref[...] += _dot(g_ref[...], a_ref[...], TN)

        @pl.when(j >= _HALF)
        def _():
            acc_ref[...] += _dot(u_ref[...], a_ref[...], TN)

        @pl.when(k == nk - 1)
        def _():
            o_ref[...] = acc_ref[...].astype(BF16)

    return pl.pallas_call(
        body, name="ffn_in_wgrad", grid=(NDEV, nk),
        in_specs=[pl.BlockSpec((tk, D), lambda j, k: (k, 0)),
                  pl.BlockSpec((tk, 768), lambda j, k: (jnp.where(j < _HALF, k, 0), jnp.minimum(j, _HALF - 1))),
                  pl.BlockSpec((tk, 768), lambda j, k: (jnp.where(j < _HALF, 0, k), jnp.maximum(j - _HALF, 0)))],
        out_specs=pl.BlockSpec((None, 768, D), lambda j, k: (j, 0, 0)),
        out_shape=jax.ShapeDtypeStruct((NDEV, 768, D), BF16),
        scratch_shapes=[pltpu.VMEM((768, D), F32)],
        compiler_params=_cp(("parallel", "arbitrary")),
    )(h2, dg, du)


def _mm_colblocked_tn(name, a, g):
    t = a.shape[0]
    tk = _row_tile(t, BIG_ROWS)
    return _mm(name, a, g, (1, NDEV, t // tk),
               pl.BlockSpec((tk, D), lambda i, j, k: (k, 0)),
               pl.BlockSpec((tk, 768), lambda i, j, k: (k, j)),
               pl.BlockSpec((None, D, 768), lambda i, j, k: (j, 0, 0)),
               jax.ShapeDtypeStruct((NDEV, D, 768), BF16), TN, (D, 768))


def _mm_nt(name, a, w, out_dtype, a_col=0, w_lead=None):
    t = a.shape[0]
    if w_lead is None:
        kdim, n = w.shape
        b_spec = pl.BlockSpec((min(kdim, 1024), n), lambda i, j, k: (j, 0))
    else:
        _, kdim, n = w.shape
        b_spec = pl.BlockSpec((None, min(kdim, 1024), n), lambda i, j, k: (w_lead, j, 0))
    tn = min(kdim, 1024)
    tm = _row_tile(t, BIG_ROWS)
    return _mm(name, a, w, (t // tm, kdim // tn, 1),
               pl.BlockSpec((tm, n), lambda i, j, k: (i, a_col)),
               b_spec,
               pl.BlockSpec((tm, tn), lambda i, j, k: (i, j)),
               jax.ShapeDtypeStruct((t, kdim), out_dtype), NT, (tm, tn))


def _mm_tn(name, a, g, out_dtype=BF16):
    t, kdim = a.shape
    n = g.shape[1]
    tk = _row_tile(t, BIG_ROWS)
    tm = min(kdim, 1024)
    tn = min(n, 1024)
    return _mm(name, a, g, (kdim // tm, n // tn, t // tk),
               pl.BlockSpec((tk, tm), lambda i, j, k: (k, i)),
               pl.BlockSpec((tk, tn), lambda i, j, k: (k, j)),
               pl.BlockSpec((tm, tn), lambda i, j, k: (i, j)),
               jax.ShapeDtypeStruct((kdim, n), out_dtype), TN, (tm, tn))


def _mm_residual(name, a, w, x, gt, seq):
    t, kdim = a.shape
    tm = _row_tile(seq, 1024)
    tn = D
    tk = min(kdim, 1024)
    nk = kdim // tk
    per = seq // tm

    def body(a_ref, w_ref, x_ref, gt_ref, xo_ref, y_ref, acc_ref):
        k = pl.program_id(2)

        @pl.when(k == 0)
        def _():
            acc_ref[...] = jnp.zeros_like(acc_ref)

        acc_ref[...] += _dot(a_ref[...], w_ref[...])

        @pl.when(k == nk - 1)
        def _():
            y = acc_ref[...]
            xo_ref[...] = x_ref[...] + gt_ref[0] * y
            y_ref[...] = y.astype(BF16)

    return pl.pallas_call(
        body, name=name, grid=(t // tm, D // tn, nk),
        in_specs=[pl.BlockSpec((tm, tk), lambda i, j, k: (i, k)),
                  pl.BlockSpec((tk, tn), lambda i, j, k: (k, j)),
                  pl.BlockSpec((tm, tn), lambda i, j, k: (i, j)),
                  pl.BlockSpec((1, 1, tn), lambda i, j, k: (i // per, 0, j))],
        out_specs=[pl.BlockSpec((tm, tn), lambda i, j, k: (i, j)),
                   pl.BlockSpec((tm, tn), lambda i, j, k: (i, j))],
        out_shape=[jax.ShapeDtypeStruct((t, D), F32), jax.ShapeDtypeStruct((t, D), BF16)],
        scratch_shapes=[pltpu.VMEM((tm, tn), F32)],
        compiler_params=_cp(("parallel", "parallel", "arbitrary")),
    )(a, w, x, gt)


def _ada_fwd(c_all, w_ada, b_blk):
    nl = w_ada.shape[0]
    nb = c_all.shape[0]

    def body(c_ref, w_ref, b_ref, o_ref):
        c = c_ref[...]
        ca = (c * jax.nn.sigmoid(c)).astype(BF16)
        o_ref[...] = _dot(ca, w_ref[...].astype(BF16)) + b_ref[...]

    return pl.pallas_call(
        body, name="ada_fwd", grid=(nl,),
        in_specs=[pl.BlockSpec((nb, D), lambda l: (0, 0)),
                  pl.BlockSpec((None, D, 768), lambda l: (l, 0, 0)),
                  pl.BlockSpec((None, 1, 768), lambda l: (l, 0, 0))],
        out_specs=pl.BlockSpec((None, nb, 768), lambda l: (l, 0, 0)),
        out_shape=jax.ShapeDtypeStruct((nl, nb, 768), F32),
        compiler_params=_cp(("parallel",)),
    )(c_all, w_ada, b_blk)


def _ada_bwd(c_all, dmod_blk):
    nl = dmod_blk.shape[0]
    nb = c_all.shape[0]

    def body(c_ref, d_ref, o_ref):
        c = c_ref[...]
        ca = (c * jax.nn.sigmoid(c)).astype(BF16)
        o_ref[...] = _dot(ca, d_ref[...].astype(BF16), TN)

    return pl.pallas_call(
        body, name="ada_bwd", grid=(nl,),
        in_specs=[pl.BlockSpec((nb, D), lambda l: (0, 0)),
                  pl.BlockSpec((None, nb, 768), lambda l: (l, 0, 0))],
        out_specs=pl.BlockSpec((None, D, 768), lambda l: (l, 0, 0)),
        out_shape=jax.ShapeDtypeStruct((nl, D, 768), F32),
        compiler_params=_cp(("parallel",)),
    )(c_all, dmod_blk)


def _seq_tile(seq):
    return _row_tile(seq, 512)


def _norm_mod_fwd(x, g, sc, sh):
    nb, seq, _ = x.shape
    ts = _seq_tile(seq)

    def body(x_ref, g_ref, sc_ref, sh_ref, h_ref):
        xv = x_ref[0]
        r = lax.rsqrt(jnp.mean(xv * xv, axis=-1, keepdims=True) + EPS)
        h_ref[0] = ((xv * r) * g_ref[...] * (1.0 + sc_ref[0]) + sh_ref[0]).astype(BF16)

    return pl.pallas_call(
        body, name="norm_mod_fwd", grid=(nb, seq // ts),
        in_specs=[pl.BlockSpec((1, ts, D), lambda b, s: (b, s, 0)),
                  pl.BlockSpec((1, D), lambda b, s: (0, 0)),
                  pl.BlockSpec((1, 1, D), lambda b, s: (b, 0, 0)),
                  pl.BlockSpec((1, 1, D), lambda b, s: (b, 0, 0))],
        out_specs=pl.BlockSpec((1, ts, D), lambda b, s: (b, s, 0)),
        out_shape=jax.ShapeDtypeStruct((nb, seq, D), BF16),
        compiler_params=_cp(("parallel", "parallel")),
    )(x, g, sc, sh)


def _norm_mod_bwd(x, dh, dres, g, sc):
    nb, seq, _ = x.shape
    ts = _seq_tile(seq)

    def body(x_ref, dh_ref, dres_ref, g_ref, sc_ref, dx_ref, dsh_ref, dsc_ref, dg_ref):
        @pl.when(pl.program_id(1) == 0)
        def _():
            dsh_ref[...] = jnp.zeros_like(dsh_ref)
            dsc_ref[...] = jnp.zeros_like(dsc_ref)
            dg_ref[...] = jnp.zeros_like(dg_ref)

        xv = x_ref[0]
        dh = dh_ref[0]
        gv = g_ref[...]
        onesc = 1.0 + sc_ref[0]
        r = lax.rsqrt(jnp.mean(xv * xv, axis=-1, keepdims=True) + EPS)
        xh = xv * r
        dsh_ref[0] += jnp.sum(dh, axis=0, keepdims=True)
        dsc_ref[0] += jnp.sum(dh * (xh * gv), axis=0, keepdims=True)
        dg_ref[0] += jnp.sum(dh * onesc * xh, axis=0, keepdims=True)
        dxh = dh * (gv * onesc)
        dx = r * (dxh - xh * jnp.mean(dxh * xh, axis=-1, keepdims=True))
        dx_ref[0] = dres_ref[0] + dx

    vec = jax.ShapeDtypeStruct((nb, 1, D), F32)
    vspec = pl.BlockSpec((1, 1, D), lambda b, s: (b, 0, 0))
    tile = pl.BlockSpec((1, ts, D), lambda b, s: (b, s, 0))
    return pl.pallas_call(
        body, name="norm_mod_bwd", grid=(nb, seq // ts),
        in_specs=[tile, tile, tile, pl.BlockSpec((1, D), lambda b, s: (0, 0)), vspec],
        out_specs=[tile, vspec, vspec, vspec],
        out_shape=[jax.ShapeDtypeStruct((nb, seq, D), F32), vec, vec, vec],
        compiler_params=_cp(("parallel", "arbitrary")),
    )(x, dh, dres, g, sc)


def _gate_bwd(dx, y, gt):
    nb, seq, _ = dx.shape
    ts = _seq_tile(seq)

    def body(dx_ref, y_ref, gt_ref, dy_ref, dgt_ref):
        @pl.when(pl.program_id(1) == 0)
        def _():
            dgt_ref[...] = jnp.zeros_like(dgt_ref)

        d = dx_ref[0]
        dy_ref[0] = (gt_ref[0] * d).astype(BF16)
        dgt_ref[0] += jnp.sum(d * y_ref[0].astype(F32), axis=0, keepdims=True)

    vspec = pl.BlockSpec((1, 1, D), lambda b, s: (b, 0, 0))
    tile = pl.BlockSpec((1, ts, D), lambda b, s: (b, s, 0))
    return pl.pallas_call(
        body, name="gate_bwd", grid=(nb, seq // ts),
        in_specs=[tile, tile, vspec], out_specs=[tile, vspec],
        out_shape=[jax.ShapeDtypeStruct((nb, seq, D), BF16), jax.ShapeDtypeStruct((nb, 1, D), F32)],
        compiler_params=_cp(("parallel", "arbitrary")),
    )(dx, y, gt)


def _loss_head(x, tgt, g):
    nb, seq, _ = x.shape
    ts = _seq_tile(seq)

    def body(x_ref, t_ref, g_ref, dx_ref, loss_ref, dg_ref):
        @pl.when(pl.program_id(1) == 0)
        def _():
            loss_ref[...] = jnp.zeros_like(loss_ref)
            dg_ref[...] = jnp.zeros_like(dg_ref)

        xv = x_ref[0]
        gv = g_ref[...]
        r = lax.rsqrt(jnp.mean(xv * xv, axis=-1, keepdims=True) + EPS)
        xh = xv * r
        err = xh * gv - t_ref[0]
        per_tok = jnp.mean(err * err, axis=-1, keepdims=True)
        loss_ref[0] += 0.5 * jnp.sum(per_tok, axis=0, keepdims=True)
        dy = err * (1.0 / D)
        dg_ref[0] += jnp.sum(dy * xh, axis=0, keepdims=True)
        dxh = dy * gv
        dx_ref[0] = r * (dxh - xh * jnp.mean(dxh * xh, axis=-1, keepdims=True))

    tile = pl.BlockSpec((1, ts, D), lambda b, s: (b, s, 0))
    return pl.pallas_call(
        body, name="loss_head", grid=(nb, seq // ts),
        in_specs=[tile, tile, pl.BlockSpec((1, D), lambda b, s: (0, 0))],
        out_specs=[tile, pl.BlockSpec((1, 1, 128), lambda b, s: (b, 0, 0)),
                   pl.BlockSpec((1, 1, D), lambda b, s: (b, 0, 0))],
        out_shape=[jax.ShapeDtypeStruct((nb, seq, D), F32), jax.ShapeDtypeStruct((nb, 1, 128), F32),
                   jax.ShapeDtypeStruct((nb, 1, D), F32)],
        compiler_params=_cp(("parallel", "arbitrary")),
    )(x, tgt, g)


_GELU_C = math.sqrt(2.0 / math.pi)


def _gelu(x):
    return 0.5 * x * (1.0 + jnp.tanh(_GELU_C * (x + 0.044715 * (x * x * x))))


def _gelu_and_grad(x):
    t = jnp.tanh(_GELU_C * (x + 0.044715 * (x * x * x)))
    y = 0.5 * x * (1.0 + t)
    dy = 0.5 * (1.0 + t) + 0.5 * x * (1.0 - t * t) * (_GELU_C * (1.0 + 3.0 * 0.044715 * (x * x)))
    return y, dy


def _tril_mask():
    row = lax.broadcasted_iota(jnp.int32, (CH, CH), 0)
    col = lax.broadcasted_iota(jnp.int32, (CH, CH), 1)
    return row >= col


def _gmlp_fwd(proj, ln_g, ln_b, ws, bst):
    t = proj.shape[0]
    tm = _row_tile(t, 512)

    def body(u_ref, v_ref, lg_ref, lb_ref, ws_ref, bst_ref, o_ref):
        tril = _tril_mask()
        wm = [jnp.where(tril, ws_ref[g], 0.0).astype(BF16) for g in range(NG)]
        for ch in range(tm // CH):
            rows = slice(ch * CH, (ch + 1) * CH)
            u = _gelu(u_ref[rows, :].astype(F32))
            v = _gelu(v_ref[rows, :].astype(F32))
            mu = jnp.mean(v, axis=-1, keepdims=True)
            xc = v - mu
            rstd = lax.rsqrt(jnp.mean(xc * xc, axis=-1, keepdims=True) + EPS)
            vn = ((xc * rstd) * lg_ref[...] + lb_ref[...]).astype(BF16)
            for g in range(NG):
                cols = slice(g * CH, (g + 1) * CH)
                s = _dot(wm[g], vn[:, cols]) + bst_ref[:, g:g + 1]
                o_ref[rows, cols] = (u[:, cols] * s).astype(BF16)

    return pl.pallas_call(
        body, name="gmlp_fwd", grid=(t // tm,),
        in_specs=[pl.BlockSpec((tm, BW), lambda i: (i, 0)),
                  pl.BlockSpec((tm, BW), lambda i: (i, 1)),
                  pl.BlockSpec((1, BW), lambda i: (0, 0)),
                  pl.BlockSpec((1, BW), lambda i: (0, 0)),
                  pl.BlockSpec((NG, CH, CH), lambda i: (0, 0, 0)),
                  pl.BlockSpec((CH, NG), lambda i: (0, 0))],
        out_specs=pl.BlockSpec((tm, BW), lambda i: (i, 0)),
        out_shape=jax.ShapeDtypeStruct((t, BW), BF16),
        compiler_params=_cp(("parallel",)),
    )(proj, proj, ln_g, ln_b, ws, bst)


def _gmlp_bwd(proj, dout, ln_g, ln_b, ws, bst):
    t = proj.shape[0]
    tm = _row_tile(t, 512)

    def body(u_ref, v_ref, do_ref, lg_ref, lb_ref, ws_ref, bst_ref, dp_ref, gws_ref, gbs_ref, glg_ref, glb_ref):
        @pl.when(pl.program_id(0) == 0)
        def _():
            gws_ref[...] = jnp.zeros_like(gws_ref)
            gbs_ref[...] = jnp.zeros_like(gbs_ref)
            glg_ref[...] = jnp.zeros_like(glg_ref)
            glb_ref[...] = jnp.zeros_like(glb_ref)

        tril = _tril_mask()
        wm = [jnp.where(tril, ws_ref[g], 0.0).astype(BF16) for g in range(NG)]
        ones = jnp.ones((CH, CH), BF16)
        lg = lg_ref[...]
        for ch in range(tm // CH):
            rows = slice(ch * CH, (ch + 1) * CH)
            u, du_fac = _gelu_and_grad(u_ref[rows, :].astype(F32))
            v, dv_fac = _gelu_and_grad(v_ref[rows, :].astype(F32))
            do = do_ref[rows, :].astype(F32)
            mu = jnp.mean(v, axis=-1, keepdims=True)
            xc = v - mu
            rstd = lax.rsqrt(jnp.mean(xc * xc, axis=-1, keepdims=True) + EPS)
            xh = xc * rstd
            vn = (xh * lg + lb_ref[...]).astype(BF16)
            dvn_parts = []
            for g in range(NG):
                cols = slice(g * CH, (g + 1) * CH)
                s = _dot(wm[g], vn[:, cols]) + bst_ref[:, g:g + 1]
                dp_ref[rows, cols] = (do[:, cols] * s * du_fac[:, cols]).astype(BF16)
                ds = (do[:, cols] * u[:, cols]).astype(BF16)
                gws_ref[g] += jnp.where(tril, _dot(ds, vn[:, cols], NT), 0.0)
                gbs_ref[g] += _dot(ds, ones)
                dvn_parts.append(_dot(wm[g], ds, TN))
            dvn = jnp.concatenate(dvn_parts, axis=1)
            glb_ref[...] += jnp.sum(dvn, axis=0, keepdims=True)
            glg_ref[...] += jnp.sum(dvn * xh, axis=0, keepdims=True)
            dxh = dvn * lg
            dv = rstd * (dxh - jnp.mean(dxh, axis=-1, keepdims=True)
                         - xh * jnp.mean(dxh * xh, axis=-1, keepdims=True))
            dp_ref[rows, BW:2 * BW] = (dv * dv_fac).astype(BF16)

    small = pl.BlockSpec((NG, CH, CH), lambda i: (0, 0, 0))
    vec = pl.BlockSpec((1, BW), lambda i: (0, 0))
    return pl.pallas_call(
        body, name="gmlp_bwd", grid=(t // tm,),
        in_specs=[pl.BlockSpec((tm, BW), lambda i: (i, 0)),
                  pl.BlockSpec((tm, BW), lambda i: (i, 1)),
                  pl.BlockSpec((tm, BW), lambda i: (i, 0)),
                  vec, vec, small, pl.BlockSpec((CH, NG), lambda i: (0, 0))],
        out_specs=[pl.BlockSpec((tm, 2 * BW), lambda i: (i, 0)), small, small, vec, vec],
        out_shape=[jax.ShapeDtypeStruct((t, 2 * BW), BF16),
                   jax.ShapeDtypeStruct((NG, CH, CH), F32), jax.ShapeDtypeStruct((NG, CH, CH), F32),
                   jax.ShapeDtypeStruct((1, BW), F32), jax.ShapeDtypeStruct((1, BW), F32)],
        compiler_params=_cp(("arbitrary",)),
    )(proj, proj, dout, ln_g, ln_b, ws, bst)


def _pool_bands():
    row = lax.broadcasted_iota(jnp.int32, (CH, CH), 0)
    col = lax.broadcasted_iota(jnp.int32, (CH, CH), 1)
    cur, prev = [], []
    for w in POOL_WINDOWS:
        cur.append(jnp.where((row >= col) & (row - col < w), 1.0, 0.0).astype(BF16))
        prev.append(jnp.where(row + CH - col < w, 1.0, 0.0).astype(BF16))
    return cur, prev


def _pool_inv_count(r0, w):
    pos = r0 + lax.broadcasted_iota(jnp.int32, (CH, 1), 0)
    return 1.0 / jnp.minimum(pos + 1, w).astype(F32)


def _pool_diff(x_ref, r0, rp, has_prev, cur, prev, g):
    cols = slice(g * CH, (g + 1) * CH)
    xc = x_ref[pl.ds(r0, CH), cols]
    xp = x_ref[pl.ds(rp, CH), cols]
    ws = _dot(cur[g], xc) + has_prev * _dot(prev[g], xp)
    return ws * _pool_inv_count(r0, POOL_WINDOWS[g]) - xc.astype(F32)


def _pool_fwd(proj3, pw, pscale):
    nb, seq, _ = proj3.shape
    nch = seq // CH

    def body(x_ref, pw_ref, ps_ref, o_ref):
        cur, prev = _pool_bands()
        pwb = [pw_ref[g].astype(BF16) for g in range(NG)]

        def chunk(ch, carry):
            r0 = pl.multiple_of(ch * CH, CH)
            rp = pl.multiple_of(jnp.maximum(ch - 1, 0) * CH, CH)
            has_prev = jnp.where(ch > 0, 1.0, 0.0)
            for g in range(NG):
                cols = slice(g * CH, (g + 1) * CH)
                d = _pool_diff(x_ref, r0, rp, has_prev, cur, prev, g)
                y = _dot(d.astype(BF16), pwb[g]) * ps_ref[:, cols]
                o_ref[pl.ds(r0, CH), cols] = y.astype(BF16)
            return carry

        lax.fori_loop(0, nch, chunk, 0, unroll=2)

    return pl.pallas_call(
        body, name="pool_fwd", grid=(nb,),
        in_specs=[pl.BlockSpec((None, seq, BW), lambda b: (b, 0, 5)),
                  pl.BlockSpec((NG, CH, CH), lambda b: (0, 0, 0)),
                  pl.BlockSpec((1, BW), lambda b: (0, 0))],
        out_specs=pl.BlockSpec((None, seq, BW), lambda b: (b, 0, 0)),
        out_shape=jax.ShapeDtypeStruct((nb, seq, BW), BF16),
        compiler_params=_cp(("parallel",)),
    )(proj3, pw, pscale)


def _pool_bwd(proj3, dout3, pw, pscale):
    nb, seq, _ = proj3.shape
    nch = seq // CH

    def body(x_ref, do_ref, pw_ref, ps_ref, dx_ref, gpw_ref, gps_ref, e_ref):
        @pl.when(pl.program_id(0) == 0)
        def _():
            gpw_ref[...] = jnp.zeros_like(gpw_ref)
            gps_ref[...] = jnp.zeros_like(gps_ref)

        cur, prev = _pool_bands()
        pwb = [pw_ref[g].astype(BF16) for g in range(NG)]

        def first(ch, carry):
            r0 = pl.multiple_of(ch * CH, CH)
            rp = pl.multiple_of(jnp.maximum(ch - 1, 0) * CH, CH)
            has_prev = jnp.where(ch > 0, 1.0, 0.0)
            for g in range(NG):
                cols = slice(g * CH, (g + 1) * CH)
                d = _pool_diff(x_ref, r0, rp, has_prev, cur, prev, g).astype(BF16)
                do = do_ref[pl.ds(r0, CH), cols].astype(F32)
                ypre = _dot(d, pwb[g])
                gps_ref[:, cols] += jnp.sum(do * ypre, axis=0, keepdims=True)
                dyp = (do * ps_ref[:, cols]).astype(BF16)
                gpw_ref[g] += _dot(d, dyp, TN)
                e_ref[pl.ds(r0, CH), cols] = _dot(dyp, pwb[g], NT)
            return carry

        lax.fori_loop(0, nch, first, 0, unroll=2)

        def second(ch, carry):
            r0 = pl.multiple_of(ch * CH, CH)
            rn = pl.multiple_of(jnp.minimum(ch + 1, nch - 1) * CH, CH)
            has_next = jnp.where(ch < nch - 1, 1.0, 0.0)
            for g in range(NG):
                cols = slice(g * CH, (g + 1) * CH)
                w = POOL_WINDOWS[g]
                dd = e_ref[pl.ds(r0, CH), cols]
                ec = (dd * _pool_inv_count(r0, w)).astype(BF16)
                en = (e_ref[pl.ds(rn, CH), cols] * _pool_inv_count(rn, w)).astype(BF16)
                dx = _dot(cur[g], ec, TN) + has_next * _dot(prev[g], en, TN) - dd
                dx_ref[pl.ds(r0, CH), cols] = dx.astype(BF16)
            return carry

        lax.fori_loop(0, nch, second, 0, unroll=2)

    small = pl.BlockSpec((NG, CH, CH), lambda b: (0, 0, 0))
    vec = pl.BlockSpec((1, BW), lambda b: (0, 0))
    return pl.pallas_call(
        body, name="pool_bwd", grid=(nb,),
        in_specs=[pl.BlockSpec((None, seq, BW), lambda b: (b, 0, 5)),
                  pl.BlockSpec((None, seq, BW), lambda b: (b, 0, 0)), small, vec],
        out_specs=[pl.BlockSpec((None, seq, BW), lambda b: (b, 0, 0)), small, vec],
        out_shape=[jax.ShapeDtypeStruct((nb, seq, BW), BF16),
                   jax.ShapeDtypeStruct((NG, CH, CH), F32), jax.ShapeDtypeStruct((1, BW), F32)],
        scratch_shapes=[pltpu.VMEM((seq, BW), F32)],
        compiler_params=_cp(("arbitrary",)),
    )(proj3, dout3, pw, pscale)


SB_BQ = 256
SB_BK = 256
SB_SCALE = HD ** -0.5


SB_EXIT = -110.0


def _sb_tile(qs, k, mask):
    z = _dot(qs, k, NT)
    lb = jnp.minimum(z, 0.0) - jnp.log(1.0 + jnp.exp(-jnp.abs(z)))
    lom = lb - z
    if mask is not None:
        lom = jnp.where(mask, lom, 0.0)
    return lb, lom


def _sb_alive(c):
    top = functools.reduce(jnp.maximum, [jnp.max(state[1]) for state in c])
    return (top > SB_EXIT).astype(jnp.int32)


def _sb_past_blocks(step, c, npast):
    def cond(s):
        return jnp.logical_and(s[0] < npast, s[1] > 0)

    def body(s):
        i, _, c = s
        c = step(pl.multiple_of((npast - 1 - i) * SB_BK, SB_BK), c, None)
        return i + 1, _sb_alive(c), c

    return lax.while_loop(cond, body, (jnp.int32(0), _sb_alive(c), c))[2]


def _sb_diag_mask(bq, d):
    row = lax.broadcasted_iota(jnp.int32, (bq, SB_BK), 0)
    col = lax.broadcasted_iota(jnp.int32, (bq, SB_BK), 1)
    return col + d * SB_BK < row


def _sb_scaled(q):
    return (q.astype(F32) * SB_SCALE).astype(BF16)


def _dot_tri(a, m):
    return _dot(a.astype(BF16), m)


def _dot_tri2(a, m):
    hi = a.astype(BF16)
    lo = (a - hi.astype(F32)).astype(BF16)
    return _dot(hi, m) + _dot(lo, m)


def _sb_fwd(proj3, gather=()):
    nb, seq, _ = proj3.shape
    bq = min(SB_BQ, seq)
    nq = seq // bq
    ndiag = bq // SB_BK

    def body(q_ref, k_ref, v_ref, o_ref):
        row = lax.broadcasted_iota(jnp.int32, (SB_BK, SB_BK), 0)
        col = lax.broadcasted_iota(jnp.int32, (SB_BK, SB_BK), 1)
        upper = jnp.where(row > col, 1.0, 0.0).astype(BF16)
        heads = [slice(hh * HD, (hh + 1) * HD) for hh in range(2)]

        def qloop(qi, carry):
            q0 = pl.multiple_of(qi * bq, bq)
            qs = [_sb_scaled(q_ref[pl.ds(q0, bq), lanes]) for lanes in heads]

            def step(k0, c, mask):
                tiles = [_sb_tile(q, k_ref[pl.ds(k0, SB_BK), lanes], mask) for lanes, q in zip(heads, qs)]
                sums = [_dot_tri(lom, upper) for _, lom in tiles]
                out = []
                for lanes, (acc, cr), (lb, lom), cs in zip(heads, c, tiles, sums):
                    a = jnp.exp(lb + (cs + cr))
                    if mask is not None:
                        a = jnp.where(mask, a, 0.0)
                    rsum = cs[:, 0:1] + lom[:, 0:1]
                    out.append((acc + _dot(a.astype(BF16), v_ref[pl.ds(k0, SB_BK), lanes]), cr + rsum))
                return tuple(out)

            c = tuple((jnp.zeros((bq, HD), F32), jnp.zeros((bq, 1), F32)) for _ in heads)
            for d in reversed(range(ndiag)):
                c = step(pl.multiple_of(q0 + d * SB_BK, SB_BK), c, _sb_diag_mask(bq, d))
            c = _sb_past_blocks(step, c, qi * ndiag)
            for lanes, (acc, _) in zip(heads, c):
                o_ref[pl.ds(q0, bq), lanes] = acc
            return carry

        lax.fori_loop(0, nq, qloop, 0)

    def spec(c0):
        return pl.BlockSpec((None, seq, 128), lambda b, p: (b, 0, c0 + p))

    grid = (nb, BW // 128)
    body, ex_in, ex_out, ex_shape, ex_sems = _host_exchange(body, 3, 1, grid, gather, False)
    outs = pl.pallas_call(
        body, name="sb_fwd", grid=grid,
        in_specs=[spec(8), spec(12), spec(16)] + ex_in,
        out_specs=[spec(0)] + ex_out,
        out_shape=[jax.ShapeDtypeStruct((nb, seq, BW), F32)] + ex_shape,
        scratch_shapes=ex_sems,
        compiler_params=_cp(("arbitrary", "arbitrary")),
    )(proj3, proj3, proj3, *gather)
    return outs[0], outs[1:]


def _sb_bwd(proj3, do3, o3, scatter=()):
    nb, seq, _ = proj3.shape
    bq = min(SB_BQ, seq)
    nq = seq // bq
    ndiag = bq // SB_BK

    def body(q_ref, k_ref, v_ref, do_ref, o_ref, dq_ref, dk_ref, dv_ref, dk_acc, dv_acc):
        row = lax.broadcasted_iota(jnp.int32, (SB_BK, SB_BK), 0)
        col = lax.broadcasted_iota(jnp.int32, (SB_BK, SB_BK), 1)
        upper = jnp.where(row > col, 1.0, 0.0).astype(BF16)
        later = jnp.where(row >= col, 1.0, 0.0).astype(BF16)
        dk_acc[...] = jnp.zeros_like(dk_acc)
        dv_acc[...] = jnp.zeros_like(dv_acc)
        heads = [slice(hh * HD, (hh + 1) * HD) for hh in range(2)]

        def qloop(qi, carry):
            q0 = pl.multiple_of(qi * bq, bq)
            qs = [_sb_scaled(q_ref[pl.ds(q0, bq), lanes]) for lanes in heads]
            dos = [do_ref[pl.ds(q0, bq), lanes] for lanes in heads]
            gtot = [jnp.sum(do.astype(F32) * o_ref[pl.ds(q0, bq), lanes], axis=1, keepdims=True)
                    for do, lanes in zip(dos, heads)]

            def step(k0, c, mask):
                ks = [k_ref[pl.ds(k0, SB_BK), lanes] for lanes in heads]
                tiles = [_sb_tile(q, k, mask) for q, k in zip(qs, ks)]
                sums = [_dot_tri(lom, upper) for _, lom in tiles]
                das = [_dot(do, v_ref[pl.ds(k0, SB_BK), lanes], NT) for do, lanes in zip(dos, heads)]
                gls, avs = [], []
                for hh, (_, cr, _) in enumerate(c):
                    a = jnp.exp(tiles[hh][0] + (sums[hh] + cr))
                    if mask is not None:
                        a = jnp.where(mask, a, 0.0)
                    ab = a.astype(BF16)
                    avs.append(ab)
                    gls.append(das[hh] * ab.astype(F32))
                tails = [_dot_tri2(gl, later) for gl in gls]
                out = []
                for hh, (dq, cr, gdone) in enumerate(c):
                    lb, lom = tiles[hh]
                    pre = gtot[hh] - gdone - tails[hh]
                    dz = gls[hh] - jnp.exp(lb) * (gls[hh] + pre)
                    if mask is not None:
                        dz = jnp.where(mask, dz, 0.0)
                    dz = dz.astype(BF16)
                    dk_acc[hh, pl.ds(k0, SB_BK), :] += _dot(dz, qs[hh], TN)
                    dv_acc[hh, pl.ds(k0, SB_BK), :] += _dot(avs[hh], dos[hh], TN)
                    rsum = sums[hh][:, 0:1] + lom[:, 0:1]
                    out.append((dq + _dot(dz, ks[hh]), cr + rsum, gdone + tails[hh][:, 0:1]))
                return tuple(out)

            c = tuple((jnp.zeros((bq, HD), F32), jnp.zeros((bq, 1), F32), jnp.zeros((bq, 1), F32))
                      for _ in heads)
            for d in reversed(range(ndiag)):
                c = step(pl.multiple_of(q0 + d * SB_BK, SB_BK), c, _sb_diag_mask(bq, d))
            c = _sb_past_blocks(step, c, qi * ndiag)
            for lanes, (dq, _, _) in zip(heads, c):
                dq_ref[pl.ds(q0, bq), lanes] = (dq * SB_SCALE).astype(BF16)
            return carry

        lax.fori_loop(0, nq, qloop, 0)
        for hh in range(2):
            lanes = slice(hh * HD, (hh + 1) * HD)
            dk_ref[:, lanes] = dk_acc[hh].astype(BF16)
            dv_ref[:, lanes] = dv_acc[hh].astype(BF16)

    def spec(c0):
        return pl.BlockSpec((None, seq, 128), lambda b, p: (b, 0, c0 + p))

    grid = (nb, BW // 128)
    body, ex_in, ex_out, ex_shape, ex_sems = _host_exchange(body, 5, 3, grid, scatter, True)
    outs = pl.pallas_call(
        body, name="sb_bwd", grid=grid,
        in_specs=[spec(8), spec(12), spec(16), spec(0), spec(0)] + ex_in,
        out_specs=[spec(0), spec(0), spec(0)] + ex_out,
        out_shape=[jax.ShapeDtypeStruct((nb, seq, BW), BF16)] * 3 + ex_shape,
        scratch_shapes=[pltpu.VMEM((2, seq, HD), F32), pltpu.VMEM((2, seq, HD), F32)] + ex_sems,
        compiler_params=_cp(("arbitrary", "arbitrary")),
    )(proj3, proj3, proj3, do3, o3, *scatter)
    return outs[:3], outs[3:]


def _merge_fwd(brs, wb, proj):
    t = proj.shape[0]
    tm = _row_tile(t, 512)
    tn = 512
    nj = D // tn

    def body(b0, b1, b2, wb_ref, l0, l1, l2, m_ref, y0, y1, y2):
        acc = None
        for br, n, lg, y_ref in ((b0, 0, l0, y0), (b1, 1, l1, y1), (b2, 2, l2, y2)):
            y = _dot(br[...].astype(BF16), wb_ref[n])
            y_ref[...] = y.astype(BF16)
            term = jax.nn.sigmoid(lg[...].astype(F32)) * y
            acc = term if acc is None else acc + term
        m_ref[...] = acc.astype(BF16)

    def lspec(n):
        return pl.BlockSpec((tm, tn), lambda i, j: (i, (3 * D + n * D) // tn + j))

    tile = pl.BlockSpec((tm, tn), lambda i, j: (i, j))
    bspec = pl.BlockSpec((tm, BW), lambda i, j: (i, 0))
    return pl.pallas_call(
        body, name="merge_fwd", grid=(t // tm, nj),
        in_specs=[bspec, bspec, bspec, pl.BlockSpec((NB, BW, tn), lambda i, j: (0, 0, j)),
                  lspec(0), lspec(1), lspec(2)],
        out_specs=[tile] * 4,
        out_shape=[jax.ShapeDtypeStruct((t, D), BF16)] * 4,
        compiler_params=_cp(("parallel", "parallel")),
    )(brs[0], brs[1], brs[2], wb, proj, proj, proj)


def _merge_bwd(dm, ys, proj):
    t = proj.shape[0]
    tm = _row_tile(t, 512)
    tn = 512

    def body(dm_ref, y0, y1, y2, l0, l1, l2, dl0, dl1, dl2, dy0, dy1, dy2):
        dmv = dm_ref[...].astype(F32)
        for y_ref, lg, dl_ref, dy_ref in ((y0, l0, dl0, dy0), (y1, l1, dl1, dy1), (y2, l2, dl2, dy2)):
            g = jax.nn.sigmoid(lg[...].astype(F32))
            dl_ref[...] = (dmv * y_ref[...].astype(F32) * g * (1.0 - g)).astype(BF16)
            dy_ref[...] = (dmv * g).astype(BF16)

    def lspec(n):
        return pl.BlockSpec((tm, tn), lambda i, j: (i, (3 * D + n * D) // tn + j))

    tile = pl.BlockSpec((tm, tn), lambda i, j: (i, j))
    return pl.pallas_call(
        body, name="merge_bwd", grid=(t // tm, D // tn),
        in_specs=[tile] * 4 + [lspec(0), lspec(1), lspec(2)],
        out_specs=[tile] * 6,
        out_shape=[jax.ShapeDtypeStruct((t, D), BF16)] * 6,
        compiler_params=_cp(("parallel", "parallel")),
    )(dm, ys[0], ys[1], ys[2], proj, proj, proj)


def _adamw_rows(rows):
    if rows <= 512:
        return rows
    return next(tr for tr in (512, 384, 352, 256, 128, 64, 32, 16, 8) if rows % tr == 0)


def _adamw_math(npart, p_ref, w_ref, m_ref, v_ref, g_ref, d_ref, mo_ref, vo_ref):
    c1 = 1.0 - ADAM_B1 ** ADAM_STEP
    c2 = 1.0 - ADAM_B2 ** ADAM_STEP
    g = p_ref[0].astype(F32)
    for p in range(1, npart):
        g = g + p_ref[p].astype(F32)
    mn = ADAM_B1 * m_ref[...] + (1.0 - ADAM_B1) * g
    vn = ADAM_B2 * v_ref[...] + (1.0 - ADAM_B2) * (g * g)
    m_hat = mn / c1
    v_hat = vn / c2
    g_ref[...] = g
    d_ref[...] = -ADAM_LR * (m_hat / (jnp.sqrt(v_hat) + ADAM_EPS) + ADAM_WD * w_ref[...])
    mo_ref[...] = mn
    vo_ref[...] = vn


def _adamw_layer(name, parts, w, m, v, layer, bufs):
    nl, cols = w.shape[0], w.shape[-1]
    rows = int(math.prod(w.shape[1:-1]))
    npart = parts.shape[0]
    tr = _adamw_rows(rows)
    if bufs is None:
        bufs = [lax.empty((nl, rows, cols), F32) for _ in range(4)]

    def body(p_ref, w_ref, m_ref, v_ref, b0, b1, b2, b3, g_ref, d_ref, mo_ref, vo_ref):
        _adamw_math(npart, p_ref, w_ref, m_ref, v_ref, g_ref, d_ref, mo_ref, vo_ref)

    slab = pl.BlockSpec((None, tr, cols), lambda i: (layer, i, 0))
    sds = jax.ShapeDtypeStruct((nl, rows, cols), F32)
    return pl.pallas_call(
        body, name=name, grid=(rows // tr,),
        in_specs=[pl.BlockSpec((npart, tr, cols), lambda i: (0, i, 0)), slab, slab, slab] + [_HBM] * 4,
        out_specs=[slab] * 4, out_shape=[sds] * 4,
        input_output_aliases={4: 0, 5: 1, 6: 2, 7: 3},
        compiler_params=_cp(("parallel",)),
    )(parts.reshape(npart, rows, cols), w.reshape(nl, rows, cols), m.reshape(nl, rows, cols),
      v.reshape(nl, rows, cols), *bufs)


def _adamw_reduce(name, parts, w, m, v):
    shape = w.shape
    cols = shape[-1]
    rows = int(math.prod(shape[:-1])) if len(shape) > 1 else 1
    npart = parts.shape[0]
    tr = _adamw_rows(rows)

    def body(p_ref, w_ref, m_ref, v_ref, g_ref, d_ref, mo_ref, vo_ref):
        _adamw_math(npart, p_ref, w_ref, m_ref, v_ref, g_ref, d_ref, mo_ref, vo_ref)

    tile = pl.BlockSpec((tr, cols), lambda i: (i, 0))
    sds = jax.ShapeDtypeStruct((rows, cols), F32)
    outs = pl.pallas_call(
        body, name=name, grid=(rows // tr,),
        in_specs=[pl.BlockSpec((npart, tr, cols), lambda i: (0, i, 0)), tile, tile, tile],
        out_specs=[tile] * 4, out_shape=[sds] * 4,
        compiler_params=_cp(("parallel",)),
    )(parts.reshape(npart, rows, cols), w.reshape(rows, cols), m.reshape(rows, cols), v.reshape(rows, cols))
    return tuple(o.reshape(shape) for o in outs)


def _pad_ffn_in(w):
    lead = w.shape[:-1]
    w = w.reshape(lead + (2, FF_HALF))
    w = jnp.pad(w, [(0, 0)] * len(lead) + [(0, 0), (0, FF_HALF_PAD - FF_HALF)])
    return w.reshape(lead + (FF_IN_PAD,))


def kernel(x, c, rms_g1, rms_g2, w_ada, b_ada, w_in, gm_ln_g, gm_ln_b, gm_w_spatial, gm_b_spatial, pool_w, pool_scale, w_branch, w_out, w_ffn_in, w_ffn_out, final_g, loss_target, m_rms_g1, m_rms_g2, m_w_ada, m_b_ada, m_w_in, m_gm_ln_g, m_gm_ln_b, m_gm_w_spatial, m_gm_b_spatial, m_pool_w, m_pool_scale, m_w_branch, m_w_out, m_w_ffn_in, m_w_ffn_out, m_final_g, v_rms_g1, v_rms_g2, v_w_ada, v_b_ada, v_w_in, v_gm_ln_g, v_gm_ln_b, v_gm_w_spatial, v_gm_b_spatial, v_pool_w, v_pool_scale, v_w_branch, v_w_out, v_w_ffn_in, v_w_ffn_out, v_final_g):
    nb, seq, _ = x.shape
    nl = w_in.shape[0]
    t = nb * seq
    ntot = NDEV * nb
    me = _my_index()
    assert x.shape[2] == D and w_in.shape[1:] == (D, 768) and w_ffn_in.shape[1:] == (D, FF_IN_SHARD)
    assert seq % CH == 0

    w_ffn_in_p = _pad_ffn_in(w_ffn_in).astype(BF16)
    w_ffn_out_p = jnp.pad(w_ffn_out, ((0, 0), (0, FF_HALF_PAD - FF_HALF), (0, 0))).astype(BF16)
    w_in_b = w_in.astype(BF16)
    w_branch_b = w_branch.astype(BF16)
    w_out_b = w_out.astype(BF16)
    (g_in_next,) = _exchange([w_in_b[0]], "gather_w_in0", False)

    (c_all,) = _exchange([c], "gather_c", False)
    c_all = c_all.reshape(ntot, D)
    b_blk = lax.dynamic_slice_in_dim(b_ada, me * 768, 768, axis=1).reshape(nl, 1, 768)
    mod_blk = _ada_fwd(c_all, w_ada, b_blk)
    (mod_all,) = _exchange([mod_blk], "gather_mod", False)
    mod_all = jnp.transpose(mod_all, (1, 2, 0, 3)).reshape(nl, ntot, NMOD * D)
    mod = lax.dynamic_slice_in_dim(mod_all, me * nb, nb, axis=1).reshape(nl, nb, NMOD, 1, D)

    saved = []
    gathered = []
    xc = x
    for l in range(nl):
        sh1, sc1, gt1, sh2, sc2, gt2 = [mod[l, :, i] for i in range(NMOD)]
        h = _norm_mod_fwd(xc, rms_g1[l].reshape(1, D), sc1, sh1).reshape(t, D)
        proj, (g_ffn_in_w,) = _mm_colblocked("proj_fwd", h, g_in_next, BF16, [w_ffn_in_p[l]])
        proj3 = proj.reshape(nb, seq, IN_COLS)
        br_gm = _gmlp_fwd(proj, gm_ln_g[l].reshape(1, BW), gm_ln_b[l].reshape(1, BW),
                          gm_w_spatial[l], gm_b_spatial[l].T)
        sb_o, got = _sb_fwd(proj3, [w_branch_b[l], w_out_b[l], w_ffn_out_p[l]])
        gw = dict(w_in=g_in_next,
                  w_branch=jnp.transpose(got[0], (1, 2, 0, 3)).reshape(NB, BW, D),
                  w_out=got[1].reshape(D, D),
                  w_ffn_in=g_ffn_in_w,
                  w_ffn_out=got[2].reshape(FFP, D))
        gathered.append(gw)
        br_pool = _pool_fwd(proj3, pool_w[l], pool_scale[l].reshape(1, BW))
        brs = [br_gm, sb_o.reshape(t, BW), br_pool.reshape(t, BW)]
        merged, y0, y1, y2 = _merge_fwd(brs, gw["w_branch"], proj)
        x_mid, mo = _mm_residual("out_fwd", merged, gw["w_out"], xc.reshape(t, D), gt1, seq)
        x_mid = x_mid.reshape(nb, seq, D)
        h2 = _norm_mod_fwd(x_mid, rms_g2[l].reshape(1, D), sc2, sh2).reshape(t, D)
        fg, fu, act, got = _ffn_in_fwd(h2, gw["w_ffn_in"], [w_in_b[l + 1]] if l + 1 < nl else [])
        if l + 1 < nl:
            g_in_next = got[0]
        x_out, fo = _mm_residual("ffn_out_fwd", act, gw["w_ffn_out"], x_mid.reshape(t, D), gt2, seq)
        saved.append(dict(x_in=xc, h=h, proj=proj, brs=brs, sb_o=sb_o, ys=(y0, y1, y2), merged=merged,
                          mo=mo, x_mid=x_mid, h2=h2, fg=fg, fu=fu, act=act, fo=fo))
        xc = x_out.reshape(nb, seq, D)

    dx, loss_part, dfinal_part = _loss_head(xc, loss_target, final_g.reshape(1, D))
    loss = lax.psum(jnp.sum(loss_part[:, 0, 0]), ("x", "y", "c"))

    big_names = ("w_in", "w_branch", "w_out", "w_ffn_in", "w_ffn_out")
    bufs = {name: None for name in big_names}
    w_ffn_in_t, m_w_ffn_in_t, v_w_ffn_in_t = [jnp.swapaxes(a, 1, 2) for a in (w_ffn_in, m_w_ffn_in, v_w_ffn_in)]
    small_parts = {k: [None] * nl for k in ("rms_g1", "rms_g2", "gm_ln_g", "gm_ln_b", "gm_w_spatial",
                                            "gm_b_spatial", "pool_w", "pool_scale")}
    dmod = [None] * nl
    for l in reversed(range(nl)):
        gw = gathered[l]
        sv = saved[l]
        sh1, sc1, gt1, sh2, sc2, gt2 = [mod[l, :, i] for i in range(NMOD)]
        dfo, dgt2 = _gate_bwd(dx, sv["fo"].reshape(nb, seq, D), gt2)
        dfo = dfo.reshape(t, D)
        g_ffn_out = _mm_tn("ffn_out_wgrad", sv["act"], dfo)
        dfg, dfu = _ffn_out_dgrad(dfo, gw["w_ffn_out"], sv["fg"], sv["fu"])
        dh2 = _ffn_in_dgrad(dfg, dfu, gw["w_ffn_in"])
        g_ffn_in = _ffn_in_wgrad(sv["h2"], dfg, dfu)
        dx_mid, dsh2, dsc2, dg2 = _norm_mod_bwd(sv["x_mid"], dh2.reshape(nb, seq, D), dx,
                                                rms_g2[l].reshape(1, D), sc2)
        dmo, dgt1 = _gate_bwd(dx_mid, sv["mo"].reshape(nb, seq, D), gt1)
        dmo = dmo.reshape(t, D)
        dmerged = _mm_nt("out_dgrad", dmo, gw["w_out"], BF16)
        g_out = _mm_tn("out_wgrad", sv["merged"], dmo)
        dls_dys = _merge_bwd(dmerged, sv["ys"], sv["proj"])
        dls, dys = dls_dys[:3], dls_dys[3:]
        dbrs, g_br = [], []
        for n in range(NB):
            dbrs.append(_mm_nt("branch_dgrad", dys[n], gw["w_branch"], BF16, w_lead=n))
            g_br.append(_mm_tn("branch_wgrad", sv["brs"][n], dys[n]))
        proj3 = sv["proj"].reshape(nb, seq, IN_COLS)
        d_gm, g_ws, g_bs, g_lg, g_lb = _gmlp_bwd(sv["proj"], dbrs[0], gm_ln_g[l].reshape(1, BW),
                                                 gm_ln_b[l].reshape(1, BW), gm_w_spatial[l], gm_b_spatial[l].T)
        g_br_dev = jnp.transpose(jnp.stack(g_br).reshape(NB, BW, NDEV, D // NDEV), (2, 0, 1, 3))
        carried = [g_br_dev, g_out.reshape(NDEV, D // NDEV, D), g_ffn_in, g_ffn_out.reshape(NDEV, FF_HALF_PAD, D)]
        d_sb, recv = _sb_bwd(proj3, dbrs[1].reshape(nb, seq, BW), sv["sb_o"], carried)
        bufs["w_branch"] = _adamw_layer("adamw_w_branch", recv[0], w_branch, m_w_branch, v_w_branch, l,
                                        bufs["w_branch"])
        bufs["w_out"] = _adamw_layer("adamw_w_out", recv[1], w_out, m_w_out, v_w_out, l, bufs["w_out"])
        r_fi = recv[2].reshape(NDEV, 2, FF_HALF_PAD, D)[:, :, :FF_HALF].reshape(NDEV, FF_IN_SHARD, D)
        bufs["w_ffn_in"] = _adamw_layer("adamw_w_ffn_in", r_fi, w_ffn_in_t, m_w_ffn_in_t, v_w_ffn_in_t, l,
                                        bufs["w_ffn_in"])
        bufs["w_ffn_out"] = _adamw_layer("adamw_w_ffn_out", recv[3][:, :FF_HALF], w_ffn_out, m_w_ffn_out,
                                         v_w_ffn_out, l, bufs["w_ffn_out"])
        d_pool, g_pw, g_ps = _pool_bwd(proj3, dbrs[2].reshape(nb, seq, BW), pool_w[l], pool_scale[l].reshape(1, BW))
        dproj = jnp.concatenate([d_gm] + [a.reshape(t, BW) for a in d_sb] + [d_pool.reshape(t, BW)] + list(dls),
                                axis=1)
        g_in = _mm_colblocked_tn("proj_wgrad", sv["h"], dproj)
        dh, (r_in,) = _mm_colblocked_nt("proj_dgrad", dproj, gw["w_in"], F32, [g_in])
        bufs["w_in"] = _adamw_layer("adamw_w_in", r_in, w_in, m_w_in, v_w_in, l, bufs["w_in"])
        dx, dsh1, dsc1, dg1 = _norm_mod_bwd(sv["x_in"], dh.reshape(nb, seq, D), dx_mid,
                                            rms_g1[l].reshape(1, D), sc1)

        dmod[l] = jnp.concatenate([dsh1, dsc1, dgt1, dsh2, dsc2, dgt2], axis=-1)
        small_parts["rms_g1"][l] = jnp.sum(dg1, axis=0)
        small_parts["rms_g2"][l] = jnp.sum(dg2, axis=0)
        small_parts["gm_ln_g"][l] = g_lg
        small_parts["gm_ln_b"][l] = g_lb
        small_parts["gm_w_spatial"][l] = g_ws
        small_parts["gm_b_spatial"][l] = g_bs[:, :, 0]
        small_parts["pool_w"][l] = g_pw
        small_parts["pool_scale"][l] = g_ps

    dmod_mine = jnp.stack(dmod).reshape(nl, nb, NMOD * D)
    names = list(small_parts)
    stacked = [jnp.stack(small_parts[k]).astype(BF16 if k in ("gm_w_spatial", "pool_w") else F32) for k in names]
    gathered_small = _exchange(stacked + [dfinal_part, dmod_mine], "gather_small", False)
    dmod_all = jnp.transpose(gathered_small[-1], (1, 0, 2, 3)).reshape(nl, ntot, NMOD * D)
    dfinal_all = gathered_small[-2].reshape(ntot, D)

    results = {}
    weights = dict(rms_g1=(rms_g1, m_rms_g1, v_rms_g1), rms_g2=(rms_g2, m_rms_g2, v_rms_g2),
                   gm_ln_g=(gm_ln_g, m_gm_ln_g, v_gm_ln_g), gm_ln_b=(gm_ln_b, m_gm_ln_b, v_gm_ln_b),
                   gm_w_spatial=(gm_w_spatial, m_gm_w_spatial, v_gm_w_spatial),
                   gm_b_spatial=(gm_b_spatial, m_gm_b_spatial, v_gm_b_spatial),
                   pool_w=(pool_w, m_pool_w, v_pool_w), pool_scale=(pool_scale, m_pool_scale, v_pool_scale))
    for k, parts in zip(names, gathered_small[:len(names)]):
        w, m, v = weights[k]
        results[k] = _adamw_reduce("adamw_" + k, parts.reshape((NDEV,) + w.shape), w, m, v)
    results["final_g"] = _adamw_reduce("adamw_final_g", dfinal_all, final_g, m_final_g, v_final_g)
    results["b_ada"] = _adamw_reduce("adamw_b_ada", jnp.transpose(dmod_all, (1, 0, 2)), b_ada, m_b_ada, v_b_ada)
    dmod_blk = lax.dynamic_slice_in_dim(dmod_all, me * 768, 768, axis=2)
    g_w_ada = _ada_bwd(c_all, dmod_blk)
    results["w_ada"] = _adamw_reduce("adamw_w_ada", g_w_ada[None], w_ada, m_w_ada, v_w_ada)
    stacked_w = dict(w_in=w_in, w_branch=w_branch, w_out=w_out, w_ffn_in=w_ffn_in_t, w_ffn_out=w_ffn_out)
    for name in big_names:
        results[name] = tuple(b.reshape(stacked_w[name].shape) for b in bufs[name])
    results["w_ffn_in"] = tuple(jnp.swapaxes(b, 1, 2) for b in results["w_ffn_in"])

    order = ["rms_g1", "rms_g2", "w_ada", "b_ada", "w_in", "gm_ln_g", "gm_ln_b", "gm_w_spatial", "gm_b_spatial",
             "pool_w", "pool_scale", "w_branch", "w_out", "w_ffn_in", "w_ffn_out", "final_g"]
    out = [loss, dx]
    for i in range(4):
        out.extend(results[k][i] for k in order)
    return tuple(out)
```

```python
import functools
import math

import jax
import jax.numpy as jnp
from jax import lax
from jax.experimental import pallas as pl
from jax.experimental.pallas import tpu as pltpu

F32 = jnp.float32
BF16 = jnp.bfloat16
MESH = pl.DeviceIdType.MESH

D = 1024
BW = 512
NB = 3
CH = 128
NG = 4
HD = 64
POOL_WINDOWS = (2, 4, 8, 16)
DFF = 2816
NMOD = 6
EPS = 1e-6
IN_COLS = 6 * D
NDEV = 8
FF_IN_SHARD = 2 * DFF // NDEV
FF_HALF = FF_IN_SHARD // 2
FF_HALF_PAD = 384
FF_IN_PAD = 2 * FF_HALF_PAD
FFP = NDEV // 2 * FF_IN_PAD

ADAM_LR = 0.001
ADAM_B1 = 0.9
ADAM_B2 = 0.999
ADAM_EPS = 1e-08
ADAM_WD = 0.01
ADAM_STEP = 10

VMEM_LIMIT = 48 * 1024 * 1024
BIG_ROWS = 2048

NN = (((1,), (0,)), ((), ()))
NT = (((1,), (1,)), ((), ()))
TN = (((0,), (0,)), ((), ()))


def _cp(sem=None):
    return pltpu.CompilerParams(dimension_semantics=sem, vmem_limit_bytes=VMEM_LIMIT)


def _dot(a, b, dims=NN):
    return lax.dot_general(a, b, dims, preferred_element_type=F32)


def _my_index():
    return 4 * lax.axis_index("x") + 2 * lax.axis_index("y") + lax.axis_index("c")


def _peer(k):
    x, y, c = lax.axis_index("x"), lax.axis_index("y"), lax.axis_index("c")
    px = 1 - x if k & 4 else x
    py = 1 - y if k & 2 else y
    pc = 1 - c if k & 1 else c
    return (px, py, pc), 4 * px + 2 * py + pc


def _exchange(xs, name, all_to_all):
    n = len(xs)

    def body(*refs):
        _exchange_start(refs[:n], refs[n:2 * n], refs[2 * n:], all_to_all)
        _exchange_finish(refs[:n], refs[n:2 * n], refs[2 * n:], all_to_all)

    return pl.pallas_call(
        body, name=name, out_shape=_exchange_out_shape(xs, all_to_all),
        in_specs=[_HBM] * n, out_specs=[_HBM] * n, scratch_shapes=_exchange_sems(n),
    )(*xs)


_HBM = pl.BlockSpec(memory_space=pl.ANY)


def _exchange_out_shape(xs, all_to_all):
    if all_to_all:
        return [jax.ShapeDtypeStruct(x.shape, x.dtype) for x in xs]
    return [jax.ShapeDtypeStruct((NDEV,) + x.shape, x.dtype) for x in xs]


def _exchange_sems(n):
    return [pltpu.SemaphoreType.DMA((n * 7,)), pltpu.SemaphoreType.DMA((n * 7,)), pltpu.SemaphoreType.DMA((n,))]


def _exchange_copies(ins, outs, sems, all_to_all):
    send_sems, recv_sems, local_sems = sems
    me = _my_index()
    local, sends, recvs = [], [], []
    for a in range(len(ins)):
        src = ins[a].at[me] if all_to_all else ins[a]
        local.append(pltpu.make_async_copy(src, outs[a].at[me], local_sems.at[a]))
    for k in range(1, NDEV):
        dev, idx = _peer(k)
        for a in range(len(ins)):
            src = ins[a].at[idx] if all_to_all else ins[a]
            sem = dict(send_sem=send_sems.at[a * 7 + k - 1], recv_sem=recv_sems.at[a * 7 + k - 1],
                       device_id=dev, device_id_type=MESH)
            sends.append(pltpu.make_async_remote_copy(src_ref=src, dst_ref=outs[a].at[me], **sem))
            recvs.append(pltpu.make_async_remote_copy(src_ref=src, dst_ref=outs[a].at[idx], **sem))
    return local, sends, recvs


def _exchange_start(ins, outs, sems, all_to_all):
    local, sends, _ = _exchange_copies(ins, outs, sems, all_to_all)
    for cp in local + sends:
        cp.start()


def _exchange_finish(ins, outs, sems, all_to_all):
    local, sends, recvs = _exchange_copies(ins, outs, sems, all_to_all)
    for cp in sends:
        cp.wait_send()
    for cp in recvs:
        cp.wait_recv()
    for cp in local:
        cp.wait()


def _host_exchange(body, n_in, n_out, grid, xs, all_to_all):
    n = len(xs)
    if n == 0:
        return body, [], [], [], []

    def hosted(*refs):
        ins, ex_ins = refs[:n_in], refs[n_in:n_in + n]
        outs, ex_outs = refs[n_in + n:n_in + n + n_out], refs[n_in + n + n_out:n_in + 2 * n + n_out]
        scratch = refs[n_in + 2 * n + n_out:]
        own, sems = scratch[:len(scratch) - 3], scratch[len(scratch) - 3:]
        first = functools.reduce(jnp.logical_and, [pl.program_id(a) == 0 for a in range(len(grid))])
        last = functools.reduce(jnp.logical_and, [pl.program_id(a) == grid[a] - 1 for a in range(len(grid))])

        @pl.when(first)
        def _():
            _exchange_start(ex_ins, ex_outs, sems, all_to_all)

        body(*ins, *outs, *own)

        @pl.when(last)
        def _():
            _exchange_finish(ex_ins, ex_outs, sems, all_to_all)

    return hosted, [_HBM] * n, [_HBM] * n, _exchange_out_shape(xs, all_to_all), _exchange_sems(n)


def _mm(name, a, b, grid, a_spec, b_spec, o_spec, out_sds, dims, acc_shape, carried=(), all_to_all=True):
    nk = grid[2]

    if nk == 1:
        def body(a_ref, b_ref, o_ref):
            o_ref[...] = _dot(a_ref[...].astype(BF16), b_ref[...].astype(BF16), dims).astype(o_ref.dtype)
        scratch = []
    else:
        def body(a_ref, b_ref, o_ref, acc_ref):
            k = pl.program_id(2)

            @pl.when(k == 0)
            def _():
                acc_ref[...] = jnp.zeros_like(acc_ref)

            acc_ref[...] += _dot(a_ref[...].astype(BF16), b_ref[...].astype(BF16), dims)

            @pl.when(k == nk - 1)
            def _():
                o_ref[...] = acc_ref[...].astype(o_ref.dtype)
        scratch = [pltpu.VMEM(acc_shape, F32)]

    body, ex_in, ex_out, ex_shape, ex_sems = _host_exchange(body, 2, 1, grid, carried, all_to_all)
    outs = pl.pallas_call(
        body, name=name, grid=grid, in_specs=[a_spec, b_spec] + ex_in, out_specs=[o_spec] + ex_out,
        out_shape=[out_sds] + ex_shape,
        scratch_shapes=scratch + ex_sems,
        compiler_params=_cp(("arbitrary",) * 3 if carried else ("parallel", "parallel", "arbitrary")),
    )(a, b, *carried)
    return (outs[0], outs[1:]) if carried else outs[0]


def _row_tile(t, want):
    tm = min(t, want)
    assert t % tm == 0
    return tm


def _mm_colblocked(name, a, wg, out_dtype, gather=()):
    t = a.shape[0]
    tm = _row_tile(t, BIG_ROWS)
    return _mm(name, a, wg, (t // tm, NDEV, 1),
               pl.BlockSpec((tm, D), lambda i, j, k: (i, 0)),
               pl.BlockSpec((None, D, 768), lambda i, j, k: (j, 0, 0)),
               pl.BlockSpec((tm, 768), lambda i, j, k: (i, j)),
               jax.ShapeDtypeStruct((t, NDEV * 768), out_dtype), NN, (tm, 768), gather, False)


def _mm_colblocked_nt(name, g, wg, out_dtype, scatter=()):
    t = g.shape[0]
    tm = _row_tile(t, BIG_ROWS)
    return _mm(name, g, wg, (t // tm, 1, NDEV),
               pl.BlockSpec((tm, 768), lambda i, j, k: (i, k)),
               pl.BlockSpec((None, D, 768), lambda i, j, k: (k, 0, 0)),
               pl.BlockSpec((tm, D), lambda i, j, k: (i, 0)),
               jax.ShapeDtypeStruct((t, D), out_dtype), NT, (tm, D), scatter)


_HALF = NDEV // 2


def _ffn_in_fwd(h2, wg, gather=()):
    t = h2.shape[0]
    tm = _row_tile(t, 1024)

    def body(a_ref, wg_ref, wu_ref, g_ref, u_ref, act_ref):
        a = a_ref[...]
        g = _dot(a, wg_ref[...])
        u = _dot(a, wu_ref[...])
        g_ref[...] = g.astype(BF16)
        u_ref[...] = u.astype(BF16)
        act_ref[...] = (g * jax.nn.sigmoid(g) * u).astype(BF16)

    tile = pl.BlockSpec((tm, 768), lambda i, j: (i, j))
    grid = (t // tm, _HALF)
    body, ex_in, ex_out, ex_shape, ex_sems = _host_exchange(body, 3, 3, grid, gather, False)
    outs = pl.pallas_call(
        body, name="ffn_in_fwd", grid=grid,
        in_specs=[pl.BlockSpec((tm, D), lambda i, j: (i, 0)),
                  pl.BlockSpec((None, D, 768), lambda i, j: (j, 0, 0)),
                  pl.BlockSpec((None, D, 768), lambda i, j: (j + _HALF, 0, 0))] + ex_in,
        out_specs=[tile] * 3 + ex_out, out_shape=[jax.ShapeDtypeStruct((t, FFP), BF16)] * 3 + ex_shape,
        scratch_shapes=ex_sems,
        compiler_params=_cp(("arbitrary", "arbitrary") if gather else ("parallel", "parallel")),
    )(h2, wg, wg, *gather)
    return outs[0], outs[1], outs[2], outs[3:]


def _ffn_out_dgrad(dfo, w, fg, fu):
    t = dfo.shape[0]
    tm = _row_tile(t, 1024)

    def body(a_ref, w_ref, g_ref, u_ref, dg_ref, du_ref):
        d = _dot(a_ref[...], w_ref[...], NT)
        g = g_ref[...].astype(F32)
        s = jax.nn.sigmoid(g)
        dg_ref[...] = (d * u_ref[...].astype(F32) * (s * (1.0 + g * (1.0 - s)))).astype(BF16)
        du_ref[...] = (d * (g * s)).astype(BF16)

    tile = pl.BlockSpec((tm, 768), lambda i, j: (i, j))
    return pl.pallas_call(
        body, name="ffn_out_dgrad", grid=(t // tm, _HALF),
        in_specs=[pl.BlockSpec((tm, D), lambda i, j: (i, 0)),
                  pl.BlockSpec((768, D), lambda i, j: (j, 0)), tile, tile],
        out_specs=[tile] * 2, out_shape=[jax.ShapeDtypeStruct((t, FFP), BF16)] * 2,
        compiler_params=_cp(("parallel", "parallel")),
    )(dfo, w, fg, fu)


def _ffn_in_dgrad(dg, du, wg):
    t = dg.shape[0]
    tm = _row_tile(t, BIG_ROWS)

    def body(g_ref, u_ref, w_ref, o_ref, acc_ref):
        k = pl.program_id(1)

        @pl.when(k == 0)
        def _():
            acc_ref[...] = jnp.zeros_like(acc_ref)

        @pl.when(k < _HALF)
        def _():
            acc_ref[...] += _dot(g_ref[...], w_ref[...], NT)

        @pl.when(k >= _HALF)
        def _():
            acc_ref[...] += _dot(u_ref[...], w_ref[...], NT)

        @pl.when(k == NDEV - 1)
        def _():
            o_ref[...] = acc_ref[...]

    return pl.pallas_call(
        body, name="ffn_in_dgrad", grid=(t // tm, NDEV),
        in_specs=[pl.BlockSpec((tm, 768), lambda i, k: (i, jnp.minimum(k, _HALF - 1))),
                  pl.BlockSpec((tm, 768), lambda i, k: (i, jnp.maximum(k - _HALF, 0))),
                  pl.BlockSpec((None, D, 768), lambda i, k: (k, 0, 0))],
        out_specs=pl.BlockSpec((tm, D), lambda i, k: (i, 0)),
        out_shape=jax.ShapeDtypeStruct((t, D), F32),
        scratch_shapes=[pltpu.VMEM((tm, D), F32)],
        compiler_params=_cp(("parallel", "arbitrary")),
    )(dg, du, wg)


def _ffn_in_wgrad(h2, dg, du):
    t = h2.shape[0]
    tk = _row_tile(t, BIG_ROWS)
    nk = t // tk

    def body(a_ref, g_ref, u_ref, o_ref, acc_ref):
        j, k = pl.program_id(0), pl.program_id(1)

        @pl.when(k == 0)
        def _():
            acc_ref[...] = jnp.zeros_like(acc_ref)

        @pl.when(j < _HALF)
        def _():
            acc_ref[...] += _dot(g_ref[...], a_ref[...], TN)

        @pl.when(j >= _HALF)
        def _():
            acc_ref[...] += _dot(u_ref[...], a_ref[...], TN)

        @pl.when(k == nk - 1)
        def _():
            o_ref[...] = acc_ref[...].astype(BF16)

    return pl.pallas_call(
        body, name="ffn_in_wgrad", grid=(NDEV, nk),
        in_specs=[pl.BlockSpec((tk, D), lambda j, k: (k, 0)),
                  pl.BlockSpec((tk, 768), lambda j, k: (jnp.where(j < _HALF, k, 0), jnp.minimum(j, _HALF - 1))),
                  pl.BlockSpec((tk, 768), lambda j, k: (jnp.where(j < _HALF, 0, k), jnp.maximum(j - _HALF, 0)))],
        out_specs=pl.BlockSpec((None, 768, D), lambda j, k: (j, 0, 0)),
        out_shape=jax.ShapeDtypeStruct((NDEV, 768, D), BF16),
        scratch_shapes=[pltpu.VMEM((768, D), F32)],
        compiler_params=_cp(("parallel", "arbitrary")),
    )(h2, dg, du)


def _mm_colblocked_tn(name, a, g):
    t = a.shape[0]
    tk = _row_tile(t, BIG_ROWS)
    return _mm(name, a, g, (1, NDEV, t // tk),
               pl.BlockSpec((tk, D), lambda i, j, k: (k, 0)),
               pl.BlockSpec((tk, 768), lambda i, j, k: (k, j)),
               pl.BlockSpec((None, D, 768), lambda i, j, k: (j, 0, 0)),
               jax.ShapeDtypeStruct((NDEV, D, 768), BF16), TN, (D, 768))


def _mm_nt(name, a, w, out_dtype, a_col=0, w_lead=None):
    t = a.shape[0]
    if w_lead is None:
        kdim, n = w.shape
        b_spec = pl.BlockSpec((min(kdim, 1024), n), lambda i, j, k: (j, 0))
    else:
        _, kdim, n = w.shape
        b_spec = pl.BlockSpec((None, min(kdim, 1024), n), lambda i, j, k: (w_lead, j, 0))
    tn = min(kdim, 1024)
    tm = _row_tile(t, BIG_ROWS)
    return _mm(name, a, w, (t // tm, kdim // tn, 1),
               pl.BlockSpec((tm, n), lambda i, j, k: (i, a_col)),
               b_spec,
               pl.BlockSpec((tm, tn), lambda i, j, k: (i, j)),
               jax.ShapeDtypeStruct((t, kdim), out_dtype), NT, (tm, tn))


def _mm_tn(name, a, g, out_dtype=BF16):
    t, kdim = a.shape
    n = g.shape[1]
    tk = _row_tile(t, BIG_ROWS)
    tm = min(kdim, 1024)
    tn = min(n, 1024)
    return _mm(name, a, g, (kdim // tm, n // tn, t // tk),
               pl.BlockSpec((tk, tm), lambda i, j, k: (k, i)),
               pl.BlockSpec((tk, tn), lambda i, j, k: (k, j)),
               pl.BlockSpec((tm, tn), lambda i, j, k: (i, j)),
               jax.ShapeDtypeStruct((kdim, n), out_dtype), TN, (tm, tn))


def _mm_residual(name, a, w, x, gt, seq):
    t, kdim = a.shape
    tm = _row_tile(seq, 1024)
    tn = D
    tk = min(kdim, 1024)
    nk = kdim // tk
    per = seq // tm

    def body(a_ref, w_ref, x_ref, gt_ref, xo_ref, y_ref, acc_ref):
        k = pl.program_id(2)

        @pl.when(k == 0)
        def _():
            acc_ref[...] = jnp.zeros_like(acc_ref)

        acc_ref[...] += _dot(a_ref[...], w_ref[...])

        @pl.when(k == nk - 1)
        def _():
            y = acc_ref[...]
            xo_ref[...] = x_ref[...] + gt_ref[0] * y
            y_ref[...] = y.astype(BF16)

    return pl.pallas_call(
        body, name=name, grid=(t // tm, D // tn, nk),
        in_specs=[pl.BlockSpec((tm, tk), lambda i, j, k: (i, k)),
                  pl.BlockSpec((tk, tn), lambda i, j, k: (k, j)),
                  pl.BlockSpec((tm, tn), lambda i, j, k: (i, j)),
                  pl.BlockSpec((1, 1, tn), lambda i, j, k: (i // per, 0, j))],
        out_specs=[pl.BlockSpec((tm, tn), lambda i, j, k: (i, j)),
                   pl.BlockSpec((tm, tn), lambda i, j, k: (i, j))],
        out_shape=[jax.ShapeDtypeStruct((t, D), F32), jax.ShapeDtypeStruct((t, D), BF16)],
        scratch_shapes=[pltpu.VMEM((tm, tn), F32)],
        compiler_params=_cp(("parallel", "parallel", "arbitrary")),
    )(a, w, x, gt)


def _ada_fwd(c_all, w_ada, b_blk):
    nl = w_ada.shape[0]
    nb = c_all.shape[0]

    def body(c_ref, w_ref, b_ref, o_ref):
        c = c_ref[...]
        ca = (c * jax.nn.sigmoid(c)).astype(BF16)
        o_ref[...] = _dot(ca, w_ref[...].astype(BF16)) + b_ref[...]

    return pl.pallas_call(
        body, name="ada_fwd", grid=(nl,),
        in_specs=[pl.BlockSpec((nb, D), lambda l: (0, 0)),
                  pl.BlockSpec((None, D, 768), lambda l: (l, 0, 0)),
                  pl.BlockSpec((None, 1, 768), lambda l: (l, 0, 0))],
        out_specs=pl.BlockSpec((None, nb, 768), lambda l: (l, 0, 0)),
        out_shape=jax.ShapeDtypeStruct((nl, nb, 768), F32),
        compiler_params=_cp(("parallel",)),
    )(c_all, w_ada, b_blk)


def _ada_bwd(c_all, dmod_blk):
    nl = dmod_blk.shape[0]
    nb = c_all.shape[0]

    def body(c_ref, d_ref, o_ref):
        c = c_ref[...]
        ca = (c * jax.nn.sigmoid(c)).astype(BF16)
        o_ref[...] = _dot(ca, d_ref[...].astype(BF16), TN)

    return pl.pallas_call(
        body, name="ada_bwd", grid=(nl,),
        in_specs=[pl.BlockSpec((nb, D), lambda l: (0, 0)),
                  pl.BlockSpec((None, nb, 768), lambda l: (l, 0, 0))],
        out_specs=pl.BlockSpec((None, D, 768), lambda l: (l, 0, 0)),
        out_shape=jax.ShapeDtypeStruct((nl, D, 768), F32),
        compiler_params=_cp(("parallel",)),
    )(c_all, dmod_blk)


def _seq_tile(seq):
    return _row_tile(seq, 512)


def _norm_mod_fwd(x, g, sc, sh):
    nb, seq, _ = x.shape
    ts = _seq_tile(seq)

    def body(x_ref, g_ref, sc_ref, sh_ref, h_ref):
        xv = x_ref[0]
        r = lax.rsqrt(jnp.mean(xv * xv, axis=-1, keepdims=True) + EPS)
        h_ref[0] = ((xv * r) * g_ref[...] * (1.0 + sc_ref[0]) + sh_ref[0]).astype(BF16)

    return pl.pallas_call(
        body, name="norm_mod_fwd", grid=(nb, seq // ts),
        in_specs=[pl.BlockSpec((1, ts, D), lambda b, s: (b, s, 0)),
                  pl.BlockSpec((1, D), lambda b, s: (0, 0)),
                  pl.BlockSpec((1, 1, D), lambda b, s: (b, 0, 0)),
                  pl.BlockSpec((1, 1, D), lambda b, s: (b, 0, 0))],
        out_specs=pl.BlockSpec((1, ts, D), lambda b, s: (b, s, 0)),
        out_shape=jax.ShapeDtypeStruct((nb, seq, D), BF16),
        compiler_params=_cp(("parallel", "parallel")),
    )(x, g, sc, sh)


def _gate_bwd_tile(d, y_ref, gt_ref, dy_ref, dgt_ref):
    @pl.when(pl.program_id(1) == 0)
    def _():
        dgt_ref[...] = jnp.zeros_like(dgt_ref)

    dy_ref[0] = (gt_ref[0] * d).astype(BF16)
    dgt_ref[0] += jnp.sum(d * y_ref[0].astype(F32), axis=0, keepdims=True)


def _norm_mod_bwd(x, dh, dres, g, sc, gate=None):
    nb, seq, _ = x.shape
    ts = _seq_tile(seq)

    def body(x_ref, dh_ref, dres_ref, g_ref, sc_ref, *rest):
        if gate is None:
            dx_ref, dsh_ref, dsc_ref, dg_ref = rest
        else:
            y_ref, gt_ref, dx_ref, dsh_ref, dsc_ref, dg_ref, dy_ref, dgt_ref = rest

        @pl.when(pl.program_id(1) == 0)
        def _():
            dsh_ref[...] = jnp.zeros_like(dsh_ref)
            dsc_ref[...] = jnp.zeros_like(dsc_ref)
            dg_ref[...] = jnp.zeros_like(dg_ref)

        xv = x_ref[0]
        dh = dh_ref[0]
        gv = g_ref[...]
        onesc = 1.0 + sc_ref[0]
        r = lax.rsqrt(jnp.mean(xv * xv, axis=-1, keepdims=True) + EPS)
        xh = xv * r
        dsh_ref[0] += jnp.sum(dh, axis=0, keepdims=True)
        dsc_ref[0] += jnp.sum(dh * (xh * gv), axis=0, keepdims=True)
        dg_ref[0] += jnp.sum(dh * onesc * xh, axis=0, keepdims=True)
        dxh = dh * (gv * onesc)
        dx = r * (dxh - xh * jnp.mean(dxh * xh, axis=-1, keepdims=True))
        dx_total = dres_ref[0] + dx
        dx_ref[0] = dx_total
        if gate is not None:
            _gate_bwd_tile(dx_total, y_ref, gt_ref, dy_ref, dgt_ref)

    vec = jax.ShapeDtypeStruct((nb, 1, D), F32)
    vspec = pl.BlockSpec((1, 1, D), lambda b, s: (b, 0, 0))
    tile = pl.BlockSpec((1, ts, D), lambda b, s: (b, s, 0))
    gated = gate is not None
    return pl.pallas_call(
        body, name="norm_mod_bwd", grid=(nb, seq // ts),
        in_specs=[tile, tile, tile, pl.BlockSpec((1, D), lambda b, s: (0, 0)), vspec] + [tile, vspec] * gated,
        out_specs=[tile, vspec, vspec, vspec] + [tile, vspec] * gated,
        out_shape=[jax.ShapeDtypeStruct((nb, seq, D), F32), vec, vec, vec]
        + [jax.ShapeDtypeStruct((nb, seq, D), BF16), vec] * gated,
        compiler_params=_cp(("parallel", "arbitrary")),
    )(x, dh, dres, g, sc, *(gate or ()))


def _loss_head(x, tgt, g, y, gt):
    nb, seq, _ = x.shape
    ts = _seq_tile(seq)

    def body(x_ref, t_ref, g_ref, y_ref, gt_ref, dx_ref, loss_ref, dg_ref, dy_ref, dgt_ref):
        @pl.when(pl.program_id(1) == 0)
        def _():
            loss_ref[...] = jnp.zeros_like(loss_ref)
            dg_ref[...] = jnp.zeros_like(dg_ref)

        xv = x_ref[0]
        gv = g_ref[...]
        r = lax.rsqrt(jnp.mean(xv * xv, axis=-1, keepdims=True) + EPS)
        xh = xv * r
        err = xh * gv - t_ref[0]
        per_tok = jnp.mean(err * err, axis=-1, keepdims=True)
        loss_ref[0] += 0.5 * jnp.sum(per_tok, axis=0, keepdims=True)
        dy = err * (1.0 / D)
        dg_ref[0] += jnp.sum(dy * xh, axis=0, keepdims=True)
        dxh = dy * gv
        dx = r * (dxh - xh * jnp.mean(dxh * xh, axis=-1, keepdims=True))
        dx_ref[0] = dx
        _gate_bwd_tile(dx, y_ref, gt_ref, dy_ref, dgt_ref)

    tile = pl.BlockSpec((1, ts, D), lambda b, s: (b, s, 0))
    vspec = pl.BlockSpec((1, 1, D), lambda b, s: (b, 0, 0))
    vec = jax.ShapeDtypeStruct((nb, 1, D), F32)
    return pl.pallas_call(
        body, name="loss_head", grid=(nb, seq // ts),
        in_specs=[tile, tile, pl.BlockSpec((1, D), lambda b, s: (0, 0)), tile, vspec],
        out_specs=[tile, pl.BlockSpec((1, 1, 128), lambda b, s: (b, 0, 0)), vspec, tile, vspec],
        out_shape=[jax.ShapeDtypeStruct((nb, seq, D), F32), jax.ShapeDtypeStruct((nb, 1, 128), F32), vec,
                   jax.ShapeDtypeStruct((nb, seq, D), BF16), vec],
        compiler_params=_cp(("parallel", "arbitrary")),
    )(x, tgt, g, y, gt)


_GELU_C = math.sqrt(2.0 / math.pi)


def _gelu(x):
    return 0.5 * x * (1.0 + jnp.tanh(_GELU_C * (x + 0.044715 * (x * x * x))))


def _gelu_and_grad(x):
    t = jnp.tanh(_GELU_C * (x + 0.044715 * (x * x * x)))
    y = 0.5 * x * (1.0 + t)
    dy = 0.5 * (1.0 + t) + 0.5 * x * (1.0 - t * t) * (_GELU_C * (1.0 + 3.0 * 0.044715 * (x * x)))
    return y, dy


def _tril_mask():
    row = lax.broadcasted_iota(jnp.int32, (CH, CH), 0)
    col = lax.broadcasted_iota(jnp.int32, (CH, CH), 1)
    return row >= col


def _gmlp_fwd(proj, ln_g, ln_b, ws, bst):
    t = proj.shape[0]
    tm = _row_tile(t, 512)

    def body(u_ref, v_ref, lg_ref, lb_ref, ws_ref, bst_ref, o_ref):
        tril = _tril_mask()
        wm = [jnp.where(tril, ws_ref[g], 0.0).astype(BF16) for g in range(NG)]
        for ch in range(tm // CH):
            rows = slice(ch * CH, (ch + 1) * CH)
            u = _gelu(u_ref[rows, :].astype(F32))
            v = _gelu(v_ref[rows, :].astype(F32))
            mu = jnp.mean(v, axis=-1, keepdims=True)
            xc = v - mu
            rstd = lax.rsqrt(jnp.mean(xc * xc, axis=-1, keepdims=True) + EPS)
            vn = ((xc * rstd) * lg_ref[...] + lb_ref[...]).astype(BF16)
            for g in range(NG):
                cols = slice(g * CH, (g + 1) * CH)
                s = _dot(wm[g], vn[:, cols]) + bst_ref[:, g:g + 1]
                o_ref[rows, cols] = (u[:, cols] * s).astype(BF16)

    return pl.pallas_call(
        body, name="gmlp_fwd", grid=(t // tm,),
        in_specs=[pl.BlockSpec((tm, BW), lambda i: (i, 0)),
                  pl.BlockSpec((tm, BW), lambda i: (i, 1)),
                  pl.BlockSpec((1, BW), lambda i: (0, 0)),
                  pl.BlockSpec((1, BW), lambda i: (0, 0)),
                  pl.BlockSpec((NG, CH, CH), lambda i: (0, 0, 0)),
                  pl.BlockSpec((CH, NG), lambda i: (0, 0))],
        out_specs=pl.BlockSpec((tm, BW), lambda i: (i, 0)),
        out_shape=jax.ShapeDtypeStruct((t, BW), BF16),
        compiler_params=_cp(("parallel",)),
    )(proj, proj, ln_g, ln_b, ws, bst)


def _gmlp_bwd(proj, dout, ln_g, ln_b, ws, bst):
    t = proj.shape[0]
    tm = _row_tile(t, 512)

    def body(u_ref, v_ref, do_ref, lg_ref, lb_ref, ws_ref, bst_ref, dp_ref, gws_ref, gbs_ref, glg_ref, glb_ref):
        @pl.when(pl.program_id(0) == 0)
        def _():
            gws_ref[...] = jnp.zeros_like(gws_ref)
            gbs_ref[...] = jnp.zeros_like(gbs_ref)
            glg_ref[...] = jnp.zeros_like(glg_ref)
            glb_ref[...] = jnp.zeros_like(glb_ref)

        tril = _tril_mask()
        wm = [jnp.where(tril, ws_ref[g], 0.0).astype(BF16) for g in range(NG)]
        ones = jnp.ones((CH, CH), BF16)
        lg = lg_ref[...]
        for ch in range(tm // CH):
            rows = slice(ch * CH, (ch + 1) * CH)
            u, du_fac = _gelu_and_grad(u_ref[rows, :].astype(F32))
            v, dv_fac = _gelu_and_grad(v_ref[rows, :].astype(F32))
            do = do_ref[rows, :].astype(F32)
            mu = jnp.mean(v, axis=-1, keepdims=True)
            xc = v - mu
            rstd = lax.rsqrt(jnp.mean(xc * xc, axis=-1, keepdims=True) + EPS)
            xh = xc * rstd
            vn = (xh * lg + lb_ref[...]).astype(BF16)
            dvn_parts = []
            for g in range(NG):
                cols = slice(g * CH, (g + 1) * CH)
                s = _dot(wm[g], vn[:, cols]) + bst_ref[:, g:g + 1]
                dp_ref[rows, cols] = (do[:, cols] * s * du_fac[:, cols]).astype(BF16)
                ds = (do[:, cols] * u[:, cols]).astype(BF16)
                gws_ref[g] += jnp.where(tril, _dot(ds, vn[:, cols], NT), 0.0)
                gbs_ref[g] += _dot(ds, ones)
                dvn_parts.append(_dot(wm[g], ds, TN))
            dvn = jnp.concatenate(dvn_parts, axis=1)
            glb_ref[...] += jnp.sum(dvn, axis=0, keepdims=True)
            glg_ref[...] += jnp.sum(dvn * xh, axis=0, keepdims=True)
            dxh = dvn * lg
            dv = rstd * (dxh - jnp.mean(dxh, axis=-1, keepdims=True)
                         - xh * jnp.mean(dxh * xh, axis=-1, keepdims=True))
            dp_ref[rows, BW:2 * BW] = (dv * dv_fac).astype(BF16)

    small = pl.BlockSpec((NG, CH, CH), lambda i: (0, 0, 0))
    vec = pl.BlockSpec((1, BW), lambda i: (0, 0))
    return pl.pallas_call(
        body, name="gmlp_bwd", grid=(t // tm,),
        in_specs=[pl.BlockSpec((tm, BW), lambda i: (i, 0)),
                  pl.BlockSpec((tm, BW), lambda i: (i, 1)),
                  pl.BlockSpec((tm, BW), lambda i: (i, 0)),
                  vec, vec, small, pl.BlockSpec((CH, NG), lambda i: (0, 0))],
        out_specs=[pl.BlockSpec((tm, 2 * BW), lambda i: (i, 0)), small, small, vec, vec],
        out_shape=[jax.ShapeDtypeStruct((t, 2 * BW), BF16),
                   jax.ShapeDtypeStruct((NG, CH, CH), F32), jax.ShapeDtypeStruct((NG, CH, CH), F32),
                   jax.ShapeDtypeStruct((1, BW), F32), jax.ShapeDtypeStruct((1, BW), F32)],
        compiler_params=_cp(("arbitrary",)),
    )(proj, proj, dout, ln_g, ln_b, ws, bst)


def _pool_bands():
    row = lax.broadcasted_iota(jnp.int32, (CH, CH), 0)
    col = lax.broadcasted_iota(jnp.int32, (CH, CH), 1)
    cur, prev = [], []
    for w in POOL_WINDOWS:
        cur.append(jnp.where((row >= col) & (row - col < w), 1.0, 0.0).astype(BF16))
        prev.append(jnp.where(row + CH - col < w, 1.0, 0.0).astype(BF16))
    return cur, prev


def _pool_inv_count(r0, w):
    pos = r0 + lax.broadcasted_iota(jnp.int32, (CH, 1), 0)
    return 1.0 / jnp.minimum(pos + 1, w).astype(F32)


def _pool_diff(x_ref, r0, rp, has_prev, cur, prev, g):
    cols = slice(g * CH, (g + 1) * CH)
    xc = x_ref[pl.ds(r0, CH), cols]
    xp = x_ref[pl.ds(rp, CH), cols]
    ws = _dot(cur[g], xc) + has_prev * _dot(prev[g], xp)
    return ws * _pool_inv_count(r0, POOL_WINDOWS[g]) - xc.astype(F32)


def _pool_fwd(proj3, pw, pscale):
    nb, seq, _ = proj3.shape
    nch = seq // CH

    def body(x_ref, pw_ref, ps_ref, o_ref):
        cur, prev = _pool_bands()
        pwb = [pw_ref[g].astype(BF16) for g in range(NG)]

        def chunk(ch, carry):
            r0 = pl.multiple_of(ch * CH, CH)
            rp = pl.multiple_of(jnp.maximum(ch - 1, 0) * CH, CH)
            has_prev = jnp.where(ch > 0, 1.0, 0.0)
            for g in range(NG):
                cols = slice(g * CH, (g + 1) * CH)
                d = _pool_diff(x_ref, r0, rp, has_prev, cur, prev, g)
                y = _dot(d.astype(BF16), pwb[g]) * ps_ref[:, cols]
                o_ref[pl.ds(r0, CH), cols] = y.astype(BF16)
            return carry

        lax.fori_loop(0, nch, chunk, 0, unroll=2)

    return pl.pallas_call(
        body, name="pool_fwd", grid=(nb,),
        in_specs=[pl.BlockSpec((None, seq, BW), lambda b: (b, 0, 5)),
                  pl.BlockSpec((NG, CH, CH), lambda b: (0, 0, 0)),
                  pl.BlockSpec((1, BW), lambda b: (0, 0))],
        out_specs=pl.BlockSpec((None, seq, BW), lambda b: (b, 0, 0)),
        out_shape=jax.ShapeDtypeStruct((nb, seq, BW), BF16),
        compiler_params=_cp(("parallel",)),
    )(proj3, pw, pscale)


def _pool_bwd(proj3, dout3, pw, pscale):
    nb, seq, _ = proj3.shape
    nch = seq // CH

    def body(x_ref, do_ref, pw_ref, ps_ref, dx_ref, gpw_ref, gps_ref, e_ref):
        @pl.when(pl.program_id(0) == 0)
        def _():
            gpw_ref[...] = jnp.zeros_like(gpw_ref)
            gps_ref[...] = jnp.zeros_like(gps_ref)

        cur, prev = _pool_bands()
        pwb = [pw_ref[g].astype(BF16) for g in range(NG)]

        def first(ch, carry):
            r0 = pl.multiple_of(ch * CH, CH)
            rp = pl.multiple_of(jnp.maximum(ch - 1, 0) * CH, CH)
            has_prev = jnp.where(ch > 0, 1.0, 0.0)
            for g in range(NG):
                cols = slice(g * CH, (g + 1) * CH)
                d = _pool_diff(x_ref, r0, rp, has_prev, cur, prev, g).astype(BF16)
                do = do_ref[pl.ds(r0, CH), cols].astype(F32)
                ypre = _dot(d, pwb[g])
                gps_ref[:, cols] += jnp.sum(do * ypre, axis=0, keepdims=True)
                dyp = (do * ps_ref[:, cols]).astype(BF16)
                gpw_ref[g] += _dot(d, dyp, TN)
                e_ref[pl.ds(r0, CH), cols] = _dot(dyp, pwb[g], NT)
            return carry

        lax.fori_loop(0, nch, first, 0, unroll=2)

        def second(ch, carry):
            r0 = pl.multiple_of(ch * CH, CH)
            rn = pl.multiple_of(jnp.minimum(ch + 1, nch - 1) * CH, CH)
            has_next = jnp.where(ch < nch - 1, 1.0, 0.0)
            for g in range(NG):
                cols = slice(g * CH, (g + 1) * CH)
                w = POOL_WINDOWS[g]
                dd = e_ref[pl.ds(r0, CH), cols]
                ec = (dd * _pool_inv_count(r0, w)).astype(BF16)
                en = (e_ref[pl.ds(rn, CH), cols] * _pool_inv_count(rn, w)).astype(BF16)
                dx = _dot(cur[g], ec, TN) + has_next * _dot(prev[g], en, TN) - dd
                dx_ref[pl.ds(r0, CH), cols] = dx.astype(BF16)
            return carry

        lax.fori_loop(0, nch, second, 0, unroll=2)

    small = pl.BlockSpec((NG, CH, CH), lambda b: (0, 0, 0))
    vec = pl.BlockSpec((1, BW), lambda b: (0, 0))
    return pl.pallas_call(
        body, name="pool_bwd", grid=(nb,),
        in_specs=[pl.BlockSpec((None, seq, BW), lambda b: (b, 0, 5)),
                  pl.BlockSpec((None, seq, BW), lambda b: (b, 0, 0)), small, vec],
        out_specs=[pl.BlockSpec((None, seq, BW), lambda b: (b, 0, 0)), small, vec],
        out_shape=[jax.ShapeDtypeStruct((nb, seq, BW), BF16),
                   jax.ShapeDtypeStruct((NG, CH, CH), F32), jax.ShapeDtypeStruct((1, BW), F32)],
        scratch_shapes=[pltpu.VMEM((seq, BW), F32)],
        compiler_params=_cp(("arbitrary",)),
    )(proj3, dout3, pw, pscale)


SB_BQ = 256
SB_BK = 256
SB_SCALE = HD ** -0.5


SB_EXIT = -110.0


def _sb_tile(qs, k, mask):
    z = _dot(qs, k, NT)
    lb = jnp.minimum(z, 0.0) - jnp.log(1.0 + jnp.exp(-jnp.abs(z)))
    lom = lb - z
    if mask is not None:
        lom = jnp.where(mask, lom, 0.0)
    return lb, lom


def _sb_alive(c):
    top = functools.reduce(jnp.maximum, [jnp.max(state[1]) for state in c])
    return (top > SB_EXIT).astype(jnp.int32)


def _sb_past_blocks(step, c, npast):
    def cond(s):
        return jnp.logical_and(s[0] < npast, s[1] > 0)

    def body(s):
        i, _, c = s
        c = step(pl.multiple_of((npast - 1 - i) * SB_BK, SB_BK), c, None)
        return i + 1, _sb_alive(c), c

    return lax.while_loop(cond, body, (jnp.int32(0), _sb_alive(c), c))[2]


def _sb_diag_mask(bq, d):
    row = lax.broadcasted_iota(jnp.int32, (bq, SB_BK), 0)
    col = lax.broadcasted_iota(jnp.int32, (bq, SB_BK), 1)
    return col + d * SB_BK < row


def _sb_scaled(q):
    return (q.astype(F32) * SB_SCALE).astype(BF16)


def _dot_tri(a, m):
    return _dot(a.astype(BF16), m)


def _dot_tri2(a, m):
    hi = a.astype(BF16)
    lo = (a - hi.astype(F32)).astype(BF16)
    return _dot(hi, m) + _dot(lo, m)


def _sb_fwd(proj3, gather=()):
    nb, seq, _ = proj3.shape
    bq = min(SB_BQ, seq)
    nq = seq // bq
    ndiag = bq // SB_BK

    def body(q_ref, k_ref, v_ref, o_ref):
        row = lax.broadcasted_iota(jnp.int32, (SB_BK, SB_BK), 0)
        col = lax.broadcasted_iota(jnp.int32, (SB_BK, SB_BK), 1)
        upper = jnp.where(row > col, 1.0, 0.0).astype(BF16)
        heads = [slice(hh * HD, (hh + 1) * HD) for hh in range(2)]

        def qloop(qi, carry):
            q0 = pl.multiple_of(qi * bq, bq)
            qs = [_sb_scaled(q_ref[pl.ds(q0, bq), lanes]) for lanes in heads]

            def step(k0, c, mask):
                tiles = [_sb_tile(q, k_ref[pl.ds(k0, SB_BK), lanes], mask) for lanes, q in zip(heads, qs)]
                sums = [_dot_tri(lom, upper) for _, lom in tiles]
                out = []
                for lanes, (acc, cr), (lb, lom), cs in zip(heads, c, tiles, sums):
                    a = jnp.exp(lb + (cs + cr))
                    if mask is not None:
                        a = jnp.where(mask, a, 0.0)
                    rsum = cs[:, 0:1] + lom[:, 0:1]
                    out.append((acc + _dot(a.astype(BF16), v_ref[pl.ds(k0, SB_BK), lanes]), cr + rsum))
                return tuple(out)

            c = tuple((jnp.zeros((bq, HD), F32), jnp.zeros((bq, 1), F32)) for _ in heads)
            for d in reversed(range(ndiag)):
                c = step(pl.multiple_of(q0 + d * SB_BK, SB_BK), c, _sb_diag_mask(bq, d))
            c = _sb_past_blocks(step, c, qi * ndiag)
            for lanes, (acc, _) in zip(heads, c):
                o_ref[pl.ds(q0, bq), lanes] = acc
            return carry

        lax.fori_loop(0, nq, qloop, 0)

    def spec(c0):
        return pl.BlockSpec((None, seq, 128), lambda b, p: (b, 0, c0 + p))

    grid = (nb, BW // 128)
    body, ex_in, ex_out, ex_shape, ex_sems = _host_exchange(body, 3, 1, grid, gather, False)
    outs = pl.pallas_call(
        body, name="sb_fwd", grid=grid,
        in_specs=[spec(8), spec(12), spec(16)] + ex_in,
        out_specs=[spec(0)] + ex_out,
        out_shape=[jax.ShapeDtypeStruct((nb, seq, BW), F32)] + ex_shape,
        scratch_shapes=ex_sems,
        compiler_params=_cp(("arbitrary", "arbitrary")),
    )(proj3, proj3, proj3, *gather)
    return outs[0], outs[1:]


def _sb_bwd(proj3, do3, o3, scatter=()):
    nb, seq, _ = proj3.shape
    bq = min(SB_BQ, seq)
    nq = seq // bq
    ndiag = bq // SB_BK

    def body(q_ref, k_ref, v_ref, do_ref, o_ref, dq_ref, dk_ref, dv_ref, dk_acc, dv_acc):
        row = lax.broadcasted_iota(jnp.int32, (SB_BK, SB_BK), 0)
        col = lax.broadcasted_iota(jnp.int32, (SB_BK, SB_BK), 1)
        upper = jnp.where(row > col, 1.0, 0.0).astype(BF16)
        later = jnp.where(row >= col, 1.0, 0.0).astype(BF16)
        dk_acc[...] = jnp.zeros_like(dk_acc)
        dv_acc[...] = jnp.zeros_like(dv_acc)
        heads = [slice(hh * HD, (hh + 1) * HD) for hh in range(2)]

        def qloop(qi, carry):
            q0 = pl.multiple_of(qi * bq, bq)
            qs = [_sb_scaled(q_ref[pl.ds(q0, bq), lanes]) for lanes in heads]
            dos = [do_ref[pl.ds(q0, bq), lanes] for lanes in heads]
            gtot = [jnp.sum(do.astype(F32) * o_ref[pl.ds(q0, bq), lanes], axis=1, keepdims=True)
                    for do, lanes in zip(dos, heads)]

            def step(k0, c, mask):
                ks = [k_ref[pl.ds(k0, SB_BK), lanes] for lanes in heads]
                tiles = [_sb_tile(q, k, mask) for q, k in zip(qs, ks)]
                sums = [_dot_tri(lom, upper) for _, lom in tiles]
                das = [_dot(do, v_ref[pl.ds(k0, SB_BK), lanes], NT) for do, lanes in zip(dos, heads)]
                gls, avs = [], []
                for hh, (_, cr, _) in enumerate(c):
                    a = jnp.exp(tiles[hh][0] + (sums[hh] + cr))
                    if mask is not None:
                        a = jnp.where(mask, a, 0.0)
                    ab = a.astype(BF16)
                    avs.append(ab)
                    gls.append(das[hh] * ab.astype(F32))
                tails = [_dot_tri2(gl, later) for gl in gls]
                out = []
                for hh, (dq, cr, gdone) in enumerate(c):
                    lb, lom = tiles[hh]
                    pre = gtot[hh] - gdone - tails[hh]
                    dz = gls[hh] - jnp.exp(lb) * (gls[hh] + pre)
                    if mask is not None:
                        dz = jnp.where(mask, dz, 0.0)
                    dz = dz.astype(BF16)
                    dk_acc[hh, pl.ds(k0, SB_BK), :] += _dot(dz, qs[hh], TN)
                    dv_acc[hh, pl.ds(k0, SB_BK), :] += _dot(avs[hh], dos[hh], TN)
                    rsum = sums[hh][:, 0:1] + lom[:, 0:1]
                    out.append((dq + _dot(dz, ks[hh]), cr + rsum, gdone + tails[hh][:, 0:1]))
                return tuple(out)

            c = tuple((jnp.zeros((bq, HD), F32), jnp.zeros((bq, 1), F32), jnp.zeros((bq, 1), F32))
                      for _ in heads)
            for d in reversed(range(ndiag)):
                c = step(pl.multiple_of(q0 + d * SB_BK, SB_BK), c, _sb_diag_mask(bq, d))
            c = _sb_past_blocks(step, c, qi * ndiag)
            for lanes, (dq, _, _) in zip(heads, c):
                dq_ref[pl.ds(q0, bq), lanes] = (dq * SB_SCALE).astype(BF16)
            return carry

        lax.fori_loop(0, nq, qloop, 0)
        for hh in range(2):
            lanes = slice(hh * HD, (hh + 1) * HD)
            dk_ref[:, lanes] = dk_acc[hh].astype(BF16)
            dv_ref[:, lanes] = dv_acc[hh].astype(BF16)

    def spec(c0):
        return pl.BlockSpec((None, seq, 128), lambda b, p: (b, 0, c0 + p))

    grid = (nb, BW // 128)
    body, ex_in, ex_out, ex_shape, ex_sems = _host_exchange(body, 5, 3, grid, scatter, True)
    outs = pl.pallas_call(
        body, name="sb_bwd", grid=grid,
        in_specs=[spec(8), spec(12), spec(16), spec(0), spec(0)] + ex_in,
        out_specs=[spec(0), spec(0), spec(0)] + ex_out,
        out_shape=[jax.ShapeDtypeStruct((nb, seq, BW), BF16)] * 3 + ex_shape,
        scratch_shapes=[pltpu.VMEM((2, seq, HD), F32), pltpu.VMEM((2, seq, HD), F32)] + ex_sems,
        compiler_params=_cp(("arbitrary", "arbitrary")),
    )(proj3, proj3, proj3, do3, o3, *scatter)
    return outs[:3], outs[3:]


def _merge_fwd(brs, wb, proj):
    t = proj.shape[0]
    tm = _row_tile(t, 512)
    tn = 512
    nj = D // tn

    def body(b0, b1, b2, wb_ref, l0, l1, l2, m_ref, y0, y1, y2):
        acc = None
        for br, n, lg, y_ref in ((b0, 0, l0, y0), (b1, 1, l1, y1), (b2, 2, l2, y2)):
            y = _dot(br[...].astype(BF16), wb_ref[n])
            y_ref[...] = y.astype(BF16)
            term = jax.nn.sigmoid(lg[...].astype(F32)) * y
            acc = term if acc is None else acc + term
        m_ref[...] = acc.astype(BF16)

    def lspec(n):
        return pl.BlockSpec((tm, tn), lambda i, j: (i, (3 * D + n * D) // tn + j))

    tile = pl.BlockSpec((tm, tn), lambda i, j: (i, j))
    bspec = pl.BlockSpec((tm, BW), lambda i, j: (i, 0))
    return pl.pallas_call(
        body, name="merge_fwd", grid=(t // tm, nj),
        in_specs=[bspec, bspec, bspec, pl.BlockSpec((NB, BW, tn), lambda i, j: (0, 0, j)),
                  lspec(0), lspec(1), lspec(2)],
        out_specs=[tile] * 4,
        out_shape=[jax.ShapeDtypeStruct((t, D), BF16)] * 4,
        compiler_params=_cp(("parallel", "parallel")),
    )(brs[0], brs[1], brs[2], wb, proj, proj, proj)


def _merge_bwd(dm, ys, proj):
    t = proj.shape[0]
    tm = _row_tile(t, 512)
    tn = 512

    def body(dm_ref, y0, y1, y2, l0, l1, l2, dl0, dl1, dl2, dy0, dy1, dy2):
        dmv = dm_ref[...].astype(F32)
        for y_ref, lg, dl_ref, dy_ref in ((y0, l0, dl0, dy0), (y1, l1, dl1, dy1), (y2, l2, dl2, dy2)):
            g = jax.nn.sigmoid(lg[...].astype(F32))
            dl_ref[...] = (dmv * y_ref[...].astype(F32) * g * (1.0 - g)).astype(BF16)
            dy_ref[...] = (dmv * g).astype(BF16)

    def lspec(n):
        return pl.BlockSpec((tm, tn), lambda i, j: (i, (3 * D + n * D) // tn + j))

    tile = pl.BlockSpec((tm, tn), lambda i, j: (i, j))
    return pl.pallas_call(
        body, name="merge_bwd", grid=(t // tm, D // tn),
        in_specs=[tile] * 4 + [lspec(0), lspec(1), lspec(2)],
        out_specs=[tile] * 6,
        out_shape=[jax.ShapeDtypeStruct((t, D), BF16)] * 6,
        compiler_params=_cp(("parallel", "parallel")),
    )(dm, ys[0], ys[1], ys[2], proj, proj, proj)


def _adamw_rows(rows):
    if rows <= 512:
        return rows
    return next(tr for tr in (512, 384, 352, 256, 128, 64, 32, 16, 8) if rows % tr == 0)


def _adamw_math(npart, p_ref, w_ref, m_ref, v_ref, g_ref, d_ref, mo_ref, vo_ref):
    c1 = 1.0 - ADAM_B1 ** ADAM_STEP
    c2 = 1.0 - ADAM_B2 ** ADAM_STEP
    g = p_ref[0].astype(F32)
    for p in range(1, npart):
        g = g + p_ref[p].astype(F32)
    mn = ADAM_B1 * m_ref[...] + (1.0 - ADAM_B1) * g
    vn = ADAM_B2 * v_ref[...] + (1.0 - ADAM_B2) * (g * g)
    m_hat = mn / c1
    v_hat = vn / c2
    g_ref[...] = g
    d_ref[...] = -ADAM_LR * (m_hat / (jnp.sqrt(v_hat) + ADAM_EPS) + ADAM_WD * w_ref[...])
    mo_ref[...] = mn
    vo_ref[...] = vn


def _adamw_layer(name, parts, w, m, v, layer, bufs):
    nl, cols = w.shape[0], w.shape[-1]
    rows = int(math.prod(w.shape[1:-1]))
    npart = parts.shape[0]
    tr = _adamw_rows(rows)
    if bufs is None:
        bufs = [lax.empty((nl, rows, cols), F32) for _ in range(4)]

    def body(p_ref, w_ref, m_ref, v_ref, b0, b1, b2, b3, g_ref, d_ref, mo_ref, vo_ref):
        _adamw_math(npart, p_ref, w_ref, m_ref, v_ref, g_ref, d_ref, mo_ref, vo_ref)

    slab = pl.BlockSpec((None, tr, cols), lambda i: (layer, i, 0))
    sds = jax.ShapeDtypeStruct((nl, rows, cols), F32)
    return pl.pallas_call(
        body, name=name, grid=(rows // tr,),
        in_specs=[pl.BlockSpec((npart, tr, cols), lambda i: (0, i, 0)), slab, slab, slab] + [_HBM] * 4,
        out_specs=[slab] * 4, out_shape=[sds] * 4,
        input_output_aliases={4: 0, 5: 1, 6: 2, 7: 3},
        compiler_params=_cp(("parallel",)),
    )(parts.reshape(npart, rows, cols), w.reshape(nl, rows, cols), m.reshape(nl, rows, cols),
      v.reshape(nl, rows, cols), *bufs)


def _adamw_reduce(name, parts, w, m, v):
    shape = w.shape
    cols = shape[-1]
    rows = int(math.prod(shape[:-1])) if len(shape) > 1 else 1
    npart = parts.shape[0]
    tr = _adamw_rows(rows)

    def body(p_ref, w_ref, m_ref, v_ref, g_ref, d_ref, mo_ref, vo_ref):
        _adamw_math(npart, p_ref, w_ref, m_ref, v_ref, g_ref, d_ref, mo_ref, vo_ref)

    tile = pl.BlockSpec((tr, cols), lambda i: (i, 0))
    sds = jax.ShapeDtypeStruct((rows, cols), F32)
    outs = pl.pallas_call(
        body, name=name, grid=(rows // tr,),
        in_specs=[pl.BlockSpec((npart, tr, cols), lambda i: (0, i, 0)), tile, tile, tile],
        out_specs=[tile] * 4, out_shape=[sds] * 4,
        compiler_params=_cp(("parallel",)),
    )(parts.reshape(npart, rows, cols), w.reshape(rows, cols), m.reshape(rows, cols), v.reshape(rows, cols))
    return tuple(o.reshape(shape) for o in outs)


def _pad_ffn_in(w):
    lead = w.shape[:-1]
    w = w.reshape(lead + (2, FF_HALF))
    w = jnp.pad(w, [(0, 0)] * len(lead) + [(0, 0), (0, FF_HALF_PAD - FF_HALF)])
    return w.reshape(lead + (FF_IN_PAD,))


def kernel(x, c, rms_g1, rms_g2, w_ada, b_ada, w_in, gm_ln_g, gm_ln_b, gm_w_spatial, gm_b_spatial, pool_w, pool_scale, w_branch, w_out, w_ffn_in, w_ffn_out, final_g, loss_target, m_rms_g1, m_rms_g2, m_w_ada, m_b_ada, m_w_in, m_gm_ln_g, m_gm_ln_b, m_gm_w_spatial, m_gm_b_spatial, m_pool_w, m_pool_scale, m_w_branch, m_w_out, m_w_ffn_in, m_w_ffn_out, m_final_g, v_rms_g1, v_rms_g2, v_w_ada, v_b_ada, v_w_in, v_gm_ln_g, v_gm_ln_b, v_gm_w_spatial, v_gm_b_spatial, v_pool_w, v_pool_scale, v_w_branch, v_w_out, v_w_ffn_in, v_w_ffn_out, v_final_g):
    nb, seq, _ = x.shape
    nl = w_in.shape[0]
    t = nb * seq
    ntot = NDEV * nb
    me = _my_index()
    assert x.shape[2] == D and w_in.shape[1:] == (D, 768) and w_ffn_in.shape[1:] == (D, FF_IN_SHARD)
    assert seq % CH == 0

    w_ffn_in_p = _pad_ffn_in(w_ffn_in).astype(BF16)
    w_ffn_out_p = jnp.pad(w_ffn_out, ((0, 0), (0, FF_HALF_PAD - FF_HALF), (0, 0))).astype(BF16)
    w_in_b = w_in.astype(BF16)
    w_branch_b = w_branch.astype(BF16)
    w_out_b = w_out.astype(BF16)
    (g_in_next,) = _exchange([w_in_b[0]], "gather_w_in0", False)

    (c_all,) = _exchange([c], "gather_c", False)
    c_all = c_all.reshape(ntot, D)
    b_blk = lax.dynamic_slice_in_dim(b_ada, me * 768, 768, axis=1).reshape(nl, 1, 768)
    mod_blk = _ada_fwd(c_all, w_ada, b_blk)
    (mod_all,) = _exchange([mod_blk], "gather_mod", False)
    mod_all = jnp.transpose(mod_all, (1, 2, 0, 3)).reshape(nl, ntot, NMOD * D)
    mod = lax.dynamic_slice_in_dim(mod_all, me * nb, nb, axis=1).reshape(nl, nb, NMOD, 1, D)

    saved = []
    gathered = []
    xc = x
    for l in range(nl):
        sh1, sc1, gt1, sh2, sc2, gt2 = [mod[l, :, i] for i in range(NMOD)]
        h = _norm_mod_fwd(xc, rms_g1[l].reshape(1, D), sc1, sh1).reshape(t, D)
        proj, (g_ffn_in_w,) = _mm_colblocked("proj_fwd", h, g_in_next, BF16, [w_ffn_in_p[l]])
        proj3 = proj.reshape(nb, seq, IN_COLS)
        br_gm = _gmlp_fwd(proj, gm_ln_g[l].reshape(1, BW), gm_ln_b[l].reshape(1, BW),
                          gm_w_spatial[l], gm_b_spatial[l].T)
        sb_o, got = _sb_fwd(proj3, [w_branch_b[l], w_out_b[l]] + ([w_in_b[l + 1]] if l + 1 < nl else []))
        gw = dict(w_in=g_in_next,
                  w_branch=jnp.transpose(got[0], (1, 2, 0, 3)).reshape(NB, BW, D),
                  w_out=got[1].reshape(D, D),
                  w_ffn_in=g_ffn_in_w)
        gathered.append(gw)
        if l + 1 < nl:
            g_in_next = got[2]
        br_pool = _pool_fwd(proj3, pool_w[l], pool_scale[l].reshape(1, BW))
        brs = [br_gm, sb_o.reshape(t, BW), br_pool.reshape(t, BW)]
        merged, y0, y1, y2 = _merge_fwd(brs, gw["w_branch"], proj)
        x_mid, mo = _mm_residual("out_fwd", merged, gw["w_out"], xc.reshape(t, D), gt1, seq)
        x_mid = x_mid.reshape(nb, seq, D)
        h2 = _norm_mod_fwd(x_mid, rms_g2[l].reshape(1, D), sc2, sh2).reshape(t, D)
        fg, fu, act, got = _ffn_in_fwd(h2, gw["w_ffn_in"], [w_ffn_out_p[l]])
        gw["w_ffn_out"] = got[0].reshape(FFP, D)
        x_out, fo = _mm_residual("ffn_out_fwd", act, gw["w_ffn_out"], x_mid.reshape(t, D), gt2, seq)
        saved.append(dict(x_in=xc, h=h, proj=proj, brs=brs, sb_o=sb_o, ys=(y0, y1, y2), merged=merged,
                          mo=mo, x_mid=x_mid, h2=h2, fg=fg, fu=fu, act=act, fo=fo))
        xc = x_out.reshape(nb, seq, D)

    dx, loss_part, dfinal_part, dfo, dgt2 = _loss_head(xc, loss_target, final_g.reshape(1, D),
                                                       saved[-1]["fo"].reshape(nb, seq, D), mod[nl - 1, :, 5])
    loss = lax.psum(jnp.sum(loss_part[:, 0, 0]), ("x", "y", "c"))

    big_names = ("w_in", "w_branch", "w_out", "w_ffn_in", "w_ffn_out")
    bufs = {name: None for name in big_names}
    w_ffn_in_t, m_w_ffn_in_t, v_w_ffn_in_t = [jnp.swapaxes(a, 1, 2) for a in (w_ffn_in, m_w_ffn_in, v_w_ffn_in)]
    small_parts = {k: [None] * nl for k in ("rms_g1", "rms_g2", "gm_ln_g", "gm_ln_b", "gm_w_spatial",
                                            "gm_b_spatial", "pool_w", "pool_scale")}
    dmod = [None] * nl
    for l in reversed(range(nl)):
        gw = gathered[l]
        sv = saved[l]
        sh1, sc1, gt1, sh2, sc2, gt2 = [mod[l, :, i] for i in range(NMOD)]
        dfo = dfo.reshape(t, D)
        g_ffn_out = _mm_tn("ffn_out_wgrad", sv["act"], dfo)
        dfg, dfu = _ffn_out_dgrad(dfo, gw["w_ffn_out"], sv["fg"], sv["fu"])
        dh2 = _ffn_in_dgrad(dfg, dfu, gw["w_ffn_in"])
        g_ffn_in = _ffn_in_wgrad(sv["h2"], dfg, dfu)
        dx_mid, dsh2, dsc2, dg2, dmo, dgt1 = _norm_mod_bwd(
            sv["x_mid"], dh2.reshape(nb, seq, D), dx, rms_g2[l].reshape(1, D), sc2,
            gate=(sv["mo"].reshape(nb, seq, D), gt1))
        dmo = dmo.reshape(t, D)
        dmerged = _mm_nt("out_dgrad", dmo, gw["w_out"], BF16)
        g_out = _mm_tn("out_wgrad", sv["merged"], dmo)
        dls_dys = _merge_bwd(dmerged, sv["ys"], sv["proj"])
        dls, dys = dls_dys[:3], dls_dys[3:]
        dbrs, g_br = [], []
        for n in range(NB):
            dbrs.append(_mm_nt("branch_dgrad", dys[n], gw["w_branch"], BF16, w_lead=n))
            g_br.append(_mm_tn("branch_wgrad", sv["brs"][n], dys[n]))
        proj3 = sv["proj"].reshape(nb, seq, IN_COLS)
        d_gm, g_ws, g_bs, g_lg, g_lb = _gmlp_bwd(sv["proj"], dbrs[0], gm_ln_g[l].reshape(1, BW),
                                                 gm_ln_b[l].reshape(1, BW), gm_w_spatial[l], gm_b_spatial[l].T)
        g_br_dev = jnp.transpose(jnp.stack(g_br).reshape(NB, BW, NDEV, D // NDEV), (2, 0, 1, 3))
        carried = [g_br_dev, g_out.reshape(NDEV, D // NDEV, D), g_ffn_in, g_ffn_out.reshape(NDEV, FF_HALF_PAD, D)]
        d_sb, recv = _sb_bwd(proj3, dbrs[1].reshape(nb, seq, BW), sv["sb_o"], carried)
        bufs["w_branch"] = _adamw_layer("adamw_w_branch", recv[0], w_branch, m_w_branch, v_w_branch, l,
                                        bufs["w_branch"])
        bufs["w_out"] = _adamw_layer("adamw_w_out", recv[1], w_out, m_w_out, v_w_out, l, bufs["w_out"])
        r_fi = recv[2].reshape(NDEV, 2, FF_HALF_PAD, D)[:, :, :FF_HALF].reshape(NDEV, FF_IN_SHARD, D)
        bufs["w_ffn_in"] = _adamw_layer("adamw_w_ffn_in", r_fi, w_ffn_in_t, m_w_ffn_in_t, v_w_ffn_in_t, l,
                                        bufs["w_ffn_in"])
        bufs["w_ffn_out"] = _adamw_layer("adamw_w_ffn_out", recv[3][:, :FF_HALF], w_ffn_out, m_w_ffn_out,
                                         v_w_ffn_out, l, bufs["w_ffn_out"])
        d_pool, g_pw, g_ps = _pool_bwd(proj3, dbrs[2].reshape(nb, seq, BW), pool_w[l], pool_scale[l].reshape(1, BW))
        dproj = jnp.concatenate([d_gm] + [a.reshape(t, BW) for a in d_sb] + [d_pool.reshape(t, BW)] + list(dls),
                                axis=1)
        g_in = _mm_colblocked_tn("proj_wgrad", sv["h"], dproj)
        dh, (r_in,) = _mm_colblocked_nt("proj_dgrad", dproj, gw["w_in"], F32, [g_in])
        bufs["w_in"] = _adamw_layer("adamw_w_in", r_in, w_in, m_w_in, v_w_in, l, bufs["w_in"])
        dmod_tail = [dgt1, dsh2, dsc2, dgt2]
        if l > 0:
            dx, dsh1, dsc1, dg1, dfo, dgt2 = _norm_mod_bwd(
                sv["x_in"], dh.reshape(nb, seq, D), dx_mid, rms_g1[l].reshape(1, D), sc1,
                gate=(saved[l - 1]["fo"].reshape(nb, seq, D), mod[l - 1, :, 5]))
        else:
            dx, dsh1, dsc1, dg1 = _norm_mod_bwd(sv["x_in"], dh.reshape(nb, seq, D), dx_mid,
                                                rms_g1[l].reshape(1, D), sc1)

        dmod[l] = jnp.concatenate([dsh1, dsc1] + dmod_tail, axis=-1)
        small_parts["rms_g1"][l] = jnp.sum(dg1, axis=0)
        small_parts["rms_g2"][l] = jnp.sum(dg2, axis=0)
        small_parts["gm_ln_g"][l] = g_lg
        small_parts["gm_ln_b"][l] = g_lb
        small_parts["gm_w_spatial"][l] = g_ws
        small_parts["gm_b_spatial"][l] = g_bs[:, :, 0]
        small_parts["pool_w"][l] = g_pw
        small_parts["pool_scale"][l] = g_ps

    dmod_mine = jnp.stack(dmod).reshape(nl, nb, NMOD * D)
    names = list(small_parts)
    stacked = [jnp.stack(small_parts[k]).astype(BF16 if k in ("gm_w_spatial", "pool_w") else F32) for k in names]
    gathered_small = _exchange(stacked + [dfinal_part, dmod_mine], "gather_small", False)
    dmod_all = jnp.transpose(gathered_small[-1], (1, 0, 2, 3)).reshape(nl, ntot, NMOD * D)
    dfinal_all = gathered_small[-2].reshape(ntot, D)

    results = {}
    weights = dict(rms_g1=(rms_g1, m_rms_g1, v_rms_g1), rms_g2=(rms_g2, m_rms_g2, v_rms_g2),
                   gm_ln_g=(gm_ln_g, m_gm_ln_g, v_gm_ln_g), gm_ln_b=(gm_ln_b, m_gm_ln_b, v_gm_ln_b),
                   gm_w_spatial=(gm_w_spatial, m_gm_w_spatial, v_gm_w_spatial),
                   gm_b_spatial=(gm_b_spatial, m_gm_b_spatial, v_gm_b_spatial),
                   pool_w=(pool_w, m_pool_w, v_pool_w), pool_scale=(pool_scale, m_pool_scale, v_pool_scale))
    for k, parts in zip(names, gathered_small[:len(names)]):
        w, m, v = weights[k]
        results[k] = _adamw_reduce("adamw_" + k, parts.reshape((NDEV,) + w.shape), w, m, v)
    results["final_g"] = _adamw_reduce("adamw_final_g", dfinal_all, final_g, m_final_g, v_final_g)
    results["b_ada"] = _adamw_reduce("adamw_b_ada", jnp.transpose(dmod_all, (1, 0, 2)), b_ada, m_b_ada, v_b_ada)
    dmod_blk = lax.dynamic_slice_in_dim(dmod_all, me * 768, 768, axis=2)
    g_w_ada = _ada_bwd(c_all, dmod_blk)
    results["w_ada"] = _adamw_reduce("adamw_w_ada", g_w_ada[None], w_ada, m_w_ada, v_w_ada)
    stacked_w = dict(w_in=w_in, w_branch=w_branch, w_out=w_out, w_ffn_in=w_ffn_in_t, w_ffn_out=w_ffn_out)
    for name in big_names:
        results[name] = tuple(b.reshape(stacked_w[name].shape) for b in bufs[name])
    results["w_ffn_in"] = tuple(jnp.swapaxes(b, 1, 2) for b in results["w_ffn_in"])

    order = ["rms_g1", "rms_g2", "w_ada", "b_ada", "w_in", "gm_ln_g", "gm_ln_b", "gm_w_spatial", "gm_b_spatial",
             "pool_w", "pool_scale", "w_branch", "w_out", "w_ffn_in", "w_ffn_out", "final_g"]
    out = [loss, dx]
    for i in range(4):
        out.extend(results[k][i] for k in order)
    return tuple(out)
```

```python
import functools
import math

import jax
import jax.numpy as jnp
from jax import lax
from jax.experimental import pallas as pl
from jax.experimental.pallas import tpu as pltpu

F32 = jnp.float32
BF16 = jnp.bfloat16
MESH = pl.DeviceIdType.MESH

D = 1024
BW = 512
NB = 3
CH = 128
NG = 4
HD = 64
POOL_WINDOWS = (2, 4, 8, 16)
DFF = 2816
NMOD = 6
EPS = 1e-6
IN_COLS = 6 * D
NDEV = 8
FF_IN_SHARD = 2 * DFF // NDEV
FF_HALF = FF_IN_SHARD // 2
FF_HALF_PAD = 384
FF_IN_PAD = 2 * FF_HALF_PAD
FFP = NDEV // 2 * FF_IN_PAD

ADAM_LR = 0.001
ADAM_B1 = 0.9
ADAM_B2 = 0.999
ADAM_EPS = 1e-08
ADAM_WD = 0.01
ADAM_STEP = 10

VMEM_LIMIT = 48 * 1024 * 1024
BIG_ROWS = 2048
NN = (((1,), (0,)), ((), ()))
NT = (((1,), (1,)), ((), ()))
TN = (((0,), (0,)), ((), ()))


def _cp(sem=None):
    return pltpu.CompilerParams(dimension_semantics=sem, vmem_limit_bytes=VMEM_LIMIT)


def _dot(a, b, dims=NN):
    return lax.dot_general(a, b, dims, preferred_element_type=F32)


def _my_index():
    return 4 * lax.axis_index("x") + 2 * lax.axis_index("y") + lax.axis_index("c")


def _peer(k):
    x, y, c = lax.axis_index("x"), lax.axis_index("y"), lax.axis_index("c")
    px = 1 - x if k & 4 else x
    py = 1 - y if k & 2 else y
    pc = 1 - c if k & 1 else c
    return (px, py, pc), 4 * px + 2 * py + pc


def _exchange(xs, name, all_to_all):
    n = len(xs)

    def body(*refs):
        _exchange_start(refs[:n], refs[n:2 * n], refs[2 * n:], all_to_all)
        _exchange_relay(refs[:n], refs[n:2 * n], refs[2 * n:], all_to_all)
        _exchange_finish(refs[:n], refs[n:2 * n], refs[2 * n:], all_to_all)

    return pl.pallas_call(
        body, name=name, out_shape=_exchange_out_shape(xs, all_to_all),
        in_specs=[_HBM] * n, out_specs=[_HBM] * n, scratch_shapes=_exchange_sems(n),
    )(*xs)


_HBM = pl.BlockSpec(memory_space=pl.ANY)


def _exchange_out_shape(xs, all_to_all):
    if all_to_all:
        return [jax.ShapeDtypeStruct(x.shape, x.dtype) for x in xs]
    return [jax.ShapeDtypeStruct((NDEV,) + x.shape, x.dtype) for x in xs]


def _exchange_sems(n):
    return [pltpu.SemaphoreType.DMA((n * 7,)), pltpu.SemaphoreType.DMA((n * 7,)), pltpu.SemaphoreType.DMA((n,))]


def _all_to_all_copies(ins, outs, sems):
    send_sems, recv_sems, local_sems = sems
    me = _my_index()
    local, sends, recvs = [], [], []
    for a in range(len(ins)):
        local.append(pltpu.make_async_copy(ins[a].at[me], outs[a].at[me], local_sems.at[a]))
    for k in range(1, NDEV):
        dev, idx = _peer(k)
        for a in range(len(ins)):
            sem = dict(send_sem=send_sems.at[a * 7 + k - 1], recv_sem=recv_sems.at[a * 7 + k - 1],
                       device_id=dev, device_id_type=MESH)
            sends.append(pltpu.make_async_remote_copy(src_ref=ins[a].at[idx], dst_ref=outs[a].at[me], **sem))
            recvs.append(pltpu.make_async_remote_copy(src_ref=ins[a].at[idx], dst_ref=outs[a].at[idx], **sem))
    return local, sends, recvs


def _gather_copies(ins, outs, sems):
    send_sems, recv_sems, local_sems = sems
    x, y, c = lax.axis_index("x"), lax.axis_index("y"), lax.axis_index("c")
    me, other = 4 * x + 2 * y + c, 4 * x + 2 * y + (1 - c)
    other_dev = (x, y, 1 - c)
    chips = [(1 - x, y), (x, 1 - y), (1 - x, 1 - y)]
    local, own, relay, from_other = [], [], [], []
    for a in range(len(ins)):
        def copy(k, src, block, dev, a=a):
            return pltpu.make_async_remote_copy(
                src_ref=src, dst_ref=outs[a].at[block], send_sem=send_sems.at[a * 7 + k],
                recv_sem=recv_sems.at[a * 7 + k], device_id=dev, device_id_type=MESH)

        local.append(pltpu.make_async_copy(ins[a], outs[a].at[me], local_sems.at[a]))
        own.append(copy(0, ins[a], me, other_dev))
        from_other.append(copy(0, ins[a], other, other_dev))
        for j, (px, py) in enumerate(chips):
            far = 4 * px + 2 * py + c
            own.append(copy(1 + j, ins[a], me, (px, py, c)))
            relay.append((copy(1 + j, ins[a], far, (px, py, c)), copy(4 + j, outs[a].at[far], far, other_dev)))
            from_other.append(copy(4 + j, ins[a], 4 * px + 2 * py + (1 - c), other_dev))
    return local, own, relay, from_other


def _exchange_start(ins, outs, sems, all_to_all):
    local, sends = (_all_to_all_copies if all_to_all else _gather_copies)(ins, outs, sems)[:2]
    for cp in local + sends:
        cp.start()


def _exchange_relay(ins, outs, sems, all_to_all):
    if not all_to_all:
        for arrival, passing_on in _gather_copies(ins, outs, sems)[2]:
            arrival.wait_recv()
            passing_on.start()


def _exchange_finish(ins, outs, sems, all_to_all):
    if all_to_all:
        local, sends, recvs = _all_to_all_copies(ins, outs, sems)
    else:
        local, own, relay, recvs = _gather_copies(ins, outs, sems)
        sends = own + [passing_on for _, passing_on in relay]
    for cp in sends:
        cp.wait_send()
    for cp in recvs:
        cp.wait_recv()
    for cp in local:
        cp.wait()


def _host_exchange(body, n_in, n_out, grid, xs, all_to_all):
    n = len(xs)
    if n == 0:
        return body, [], [], [], []
    steps = math.prod(grid)
    half = steps // 2 if steps >= 3 else steps - 1

    def hosted(*refs):
        ins, ex_ins = refs[:n_in], refs[n_in:n_in + n]
        outs, ex_outs = refs[n_in + n:n_in + n + n_out], refs[n_in + n + n_out:n_in + 2 * n + n_out]
        scratch = refs[n_in + 2 * n + n_out:]
        own, sems = scratch[:len(scratch) - 3], scratch[len(scratch) - 3:]
        step = 0
        for a in range(len(grid)):
            step = step * grid[a] + pl.program_id(a)

        @pl.when(step == 0)
        def _():
            _exchange_start(ex_ins, ex_outs, sems, all_to_all)

        body(*ins, *outs, *own)

        @pl.when(step == half)
        def _():
            _exchange_relay(ex_ins, ex_outs, sems, all_to_all)

        @pl.when(step == steps - 1)
        def _():
            _exchange_finish(ex_ins, ex_outs, sems, all_to_all)

    return hosted, [_HBM] * n, [_HBM] * n, _exchange_out_shape(xs, all_to_all), _exchange_sems(n)


def _mm(name, a, b, grid, a_spec, b_spec, o_spec, out_sds, dims, acc_shape, carried=(), all_to_all=True):
    nk = grid[2]

    if nk == 1:
        def body(a_ref, b_ref, o_ref):
            o_ref[...] = _dot(a_ref[...].astype(BF16), b_ref[...].astype(BF16), dims).astype(o_ref.dtype)
        scratch = []
    else:
        def body(a_ref, b_ref, o_ref, acc_ref):
            k = pl.program_id(2)

            @pl.when(k == 0)
            def _():
                acc_ref[...] = jnp.zeros_like(acc_ref)

            acc_ref[...] += _dot(a_ref[...].astype(BF16), b_ref[...].astype(BF16), dims)

            @pl.when(k == nk - 1)
            def _():
                o_ref[...] = acc_ref[...].astype(o_ref.dtype)
        scratch = [pltpu.VMEM(acc_shape, F32)]

    body, ex_in, ex_out, ex_shape, ex_sems = _host_exchange(body, 2, 1, grid, carried, all_to_all)
    outs = pl.pallas_call(
        body, name=name, grid=grid, in_specs=[a_spec, b_spec] + ex_in, out_specs=[o_spec] + ex_out,
        out_shape=[out_sds] + ex_shape,
        scratch_shapes=scratch + ex_sems,
        compiler_params=_cp(("arbitrary",) * 3 if carried else ("parallel", "parallel", "arbitrary")),
    )(a, b, *carried)
    return (outs[0], outs[1:]) if carried else outs[0]


def _row_tile(t, want):
    tm = min(t, want)
    assert t % tm == 0
    return tm


def _mm_colblocked(name, a, wg, out_dtype, gather=()):
    t = a.shape[0]
    tm = _row_tile(t, BIG_ROWS)
    return _mm(name, a, wg, (t // tm, NDEV, 1),
               pl.BlockSpec((tm, D), lambda i, j, k: (i, 0)),
               pl.BlockSpec((None, D, 768), lambda i, j, k: (j, 0, 0)),
               pl.BlockSpec((tm, 768), lambda i, j, k: (i, j)),
               jax.ShapeDtypeStruct((t, NDEV * 768), out_dtype), NN, (tm, 768), gather, False)


def _mm_colblocked_nt(name, g, wg, out_dtype, scatter=()):
    t = g.shape[0]
    tm = _row_tile(t, BIG_ROWS)
    return _mm(name, g, wg, (t // tm, 1, NDEV),
               pl.BlockSpec((tm, 768), lambda i, j, k: (i, k)),
               pl.BlockSpec((None, D, 768), lambda i, j, k: (k, 0, 0)),
               pl.BlockSpec((tm, D), lambda i, j, k: (i, 0)),
               jax.ShapeDtypeStruct((t, D), out_dtype), NT, (tm, D), scatter)


_HALF = NDEV // 2


def _ffn_in_fwd(h2, wg, gather=()):
    t = h2.shape[0]
    tm = _row_tile(t, 1024)

    def body(a_ref, wg_ref, wu_ref, g_ref, u_ref, act_ref):
        a = a_ref[...]
        g = _dot(a, wg_ref[...])
        u = _dot(a, wu_ref[...])
        g_ref[...] = g.astype(BF16)
        u_ref[...] = u.astype(BF16)
        act_ref[...] = (g * jax.nn.sigmoid(g) * u).astype(BF16)

    tile = pl.BlockSpec((tm, 768), lambda i, j: (i, j))
    grid = (t // tm, _HALF)
    body, ex_in, ex_out, ex_shape, ex_sems = _host_exchange(body, 3, 3, grid, gather, False)
    outs = pl.pallas_call(
        body, name="ffn_in_fwd", grid=grid,
        in_specs=[pl.BlockSpec((tm, D), lambda i, j: (i, 0)),
                  pl.BlockSpec((None, D, 768), lambda i, j: (j, 0, 0)),
                  pl.BlockSpec((None, D, 768), lambda i, j: (j + _HALF, 0, 0))] + ex_in,
        out_specs=[tile] * 3 + ex_out, out_shape=[jax.ShapeDtypeStruct((t, FFP), BF16)] * 3 + ex_shape,
        scratch_shapes=ex_sems,
        compiler_params=_cp(("arbitrary", "arbitrary") if gather else ("parallel", "parallel")),
    )(h2, wg, wg, *gather)
    return outs[0], outs[1], outs[2], outs[3:]


def _ffn_out_dgrad(dfo, w, fg, fu):
    t = dfo.shape[0]
    tm = _row_tile(t, 1024)

    def body(a_ref, w_ref, g_ref, u_ref, dg_ref, du_ref):
        d = _dot(a_ref[...], w_ref[...], NT)
        g = g_ref[...].astype(F32)
        s = jax.nn.sigmoid(g)
        gs = g * s
        dg_ref[...] = (d * u_ref[...].astype(F32) * (s + gs * (1.0 - s))).astype(BF16)
        du_ref[...] = (d * gs).astype(BF16)

    tile = pl.BlockSpec((tm, 768), lambda i, j: (i, j))
    return pl.pallas_call(
        body, name="ffn_out_dgrad", grid=(t // tm, _HALF),
        in_specs=[pl.BlockSpec((tm, D), lambda i, j: (i, 0)),
                  pl.BlockSpec((768, D), lambda i, j: (j, 0)), tile, tile],
        out_specs=[tile] * 2, out_shape=[jax.ShapeDtypeStruct((t, FFP), BF16)] * 2,
        compiler_params=_cp(("parallel", "parallel")),
    )(dfo, w, fg, fu)


def _ffn_in_dgrad(dg, du, wg):
    t = dg.shape[0]
    tm = _row_tile(t, BIG_ROWS)

    def body(g_ref, u_ref, w_ref, o_ref, acc_ref):
        k = pl.program_id(1)

        @pl.when(k == 0)
        def _():
            acc_ref[...] = jnp.zeros_like(acc_ref)

        @pl.when(k < _HALF)
        def _():
            acc_ref[...] += _dot(g_ref[...], w_ref[...], NT)

        @pl.when(k >= _HALF)
        def _():
            acc_ref[...] += _dot(u_ref[...], w_ref[...], NT)

        @pl.when(k == NDEV - 1)
        def _():
            o_ref[...] = acc_ref[...]

    return pl.pallas_call(
        body, name="ffn_in_dgrad", grid=(t // tm, NDEV),
        in_specs=[pl.BlockSpec((tm, 768), lambda i, k: (i, jnp.minimum(k, _HALF - 1))),
                  pl.BlockSpec((tm, 768), lambda i, k: (i, jnp.maximum(k - _HALF, 0))),
                  pl.BlockSpec((None, D, 768), lambda i, k: (k, 0, 0))],
        out_specs=pl.BlockSpec((tm, D), lambda i, k: (i, 0)),
        out_shape=jax.ShapeDtypeStruct((t, D), F32),
        scratch_shapes=[pltpu.VMEM((tm, D), F32)],
        compiler_params=_cp(("parallel", "arbitrary")),
    )(dg, du, wg)


def _ffn_in_wgrad(h2, dg, du):
    t = h2.shape[0]
    tk = _row_tile(t, BIG_ROWS)
    nk = t // tk

    def body(a_ref, g_ref, u_ref, o_ref, acc_ref):
        j, k = pl.program_id(0), pl.program_id(1)

        @pl.when(k == 0)
        def _():
            acc_ref[...] = jnp.zeros_like(acc_ref)

        @pl.when(j < _HALF)
        def _():
            acc_ref[...] += _dot(g_ref[...], a_ref[...], TN)

        @pl.when(j >= _HALF)
        def _():
            acc_ref[...] += _dot(u_ref[...], a_ref[...], TN)

        @pl.when(k == nk - 1)
        def _():
            o_ref[...] = acc_ref[...].astype(BF16)

    return pl.pallas_call(
        body, name="ffn_in_wgrad", grid=(NDEV, nk),
        in_specs=[pl.BlockSpec((tk, D), lambda j, k: (k, 0)),
                  pl.BlockSpec((tk, 768), lambda j, k: (jnp.where(j < _HALF, k, 0), jnp.minimum(j, _HALF - 1))),
                  pl.BlockSpec((tk, 768), lambda j, k: (jnp.where(j < _HALF, 0, k), jnp.maximum(j - _HALF, 0)))],
        out_specs=pl.BlockSpec((None, 768, D), lambda j, k: (j, 0, 0)),
        out_shape=jax.ShapeDtypeStruct((NDEV, 768, D), BF16),
        scratch_shapes=[pltpu.VMEM((768, D), F32)],
        compiler_params=_cp(("parallel", "arbitrary")),
    )(h2, dg, du)


def _mm_colblocked_tn(name, a, g):
    t = a.shape[0]
    tk = _row_tile(t, BIG_ROWS)
    return _mm(name, a, g, (1, NDEV, t // tk),
               pl.BlockSpec((tk, D), lambda i, j, k: (k, 0)),
               pl.BlockSpec((tk, 768), lambda i, j, k: (k, j)),
               pl.BlockSpec((None, D, 768), lambda i, j, k: (j, 0, 0)),
               jax.ShapeDtypeStruct((NDEV, D, 768), BF16), TN, (D, 768))


def _mm_nt(name, a, w, out_dtype, a_col=0, w_lead=None):
    t = a.shape[0]
    if w_lead is None:
        kdim, n = w.shape
        b_spec = pl.BlockSpec((min(kdim, 1024), n), lambda i, j, k: (j, 0))
    else:
        _, kdim, n = w.shape
        b_spec = pl.BlockSpec((None, min(kdim, 1024), n), lambda i, j, k: (w_lead, j, 0))
    tn = min(kdim, 1024)
    tm = _row_tile(t, BIG_ROWS)
    return _mm(name, a, w, (t // tm, kdim // tn, 1),
               pl.BlockSpec((tm, n), lambda i, j, k: (i, a_col)),
               b_spec,
               pl.BlockSpec((tm, tn), lambda i, j, k: (i, j)),
               jax.ShapeDtypeStruct((t, kdim), out_dtype), NT, (tm, tn))


def _mm_tn(name, a, g, out_dtype=BF16):
    t, kdim = a.shape
    n = g.shape[1]
    tk = _row_tile(t, BIG_ROWS)
    tm = min(kdim, 1024)
    tn = min(n, 1024)
    return _mm(name, a, g, (kdim // tm, n // tn, t // tk),
               pl.BlockSpec((tk, tm), lambda i, j, k: (k, i)),
               pl.BlockSpec((tk, tn), lambda i, j, k: (k, j)),
               pl.BlockSpec((tm, tn), lambda i, j, k: (i, j)),
               jax.ShapeDtypeStruct((kdim, n), out_dtype), TN, (tm, tn))


def _mm_residual(name, a, w, x, gt, seq):
    t, kdim = a.shape
    tm = _row_tile(seq, 1024)
    tn = D
    tk = min(kdim, 1024)
    nk = kdim // tk
    per = seq // tm

    def body(a_ref, w_ref, x_ref, gt_ref, xo_ref, y_ref, acc_ref):
        k = pl.program_id(2)

        @pl.when(k == 0)
        def _():
            acc_ref[...] = jnp.zeros_like(acc_ref)

        acc_ref[...] += _dot(a_ref[...], w_ref[...])

        @pl.when(k == nk - 1)
        def _():
            y = acc_ref[...]
            xo_ref[...] = x_ref[...] + gt_ref[0] * y
            y_ref[...] = y.astype(BF16)

    return pl.pallas_call(
        body, name=name, grid=(t // tm, D // tn, nk),
        in_specs=[pl.BlockSpec((tm, tk), lambda i, j, k: (i, k)),
                  pl.BlockSpec((tk, tn), lambda i, j, k: (k, j)),
                  pl.BlockSpec((tm, tn), lambda i, j, k: (i, j)),
                  pl.BlockSpec((1, 1, tn), lambda i, j, k: (i // per, 0, j))],
        out_specs=[pl.BlockSpec((tm, tn), lambda i, j, k: (i, j)),
                   pl.BlockSpec((tm, tn), lambda i, j, k: (i, j))],
        out_shape=[jax.ShapeDtypeStruct((t, D), F32), jax.ShapeDtypeStruct((t, D), BF16)],
        scratch_shapes=[pltpu.VMEM((tm, tn), F32)],
        compiler_params=_cp(("parallel", "parallel", "arbitrary")),
    )(a, w, x, gt)


def _ada_fwd(c_all, w_ada, b_blk):
    nl = w_ada.shape[0]
    nb = c_all.shape[0]

    def body(c_ref, w_ref, b_ref, o_ref):
        c = c_ref[...]
        ca = (c * jax.nn.sigmoid(c)).astype(BF16)
        o_ref[...] = _dot(ca, w_ref[...].astype(BF16)) + b_ref[...]

    return pl.pallas_call(
        body, name="ada_fwd", grid=(nl,),
        in_specs=[pl.BlockSpec((nb, D), lambda l: (0, 0)),
                  pl.BlockSpec((None, D, 768), lambda l: (l, 0, 0)),
                  pl.BlockSpec((None, 1, 768), lambda l: (l, 0, 0))],
        out_specs=pl.BlockSpec((None, nb, 768), lambda l: (l, 0, 0)),
        out_shape=jax.ShapeDtypeStruct((nl, nb, 768), F32),
        compiler_params=_cp(("parallel",)),
    )(c_all, w_ada, b_blk)


def _ada_bwd(c_all, dmod_blk):
    nl = dmod_blk.shape[0]
    nb = c_all.shape[0]

    def body(c_ref, d_ref, o_ref):
        c = c_ref[...]
        ca = (c * jax.nn.sigmoid(c)).astype(BF16)
        o_ref[...] = _dot(ca, d_ref[...].astype(BF16), TN)

    return pl.pallas_call(
        body, name="ada_bwd", grid=(nl,),
        in_specs=[pl.BlockSpec((nb, D), lambda l: (0, 0)),
                  pl.BlockSpec((None, nb, 768), lambda l: (l, 0, 0))],
        out_specs=pl.BlockSpec((None, D, 768), lambda l: (l, 0, 0)),
        out_shape=jax.ShapeDtypeStruct((nl, D, 768), F32),
        compiler_params=_cp(("parallel",)),
    )(c_all, dmod_blk)


def _seq_tile(seq):
    return _row_tile(seq, 512)


def _norm_mod_fwd(x, g, sc, sh):
    nb, seq, _ = x.shape
    ts = _seq_tile(seq)

    def body(x_ref, g_ref, sc_ref, sh_ref, h_ref):
        xv = x_ref[0]
        r = lax.rsqrt(jnp.mean(xv * xv, axis=-1, keepdims=True) + EPS)
        h_ref[0] = ((xv * r) * g_ref[...] * (1.0 + sc_ref[0]) + sh_ref[0]).astype(BF16)

    return pl.pallas_call(
        body, name="norm_mod_fwd", grid=(nb, seq // ts),
        in_specs=[pl.BlockSpec((1, ts, D), lambda b, s: (b, s, 0)),
                  pl.BlockSpec((1, D), lambda b, s: (0, 0)),
                  pl.BlockSpec((1, 1, D), lambda b, s: (b, 0, 0)),
                  pl.BlockSpec((1, 1, D), lambda b, s: (b, 0, 0))],
        out_specs=pl.BlockSpec((1, ts, D), lambda b, s: (b, s, 0)),
        out_shape=jax.ShapeDtypeStruct((nb, seq, D), BF16),
        compiler_params=_cp(("parallel", "parallel")),
    )(x, g, sc, sh)


def _gate_bwd_tile(d, y_ref, gt_ref, dy_ref, dgt_ref):
    @pl.when(pl.program_id(1) == 0)
    def _():
        dgt_ref[...] = jnp.zeros_like(dgt_ref)

    dy_ref[0] = (gt_ref[0] * d).astype(BF16)
    dgt_ref[0] += jnp.sum(d * y_ref[0].astype(F32), axis=0, keepdims=True)


def _norm_mod_bwd(x, dh, dres, g, sc, gate=None):
    nb, seq, _ = x.shape
    ts = _seq_tile(seq)

    def body(x_ref, dh_ref, dres_ref, g_ref, sc_ref, *rest):
        if gate is None:
            dx_ref, dsh_ref, dsc_ref, dg_ref = rest
        else:
            y_ref, gt_ref, dx_ref, dsh_ref, dsc_ref, dg_ref, dy_ref, dgt_ref = rest

        @pl.when(pl.program_id(1) == 0)
        def _():
            dsh_ref[...] = jnp.zeros_like(dsh_ref)
            dsc_ref[...] = jnp.zeros_like(dsc_ref)
            dg_ref[...] = jnp.zeros_like(dg_ref)

        xv = x_ref[0]
        dh = dh_ref[0]
        gv = g_ref[...]
        onesc = 1.0 + sc_ref[0]
        r = lax.rsqrt(jnp.mean(xv * xv, axis=-1, keepdims=True) + EPS)
        xh = xv * r
        dsh_ref[0] += jnp.sum(dh, axis=0, keepdims=True)
        dsc_ref[0] += jnp.sum(dh * (xh * gv), axis=0, keepdims=True)
        dg_ref[0] += jnp.sum(dh * onesc * xh, axis=0, keepdims=True)
        dxh = dh * (gv * onesc)
        dx = r * (dxh - xh * jnp.mean(dxh * xh, axis=-1, keepdims=True))
        dx_total = dres_ref[0] + dx
        dx_ref[0] = dx_total
        if gate is not None:
            _gate_bwd_tile(dx_total, y_ref, gt_ref, dy_ref, dgt_ref)

    vec = jax.ShapeDtypeStruct((nb, 1, D), F32)
    vspec = pl.BlockSpec((1, 1, D), lambda b, s: (b, 0, 0))
    tile = pl.BlockSpec((1, ts, D), lambda b, s: (b, s, 0))
    gated = gate is not None
    return pl.pallas_call(
        body, name="norm_mod_bwd", grid=(nb, seq // ts),
        in_specs=[tile, tile, tile, pl.BlockSpec((1, D), lambda b, s: (0, 0)), vspec] + [tile, vspec] * gated,
        out_specs=[tile, vspec, vspec, vspec] + [tile, vspec] * gated,
        out_shape=[jax.ShapeDtypeStruct((nb, seq, D), F32), vec, vec, vec]
        + [jax.ShapeDtypeStruct((nb, seq, D), BF16), vec] * gated,
        compiler_params=_cp(("parallel", "arbitrary")),
    )(x, dh, dres, g, sc, *(gate or ()))


def _loss_head(x, tgt, g, y, gt):
    nb, seq, _ = x.shape
    ts = _seq_tile(seq)

    def body(x_ref, t_ref, g_ref, y_ref, gt_ref, dx_ref, loss_ref, dg_ref, dy_ref, dgt_ref):
        @pl.when(pl.program_id(1) == 0)
        def _():
            loss_ref[...] = jnp.zeros_like(loss_ref)
            dg_ref[...] = jnp.zeros_like(dg_ref)

        xv = x_ref[0]
        gv = g_ref[...]
        r = lax.rsqrt(jnp.mean(xv * xv, axis=-1, keepdims=True) + EPS)
        xh = xv * r
        err = xh * gv - t_ref[0]
        per_tok = jnp.mean(err * err, axis=-1, keepdims=True)
        loss_ref[0] += 0.5 * jnp.sum(per_tok, axis=0, keepdims=True)
        dy = err * (1.0 / D)
        dg_ref[0] += jnp.sum(dy * xh, axis=0, keepdims=True)
        dxh = dy * gv
        dx = r * (dxh - xh * jnp.mean(dxh * xh, axis=-1, keepdims=True))
        dx_ref[0] = dx
        _gate_bwd_tile(dx, y_ref, gt_ref, dy_ref, dgt_ref)

    tile = pl.BlockSpec((1, ts, D), lambda b, s: (b, s, 0))
    vspec = pl.BlockSpec((1, 1, D), lambda b, s: (b, 0, 0))
    vec = jax.ShapeDtypeStruct((nb, 1, D), F32)
    return pl.pallas_call(
        body, name="loss_head", grid=(nb, seq // ts),
        in_specs=[tile, tile, pl.BlockSpec((1, D), lambda b, s: (0, 0)), tile, vspec],
        out_specs=[tile, pl.BlockSpec((1, 1, 128), lambda b, s: (b, 0, 0)), vspec, tile, vspec],
        out_shape=[jax.ShapeDtypeStruct((nb, seq, D), F32), jax.ShapeDtypeStruct((nb, 1, 128), F32), vec,
                   jax.ShapeDtypeStruct((nb, seq, D), BF16), vec],
        compiler_params=_cp(("parallel", "arbitrary")),
    )(x, tgt, g, y, gt)


_GELU_C = math.sqrt(2.0 / math.pi)


def _gelu(x):
    return 0.5 * x * (1.0 + jnp.tanh(_GELU_C * (x + 0.044715 * (x * x * x))))


def _gelu_and_grad(x):
    t = jnp.tanh(_GELU_C * (x + 0.044715 * (x * x * x)))
    y = 0.5 * x * (1.0 + t)
    dy = 0.5 * (1.0 + t) + 0.5 * x * (1.0 - t * t) * (_GELU_C * (1.0 + 3.0 * 0.044715 * (x * x)))
    return y, dy


def _tril_mask():
    row = lax.broadcasted_iota(jnp.int32, (CH, CH), 0)
    col = lax.broadcasted_iota(jnp.int32, (CH, CH), 1)
    return row >= col


def _gmlp_fwd(proj, ln_g, ln_b, ws, bst):
    t = proj.shape[0]
    tm = _row_tile(t, 512)

    def body(u_ref, v_ref, lg_ref, lb_ref, ws_ref, bst_ref, o_ref):
        tril = _tril_mask()
        wm = [jnp.where(tril, ws_ref[g], 0.0).astype(BF16) for g in range(NG)]
        for ch in range(tm // CH):
            rows = slice(ch * CH, (ch + 1) * CH)
            u = _gelu(u_ref[rows, :].astype(F32))
            v = _gelu(v_ref[rows, :].astype(F32))
            mu = jnp.mean(v, axis=-1, keepdims=True)
            xc = v - mu
            rstd = lax.rsqrt(jnp.mean(xc * xc, axis=-1, keepdims=True) + EPS)
            vn = ((xc * rstd) * lg_ref[...] + lb_ref[...]).astype(BF16)
            for g in range(NG):
                cols = slice(g * CH, (g + 1) * CH)
                s = _dot(wm[g], vn[:, cols]) + bst_ref[:, g:g + 1]
                o_ref[rows, cols] = (u[:, cols] * s).astype(BF16)

    return pl.pallas_call(
        body, name="gmlp_fwd", grid=(t // tm,),
        in_specs=[pl.BlockSpec((tm, BW), lambda i: (i, 0)),
                  pl.BlockSpec((tm, BW), lambda i: (i, 1)),
                  pl.BlockSpec((1, BW), lambda i: (0, 0)),
                  pl.BlockSpec((1, BW), lambda i: (0, 0)),
                  pl.BlockSpec((NG, CH, CH), lambda i: (0, 0, 0)),
                  pl.BlockSpec((CH, NG), lambda i: (0, 0))],
        out_specs=pl.BlockSpec((tm, BW), lambda i: (i, 0)),
        out_shape=jax.ShapeDtypeStruct((t, BW), BF16),
        compiler_params=_cp(("parallel",)),
    )(proj, proj, ln_g, ln_b, ws, bst)


def _gmlp_bwd(proj, dout, ln_g, ln_b, ws, bst):
    t = proj.shape[0]
    tm = _row_tile(t, 512)

    def body(u_ref, v_ref, do_ref, lg_ref, lb_ref, ws_ref, bst_ref, dp_ref, gws_ref, gbs_ref, glg_ref, glb_ref):
        @pl.when(pl.program_id(0) == 0)
        def _():
            gws_ref[...] = jnp.zeros_like(gws_ref)
            gbs_ref[...] = jnp.zeros_like(gbs_ref)
            glg_ref[...] = jnp.zeros_like(glg_ref)
            glb_ref[...] = jnp.zeros_like(glb_ref)

        tril = _tril_mask()
        wm = [jnp.where(tril, ws_ref[g], 0.0).astype(BF16) for g in range(NG)]
        ones = jnp.ones((CH, CH), BF16)
        lg = lg_ref[...]
        for ch in range(tm // CH):
            rows = slice(ch * CH, (ch + 1) * CH)
            u, du_fac = _gelu_and_grad(u_ref[rows, :].astype(F32))
            v, dv_fac = _gelu_and_grad(v_ref[rows, :].astype(F32))
            do = do_ref[rows, :].astype(F32)
            mu = jnp.mean(v, axis=-1, keepdims=True)
            xc = v - mu
            rstd = lax.rsqrt(jnp.mean(xc * xc, axis=-1, keepdims=True) + EPS)
            xh = xc * rstd
            vn = (xh * lg + lb_ref[...]).astype(BF16)
            dvn_parts = []
            for g in range(NG):
                cols = slice(g * CH, (g + 1) * CH)
                s = _dot(wm[g], vn[:, cols]) + bst_ref[:, g:g + 1]
                dp_ref[rows, cols] = (do[:, cols] * s * du_fac[:, cols]).astype(BF16)
                ds = (do[:, cols] * u[:, cols]).astype(BF16)
                gws_ref[g] += jnp.where(tril, _dot(ds, vn[:, cols], NT), 0.0)
                gbs_ref[g] += _dot(ds, ones)
                dvn_parts.append(_dot(wm[g], ds, TN))
            dvn = jnp.concatenate(dvn_parts, axis=1)
            glb_ref[...] += jnp.sum(dvn, axis=0, keepdims=True)
            glg_ref[...] += jnp.sum(dvn * xh, axis=0, keepdims=True)
            dxh = dvn * lg
            dv = rstd * (dxh - jnp.mean(dxh, axis=-1, keepdims=True)
                         - xh * jnp.mean(dxh * xh, axis=-1, keepdims=True))
            dp_ref[rows, BW:2 * BW] = (dv * dv_fac).astype(BF16)

    small = pl.BlockSpec((NG, CH, CH), lambda i: (0, 0, 0))
    vec = pl.BlockSpec((1, BW), lambda i: (0, 0))
    return pl.pallas_call(
        body, name="gmlp_bwd", grid=(t // tm,),
        in_specs=[pl.BlockSpec((tm, BW), lambda i: (i, 0)),
                  pl.BlockSpec((tm, BW), lambda i: (i, 1)),
                  pl.BlockSpec((tm, BW), lambda i: (i, 0)),
                  vec, vec, small, pl.BlockSpec((CH, NG), lambda i: (0, 0))],
        out_specs=[pl.BlockSpec((tm, 2 * BW), lambda i: (i, 0)), small, small, vec, vec],
        out_shape=[jax.ShapeDtypeStruct((t, 2 * BW), BF16),
                   jax.ShapeDtypeStruct((NG, CH, CH), F32), jax.ShapeDtypeStruct((NG, CH, CH), F32),
                   jax.ShapeDtypeStruct((1, BW), F32), jax.ShapeDtypeStruct((1, BW), F32)],
        compiler_params=_cp(("arbitrary",)),
    )(proj, proj, dout, ln_g, ln_b, ws, bst)


def _pool_bands():
    row = lax.broadcasted_iota(jnp.int32, (CH, CH), 0)
    col = lax.broadcasted_iota(jnp.int32, (CH, CH), 1)
    cur, prev = [], []
    for w in POOL_WINDOWS:
        cur.append(jnp.where((row >= col) & (row - col < w), 1.0, 0.0).astype(BF16))
        prev.append(jnp.where(row + CH - col < w, 1.0, 0.0).astype(BF16))
    return cur, prev


def _pool_inv_count(r0, w):
    pos = r0 + lax.broadcasted_iota(jnp.int32, (CH, 1), 0)
    return 1.0 / jnp.minimum(pos + 1, w).astype(F32)


def _pool_diff(x_ref, r0, rp, has_prev, cur, prev, g):
    cols = slice(g * CH, (g + 1) * CH)
    xc = x_ref[pl.ds(r0, CH), cols]
    xp = x_ref[pl.ds(rp, CH), cols]
    ws = _dot(cur[g], xc) + has_prev * _dot(prev[g], xp)
    return ws * _pool_inv_count(r0, POOL_WINDOWS[g]) - xc.astype(F32)


def _pool_fwd(proj3, pw, pscale):
    nb, seq, _ = proj3.shape
    nch = seq // CH

    def body(x_ref, pw_ref, ps_ref, o_ref):
        cur, prev = _pool_bands()
        pwb = [pw_ref[g].astype(BF16) for g in range(NG)]

        def chunk(ch, carry):
            r0 = pl.multiple_of(ch * CH, CH)
            rp = pl.multiple_of(jnp.maximum(ch - 1, 0) * CH, CH)
            has_prev = jnp.where(ch > 0, 1.0, 0.0)
            for g in range(NG):
                cols = slice(g * CH, (g + 1) * CH)
                d = _pool_diff(x_ref, r0, rp, has_prev, cur, prev, g)
                y = _dot(d.astype(BF16), pwb[g]) * ps_ref[:, cols]
                o_ref[pl.ds(r0, CH), cols] = y.astype(BF16)
            return carry

        lax.fori_loop(0, nch, chunk, 0, unroll=2)

    return pl.pallas_call(
        body, name="pool_fwd", grid=(nb,),
        in_specs=[pl.BlockSpec((None, seq, BW), lambda b: (b, 0, 5)),
                  pl.BlockSpec((NG, CH, CH), lambda b: (0, 0, 0)),
                  pl.BlockSpec((1, BW), lambda b: (0, 0))],
        out_specs=pl.BlockSpec((None, seq, BW), lambda b: (b, 0, 0)),
        out_shape=jax.ShapeDtypeStruct((nb, seq, BW), BF16),
        compiler_params=_cp(("parallel",)),
    )(proj3, pw, pscale)


def _pool_bwd(proj3, dout3, pw, pscale):
    nb, seq, _ = proj3.shape
    nch = seq // CH

    def body(x_ref, do_ref, pw_ref, ps_ref, dx_ref, gpw_ref, gps_ref, e_ref):
        @pl.when(pl.program_id(0) == 0)
        def _():
            gpw_ref[...] = jnp.zeros_like(gpw_ref)
            gps_ref[...] = jnp.zeros_like(gps_ref)

        cur, prev = _pool_bands()
        pwb = [pw_ref[g].astype(BF16) for g in range(NG)]

        def first(ch, carry):
            r0 = pl.multiple_of(ch * CH, CH)
            rp = pl.multiple_of(jnp.maximum(ch - 1, 0) * CH, CH)
            has_prev = jnp.where(ch > 0, 1.0, 0.0)
            for g in range(NG):
                cols = slice(g * CH, (g + 1) * CH)
                d = _pool_diff(x_ref, r0, rp, has_prev, cur, prev, g).astype(BF16)
                do = do_ref[pl.ds(r0, CH), cols].astype(F32)
                ypre = _dot(d, pwb[g])
                gps_ref[:, cols] += jnp.sum(do * ypre, axis=0, keepdims=True)
                dyp = (do * ps_ref[:, cols]).astype(BF16)
                gpw_ref[g] += _dot(d, dyp, TN)
                e_ref[pl.ds(r0, CH), cols] = _dot(dyp, pwb[g], NT)
            return carry

        lax.fori_loop(0, nch, first, 0, unroll=2)

        def second(ch, carry):
            r0 = pl.multiple_of(ch * CH, CH)
            rn = pl.multiple_of(jnp.minimum(ch + 1, nch - 1) * CH, CH)
            has_next = jnp.where(ch < nch - 1, 1.0, 0.0)
            for g in range(NG):
                cols = slice(g * CH, (g + 1) * CH)
                w = POOL_WINDOWS[g]
                dd = e_ref[pl.ds(r0, CH), cols]
                ec = (dd * _pool_inv_count(r0, w)).astype(BF16)
                en = (e_ref[pl.ds(rn, CH), cols] * _pool_inv_count(rn, w)).astype(BF16)
                dx = _dot(cur[g], ec, TN) + has_next * _dot(prev[g], en, TN) - dd
                dx_ref[pl.ds(r0, CH), cols] = dx.astype(BF16)
            return carry

        lax.fori_loop(0, nch, second, 0, unroll=2)

    small = pl.BlockSpec((NG, CH, CH), lambda b: (0, 0, 0))
    vec = pl.BlockSpec((1, BW), lambda b: (0, 0))
    return pl.pallas_call(
        body, name="pool_bwd", grid=(nb,),
        in_specs=[pl.BlockSpec((None, seq, BW), lambda b: (b, 0, 5)),
                  pl.BlockSpec((None, seq, BW), lambda b: (b, 0, 0)), small, vec],
        out_specs=[pl.BlockSpec((None, seq, BW), lambda b: (b, 0, 0)), small, vec],
        out_shape=[jax.ShapeDtypeStruct((nb, seq, BW), BF16),
                   jax.ShapeDtypeStruct((NG, CH, CH), F32), jax.ShapeDtypeStruct((1, BW), F32)],
        scratch_shapes=[pltpu.VMEM((seq, BW), F32)],
        compiler_params=_cp(("arbitrary",)),
    )(proj3, dout3, pw, pscale)


SB_BQ = 256
SB_BK = 256
SB_SCALE = HD ** -0.5


SB_EXIT = -110.0


def _sb_tile(qs, k, mask):
    z = _dot(qs, k, NT)
    lb = jnp.minimum(z, 0.0) - jnp.log(1.0 + jnp.exp(-jnp.abs(z)))
    lom = lb - z
    if mask is not None:
        lom = jnp.where(mask, lom, 0.0)
    return lb, lom


def _sb_alive(c):
    top = functools.reduce(jnp.maximum, [jnp.max(state[1]) for state in c])
    return (top > SB_EXIT).astype(jnp.int32)


def _sb_past_blocks(step, c, npast):
    def cond(s):
        return jnp.logical_and(s[0] < npast, s[1] > 0)

    def body(s):
        i, _, c = s
        c = step(pl.multiple_of((npast - 1 - i) * SB_BK, SB_BK), c, None)
        return i + 1, _sb_alive(c), c

    return lax.while_loop(cond, body, (jnp.int32(0), _sb_alive(c), c))[2]


def _sb_diag_mask(bq, d):
    row = lax.broadcasted_iota(jnp.int32, (bq, SB_BK), 0)
    col = lax.broadcasted_iota(jnp.int32, (bq, SB_BK), 1)
    return col + d * SB_BK < row


def _sb_scaled(q):
    return (q.astype(F32) * SB_SCALE).astype(BF16)


def _dot_tri(a, m):
    return _dot(a.astype(BF16), m)


def _dot_tri2(a, m):
    hi = a.astype(BF16)
    lo = (a - hi.astype(F32)).astype(BF16)
    return _dot(hi, m) + _dot(lo, m)


def _sb_fwd(proj3, gather=()):
    nb, seq, _ = proj3.shape
    bq = min(SB_BQ, seq)
    nq = seq // bq
    ndiag = bq // SB_BK

    def body(q_ref, k_ref, v_ref, o_ref):
        row = lax.broadcasted_iota(jnp.int32, (SB_BK, SB_BK), 0)
        col = lax.broadcasted_iota(jnp.int32, (SB_BK, SB_BK), 1)
        upper = jnp.where(row > col, 1.0, 0.0).astype(BF16)
        heads = [slice(hh * HD, (hh + 1) * HD) for hh in range(2)]

        def qloop(qi, carry):
            q0 = pl.multiple_of(qi * bq, bq)
            qs = [_sb_scaled(q_ref[pl.ds(q0, bq), lanes]) for lanes in heads]

            def step(k0, c, mask):
                tiles = [_sb_tile(q, k_ref[pl.ds(k0, SB_BK), lanes], mask) for lanes, q in zip(heads, qs)]
                sums = [_dot_tri(lom, upper) for _, lom in tiles]
                out = []
                for lanes, (acc, cr), (lb, lom), cs in zip(heads, c, tiles, sums):
                    a = jnp.exp(lb + (cs + cr))
                    if mask is not None:
                        a = jnp.where(mask, a, 0.0)
                    rsum = cs[:, 0:1] + lom[:, 0:1]
                    out.append((acc + _dot(a.astype(BF16), v_ref[pl.ds(k0, SB_BK), lanes]), cr + rsum))
                return tuple(out)

            c = tuple((jnp.zeros((bq, HD), F32), jnp.zeros((bq, 1), F32)) for _ in heads)
            for d in reversed(range(ndiag)):
                c = step(pl.multiple_of(q0 + d * SB_BK, SB_BK), c, _sb_diag_mask(bq, d))
            c = _sb_past_blocks(step, c, qi * ndiag)
            for lanes, (acc, _) in zip(heads, c):
                o_ref[pl.ds(q0, bq), lanes] = acc
            return carry

        lax.fori_loop(0, nq, qloop, 0)

    def spec(c0):
        return pl.BlockSpec((None, seq, 128), lambda b, p: (b, 0, c0 + p))

    grid = (nb, BW // 128)
    body, ex_in, ex_out, ex_shape, ex_sems = _host_exchange(body, 3, 1, grid, gather, False)
    outs = pl.pallas_call(
        body, name="sb_fwd", grid=grid,
        in_specs=[spec(8), spec(12), spec(16)] + ex_in,
        out_specs=[spec(0)] + ex_out,
        out_shape=[jax.ShapeDtypeStruct((nb, seq, BW), F32)] + ex_shape,
        scratch_shapes=ex_sems,
        compiler_params=_cp(("arbitrary", "arbitrary")),
    )(proj3, proj3, proj3, *gather)
    return outs[0], outs[1:]


def _sb_bwd(proj3, do3, o3, scatter=()):
    nb, seq, _ = proj3.shape
    bq = min(SB_BQ, seq)
    nq = seq // bq
    ndiag = bq // SB_BK

    def body(q_ref, k_ref, v_ref, do_ref, o_ref, dq_ref, dk_ref, dv_ref, dk_acc, dv_acc):
        row = lax.broadcasted_iota(jnp.int32, (SB_BK, SB_BK), 0)
        col = lax.broadcasted_iota(jnp.int32, (SB_BK, SB_BK), 1)
        upper = jnp.where(row > col, 1.0, 0.0).astype(BF16)
        later = jnp.where(row >= col, 1.0, 0.0).astype(BF16)
        dk_acc[...] = jnp.zeros_like(dk_acc)
        dv_acc[...] = jnp.zeros_like(dv_acc)
        heads = [slice(hh * HD, (hh + 1) * HD) for hh in range(2)]

        def qloop(qi, carry):
            q0 = pl.multiple_of(qi * bq, bq)
            qs = [_sb_scaled(q_ref[pl.ds(q0, bq), lanes]) for lanes in heads]
            dos = [do_ref[pl.ds(q0, bq), lanes] for lanes in heads]
            gtot = [jnp.sum(do.astype(F32) * o_ref[pl.ds(q0, bq), lanes], axis=1, keepdims=True)
                    for do, lanes in zip(dos, heads)]

            def step(k0, c, mask):
                ks = [k_ref[pl.ds(k0, SB_BK), lanes] for lanes in heads]
                tiles = [_sb_tile(q, k, mask) for q, k in zip(qs, ks)]
                sums = [_dot_tri(lom, upper) for _, lom in tiles]
                das = [_dot(do, v_ref[pl.ds(k0, SB_BK), lanes], NT) for do, lanes in zip(dos, heads)]
                gls, avs = [], []
                for hh, (_, cr, _) in enumerate(c):
                    a = jnp.exp(tiles[hh][0] + (sums[hh] + cr))
                    if mask is not None:
                        a = jnp.where(mask, a, 0.0)
                    ab = a.astype(BF16)
                    avs.append(ab)
                    gls.append(das[hh] * ab.astype(F32))
                tails = [_dot_tri2(gl, later) for gl in gls]
                out = []
                for hh, (dq, cr, gdone) in enumerate(c):
                    lb, lom = tiles[hh]
                    pre = gtot[hh] - gdone - tails[hh]
                    dz = gls[hh] - jnp.exp(lb) * (gls[hh] + pre)
                    if mask is not None:
                        dz = jnp.where(mask, dz, 0.0)
                    dz = dz.astype(BF16)
                    dk_acc[hh, pl.ds(k0, SB_BK), :] += _dot(dz, qs[hh], TN)
                    dv_acc[hh, pl.ds(k0, SB_BK), :] += _dot(avs[hh], dos[hh], TN)
                    rsum = sums[hh][:, 0:1] + lom[:, 0:1]
                    out.append((dq + _dot(dz, ks[hh]), cr + rsum, gdone + tails[hh][:, 0:1]))
                return tuple(out)

            c = tuple((jnp.zeros((bq, HD), F32), jnp.zeros((bq, 1), F32), jnp.zeros((bq, 1), F32))
                      for _ in heads)
            for d in reversed(range(ndiag)):
                c = step(pl.multiple_of(q0 + d * SB_BK, SB_BK), c, _sb_diag_mask(bq, d))
            c = _sb_past_blocks(step, c, qi * ndiag)
            for lanes, (dq, _, _) in zip(heads, c):
                dq_ref[pl.ds(q0, bq), lanes] = (dq * SB_SCALE).astype(BF16)
            return carry

        lax.fori_loop(0, nq, qloop, 0)
        for hh in range(2):
            lanes = slice(hh * HD, (hh + 1) * HD)
            dk_ref[:, lanes] = dk_acc[hh].astype(BF16)
            dv_ref[:, lanes] = dv_acc[hh].astype(BF16)

    def spec(c0):
        return pl.BlockSpec((None, seq, 128), lambda b, p: (b, 0, c0 + p))

    grid = (nb, BW // 128)
    body, ex_in, ex_out, ex_shape, ex_sems = _host_exchange(body, 5, 3, grid, scatter, True)
    outs = pl.pallas_call(
        body, name="sb_bwd", grid=grid,
        in_specs=[spec(8), spec(12), spec(16), spec(0), spec(0)] + ex_in,
        out_specs=[spec(0), spec(0), spec(0)] + ex_out,
        out_shape=[jax.ShapeDtypeStruct((nb, seq, BW), BF16)] * 3 + ex_shape,
        scratch_shapes=[pltpu.VMEM((2, seq, HD), F32), pltpu.VMEM((2, seq, HD), F32)] + ex_sems,
        compiler_params=_cp(("arbitrary", "arbitrary")),
    )(proj3, proj3, proj3, do3, o3, *scatter)
    return outs[:3], outs[3:]


def _merge_fwd(brs, wb, proj):
    t = proj.shape[0]
    tm = _row_tile(t, 512)
    tn = 512
    nj = D // tn

    def body(b0, b1, b2, wb_ref, l0, l1, l2, m_ref, y0, y1, y2):
        acc = None
        for br, n, lg, y_ref in ((b0, 0, l0, y0), (b1, 1, l1, y1), (b2, 2, l2, y2)):
            y = _dot(br[...].astype(BF16), wb_ref[n])
            y_ref[...] = y.astype(BF16)
            term = jax.nn.sigmoid(lg[...].astype(F32)) * y
            acc = term if acc is None else acc + term
        m_ref[...] = acc.astype(BF16)

    def lspec(n):
        return pl.BlockSpec((tm, tn), lambda i, j: (i, (3 * D + n * D) // tn + j))

    tile = pl.BlockSpec((tm, tn), lambda i, j: (i, j))
    bspec = pl.BlockSpec((tm, BW), lambda i, j: (i, 0))
    return pl.pallas_call(
        body, name="merge_fwd", grid=(t // tm, nj),
        in_specs=[bspec, bspec, bspec, pl.BlockSpec((NB, BW, tn), lambda i, j: (0, 0, j)),
                  lspec(0), lspec(1), lspec(2)],
        out_specs=[tile] * 4,
        out_shape=[jax.ShapeDtypeStruct((t, D), BF16)] * 4,
        compiler_params=_cp(("parallel", "parallel")),
    )(brs[0], brs[1], brs[2], wb, proj, proj, proj)


def _merge_bwd(dm, ys, proj):
    t = proj.shape[0]
    tm = _row_tile(t, 512)
    tn = 512

    def body(dm_ref, y0, y1, y2, l0, l1, l2, dl0, dl1, dl2, dy0, dy1, dy2):
        dmv = dm_ref[...].astype(F32)
        for y_ref, lg, dl_ref, dy_ref in ((y0, l0, dl0, dy0), (y1, l1, dl1, dy1), (y2, l2, dl2, dy2)):
            g = jax.nn.sigmoid(lg[...].astype(F32))
            dl_ref[...] = (dmv * y_ref[...].astype(F32) * g * (1.0 - g)).astype(BF16)
            dy_ref[...] = (dmv * g).astype(BF16)

    def lspec(n):
        return pl.BlockSpec((tm, tn), lambda i, j: (i, (3 * D + n * D) // tn + j))

    tile = pl.BlockSpec((tm, tn), lambda i, j: (i, j))
    return pl.pallas_call(
        body, name="merge_bwd", grid=(t // tm, D // tn),
        in_specs=[tile] * 4 + [lspec(0), lspec(1), lspec(2)],
        out_specs=[tile] * 6,
        out_shape=[jax.ShapeDtypeStruct((t, D), BF16)] * 6,
        compiler_params=_cp(("parallel", "parallel")),
    )(dm, ys[0], ys[1], ys[2], proj, proj, proj)


def _adamw_rows(rows):
    if rows <= 512:
        return rows
    return next(tr for tr in (512, 384, 352, 256, 128, 64, 32, 16, 8) if rows % tr == 0)


def _adamw_math(npart, p_ref, w_ref, m_ref, v_ref, g_ref, d_ref, mo_ref, vo_ref):
    c1 = 1.0 - ADAM_B1 ** ADAM_STEP
    c2 = 1.0 - ADAM_B2 ** ADAM_STEP
    g = p_ref[0].astype(F32)
    for p in range(1, npart):
        g = g + p_ref[p].astype(F32)
    mn = ADAM_B1 * m_ref[...] + (1.0 - ADAM_B1) * g
    vn = ADAM_B2 * v_ref[...] + (1.0 - ADAM_B2) * (g * g)
    m_hat = mn / c1
    v_hat = vn / c2
    g_ref[...] = g
    d_ref[...] = -ADAM_LR * (m_hat / (jnp.sqrt(v_hat) + ADAM_EPS) + ADAM_WD * w_ref[...])
    mo_ref[...] = mn
    vo_ref[...] = vn


def _adamw_layer(name, parts, w, m, v, layer, bufs):
    nl, cols = w.shape[0], w.shape[-1]
    rows = int(math.prod(w.shape[1:-1]))
    npart = parts.shape[0]
    tr = _adamw_rows(rows)
    if bufs is None:
        bufs = [lax.empty((nl, rows, cols), F32) for _ in range(4)]

    def body(p_ref, w_ref, m_ref, v_ref, b0, b1, b2, b3, g_ref, d_ref, mo_ref, vo_ref):
        _adamw_math(npart, p_ref, w_ref, m_ref, v_ref, g_ref, d_ref, mo_ref, vo_ref)

    slab = pl.BlockSpec((None, tr, cols), lambda i: (layer, i, 0))
    sds = jax.ShapeDtypeStruct((nl, rows, cols), F32)
    return pl.pallas_call(
        body, name=name, grid=(rows // tr,),
        in_specs=[pl.BlockSpec((npart, tr, cols), lambda i: (0, i, 0)), slab, slab, slab] + [_HBM] * 4,
        out_specs=[slab] * 4, out_shape=[sds] * 4,
        input_output_aliases={4: 0, 5: 1, 6: 2, 7: 3},
        compiler_params=_cp(("parallel",)),
    )(parts.reshape(npart, rows, cols), w.reshape(nl, rows, cols), m.reshape(nl, rows, cols),
      v.reshape(nl, rows, cols), *bufs)


def _adamw_reduce(name, parts, w, m, v):
    shape = w.shape
    cols = shape[-1]
    rows = int(math.prod(shape[:-1])) if len(shape) > 1 else 1
    npart = parts.shape[0]
    tr = _adamw_rows(rows)

    def body(p_ref, w_ref, m_ref, v_ref, g_ref, d_ref, mo_ref, vo_ref):
        _adamw_math(npart, p_ref, w_ref, m_ref, v_ref, g_ref, d_ref, mo_ref, vo_ref)

    tile = pl.BlockSpec((tr, cols), lambda i: (i, 0))
    sds = jax.ShapeDtypeStruct((rows, cols), F32)
    outs = pl.pallas_call(
        body, name=name, grid=(rows // tr,),
        in_specs=[pl.BlockSpec((npart, tr, cols), lambda i: (0, i, 0)), tile, tile, tile],
        out_specs=[tile] * 4, out_shape=[sds] * 4,
        compiler_params=_cp(("parallel",)),
    )(parts.reshape(npart, rows, cols), w.reshape(rows, cols), m.reshape(rows, cols), v.reshape(rows, cols))
    return tuple(o.reshape(shape) for o in outs)


def _pad_ffn_in(w):
    lead = w.shape[:-1]
    w = w.reshape(lead + (2, FF_HALF))
    w = jnp.pad(w, [(0, 0)] * len(lead) + [(0, 0), (0, FF_HALF_PAD - FF_HALF)])
    return w.reshape(lead + (FF_IN_PAD,))


def kernel(x, c, rms_g1, rms_g2, w_ada, b_ada, w_in, gm_ln_g, gm_ln_b, gm_w_spatial, gm_b_spatial, pool_w, pool_scale, w_branch, w_out, w_ffn_in, w_ffn_out, final_g, loss_target, m_rms_g1, m_rms_g2, m_w_ada, m_b_ada, m_w_in, m_gm_ln_g, m_gm_ln_b, m_gm_w_spatial, m_gm_b_spatial, m_pool_w, m_pool_scale, m_w_branch, m_w_out, m_w_ffn_in, m_w_ffn_out, m_final_g, v_rms_g1, v_rms_g2, v_w_ada, v_b_ada, v_w_in, v_gm_ln_g, v_gm_ln_b, v_gm_w_spatial, v_gm_b_spatial, v_pool_w, v_pool_scale, v_w_branch, v_w_out, v_w_ffn_in, v_w_ffn_out, v_final_g):
    nb, seq, _ = x.shape
    nl = w_in.shape[0]
    t = nb * seq
    ntot = NDEV * nb
    me = _my_index()
    assert x.shape[2] == D and w_in.shape[1:] == (D, 768) and w_ffn_in.shape[1:] == (D, FF_IN_SHARD)
    assert seq % CH == 0

    w_ffn_in_p = _pad_ffn_in(w_ffn_in).astype(BF16)
    w_ffn_out_p = jnp.pad(w_ffn_out, ((0, 0), (0, FF_HALF_PAD - FF_HALF), (0, 0))).astype(BF16)
    w_in_b = w_in.astype(BF16)
    w_branch_b = w_branch.astype(BF16)
    w_out_b = w_out.astype(BF16)
    (g_in_next,) = _exchange([w_in_b[0]], "gather_w_in0", False)

    (c_all,) = _exchange([c], "gather_c", False)
    c_all = c_all.reshape(ntot, D)
    b_blk = lax.dynamic_slice_in_dim(b_ada, me * 768, 768, axis=1).reshape(nl, 1, 768)
    mod_blk = _ada_fwd(c_all, w_ada, b_blk)
    (mod_all,) = _exchange([mod_blk], "gather_mod", False)
    mod_all = jnp.transpose(mod_all, (1, 2, 0, 3)).reshape(nl, ntot, NMOD * D)
    mod = lax.dynamic_slice_in_dim(mod_all, me * nb, nb, axis=1).reshape(nl, nb, NMOD, 1, D)

    saved = []
    gathered = []
    xc = x
    for l in range(nl):
        sh1, sc1, gt1, sh2, sc2, gt2 = [mod[l, :, i] for i in range(NMOD)]
        h = _norm_mod_fwd(xc, rms_g1[l].reshape(1, D), sc1, sh1).reshape(t, D)
        proj, (g_ffn_in_w,) = _mm_colblocked("proj_fwd", h, g_in_next, BF16, [w_ffn_in_p[l]])
        proj3 = proj.reshape(nb, seq, IN_COLS)
        br_gm = _gmlp_fwd(proj, gm_ln_g[l].reshape(1, BW), gm_ln_b[l].reshape(1, BW),
                          gm_w_spatial[l], gm_b_spatial[l].T)
        sb_o, got = _sb_fwd(proj3, [w_branch_b[l], w_out_b[l]] + ([w_in_b[l + 1]] if l + 1 < nl else []))
        gw = dict(w_in=g_in_next,
                  w_branch=jnp.transpose(got[0], (1, 2, 0, 3)).reshape(NB, BW, D),
                  w_out=got[1].reshape(D, D),
                  w_ffn_in=g_ffn_in_w)
        gathered.append(gw)
        if l + 1 < nl:
            g_in_next = got[2]
        br_pool = _pool_fwd(proj3, pool_w[l], pool_scale[l].reshape(1, BW))
        brs = [br_gm, sb_o.reshape(t, BW), br_pool.reshape(t, BW)]
        merged, y0, y1, y2 = _merge_fwd(brs, gw["w_branch"], proj)
        x_mid, mo = _mm_residual("out_fwd", merged, gw["w_out"], xc.reshape(t, D), gt1, seq)
        x_mid = x_mid.reshape(nb, seq, D)
        h2 = _norm_mod_fwd(x_mid, rms_g2[l].reshape(1, D), sc2, sh2).reshape(t, D)
        fg, fu, act, got = _ffn_in_fwd(h2, gw["w_ffn_in"], [w_ffn_out_p[l]])
        gw["w_ffn_out"] = got[0].reshape(FFP, D)
        x_out, fo = _mm_residual("ffn_out_fwd", act, gw["w_ffn_out"], x_mid.reshape(t, D), gt2, seq)
        saved.append(dict(x_in=xc, h=h, proj=proj, brs=brs, sb_o=sb_o, ys=(y0, y1, y2), merged=merged,
                          mo=mo, x_mid=x_mid, h2=h2, fg=fg, fu=fu, act=act, fo=fo))
        xc = x_out.reshape(nb, seq, D)

    dx, loss_part, dfinal_part, dfo, dgt2 = _loss_head(xc, loss_target, final_g.reshape(1, D),
                                                       saved[-1]["fo"].reshape(nb, seq, D), mod[nl - 1, :, 5])
    loss = lax.psum(jnp.sum(loss_part[:, 0, 0]), ("x", "y", "c"))

    big_names = ("w_in", "w_branch", "w_out", "w_ffn_in", "w_ffn_out")
    bufs = {name: None for name in big_names}
    w_ffn_in_t, m_w_ffn_in_t, v_w_ffn_in_t = [jnp.swapaxes(a, 1, 2) for a in (w_ffn_in, m_w_ffn_in, v_w_ffn_in)]
    small_parts = {k: [None] * nl for k in ("rms_g1", "rms_g2", "gm_ln_g", "gm_ln_b", "gm_w_spatial",
                                            "gm_b_spatial", "pool_w", "pool_scale")}
    dmod = [None] * nl
    for l in reversed(range(nl)):
        gw = gathered[l]
        sv = saved[l]
        sh1, sc1, gt1, sh2, sc2, gt2 = [mod[l, :, i] for i in range(NMOD)]
        dfo = dfo.reshape(t, D)
        g_ffn_out = _mm_tn("ffn_out_wgrad", sv["act"], dfo)
        dfg, dfu = _ffn_out_dgrad(dfo, gw["w_ffn_out"], sv["fg"], sv["fu"])
        dh2 = _ffn_in_dgrad(dfg, dfu, gw["w_ffn_in"])
        g_ffn_in = _ffn_in_wgrad(sv["h2"], dfg, dfu)
        dx_mid, dsh2, dsc2, dg2, dmo, dgt1 = _norm_mod_bwd(
            sv["x_mid"], dh2.reshape(nb, seq, D), dx, rms_g2[l].reshape(1, D), sc2,
            gate=(sv["mo"].reshape(nb, seq, D), gt1))
        dmo = dmo.reshape(t, D)
        dmerged = _mm_nt("out_dgrad", dmo, gw["w_out"], BF16)
        g_out = _mm_tn("out_wgrad", sv["merged"], dmo)
        dls_dys = _merge_bwd(dmerged, sv["ys"], sv["proj"])
        dls, dys = dls_dys[:3], dls_dys[3:]
        dbrs, g_br = [], []
        for n in range(NB):
            dbrs.append(_mm_nt("branch_dgrad", dys[n], gw["w_branch"], BF16, w_lead=n))
            g_br.append(_mm_tn("branch_wgrad", sv["brs"][n], dys[n]))
        proj3 = sv["proj"].reshape(nb, seq, IN_COLS)
        d_gm, g_ws, g_bs, g_lg, g_lb = _gmlp_bwd(sv["proj"], dbrs[0], gm_ln_g[l].reshape(1, BW),
                                                 gm_ln_b[l].reshape(1, BW), gm_w_spatial[l], gm_b_spatial[l].T)
        g_br_dev = jnp.transpose(jnp.stack(g_br).reshape(NB, BW, NDEV, D // NDEV), (2, 0, 1, 3))
        carried = [g_br_dev, g_out.reshape(NDEV, D // NDEV, D), g_ffn_in, g_ffn_out.reshape(NDEV, FF_HALF_PAD, D)]
        d_sb, recv = _sb_bwd(proj3, dbrs[1].reshape(nb, seq, BW), sv["sb_o"], carried)
        bufs["w_branch"] = _adamw_layer("adamw_w_branch", recv[0], w_branch, m_w_branch, v_w_branch, l,
                                        bufs["w_branch"])
        bufs["w_out"] = _adamw_layer("adamw_w_out", recv[1], w_out, m_w_out, v_w_out, l, bufs["w_out"])
        r_fi = recv[2].reshape(NDEV, 2, FF_HALF_PAD, D)[:, :, :FF_HALF].reshape(NDEV, FF_IN_SHARD, D)
        bufs["w_ffn_in"] = _adamw_layer("adamw_w_ffn_in", r_fi, w_ffn_in_t, m_w_ffn_in_t, v_w_ffn_in_t, l,
                                        bufs["w_ffn_in"])
        bufs["w_ffn_out"] = _adamw_layer("adamw_w_ffn_out", recv[3][:, :FF_HALF], w_ffn_out, m_w_ffn_out,
                                         v_w_ffn_out, l, bufs["w_ffn_out"])
        d_pool, g_pw, g_ps = _pool_bwd(proj3, dbrs[2].reshape(nb, seq, BW), pool_w[l], pool_scale[l].reshape(1, BW))
        dproj = jnp.concatenate([d_gm] + [a.reshape(t, BW) for a in d_sb] + [d_pool.reshape(t, BW)] + list(dls),
                                axis=1)
        g_in = _mm_colblocked_tn("proj_wgrad", sv["h"], dproj)
        dh, (r_in,) = _mm_colblocked_nt("proj_dgrad", dproj, gw["w_in"], F32, [g_in])
        bufs["w_in"] = _adamw_layer("adamw_w_in", r_in, w_in, m_w_in, v_w_in, l, bufs["w_in"])
        dmod_tail = [dgt1, dsh2, dsc2, dgt2]
        if l > 0:
            dx, dsh1, dsc1, dg1, dfo, dgt2 = _norm_mod_bwd(
                sv["x_in"], dh.reshape(nb, seq, D), dx_mid, rms_g1[l].reshape(1, D), sc1,
                gate=(saved[l - 1]["fo"].reshape(nb, seq, D), mod[l - 1, :, 5]))
        else:
            dx, dsh1, dsc1, dg1 = _norm_mod_bwd(sv["x_in"], dh.reshape(nb, seq, D), dx_mid,
                                                rms_g1[l].reshape(1, D), sc1)

        dmod[l] = jnp.concatenate([dsh1, dsc1] + dmod_tail, axis=-1)
        small_parts["rms_g1"][l] = jnp.sum(dg1, axis=0)
        small_parts["rms_g2"][l] = jnp.sum(dg2, axis=0)
        small_parts["gm_ln_g"][l] = g_lg
        small_parts["gm_ln_b"][l] = g_lb
        small_parts["gm_w_spatial"][l] = g_ws
        small_parts["gm_b_spatial"][l] = g_bs[:, :, 0]
        small_parts["pool_w"][l] = g_pw
        small_parts["pool_scale"][l] = g_ps

    dmod_mine = jnp.stack(dmod).reshape(nl, nb, NMOD * D)
    names = list(small_parts)
    stacked = [jnp.stack(small_parts[k]).astype(BF16 if k in ("gm_w_spatial", "pool_w") else F32) for k in names]
    gathered_small = _exchange(stacked + [dfinal_part, dmod_mine], "gather_small", False)
    dmod_all = jnp.transpose(gathered_small[-1], (1, 0, 2, 3)).reshape(nl, ntot, NMOD * D)
    dfinal_all = gathered_small[-2].reshape(ntot, D)

    results = {}
    weights = dict(rms_g1=(rms_g1, m_rms_g1, v_rms_g1), rms_g2=(rms_g2, m_rms_g2, v_rms_g2),
                   gm_ln_g=(gm_ln_g, m_gm_ln_g, v_gm_ln_g), gm_ln_b=(gm_ln_b, m_gm_ln_b, v_gm_ln_b),
                   gm_w_spatial=(gm_w_spatial, m_gm_w_spatial, v_gm_w_spatial),
                   gm_b_spatial=(gm_b_spatial, m_gm_b_spatial, v_gm_b_spatial),
                   pool_w=(pool_w, m_pool_w, v_pool_w), pool_scale=(pool_scale, m_pool_scale, v_pool_scale))
    for k, parts in zip(names, gathered_small[:len(names)]):
        w, m, v = weights[k]
        results[k] = _adamw_reduce("adamw_" + k, parts.reshape((NDEV,) + w.shape), w, m, v)
    results["final_g"] = _adamw_reduce("adamw_final_g", dfinal_all, final_g, m_final_g, v_final_g)
    results["b_ada"] = _adamw_reduce("adamw_b_ada", jnp.transpose(dmod_all, (1, 0, 2)), b_ada, m_b_ada, v_b_ada)
    dmod_blk = lax.dynamic_slice_in_dim(dmod_all, me * 768, 768, axis=2)
    g_w_ada = _ada_bwd(c_all, dmod_blk)
    results["w_ada"] = _adamw_reduce("adamw_w_ada", g_w_ada[None], w_ada, m_w_ada, v_w_ada)
    stacked_w = dict(w_in=w_in, w_branch=w_branch, w_out=w_out, w_ffn_in=w_ffn_in_t, w_ffn_out=w_ffn_out)
    for name in big_names:
        results[name] = tuple(b.reshape(stacked_w[name].shape) for b in bufs[name])
    results["w_ffn_in"] = tuple(jnp.swapaxes(b, 1, 2) for b in results["w_ffn_in"])

    order = ["rms_g1", "rms_g2", "w_ada", "b_ada", "w_in", "gm_ln_g", "gm_ln_b", "gm_w_spatial", "gm_b_spatial",
             "pool_w", "pool_scale", "w_branch", "w_out", "w_ffn_in", "w_ffn_out", "final_g"]
    out = [loss, dx]
    for i in range(4):
        out.extend(results[k][i] for k in order)
    return tuple(out)
```

```python
import functools
import math

import jax
import jax.numpy as jnp
from jax import lax
from jax.experimental import pallas as pl
from jax.experimental.pallas import tpu as pltpu

F32 = jnp.float32
BF16 = jnp.bfloat16
MESH = pl.DeviceIdType.MESH

D = 1024
BW = 512
NB = 3
CH = 128
NG = 4
HD = 64
POOL_WINDOWS = (2, 4, 8, 16)
DFF = 2816
NMOD = 6
EPS = 1e-6
IN_COLS = 6 * D
NDEV = 8
FF_IN_SHARD = 2 * DFF // NDEV
FF_HALF = FF_IN_SHARD // 2
FF_HALF_PAD = 384
FF_IN_PAD = 2 * FF_HALF_PAD
FFP = NDEV // 2 * FF_IN_PAD

ADAM_LR = 0.001
ADAM_B1 = 0.9
ADAM_B2 = 0.999
ADAM_EPS = 1e-08
ADAM_WD = 0.01
ADAM_STEP = 10

VMEM_LIMIT = 48 * 1024 * 1024
BIG_ROWS = 2048
NN = (((1,), (0,)), ((), ()))
NT = (((1,), (1,)), ((), ()))
TN = (((0,), (0,)), ((), ()))


def _cp(sem=None):
    return pltpu.CompilerParams(dimension_semantics=sem, vmem_limit_bytes=VMEM_LIMIT)


def _dot(a, b, dims=NN):
    return lax.dot_general(a, b, dims, preferred_element_type=F32)


def _my_index():
    return 4 * lax.axis_index("x") + 2 * lax.axis_index("y") + lax.axis_index("c")


def _peer(k):
    x, y, c = lax.axis_index("x"), lax.axis_index("y"), lax.axis_index("c")
    px = 1 - x if k & 4 else x
    py = 1 - y if k & 2 else y
    pc = 1 - c if k & 1 else c
    return (px, py, pc), 4 * px + 2 * py + pc


def _exchange(xs, name, all_to_all):
    n = len(xs)

    def body(*refs):
        _exchange_start(refs[:n], refs[n:2 * n], refs[2 * n:], all_to_all)
        _exchange_relay(refs[:n], refs[n:2 * n], refs[2 * n:], all_to_all)
        _exchange_finish(refs[:n], refs[n:2 * n], refs[2 * n:], all_to_all)

    return pl.pallas_call(
        body, name=name, out_shape=_exchange_out_shape(xs, all_to_all),
        in_specs=[_HBM] * n, out_specs=[_HBM] * n, scratch_shapes=_exchange_sems(n),
    )(*xs)


_HBM = pl.BlockSpec(memory_space=pl.ANY)


def _exchange_out_shape(xs, all_to_all):
    if all_to_all:
        return [jax.ShapeDtypeStruct(x.shape, x.dtype) for x in xs]
    return [jax.ShapeDtypeStruct((NDEV,) + x.shape, x.dtype) for x in xs]


def _exchange_sems(n):
    return [pltpu.SemaphoreType.DMA((n * 7,)), pltpu.SemaphoreType.DMA((n * 7,)), pltpu.SemaphoreType.DMA((n,))]


def _all_to_all_copies(ins, outs, sems):
    send_sems, recv_sems, local_sems = sems
    me = _my_index()
    local, sends, recvs = [], [], []
    for a in range(len(ins)):
        local.append(pltpu.make_async_copy(ins[a].at[me], outs[a].at[me], local_sems.at[a]))
    for k in range(1, NDEV):
        dev, idx = _peer(k)
        for a in range(len(ins)):
            sem = dict(send_sem=send_sems.at[a * 7 + k - 1], recv_sem=recv_sems.at[a * 7 + k - 1],
                       device_id=dev, device_id_type=MESH)
            sends.append(pltpu.make_async_remote_copy(src_ref=ins[a].at[idx], dst_ref=outs[a].at[me], **sem))
            recvs.append(pltpu.make_async_remote_copy(src_ref=ins[a].at[idx], dst_ref=outs[a].at[idx], **sem))
    return local, sends, recvs


def _gather_copies(ins, outs, sems):
    send_sems, recv_sems, local_sems = sems
    x, y, c = lax.axis_index("x"), lax.axis_index("y"), lax.axis_index("c")
    me, other = 4 * x + 2 * y + c, 4 * x + 2 * y + (1 - c)
    other_dev = (x, y, 1 - c)
    chips = [(1 - x, y), (x, 1 - y), (1 - x, 1 - y)]
    local, own, relay, from_other = [], [], [], []
    for a in range(len(ins)):
        def copy(k, src, block, dev, a=a):
            return pltpu.make_async_remote_copy(
                src_ref=src, dst_ref=outs[a].at[block], send_sem=send_sems.at[a * 7 + k],
                recv_sem=recv_sems.at[a * 7 + k], device_id=dev, device_id_type=MESH)

        local.append(pltpu.make_async_copy(ins[a], outs[a].at[me], local_sems.at[a]))
        own.append(copy(0, ins[a], me, other_dev))
        from_other.append(copy(0, ins[a], other, other_dev))
        for j, (px, py) in enumerate(chips):
            far = 4 * px + 2 * py + c
            own.append(copy(1 + j, ins[a], me, (px, py, c)))
            relay.append((copy(1 + j, ins[a], far, (px, py, c)), copy(4 + j, outs[a].at[far], far, other_dev)))
            from_other.append(copy(4 + j, ins[a], 4 * px + 2 * py + (1 - c), other_dev))
    return local, own, relay, from_other


def _exchange_start(ins, outs, sems, all_to_all):
    local, sends = (_all_to_all_copies if all_to_all else _gather_copies)(ins, outs, sems)[:2]
    for cp in local + sends:
        cp.start()


def _exchange_relay(ins, outs, sems, all_to_all):
    if not all_to_all:
        for arrival, passing_on in _gather_copies(ins, outs, sems)[2]:
            arrival.wait_recv()
            passing_on.start()


def _exchange_finish(ins, outs, sems, all_to_all):
    if all_to_all:
        local, sends, recvs = _all_to_all_copies(ins, outs, sems)
    else:
        local, own, relay, recvs = _gather_copies(ins, outs, sems)
        sends = own + [passing_on for _, passing_on in relay]
    for cp in sends:
        cp.wait_send()
    for cp in recvs:
        cp.wait_recv()
    for cp in local:
        cp.wait()


def _host_exchange(body, n_in, n_out, grid, xs, all_to_all):
    n = len(xs)
    if n == 0:
        return body, [], [], [], []
    steps = math.prod(grid)
    half = steps // 2 if steps >= 3 else steps - 1

    def hosted(*refs):
        ins, ex_ins = refs[:n_in], refs[n_in:n_in + n]
        outs, ex_outs = refs[n_in + n:n_in + n + n_out], refs[n_in + n + n_out:n_in + 2 * n + n_out]
        scratch = refs[n_in + 2 * n + n_out:]
        own, sems = scratch[:len(scratch) - 3], scratch[len(scratch) - 3:]
        step = 0
        for a in range(len(grid)):
            step = step * grid[a] + pl.program_id(a)

        @pl.when(step == 0)
        def _():
            _exchange_start(ex_ins, ex_outs, sems, all_to_all)

        body(*ins, *outs, *own)

        @pl.when(step == half)
        def _():
            _exchange_relay(ex_ins, ex_outs, sems, all_to_all)

        @pl.when(step == steps - 1)
        def _():
            _exchange_finish(ex_ins, ex_outs, sems, all_to_all)

    return hosted, [_HBM] * n, [_HBM] * n, _exchange_out_shape(xs, all_to_all), _exchange_sems(n)


def _mm(name, a, b, grid, a_spec, b_spec, o_spec, out_sds, dims, acc_shape, carried=(), all_to_all=True):
    nk = grid[2]

    if nk == 1:
        def body(a_ref, b_ref, o_ref):
            o_ref[...] = _dot(a_ref[...].astype(BF16), b_ref[...].astype(BF16), dims).astype(o_ref.dtype)
        scratch = []
    else:
        def body(a_ref, b_ref, o_ref, acc_ref):
            k = pl.program_id(2)

            @pl.when(k == 0)
            def _():
                acc_ref[...] = jnp.zeros_like(acc_ref)

            acc_ref[...] += _dot(a_ref[...].astype(BF16), b_ref[...].astype(BF16), dims)

            @pl.when(k == nk - 1)
            def _():
                o_ref[...] = acc_ref[...].astype(o_ref.dtype)
        scratch = [pltpu.VMEM(acc_shape, F32)]

    body, ex_in, ex_out, ex_shape, ex_sems = _host_exchange(body, 2, 1, grid, carried, all_to_all)
    outs = pl.pallas_call(
        body, name=name, grid=grid, in_specs=[a_spec, b_spec] + ex_in, out_specs=[o_spec] + ex_out,
        out_shape=[out_sds] + ex_shape,
        scratch_shapes=scratch + ex_sems,
        compiler_params=_cp(("arbitrary",) * 3 if carried else ("parallel", "parallel", "arbitrary")),
    )(a, b, *carried)
    return (outs[0], outs[1:]) if carried else outs[0]


def _row_tile(t, want):
    tm = min(t, want)
    assert t % tm == 0
    return tm


def _mm_colblocked(name, a, wg, out_dtype, gather=()):
    t = a.shape[0]
    tm = _row_tile(t, BIG_ROWS)
    return _mm(name, a, wg, (t // tm, NDEV, 1),
               pl.BlockSpec((tm, D), lambda i, j, k: (i, 0)),
               pl.BlockSpec((None, D, 768), lambda i, j, k: (j, 0, 0)),
               pl.BlockSpec((tm, 768), lambda i, j, k: (i, j)),
               jax.ShapeDtypeStruct((t, NDEV * 768), out_dtype), NN, (tm, 768), gather, False)


def _mm_colblocked_nt(name, g, wg, out_dtype, scatter=()):
    t = g.shape[0]
    tm = _row_tile(t, BIG_ROWS)
    return _mm(name, g, wg, (t // tm, 1, NDEV),
               pl.BlockSpec((tm, 768), lambda i, j, k: (i, k)),
               pl.BlockSpec((None, D, 768), lambda i, j, k: (k, 0, 0)),
               pl.BlockSpec((tm, D), lambda i, j, k: (i, 0)),
               jax.ShapeDtypeStruct((t, D), out_dtype), NT, (tm, D), scatter)


_HALF = NDEV // 2


def _ffn_in_fwd(h2, wg, gather=()):
    t = h2.shape[0]
    tm = _row_tile(t, 1024)

    def body(a_ref, wg_ref, wu_ref, g_ref, u_ref, act_ref):
        a = a_ref[...]
        g = _dot(a, wg_ref[...])
        u = _dot(a, wu_ref[...])
        g_ref[...] = g.astype(BF16)
        u_ref[...] = u.astype(BF16)
        act_ref[...] = (g * jax.nn.sigmoid(g) * u).astype(BF16)

    tile = pl.BlockSpec((tm, 768), lambda i, j: (i, j))
    grid = (t // tm, _HALF)
    body, ex_in, ex_out, ex_shape, ex_sems = _host_exchange(body, 3, 3, grid, gather, False)
    outs = pl.pallas_call(
        body, name="ffn_in_fwd", grid=grid,
        in_specs=[pl.BlockSpec((tm, D), lambda i, j: (i, 0)),
                  pl.BlockSpec((None, D, 768), lambda i, j: (j, 0, 0)),
                  pl.BlockSpec((None, D, 768), lambda i, j: (j + _HALF, 0, 0))] + ex_in,
        out_specs=[tile] * 3 + ex_out, out_shape=[jax.ShapeDtypeStruct((t, FFP), BF16)] * 3 + ex_shape,
        scratch_shapes=ex_sems,
        compiler_params=_cp(("arbitrary", "arbitrary") if gather else ("parallel", "parallel")),
    )(h2, wg, wg, *gather)
    return outs[0], outs[1], outs[2], outs[3:]


def _ffn_out_dgrad(dfo, w, fg, fu):
    t = dfo.shape[0]
    tm = _row_tile(t, 1024)

    def body(a_ref, w_ref, g_ref, u_ref, dg_ref, du_ref):
        d = _dot(a_ref[...], w_ref[...], NT)
        g = g_ref[...].astype(F32)
        s = jax.nn.sigmoid(g)
        gs = g * s
        dg_ref[...] = (d * u_ref[...].astype(F32) * (s + gs * (1.0 - s))).astype(BF16)
        du_ref[...] = (d * gs).astype(BF16)

    tile = pl.BlockSpec((tm, 768), lambda i, j: (i, j))
    return pl.pallas_call(
        body, name="ffn_out_dgrad", grid=(t // tm, _HALF),
        in_specs=[pl.BlockSpec((tm, D), lambda i, j: (i, 0)),
                  pl.BlockSpec((768, D), lambda i, j: (j, 0)), tile, tile],
        out_specs=[tile] * 2, out_shape=[jax.ShapeDtypeStruct((t, FFP), BF16)] * 2,
        compiler_params=_cp(("parallel", "parallel")),
    )(dfo, w, fg, fu)


def _ffn_in_dgrad(dg, du, wg):
    t = dg.shape[0]
    tm = _row_tile(t, BIG_ROWS)

    def body(g_ref, u_ref, w_ref, o_ref, acc_ref):
        k = pl.program_id(1)

        @pl.when(k == 0)
        def _():
            acc_ref[...] = jnp.zeros_like(acc_ref)

        @pl.when(k < _HALF)
        def _():
            acc_ref[...] += _dot(g_ref[...], w_ref[...], NT)

        @pl.when(k >= _HALF)
        def _():
            acc_ref[...] += _dot(u_ref[...], w_ref[...], NT)

        @pl.when(k == NDEV - 1)
        def _():
            o_ref[...] = acc_ref[...]

    return pl.pallas_call(
        body, name="ffn_in_dgrad", grid=(t // tm, NDEV),
        in_specs=[pl.BlockSpec((tm, 768), lambda i, k: (i, jnp.minimum(k, _HALF - 1))),
                  pl.BlockSpec((tm, 768), lambda i, k: (i, jnp.maximum(k - _HALF, 0))),
                  pl.BlockSpec((None, D, 768), lambda i, k: (k, 0, 0))],
        out_specs=pl.BlockSpec((tm, D), lambda i, k: (i, 0)),
        out_shape=jax.ShapeDtypeStruct((t, D), F32),
        scratch_shapes=[pltpu.VMEM((tm, D), F32)],
        compiler_params=_cp(("parallel", "arbitrary")),
    )(dg, du, wg)


def _ffn_in_wgrad(h2, dg, du):
    t = h2.shape[0]
    tk = _row_tile(t, BIG_ROWS)
    nk = t // tk

    def body(a_ref, g_ref, u_ref, o_ref, acc_ref):
        j, k = pl.program_id(0), pl.program_id(1)

        @pl.when(k == 0)
        def _():
            acc_ref[...] = jnp.zeros_like(acc_ref)

        @pl.when(j < _HALF)
        def _():
            acc_ref[...] += _dot(g_ref[...], a_ref[...], TN)

        @pl.when(j >= _HALF)
        def _():
            acc_ref[...] += _dot(u_ref[...], a_ref[...], TN)

        @pl.when(k == nk - 1)
        def _():
            o_ref[...] = acc_ref[...].astype(BF16)

    return pl.pallas_call(
        body, name="ffn_in_wgrad", grid=(NDEV, nk),
        in_specs=[pl.BlockSpec((tk, D), lambda j, k: (k, 0)),
                  pl.BlockSpec((tk, 768), lambda j, k: (jnp.where(j < _HALF, k, 0), jnp.minimum(j, _HALF - 1))),
                  pl.BlockSpec((tk, 768), lambda j, k: (jnp.where(j < _HALF, 0, k), jnp.maximum(j - _HALF, 0)))],
        out_specs=pl.BlockSpec((None, 768, D), lambda j, k: (j, 0, 0)),
        out_shape=jax.ShapeDtypeStruct((NDEV, 768, D), BF16),
        scratch_shapes=[pltpu.VMEM((768, D), F32)],
        compiler_params=_cp(("parallel", "arbitrary")),
    )(h2, dg, du)


def _mm_colblocked_tn(name, a, g):
    t = a.shape[0]
    tk = _row_tile(t, BIG_ROWS)
    return _mm(name, a, g, (1, NDEV, t // tk),
               pl.BlockSpec((tk, D), lambda i, j, k: (k, 0)),
               pl.BlockSpec((tk, 768), lambda i, j, k: (k, j)),
               pl.BlockSpec((None, D, 768), lambda i, j, k: (j, 0, 0)),
               jax.ShapeDtypeStruct((NDEV, D, 768), BF16), TN, (D, 768))


def _mm_nt(name, a, w, out_dtype, a_col=0, w_lead=None):
    t = a.shape[0]
    if w_lead is None:
        kdim, n = w.shape
        b_spec = pl.BlockSpec((min(kdim, 1024), n), lambda i, j, k: (j, 0))
    else:
        _, kdim, n = w.shape
        b_spec = pl.BlockSpec((None, min(kdim, 1024), n), lambda i, j, k: (w_lead, j, 0))
    tn = min(kdim, 1024)
    tm = _row_tile(t, BIG_ROWS)
    return _mm(name, a, w, (t // tm, kdim // tn, 1),
               pl.BlockSpec((tm, n), lambda i, j, k: (i, a_col)),
               b_spec,
               pl.BlockSpec((tm, tn), lambda i, j, k: (i, j)),
               jax.ShapeDtypeStruct((t, kdim), out_dtype), NT, (tm, tn))


def _mm_tn(name, a, g, out_dtype=BF16):
    t, kdim = a.shape
    n = g.shape[1]
    tk = _row_tile(t, BIG_ROWS)
    tm = min(kdim, 1024)
    tn = min(n, 1024)
    return _mm(name, a, g, (kdim // tm, n // tn, t // tk),
               pl.BlockSpec((tk, tm), lambda i, j, k: (k, i)),
               pl.BlockSpec((tk, tn), lambda i, j, k: (k, j)),
               pl.BlockSpec((tm, tn), lambda i, j, k: (i, j)),
               jax.ShapeDtypeStruct((kdim, n), out_dtype), TN, (tm, tn))


def _mm_residual(name, a, w, x, gt, seq):
    t, kdim = a.shape
    tm = _row_tile(seq, 1024)
    tn = D
    tk = min(kdim, 1024)
    nk = kdim // tk
    per = seq // tm

    def body(a_ref, w_ref, x_ref, gt_ref, xo_ref, y_ref, acc_ref):
        k = pl.program_id(2)

        @pl.when(k == 0)
        def _():
            acc_ref[...] = jnp.zeros_like(acc_ref)

        acc_ref[...] += _dot(a_ref[...], w_ref[...])

        @pl.when(k == nk - 1)
        def _():
            y = acc_ref[...]
            xo_ref[...] = x_ref[...] + gt_ref[0] * y
            y_ref[...] = y.astype(BF16)

    return pl.pallas_call(
        body, name=name, grid=(t // tm, D // tn, nk),
        in_specs=[pl.BlockSpec((tm, tk), lambda i, j, k: (i, k)),
                  pl.BlockSpec((tk, tn), lambda i, j, k: (k, j)),
                  pl.BlockSpec((tm, tn), lambda i, j, k: (i, j)),
                  pl.BlockSpec((1, 1, tn), lambda i, j, k: (i // per, 0, j))],
        out_specs=[pl.BlockSpec((tm, tn), lambda i, j, k: (i, j)),
                   pl.BlockSpec((tm, tn), lambda i, j, k: (i, j))],
        out_shape=[jax.ShapeDtypeStruct((t, D), F32), jax.ShapeDtypeStruct((t, D), BF16)],
        scratch_shapes=[pltpu.VMEM((tm, tn), F32)],
        compiler_params=_cp(("parallel", "parallel", "arbitrary")),
    )(a, w, x, gt)


def _ada_fwd(c_all, w_ada, b_blk):
    nl = w_ada.shape[0]
    nb = c_all.shape[0]

    def body(c_ref, w_ref, b_ref, o_ref):
        c = c_ref[...]
        ca = (c * jax.nn.sigmoid(c)).astype(BF16)
        o_ref[...] = _dot(ca, w_ref[...].astype(BF16)) + b_ref[...]

    return pl.pallas_call(
        body, name="ada_fwd", grid=(nl,),
        in_specs=[pl.BlockSpec((nb, D), lambda l: (0, 0)),
                  pl.BlockSpec((None, D, 768), lambda l: (l, 0, 0)),
                  pl.BlockSpec((None, 1, 768), lambda l: (l, 0, 0))],
        out_specs=pl.BlockSpec((None, nb, 768), lambda l: (l, 0, 0)),
        out_shape=jax.ShapeDtypeStruct((nl, nb, 768), F32),
        compiler_params=_cp(("parallel",)),
    )(c_all, w_ada, b_blk)


def _ada_bwd(c_all, dmod_blk):
    nl = dmod_blk.shape[0]
    nb = c_all.shape[0]

    def body(c_ref, d_ref, o_ref):
        c = c_ref[...]
        ca = (c * jax.nn.sigmoid(c)).astype(BF16)
        o_ref[...] = _dot(ca, d_ref[...].astype(BF16), TN)

    return pl.pallas_call(
        body, name="ada_bwd", grid=(nl,),
        in_specs=[pl.BlockSpec((nb, D), lambda l: (0, 0)),
                  pl.BlockSpec((None, nb, 768), lambda l: (l, 0, 0))],
        out_specs=pl.BlockSpec((None, D, 768), lambda l: (l, 0, 0)),
        out_shape=jax.ShapeDtypeStruct((nl, D, 768), F32),
        compiler_params=_cp(("parallel",)),
    )(c_all, dmod_blk)


def _seq_tile(seq):
    return _row_tile(seq, 512)


def _norm_mod_fwd(x, g, sc, sh):
    nb, seq, _ = x.shape
    ts = _seq_tile(seq)

    def body(x_ref, g_ref, sc_ref, sh_ref, h_ref):
        xv = x_ref[0]
        r = lax.rsqrt(jnp.mean(xv * xv, axis=-1, keepdims=True) + EPS)
        h_ref[0] = ((xv * r) * g_ref[...] * (1.0 + sc_ref[0]) + sh_ref[0]).astype(BF16)

    return pl.pallas_call(
        body, name="norm_mod_fwd", grid=(nb, seq // ts),
        in_specs=[pl.BlockSpec((1, ts, D), lambda b, s: (b, s, 0)),
                  pl.BlockSpec((1, D), lambda b, s: (0, 0)),
                  pl.BlockSpec((1, 1, D), lambda b, s: (b, 0, 0)),
                  pl.BlockSpec((1, 1, D), lambda b, s: (b, 0, 0))],
        out_specs=pl.BlockSpec((1, ts, D), lambda b, s: (b, s, 0)),
        out_shape=jax.ShapeDtypeStruct((nb, seq, D), BF16),
        compiler_params=_cp(("parallel", "parallel")),
    )(x, g, sc, sh)


def _gate_bwd_tile(d, y_ref, gt_ref, dy_ref, dgt_ref):
    @pl.when(pl.program_id(1) == 0)
    def _():
        dgt_ref[...] = jnp.zeros_like(dgt_ref)

    dy_ref[0] = (gt_ref[0] * d).astype(BF16)
    dgt_ref[0] += jnp.sum(d * y_ref[0].astype(F32), axis=0, keepdims=True)


def _norm_mod_bwd(x, dh, dres, g, sc, gate=None):
    nb, seq, _ = x.shape
    ts = _seq_tile(seq)

    def body(x_ref, dh_ref, dres_ref, g_ref, sc_ref, *rest):
        if gate is None:
            dx_ref, dsh_ref, dsc_ref, dg_ref = rest
        else:
            y_ref, gt_ref, dx_ref, dsh_ref, dsc_ref, dg_ref, dy_ref, dgt_ref = rest

        @pl.when(pl.program_id(1) == 0)
        def _():
            dsh_ref[...] = jnp.zeros_like(dsh_ref)
            dsc_ref[...] = jnp.zeros_like(dsc_ref)
            dg_ref[...] = jnp.zeros_like(dg_ref)

        xv = x_ref[0]
        dh = dh_ref[0]
        gv = g_ref[...]
        onesc = 1.0 + sc_ref[0]
        r = lax.rsqrt(jnp.mean(xv * xv, axis=-1, keepdims=True) + EPS)
        xh = xv * r
        dsh_ref[0] += jnp.sum(dh, axis=0, keepdims=True)
        dsc_ref[0] += jnp.sum(dh * (xh * gv), axis=0, keepdims=True)
        dg_ref[0] += jnp.sum(dh * onesc * xh, axis=0, keepdims=True)
        dxh = dh * (gv * onesc)
        dx = r * (dxh - xh * jnp.mean(dxh * xh, axis=-1, keepdims=True))
        dx_total = dres_ref[0] + dx
        dx_ref[0] = dx_total
        if gate is not None:
            _gate_bwd_tile(dx_total, y_ref, gt_ref, dy_ref, dgt_ref)

    vec = jax.ShapeDtypeStruct((nb, 1, D), F32)
    vspec = pl.BlockSpec((1, 1, D), lambda b, s: (b, 0, 0))
    tile = pl.BlockSpec((1, ts, D), lambda b, s: (b, s, 0))
    gated = gate is not None
    return pl.pallas_call(
        body, name="norm_mod_bwd", grid=(nb, seq // ts),
        in_specs=[tile, tile, tile, pl.BlockSpec((1, D), lambda b, s: (0, 0)), vspec] + [tile, vspec] * gated,
        out_specs=[tile, vspec, vspec, vspec] + [tile, vspec] * gated,
        out_shape=[jax.ShapeDtypeStruct((nb, seq, D), F32), vec, vec, vec]
        + [jax.ShapeDtypeStruct((nb, seq, D), BF16), vec] * gated,
        compiler_params=_cp(("parallel", "arbitrary")),
    )(x, dh, dres, g, sc, *(gate or ()))


def _loss_head(x, tgt, g, y, gt):
    nb, seq, _ = x.shape
    ts = _seq_tile(seq)

    def body(x_ref, t_ref, g_ref, y_ref, gt_ref, dx_ref, loss_ref, dg_ref, dy_ref, dgt_ref):
        @pl.when(pl.program_id(1) == 0)
        def _():
            loss_ref[...] = jnp.zeros_like(loss_ref)
            dg_ref[...] = jnp.zeros_like(dg_ref)

        xv = x_ref[0]
        gv = g_ref[...]
        r = lax.rsqrt(jnp.mean(xv * xv, axis=-1, keepdims=True) + EPS)
        xh = xv * r
        err = xh * gv - t_ref[0]
        per_tok = jnp.mean(err * err, axis=-1, keepdims=True)
        loss_ref[0] += 0.5 * jnp.sum(per_tok, axis=0, keepdims=True)
        dy = err * (1.0 / D)
        dg_ref[0] += jnp.sum(dy * xh, axis=0, keepdims=True)
        dxh = dy * gv
        dx = r * (dxh - xh * jnp.mean(dxh * xh, axis=-1, keepdims=True))
        dx_ref[0] = dx
        _gate_bwd_tile(dx, y_ref, gt_ref, dy_ref, dgt_ref)

    tile = pl.BlockSpec((1, ts, D), lambda b, s: (b, s, 0))
    vspec = pl.BlockSpec((1, 1, D), lambda b, s: (b, 0, 0))
    vec = jax.ShapeDtypeStruct((nb, 1, D), F32)
    return pl.pallas_call(
        body, name="loss_head", grid=(nb, seq // ts),
        in_specs=[tile, tile, pl.BlockSpec((1, D), lambda b, s: (0, 0)), tile, vspec],
        out_specs=[tile, pl.BlockSpec((1, 1, 128), lambda b, s: (b, 0, 0)), vspec, tile, vspec],
        out_shape=[jax.ShapeDtypeStruct((nb, seq, D), F32), jax.ShapeDtypeStruct((nb, 1, 128), F32), vec,
                   jax.ShapeDtypeStruct((nb, seq, D), BF16), vec],
        compiler_params=_cp(("parallel", "arbitrary")),
    )(x, tgt, g, y, gt)


_GELU_C = math.sqrt(2.0 / math.pi)


def _gelu(x):
    return 0.5 * x * (1.0 + jnp.tanh(_GELU_C * (x + 0.044715 * (x * x * x))))


def _gelu_and_grad(x):
    t = jnp.tanh(_GELU_C * (x + 0.044715 * (x * x * x)))
    y = 0.5 * x * (1.0 + t)
    dy = 0.5 * (1.0 + t) + 0.5 * x * (1.0 - t * t) * (_GELU_C * (1.0 + 3.0 * 0.044715 * (x * x)))
    return y, dy


def _tril_mask():
    row = lax.broadcasted_iota(jnp.int32, (CH, CH), 0)
    col = lax.broadcasted_iota(jnp.int32, (CH, CH), 1)
    return row >= col


def _gmlp_fwd(proj, ln_g, ln_b, ws, bst):
    t = proj.shape[0]
    tm = _row_tile(t, 512)

    def body(u_ref, v_ref, lg_ref, lb_ref, ws_ref, bst_ref, o_ref):
        tril = _tril_mask()
        wm = [jnp.where(tril, ws_ref[g], 0.0).astype(BF16) for g in range(NG)]
        for ch in range(tm // CH):
            rows = slice(ch * CH, (ch + 1) * CH)
            u = _gelu(u_ref[rows, :].astype(F32))
            v = _gelu(v_ref[rows, :].astype(F32))
            mu = jnp.mean(v, axis=-1, keepdims=True)
            xc = v - mu
            rstd = lax.rsqrt(jnp.mean(xc * xc, axis=-1, keepdims=True) + EPS)
            vn = ((xc * rstd) * lg_ref[...] + lb_ref[...]).astype(BF16)
            for g in range(NG):
                cols = slice(g * CH, (g + 1) * CH)
                s = _dot(wm[g], vn[:, cols]) + bst_ref[:, g:g + 1]
                o_ref[rows, cols] = (u[:, cols] * s).astype(BF16)

    return pl.pallas_call(
        body, name="gmlp_fwd", grid=(t // tm,),
        in_specs=[pl.BlockSpec((tm, BW), lambda i: (i, 0)),
                  pl.BlockSpec((tm, BW), lambda i: (i, 1)),
                  pl.BlockSpec((1, BW), lambda i: (0, 0)),
                  pl.BlockSpec((1, BW), lambda i: (0, 0)),
                  pl.BlockSpec((NG, CH, CH), lambda i: (0, 0, 0)),
                  pl.BlockSpec((CH, NG), lambda i: (0, 0))],
        out_specs=pl.BlockSpec((tm, BW), lambda i: (i, 0)),
        out_shape=jax.ShapeDtypeStruct((t, BW), BF16),
        compiler_params=_cp(("parallel",)),
    )(proj, proj, ln_g, ln_b, ws, bst)


def _gmlp_bwd(proj, dout, ln_g, ln_b, ws, bst, dproj):
    t = proj.shape[0]
    tm = _row_tile(t, 512)

    def body(u_ref, v_ref, do_ref, lg_ref, lb_ref, ws_ref, bst_ref, buf_ref,
             dp_ref, gws_ref, gbs_ref, glg_ref, glb_ref):
        @pl.when(pl.program_id(0) == 0)
        def _():
            gws_ref[...] = jnp.zeros_like(gws_ref)
            gbs_ref[...] = jnp.zeros_like(gbs_ref)
            glg_ref[...] = jnp.zeros_like(glg_ref)
            glb_ref[...] = jnp.zeros_like(glb_ref)

        tril = _tril_mask()
        wm = [jnp.where(tril, ws_ref[g], 0.0).astype(BF16) for g in range(NG)]
        ones = jnp.ones((CH, CH), BF16)
        lg = lg_ref[...]
        for ch in range(tm // CH):
            rows = slice(ch * CH, (ch + 1) * CH)
            u, du_fac = _gelu_and_grad(u_ref[rows, :].astype(F32))
            v, dv_fac = _gelu_and_grad(v_ref[rows, :].astype(F32))
            do = do_ref[rows, :].astype(F32)
            mu = jnp.mean(v, axis=-1, keepdims=True)
            xc = v - mu
            rstd = lax.rsqrt(jnp.mean(xc * xc, axis=-1, keepdims=True) + EPS)
            xh = xc * rstd
            vn = (xh * lg + lb_ref[...]).astype(BF16)
            dvn_parts = []
            for g in range(NG):
                cols = slice(g * CH, (g + 1) * CH)
                s = _dot(wm[g], vn[:, cols]) + bst_ref[:, g:g + 1]
                dp_ref[rows, cols] = (do[:, cols] * s * du_fac[:, cols]).astype(BF16)
                ds = (do[:, cols] * u[:, cols]).astype(BF16)
                gws_ref[g] += jnp.where(tril, _dot(ds, vn[:, cols], NT), 0.0)
                gbs_ref[g] += _dot(ds, ones)
                dvn_parts.append(_dot(wm[g], ds, TN))
            dvn = jnp.concatenate(dvn_parts, axis=1)
            glb_ref[...] += jnp.sum(dvn, axis=0, keepdims=True)
            glg_ref[...] += jnp.sum(dvn * xh, axis=0, keepdims=True)
            dxh = dvn * lg
            dv = rstd * (dxh - jnp.mean(dxh, axis=-1, keepdims=True)
                         - xh * jnp.mean(dxh * xh, axis=-1, keepdims=True))
            dp_ref[rows, BW:2 * BW] = (dv * dv_fac).astype(BF16)

    small = pl.BlockSpec((NG, CH, CH), lambda i: (0, 0, 0))
    vec = pl.BlockSpec((1, BW), lambda i: (0, 0))
    return pl.pallas_call(
        body, name="gmlp_bwd", grid=(t // tm,),
        in_specs=[pl.BlockSpec((tm, BW), lambda i: (i, 0)),
                  pl.BlockSpec((tm, BW), lambda i: (i, 1)),
                  pl.BlockSpec((tm, BW), lambda i: (i, 0)),
                  vec, vec, small, pl.BlockSpec((CH, NG), lambda i: (0, 0)), _HBM],
        out_specs=[pl.BlockSpec((tm, 2 * BW), lambda i: (i, 0)), small, small, vec, vec],
        out_shape=[jax.ShapeDtypeStruct((t, IN_COLS), BF16),
                   jax.ShapeDtypeStruct((NG, CH, CH), F32), jax.ShapeDtypeStruct((NG, CH, CH), F32),
                   jax.ShapeDtypeStruct((1, BW), F32), jax.ShapeDtypeStruct((1, BW), F32)],
        input_output_aliases={7: 0},
        compiler_params=_cp(("arbitrary",)),
    )(proj, proj, dout, ln_g, ln_b, ws, bst, dproj)


def _pool_bands():
    row = lax.broadcasted_iota(jnp.int32, (CH, CH), 0)
    col = lax.broadcasted_iota(jnp.int32, (CH, CH), 1)
    cur, prev = [], []
    for w in POOL_WINDOWS:
        cur.append(jnp.where((row >= col) & (row - col < w), 1.0, 0.0).astype(BF16))
        prev.append(jnp.where(row + CH - col < w, 1.0, 0.0).astype(BF16))
    return cur, prev


def _pool_inv_count(r0, w):
    pos = r0 + lax.broadcasted_iota(jnp.int32, (CH, 1), 0)
    return 1.0 / jnp.minimum(pos + 1, w).astype(F32)


def _pool_diff(x_ref, r0, rp, has_prev, cur, prev, g):
    cols = slice(g * CH, (g + 1) * CH)
    xc = x_ref[pl.ds(r0, CH), cols]
    xp = x_ref[pl.ds(rp, CH), cols]
    ws = _dot(cur[g], xc) + has_prev * _dot(prev[g], xp)
    return ws * _pool_inv_count(r0, POOL_WINDOWS[g]) - xc.astype(F32)


def _pool_fwd(proj3, pw, pscale):
    nb, seq, _ = proj3.shape
    nch = seq // CH

    def body(x_ref, pw_ref, ps_ref, o_ref):
        cur, prev = _pool_bands()
        pwb = [pw_ref[g].astype(BF16) for g in range(NG)]

        def chunk(ch, carry):
            r0 = pl.multiple_of(ch * CH, CH)
            rp = pl.multiple_of(jnp.maximum(ch - 1, 0) * CH, CH)
            has_prev = jnp.where(ch > 0, 1.0, 0.0)
            for g in range(NG):
                cols = slice(g * CH, (g + 1) * CH)
                d = _pool_diff(x_ref, r0, rp, has_prev, cur, prev, g)
                y = _dot(d.astype(BF16), pwb[g]) * ps_ref[:, cols]
                o_ref[pl.ds(r0, CH), cols] = y.astype(BF16)
            return carry

        lax.fori_loop(0, nch, chunk, 0, unroll=2)

    return pl.pallas_call(
        body, name="pool_fwd", grid=(nb,),
        in_specs=[pl.BlockSpec((None, seq, BW), lambda b: (b, 0, 5)),
                  pl.BlockSpec((NG, CH, CH), lambda b: (0, 0, 0)),
                  pl.BlockSpec((1, BW), lambda b: (0, 0))],
        out_specs=pl.BlockSpec((None, seq, BW), lambda b: (b, 0, 0)),
        out_shape=jax.ShapeDtypeStruct((nb, seq, BW), BF16),
        compiler_params=_cp(("parallel",)),
    )(proj3, pw, pscale)


def _pool_bwd(proj3, dout3, pw, pscale):
    nb, seq, _ = proj3.shape
    nch = seq // CH

    def body(x_ref, do_ref, pw_ref, ps_ref, dx_ref, gpw_ref, gps_ref, e_ref):
        @pl.when(pl.program_id(0) == 0)
        def _():
            gpw_ref[...] = jnp.zeros_like(gpw_ref)
            gps_ref[...] = jnp.zeros_like(gps_ref)

        cur, prev = _pool_bands()
        pwb = [pw_ref[g].astype(BF16) for g in range(NG)]

        def first(ch, carry):
            r0 = pl.multiple_of(ch * CH, CH)
            rp = pl.multiple_of(jnp.maximum(ch - 1, 0) * CH, CH)
            has_prev = jnp.where(ch > 0, 1.0, 0.0)
            for g in range(NG):
                cols = slice(g * CH, (g + 1) * CH)
                d = _pool_diff(x_ref, r0, rp, has_prev, cur, prev, g).astype(BF16)
                do = do_ref[pl.ds(r0, CH), cols].astype(F32)
                ypre = _dot(d, pwb[g])
                gps_ref[:, cols] += jnp.sum(do * ypre, axis=0, keepdims=True)
                dyp = (do * ps_ref[:, cols]).astype(BF16)
                gpw_ref[g] += _dot(d, dyp, TN)
                e_ref[pl.ds(r0, CH), cols] = _dot(dyp, pwb[g], NT)
            return carry

        lax.fori_loop(0, nch, first, 0, unroll=2)

        def second(ch, carry):
            r0 = pl.multiple_of(ch * CH, CH)
            rn = pl.multiple_of(jnp.minimum(ch + 1, nch - 1) * CH, CH)
            has_next = jnp.where(ch < nch - 1, 1.0, 0.0)
            for g in range(NG):
                cols = slice(g * CH, (g + 1) * CH)
                w = POOL_WINDOWS[g]
                dd = e_ref[pl.ds(r0, CH), cols]
                ec = (dd * _pool_inv_count(r0, w)).astype(BF16)
                en = (e_ref[pl.ds(rn, CH), cols] * _pool_inv_count(rn, w)).astype(BF16)
                dx = _dot(cur[g], ec, TN) + has_next * _dot(prev[g], en, TN) - dd
                dx_ref[pl.ds(r0, CH), cols] = dx.astype(BF16)
            return carry

        lax.fori_loop(0, nch, second, 0, unroll=2)

    small = pl.BlockSpec((NG, CH, CH), lambda b: (0, 0, 0))
    vec = pl.BlockSpec((1, BW), lambda b: (0, 0))
    return pl.pallas_call(
        body, name="pool_bwd", grid=(nb,),
        in_specs=[pl.BlockSpec((None, seq, BW), lambda b: (b, 0, 5)),
                  pl.BlockSpec((None, seq, BW), lambda b: (b, 0, 0)), small, vec],
        out_specs=[pl.BlockSpec((None, seq, BW), lambda b: (b, 0, 0)), small, vec],
        out_shape=[jax.ShapeDtypeStruct((nb, seq, BW), BF16),
                   jax.ShapeDtypeStruct((NG, CH, CH), F32), jax.ShapeDtypeStruct((1, BW), F32)],
        scratch_shapes=[pltpu.VMEM((seq, BW), F32)],
        compiler_params=_cp(("arbitrary",)),
    )(proj3, dout3, pw, pscale)


SB_BQ = 256
SB_BK = 256
SB_SCALE = HD ** -0.5


SB_EXIT = -110.0


def _sb_tile(qs, k, mask):
    z = _dot(qs, k, NT)
    lb = jnp.minimum(z, 0.0) - jnp.log(1.0 + jnp.exp(-jnp.abs(z)))
    lom = lb - z
    if mask is not None:
        lom = jnp.where(mask, lom, 0.0)
    return lb, lom


def _sb_alive(c):
    top = functools.reduce(jnp.maximum, [jnp.max(state[1]) for state in c])
    return (top > SB_EXIT).astype(jnp.int32)


def _sb_past_blocks(step, c, npast):
    def cond(s):
        return jnp.logical_and(s[0] < npast, s[1] > 0)

    def body(s):
        i, _, c = s
        c = step(pl.multiple_of((npast - 1 - i) * SB_BK, SB_BK), c, None)
        return i + 1, _sb_alive(c), c

    return lax.while_loop(cond, body, (jnp.int32(0), _sb_alive(c), c))[2]


def _sb_diag_mask(bq, d):
    row = lax.broadcasted_iota(jnp.int32, (bq, SB_BK), 0)
    col = lax.broadcasted_iota(jnp.int32, (bq, SB_BK), 1)
    return col + d * SB_BK < row


def _sb_scaled(q):
    return (q.astype(F32) * SB_SCALE).astype(BF16)


def _dot_tri(a, m):
    return _dot(a.astype(BF16), m)


def _dot_tri2(a, m):
    hi = a.astype(BF16)
    lo = (a - hi.astype(F32)).astype(BF16)
    return _dot(hi, m) + _dot(lo, m)


def _sb_fwd(proj3, gather=()):
    nb, seq, _ = proj3.shape
    bq = min(SB_BQ, seq)
    nq = seq // bq
    ndiag = bq // SB_BK

    def body(q_ref, k_ref, v_ref, o_ref):
        row = lax.broadcasted_iota(jnp.int32, (SB_BK, SB_BK), 0)
        col = lax.broadcasted_iota(jnp.int32, (SB_BK, SB_BK), 1)
        upper = jnp.where(row > col, 1.0, 0.0).astype(BF16)
        heads = [slice(hh * HD, (hh + 1) * HD) for hh in range(2)]

        def qloop(qi, carry):
            q0 = pl.multiple_of(qi * bq, bq)
            qs = [_sb_scaled(q_ref[pl.ds(q0, bq), lanes]) for lanes in heads]

            def step(k0, c, mask):
                tiles = [_sb_tile(q, k_ref[pl.ds(k0, SB_BK), lanes], mask) for lanes, q in zip(heads, qs)]
                sums = [_dot_tri(lom, upper) for _, lom in tiles]
                out = []
                for lanes, (acc, cr), (lb, lom), cs in zip(heads, c, tiles, sums):
                    a = jnp.exp(lb + (cs + cr))
                    if mask is not None:
                        a = jnp.where(mask, a, 0.0)
                    rsum = cs[:, 0:1] + lom[:, 0:1]
                    out.append((acc + _dot(a.astype(BF16), v_ref[pl.ds(k0, SB_BK), lanes]), cr + rsum))
                return tuple(out)

            c = tuple((jnp.zeros((bq, HD), F32), jnp.zeros((bq, 1), F32)) for _ in heads)
            for d in reversed(range(ndiag)):
                c = step(pl.multiple_of(q0 + d * SB_BK, SB_BK), c, _sb_diag_mask(bq, d))
            c = _sb_past_blocks(step, c, qi * ndiag)
            for lanes, (acc, _) in zip(heads, c):
                o_ref[pl.ds(q0, bq), lanes] = acc
            return carry

        lax.fori_loop(0, nq, qloop, 0)

    def spec(c0):
        return pl.BlockSpec((None, seq, 128), lambda b, p: (b, 0, c0 + p))

    grid = (nb, BW // 128)
    body, ex_in, ex_out, ex_shape, ex_sems = _host_exchange(body, 3, 1, grid, gather, False)
    outs = pl.pallas_call(
        body, name="sb_fwd", grid=grid,
        in_specs=[spec(8), spec(12), spec(16)] + ex_in,
        out_specs=[spec(0)] + ex_out,
        out_shape=[jax.ShapeDtypeStruct((nb, seq, BW), F32)] + ex_shape,
        scratch_shapes=ex_sems,
        compiler_params=_cp(("arbitrary", "arbitrary")),
    )(proj3, proj3, proj3, *gather)
    return outs[0], outs[1:]


def _sb_bwd(proj3, do3, o3, scatter=()):
    nb, seq, _ = proj3.shape
    bq = min(SB_BQ, seq)
    nq = seq // bq
    ndiag = bq // SB_BK

    def body(q_ref, k_ref, v_ref, do_ref, o_ref, dq_ref, dk_ref, dv_ref, dk_acc, dv_acc):
        row = lax.broadcasted_iota(jnp.int32, (SB_BK, SB_BK), 0)
        col = lax.broadcasted_iota(jnp.int32, (SB_BK, SB_BK), 1)
        upper = jnp.where(row > col, 1.0, 0.0).astype(BF16)
        later = jnp.where(row >= col, 1.0, 0.0).astype(BF16)
        dk_acc[...] = jnp.zeros_like(dk_acc)
        dv_acc[...] = jnp.zeros_like(dv_acc)
        heads = [slice(hh * HD, (hh + 1) * HD) for hh in range(2)]

        def qloop(qi, carry):
            q0 = pl.multiple_of(qi * bq, bq)
            qs = [_sb_scaled(q_ref[pl.ds(q0, bq), lanes]) for lanes in heads]
            dos = [do_ref[pl.ds(q0, bq), lanes] for lanes in heads]
            gtot = [jnp.sum(do.astype(F32) * o_ref[pl.ds(q0, bq), lanes], axis=1, keepdims=True)
                    for do, lanes in zip(dos, heads)]

            def step(k0, c, mask):
                ks = [k_ref[pl.ds(k0, SB_BK), lanes] for lanes in heads]
                tiles = [_sb_tile(q, k, mask) for q, k in zip(qs, ks)]
                sums = [_dot_tri(lom, upper) for _, lom in tiles]
                das = [_dot(do, v_ref[pl.ds(k0, SB_BK), lanes], NT) for do, lanes in zip(dos, heads)]
                gls, avs = [], []
                for hh, (_, cr, _) in enumerate(c):
                    a = jnp.exp(tiles[hh][0] + (sums[hh] + cr))
                    if mask is not None:
                        a = jnp.where(mask, a, 0.0)
                    ab = a.astype(BF16)
                    avs.append(ab)
                    gls.append(das[hh] * ab.astype(F32))
                tails = [_dot_tri2(gl, later) for gl in gls]
                out = []
                for hh, (dq, cr, gdone) in enumerate(c):
                    lb, lom = tiles[hh]
                    pre = gtot[hh] - gdone - tails[hh]
                    dz = gls[hh] - jnp.exp(lb) * (gls[hh] + pre)
                    if mask is not None:
                        dz = jnp.where(mask, dz, 0.0)
                    dz = dz.astype(BF16)
                    dk_acc[hh, pl.ds(k0, SB_BK), :] += _dot(dz, qs[hh], TN)
                    dv_acc[hh, pl.ds(k0, SB_BK), :] += _dot(avs[hh], dos[hh], TN)
                    rsum = sums[hh][:, 0:1] + lom[:, 0:1]
                    out.append((dq + _dot(dz, ks[hh]), cr + rsum, gdone + tails[hh][:, 0:1]))
                return tuple(out)

            c = tuple((jnp.zeros((bq, HD), F32), jnp.zeros((bq, 1), F32), jnp.zeros((bq, 1), F32))
                      for _ in heads)
            for d in reversed(range(ndiag)):
                c = step(pl.multiple_of(q0 + d * SB_BK, SB_BK), c, _sb_diag_mask(bq, d))
            c = _sb_past_blocks(step, c, qi * ndiag)
            for lanes, (dq, _, _) in zip(heads, c):
                dq_ref[pl.ds(q0, bq), lanes] = (dq * SB_SCALE).astype(BF16)
            return carry

        lax.fori_loop(0, nq, qloop, 0)
        for hh in range(2):
            lanes = slice(hh * HD, (hh + 1) * HD)
            dk_ref[:, lanes] = dk_acc[hh].astype(BF16)
            dv_ref[:, lanes] = dv_acc[hh].astype(BF16)

    def spec(c0):
        return pl.BlockSpec((None, seq, 128), lambda b, p: (b, 0, c0 + p))

    grid = (nb, BW // 128)
    body, ex_in, ex_out, ex_shape, ex_sems = _host_exchange(body, 5, 3, grid, scatter, True)
    outs = pl.pallas_call(
        body, name="sb_bwd", grid=grid,
        in_specs=[spec(8), spec(12), spec(16), spec(0), spec(0)] + ex_in,
        out_specs=[spec(0), spec(0), spec(0)] + ex_out,
        out_shape=[jax.ShapeDtypeStruct((nb, seq, BW), BF16)] * 3 + ex_shape,
        scratch_shapes=[pltpu.VMEM((2, seq, HD), F32), pltpu.VMEM((2, seq, HD), F32)] + ex_sems,
        compiler_params=_cp(("arbitrary", "arbitrary")),
    )(proj3, proj3, proj3, do3, o3, *scatter)
    return outs[:3], outs[3:]


def _merge_fwd(brs, wb, proj):
    t = proj.shape[0]
    tm = _row_tile(t, 512)
    tn = 512
    nj = D // tn

    def body(b0, b1, b2, wb_ref, l0, l1, l2, m_ref, y0, y1, y2):
        acc = None
        for br, n, lg, y_ref in ((b0, 0, l0, y0), (b1, 1, l1, y1), (b2, 2, l2, y2)):
            y = _dot(br[...].astype(BF16), wb_ref[n])
            y_ref[...] = y.astype(BF16)
            term = jax.nn.sigmoid(lg[...].astype(F32)) * y
            acc = term if acc is None else acc + term
        m_ref[...] = acc.astype(BF16)

    def lspec(n):
        return pl.BlockSpec((tm, tn), lambda i, j: (i, (3 * D + n * D) // tn + j))

    tile = pl.BlockSpec((tm, tn), lambda i, j: (i, j))
    bspec = pl.BlockSpec((tm, BW), lambda i, j: (i, 0))
    return pl.pallas_call(
        body, name="merge_fwd", grid=(t // tm, nj),
        in_specs=[bspec, bspec, bspec, pl.BlockSpec((NB, BW, tn), lambda i, j: (0, 0, j)),
                  lspec(0), lspec(1), lspec(2)],
        out_specs=[tile] * 4,
        out_shape=[jax.ShapeDtypeStruct((t, D), BF16)] * 4,
        compiler_params=_cp(("parallel", "parallel")),
    )(brs[0], brs[1], brs[2], wb, proj, proj, proj)


def _merge_bwd(dm, ys, proj):
    t = proj.shape[0]
    tm = _row_tile(t, 512)

    def body(dm_ref, y0, y1, y2, lg_ref, dp_ref, dy0, dy1, dy2):
        dmv = dm_ref[...].astype(F32)
        for n, (y_ref, dy_ref) in enumerate(((y0, dy0), (y1, dy1), (y2, dy2))):
            cols = slice(n * D, (n + 1) * D)
            g = jax.nn.sigmoid(lg_ref[:, cols].astype(F32))
            dp_ref[:, cols] = (dmv * y_ref[...].astype(F32) * g * (1.0 - g)).astype(BF16)
            dy_ref[...] = (dmv * g).astype(BF16)

    tile = pl.BlockSpec((tm, D), lambda i: (i, 0))
    gates = pl.BlockSpec((tm, NB * D), lambda i: (i, 1))
    return pl.pallas_call(
        body, name="merge_bwd", grid=(t // tm,),
        in_specs=[tile] * 4 + [gates],
        out_specs=[gates] + [tile] * 3,
        out_shape=[jax.ShapeDtypeStruct((t, IN_COLS), BF16)] + [jax.ShapeDtypeStruct((t, D), BF16)] * 3,
        compiler_params=_cp(("parallel",)),
    )(dm, ys[0], ys[1], ys[2], proj)


def _adamw_rows(rows):
    if rows <= 512:
        return rows
    return next(tr for tr in (512, 384, 352, 256, 128, 64, 32, 16, 8) if rows % tr == 0)


def _adamw_math(npart, p_ref, w_ref, m_ref, v_ref, g_ref, d_ref, mo_ref, vo_ref):
    c1 = 1.0 - ADAM_B1 ** ADAM_STEP
    c2 = 1.0 - ADAM_B2 ** ADAM_STEP
    g = p_ref[0].astype(F32)
    for p in range(1, npart):
        g = g + p_ref[p].astype(F32)
    mn = ADAM_B1 * m_ref[...] + (1.0 - ADAM_B1) * g
    vn = ADAM_B2 * v_ref[...] + (1.0 - ADAM_B2) * (g * g)
    m_hat = mn / c1
    v_hat = vn / c2
    g_ref[...] = g
    d_ref[...] = -ADAM_LR * (m_hat / (jnp.sqrt(v_hat) + ADAM_EPS) + ADAM_WD * w_ref[...])
    mo_ref[...] = mn
    vo_ref[...] = vn


def _adamw_layer(name, parts, w, m, v, layer, bufs):
    nl, cols = w.shape[0], w.shape[-1]
    rows = int(math.prod(w.shape[1:-1]))
    npart = parts.shape[0]
    tr = _adamw_rows(rows)
    if bufs is None:
        bufs = [lax.empty((nl, rows, cols), F32) for _ in range(4)]

    def body(p_ref, w_ref, m_ref, v_ref, b0, b1, b2, b3, g_ref, d_ref, mo_ref, vo_ref):
        _adamw_math(npart, p_ref, w_ref, m_ref, v_ref, g_ref, d_ref, mo_ref, vo_ref)

    slab = pl.BlockSpec((None, tr, cols), lambda i: (layer, i, 0))
    sds = jax.ShapeDtypeStruct((nl, rows, cols), F32)
    return pl.pallas_call(
        body, name=name, grid=(rows // tr,),
        in_specs=[pl.BlockSpec((npart, tr, cols), lambda i: (0, i, 0)), slab, slab, slab] + [_HBM] * 4,
        out_specs=[slab] * 4, out_shape=[sds] * 4,
        input_output_aliases={4: 0, 5: 1, 6: 2, 7: 3},
        compiler_params=_cp(("parallel",)),
    )(parts.reshape(npart, rows, cols), w.reshape(nl, rows, cols), m.reshape(nl, rows, cols),
      v.reshape(nl, rows, cols), *bufs)


def _adamw_reduce(name, parts, w, m, v):
    shape = w.shape
    cols = shape[-1]
    rows = int(math.prod(shape[:-1])) if len(shape) > 1 else 1
    npart = parts.shape[0]
    tr = _adamw_rows(rows)

    def body(p_ref, w_ref, m_ref, v_ref, g_ref, d_ref, mo_ref, vo_ref):
        _adamw_math(npart, p_ref, w_ref, m_ref, v_ref, g_ref, d_ref, mo_ref, vo_ref)

    tile = pl.BlockSpec((tr, cols), lambda i: (i, 0))
    sds = jax.ShapeDtypeStruct((rows, cols), F32)
    outs = pl.pallas_call(
        body, name=name, grid=(rows // tr,),
        in_specs=[pl.BlockSpec((npart, tr, cols), lambda i: (0, i, 0)), tile, tile, tile],
        out_specs=[tile] * 4, out_shape=[sds] * 4,
        compiler_params=_cp(("parallel",)),
    )(parts.reshape(npart, rows, cols), w.reshape(rows, cols), m.reshape(rows, cols), v.reshape(rows, cols))
    return tuple(o.reshape(shape) for o in outs)


def _pad_ffn_in(w):
    lead = w.shape[:-1]
    w = w.reshape(lead + (2, FF_HALF))
    w = jnp.pad(w, [(0, 0)] * len(lead) + [(0, 0), (0, FF_HALF_PAD - FF_HALF)])
    return w.reshape(lead + (FF_IN_PAD,))


def kernel(x, c, rms_g1, rms_g2, w_ada, b_ada, w_in, gm_ln_g, gm_ln_b, gm_w_spatial, gm_b_spatial, pool_w, pool_scale, w_branch, w_out, w_ffn_in, w_ffn_out, final_g, loss_target, m_rms_g1, m_rms_g2, m_w_ada, m_b_ada, m_w_in, m_gm_ln_g, m_gm_ln_b, m_gm_w_spatial, m_gm_b_spatial, m_pool_w, m_pool_scale, m_w_branch, m_w_out, m_w_ffn_in, m_w_ffn_out, m_final_g, v_rms_g1, v_rms_g2, v_w_ada, v_b_ada, v_w_in, v_gm_ln_g, v_gm_ln_b, v_gm_w_spatial, v_gm_b_spatial, v_pool_w, v_pool_scale, v_w_branch, v_w_out, v_w_ffn_in, v_w_ffn_out, v_final_g):
    nb, seq, _ = x.shape
    nl = w_in.shape[0]
    t = nb * seq
    ntot = NDEV * nb
    me = _my_index()
    assert x.shape[2] == D and w_in.shape[1:] == (D, 768) and w_ffn_in.shape[1:] == (D, FF_IN_SHARD)
    assert seq % CH == 0

    w_ffn_in_p = _pad_ffn_in(w_ffn_in).astype(BF16)
    w_ffn_out_p = jnp.pad(w_ffn_out, ((0, 0), (0, FF_HALF_PAD - FF_HALF), (0, 0))).astype(BF16)
    w_in_b = w_in.astype(BF16)
    w_branch_b = w_branch.astype(BF16)
    w_out_b = w_out.astype(BF16)
    (g_in_next,) = _exchange([w_in_b[0]], "gather_w_in0", False)

    (c_all,) = _exchange([c], "gather_c", False)
    c_all = c_all.reshape(ntot, D)
    b_blk = lax.dynamic_slice_in_dim(b_ada, me * 768, 768, axis=1).reshape(nl, 1, 768)
    mod_blk = _ada_fwd(c_all, w_ada, b_blk)
    (mod_all,) = _exchange([mod_blk], "gather_mod", False)
    mod_all = jnp.transpose(mod_all, (1, 2, 0, 3)).reshape(nl, ntot, NMOD * D)
    mod = lax.dynamic_slice_in_dim(mod_all, me * nb, nb, axis=1).reshape(nl, nb, NMOD, 1, D)

    saved = []
    gathered = []
    xc = x
    for l in range(nl):
        sh1, sc1, gt1, sh2, sc2, gt2 = [mod[l, :, i] for i in range(NMOD)]
        h = _norm_mod_fwd(xc, rms_g1[l].reshape(1, D), sc1, sh1).reshape(t, D)
        proj, (g_ffn_in_w,) = _mm_colblocked("proj_fwd", h, g_in_next, BF16, [w_ffn_in_p[l]])
        proj3 = proj.reshape(nb, seq, IN_COLS)
        br_gm = _gmlp_fwd(proj, gm_ln_g[l].reshape(1, BW), gm_ln_b[l].reshape(1, BW),
                          gm_w_spatial[l], gm_b_spatial[l].T)
        sb_o, got = _sb_fwd(proj3, [w_branch_b[l], w_out_b[l]] + ([w_in_b[l + 1]] if l + 1 < nl else []))
        gw = dict(w_in=g_in_next,
                  w_branch=jnp.transpose(got[0], (1, 2, 0, 3)).reshape(NB, BW, D),
                  w_out=got[1].reshape(D, D),
                  w_ffn_in=g_ffn_in_w)
        gathered.append(gw)
        if l + 1 < nl:
            g_in_next = got[2]
        br_pool = _pool_fwd(proj3, pool_w[l], pool_scale[l].reshape(1, BW))
        brs = [br_gm, sb_o.reshape(t, BW), br_pool.reshape(t, BW)]
        merged, y0, y1, y2 = _merge_fwd(brs, gw["w_branch"], proj)
        x_mid, mo = _mm_residual("out_fwd", merged, gw["w_out"], xc.reshape(t, D), gt1, seq)
        x_mid = x_mid.reshape(nb, seq, D)
        h2 = _norm_mod_fwd(x_mid, rms_g2[l].reshape(1, D), sc2, sh2).reshape(t, D)
        fg, fu, act, got = _ffn_in_fwd(h2, gw["w_ffn_in"], [w_ffn_out_p[l]])
        gw["w_ffn_out"] = got[0].reshape(FFP, D)
        x_out, fo = _mm_residual("ffn_out_fwd", act, gw["w_ffn_out"], x_mid.reshape(t, D), gt2, seq)
        saved.append(dict(x_in=xc, h=h, proj=proj, brs=brs, sb_o=sb_o, ys=(y0, y1, y2), merged=merged,
                          mo=mo, x_mid=x_mid, h2=h2, fg=fg, fu=fu, act=act, fo=fo))
        xc = x_out.reshape(nb, seq, D)

    dx, loss_part, dfinal_part, dfo, dgt2 = _loss_head(xc, loss_target, final_g.reshape(1, D),
                                                       saved[-1]["fo"].reshape(nb, seq, D), mod[nl - 1, :, 5])
    loss = lax.psum(jnp.sum(loss_part[:, 0, 0]), ("x", "y", "c"))

    big_names = ("w_in", "w_branch", "w_out", "w_ffn_in", "w_ffn_out")
    bufs = {name: None for name in big_names}
    w_ffn_in_t, m_w_ffn_in_t, v_w_ffn_in_t = [jnp.swapaxes(a, 1, 2) for a in (w_ffn_in, m_w_ffn_in, v_w_ffn_in)]
    small_parts = {k: [None] * nl for k in ("rms_g1", "rms_g2", "gm_ln_g", "gm_ln_b", "gm_w_spatial",
                                            "gm_b_spatial", "pool_w", "pool_scale")}
    dmod = [None] * nl
    for l in reversed(range(nl)):
        gw = gathered[l]
        sv = saved[l]
        sh1, sc1, gt1, sh2, sc2, gt2 = [mod[l, :, i] for i in range(NMOD)]
        dfo = dfo.reshape(t, D)
        g_ffn_out = _mm_tn("ffn_out_wgrad", sv["act"], dfo)
        dfg, dfu = _ffn_out_dgrad(dfo, gw["w_ffn_out"], sv["fg"], sv["fu"])
        dh2 = _ffn_in_dgrad(dfg, dfu, gw["w_ffn_in"])
        g_ffn_in = _ffn_in_wgrad(sv["h2"], dfg, dfu)
        dx_mid, dsh2, dsc2, dg2, dmo, dgt1 = _norm_mod_bwd(
            sv["x_mid"], dh2.reshape(nb, seq, D), dx, rms_g2[l].reshape(1, D), sc2,
            gate=(sv["mo"].reshape(nb, seq, D), gt1))
        dmo = dmo.reshape(t, D)
        dmerged = _mm_nt("out_dgrad", dmo, gw["w_out"], BF16)
        g_out = _mm_tn("out_wgrad", sv["merged"], dmo)
        dproj, *dys = _merge_bwd(dmerged, sv["ys"], sv["proj"])
        dbrs, g_br = [], []
        for n in range(NB):
            dbrs.append(_mm_nt("branch_dgrad", dys[n], gw["w_branch"], BF16, w_lead=n))
            g_br.append(_mm_tn("branch_wgrad", sv["brs"][n], dys[n]))
        proj3 = sv["proj"].reshape(nb, seq, IN_COLS)
        dproj, g_ws, g_bs, g_lg, g_lb = _gmlp_bwd(sv["proj"], dbrs[0], gm_ln_g[l].reshape(1, BW),
                                                  gm_ln_b[l].reshape(1, BW), gm_w_spatial[l], gm_b_spatial[l].T,
                                                  dproj)
        g_br_dev = jnp.transpose(jnp.stack(g_br).reshape(NB, BW, NDEV, D // NDEV), (2, 0, 1, 3))
        carried = [g_br_dev, g_out.reshape(NDEV, D // NDEV, D), g_ffn_in, g_ffn_out.reshape(NDEV, FF_HALF_PAD, D)]
        d_sb, recv = _sb_bwd(proj3, dbrs[1].reshape(nb, seq, BW), sv["sb_o"], carried)
        bufs["w_branch"] = _adamw_layer("adamw_w_branch", recv[0], w_branch, m_w_branch, v_w_branch, l,
                                        bufs["w_branch"])
        bufs["w_out"] = _adamw_layer("adamw_w_out", recv[1], w_out, m_w_out, v_w_out, l, bufs["w_out"])
        r_fi = recv[2].reshape(NDEV, 2, FF_HALF_PAD, D)[:, :, :FF_HALF].reshape(NDEV, FF_IN_SHARD, D)
        bufs["w_ffn_in"] = _adamw_layer("adamw_w_ffn_in", r_fi, w_ffn_in_t, m_w_ffn_in_t, v_w_ffn_in_t, l,
                                        bufs["w_ffn_in"])
        bufs["w_ffn_out"] = _adamw_layer("adamw_w_ffn_out", recv[3][:, :FF_HALF], w_ffn_out, m_w_ffn_out,
                                         v_w_ffn_out, l, bufs["w_ffn_out"])
        d_pool, g_pw, g_ps = _pool_bwd(proj3, dbrs[2].reshape(nb, seq, BW), pool_w[l], pool_scale[l].reshape(1, BW))
        for i, piece in enumerate(list(d_sb) + [d_pool]):
            dproj = lax.dynamic_update_slice(dproj, piece.reshape(t, BW), (0, 2 * BW + i * BW))
        g_in = _mm_colblocked_tn("proj_wgrad", sv["h"], dproj)
        dh, (r_in,) = _mm_colblocked_nt("proj_dgrad", dproj, gw["w_in"], F32, [g_in])
        bufs["w_in"] = _adamw_layer("adamw_w_in", r_in, w_in, m_w_in, v_w_in, l, bufs["w_in"])
        dmod_tail = [dgt1, dsh2, dsc2, dgt2]
        if l > 0:
            dx, dsh1, dsc1, dg1, dfo, dgt2 = _norm_mod_bwd(
                sv["x_in"], dh.reshape(nb, seq, D), dx_mid, rms_g1[l].reshape(1, D), sc1,
                gate=(saved[l - 1]["fo"].reshape(nb, seq, D), mod[l - 1, :, 5]))
        else:
            dx, dsh1, dsc1, dg1 = _norm_mod_bwd(sv["x_in"], dh.reshape(nb, seq, D), dx_mid,
                                                rms_g1[l].reshape(1, D), sc1)

        dmod[l] = jnp.concatenate([dsh1, dsc1] + dmod_tail, axis=-1)
        small_parts["rms_g1"][l] = jnp.sum(dg1, axis=0)
        small_parts["rms_g2"][l] = jnp.sum(dg2, axis=0)
        small_parts["gm_ln_g"][l] = g_lg
        small_parts["gm_ln_b"][l] = g_lb
        small_parts["gm_w_spatial"][l] = g_ws
        small_parts["gm_b_spatial"][l] = g_bs[:, :, 0]
        small_parts["pool_w"][l] = g_pw
        small_parts["pool_scale"][l] = g_ps

    dmod_mine = jnp.stack(dmod).reshape(nl, nb, NMOD * D)
    names = list(small_parts)
    stacked = [jnp.stack(small_parts[k]).astype(BF16 if k in ("gm_w_spatial", "pool_w") else F32) for k in names]
    gathered_small = _exchange(stacked + [dfinal_part, dmod_mine], "gather_small", False)
    dmod_all = jnp.transpose(gathered_small[-1], (1, 0, 2, 3)).reshape(nl, ntot, NMOD * D)
    dfinal_all = gathered_small[-2].reshape(ntot, D)

    results = {}
    weights = dict(rms_g1=(rms_g1, m_rms_g1, v_rms_g1), rms_g2=(rms_g2, m_rms_g2, v_rms_g2),
                   gm_ln_g=(gm_ln_g, m_gm_ln_g, v_gm_ln_g), gm_ln_b=(gm_ln_b, m_gm_ln_b, v_gm_ln_b),
                   gm_w_spatial=(gm_w_spatial, m_gm_w_spatial, v_gm_w_spatial),
                   gm_b_spatial=(gm_b_spatial, m_gm_b_spatial, v_gm_b_spatial),
                   pool_w=(pool_w, m_pool_w, v_pool_w), pool_scale=(pool_scale, m_pool_scale, v_pool_scale))
    for k, parts in zip(names, gathered_small[:len(names)]):
        w, m, v = weights[k]
        results[k] = _adamw_reduce("adamw_" + k, parts.reshape((NDEV,) + w.shape), w, m, v)
    results["final_g"] = _adamw_reduce("adamw_final_g", dfinal_all, final_g, m_final_g, v_final_g)
    results["b_ada"] = _adamw_reduce("adamw_b_ada", jnp.transpose(dmod_all, (1, 0, 2)), b_ada, m_b_ada, v_b_ada)
    dmod_blk = lax.dynamic_slice_in_dim(dmod_all, me * 768, 768, axis=2)
    g_w_ada = _ada_bwd(c_all, dmod_blk)
    results["w_ada"] = _adamw_reduce("adamw_w_ada", g_w_ada[None], w_ada, m_w_ada, v_w_ada)
    stacked_w = dict(w_in=w_in, w_branch=w_branch, w_out=w_out, w_ffn_in=w_ffn_in_t, w_ffn_out=w_ffn_out)
    for name in big_names:
        results[name] = tuple(b.reshape(stacked_w[name].shape) for b in bufs[name])
    results["w_ffn_in"] = tuple(jnp.swapaxes(b, 1, 2) for b in results["w_ffn_in"])

    order = ["rms_g1", "rms_g2", "w_ada", "b_ada", "w_in", "gm_ln_g", "gm_ln_b", "gm_w_spatial", "gm_b_spatial",
             "pool_w", "pool_scale", "w_branch", "w_out", "w_ffn_in", "w_ffn_out", "final_g"]
    out = [loss, dx]
    for i in range(4):
        out.extend(results[k][i] for k in order)
    return tuple(out)
```

```python
import functools
import math

import jax
import jax.numpy as jnp
from jax import lax
from jax.experimental import pallas as pl
from jax.experimental.pallas import tpu as pltpu

F32 = jnp.float32
BF16 = jnp.bfloat16
MESH = pl.DeviceIdType.MESH

D = 1024
BW = 512
NB = 3
CH = 128
NG = 4
HD = 64
POOL_WINDOWS = (2, 4, 8, 16)
DFF = 2816
NMOD = 6
EPS = 1e-6
IN_COLS = 6 * D
NDEV = 8
FF_IN_SHARD = 2 * DFF // NDEV
FF_HALF = FF_IN_SHARD // 2
FF_HALF_PAD = 384
FF_IN_PAD = 2 * FF_HALF_PAD
FFP = NDEV // 2 * FF_IN_PAD

ADAM_LR = 0.001
ADAM_B1 = 0.9
ADAM_B2 = 0.999
ADAM_EPS = 1e-08
ADAM_WD = 0.01
ADAM_STEP = 10

VMEM_LIMIT = 48 * 1024 * 1024
BIG_ROWS = 2048
NN = (((1,), (0,)), ((), ()))
NT = (((1,), (1,)), ((), ()))
TN = (((0,), (0,)), ((), ()))


def _cp(sem=None):
    return pltpu.CompilerParams(dimension_semantics=sem, vmem_limit_bytes=VMEM_LIMIT)


def _dot(a, b, dims=NN):
    return lax.dot_general(a, b, dims, preferred_element_type=F32)


def _my_index():
    return 4 * lax.axis_index("x") + 2 * lax.axis_index("y") + lax.axis_index("c")


def _peer(k):
    x, y, c = lax.axis_index("x"), lax.axis_index("y"), lax.axis_index("c")
    px = 1 - x if k & 4 else x
    py = 1 - y if k & 2 else y
    pc = 1 - c if k & 1 else c
    return (px, py, pc), 4 * px + 2 * py + pc


def _exchange(xs, name, all_to_all):
    n = len(xs)

    def body(*refs):
        _exchange_start(refs[:n], refs[n:2 * n], refs[2 * n:], all_to_all)
        _exchange_relay(refs[:n], refs[n:2 * n], refs[2 * n:], all_to_all)
        _exchange_finish(refs[:n], refs[n:2 * n], refs[2 * n:], all_to_all)

    return pl.pallas_call(
        body, name=name, out_shape=_exchange_out_shape(xs, all_to_all),
        in_specs=[_HBM] * n, out_specs=[_HBM] * n, scratch_shapes=_exchange_sems(n),
    )(*xs)


_HBM = pl.BlockSpec(memory_space=pl.ANY)


def _exchange_out_shape(xs, all_to_all):
    if all_to_all:
        return [jax.ShapeDtypeStruct(x.shape, x.dtype) for x in xs]
    return [jax.ShapeDtypeStruct((NDEV,) + x.shape, x.dtype) for x in xs]


def _exchange_sems(n):
    return [pltpu.SemaphoreType.DMA((n * 7,)), pltpu.SemaphoreType.DMA((n * 7,)), pltpu.SemaphoreType.DMA((n,))]


def _all_to_all_copies(ins, outs, sems):
    send_sems, recv_sems, local_sems = sems
    me = _my_index()
    local, sends, recvs = [], [], []
    for a in range(len(ins)):
        local.append(pltpu.make_async_copy(ins[a].at[me], outs[a].at[me], local_sems.at[a]))
    for k in range(1, NDEV):
        dev, idx = _peer(k)
        for a in range(len(ins)):
            sem = dict(send_sem=send_sems.at[a * 7 + k - 1], recv_sem=recv_sems.at[a * 7 + k - 1],
                       device_id=dev, device_id_type=MESH)
            sends.append(pltpu.make_async_remote_copy(src_ref=ins[a].at[idx], dst_ref=outs[a].at[me], **sem))
            recvs.append(pltpu.make_async_remote_copy(src_ref=ins[a].at[idx], dst_ref=outs[a].at[idx], **sem))
    return local, sends, recvs


def _gather_copies(ins, outs, sems):
    send_sems, recv_sems, local_sems = sems
    x, y, c = lax.axis_index("x"), lax.axis_index("y"), lax.axis_index("c")
    me, other = 4 * x + 2 * y + c, 4 * x + 2 * y + (1 - c)
    other_dev = (x, y, 1 - c)
    chips = [(1 - x, y), (x, 1 - y), (1 - x, 1 - y)]
    local, own, relay, from_other = [], [], [], []
    for a in range(len(ins)):
        def copy(k, src, block, dev, a=a):
            return pltpu.make_async_remote_copy(
                src_ref=src, dst_ref=outs[a].at[block], send_sem=send_sems.at[a * 7 + k],
                recv_sem=recv_sems.at[a * 7 + k], device_id=dev, device_id_type=MESH)

        local.append(pltpu.make_async_copy(ins[a], outs[a].at[me], local_sems.at[a]))
        own.append(copy(0, ins[a], me, other_dev))
        from_other.append(copy(0, ins[a], other, other_dev))
        for j, (px, py) in enumerate(chips):
            far = 4 * px + 2 * py + c
            own.append(copy(1 + j, ins[a], me, (px, py, c)))
            relay.append((copy(1 + j, ins[a], far, (px, py, c)), copy(4 + j, outs[a].at[far], far, other_dev)))
            from_other.append(copy(4 + j, ins[a], 4 * px + 2 * py + (1 - c), other_dev))
    return local, own, relay, from_other


def _exchange_start(ins, outs, sems, all_to_all):
    local, sends = (_all_to_all_copies if all_to_all else _gather_copies)(ins, outs, sems)[:2]
    for cp in local + sends:
        cp.start()


def _exchange_relay(ins, outs, sems, all_to_all):
    if not all_to_all:
        for arrival, passing_on in _gather_copies(ins, outs, sems)[2]:
            arrival.wait_recv()
            passing_on.start()


def _exchange_finish(ins, outs, sems, all_to_all):
    if all_to_all:
        local, sends, recvs = _all_to_all_copies(ins, outs, sems)
    else:
        local, own, relay, recvs = _gather_copies(ins, outs, sems)
        sends = own + [passing_on for _, passing_on in relay]
    for cp in sends:
        cp.wait_send()
    for cp in recvs:
        cp.wait_recv()
    for cp in local:
        cp.wait()


def _host_exchange(body, n_in, n_out, grid, xs, all_to_all):
    n = len(xs)
    if n == 0:
        return body, [], [], [], []
    steps = math.prod(grid)
    half = steps // 2 if steps >= 3 else steps - 1

    def hosted(*refs):
        ins, ex_ins = refs[:n_in], refs[n_in:n_in + n]
        outs, ex_outs = refs[n_in + n:n_in + n + n_out], refs[n_in + n + n_out:n_in + 2 * n + n_out]
        scratch = refs[n_in + 2 * n + n_out:]
        own, sems = scratch[:len(scratch) - 3], scratch[len(scratch) - 3:]
        step = 0
        for a in range(len(grid)):
            step = step * grid[a] + pl.program_id(a)

        @pl.when(step == 0)
        def _():
            _exchange_start(ex_ins, ex_outs, sems, all_to_all)

        body(*ins, *outs, *own)

        @pl.when(step == half)
        def _():
            _exchange_relay(ex_ins, ex_outs, sems, all_to_all)

        @pl.when(step == steps - 1)
        def _():
            _exchange_finish(ex_ins, ex_outs, sems, all_to_all)

    return hosted, [_HBM] * n, [_HBM] * n, _exchange_out_shape(xs, all_to_all), _exchange_sems(n)


def _mm(name, a, b, grid, a_spec, b_spec, o_spec, out_sds, dims, acc_shape, carried=(), all_to_all=True):
    nk = grid[2]

    if nk == 1:
        def body(a_ref, b_ref, o_ref):
            o_ref[...] = _dot(a_ref[...].astype(BF16), b_ref[...].astype(BF16), dims).astype(o_ref.dtype)
        scratch = []
    else:
        def body(a_ref, b_ref, o_ref, acc_ref):
            k = pl.program_id(2)

            @pl.when(k == 0)
            def _():
                acc_ref[...] = jnp.zeros_like(acc_ref)

            acc_ref[...] += _dot(a_ref[...].astype(BF16), b_ref[...].astype(BF16), dims)

            @pl.when(k == nk - 1)
            def _():
                o_ref[...] = acc_ref[...].astype(o_ref.dtype)
        scratch = [pltpu.VMEM(acc_shape, F32)]

    body, ex_in, ex_out, ex_shape, ex_sems = _host_exchange(body, 2, 1, grid, carried, all_to_all)
    outs = pl.pallas_call(
        body, name=name, grid=grid, in_specs=[a_spec, b_spec] + ex_in, out_specs=[o_spec] + ex_out,
        out_shape=[out_sds] + ex_shape,
        scratch_shapes=scratch + ex_sems,
        compiler_params=_cp(("arbitrary",) * 3 if carried else ("parallel", "parallel", "arbitrary")),
    )(a, b, *carried)
    return (outs[0], outs[1:]) if carried else outs[0]


def _row_tile(t, want):
    tm = min(t, want)
    assert t % tm == 0
    return tm


def _mm_colblocked(name, a, wg, out_dtype, gather=()):
    t = a.shape[0]
    tm = _row_tile(t, BIG_ROWS)
    return _mm(name, a, wg, (t // tm, NDEV, 1),
               pl.BlockSpec((tm, D), lambda i, j, k: (i, 0)),
               pl.BlockSpec((None, D, 768), lambda i, j, k: (j, 0, 0)),
               pl.BlockSpec((tm, 768), lambda i, j, k: (i, j)),
               jax.ShapeDtypeStruct((t, NDEV * 768), out_dtype), NN, (tm, 768), gather, False)


def _mm_colblocked_nt(name, g, wg, out_dtype, scatter=()):
    t = g.shape[0]
    tm = _row_tile(t, BIG_ROWS)
    return _mm(name, g, wg, (t // tm, 1, NDEV),
               pl.BlockSpec((tm, 768), lambda i, j, k: (i, k)),
               pl.BlockSpec((None, D, 768), lambda i, j, k: (k, 0, 0)),
               pl.BlockSpec((tm, D), lambda i, j, k: (i, 0)),
               jax.ShapeDtypeStruct((t, D), out_dtype), NT, (tm, D), scatter)


_HALF = NDEV // 2


def _ffn_in_fwd(h2, wg, gather=()):
    t = h2.shape[0]
    tm = _row_tile(t, 1024)

    def body(a_ref, wg_ref, wu_ref, g_ref, u_ref, act_ref):
        a = a_ref[...]
        g = _dot(a, wg_ref[...])
        u = _dot(a, wu_ref[...])
        g_ref[...] = g.astype(BF16)
        u_ref[...] = u.astype(BF16)
        act_ref[...] = (g * jax.nn.sigmoid(g) * u).astype(BF16)

    tile = pl.BlockSpec((tm, 768), lambda i, j: (i, j))
    grid = (t // tm, _HALF)
    body, ex_in, ex_out, ex_shape, ex_sems = _host_exchange(body, 3, 3, grid, gather, False)
    outs = pl.pallas_call(
        body, name="ffn_in_fwd", grid=grid,
        in_specs=[pl.BlockSpec((tm, D), lambda i, j: (i, 0)),
                  pl.BlockSpec((None, D, 768), lambda i, j: (j, 0, 0)),
                  pl.BlockSpec((None, D, 768), lambda i, j: (j + _HALF, 0, 0))] + ex_in,
        out_specs=[tile] * 3 + ex_out, out_shape=[jax.ShapeDtypeStruct((t, FFP), BF16)] * 3 + ex_shape,
        scratch_shapes=ex_sems,
        compiler_params=_cp(("arbitrary", "arbitrary") if gather else ("parallel", "parallel")),
    )(h2, wg, wg, *gather)
    return outs[0], outs[1], outs[2], outs[3:]


def _ffn_out_dgrad(dfo, w, fg, fu):
    t = dfo.shape[0]
    tm = _row_tile(t, 1024)

    def body(a_ref, w_ref, g_ref, u_ref, dg_ref, du_ref):
        d = _dot(a_ref[...], w_ref[...], NT)
        g = g_ref[...].astype(F32)
        s = jax.nn.sigmoid(g)
        gs = g * s
        dg_ref[...] = (d * u_ref[...].astype(F32) * (s + gs * (1.0 - s))).astype(BF16)
        du_ref[...] = (d * gs).astype(BF16)

    tile = pl.BlockSpec((tm, 768), lambda i, j: (i, j))
    return pl.pallas_call(
        body, name="ffn_out_dgrad", grid=(t // tm, _HALF),
        in_specs=[pl.BlockSpec((tm, D), lambda i, j: (i, 0)),
                  pl.BlockSpec((768, D), lambda i, j: (j, 0)), tile, tile],
        out_specs=[tile] * 2, out_shape=[jax.ShapeDtypeStruct((t, FFP), BF16)] * 2,
        compiler_params=_cp(("parallel", "parallel")),
    )(dfo, w, fg, fu)


def _ffn_in_dgrad(dg, du, wg):
    t = dg.shape[0]
    tm = _row_tile(t, BIG_ROWS)

    def body(g_ref, u_ref, w_ref, o_ref, acc_ref):
        k = pl.program_id(1)

        @pl.when(k == 0)
        def _():
            acc_ref[...] = jnp.zeros_like(acc_ref)

        @pl.when(k < _HALF)
        def _():
            acc_ref[...] += _dot(g_ref[...], w_ref[...], NT)

        @pl.when(k >= _HALF)
        def _():
            acc_ref[...] += _dot(u_ref[...], w_ref[...], NT)

        @pl.when(k == NDEV - 1)
        def _():
            o_ref[...] = acc_ref[...]

    return pl.pallas_call(
        body, name="ffn_in_dgrad", grid=(t // tm, NDEV),
        in_specs=[pl.BlockSpec((tm, 768), lambda i, k: (i, jnp.minimum(k, _HALF - 1))),
                  pl.BlockSpec((tm, 768), lambda i, k: (i, jnp.maximum(k - _HALF, 0))),
                  pl.BlockSpec((None, D, 768), lambda i, k: (k, 0, 0))],
        out_specs=pl.BlockSpec((tm, D), lambda i, k: (i, 0)),
        out_shape=jax.ShapeDtypeStruct((t, D), F32),
        scratch_shapes=[pltpu.VMEM((tm, D), F32)],
        compiler_params=_cp(("parallel", "arbitrary")),
    )(dg, du, wg)


def _ffn_in_wgrad(h2, dg, du):
    t = h2.shape[0]
    tk = _row_tile(t, BIG_ROWS)
    nk = t // tk

    def body(a_ref, g_ref, u_ref, o_ref, acc_ref):
        j, k = pl.program_id(0), pl.program_id(1)

        @pl.when(k == 0)
        def _():
            acc_ref[...] = jnp.zeros_like(acc_ref)

        @pl.when(j < _HALF)
        def _():
            acc_ref[...] += _dot(g_ref[...], a_ref[...], TN)

        @pl.when(j >= _HALF)
        def _():
            acc_ref[...] += _dot(u_ref[...], a_ref[...], TN)

        @pl.when(k == nk - 1)
        def _():
            o_ref[...] = acc_ref[...].astype(BF16)

    return pl.pallas_call(
        body, name="ffn_in_wgrad", grid=(NDEV, nk),
        in_specs=[pl.BlockSpec((tk, D), lambda j, k: (k, 0)),
                  pl.BlockSpec((tk, 768), lambda j, k: (jnp.where(j < _HALF, k, 0), jnp.minimum(j, _HALF - 1))),
                  pl.BlockSpec((tk, 768), lambda j, k: (jnp.where(j < _HALF, 0, k), jnp.maximum(j - _HALF, 0)))],
        out_specs=pl.BlockSpec((None, 768, D), lambda j, k: (j, 0, 0)),
        out_shape=jax.ShapeDtypeStruct((NDEV, 768, D), BF16),
        scratch_shapes=[pltpu.VMEM((768, D), F32)],
        compiler_params=_cp(("parallel", "arbitrary")),
    )(h2, dg, du)


def _mm_colblocked_tn(name, a, g):
    t = a.shape[0]
    tk = _row_tile(t, BIG_ROWS)
    return _mm(name, a, g, (1, NDEV, t // tk),
               pl.BlockSpec((tk, D), lambda i, j, k: (k, 0)),
               pl.BlockSpec((tk, 768), lambda i, j, k: (k, j)),
               pl.BlockSpec((None, D, 768), lambda i, j, k: (j, 0, 0)),
               jax.ShapeDtypeStruct((NDEV, D, 768), BF16), TN, (D, 768))


def _mm_nt(name, a, w, out_dtype, a_col=0, w_lead=None):
    t = a.shape[0]
    if w_lead is None:
        kdim, n = w.shape
        b_spec = pl.BlockSpec((min(kdim, 1024), n), lambda i, j, k: (j, 0))
    else:
        _, kdim, n = w.shape
        b_spec = pl.BlockSpec((None, min(kdim, 1024), n), lambda i, j, k: (w_lead, j, 0))
    tn = min(kdim, 1024)
    tm = _row_tile(t, BIG_ROWS)
    return _mm(name, a, w, (t // tm, kdim // tn, 1),
               pl.BlockSpec((tm, n), lambda i, j, k: (i, a_col)),
               b_spec,
               pl.BlockSpec((tm, tn), lambda i, j, k: (i, j)),
               jax.ShapeDtypeStruct((t, kdim), out_dtype), NT, (tm, tn))


def _mm_tn(name, a, g, out_dtype=BF16):
    t, kdim = a.shape
    n = g.shape[1]
    tk = _row_tile(t, BIG_ROWS)
    tm = min(kdim, 1024)
    tn = min(n, 1024)
    return _mm(name, a, g, (kdim // tm, n // tn, t // tk),
               pl.BlockSpec((tk, tm), lambda i, j, k: (k, i)),
               pl.BlockSpec((tk, tn), lambda i, j, k: (k, j)),
               pl.BlockSpec((tm, tn), lambda i, j, k: (i, j)),
               jax.ShapeDtypeStruct((kdim, n), out_dtype), TN, (tm, tn))


def _mm_residual(name, a, w, x, gt, seq):
    t, kdim = a.shape
    tm = _row_tile(seq, 1024)
    tn = D
    tk = min(kdim, 1024)
    nk = kdim // tk
    per = seq // tm

    def body(a_ref, w_ref, x_ref, gt_ref, xo_ref, y_ref, acc_ref):
        k = pl.program_id(2)

        @pl.when(k == 0)
        def _():
            acc_ref[...] = jnp.zeros_like(acc_ref)

        acc_ref[...] += _dot(a_ref[...], w_ref[...])

        @pl.when(k == nk - 1)
        def _():
            y = acc_ref[...]
            xo_ref[...] = x_ref[...] + gt_ref[0] * y
            y_ref[...] = y.astype(BF16)

    return pl.pallas_call(
        body, name=name, grid=(t // tm, D // tn, nk),
        in_specs=[pl.BlockSpec((tm, tk), lambda i, j, k: (i, k)),
                  pl.BlockSpec((tk, tn), lambda i, j, k: (k, j)),
                  pl.BlockSpec((tm, tn), lambda i, j, k: (i, j)),
                  pl.BlockSpec((1, 1, tn), lambda i, j, k: (i // per, 0, j))],
        out_specs=[pl.BlockSpec((tm, tn), lambda i, j, k: (i, j)),
                   pl.BlockSpec((tm, tn), lambda i, j, k: (i, j))],
        out_shape=[jax.ShapeDtypeStruct((t, D), F32), jax.ShapeDtypeStruct((t, D), BF16)],
        scratch_shapes=[pltpu.VMEM((tm, tn), F32)],
        compiler_params=_cp(("parallel", "parallel", "arbitrary")),
    )(a, w, x, gt)


def _ada_fwd(c_all, w_ada, b_blk):
    nl = w_ada.shape[0]
    nb = c_all.shape[0]

    def body(c_ref, w_ref, b_ref, o_ref):
        c = c_ref[...]
        ca = (c * jax.nn.sigmoid(c)).astype(BF16)
        o_ref[...] = _dot(ca, w_ref[...].astype(BF16)) + b_ref[...]

    return pl.pallas_call(
        body, name="ada_fwd", grid=(nl,),
        in_specs=[pl.BlockSpec((nb, D), lambda l: (0, 0)),
                  pl.BlockSpec((None, D, 768), lambda l: (l, 0, 0)),
                  pl.BlockSpec((None, 1, 768), lambda l: (l, 0, 0))],
        out_specs=pl.BlockSpec((None, nb, 768), lambda l: (l, 0, 0)),
        out_shape=jax.ShapeDtypeStruct((nl, nb, 768), F32),
        compiler_params=_cp(("parallel",)),
    )(c_all, w_ada, b_blk)


def _ada_bwd(c_all, dmod_blk):
    nl = dmod_blk.shape[0]
    nb = c_all.shape[0]

    def body(c_ref, d_ref, o_ref):
        c = c_ref[...]
        ca = (c * jax.nn.sigmoid(c)).astype(BF16)
        o_ref[...] = _dot(ca, d_ref[...].astype(BF16), TN)

    return pl.pallas_call(
        body, name="ada_bwd", grid=(nl,),
        in_specs=[pl.BlockSpec((nb, D), lambda l: (0, 0)),
                  pl.BlockSpec((None, nb, 768), lambda l: (l, 0, 0))],
        out_specs=pl.BlockSpec((None, D, 768), lambda l: (l, 0, 0)),
        out_shape=jax.ShapeDtypeStruct((nl, D, 768), F32),
        compiler_params=_cp(("parallel",)),
    )(c_all, dmod_blk)


def _seq_tile(seq):
    return _row_tile(seq, 512)


def _norm_mod_fwd(x, g, sc, sh):
    nb, seq, _ = x.shape
    ts = _seq_tile(seq)

    def body(x_ref, g_ref, sc_ref, sh_ref, h_ref):
        xv = x_ref[0]
        r = lax.rsqrt(jnp.mean(xv * xv, axis=-1, keepdims=True) + EPS)
        h_ref[0] = ((xv * r) * g_ref[...] * (1.0 + sc_ref[0]) + sh_ref[0]).astype(BF16)

    return pl.pallas_call(
        body, name="norm_mod_fwd", grid=(nb, seq // ts),
        in_specs=[pl.BlockSpec((1, ts, D), lambda b, s: (b, s, 0)),
                  pl.BlockSpec((1, D), lambda b, s: (0, 0)),
                  pl.BlockSpec((1, 1, D), lambda b, s: (b, 0, 0)),
                  pl.BlockSpec((1, 1, D), lambda b, s: (b, 0, 0))],
        out_specs=pl.BlockSpec((1, ts, D), lambda b, s: (b, s, 0)),
        out_shape=jax.ShapeDtypeStruct((nb, seq, D), BF16),
        compiler_params=_cp(("parallel", "parallel")),
    )(x, g, sc, sh)


def _gate_bwd_tile(d, y_ref, gt_ref, dy_ref, dgt_ref):
    @pl.when(pl.program_id(1) == 0)
    def _():
        dgt_ref[...] = jnp.zeros_like(dgt_ref)

    dy_ref[0] = (gt_ref[0] * d).astype(BF16)
    dgt_ref[0] += jnp.sum(d * y_ref[0].astype(F32), axis=0, keepdims=True)


def _norm_mod_bwd(x, dh, dres, g, sc, gate=None):
    nb, seq, _ = x.shape
    ts = _seq_tile(seq)

    def body(x_ref, dh_ref, dres_ref, g_ref, sc_ref, *rest):
        if gate is None:
            dx_ref, dsh_ref, dsc_ref, dg_ref = rest
        else:
            y_ref, gt_ref, dx_ref, dsh_ref, dsc_ref, dg_ref, dy_ref, dgt_ref = rest

        @pl.when(pl.program_id(1) == 0)
        def _():
            dsh_ref[...] = jnp.zeros_like(dsh_ref)
            dsc_ref[...] = jnp.zeros_like(dsc_ref)
            dg_ref[...] = jnp.zeros_like(dg_ref)

        xv = x_ref[0]
        dh = dh_ref[0]
        gv = g_ref[...]
        onesc = 1.0 + sc_ref[0]
        r = lax.rsqrt(jnp.mean(xv * xv, axis=-1, keepdims=True) + EPS)
        xh = xv * r
        dsh_ref[0] += jnp.sum(dh, axis=0, keepdims=True)
        dsc_ref[0] += jnp.sum(dh * (xh * gv), axis=0, keepdims=True)
        dg_ref[0] += jnp.sum(dh * onesc * xh, axis=0, keepdims=True)
        dxh = dh * (gv * onesc)
        dx = r * (dxh - xh * jnp.mean(dxh * xh, axis=-1, keepdims=True))
        dx_total = dres_ref[0] + dx
        dx_ref[0] = dx_total
        if gate is not None:
            _gate_bwd_tile(dx_total, y_ref, gt_ref, dy_ref, dgt_ref)

    vec = jax.ShapeDtypeStruct((nb, 1, D), F32)
    vspec = pl.BlockSpec((1, 1, D), lambda b, s: (b, 0, 0))
    tile = pl.BlockSpec((1, ts, D), lambda b, s: (b, s, 0))
    gated = gate is not None
    return pl.pallas_call(
        body, name="norm_mod_bwd", grid=(nb, seq // ts),
        in_specs=[tile, tile, tile, pl.BlockSpec((1, D), lambda b, s: (0, 0)), vspec] + [tile, vspec] * gated,
        out_specs=[tile, vspec, vspec, vspec] + [tile, vspec] * gated,
        out_shape=[jax.ShapeDtypeStruct((nb, seq, D), F32), vec, vec, vec]
        + [jax.ShapeDtypeStruct((nb, seq, D), BF16), vec] * gated,
        compiler_params=_cp(("parallel", "arbitrary")),
    )(x, dh, dres, g, sc, *(gate or ()))


def _loss_head(x, tgt, g, y, gt):
    nb, seq, _ = x.shape
    ts = _seq_tile(seq)

    def body(x_ref, t_ref, g_ref, y_ref, gt_ref, dx_ref, loss_ref, dg_ref, dy_ref, dgt_ref):
        @pl.when(pl.program_id(1) == 0)
        def _():
            loss_ref[...] = jnp.zeros_like(loss_ref)
            dg_ref[...] = jnp.zeros_like(dg_ref)

        xv = x_ref[0]
        gv = g_ref[...]
        r = lax.rsqrt(jnp.mean(xv * xv, axis=-1, keepdims=True) + EPS)
        xh = xv * r
        err = xh * gv - t_ref[0]
        per_tok = jnp.mean(err * err, axis=-1, keepdims=True)
        loss_ref[0] += 0.5 * jnp.sum(per_tok, axis=0, keepdims=True)
        dy = err * (1.0 / D)
        dg_ref[0] += jnp.sum(dy * xh, axis=0, keepdims=True)
        dxh = dy * gv
        dx = r * (dxh - xh * jnp.mean(dxh * xh, axis=-1, keepdims=True))
        dx_ref[0] = dx
        _gate_bwd_tile(dx, y_ref, gt_ref, dy_ref, dgt_ref)

    tile = pl.BlockSpec((1, ts, D), lambda b, s: (b, s, 0))
    vspec = pl.BlockSpec((1, 1, D), lambda b, s: (b, 0, 0))
    vec = jax.ShapeDtypeStruct((nb, 1, D), F32)
    return pl.pallas_call(
        body, name="loss_head", grid=(nb, seq // ts),
        in_specs=[tile, tile, pl.BlockSpec((1, D), lambda b, s: (0, 0)), tile, vspec],
        out_specs=[tile, pl.BlockSpec((1, 1, 128), lambda b, s: (b, 0, 0)), vspec, tile, vspec],
        out_shape=[jax.ShapeDtypeStruct((nb, seq, D), F32), jax.ShapeDtypeStruct((nb, 1, 128), F32), vec,
                   jax.ShapeDtypeStruct((nb, seq, D), BF16), vec],
        compiler_params=_cp(("parallel", "arbitrary")),
    )(x, tgt, g, y, gt)


_GELU_C = math.sqrt(2.0 / math.pi)


def _gelu(x):
    return 0.5 * x * (1.0 + jnp.tanh(_GELU_C * (x + 0.044715 * (x * x * x))))


def _gelu_and_grad(x):
    t = jnp.tanh(_GELU_C * (x + 0.044715 * (x * x * x)))
    y = 0.5 * x * (1.0 + t)
    dy = 0.5 * (1.0 + t) + 0.5 * x * (1.0 - t * t) * (_GELU_C * (1.0 + 3.0 * 0.044715 * (x * x)))
    return y, dy


def _tril_mask():
    row = lax.broadcasted_iota(jnp.int32, (CH, CH), 0)
    col = lax.broadcasted_iota(jnp.int32, (CH, CH), 1)
    return row >= col


def _gmlp_fwd(proj, ln_g, ln_b, ws, bst):
    t = proj.shape[0]
    tm = _row_tile(t, 512)

    def body(u_ref, v_ref, lg_ref, lb_ref, ws_ref, bst_ref, o_ref):
        tril = _tril_mask()
        wm = [jnp.where(tril, ws_ref[g], 0.0).astype(BF16) for g in range(NG)]
        for ch in range(tm // CH):
            rows = slice(ch * CH, (ch + 1) * CH)
            u = _gelu(u_ref[rows, :].astype(F32))
            v = _gelu(v_ref[rows, :].astype(F32))
            mu = jnp.mean(v, axis=-1, keepdims=True)
            xc = v - mu
            rstd = lax.rsqrt(jnp.mean(xc * xc, axis=-1, keepdims=True) + EPS)
            vn = ((xc * rstd) * lg_ref[...] + lb_ref[...]).astype(BF16)
            for g in range(NG):
                cols = slice(g * CH, (g + 1) * CH)
                s = _dot(wm[g], vn[:, cols]) + bst_ref[:, g:g + 1]
                o_ref[rows, cols] = (u[:, cols] * s).astype(BF16)

    return pl.pallas_call(
        body, name="gmlp_fwd", grid=(t // tm,),
        in_specs=[pl.BlockSpec((tm, BW), lambda i: (i, 0)),
                  pl.BlockSpec((tm, BW), lambda i: (i, 1)),
                  pl.BlockSpec((1, BW), lambda i: (0, 0)),
                  pl.BlockSpec((1, BW), lambda i: (0, 0)),
                  pl.BlockSpec((NG, CH, CH), lambda i: (0, 0, 0)),
                  pl.BlockSpec((CH, NG), lambda i: (0, 0))],
        out_specs=pl.BlockSpec((tm, BW), lambda i: (i, 0)),
        out_shape=jax.ShapeDtypeStruct((t, BW), BF16),
        compiler_params=_cp(("parallel",)),
    )(proj, proj, ln_g, ln_b, ws, bst)


def _gmlp_bwd(proj, dout, ln_g, ln_b, ws, bst, dproj):
    t = proj.shape[0]
    tm = _row_tile(t, 512)

    def body(u_ref, v_ref, do_ref, lg_ref, lb_ref, ws_ref, bst_ref, buf_ref,
             dp_ref, gws_ref, gbs_ref, glg_ref, glb_ref):
        @pl.when(pl.program_id(0) == 0)
        def _():
            gws_ref[...] = jnp.zeros_like(gws_ref)
            gbs_ref[...] = jnp.zeros_like(gbs_ref)
            glg_ref[...] = jnp.zeros_like(glg_ref)
            glb_ref[...] = jnp.zeros_like(glb_ref)

        tril = _tril_mask()
        wm = [jnp.where(tril, ws_ref[g], 0.0).astype(BF16) for g in range(NG)]
        ones = jnp.ones((CH, CH), BF16)
        lg = lg_ref[...]
        for ch in range(tm // CH):
            rows = slice(ch * CH, (ch + 1) * CH)
            u, du_fac = _gelu_and_grad(u_ref[rows, :].astype(F32))
            v, dv_fac = _gelu_and_grad(v_ref[rows, :].astype(F32))
            do = do_ref[rows, :].astype(F32)
            mu = jnp.mean(v, axis=-1, keepdims=True)
            xc = v - mu
            rstd = lax.rsqrt(jnp.mean(xc * xc, axis=-1, keepdims=True) + EPS)
            xh = xc * rstd
            vn = (xh * lg + lb_ref[...]).astype(BF16)
            dvn_parts = []
            for g in range(NG):
                cols = slice(g * CH, (g + 1) * CH)
                s = _dot(wm[g], vn[:, cols]) + bst_ref[:, g:g + 1]
                dp_ref[rows, cols] = (do[:, cols] * s * du_fac[:, cols]).astype(BF16)
                ds = (do[:, cols] * u[:, cols]).astype(BF16)
                gws_ref[g] += jnp.where(tril, _dot(ds, vn[:, cols], NT), 0.0)
                gbs_ref[g] += _dot(ds, ones)
                dvn_parts.append(_dot(wm[g], ds, TN))
            dvn = jnp.concatenate(dvn_parts, axis=1)
            glb_ref[...] += jnp.sum(dvn, axis=0, keepdims=True)
            glg_ref[...] += jnp.sum(dvn * xh, axis=0, keepdims=True)
            dxh = dvn * lg
            dv = rstd * (dxh - jnp.mean(dxh, axis=-1, keepdims=True)
                         - xh * jnp.mean(dxh * xh, axis=-1, keepdims=True))
            dp_ref[rows, BW:2 * BW] = (dv * dv_fac).astype(BF16)

    small = pl.BlockSpec((NG, CH, CH), lambda i: (0, 0, 0))
    vec = pl.BlockSpec((1, BW), lambda i: (0, 0))
    return pl.pallas_call(
        body, name="gmlp_bwd", grid=(t // tm,),
        in_specs=[pl.BlockSpec((tm, BW), lambda i: (i, 0)),
                  pl.BlockSpec((tm, BW), lambda i: (i, 1)),
                  pl.BlockSpec((tm, BW), lambda i: (i, 0)),
                  vec, vec, small, pl.BlockSpec((CH, NG), lambda i: (0, 0)), _HBM],
        out_specs=[pl.BlockSpec((tm, 2 * BW), lambda i: (i, 0)), small, small, vec, vec],
        out_shape=[jax.ShapeDtypeStruct((t, IN_COLS), BF16),
                   jax.ShapeDtypeStruct((NG, CH, CH), F32), jax.ShapeDtypeStruct((NG, CH, CH), F32),
                   jax.ShapeDtypeStruct((1, BW), F32), jax.ShapeDtypeStruct((1, BW), F32)],
        input_output_aliases={7: 0},
        compiler_params=_cp(("arbitrary",)),
    )(proj, proj, dout, ln_g, ln_b, ws, bst, dproj)


def _pool_bands():
    row = lax.broadcasted_iota(jnp.int32, (CH, CH), 0)
    col = lax.broadcasted_iota(jnp.int32, (CH, CH), 1)
    cur, prev = [], []
    for w in POOL_WINDOWS:
        cur.append(jnp.where((row >= col) & (row - col < w), 1.0, 0.0).astype(BF16))
        prev.append(jnp.where(row + CH - col < w, 1.0, 0.0).astype(BF16))
    return cur, prev


def _pool_inv_count(r0, w):
    pos = r0 + lax.broadcasted_iota(jnp.int32, (CH, 1), 0)
    return 1.0 / jnp.minimum(pos + 1, w).astype(F32)


def _pool_diff(x_ref, r0, rp, has_prev, cur, prev, g):
    cols = slice(g * CH, (g + 1) * CH)
    xc = x_ref[pl.ds(r0, CH), cols]
    xp = x_ref[pl.ds(rp, CH), cols]
    ws = _dot(cur[g], xc) + has_prev * _dot(prev[g], xp)
    return ws * _pool_inv_count(r0, POOL_WINDOWS[g]) - xc.astype(F32)


def _pool_fwd(proj3, pw, pscale):
    nb, seq, _ = proj3.shape
    nch = seq // CH

    def body(x_ref, pw_ref, ps_ref, o_ref):
        cur, prev = _pool_bands()
        pwb = [pw_ref[g].astype(BF16) for g in range(NG)]

        def chunk(ch, carry):
            r0 = pl.multiple_of(ch * CH, CH)
            rp = pl.multiple_of(jnp.maximum(ch - 1, 0) * CH, CH)
            has_prev = jnp.where(ch > 0, 1.0, 0.0)
            for g in range(NG):
                cols = slice(g * CH, (g + 1) * CH)
                d = _pool_diff(x_ref, r0, rp, has_prev, cur, prev, g)
                y = _dot(d.astype(BF16), pwb[g]) * ps_ref[:, cols]
                o_ref[pl.ds(r0, CH), cols] = y.astype(BF16)
            return carry

        lax.fori_loop(0, nch, chunk, 0, unroll=2)

    return pl.pallas_call(
        body, name="pool_fwd", grid=(nb,),
        in_specs=[pl.BlockSpec((None, seq, BW), lambda b: (b, 0, 5)),
                  pl.BlockSpec((NG, CH, CH), lambda b: (0, 0, 0)),
                  pl.BlockSpec((1, BW), lambda b: (0, 0))],
        out_specs=pl.BlockSpec((None, seq, BW), lambda b: (b, 0, 0)),
        out_shape=jax.ShapeDtypeStruct((nb, seq, BW), BF16),
        compiler_params=_cp(("parallel",)),
    )(proj3, pw, pscale)


def _pool_bwd(proj3, dout3, pw, pscale, dproj3):
    nb, seq, _ = proj3.shape
    nch = seq // CH

    def body(x_ref, do_ref, pw_ref, ps_ref, buf_ref, dx_ref, gpw_ref, gps_ref, e_ref):
        @pl.when(pl.program_id(0) == 0)
        def _():
            gpw_ref[...] = jnp.zeros_like(gpw_ref)
            gps_ref[...] = jnp.zeros_like(gps_ref)

        cur, prev = _pool_bands()
        pwb = [pw_ref[g].astype(BF16) for g in range(NG)]

        def first(ch, carry):
            r0 = pl.multiple_of(ch * CH, CH)
            rp = pl.multiple_of(jnp.maximum(ch - 1, 0) * CH, CH)
            has_prev = jnp.where(ch > 0, 1.0, 0.0)
            for g in range(NG):
                cols = slice(g * CH, (g + 1) * CH)
                d = _pool_diff(x_ref, r0, rp, has_prev, cur, prev, g).astype(BF16)
                do = do_ref[pl.ds(r0, CH), cols].astype(F32)
                ypre = _dot(d, pwb[g])
                gps_ref[:, cols] += jnp.sum(do * ypre, axis=0, keepdims=True)
                dyp = (do * ps_ref[:, cols]).astype(BF16)
                gpw_ref[g] += _dot(d, dyp, TN)
                e_ref[pl.ds(r0, CH), cols] = _dot(dyp, pwb[g], NT)
            return carry

        lax.fori_loop(0, nch, first, 0, unroll=2)

        def second(ch, carry):
            r0 = pl.multiple_of(ch * CH, CH)
            rn = pl.multiple_of(jnp.minimum(ch + 1, nch - 1) * CH, CH)
            has_next = jnp.where(ch < nch - 1, 1.0, 0.0)
            for g in range(NG):
                cols = slice(g * CH, (g + 1) * CH)
                w = POOL_WINDOWS[g]
                dd = e_ref[pl.ds(r0, CH), cols]
                ec = (dd * _pool_inv_count(r0, w)).astype(BF16)
                en = (e_ref[pl.ds(rn, CH), cols] * _pool_inv_count(rn, w)).astype(BF16)
                dx = _dot(cur[g], ec, TN) + has_next * _dot(prev[g], en, TN) - dd
                dx_ref[pl.ds(r0, CH), cols] = dx.astype(BF16)
            return carry

        lax.fori_loop(0, nch, second, 0, unroll=2)

    small = pl.BlockSpec((NG, CH, CH), lambda b: (0, 0, 0))
    vec = pl.BlockSpec((1, BW), lambda b: (0, 0))
    return pl.pallas_call(
        body, name="pool_bwd", grid=(nb,),
        in_specs=[pl.BlockSpec((None, seq, BW), lambda b: (b, 0, 5)),
                  pl.BlockSpec((None, seq, BW), lambda b: (b, 0, 0)), small, vec, _HBM],
        out_specs=[pl.BlockSpec((None, seq, BW), lambda b: (b, 0, 5)), small, vec],
        out_shape=[jax.ShapeDtypeStruct((nb, seq, IN_COLS), BF16),
                   jax.ShapeDtypeStruct((NG, CH, CH), F32), jax.ShapeDtypeStruct((1, BW), F32)],
        input_output_aliases={4: 0},
        scratch_shapes=[pltpu.VMEM((seq, BW), F32)],
        compiler_params=_cp(("arbitrary",)),
    )(proj3, dout3, pw, pscale, dproj3)


SB_BQ = 256
SB_BK = 256
SB_SCALE = HD ** -0.5


SB_EXIT = -110.0


def _sb_tile(qs, k, mask):
    z = _dot(qs, k, NT)
    lb = jnp.minimum(z, 0.0) - jnp.log(1.0 + jnp.exp(-jnp.abs(z)))
    lom = lb - z
    if mask is not None:
        lom = jnp.where(mask, lom, 0.0)
    return lb, lom


def _sb_alive(c):
    top = functools.reduce(jnp.maximum, [jnp.max(state[1]) for state in c])
    return (top > SB_EXIT).astype(jnp.int32)


def _sb_past_blocks(step, c, npast):
    def cond(s):
        return jnp.logical_and(s[0] < npast, s[1] > 0)

    def body(s):
        i, _, c = s
        c = step(pl.multiple_of((npast - 1 - i) * SB_BK, SB_BK), c, None)
        return i + 1, _sb_alive(c), c

    return lax.while_loop(cond, body, (jnp.int32(0), _sb_alive(c), c))[2]


def _sb_diag_mask(bq, d):
    row = lax.broadcasted_iota(jnp.int32, (bq, SB_BK), 0)
    col = lax.broadcasted_iota(jnp.int32, (bq, SB_BK), 1)
    return col + d * SB_BK < row


def _sb_scaled(q):
    return (q.astype(F32) * SB_SCALE).astype(BF16)


def _dot_tri(a, m):
    return _dot(a.astype(BF16), m)


def _dot_tri2(a, m):
    hi = a.astype(BF16)
    lo = (a - hi.astype(F32)).astype(BF16)
    return _dot(hi, m) + _dot(lo, m)


def _sb_fwd(proj3, gather=()):
    nb, seq, _ = proj3.shape
    bq = min(SB_BQ, seq)
    nq = seq // bq
    ndiag = bq // SB_BK

    def body(q_ref, k_ref, v_ref, o_ref):
        row = lax.broadcasted_iota(jnp.int32, (SB_BK, SB_BK), 0)
        col = lax.broadcasted_iota(jnp.int32, (SB_BK, SB_BK), 1)
        upper = jnp.where(row > col, 1.0, 0.0).astype(BF16)
        heads = [slice(hh * HD, (hh + 1) * HD) for hh in range(2)]

        def qloop(qi, carry):
            q0 = pl.multiple_of(qi * bq, bq)
            qs = [_sb_scaled(q_ref[pl.ds(q0, bq), lanes]) for lanes in heads]

            def step(k0, c, mask):
                tiles = [_sb_tile(q, k_ref[pl.ds(k0, SB_BK), lanes], mask) for lanes, q in zip(heads, qs)]
                sums = [_dot_tri(lom, upper) for _, lom in tiles]
                out = []
                for lanes, (acc, cr), (lb, lom), cs in zip(heads, c, tiles, sums):
                    a = jnp.exp(lb + (cs + cr))
                    if mask is not None:
                        a = jnp.where(mask, a, 0.0)
                    rsum = cs[:, 0:1] + lom[:, 0:1]
                    out.append((acc + _dot(a.astype(BF16), v_ref[pl.ds(k0, SB_BK), lanes]), cr + rsum))
                return tuple(out)

            c = tuple((jnp.zeros((bq, HD), F32), jnp.zeros((bq, 1), F32)) for _ in heads)
            for d in reversed(range(ndiag)):
                c = step(pl.multiple_of(q0 + d * SB_BK, SB_BK), c, _sb_diag_mask(bq, d))
            c = _sb_past_blocks(step, c, qi * ndiag)
            for lanes, (acc, _) in zip(heads, c):
                o_ref[pl.ds(q0, bq), lanes] = acc
            return carry

        lax.fori_loop(0, nq, qloop, 0)

    def spec(c0):
        return pl.BlockSpec((None, seq, 128), lambda b, p: (b, 0, c0 + p))

    grid = (nb, BW // 128)
    body, ex_in, ex_out, ex_shape, ex_sems = _host_exchange(body, 3, 1, grid, gather, False)
    outs = pl.pallas_call(
        body, name="sb_fwd", grid=grid,
        in_specs=[spec(8), spec(12), spec(16)] + ex_in,
        out_specs=[spec(0)] + ex_out,
        out_shape=[jax.ShapeDtypeStruct((nb, seq, BW), F32)] + ex_shape,
        scratch_shapes=ex_sems,
        compiler_params=_cp(("arbitrary", "arbitrary")),
    )(proj3, proj3, proj3, *gather)
    return outs[0], outs[1:]


def _sb_bwd(proj3, do3, o3, scatter=()):
    nb, seq, _ = proj3.shape
    bq = min(SB_BQ, seq)
    nq = seq // bq
    ndiag = bq // SB_BK

    def body(q_ref, k_ref, v_ref, do_ref, o_ref, dq_ref, dk_ref, dv_ref, dk_acc, dv_acc):
        row = lax.broadcasted_iota(jnp.int32, (SB_BK, SB_BK), 0)
        col = lax.broadcasted_iota(jnp.int32, (SB_BK, SB_BK), 1)
        upper = jnp.where(row > col, 1.0, 0.0).astype(BF16)
        later = jnp.where(row >= col, 1.0, 0.0).astype(BF16)
        dk_acc[...] = jnp.zeros_like(dk_acc)
        dv_acc[...] = jnp.zeros_like(dv_acc)
        heads = [slice(hh * HD, (hh + 1) * HD) for hh in range(2)]

        def qloop(qi, carry):
            q0 = pl.multiple_of(qi * bq, bq)
            qs = [_sb_scaled(q_ref[pl.ds(q0, bq), lanes]) for lanes in heads]
            dos = [do_ref[pl.ds(q0, bq), lanes] for lanes in heads]
            gtot = [jnp.sum(do.astype(F32) * o_ref[pl.ds(q0, bq), lanes], axis=1, keepdims=True)
                    for do, lanes in zip(dos, heads)]

            def step(k0, c, mask):
                ks = [k_ref[pl.ds(k0, SB_BK), lanes] for lanes in heads]
                tiles = [_sb_tile(q, k, mask) for q, k in zip(qs, ks)]
                sums = [_dot_tri(lom, upper) for _, lom in tiles]
                das = [_dot(do, v_ref[pl.ds(k0, SB_BK), lanes], NT) for do, lanes in zip(dos, heads)]
                gls, avs = [], []
                for hh, (_, cr, _) in enumerate(c):
                    a = jnp.exp(tiles[hh][0] + (sums[hh] + cr))
                    if mask is not None:
                        a = jnp.where(mask, a, 0.0)
                    ab = a.astype(BF16)
                    avs.append(ab)
                    gls.append(das[hh] * ab.astype(F32))
                tails = [_dot_tri2(gl, later) for gl in gls]
                out = []
                for hh, (dq, cr, gdone) in enumerate(c):
                    lb, lom = tiles[hh]
                    pre = gtot[hh] - gdone - tails[hh]
                    dz = gls[hh] - jnp.exp(lb) * (gls[hh] + pre)
                    if mask is not None:
                        dz = jnp.where(mask, dz, 0.0)
                    dz = dz.astype(BF16)
                    dk_acc[hh, pl.ds(k0, SB_BK), :] += _dot(dz, qs[hh], TN)
                    dv_acc[hh, pl.ds(k0, SB_BK), :] += _dot(avs[hh], dos[hh], TN)
                    rsum = sums[hh][:, 0:1] + lom[:, 0:1]
                    out.append((dq + _dot(dz, ks[hh]), cr + rsum, gdone + tails[hh][:, 0:1]))
                return tuple(out)

            c = tuple((jnp.zeros((bq, HD), F32), jnp.zeros((bq, 1), F32), jnp.zeros((bq, 1), F32))
                      for _ in heads)
            for d in reversed(range(ndiag)):
                c = step(pl.multiple_of(q0 + d * SB_BK, SB_BK), c, _sb_diag_mask(bq, d))
            c = _sb_past_blocks(step, c, qi * ndiag)
            for lanes, (dq, _, _) in zip(heads, c):
                dq_ref[pl.ds(q0, bq), lanes] = (dq * SB_SCALE).astype(BF16)
            return carry

        lax.fori_loop(0, nq, qloop, 0)
        for hh in range(2):
            lanes = slice(hh * HD, (hh + 1) * HD)
            dk_ref[:, lanes] = dk_acc[hh].astype(BF16)
            dv_ref[:, lanes] = dv_acc[hh].astype(BF16)

    def spec(c0):
        return pl.BlockSpec((None, seq, 128), lambda b, p: (b, 0, c0 + p))

    grid = (nb, BW // 128)
    body, ex_in, ex_out, ex_shape, ex_sems = _host_exchange(body, 5, 3, grid, scatter, True)
    outs = pl.pallas_call(
        body, name="sb_bwd", grid=grid,
        in_specs=[spec(8), spec(12), spec(16), spec(0), spec(0)] + ex_in,
        out_specs=[spec(0), spec(0), spec(0)] + ex_out,
        out_shape=[jax.ShapeDtypeStruct((nb, seq, BW), BF16)] * 3 + ex_shape,
        scratch_shapes=[pltpu.VMEM((2, seq, HD), F32), pltpu.VMEM((2, seq, HD), F32)] + ex_sems,
        compiler_params=_cp(("arbitrary", "arbitrary")),
    )(proj3, proj3, proj3, do3, o3, *scatter)
    return outs[:3], outs[3:]


def _merge_fwd(brs, wb, proj):
    t = proj.shape[0]
    tm = _row_tile(t, 512)
    tn = 512
    nj = D // tn

    def body(b0, b1, b2, wb_ref, l0, l1, l2, m_ref, y0, y1, y2):
        acc = None
        for br, n, lg, y_ref in ((b0, 0, l0, y0), (b1, 1, l1, y1), (b2, 2, l2, y2)):
            y = _dot(br[...].astype(BF16), wb_ref[n])
            y_ref[...] = y.astype(BF16)
            term = jax.nn.sigmoid(lg[...].astype(F32)) * y
            acc = term if acc is None else acc + term
        m_ref[...] = acc.astype(BF16)

    def lspec(n):
        return pl.BlockSpec((tm, tn), lambda i, j: (i, (3 * D + n * D) // tn + j))

    tile = pl.BlockSpec((tm, tn), lambda i, j: (i, j))
    bspec = pl.BlockSpec((tm, BW), lambda i, j: (i, 0))
    return pl.pallas_call(
        body, name="merge_fwd", grid=(t // tm, nj),
        in_specs=[bspec, bspec, bspec, pl.BlockSpec((NB, BW, tn), lambda i, j: (0, 0, j)),
                  lspec(0), lspec(1), lspec(2)],
        out_specs=[tile] * 4,
        out_shape=[jax.ShapeDtypeStruct((t, D), BF16)] * 4,
        compiler_params=_cp(("parallel", "parallel")),
    )(brs[0], brs[1], brs[2], wb, proj, proj, proj)


def _merge_bwd(dm, ys, proj):
    t = proj.shape[0]
    tm = _row_tile(t, 512)

    def body(dm_ref, y0, y1, y2, lg_ref, dp_ref, dy0, dy1, dy2):
        dmv = dm_ref[...].astype(F32)
        for n, (y_ref, dy_ref) in enumerate(((y0, dy0), (y1, dy1), (y2, dy2))):
            cols = slice(n * D, (n + 1) * D)
            g = jax.nn.sigmoid(lg_ref[:, cols].astype(F32))
            dp_ref[:, cols] = (dmv * y_ref[...].astype(F32) * g * (1.0 - g)).astype(BF16)
            dy_ref[...] = (dmv * g).astype(BF16)

    tile = pl.BlockSpec((tm, D), lambda i: (i, 0))
    gates = pl.BlockSpec((tm, NB * D), lambda i: (i, 1))
    return pl.pallas_call(
        body, name="merge_bwd", grid=(t // tm,),
        in_specs=[tile] * 4 + [gates],
        out_specs=[gates] + [tile] * 3,
        out_shape=[jax.ShapeDtypeStruct((t, IN_COLS), BF16)] + [jax.ShapeDtypeStruct((t, D), BF16)] * 3,
        compiler_params=_cp(("parallel",)),
    )(dm, ys[0], ys[1], ys[2], proj)


def _adamw_rows(rows):
    if rows <= 512:
        return rows
    return next(tr for tr in (512, 384, 352, 256, 128, 64, 32, 16, 8) if rows % tr == 0)


def _adamw_math(npart, p_ref, w_ref, m_ref, v_ref, g_ref, d_ref, mo_ref, vo_ref):
    c1 = 1.0 - ADAM_B1 ** ADAM_STEP
    c2 = 1.0 - ADAM_B2 ** ADAM_STEP
    g = p_ref[0].astype(F32)
    for p in range(1, npart):
        g = g + p_ref[p].astype(F32)
    mn = ADAM_B1 * m_ref[...] + (1.0 - ADAM_B1) * g
    vn = ADAM_B2 * v_ref[...] + (1.0 - ADAM_B2) * (g * g)
    m_hat = mn / c1
    v_hat = vn / c2
    g_ref[...] = g
    d_ref[...] = -ADAM_LR * (m_hat / (jnp.sqrt(v_hat) + ADAM_EPS) + ADAM_WD * w_ref[...])
    mo_ref[...] = mn
    vo_ref[...] = vn


def _adamw_layer(name, parts, w, m, v, layer, bufs, padded=False):
    nl, cols = w.shape[0], w.shape[-1]
    rows = int(math.prod(w.shape[1:-1]))
    npart = parts.shape[0]
    tr = _adamw_rows(rows)
    if padded:
        assert parts.shape[1] == rows // tr and parts.shape[2] >= tr
        parts_spec = pl.BlockSpec((npart, None, tr, cols), lambda i: (0, i, 0, 0))
    else:
        parts = parts.reshape(npart, rows, cols)
        parts_spec = pl.BlockSpec((npart, tr, cols), lambda i: (0, i, 0))
    if bufs is None:
        bufs = [lax.empty((nl, rows, cols), F32) for _ in range(4)]

    def body(p_ref, w_ref, m_ref, v_ref, b0, b1, b2, b3, g_ref, d_ref, mo_ref, vo_ref):
        _adamw_math(npart, p_ref, w_ref, m_ref, v_ref, g_ref, d_ref, mo_ref, vo_ref)

    slab = pl.BlockSpec((None, tr, cols), lambda i: (layer, i, 0))
    sds = jax.ShapeDtypeStruct((nl, rows, cols), F32)
    return pl.pallas_call(
        body, name=name, grid=(rows // tr,),
        in_specs=[parts_spec, slab, slab, slab] + [_HBM] * 4,
        out_specs=[slab] * 4, out_shape=[sds] * 4,
        input_output_aliases={4: 0, 5: 1, 6: 2, 7: 3},
        compiler_params=_cp(("parallel",)),
    )(parts, w.reshape(nl, rows, cols), m.reshape(nl, rows, cols), v.reshape(nl, rows, cols), *bufs)


def _adamw_reduce(name, parts, w, m, v):
    shape = w.shape
    cols = shape[-1]
    rows = int(math.prod(shape[:-1])) if len(shape) > 1 else 1
    npart = parts.shape[0]
    tr = _adamw_rows(rows)

    def body(p_ref, w_ref, m_ref, v_ref, g_ref, d_ref, mo_ref, vo_ref):
        _adamw_math(npart, p_ref, w_ref, m_ref, v_ref, g_ref, d_ref, mo_ref, vo_ref)

    tile = pl.BlockSpec((tr, cols), lambda i: (i, 0))
    sds = jax.ShapeDtypeStruct((rows, cols), F32)
    outs = pl.pallas_call(
        body, name=name, grid=(rows // tr,),
        in_specs=[pl.BlockSpec((npart, tr, cols), lambda i: (0, i, 0)), tile, tile, tile],
        out_specs=[tile] * 4, out_shape=[sds] * 4,
        compiler_params=_cp(("parallel",)),
    )(parts.reshape(npart, rows, cols), w.reshape(rows, cols), m.reshape(rows, cols), v.reshape(rows, cols))
    return tuple(o.reshape(shape) for o in outs)


def _pad_ffn_in(w):
    lead = w.shape[:-1]
    w = w.reshape(lead + (2, FF_HALF))
    w = jnp.pad(w, [(0, 0)] * len(lead) + [(0, 0), (0, FF_HALF_PAD - FF_HALF)])
    return w.reshape(lead + (FF_IN_PAD,))


def kernel(x, c, rms_g1, rms_g2, w_ada, b_ada, w_in, gm_ln_g, gm_ln_b, gm_w_spatial, gm_b_spatial, pool_w, pool_scale, w_branch, w_out, w_ffn_in, w_ffn_out, final_g, loss_target, m_rms_g1, m_rms_g2, m_w_ada, m_b_ada, m_w_in, m_gm_ln_g, m_gm_ln_b, m_gm_w_spatial, m_gm_b_spatial, m_pool_w, m_pool_scale, m_w_branch, m_w_out, m_w_ffn_in, m_w_ffn_out, m_final_g, v_rms_g1, v_rms_g2, v_w_ada, v_b_ada, v_w_in, v_gm_ln_g, v_gm_ln_b, v_gm_w_spatial, v_gm_b_spatial, v_pool_w, v_pool_scale, v_w_branch, v_w_out, v_w_ffn_in, v_w_ffn_out, v_final_g):
    nb, seq, _ = x.shape
    nl = w_in.shape[0]
    t = nb * seq
    ntot = NDEV * nb
    me = _my_index()
    assert x.shape[2] == D and w_in.shape[1:] == (D, 768) and w_ffn_in.shape[1:] == (D, FF_IN_SHARD)
    assert seq % CH == 0

    w_ffn_in_p = _pad_ffn_in(w_ffn_in).astype(BF16)
    w_ffn_out_p = jnp.pad(w_ffn_out, ((0, 0), (0, FF_HALF_PAD - FF_HALF), (0, 0))).astype(BF16)
    w_in_b = w_in.astype(BF16)
    w_branch_b = w_branch.astype(BF16)
    w_out_b = w_out.astype(BF16)
    (g_in_next,) = _exchange([w_in_b[0]], "gather_w_in0", False)

    (c_all,) = _exchange([c], "gather_c", False)
    c_all = c_all.reshape(ntot, D)
    b_blk = lax.dynamic_slice_in_dim(b_ada, me * 768, 768, axis=1).reshape(nl, 1, 768)
    mod_blk = _ada_fwd(c_all, w_ada, b_blk)
    (mod_all,) = _exchange([mod_blk], "gather_mod", False)
    mod_all = jnp.transpose(mod_all, (1, 2, 0, 3)).reshape(nl, ntot, NMOD * D)
    mod = lax.dynamic_slice_in_dim(mod_all, me * nb, nb, axis=1).reshape(nl, nb, NMOD, 1, D)

    saved = []
    gathered = []
    xc = x
    for l in range(nl):
        sh1, sc1, gt1, sh2, sc2, gt2 = [mod[l, :, i] for i in range(NMOD)]
        h = _norm_mod_fwd(xc, rms_g1[l].reshape(1, D), sc1, sh1).reshape(t, D)
        proj, (g_ffn_in_w,) = _mm_colblocked("proj_fwd", h, g_in_next, BF16, [w_ffn_in_p[l]])
        proj3 = proj.reshape(nb, seq, IN_COLS)
        br_gm = _gmlp_fwd(proj, gm_ln_g[l].reshape(1, BW), gm_ln_b[l].reshape(1, BW),
                          gm_w_spatial[l], gm_b_spatial[l].T)
        sb_o, got = _sb_fwd(proj3, [w_branch_b[l], w_out_b[l]] + ([w_in_b[l + 1]] if l + 1 < nl else []))
        gw = dict(w_in=g_in_next,
                  w_branch=jnp.transpose(got[0], (1, 2, 0, 3)).reshape(NB, BW, D),
                  w_out=got[1].reshape(D, D),
                  w_ffn_in=g_ffn_in_w)
        gathered.append(gw)
        if l + 1 < nl:
            g_in_next = got[2]
        br_pool = _pool_fwd(proj3, pool_w[l], pool_scale[l].reshape(1, BW))
        brs = [br_gm, sb_o.reshape(t, BW), br_pool.reshape(t, BW)]
        merged, y0, y1, y2 = _merge_fwd(brs, gw["w_branch"], proj)
        x_mid, mo = _mm_residual("out_fwd", merged, gw["w_out"], xc.reshape(t, D), gt1, seq)
        x_mid = x_mid.reshape(nb, seq, D)
        h2 = _norm_mod_fwd(x_mid, rms_g2[l].reshape(1, D), sc2, sh2).reshape(t, D)
        fg, fu, act, got = _ffn_in_fwd(h2, gw["w_ffn_in"], [w_ffn_out_p[l]])
        gw["w_ffn_out"] = got[0].reshape(FFP, D)
        x_out, fo = _mm_residual("ffn_out_fwd", act, gw["w_ffn_out"], x_mid.reshape(t, D), gt2, seq)
        saved.append(dict(x_in=xc, h=h, proj=proj, brs=brs, sb_o=sb_o, ys=(y0, y1, y2), merged=merged,
                          mo=mo, x_mid=x_mid, h2=h2, fg=fg, fu=fu, act=act, fo=fo))
        xc = x_out.reshape(nb, seq, D)

    dx, loss_part, dfinal_part, dfo, dgt2 = _loss_head(xc, loss_target, final_g.reshape(1, D),
                                                       saved[-1]["fo"].reshape(nb, seq, D), mod[nl - 1, :, 5])
    loss = lax.psum(jnp.sum(loss_part[:, 0, 0]), ("x", "y", "c"))

    big_names = ("w_in", "w_branch", "w_out", "w_ffn_in", "w_ffn_out")
    bufs = {name: None for name in big_names}
    w_ffn_in_t, m_w_ffn_in_t, v_w_ffn_in_t = [jnp.swapaxes(a, 1, 2) for a in (w_ffn_in, m_w_ffn_in, v_w_ffn_in)]
    small_parts = {k: [None] * nl for k in ("rms_g1", "rms_g2", "gm_ln_g", "gm_ln_b", "gm_w_spatial",
                                            "gm_b_spatial", "pool_w", "pool_scale")}
    dmod = [None] * nl
    for l in reversed(range(nl)):
        gw = gathered[l]
        sv = saved[l]
        sh1, sc1, gt1, sh2, sc2, gt2 = [mod[l, :, i] for i in range(NMOD)]
        dfo = dfo.reshape(t, D)
        g_ffn_out = _mm_tn("ffn_out_wgrad", sv["act"], dfo)
        dfg, dfu = _ffn_out_dgrad(dfo, gw["w_ffn_out"], sv["fg"], sv["fu"])
        dh2 = _ffn_in_dgrad(dfg, dfu, gw["w_ffn_in"])
        g_ffn_in = _ffn_in_wgrad(sv["h2"], dfg, dfu)
        dx_mid, dsh2, dsc2, dg2, dmo, dgt1 = _norm_mod_bwd(
            sv["x_mid"], dh2.reshape(nb, seq, D), dx, rms_g2[l].reshape(1, D), sc2,
            gate=(sv["mo"].reshape(nb, seq, D), gt1))
        dmo = dmo.reshape(t, D)
        dmerged = _mm_nt("out_dgrad", dmo, gw["w_out"], BF16)
        g_out = _mm_tn("out_wgrad", sv["merged"], dmo)
        dproj, *dys = _merge_bwd(dmerged, sv["ys"], sv["proj"])
        dbrs, g_br = [], []
        for n in range(NB):
            dbrs.append(_mm_nt("branch_dgrad", dys[n], gw["w_branch"], BF16, w_lead=n))
            g_br.append(_mm_tn("branch_wgrad", sv["brs"][n], dys[n]))
        proj3 = sv["proj"].reshape(nb, seq, IN_COLS)
        dproj, g_ws, g_bs, g_lg, g_lb = _gmlp_bwd(sv["proj"], dbrs[0], gm_ln_g[l].reshape(1, BW),
                                                  gm_ln_b[l].reshape(1, BW), gm_w_spatial[l], gm_b_spatial[l].T,
                                                  dproj)
        g_br_dev = jnp.transpose(jnp.stack(g_br).reshape(NB, BW, NDEV, D // NDEV), (2, 0, 1, 3))
        carried = [g_br_dev, g_out.reshape(NDEV, D // NDEV, D), g_ffn_in, g_ffn_out.reshape(NDEV, FF_HALF_PAD, D)]
        d_sb, recv = _sb_bwd(proj3, dbrs[1].reshape(nb, seq, BW), sv["sb_o"], carried)
        bufs["w_branch"] = _adamw_layer("adamw_w_branch", recv[0], w_branch, m_w_branch, v_w_branch, l,
                                        bufs["w_branch"])
        bufs["w_out"] = _adamw_layer("adamw_w_out", recv[1], w_out, m_w_out, v_w_out, l, bufs["w_out"])
        bufs["w_ffn_in"] = _adamw_layer("adamw_w_ffn_in", recv[2].reshape(NDEV, 2, FF_HALF_PAD, D), w_ffn_in_t,
                                        m_w_ffn_in_t, v_w_ffn_in_t, l, bufs["w_ffn_in"], padded=True)
        bufs["w_ffn_out"] = _adamw_layer("adamw_w_ffn_out", recv[3].reshape(NDEV, 1, FF_HALF_PAD, D), w_ffn_out,
                                         m_w_ffn_out, v_w_ffn_out, l, bufs["w_ffn_out"], padded=True)
        dproj3, g_pw, g_ps = _pool_bwd(proj3, dbrs[2].reshape(nb, seq, BW), pool_w[l], pool_scale[l].reshape(1, BW),
                                       dproj.reshape(nb, seq, IN_COLS))
        dproj = dproj3.reshape(t, IN_COLS)
        for i, piece in enumerate(d_sb):
            dproj = lax.dynamic_update_slice(dproj, piece.reshape(t, BW), (0, 2 * BW + i * BW))
        g_in = _mm_colblocked_tn("proj_wgrad", sv["h"], dproj)
        dh, (r_in,) = _mm_colblocked_nt("proj_dgrad", dproj, gw["w_in"], F32, [g_in])
        bufs["w_in"] = _adamw_layer("adamw_w_in", r_in, w_in, m_w_in, v_w_in, l, bufs["w_in"])
        dmod_tail = [dgt1, dsh2, dsc2, dgt2]
        if l > 0:
            dx, dsh1, dsc1, dg1, dfo, dgt2 = _norm_mod_bwd(
                sv["x_in"], dh.reshape(nb, seq, D), dx_mid, rms_g1[l].reshape(1, D), sc1,
                gate=(saved[l - 1]["fo"].reshape(nb, seq, D), mod[l - 1, :, 5]))
        else:
            dx, dsh1, dsc1, dg1 = _norm_mod_bwd(sv["x_in"], dh.reshape(nb, seq, D), dx_mid,
                                                rms_g1[l].reshape(1, D), sc1)

        dmod[l] = jnp.concatenate([dsh1, dsc1] + dmod_tail, axis=-1)
        small_parts["rms_g1"][l] = jnp.sum(dg1, axis=0)
        small_parts["rms_g2"][l] = jnp.sum(dg2, axis=0)
        small_parts["gm_ln_g"][l] = g_lg
        small_parts["gm_ln_b"][l] = g_lb
        small_parts["gm_w_spatial"][l] = g_ws
        small_parts["gm_b_spatial"][l] = g_bs[:, :, 0]
        small_parts["pool_w"][l] = g_pw
        small_parts["pool_scale"][l] = g_ps

    dmod_mine = jnp.stack(dmod).reshape(nl, nb, NMOD * D)
    names = list(small_parts)
    stacked = [jnp.stack(small_parts[k]).astype(BF16 if k in ("gm_w_spatial", "pool_w") else F32) for k in names]
    gathered_small = _exchange(stacked + [dfinal_part, dmod_mine], "gather_small", False)
    dmod_all = jnp.transpose(gathered_small[-1], (1, 0, 2, 3)).reshape(nl, ntot, NMOD * D)
    dfinal_all = gathered_small[-2].reshape(ntot, D)

    results = {}
    weights = dict(rms_g1=(rms_g1, m_rms_g1, v_rms_g1), rms_g2=(rms_g2, m_rms_g2, v_rms_g2),
                   gm_ln_g=(gm_ln_g, m_gm_ln_g, v_gm_ln_g), gm_ln_b=(gm_ln_b, m_gm_ln_b, v_gm_ln_b),
                   gm_w_spatial=(gm_w_spatial, m_gm_w_spatial, v_gm_w_spatial),
                   gm_b_spatial=(gm_b_spatial, m_gm_b_spatial, v_gm_b_spatial),
                   pool_w=(pool_w, m_pool_w, v_pool_w), pool_scale=(pool_scale, m_pool_scale, v_pool_scale))
    for k, parts in zip(names, gathered_small[:len(names)]):
        w, m, v = weights[k]
        results[k] = _adamw_reduce("adamw_" + k, parts.reshape((NDEV,) + w.shape), w, m, v)
    results["final_g"] = _adamw_reduce("adamw_final_g", dfinal_all, final_g, m_final_g, v_final_g)
    results["b_ada"] = _adamw_reduce("adamw_b_ada", jnp.transpose(dmod_all, (1, 0, 2)), b_ada, m_b_ada, v_b_ada)
    dmod_blk = lax.dynamic_slice_in_dim(dmod_all, me * 768, 768, axis=2)
    g_w_ada = _ada_bwd(c_all, dmod_blk)
    results["w_ada"] = _adamw_reduce("adamw_w_ada", g_w_ada[None], w_ada, m_w_ada, v_w_ada)
    stacked_w = dict(w_in=w_in, w_branch=w_branch, w_out=w_out, w_ffn_in=w_ffn_in_t, w_ffn_out=w_ffn_out)
    for name in big_names:
        results[name] = tuple(b.reshape(stacked_w[name].shape) for b in bufs[name])
    results["w_ffn_in"] = tuple(jnp.swapaxes(b, 1, 2) for b in results["w_ffn_in"])

    order = ["rms_g1", "rms_g2", "w_ada", "b_ada", "w_in", "gm_ln_g", "gm_ln_b", "gm_w_spatial", "gm_b_spatial",
             "pool_w", "pool_scale", "w_branch", "w_out", "w_ffn_in", "w_ffn_out", "final_g"]
    out = [loss, dx]
    for i in range(4):
        out.extend(results[k][i] for k in order)
    return tuple(out)
```

```python
import functools
import math

import jax
import jax.numpy as jnp
from jax import lax
from jax.experimental import pallas as pl
from jax.experimental.pallas import tpu as pltpu

F32 = jnp.float32
BF16 = jnp.bfloat16
MESH = pl.DeviceIdType.MESH

D = 1024
BW = 512
NB = 3
CH = 128
NG = 4
HD = 64
POOL_WINDOWS = (2, 4, 8, 16)
DFF = 2816
NMOD = 6
EPS = 1e-6
IN_COLS = 6 * D
NDEV = 8
FF_IN_SHARD = 2 * DFF // NDEV
FF_HALF = FF_IN_SHARD // 2
FF_HALF_PAD = 384
FF_IN_PAD = 2 * FF_HALF_PAD
FFP = NDEV // 2 * FF_IN_PAD

ADAM_LR = 0.001
ADAM_B1 = 0.9
ADAM_B2 = 0.999
ADAM_EPS = 1e-08
ADAM_WD = 0.01
ADAM_STEP = 10

VMEM_LIMIT = 48 * 1024 * 1024
BIG_ROWS = 2048
NN = (((1,), (0,)), ((), ()))
NT = (((1,), (1,)), ((), ()))
TN = (((0,), (0,)), ((), ()))


def _cp(sem=None):
    return pltpu.CompilerParams(dimension_semantics=sem, vmem_limit_bytes=VMEM_LIMIT)


def _dot(a, b, dims=NN):
    return lax.dot_general(a, b, dims, preferred_element_type=F32)


def _my_index():
    return 4 * lax.axis_index("x") + 2 * lax.axis_index("y") + lax.axis_index("c")


def _peer(k):
    x, y, c = lax.axis_index("x"), lax.axis_index("y"), lax.axis_index("c")
    px = 1 - x if k & 4 else x
    py = 1 - y if k & 2 else y
    pc = 1 - c if k & 1 else c
    return (px, py, pc), 4 * px + 2 * py + pc


def _exchange(xs, name, all_to_all):
    n = len(xs)

    def body(*refs):
        _exchange_start(refs[:n], refs[n:2 * n], refs[2 * n:], all_to_all)
        _exchange_relay(refs[:n], refs[n:2 * n], refs[2 * n:], all_to_all)
        _exchange_finish(refs[:n], refs[n:2 * n], refs[2 * n:], all_to_all)

    return pl.pallas_call(
        body, name=name, out_shape=_exchange_out_shape(xs, all_to_all),
        in_specs=[_HBM] * n, out_specs=[_HBM] * n, scratch_shapes=_exchange_sems(n),
    )(*xs)


_HBM = pl.BlockSpec(memory_space=pl.ANY)


def _exchange_out_shape(xs, all_to_all):
    if all_to_all:
        return [jax.ShapeDtypeStruct(x.shape, x.dtype) for x in xs]
    return [jax.ShapeDtypeStruct((NDEV,) + x.shape, x.dtype) for x in xs]


def _exchange_sems(n):
    return [pltpu.SemaphoreType.DMA((n * 7,)), pltpu.SemaphoreType.DMA((n * 7,)), pltpu.SemaphoreType.DMA((n,))]


def _all_to_all_copies(ins, outs, sems):
    send_sems, recv_sems, local_sems = sems
    me = _my_index()
    local, sends, recvs = [], [], []
    for a in range(len(ins)):
        local.append(pltpu.make_async_copy(ins[a].at[me], outs[a].at[me], local_sems.at[a]))
    for k in range(1, NDEV):
        dev, idx = _peer(k)
        for a in range(len(ins)):
            sem = dict(send_sem=send_sems.at[a * 7 + k - 1], recv_sem=recv_sems.at[a * 7 + k - 1],
                       device_id=dev, device_id_type=MESH)
            sends.append(pltpu.make_async_remote_copy(src_ref=ins[a].at[idx], dst_ref=outs[a].at[me], **sem))
            recvs.append(pltpu.make_async_remote_copy(src_ref=ins[a].at[idx], dst_ref=outs[a].at[idx], **sem))
    return local, sends, recvs


def _gather_copies(ins, outs, sems):
    send_sems, recv_sems, local_sems = sems
    x, y, c = lax.axis_index("x"), lax.axis_index("y"), lax.axis_index("c")
    me, other = 4 * x + 2 * y + c, 4 * x + 2 * y + (1 - c)
    other_dev = (x, y, 1 - c)
    chips = [(1 - x, y), (x, 1 - y), (1 - x, 1 - y)]
    local, own, relay, from_other = [], [], [], []
    for a in range(len(ins)):
        def copy(k, src, block, dev, a=a):
            return pltpu.make_async_remote_copy(
                src_ref=src, dst_ref=outs[a].at[block], send_sem=send_sems.at[a * 7 + k],
                recv_sem=recv_sems.at[a * 7 + k], device_id=dev, device_id_type=MESH)

        local.append(pltpu.make_async_copy(ins[a], outs[a].at[me], local_sems.at[a]))
        own.append(copy(0, ins[a], me, other_dev))
        from_other.append(copy(0, ins[a], other, other_dev))
        for j, (px, py) in enumerate(chips):
            far = 4 * px + 2 * py + c
            own.append(copy(1 + j, ins[a], me, (px, py, c)))
            relay.append((copy(1 + j, ins[a], far, (px, py, c)), copy(4 + j, outs[a].at[far], far, other_dev)))
            from_other.append(copy(4 + j, ins[a], 4 * px + 2 * py + (1 - c), other_dev))
    return local, own, relay, from_other


def _exchange_start(ins, outs, sems, all_to_all):
    local, sends = (_all_to_all_copies if all_to_all else _gather_copies)(ins, outs, sems)[:2]
    for cp in local + sends:
        cp.start()


def _exchange_relay(ins, outs, sems, all_to_all):
    if not all_to_all:
        for arrival, passing_on in _gather_copies(ins, outs, sems)[2]:
            arrival.wait_recv()
            passing_on.start()


def _exchange_finish(ins, outs, sems, all_to_all):
    if all_to_all:
        local, sends, recvs = _all_to_all_copies(ins, outs, sems)
    else:
        local, own, relay, recvs = _gather_copies(ins, outs, sems)
        sends = own + [passing_on for _, passing_on in relay]
    for cp in sends:
        cp.wait_send()
    for cp in recvs:
        cp.wait_recv()
    for cp in local:
        cp.wait()


def _host_exchange(body, n_in, n_out, grid, xs, all_to_all):
    n = len(xs)
    if n == 0:
        return body, [], [], [], []
    steps = math.prod(grid)
    half = steps * 3 // 4 if steps >= 4 else steps - 1

    def hosted(*refs):
        ins, ex_ins = refs[:n_in], refs[n_in:n_in + n]
        outs, ex_outs = refs[n_in + n:n_in + n + n_out], refs[n_in + n + n_out:n_in + 2 * n + n_out]
        scratch = refs[n_in + 2 * n + n_out:]
        own, sems = scratch[:len(scratch) - 3], scratch[len(scratch) - 3:]
        step = 0
        for a in range(len(grid)):
            step = step * grid[a] + pl.program_id(a)

        @pl.when(step == 0)
        def _():
            _exchange_start(ex_ins, ex_outs, sems, all_to_all)

        body(*ins, *outs, *own)

        @pl.when(step == half)
        def _():
            _exchange_relay(ex_ins, ex_outs, sems, all_to_all)

        @pl.when(step == steps - 1)
        def _():
            _exchange_finish(ex_ins, ex_outs, sems, all_to_all)

    return hosted, [_HBM] * n, [_HBM] * n, _exchange_out_shape(xs, all_to_all), _exchange_sems(n)


def _mm(name, a, b, grid, a_spec, b_spec, o_spec, out_sds, dims, acc_shape, carried=(), all_to_all=True):
    nk = grid[2]

    if nk == 1:
        def body(a_ref, b_ref, o_ref):
            o_ref[...] = _dot(a_ref[...].astype(BF16), b_ref[...].astype(BF16), dims).astype(o_ref.dtype)
        scratch = []
    else:
        def body(a_ref, b_ref, o_ref, acc_ref):
            k = pl.program_id(2)

            @pl.when(k == 0)
            def _():
                acc_ref[...] = jnp.zeros_like(acc_ref)

            acc_ref[...] += _dot(a_ref[...].astype(BF16), b_ref[...].astype(BF16), dims)

            @pl.when(k == nk - 1)
            def _():
                o_ref[...] = acc_ref[...].astype(o_ref.dtype)
        scratch = [pltpu.VMEM(acc_shape, F32)]

    body, ex_in, ex_out, ex_shape, ex_sems = _host_exchange(body, 2, 1, grid, carried, all_to_all)
    outs = pl.pallas_call(
        body, name=name, grid=grid, in_specs=[a_spec, b_spec] + ex_in, out_specs=[o_spec] + ex_out,
        out_shape=[out_sds] + ex_shape,
        scratch_shapes=scratch + ex_sems,
        compiler_params=_cp(("arbitrary",) * 3 if carried else ("parallel", "parallel", "arbitrary")),
    )(a, b, *carried)
    return (outs[0], outs[1:]) if carried else outs[0]


def _row_tile(t, want):
    tm = min(t, want)
    assert t % tm == 0
    return tm


def _mm_colblocked(name, a, wg, out_dtype, gather=()):
    t = a.shape[0]
    tm = _row_tile(t, BIG_ROWS)
    return _mm(name, a, wg, (t // tm, NDEV, 1),
               pl.BlockSpec((tm, D), lambda i, j, k: (i, 0)),
               pl.BlockSpec((None, D, 768), lambda i, j, k: (j, 0, 0)),
               pl.BlockSpec((tm, 768), lambda i, j, k: (i, j)),
               jax.ShapeDtypeStruct((t, NDEV * 768), out_dtype), NN, (tm, 768), gather, False)


def _mm_colblocked_nt(name, g, wg, out_dtype, scatter=()):
    t = g.shape[0]
    tm = _row_tile(t, BIG_ROWS)
    return _mm(name, g, wg, (t // tm, 1, NDEV),
               pl.BlockSpec((tm, 768), lambda i, j, k: (i, k)),
               pl.BlockSpec((None, D, 768), lambda i, j, k: (k, 0, 0)),
               pl.BlockSpec((tm, D), lambda i, j, k: (i, 0)),
               jax.ShapeDtypeStruct((t, D), out_dtype), NT, (tm, D), scatter)


_HALF = NDEV // 2


def _ffn_in_fwd(h2, wg, gather=()):
    t = h2.shape[0]
    tm = _row_tile(t, 1024)

    def body(a_ref, wg_ref, wu_ref, g_ref, u_ref, act_ref):
        a = a_ref[...]
        g = _dot(a, wg_ref[...])
        u = _dot(a, wu_ref[...])
        g_ref[...] = g.astype(BF16)
        u_ref[...] = u.astype(BF16)
        act_ref[...] = (g * jax.nn.sigmoid(g) * u).astype(BF16)

    tile = pl.BlockSpec((tm, 768), lambda i, j: (i, j))
    grid = (t // tm, _HALF)
    body, ex_in, ex_out, ex_shape, ex_sems = _host_exchange(body, 3, 3, grid, gather, False)
    outs = pl.pallas_call(
        body, name="ffn_in_fwd", grid=grid,
        in_specs=[pl.BlockSpec((tm, D), lambda i, j: (i, 0)),
                  pl.BlockSpec((None, D, 768), lambda i, j: (j, 0, 0)),
                  pl.BlockSpec((None, D, 768), lambda i, j: (j + _HALF, 0, 0))] + ex_in,
        out_specs=[tile] * 3 + ex_out, out_shape=[jax.ShapeDtypeStruct((t, FFP), BF16)] * 3 + ex_shape,
        scratch_shapes=ex_sems,
        compiler_params=_cp(("arbitrary", "arbitrary") if gather else ("parallel", "parallel")),
    )(h2, wg, wg, *gather)
    return outs[0], outs[1], outs[2], outs[3:]


def _ffn_out_dgrad(dfo, w, fg, fu):
    t = dfo.shape[0]
    tm = _row_tile(t, 1024)

    def body(a_ref, w_ref, g_ref, u_ref, dg_ref, du_ref):
        d = _dot(a_ref[...], w_ref[...], NT)
        g = g_ref[...].astype(F32)
        s = jax.nn.sigmoid(g)
        gs = g * s
        dg_ref[...] = (d * u_ref[...].astype(F32) * (s + gs * (1.0 - s))).astype(BF16)
        du_ref[...] = (d * gs).astype(BF16)

    tile = pl.BlockSpec((tm, 768), lambda i, j: (i, j))
    return pl.pallas_call(
        body, name="ffn_out_dgrad", grid=(t // tm, _HALF),
        in_specs=[pl.BlockSpec((tm, D), lambda i, j: (i, 0)),
                  pl.BlockSpec((768, D), lambda i, j: (j, 0)), tile, tile],
        out_specs=[tile] * 2, out_shape=[jax.ShapeDtypeStruct((t, FFP), BF16)] * 2,
        compiler_params=_cp(("parallel", "parallel")),
    )(dfo, w, fg, fu)


def _ffn_in_dgrad(dg, du, wg):
    t = dg.shape[0]
    tm = _row_tile(t, BIG_ROWS)

    def body(g_ref, u_ref, w_ref, o_ref, acc_ref):
        k = pl.program_id(1)

        @pl.when(k == 0)
        def _():
            acc_ref[...] = jnp.zeros_like(acc_ref)

        @pl.when(k < _HALF)
        def _():
            acc_ref[...] += _dot(g_ref[...], w_ref[...], NT)

        @pl.when(k >= _HALF)
        def _():
            acc_ref[...] += _dot(u_ref[...], w_ref[...], NT)

        @pl.when(k == NDEV - 1)
        def _():
            o_ref[...] = acc_ref[...]

    return pl.pallas_call(
        body, name="ffn_in_dgrad", grid=(t // tm, NDEV),
        in_specs=[pl.BlockSpec((tm, 768), lambda i, k: (i, jnp.minimum(k, _HALF - 1))),
                  pl.BlockSpec((tm, 768), lambda i, k: (i, jnp.maximum(k - _HALF, 0))),
                  pl.BlockSpec((None, D, 768), lambda i, k: (k, 0, 0))],
        out_specs=pl.BlockSpec((tm, D), lambda i, k: (i, 0)),
        out_shape=jax.ShapeDtypeStruct((t, D), F32),
        scratch_shapes=[pltpu.VMEM((tm, D), F32)],
        compiler_params=_cp(("parallel", "arbitrary")),
    )(dg, du, wg)


def _ffn_in_wgrad(h2, dg, du):
    t = h2.shape[0]
    tk = _row_tile(t, BIG_ROWS)
    nk = t // tk

    def body(a_ref, g_ref, u_ref, o_ref, acc_ref):
        j, k = pl.program_id(0), pl.program_id(1)

        @pl.when(k == 0)
        def _():
            acc_ref[...] = jnp.zeros_like(acc_ref)

        @pl.when(j < _HALF)
        def _():
            acc_ref[...] += _dot(g_ref[...], a_ref[...], TN)

        @pl.when(j >= _HALF)
        def _():
            acc_ref[...] += _dot(u_ref[...], a_ref[...], TN)

        @pl.when(k == nk - 1)
        def _():
            o_ref[...] = acc_ref[...].astype(BF16)

    return pl.pallas_call(
        body, name="ffn_in_wgrad", grid=(NDEV, nk),
        in_specs=[pl.BlockSpec((tk, D), lambda j, k: (k, 0)),
                  pl.BlockSpec((tk, 768), lambda j, k: (jnp.where(j < _HALF, k, 0), jnp.minimum(j, _HALF - 1))),
                  pl.BlockSpec((tk, 768), lambda j, k: (jnp.where(j < _HALF, 0, k), jnp.maximum(j - _HALF, 0)))],
        out_specs=pl.BlockSpec((None, 768, D), lambda j, k: (j, 0, 0)),
        out_shape=jax.ShapeDtypeStruct((NDEV, 768, D), BF16),
        scratch_shapes=[pltpu.VMEM((768, D), F32)],
        compiler_params=_cp(("parallel", "arbitrary")),
    )(h2, dg, du)


def _mm_colblocked_tn(name, a, g):
    t = a.shape[0]
    tk = _row_tile(t, BIG_ROWS)
    return _mm(name, a, g, (1, NDEV, t // tk),
               pl.BlockSpec((tk, D), lambda i, j, k: (k, 0)),
               pl.BlockSpec((tk, 768), lambda i, j, k: (k, j)),
               pl.BlockSpec((None, D, 768), lambda i, j, k: (j, 0, 0)),
               jax.ShapeDtypeStruct((NDEV, D, 768), BF16), TN, (D, 768))


def _mm_nt(name, a, w, out_dtype, a_col=0, w_lead=None):
    t = a.shape[0]
    if w_lead is None:
        kdim, n = w.shape
        b_spec = pl.BlockSpec((min(kdim, 1024), n), lambda i, j, k: (j, 0))
    else:
        _, kdim, n = w.shape
        b_spec = pl.BlockSpec((None, min(kdim, 1024), n), lambda i, j, k: (w_lead, j, 0))
    tn = min(kdim, 1024)
    tm = _row_tile(t, BIG_ROWS)
    return _mm(name, a, w, (t // tm, kdim // tn, 1),
               pl.BlockSpec((tm, n), lambda i, j, k: (i, a_col)),
               b_spec,
               pl.BlockSpec((tm, tn), lambda i, j, k: (i, j)),
               jax.ShapeDtypeStruct((t, kdim), out_dtype), NT, (tm, tn))


def _mm_tn(name, a, g, out_dtype=BF16):
    t, kdim = a.shape
    n = g.shape[1]
    tk = _row_tile(t, BIG_ROWS)
    tm = min(kdim, 1024)
    tn = min(n, 1024)
    return _mm(name, a, g, (kdim // tm, n // tn, t // tk),
               pl.BlockSpec((tk, tm), lambda i, j, k: (k, i)),
               pl.BlockSpec((tk, tn), lambda i, j, k: (k, j)),
               pl.BlockSpec((tm, tn), lambda i, j, k: (i, j)),
               jax.ShapeDtypeStruct((kdim, n), out_dtype), TN, (tm, tn))


def _mm_residual(name, a, w, x, gt, seq):
    t, kdim = a.shape
    tm = _row_tile(seq, 1024)
    tn = D
    tk = min(kdim, 1024)
    nk = kdim // tk
    per = seq // tm

    def body(a_ref, w_ref, x_ref, gt_ref, xo_ref, y_ref, acc_ref):
        k = pl.program_id(2)

        @pl.when(k == 0)
        def _():
            acc_ref[...] = jnp.zeros_like(acc_ref)

        acc_ref[...] += _dot(a_ref[...], w_ref[...])

        @pl.when(k == nk - 1)
        def _():
            y = acc_ref[...]
            xo_ref[...] = x_ref[...] + gt_ref[0] * y
            y_ref[...] = y.astype(BF16)

    return pl.pallas_call(
        body, name=name, grid=(t // tm, D // tn, nk),
        in_specs=[pl.BlockSpec((tm, tk), lambda i, j, k: (i, k)),
                  pl.BlockSpec((tk, tn), lambda i, j, k: (k, j)),
                  pl.BlockSpec((tm, tn), lambda i, j, k: (i, j)),
                  pl.BlockSpec((1, 1, tn), lambda i, j, k: (i // per, 0, j))],
        out_specs=[pl.BlockSpec((tm, tn), lambda i, j, k: (i, j)),
                   pl.BlockSpec((tm, tn), lambda i, j, k: (i, j))],
        out_shape=[jax.ShapeDtypeStruct((t, D), F32), jax.ShapeDtypeStruct((t, D), BF16)],
        scratch_shapes=[pltpu.VMEM((tm, tn), F32)],
        compiler_params=_cp(("parallel", "parallel", "arbitrary")),
    )(a, w, x, gt)


def _ada_fwd(c_all, w_ada, b_blk):
    nl = w_ada.shape[0]
    nb = c_all.shape[0]

    def body(c_ref, w_ref, b_ref, o_ref):
        c = c_ref[...]
        ca = (c * jax.nn.sigmoid(c)).astype(BF16)
        o_ref[...] = _dot(ca, w_ref[...].astype(BF16)) + b_ref[...]

    return pl.pallas_call(
        body, name="ada_fwd", grid=(nl,),
        in_specs=[pl.BlockSpec((nb, D), lambda l: (0, 0)),
                  pl.BlockSpec((None, D, 768), lambda l: (l, 0, 0)),
                  pl.BlockSpec((None, 1, 768), lambda l: (l, 0, 0))],
        out_specs=pl.BlockSpec((None, nb, 768), lambda l: (l, 0, 0)),
        out_shape=jax.ShapeDtypeStruct((nl, nb, 768), F32),
        compiler_params=_cp(("parallel",)),
    )(c_all, w_ada, b_blk)


def _ada_bwd(c_all, dmod_blk):
    nl = dmod_blk.shape[0]
    nb = c_all.shape[0]

    def body(c_ref, d_ref, o_ref):
        c = c_ref[...]
        ca = (c * jax.nn.sigmoid(c)).astype(BF16)
        o_ref[...] = _dot(ca, d_ref[...].astype(BF16), TN)

    return pl.pallas_call(
        body, name="ada_bwd", grid=(nl,),
        in_specs=[pl.BlockSpec((nb, D), lambda l: (0, 0)),
                  pl.BlockSpec((None, nb, 768), lambda l: (l, 0, 0))],
        out_specs=pl.BlockSpec((None, D, 768), lambda l: (l, 0, 0)),
        out_shape=jax.ShapeDtypeStruct((nl, D, 768), F32),
        compiler_params=_cp(("parallel",)),
    )(c_all, dmod_blk)


def _seq_tile(seq):
    return _row_tile(seq, 512)


def _norm_mod_fwd(x, g, sc, sh):
    nb, seq, _ = x.shape
    ts = _seq_tile(seq)

    def body(x_ref, g_ref, sc_ref, sh_ref, h_ref):
        xv = x_ref[0]
        r = lax.rsqrt(jnp.mean(xv * xv, axis=-1, keepdims=True) + EPS)
        h_ref[0] = ((xv * r) * g_ref[...] * (1.0 + sc_ref[0]) + sh_ref[0]).astype(BF16)

    return pl.pallas_call(
        body, name="norm_mod_fwd", grid=(nb, seq // ts),
        in_specs=[pl.BlockSpec((1, ts, D), lambda b, s: (b, s, 0)),
                  pl.BlockSpec((1, D), lambda b, s: (0, 0)),
                  pl.BlockSpec((1, 1, D), lambda b, s: (b, 0, 0)),
                  pl.BlockSpec((1, 1, D), lambda b, s: (b, 0, 0))],
        out_specs=pl.BlockSpec((1, ts, D), lambda b, s: (b, s, 0)),
        out_shape=jax.ShapeDtypeStruct((nb, seq, D), BF16),
        compiler_params=_cp(("parallel", "parallel")),
    )(x, g, sc, sh)


def _gate_bwd_tile(d, y_ref, gt_ref, dy_ref, dgt_ref):
    @pl.when(pl.program_id(1) == 0)
    def _():
        dgt_ref[...] = jnp.zeros_like(dgt_ref)

    dy_ref[0] = (gt_ref[0] * d).astype(BF16)
    dgt_ref[0] += jnp.sum(d * y_ref[0].astype(F32), axis=0, keepdims=True)


def _norm_mod_bwd(x, dh, dres, g, sc, gate=None):
    nb, seq, _ = x.shape
    ts = _seq_tile(seq)

    def body(x_ref, dh_ref, dres_ref, g_ref, sc_ref, *rest):
        if gate is None:
            dx_ref, dsh_ref, dsc_ref, dg_ref = rest
        else:
            y_ref, gt_ref, dx_ref, dsh_ref, dsc_ref, dg_ref, dy_ref, dgt_ref = rest

        @pl.when(pl.program_id(1) == 0)
        def _():
            dsh_ref[...] = jnp.zeros_like(dsh_ref)
            dsc_ref[...] = jnp.zeros_like(dsc_ref)
            dg_ref[...] = jnp.zeros_like(dg_ref)

        xv = x_ref[0]
        dh = dh_ref[0]
        gv = g_ref[...]
        onesc = 1.0 + sc_ref[0]
        r = lax.rsqrt(jnp.mean(xv * xv, axis=-1, keepdims=True) + EPS)
        xh = xv * r
        dsh_ref[0] += jnp.sum(dh, axis=0, keepdims=True)
        dsc_ref[0] += jnp.sum(dh * (xh * gv), axis=0, keepdims=True)
        dg_ref[0] += jnp.sum(dh * onesc * xh, axis=0, keepdims=True)
        dxh = dh * (gv * onesc)
        dx = r * (dxh - xh * jnp.mean(dxh * xh, axis=-1, keepdims=True))
        dx_total = dres_ref[0] + dx
        dx_ref[0] = dx_total
        if gate is not None:
            _gate_bwd_tile(dx_total, y_ref, gt_ref, dy_ref, dgt_ref)

    vec = jax.ShapeDtypeStruct((nb, 1, D), F32)
    vspec = pl.BlockSpec((1, 1, D), lambda b, s: (b, 0, 0))
    tile = pl.BlockSpec((1, ts, D), lambda b, s: (b, s, 0))
    gated = gate is not None
    return pl.pallas_call(
        body, name="norm_mod_bwd", grid=(nb, seq // ts),
        in_specs=[tile, tile, tile, pl.BlockSpec((1, D), lambda b, s: (0, 0)), vspec] + [tile, vspec] * gated,
        out_specs=[tile, vspec, vspec, vspec] + [tile, vspec] * gated,
        out_shape=[jax.ShapeDtypeStruct((nb, seq, D), F32), vec, vec, vec]
        + [jax.ShapeDtypeStruct((nb, seq, D), BF16), vec] * gated,
        compiler_params=_cp(("parallel", "arbitrary")),
    )(x, dh, dres, g, sc, *(gate or ()))


def _loss_head(x, tgt, g, y, gt):
    nb, seq, _ = x.shape
    ts = _seq_tile(seq)

    def body(x_ref, t_ref, g_ref, y_ref, gt_ref, dx_ref, loss_ref, dg_ref, dy_ref, dgt_ref):
        @pl.when(pl.program_id(1) == 0)
        def _():
            loss_ref[...] = jnp.zeros_like(loss_ref)
            dg_ref[...] = jnp.zeros_like(dg_ref)

        xv = x_ref[0]
        gv = g_ref[...]
        r = lax.rsqrt(jnp.mean(xv * xv, axis=-1, keepdims=True) + EPS)
        xh = xv * r
        err = xh * gv - t_ref[0]
        per_tok = jnp.mean(err * err, axis=-1, keepdims=True)
        loss_ref[0] += 0.5 * jnp.sum(per_tok, axis=0, keepdims=True)
        dy = err * (1.0 / D)
        dg_ref[0] += jnp.sum(dy * xh, axis=0, keepdims=True)
        dxh = dy * gv
        dx = r * (dxh - xh * jnp.mean(dxh * xh, axis=-1, keepdims=True))
        dx_ref[0] = dx
        _gate_bwd_tile(dx, y_ref, gt_ref, dy_ref, dgt_ref)

    tile = pl.BlockSpec((1, ts, D), lambda b, s: (b, s, 0))
    vspec = pl.BlockSpec((1, 1, D), lambda b, s: (b, 0, 0))
    vec = jax.ShapeDtypeStruct((nb, 1, D), F32)
    return pl.pallas_call(
        body, name="loss_head", grid=(nb, seq // ts),
        in_specs=[tile, tile, pl.BlockSpec((1, D), lambda b, s: (0, 0)), tile, vspec],
        out_specs=[tile, pl.BlockSpec((1, 1, 128), lambda b, s: (b, 0, 0)), vspec, tile, vspec],
        out_shape=[jax.ShapeDtypeStruct((nb, seq, D), F32), jax.ShapeDtypeStruct((nb, 1, 128), F32), vec,
                   jax.ShapeDtypeStruct((nb, seq, D), BF16), vec],
        compiler_params=_cp(("parallel", "arbitrary")),
    )(x, tgt, g, y, gt)


_GELU_C = math.sqrt(2.0 / math.pi)


def _gelu(x):
    return 0.5 * x * (1.0 + jnp.tanh(_GELU_C * (x + 0.044715 * (x * x * x))))


def _gelu_and_grad(x):
    t = jnp.tanh(_GELU_C * (x + 0.044715 * (x * x * x)))
    y = 0.5 * x * (1.0 + t)
    dy = 0.5 * (1.0 + t) + 0.5 * x * (1.0 - t * t) * (_GELU_C * (1.0 + 3.0 * 0.044715 * (x * x)))
    return y, dy


def _tril_mask():
    row = lax.broadcasted_iota(jnp.int32, (CH, CH), 0)
    col = lax.broadcasted_iota(jnp.int32, (CH, CH), 1)
    return row >= col


def _gmlp_fwd(proj, ln_g, ln_b, ws, bst):
    t = proj.shape[0]
    tm = _row_tile(t, 512)

    def body(u_ref, v_ref, lg_ref, lb_ref, ws_ref, bst_ref, o_ref):
        tril = _tril_mask()
        wm = [jnp.where(tril, ws_ref[g], 0.0).astype(BF16) for g in range(NG)]
        for ch in range(tm // CH):
            rows = slice(ch * CH, (ch + 1) * CH)
            u = _gelu(u_ref[rows, :].astype(F32))
            v = _gelu(v_ref[rows, :].astype(F32))
            mu = jnp.mean(v, axis=-1, keepdims=True)
            xc = v - mu
            rstd = lax.rsqrt(jnp.mean(xc * xc, axis=-1, keepdims=True) + EPS)
            vn = ((xc * rstd) * lg_ref[...] + lb_ref[...]).astype(BF16)
            for g in range(NG):
                cols = slice(g * CH, (g + 1) * CH)
                s = _dot(wm[g], vn[:, cols]) + bst_ref[:, g:g + 1]
                o_ref[rows, cols] = (u[:, cols] * s).astype(BF16)

    return pl.pallas_call(
        body, name="gmlp_fwd", grid=(t // tm,),
        in_specs=[pl.BlockSpec((tm, BW), lambda i: (i, 0)),
                  pl.BlockSpec((tm, BW), lambda i: (i, 1)),
                  pl.BlockSpec((1, BW), lambda i: (0, 0)),
                  pl.BlockSpec((1, BW), lambda i: (0, 0)),
                  pl.BlockSpec((NG, CH, CH), lambda i: (0, 0, 0)),
                  pl.BlockSpec((CH, NG), lambda i: (0, 0))],
        out_specs=pl.BlockSpec((tm, BW), lambda i: (i, 0)),
        out_shape=jax.ShapeDtypeStruct((t, BW), BF16),
        compiler_params=_cp(("parallel",)),
    )(proj, proj, ln_g, ln_b, ws, bst)


def _gmlp_bwd(proj, dout, ln_g, ln_b, ws, bst, dproj):
    t = proj.shape[0]
    tm = _row_tile(t, 512)

    def body(u_ref, v_ref, do_ref, lg_ref, lb_ref, ws_ref, bst_ref, buf_ref,
             dp_ref, gws_ref, gbs_ref, glg_ref, glb_ref):
        @pl.when(pl.program_id(0) == 0)
        def _():
            gws_ref[...] = jnp.zeros_like(gws_ref)
            gbs_ref[...] = jnp.zeros_like(gbs_ref)
            glg_ref[...] = jnp.zeros_like(glg_ref)
            glb_ref[...] = jnp.zeros_like(glb_ref)

        tril = _tril_mask()
        wm = [jnp.where(tril, ws_ref[g], 0.0).astype(BF16) for g in range(NG)]
        ones = jnp.ones((CH, CH), BF16)
        lg = lg_ref[...]
        for ch in range(tm // CH):
            rows = slice(ch * CH, (ch + 1) * CH)
            u, du_fac = _gelu_and_grad(u_ref[rows, :].astype(F32))
            v, dv_fac = _gelu_and_grad(v_ref[rows, :].astype(F32))
            do = do_ref[rows, :].astype(F32)
            mu = jnp.mean(v, axis=-1, keepdims=True)
            xc = v - mu
            rstd = lax.rsqrt(jnp.mean(xc * xc, axis=-1, keepdims=True) + EPS)
            xh = xc * rstd
            vn = (xh * lg + lb_ref[...]).astype(BF16)
            dvn_parts = []
            for g in range(NG):
                cols = slice(g * CH, (g + 1) * CH)
                s = _dot(wm[g], vn[:, cols]) + bst_ref[:, g:g + 1]
                dp_ref[rows, cols] = (do[:, cols] * s * du_fac[:, cols]).astype(BF16)
                ds = (do[:, cols] * u[:, cols]).astype(BF16)
                gws_ref[g] += jnp.where(tril, _dot(ds, vn[:, cols], NT), 0.0)
                gbs_ref[g] += _dot(ds, ones)
                dvn_parts.append(_dot(wm[g], ds, TN))
            dvn = jnp.concatenate(dvn_parts, axis=1)
            glb_ref[...] += jnp.sum(dvn, axis=0, keepdims=True)
            glg_ref[...] += jnp.sum(dvn * xh, axis=0, keepdims=True)
            dxh = dvn * lg
            dv = rstd * (dxh - jnp.mean(dxh, axis=-1, keepdims=True)
                         - xh * jnp.mean(dxh * xh, axis=-1, keepdims=True))
            dp_ref[rows, BW:2 * BW] = (dv * dv_fac).astype(BF16)

    small = pl.BlockSpec((NG, CH, CH), lambda i: (0, 0, 0))
    vec = pl.BlockSpec((1, BW), lambda i: (0, 0))
    return pl.pallas_call(
        body, name="gmlp_bwd", grid=(t // tm,),
        in_specs=[pl.BlockSpec((tm, BW), lambda i: (i, 0)),
                  pl.BlockSpec((tm, BW), lambda i: (i, 1)),
                  pl.BlockSpec((tm, BW), lambda i: (i, 0)),
                  vec, vec, small, pl.BlockSpec((CH, NG), lambda i: (0, 0)), _HBM],
        out_specs=[pl.BlockSpec((tm, 2 * BW), lambda i: (i, 0)), small, small, vec, vec],
        out_shape=[jax.ShapeDtypeStruct((t, IN_COLS), BF16),
                   jax.ShapeDtypeStruct((NG, CH, CH), F32), jax.ShapeDtypeStruct((NG, CH, CH), F32),
                   jax.ShapeDtypeStruct((1, BW), F32), jax.ShapeDtypeStruct((1, BW), F32)],
        input_output_aliases={7: 0},
        compiler_params=_cp(("arbitrary",)),
    )(proj, proj, dout, ln_g, ln_b, ws, bst, dproj)


def _pool_bands():
    row = lax.broadcasted_iota(jnp.int32, (CH, CH), 0)
    col = lax.broadcasted_iota(jnp.int32, (CH, CH), 1)
    cur, prev = [], []
    for w in POOL_WINDOWS:
        cur.append(jnp.where((row >= col) & (row - col < w), 1.0, 0.0).astype(BF16))
        prev.append(jnp.where(row + CH - col < w, 1.0, 0.0).astype(BF16))
    return cur, prev


def _pool_inv_count(r0, w):
    pos = r0 + lax.broadcasted_iota(jnp.int32, (CH, 1), 0)
    return 1.0 / jnp.minimum(pos + 1, w).astype(F32)


def _pool_diff(x_ref, r0, rp, has_prev, cur, prev, g):
    cols = slice(g * CH, (g + 1) * CH)
    xc = x_ref[pl.ds(r0, CH), cols]
    xp = x_ref[pl.ds(rp, CH), cols]
    ws = _dot(cur[g], xc) + has_prev * _dot(prev[g], xp)
    return ws * _pool_inv_count(r0, POOL_WINDOWS[g]) - xc.astype(F32)


def _pool_fwd(proj3, pw, pscale):
    nb, seq, _ = proj3.shape
    nch = seq // CH

    def body(x_ref, pw_ref, ps_ref, o_ref):
        cur, prev = _pool_bands()
        pwb = [pw_ref[g].astype(BF16) for g in range(NG)]

        def chunk(ch, carry):
            r0 = pl.multiple_of(ch * CH, CH)
            rp = pl.multiple_of(jnp.maximum(ch - 1, 0) * CH, CH)
            has_prev = jnp.where(ch > 0, 1.0, 0.0)
            for g in range(NG):
                cols = slice(g * CH, (g + 1) * CH)
                d = _pool_diff(x_ref, r0, rp, has_prev, cur, prev, g)
                y = _dot(d.astype(BF16), pwb[g]) * ps_ref[:, cols]
                o_ref[pl.ds(r0, CH), cols] = y.astype(BF16)
            return carry

        lax.fori_loop(0, nch, chunk, 0, unroll=2)

    return pl.pallas_call(
        body, name="pool_fwd", grid=(nb,),
        in_specs=[pl.BlockSpec((None, seq, BW), lambda b: (b, 0, 5)),
                  pl.BlockSpec((NG, CH, CH), lambda b: (0, 0, 0)),
                  pl.BlockSpec((1, BW), lambda b: (0, 0))],
        out_specs=pl.BlockSpec((None, seq, BW), lambda b: (b, 0, 0)),
        out_shape=jax.ShapeDtypeStruct((nb, seq, BW), BF16),
        compiler_params=_cp(("parallel",)),
    )(proj3, pw, pscale)


def _pool_bwd(proj3, dout3, pw, pscale, dproj3):
    nb, seq, _ = proj3.shape
    nch = seq // CH

    def body(x_ref, do_ref, pw_ref, ps_ref, buf_ref, dx_ref, gpw_ref, gps_ref, e_ref):
        @pl.when(pl.program_id(0) == 0)
        def _():
            gpw_ref[...] = jnp.zeros_like(gpw_ref)
            gps_ref[...] = jnp.zeros_like(gps_ref)

        cur, prev = _pool_bands()
        pwb = [pw_ref[g].astype(BF16) for g in range(NG)]

        def first(ch, carry):
            r0 = pl.multiple_of(ch * CH, CH)
            rp = pl.multiple_of(jnp.maximum(ch - 1, 0) * CH, CH)
            has_prev = jnp.where(ch > 0, 1.0, 0.0)
            for g in range(NG):
                cols = slice(g * CH, (g + 1) * CH)
                d = _pool_diff(x_ref, r0, rp, has_prev, cur, prev, g).astype(BF16)
                do = do_ref[pl.ds(r0, CH), cols].astype(F32)
                ypre = _dot(d, pwb[g])
                gps_ref[:, cols] += jnp.sum(do * ypre, axis=0, keepdims=True)
                dyp = (do * ps_ref[:, cols]).astype(BF16)
                gpw_ref[g] += _dot(d, dyp, TN)
                e_ref[pl.ds(r0, CH), cols] = _dot(dyp, pwb[g], NT)
            return carry

        lax.fori_loop(0, nch, first, 0, unroll=2)

        def second(ch, carry):
            r0 = pl.multiple_of(ch * CH, CH)
            rn = pl.multiple_of(jnp.minimum(ch + 1, nch - 1) * CH, CH)
            has_next = jnp.where(ch < nch - 1, 1.0, 0.0)
            for g in range(NG):
                cols = slice(g * CH, (g + 1) * CH)
                w = POOL_WINDOWS[g]
                dd = e_ref[pl.ds(r0, CH), cols]
                ec = (dd * _pool_inv_count(r0, w)).astype(BF16)
                en = (e_ref[pl.ds(rn, CH), cols] * _pool_inv_count(rn, w)).astype(BF16)
                dx = _dot(cur[g], ec, TN) + has_next * _dot(prev[g], en, TN) - dd
                dx_ref[pl.ds(r0, CH), cols] = dx.astype(BF16)
            return carry

        lax.fori_loop(0, nch, second, 0, unroll=2)

    small = pl.BlockSpec((NG, CH, CH), lambda b: (0, 0, 0))
    vec = pl.BlockSpec((1, BW), lambda b: (0, 0))
    return pl.pallas_call(
        body, name="pool_bwd", grid=(nb,),
        in_specs=[pl.BlockSpec((None, seq, BW), lambda b: (b, 0, 5)),
                  pl.BlockSpec((None, seq, BW), lambda b: (b, 0, 0)), small, vec, _HBM],
        out_specs=[pl.BlockSpec((None, seq, BW), lambda b: (b, 0, 5)), small, vec],
        out_shape=[jax.ShapeDtypeStruct((nb, seq, IN_COLS), BF16),
                   jax.ShapeDtypeStruct((NG, CH, CH), F32), jax.ShapeDtypeStruct((1, BW), F32)],
        input_output_aliases={4: 0},
        scratch_shapes=[pltpu.VMEM((seq, BW), F32)],
        compiler_params=_cp(("arbitrary",)),
    )(proj3, dout3, pw, pscale, dproj3)


SB_BQ = 256
SB_BK = 256
SB_SCALE = HD ** -0.5


SB_EXIT = -110.0


def _sb_tile(qs, k, mask):
    z = _dot(qs, k, NT)
    lb = jnp.minimum(z, 0.0) - jnp.log(1.0 + jnp.exp(-jnp.abs(z)))
    lom = lb - z
    if mask is not None:
        lom = jnp.where(mask, lom, 0.0)
    return lb, lom


def _sb_alive(c):
    top = functools.reduce(jnp.maximum, [jnp.max(state[1]) for state in c])
    return (top > SB_EXIT).astype(jnp.int32)


def _sb_past_blocks(step, c, npast):
    def cond(s):
        return jnp.logical_and(s[0] < npast, s[1] > 0)

    def body(s):
        i, _, c = s
        c = step(pl.multiple_of((npast - 1 - i) * SB_BK, SB_BK), c, None)
        return i + 1, _sb_alive(c), c

    return lax.while_loop(cond, body, (jnp.int32(0), _sb_alive(c), c))[2]


def _sb_diag_mask(bq, d):
    row = lax.broadcasted_iota(jnp.int32, (bq, SB_BK), 0)
    col = lax.broadcasted_iota(jnp.int32, (bq, SB_BK), 1)
    return col + d * SB_BK < row


def _sb_scaled(q):
    return (q.astype(F32) * SB_SCALE).astype(BF16)


def _dot_tri(a, m):
    return _dot(a.astype(BF16), m)


def _dot_tri2(a, m):
    hi = a.astype(BF16)
    lo = (a - hi.astype(F32)).astype(BF16)
    return _dot(hi, m) + _dot(lo, m)


def _sb_fwd(proj3, gather=()):
    nb, seq, _ = proj3.shape
    bq = min(SB_BQ, seq)
    nq = seq // bq
    ndiag = bq // SB_BK

    def body(q_ref, k_ref, v_ref, o_ref):
        row = lax.broadcasted_iota(jnp.int32, (SB_BK, SB_BK), 0)
        col = lax.broadcasted_iota(jnp.int32, (SB_BK, SB_BK), 1)
        upper = jnp.where(row > col, 1.0, 0.0).astype(BF16)
        heads = [slice(hh * HD, (hh + 1) * HD) for hh in range(2)]

        def qloop(qi, carry):
            q0 = pl.multiple_of(qi * bq, bq)
            qs = [_sb_scaled(q_ref[pl.ds(q0, bq), lanes]) for lanes in heads]

            def step(k0, c, mask):
                tiles = [_sb_tile(q, k_ref[pl.ds(k0, SB_BK), lanes], mask) for lanes, q in zip(heads, qs)]
                sums = [_dot_tri(lom, upper) for _, lom in tiles]
                out = []
                for lanes, (acc, cr), (lb, lom), cs in zip(heads, c, tiles, sums):
                    a = jnp.exp(lb + (cs + cr))
                    if mask is not None:
                        a = jnp.where(mask, a, 0.0)
                    rsum = cs[:, 0:1] + lom[:, 0:1]
                    out.append((acc + _dot(a.astype(BF16), v_ref[pl.ds(k0, SB_BK), lanes]), cr + rsum))
                return tuple(out)

            c = tuple((jnp.zeros((bq, HD), F32), jnp.zeros((bq, 1), F32)) for _ in heads)
            for d in reversed(range(ndiag)):
                c = step(pl.multiple_of(q0 + d * SB_BK, SB_BK), c, _sb_diag_mask(bq, d))
            c = _sb_past_blocks(step, c, qi * ndiag)
            for lanes, (acc, _) in zip(heads, c):
                o_ref[pl.ds(q0, bq), lanes] = acc
            return carry

        lax.fori_loop(0, nq, qloop, 0)

    def spec(c0):
        return pl.BlockSpec((None, seq, 128), lambda b, p: (b, 0, c0 + p))

    grid = (nb, BW // 128)
    body, ex_in, ex_out, ex_shape, ex_sems = _host_exchange(body, 3, 1, grid, gather, False)
    outs = pl.pallas_call(
        body, name="sb_fwd", grid=grid,
        in_specs=[spec(8), spec(12), spec(16)] + ex_in,
        out_specs=[spec(0)] + ex_out,
        out_shape=[jax.ShapeDtypeStruct((nb, seq, BW), F32)] + ex_shape,
        scratch_shapes=ex_sems,
        compiler_params=_cp(("arbitrary", "arbitrary")),
    )(proj3, proj3, proj3, *gather)
    return outs[0], outs[1:]


def _sb_bwd(proj3, do3, o3, scatter=()):
    nb, seq, _ = proj3.shape
    bq = min(SB_BQ, seq)
    nq = seq // bq
    ndiag = bq // SB_BK
    assert ndiag == 1

    def body(q_ref, k_ref, v_ref, do_ref, o_ref, dq_ref, dk_ref, dv_ref, dk_acc, dv_acc):
        row = lax.broadcasted_iota(jnp.int32, (SB_BK, SB_BK), 0)
        col = lax.broadcasted_iota(jnp.int32, (SB_BK, SB_BK), 1)
        upper = jnp.where(row > col, 1.0, 0.0).astype(BF16)
        later = jnp.where(row >= col, 1.0, 0.0).astype(BF16)
        heads = [slice(hh * HD, (hh + 1) * HD) for hh in range(2)]

        def qloop(qi, carry):
            q0 = pl.multiple_of(qi * bq, bq)
            qs = [_sb_scaled(q_ref[pl.ds(q0, bq), lanes]) for lanes in heads]
            dos = [do_ref[pl.ds(q0, bq), lanes] for lanes in heads]
            gtot = [jnp.sum(do.astype(F32) * o_ref[pl.ds(q0, bq), lanes], axis=1, keepdims=True)
                    for do, lanes in zip(dos, heads)]

            def step(k0, c, mask):
                ks = [k_ref[pl.ds(k0, SB_BK), lanes] for lanes in heads]
                tiles = [_sb_tile(q, k, mask) for q, k in zip(qs, ks)]
                sums = [_dot_tri(lom, upper) for _, lom in tiles]
                das = [_dot(do, v_ref[pl.ds(k0, SB_BK), lanes], NT) for do, lanes in zip(dos, heads)]
                gls, avs = [], []
                for hh, (_, cr, _) in enumerate(c):
                    a = jnp.exp(tiles[hh][0] + (sums[hh] + cr))
                    if mask is not None:
                        a = jnp.where(mask, a, 0.0)
                    ab = a.astype(BF16)
                    avs.append(ab)
                    gls.append(das[hh] * ab.astype(F32))
                tails = [_dot_tri2(gl, later) for gl in gls]
                out = []
                for hh, (dq, cr, gdone) in enumerate(c):
                    lb, lom = tiles[hh]
                    pre = gtot[hh] - gdone - tails[hh]
                    dz = gls[hh] - jnp.exp(lb) * (gls[hh] + pre)
                    if mask is not None:
                        dz = jnp.where(mask, dz, 0.0)
                    dz = dz.astype(BF16)
                    dk_new, dv_new = _dot(dz, qs[hh], TN), _dot(avs[hh], dos[hh], TN)
                    if mask is not None:
                        dk_acc[hh, pl.ds(k0, SB_BK), :] = dk_new
                        dv_acc[hh, pl.ds(k0, SB_BK), :] = dv_new
                    else:
                        dk_acc[hh, pl.ds(k0, SB_BK), :] += dk_new
                        dv_acc[hh, pl.ds(k0, SB_BK), :] += dv_new
                    rsum = sums[hh][:, 0:1] + lom[:, 0:1]
                    out.append((dq + _dot(dz, ks[hh]), cr + rsum, gdone + tails[hh][:, 0:1]))
                return tuple(out)

            c = tuple((jnp.zeros((bq, HD), F32), jnp.zeros((bq, 1), F32), jnp.zeros((bq, 1), F32))
                      for _ in heads)
            for d in reversed(range(ndiag)):
                c = step(pl.multiple_of(q0 + d * SB_BK, SB_BK), c, _sb_diag_mask(bq, d))
            c = _sb_past_blocks(step, c, qi * ndiag)
            for lanes, (dq, _, _) in zip(heads, c):
                dq_ref[pl.ds(q0, bq), lanes] = (dq * SB_SCALE).astype(BF16)
            return carry

        lax.fori_loop(0, nq, qloop, 0)
        for hh in range(2):
            lanes = slice(hh * HD, (hh + 1) * HD)
            dk_ref[:, lanes] = dk_acc[hh].astype(BF16)
            dv_ref[:, lanes] = dv_acc[hh].astype(BF16)

    def spec(c0):
        return pl.BlockSpec((None, seq, 128), lambda b, p: (b, 0, c0 + p))

    grid = (nb, BW // 128)
    body, ex_in, ex_out, ex_shape, ex_sems = _host_exchange(body, 5, 3, grid, scatter, True)
    outs = pl.pallas_call(
        body, name="sb_bwd", grid=grid,
        in_specs=[spec(8), spec(12), spec(16), spec(0), spec(0)] + ex_in,
        out_specs=[spec(0), spec(0), spec(0)] + ex_out,
        out_shape=[jax.ShapeDtypeStruct((nb, seq, BW), BF16)] * 3 + ex_shape,
        scratch_shapes=[pltpu.VMEM((2, seq, HD), F32), pltpu.VMEM((2, seq, HD), F32)] + ex_sems,
        compiler_params=_cp(("arbitrary", "arbitrary")),
    )(proj3, proj3, proj3, do3, o3, *scatter)
    return outs[:3], outs[3:]


def _merge_fwd(brs, wb, proj):
    t = proj.shape[0]
    tm = _row_tile(t, 512)
    tn = 512
    nj = D // tn

    def body(b0, b1, b2, wb_ref, l0, l1, l2, m_ref, y0, y1, y2):
        acc = None
        for br, n, lg, y_ref in ((b0, 0, l0, y0), (b1, 1, l1, y1), (b2, 2, l2, y2)):
            y = _dot(br[...].astype(BF16), wb_ref[n])
            y_ref[...] = y.astype(BF16)
            term = jax.nn.sigmoid(lg[...].astype(F32)) * y
            acc = term if acc is None else acc + term
        m_ref[...] = acc.astype(BF16)

    def lspec(n):
        return pl.BlockSpec((tm, tn), lambda i, j: (i, (3 * D + n * D) // tn + j))

    tile = pl.BlockSpec((tm, tn), lambda i, j: (i, j))
    bspec = pl.BlockSpec((tm, BW), lambda i, j: (i, 0))
    return pl.pallas_call(
        body, name="merge_fwd", grid=(t // tm, nj),
        in_specs=[bspec, bspec, bspec, pl.BlockSpec((NB, BW, tn), lambda i, j: (0, 0, j)),
                  lspec(0), lspec(1), lspec(2)],
        out_specs=[tile] * 4,
        out_shape=[jax.ShapeDtypeStruct((t, D), BF16)] * 4,
        compiler_params=_cp(("parallel", "parallel")),
    )(brs[0], brs[1], brs[2], wb, proj, proj, proj)


def _merge_bwd(dm, ys, proj):
    t = proj.shape[0]
    tm = _row_tile(t, 512)

    def body(dm_ref, y0, y1, y2, lg_ref, dp_ref, dy0, dy1, dy2):
        dmv = dm_ref[...].astype(F32)
        for n, (y_ref, dy_ref) in enumerate(((y0, dy0), (y1, dy1), (y2, dy2))):
            cols = slice(n * D, (n + 1) * D)
            g = jax.nn.sigmoid(lg_ref[:, cols].astype(F32))
            dp_ref[:, cols] = (dmv * y_ref[...].astype(F32) * g * (1.0 - g)).astype(BF16)
            dy_ref[...] = (dmv * g).astype(BF16)

    tile = pl.BlockSpec((tm, D), lambda i: (i, 0))
    gates = pl.BlockSpec((tm, NB * D), lambda i: (i, 1))
    return pl.pallas_call(
        body, name="merge_bwd", grid=(t // tm,),
        in_specs=[tile] * 4 + [gates],
        out_specs=[gates] + [tile] * 3,
        out_shape=[jax.ShapeDtypeStruct((t, IN_COLS), BF16)] + [jax.ShapeDtypeStruct((t, D), BF16)] * 3,
        compiler_params=_cp(("parallel",)),
    )(dm, ys[0], ys[1], ys[2], proj)


def _adamw_rows(rows):
    if rows <= 512:
        return rows
    return next(tr for tr in (512, 384, 352, 256, 128, 64, 32, 16, 8) if rows % tr == 0)


def _adamw_math(npart, p_ref, w_ref, m_ref, v_ref, g_ref, d_ref, mo_ref, vo_ref):
    c1 = 1.0 - ADAM_B1 ** ADAM_STEP
    c2 = 1.0 - ADAM_B2 ** ADAM_STEP
    g = p_ref[0].astype(F32)
    for p in range(1, npart):
        g = g + p_ref[p].astype(F32)
    mn = ADAM_B1 * m_ref[...] + (1.0 - ADAM_B1) * g
    vn = ADAM_B2 * v_ref[...] + (1.0 - ADAM_B2) * (g * g)
    m_hat = mn / c1
    v_hat = vn / c2
    g_ref[...] = g
    d_ref[...] = -ADAM_LR * (m_hat / (jnp.sqrt(v_hat) + ADAM_EPS) + ADAM_WD * w_ref[...])
    mo_ref[...] = mn
    vo_ref[...] = vn


def _adamw_layer(name, parts, w, m, v, layer, bufs, padded=False):
    nl, cols = w.shape[0], w.shape[-1]
    rows = int(math.prod(w.shape[1:-1]))
    npart = parts.shape[0]
    tr = _adamw_rows(rows)
    if padded:
        assert parts.shape[1] == rows // tr and parts.shape[2] >= tr
        parts_spec = pl.BlockSpec((npart, None, tr, cols), lambda i: (0, i, 0, 0))
    else:
        parts = parts.reshape(npart, rows, cols)
        parts_spec = pl.BlockSpec((npart, tr, cols), lambda i: (0, i, 0))
    if bufs is None:
        bufs = [lax.empty((nl, rows, cols), F32) for _ in range(4)]

    def body(p_ref, w_ref, m_ref, v_ref, b0, b1, b2, b3, g_ref, d_ref, mo_ref, vo_ref):
        _adamw_math(npart, p_ref, w_ref, m_ref, v_ref, g_ref, d_ref, mo_ref, vo_ref)

    slab = pl.BlockSpec((None, tr, cols), lambda i: (layer, i, 0))
    sds = jax.ShapeDtypeStruct((nl, rows, cols), F32)
    return pl.pallas_call(
        body, name=name, grid=(rows // tr,),
        in_specs=[parts_spec, slab, slab, slab] + [_HBM] * 4,
        out_specs=[slab] * 4, out_shape=[sds] * 4,
        input_output_aliases={4: 0, 5: 1, 6: 2, 7: 3},
        compiler_params=_cp(("parallel",)),
    )(parts, w.reshape(nl, rows, cols), m.reshape(nl, rows, cols), v.reshape(nl, rows, cols), *bufs)


def _adamw_reduce(name, parts, w, m, v):
    shape = w.shape
    cols = shape[-1]
    rows = int(math.prod(shape[:-1])) if len(shape) > 1 else 1
    npart = parts.shape[0]
    tr = _adamw_rows(rows)

    def body(p_ref, w_ref, m_ref, v_ref, g_ref, d_ref, mo_ref, vo_ref):
        _adamw_math(npart, p_ref, w_ref, m_ref, v_ref, g_ref, d_ref, mo_ref, vo_ref)

    tile = pl.BlockSpec((tr, cols), lambda i: (i, 0))
    sds = jax.ShapeDtypeStruct((rows, cols), F32)
    outs = pl.pallas_call(
        body, name=name, grid=(rows // tr,),
        in_specs=[pl.BlockSpec((npart, tr, cols), lambda i: (0, i, 0)), tile, tile, tile],
        out_specs=[tile] * 4, out_shape=[sds] * 4,
        compiler_params=_cp(("parallel",)),
    )(parts.reshape(npart, rows, cols), w.reshape(rows, cols), m.reshape(rows, cols), v.reshape(rows, cols))
    return tuple(o.reshape(shape) for o in outs)


def _pad_ffn_in(w):
    lead = w.shape[:-1]
    w = w.reshape(lead + (2, FF_HALF))
    w = jnp.pad(w, [(0, 0)] * len(lead) + [(0, 0), (0, FF_HALF_PAD - FF_HALF)])
    return w.reshape(lead + (FF_IN_PAD,))


def kernel(x, c, rms_g1, rms_g2, w_ada, b_ada, w_in, gm_ln_g, gm_ln_b, gm_w_spatial, gm_b_spatial, pool_w, pool_scale, w_branch, w_out, w_ffn_in, w_ffn_out, final_g, loss_target, m_rms_g1, m_rms_g2, m_w_ada, m_b_ada, m_w_in, m_gm_ln_g, m_gm_ln_b, m_gm_w_spatial, m_gm_b_spatial, m_pool_w, m_pool_scale, m_w_branch, m_w_out, m_w_ffn_in, m_w_ffn_out, m_final_g, v_rms_g1, v_rms_g2, v_w_ada, v_b_ada, v_w_in, v_gm_ln_g, v_gm_ln_b, v_gm_w_spatial, v_gm_b_spatial, v_pool_w, v_pool_scale, v_w_branch, v_w_out, v_w_ffn_in, v_w_ffn_out, v_final_g):
    nb, seq, _ = x.shape
    nl = w_in.shape[0]
    t = nb * seq
    ntot = NDEV * nb
    me = _my_index()
    assert x.shape[2] == D and w_in.shape[1:] == (D, 768) and w_ffn_in.shape[1:] == (D, FF_IN_SHARD)
    assert seq % CH == 0

    w_ffn_in_p = _pad_ffn_in(w_ffn_in).astype(BF16)
    w_ffn_out_p = jnp.pad(w_ffn_out, ((0, 0), (0, FF_HALF_PAD - FF_HALF), (0, 0))).astype(BF16)
    w_in_b = w_in.astype(BF16)
    w_branch_b = w_branch.astype(BF16)
    w_out_b = w_out.astype(BF16)
    (g_in_next,) = _exchange([w_in_b[0]], "gather_w_in0", False)

    (c_all,) = _exchange([c], "gather_c", False)
    c_all = c_all.reshape(ntot, D)
    b_blk = lax.dynamic_slice_in_dim(b_ada, me * 768, 768, axis=1).reshape(nl, 1, 768)
    mod_blk = _ada_fwd(c_all, w_ada, b_blk)
    (mod_all,) = _exchange([mod_blk], "gather_mod", False)
    mod_all = jnp.transpose(mod_all, (1, 2, 0, 3)).reshape(nl, ntot, NMOD * D)
    mod = lax.dynamic_slice_in_dim(mod_all, me * nb, nb, axis=1).reshape(nl, nb, NMOD, 1, D)

    saved = []
    gathered = []
    xc = x
    for l in range(nl):
        sh1, sc1, gt1, sh2, sc2, gt2 = [mod[l, :, i] for i in range(NMOD)]
        h = _norm_mod_fwd(xc, rms_g1[l].reshape(1, D), sc1, sh1).reshape(t, D)
        proj, (g_ffn_in_w,) = _mm_colblocked("proj_fwd", h, g_in_next, BF16, [w_ffn_in_p[l]])
        proj3 = proj.reshape(nb, seq, IN_COLS)
        br_gm = _gmlp_fwd(proj, gm_ln_g[l].reshape(1, BW), gm_ln_b[l].reshape(1, BW),
                          gm_w_spatial[l], gm_b_spatial[l].T)
        sb_o, got = _sb_fwd(proj3, [w_branch_b[l], w_out_b[l]] + ([w_in_b[l + 1]] if l + 1 < nl else []))
        gw = dict(w_in=g_in_next,
                  w_branch=jnp.transpose(got[0], (1, 2, 0, 3)).reshape(NB, BW, D),
                  w_out=got[1].reshape(D, D),
                  w_ffn_in=g_ffn_in_w)
        gathered.append(gw)
        if l + 1 < nl:
            g_in_next = got[2]
        br_pool = _pool_fwd(proj3, pool_w[l], pool_scale[l].reshape(1, BW))
        brs = [br_gm, sb_o.reshape(t, BW), br_pool.reshape(t, BW)]
        merged, y0, y1, y2 = _merge_fwd(brs, gw["w_branch"], proj)
        x_mid, mo = _mm_residual("out_fwd", merged, gw["w_out"], xc.reshape(t, D), gt1, seq)
        x_mid = x_mid.reshape(nb, seq, D)
        h2 = _norm_mod_fwd(x_mid, rms_g2[l].reshape(1, D), sc2, sh2).reshape(t, D)
        fg, fu, act, got = _ffn_in_fwd(h2, gw["w_ffn_in"], [w_ffn_out_p[l]])
        gw["w_ffn_out"] = got[0].reshape(FFP, D)
        x_out, fo = _mm_residual("ffn_out_fwd", act, gw["w_ffn_out"], x_mid.reshape(t, D), gt2, seq)
        saved.append(dict(x_in=xc, h=h, proj=proj, brs=brs, sb_o=sb_o, ys=(y0, y1, y2), merged=merged,
                          mo=mo, x_mid=x_mid, h2=h2, fg=fg, fu=fu, act=act, fo=fo))
        xc = x_out.reshape(nb, seq, D)

    dx, loss_part, dfinal_part, dfo, dgt2 = _loss_head(xc, loss_target, final_g.reshape(1, D),
                                                       saved[-1]["fo"].reshape(nb, seq, D), mod[nl - 1, :, 5])
    loss = lax.psum(jnp.sum(loss_part[:, 0, 0]), ("x", "y", "c"))

    big_names = ("w_in", "w_branch", "w_out", "w_ffn_in", "w_ffn_out")
    bufs = {name: None for name in big_names}
    w_ffn_in_t, m_w_ffn_in_t, v_w_ffn_in_t = [jnp.swapaxes(a, 1, 2) for a in (w_ffn_in, m_w_ffn_in, v_w_ffn_in)]
    small_parts = {k: [None] * nl for k in ("rms_g1", "rms_g2", "gm_ln_g", "gm_ln_b", "gm_w_spatial",
                                            "gm_b_spatial", "pool_w", "pool_scale")}
    dmod = [None] * nl
    for l in reversed(range(nl)):
        gw = gathered[l]
        sv = saved[l]
        sh1, sc1, gt1, sh2, sc2, gt2 = [mod[l, :, i] for i in range(NMOD)]
        dfo = dfo.reshape(t, D)
        g_ffn_out = _mm_tn("ffn_out_wgrad", sv["act"], dfo)
        dfg, dfu = _ffn_out_dgrad(dfo, gw["w_ffn_out"], sv["fg"], sv["fu"])
        dh2 = _ffn_in_dgrad(dfg, dfu, gw["w_ffn_in"])
        g_ffn_in = _ffn_in_wgrad(sv["h2"], dfg, dfu)
        dx_mid, dsh2, dsc2, dg2, dmo, dgt1 = _norm_mod_bwd(
            sv["x_mid"], dh2.reshape(nb, seq, D), dx, rms_g2[l].reshape(1, D), sc2,
            gate=(sv["mo"].reshape(nb, seq, D), gt1))
        dmo = dmo.reshape(t, D)
        dmerged = _mm_nt("out_dgrad", dmo, gw["w_out"], BF16)
        g_out = _mm_tn("out_wgrad", sv["merged"], dmo)
        dproj, *dys = _merge_bwd(dmerged, sv["ys"], sv["proj"])
        dbrs, g_br = [], []
        for n in range(NB):
            dbrs.append(_mm_nt("branch_dgrad", dys[n], gw["w_branch"], BF16, w_lead=n))
            g_br.append(_mm_tn("branch_wgrad", sv["brs"][n], dys[n]))
        proj3 = sv["proj"].reshape(nb, seq, IN_COLS)
        dproj, g_ws, g_bs, g_lg, g_lb = _gmlp_bwd(sv["proj"], dbrs[0], gm_ln_g[l].reshape(1, BW),
                                                  gm_ln_b[l].reshape(1, BW), gm_w_spatial[l], gm_b_spatial[l].T,
                                                  dproj)
        g_br_dev = jnp.transpose(jnp.stack(g_br).reshape(NB, BW, NDEV, D // NDEV), (2, 0, 1, 3))
        carried = [g_br_dev, g_out.reshape(NDEV, D // NDEV, D), g_ffn_in, g_ffn_out.reshape(NDEV, FF_HALF_PAD, D)]
        d_sb, recv = _sb_bwd(proj3, dbrs[1].reshape(nb, seq, BW), sv["sb_o"], carried)
        bufs["w_branch"] = _adamw_layer("adamw_w_branch", recv[0], w_branch, m_w_branch, v_w_branch, l,
                                        bufs["w_branch"])
        bufs["w_out"] = _adamw_layer("adamw_w_out", recv[1], w_out, m_w_out, v_w_out, l, bufs["w_out"])
        bufs["w_ffn_in"] = _adamw_layer("adamw_w_ffn_in", recv[2].reshape(NDEV, 2, FF_HALF_PAD, D), w_ffn_in_t,
                                        m_w_ffn_in_t, v_w_ffn_in_t, l, bufs["w_ffn_in"], padded=True)
        bufs["w_ffn_out"] = _adamw_layer("adamw_w_ffn_out", recv[3].reshape(NDEV, 1, FF_HALF_PAD, D), w_ffn_out,
                                         m_w_ffn_out, v_w_ffn_out, l, bufs["w_ffn_out"], padded=True)
        dproj3, g_pw, g_ps = _pool_bwd(proj3, dbrs[2].reshape(nb, seq, BW), pool_w[l], pool_scale[l].reshape(1, BW),
                                       dproj.reshape(nb, seq, IN_COLS))
        dproj = dproj3.reshape(t, IN_COLS)
        for i, piece in enumerate(d_sb):
            dproj = lax.dynamic_update_slice(dproj, piece.reshape(t, BW), (0, 2 * BW + i * BW))
        g_in = _mm_colblocked_tn("proj_wgrad", sv["h"], dproj)
        dh, (r_in,) = _mm_colblocked_nt("proj_dgrad", dproj, gw["w_in"], F32, [g_in])
        bufs["w_in"] = _adamw_layer("adamw_w_in", r_in, w_in, m_w_in, v_w_in, l, bufs["w_in"])
        dmod_tail = [dgt1, dsh2, dsc2, dgt2]
        if l > 0:
            dx, dsh1, dsc1, dg1, dfo, dgt2 = _norm_mod_bwd(
                sv["x_in"], dh.reshape(nb, seq, D), dx_mid, rms_g1[l].reshape(1, D), sc1,
                gate=(saved[l - 1]["fo"].reshape(nb, seq, D), mod[l - 1, :, 5]))
        else:
            dx, dsh1, dsc1, dg1 = _norm_mod_bwd(sv["x_in"], dh.reshape(nb, seq, D), dx_mid,
                                                rms_g1[l].reshape(1, D), sc1)

        dmod[l] = jnp.concatenate([dsh1, dsc1] + dmod_tail, axis=-1)
        small_parts["rms_g1"][l] = jnp.sum(dg1, axis=0)
        small_parts["rms_g2"][l] = jnp.sum(dg2, axis=0)
        small_parts["gm_ln_g"][l] = g_lg
        small_parts["gm_ln_b"][l] = g_lb
        small_parts["gm_w_spatial"][l] = g_ws
        small_parts["gm_b_spatial"][l] = g_bs[:, :, 0]
        small_parts["pool_w"][l] = g_pw
        small_parts["pool_scale"][l] = g_ps

    dmod_mine = jnp.stack(dmod).reshape(nl, nb, NMOD * D)
    names = list(small_parts)
    stacked = [jnp.stack(small_parts[k]).astype(BF16 if k in ("gm_w_spatial", "pool_w") else F32) for k in names]
    gathered_small = _exchange(stacked + [dfinal_part, dmod_mine], "gather_small", False)
    dmod_all = jnp.transpose(gathered_small[-1], (1, 0, 2, 3)).reshape(nl, ntot, NMOD * D)
    dfinal_all = gathered_small[-2].reshape(ntot, D)

    results = {}
    weights = dict(rms_g1=(rms_g1, m_rms_g1, v_rms_g1), rms_g2=(rms_g2, m_rms_g2, v_rms_g2),
                   gm_ln_g=(gm_ln_g, m_gm_ln_g, v_gm_ln_g), gm_ln_b=(gm_ln_b, m_gm_ln_b, v_gm_ln_b),
                   gm_w_spatial=(gm_w_spatial, m_gm_w_spatial, v_gm_w_spatial),
                   gm_b_spatial=(gm_b_spatial, m_gm_b_spatial, v_gm_b_spatial),
                   pool_w=(pool_w, m_pool_w, v_pool_w), pool_scale=(pool_scale, m_pool_scale, v_pool_scale))
    for k, parts in zip(names, gathered_small[:len(names)]):
        w, m, v = weights[k]
        results[k] = _adamw_reduce("adamw_" + k, parts.reshape((NDEV,) + w.shape), w, m, v)
    results["final_g"] = _adamw_reduce("adamw_final_g", dfinal_all, final_g, m_final_g, v_final_g)
    results["b_ada"] = _adamw_reduce("adamw_b_ada", jnp.transpose(dmod_all, (1, 0, 2)), b_ada, m_b_ada, v_b_ada)
    dmod_blk = lax.dynamic_slice_in_dim(dmod_all, me * 768, 768, axis=2)
    g_w_ada = _ada_bwd(c_all, dmod_blk)
    results["w_ada"] = _adamw_reduce("adamw_w_ada", g_w_ada[None], w_ada, m_w_ada, v_w_ada)
    stacked_w = dict(w_in=w_in, w_branch=w_branch, w_out=w_out, w_ffn_in=w_ffn_in_t, w_ffn_out=w_ffn_out)
    for name in big_names:
        results[name] = tuple(b.reshape(stacked_w[name].shape) for b in bufs[name])
    results["w_ffn_in"] = tuple(jnp.swapaxes(b, 1, 2) for b in results["w_ffn_in"])

    order = ["rms_g1", "rms_g2", "w_ada", "b_ada", "w_in", "gm_ln_g", "gm_ln_b", "gm_w_spatial", "gm_b_spatial",
             "pool_w", "pool_scale", "w_branch", "w_out", "w_ffn_in", "w_ffn_out", "final_g"]
    out = [loss, dx]
    for i in range(4):
        out.extend(results[k][i] for k in order)
    return tuple(out)
```

```python
import functools
import math

import jax
import jax.numpy as jnp
from jax import lax
from jax.experimental import pallas as pl
from jax.experimental.pallas import tpu as pltpu

F32 = jnp.float32
BF16 = jnp.bfloat16
MESH = pl.DeviceIdType.MESH

D = 1024
BW = 512
NB = 3
CH = 128
NG = 4
HD = 64
POOL_WINDOWS = (2, 4, 8, 16)
DFF = 2816
NMOD = 6
EPS = 1e-6
IN_COLS = 6 * D
NDEV = 8
FF_IN_SHARD = 2 * DFF // NDEV
FF_HALF = FF_IN_SHARD // 2
FF_HALF_PAD = 384
FF_IN_PAD = 2 * FF_HALF_PAD
FFP = NDEV // 2 * FF_IN_PAD

ADAM_LR = 0.001
ADAM_B1 = 0.9
ADAM_B2 = 0.999
ADAM_EPS = 1e-08
ADAM_WD = 0.01
ADAM_STEP = 10

VMEM_LIMIT = 48 * 1024 * 1024
BIG_ROWS = 2048
NN = (((1,), (0,)), ((), ()))
NT = (((1,), (1,)), ((), ()))
TN = (((0,), (0,)), ((), ()))


def _cp(sem=None):
    return pltpu.CompilerParams(dimension_semantics=sem, vmem_limit_bytes=VMEM_LIMIT)


def _dot(a, b, dims=NN):
    return lax.dot_general(a, b, dims, preferred_element_type=F32)


def _my_index():
    return 4 * lax.axis_index("x") + 2 * lax.axis_index("y") + lax.axis_index("c")


def _peer(k):
    x, y, c = lax.axis_index("x"), lax.axis_index("y"), lax.axis_index("c")
    px = 1 - x if k & 4 else x
    py = 1 - y if k & 2 else y
    pc = 1 - c if k & 1 else c
    return (px, py, pc), 4 * px + 2 * py + pc


def _exchange(xs, name, all_to_all):
    n = len(xs)

    def body(*refs):
        _exchange_start(refs[:n], refs[n:2 * n], refs[2 * n:], all_to_all)
        _exchange_relay(refs[:n], refs[n:2 * n], refs[2 * n:], all_to_all)
        _exchange_finish(refs[:n], refs[n:2 * n], refs[2 * n:], all_to_all)

    return pl.pallas_call(
        body, name=name, out_shape=_exchange_out_shape(xs, all_to_all),
        in_specs=[_HBM] * n, out_specs=[_HBM] * n, scratch_shapes=_exchange_sems(n),
    )(*xs)


_HBM = pl.BlockSpec(memory_space=pl.ANY)


def _exchange_out_shape(xs, all_to_all):
    if all_to_all:
        return [jax.ShapeDtypeStruct(x.shape, x.dtype) for x in xs]
    return [jax.ShapeDtypeStruct((NDEV,) + x.shape, x.dtype) for x in xs]


def _exchange_sems(n):
    return [pltpu.SemaphoreType.DMA((n * 7,)), pltpu.SemaphoreType.DMA((n * 7,)), pltpu.SemaphoreType.DMA((n,))]


def _all_to_all_copies(ins, outs, sems):
    send_sems, recv_sems, local_sems = sems
    me = _my_index()
    local, sends, recvs = [], [], []
    for a in range(len(ins)):
        local.append(pltpu.make_async_copy(ins[a].at[me], outs[a].at[me], local_sems.at[a]))
    for k in range(1, NDEV):
        dev, idx = _peer(k)
        for a in range(len(ins)):
            sem = dict(send_sem=send_sems.at[a * 7 + k - 1], recv_sem=recv_sems.at[a * 7 + k - 1],
                       device_id=dev, device_id_type=MESH)
            sends.append(pltpu.make_async_remote_copy(src_ref=ins[a].at[idx], dst_ref=outs[a].at[me], **sem))
            recvs.append(pltpu.make_async_remote_copy(src_ref=ins[a].at[idx], dst_ref=outs[a].at[idx], **sem))
    return local, sends, recvs


def _gather_copies(ins, outs, sems):
    send_sems, recv_sems, local_sems = sems
    x, y, c = lax.axis_index("x"), lax.axis_index("y"), lax.axis_index("c")
    me, other = 4 * x + 2 * y + c, 4 * x + 2 * y + (1 - c)
    other_dev = (x, y, 1 - c)
    chips = [(1 - x, y), (x, 1 - y), (1 - x, 1 - y)]
    local, own, relay, from_other = [], [], [], []
    for a in range(len(ins)):
        def copy(k, src, block, dev, a=a):
            return pltpu.make_async_remote_copy(
                src_ref=src, dst_ref=outs[a].at[block], send_sem=send_sems.at[a * 7 + k],
                recv_sem=recv_sems.at[a * 7 + k], device_id=dev, device_id_type=MESH)

        local.append(pltpu.make_async_copy(ins[a], outs[a].at[me], local_sems.at[a]))
        own.append(copy(0, ins[a], me, other_dev))
        from_other.append(copy(0, ins[a], other, other_dev))
        for j, (px, py) in enumerate(chips):
            far = 4 * px + 2 * py + c
            own.append(copy(1 + j, ins[a], me, (px, py, c)))
            relay.append((copy(1 + j, ins[a], far, (px, py, c)), copy(4 + j, outs[a].at[far], far, other_dev)))
            from_other.append(copy(4 + j, ins[a], 4 * px + 2 * py + (1 - c), other_dev))
    return local, own, relay, from_other


def _exchange_start(ins, outs, sems, all_to_all):
    local, sends = (_all_to_all_copies if all_to_all else _gather_copies)(ins, outs, sems)[:2]
    for cp in local + sends:
        cp.start()


def _exchange_relay(ins, outs, sems, all_to_all):
    if not all_to_all:
        for arrival, passing_on in _gather_copies(ins, outs, sems)[2]:
            arrival.wait_recv()
            passing_on.start()


def _exchange_finish(ins, outs, sems, all_to_all):
    if all_to_all:
        local, sends, recvs = _all_to_all_copies(ins, outs, sems)
    else:
        local, own, relay, recvs = _gather_copies(ins, outs, sems)
        sends = own + [passing_on for _, passing_on in relay]
    for cp in sends:
        cp.wait_send()
    for cp in recvs:
        cp.wait_recv()
    for cp in local:
        cp.wait()


def _host_exchange(body, n_in, n_out, grid, xs, all_to_all):
    n = len(xs)
    if n == 0:
        return body, [], [], [], []
    steps = math.prod(grid)
    half = steps * 3 // 4 if steps >= 4 else steps - 1

    def hosted(*refs):
        ins, ex_ins = refs[:n_in], refs[n_in:n_in + n]
        outs, ex_outs = refs[n_in + n:n_in + n + n_out], refs[n_in + n + n_out:n_in + 2 * n + n_out]
        scratch = refs[n_in + 2 * n + n_out:]
        own, sems = scratch[:len(scratch) - 3], scratch[len(scratch) - 3:]
        step = 0
        for a in range(len(grid)):
            step = step * grid[a] + pl.program_id(a)

        @pl.when(step == 0)
        def _():
            _exchange_start(ex_ins, ex_outs, sems, all_to_all)

        body(*ins, *outs, *own)

        @pl.when(step == half)
        def _():
            _exchange_relay(ex_ins, ex_outs, sems, all_to_all)

        @pl.when(step == steps - 1)
        def _():
            _exchange_finish(ex_ins, ex_outs, sems, all_to_all)

    return hosted, [_HBM] * n, [_HBM] * n, _exchange_out_shape(xs, all_to_all), _exchange_sems(n)


def _mm(name, a, b, grid, a_spec, b_spec, o_spec, out_sds, dims, acc_shape, carried=(), all_to_all=True):
    nk = grid[2]

    if nk == 1:
        def body(a_ref, b_ref, o_ref):
            o_ref[...] = _dot(a_ref[...].astype(BF16), b_ref[...].astype(BF16), dims).astype(o_ref.dtype)
        scratch = []
    else:
        def body(a_ref, b_ref, o_ref, acc_ref):
            k = pl.program_id(2)

            @pl.when(k == 0)
            def _():
                acc_ref[...] = jnp.zeros_like(acc_ref)

            acc_ref[...] += _dot(a_ref[...].astype(BF16), b_ref[...].astype(BF16), dims)

            @pl.when(k == nk - 1)
            def _():
                o_ref[...] = acc_ref[...].astype(o_ref.dtype)
        scratch = [pltpu.VMEM(acc_shape, F32)]

    body, ex_in, ex_out, ex_shape, ex_sems = _host_exchange(body, 2, 1, grid, carried, all_to_all)
    outs = pl.pallas_call(
        body, name=name, grid=grid, in_specs=[a_spec, b_spec] + ex_in, out_specs=[o_spec] + ex_out,
        out_shape=[out_sds] + ex_shape,
        scratch_shapes=scratch + ex_sems,
        compiler_params=_cp(("arbitrary",) * 3 if carried else ("parallel", "parallel", "arbitrary")),
    )(a, b, *carried)
    return (outs[0], outs[1:]) if carried else outs[0]


def _row_tile(t, want):
    tm = min(t, want)
    assert t % tm == 0
    return tm


def _mm_colblocked(name, a, wg, out_dtype, gather=()):
    t = a.shape[0]
    tm = _row_tile(t, BIG_ROWS)
    return _mm(name, a, wg, (t // tm, NDEV, 1),
               pl.BlockSpec((tm, D), lambda i, j, k: (i, 0)),
               pl.BlockSpec((None, D, 768), lambda i, j, k: (j, 0, 0)),
               pl.BlockSpec((tm, 768), lambda i, j, k: (i, j)),
               jax.ShapeDtypeStruct((t, NDEV * 768), out_dtype), NN, (tm, 768), gather, False)


def _mm_colblocked_nt(name, g, wg, out_dtype, scatter=()):
    t = g.shape[0]
    tm = _row_tile(t, BIG_ROWS)
    return _mm(name, g, wg, (t // tm, 1, NDEV),
               pl.BlockSpec((tm, 768), lambda i, j, k: (i, k)),
               pl.BlockSpec((None, D, 768), lambda i, j, k: (k, 0, 0)),
               pl.BlockSpec((tm, D), lambda i, j, k: (i, 0)),
               jax.ShapeDtypeStruct((t, D), out_dtype), NT, (tm, D), scatter)


_HALF = NDEV // 2


def _ffn_in_fwd(h2, wg, gather=()):
    t = h2.shape[0]
    tm = _row_tile(t, 1024)

    def body(a_ref, wg_ref, wu_ref, g_ref, u_ref, act_ref):
        a = a_ref[...]
        g = _dot(a, wg_ref[...])
        u = _dot(a, wu_ref[...])
        g_ref[...] = g.astype(BF16)
        u_ref[...] = u.astype(BF16)
        act_ref[...] = (g * jax.nn.sigmoid(g) * u).astype(BF16)

    tile = pl.BlockSpec((tm, 768), lambda i, j: (i, j))
    grid = (t // tm, _HALF)
    body, ex_in, ex_out, ex_shape, ex_sems = _host_exchange(body, 3, 3, grid, gather, False)
    outs = pl.pallas_call(
        body, name="ffn_in_fwd", grid=grid,
        in_specs=[pl.BlockSpec((tm, D), lambda i, j: (i, 0)),
                  pl.BlockSpec((None, D, 768), lambda i, j: (j, 0, 0)),
                  pl.BlockSpec((None, D, 768), lambda i, j: (j + _HALF, 0, 0))] + ex_in,
        out_specs=[tile] * 3 + ex_out, out_shape=[jax.ShapeDtypeStruct((t, FFP), BF16)] * 3 + ex_shape,
        scratch_shapes=ex_sems,
        compiler_params=_cp(("arbitrary", "arbitrary") if gather else ("parallel", "parallel")),
    )(h2, wg, wg, *gather)
    return outs[0], outs[1], outs[2], outs[3:]


def _ffn_out_dgrad(dfo, w, fg, fu):
    t = dfo.shape[0]
    tm = _row_tile(t, 1024)

    def body(a_ref, w_ref, g_ref, u_ref, dg_ref, du_ref):
        d = _dot(a_ref[...], w_ref[...], NT)
        g = g_ref[...].astype(F32)
        s = jax.nn.sigmoid(g)
        gs = g * s
        dg_ref[...] = (d * u_ref[...].astype(F32) * (s + gs * (1.0 - s))).astype(BF16)
        du_ref[...] = (d * gs).astype(BF16)

    tile = pl.BlockSpec((tm, 768), lambda i, j: (i, j))
    return pl.pallas_call(
        body, name="ffn_out_dgrad", grid=(t // tm, _HALF),
        in_specs=[pl.BlockSpec((tm, D), lambda i, j: (i, 0)),
                  pl.BlockSpec((768, D), lambda i, j: (j, 0)), tile, tile],
        out_specs=[tile] * 2, out_shape=[jax.ShapeDtypeStruct((t, FFP), BF16)] * 2,
        compiler_params=_cp(("parallel", "parallel")),
    )(dfo, w, fg, fu)


def _ffn_in_dgrad(dg, du, wg):
    t = dg.shape[0]
    tm = _row_tile(t, BIG_ROWS)

    def body(g_ref, u_ref, w_ref, o_ref, acc_ref):
        k = pl.program_id(1)

        @pl.when(k == 0)
        def _():
            acc_ref[...] = jnp.zeros_like(acc_ref)

        @pl.when(k < _HALF)
        def _():
            acc_ref[...] += _dot(g_ref[...], w_ref[...], NT)

        @pl.when(k >= _HALF)
        def _():
            acc_ref[...] += _dot(u_ref[...], w_ref[...], NT)

        @pl.when(k == NDEV - 1)
        def _():
            o_ref[...] = acc_ref[...]

    return pl.pallas_call(
        body, name="ffn_in_dgrad", grid=(t // tm, NDEV),
        in_specs=[pl.BlockSpec((tm, 768), lambda i, k: (i, jnp.minimum(k, _HALF - 1))),
                  pl.BlockSpec((tm, 768), lambda i, k: (i, jnp.maximum(k - _HALF, 0))),
                  pl.BlockSpec((None, D, 768), lambda i, k: (k, 0, 0))],
        out_specs=pl.BlockSpec((tm, D), lambda i, k: (i, 0)),
        out_shape=jax.ShapeDtypeStruct((t, D), F32),
        scratch_shapes=[pltpu.VMEM((tm, D), F32)],
        compiler_params=_cp(("parallel", "arbitrary")),
    )(dg, du, wg)


def _ffn_in_wgrad(h2, dg, du):
    t = h2.shape[0]
    tk = _row_tile(t, BIG_ROWS)
    nk = t // tk

    def body(a_ref, g_ref, u_ref, o_ref, acc_ref):
        j, k = pl.program_id(0), pl.program_id(1)

        @pl.when(k == 0)
        def _():
            acc_ref[...] = jnp.zeros_like(acc_ref)

        @pl.when(j < _HALF)
        def _():
            acc_ref[...] += _dot(g_ref[...], a_ref[...], TN)

        @pl.when(j >= _HALF)
        def _():
            acc_ref[...] += _dot(u_ref[...], a_ref[...], TN)

        @pl.when(k == nk - 1)
        def _():
            o_ref[...] = acc_ref[...].astype(BF16)

    return pl.pallas_call(
        body, name="ffn_in_wgrad", grid=(NDEV, nk),
        in_specs=[pl.BlockSpec((tk, D), lambda j, k: (k, 0)),
                  pl.BlockSpec((tk, 768), lambda j, k: (jnp.where(j < _HALF, k, 0), jnp.minimum(j, _HALF - 1))),
                  pl.BlockSpec((tk, 768), lambda j, k: (jnp.where(j < _HALF, 0, k), jnp.maximum(j - _HALF, 0)))],
        out_specs=pl.BlockSpec((None, 768, D), lambda j, k: (j, 0, 0)),
        out_shape=jax.ShapeDtypeStruct((NDEV, 768, D), BF16),
        scratch_shapes=[pltpu.VMEM((768, D), F32)],
        compiler_params=_cp(("parallel", "arbitrary")),
    )(h2, dg, du)


def _mm_colblocked_tn(name, a, g):
    t = a.shape[0]
    tk = _row_tile(t, BIG_ROWS)
    return _mm(name, a, g, (1, NDEV, t // tk),
               pl.BlockSpec((tk, D), lambda i, j, k: (k, 0)),
               pl.BlockSpec((tk, 768), lambda i, j, k: (k, j)),
               pl.BlockSpec((None, D, 768), lambda i, j, k: (j, 0, 0)),
               jax.ShapeDtypeStruct((NDEV, D, 768), BF16), TN, (D, 768))


def _mm_nt(name, a, w, out_dtype, a_col=0, w_lead=None):
    t = a.shape[0]
    if w_lead is None:
        kdim, n = w.shape
        b_spec = pl.BlockSpec((min(kdim, 1024), n), lambda i, j, k: (j, 0))
    else:
        _, kdim, n = w.shape
        b_spec = pl.BlockSpec((None, min(kdim, 1024), n), lambda i, j, k: (w_lead, j, 0))
    tn = min(kdim, 1024)
    tm = _row_tile(t, BIG_ROWS)
    return _mm(name, a, w, (t // tm, kdim // tn, 1),
               pl.BlockSpec((tm, n), lambda i, j, k: (i, a_col)),
               b_spec,
               pl.BlockSpec((tm, tn), lambda i, j, k: (i, j)),
               jax.ShapeDtypeStruct((t, kdim), out_dtype), NT, (tm, tn))


def _mm_tn(name, a, g, out_dtype=BF16):
    t, kdim = a.shape
    n = g.shape[1]
    tk = _row_tile(t, BIG_ROWS)
    tm = min(kdim, 1024)
    tn = min(n, 1024)
    return _mm(name, a, g, (kdim // tm, n // tn, t // tk),
               pl.BlockSpec((tk, tm), lambda i, j, k: (k, i)),
               pl.BlockSpec((tk, tn), lambda i, j, k: (k, j)),
               pl.BlockSpec((tm, tn), lambda i, j, k: (i, j)),
               jax.ShapeDtypeStruct((kdim, n), out_dtype), TN, (tm, tn))


def _mm_residual(name, a, w, x, gt, seq):
    t, kdim = a.shape
    tm = _row_tile(seq, 1024)
    tn = D
    tk = min(kdim, 1024)
    nk = kdim // tk
    per = seq // tm

    def body(a_ref, w_ref, x_ref, gt_ref, xo_ref, y_ref, acc_ref):
        k = pl.program_id(2)

        @pl.when(k == 0)
        def _():
            acc_ref[...] = jnp.zeros_like(acc_ref)

        acc_ref[...] += _dot(a_ref[...], w_ref[...])

        @pl.when(k == nk - 1)
        def _():
            y = acc_ref[...]
            xo_ref[...] = x_ref[...] + gt_ref[0] * y
            y_ref[...] = y.astype(BF16)

    return pl.pallas_call(
        body, name=name, grid=(t // tm, D // tn, nk),
        in_specs=[pl.BlockSpec((tm, tk), lambda i, j, k: (i, k)),
                  pl.BlockSpec((tk, tn), lambda i, j, k: (k, j)),
                  pl.BlockSpec((tm, tn), lambda i, j, k: (i, j)),
                  pl.BlockSpec((1, 1, tn), lambda i, j, k: (i // per, 0, j))],
        out_specs=[pl.BlockSpec((tm, tn), lambda i, j, k: (i, j)),
                   pl.BlockSpec((tm, tn), lambda i, j, k: (i, j))],
        out_shape=[jax.ShapeDtypeStruct((t, D), F32), jax.ShapeDtypeStruct((t, D), BF16)],
        scratch_shapes=[pltpu.VMEM((tm, tn), F32)],
        compiler_params=_cp(("parallel", "parallel", "arbitrary")),
    )(a, w, x, gt)


def _ada_fwd(c_all, w_ada, b_blk):
    nl = w_ada.shape[0]
    nb = c_all.shape[0]

    def body(c_ref, w_ref, b_ref, o_ref):
        c = c_ref[...]
        ca = (c * jax.nn.sigmoid(c)).astype(BF16)
        o_ref[...] = _dot(ca, w_ref[...].astype(BF16)) + b_ref[...]

    return pl.pallas_call(
        body, name="ada_fwd", grid=(nl,),
        in_specs=[pl.BlockSpec((nb, D), lambda l: (0, 0)),
                  pl.BlockSpec((None, D, 768), lambda l: (l, 0, 0)),
                  pl.BlockSpec((None, 1, 768), lambda l: (l, 0, 0))],
        out_specs=pl.BlockSpec((None, nb, 768), lambda l: (l, 0, 0)),
        out_shape=jax.ShapeDtypeStruct((nl, nb, 768), F32),
        compiler_params=_cp(("parallel",)),
    )(c_all, w_ada, b_blk)


def _ada_bwd(c_all, dmod_blk):
    nl = dmod_blk.shape[0]
    nb = c_all.shape[0]

    def body(c_ref, d_ref, o_ref):
        c = c_ref[...]
        ca = (c * jax.nn.sigmoid(c)).astype(BF16)
        o_ref[...] = _dot(ca, d_ref[...].astype(BF16), TN)

    return pl.pallas_call(
        body, name="ada_bwd", grid=(nl,),
        in_specs=[pl.BlockSpec((nb, D), lambda l: (0, 0)),
                  pl.BlockSpec((None, nb, 768), lambda l: (l, 0, 0))],
        out_specs=pl.BlockSpec((None, D, 768), lambda l: (l, 0, 0)),
        out_shape=jax.ShapeDtypeStruct((nl, D, 768), F32),
        compiler_params=_cp(("parallel",)),
    )(c_all, dmod_blk)


def _seq_tile(seq):
    return _row_tile(seq, 512)


def _norm_mod_fwd(x, g, sc, sh):
    nb, seq, _ = x.shape
    ts = _seq_tile(seq)

    def body(x_ref, g_ref, sc_ref, sh_ref, h_ref):
        xv = x_ref[0]
        r = lax.rsqrt(jnp.mean(xv * xv, axis=-1, keepdims=True) + EPS)
        h_ref[0] = ((xv * r) * g_ref[...] * (1.0 + sc_ref[0]) + sh_ref[0]).astype(BF16)

    return pl.pallas_call(
        body, name="norm_mod_fwd", grid=(nb, seq // ts),
        in_specs=[pl.BlockSpec((1, ts, D), lambda b, s: (b, s, 0)),
                  pl.BlockSpec((1, D), lambda b, s: (0, 0)),
                  pl.BlockSpec((1, 1, D), lambda b, s: (b, 0, 0)),
                  pl.BlockSpec((1, 1, D), lambda b, s: (b, 0, 0))],
        out_specs=pl.BlockSpec((1, ts, D), lambda b, s: (b, s, 0)),
        out_shape=jax.ShapeDtypeStruct((nb, seq, D), BF16),
        compiler_params=_cp(("parallel", "parallel")),
    )(x, g, sc, sh)


def _gate_bwd_tile(d, y_ref, gt_ref, dy_ref, dgt_ref):
    @pl.when(pl.program_id(1) == 0)
    def _():
        dgt_ref[...] = jnp.zeros_like(dgt_ref)

    dy_ref[0] = (gt_ref[0] * d).astype(BF16)
    dgt_ref[0] += jnp.sum(d * y_ref[0].astype(F32), axis=0, keepdims=True)


def _norm_mod_bwd(x, dh, dres, g, sc, gate=None):
    nb, seq, _ = x.shape
    ts = _seq_tile(seq)

    def body(x_ref, dh_ref, dres_ref, g_ref, sc_ref, *rest):
        if gate is None:
            dx_ref, dsh_ref, dsc_ref, dg_ref = rest
        else:
            y_ref, gt_ref, dx_ref, dsh_ref, dsc_ref, dg_ref, dy_ref, dgt_ref = rest

        @pl.when(pl.program_id(1) == 0)
        def _():
            dsh_ref[...] = jnp.zeros_like(dsh_ref)
            dsc_ref[...] = jnp.zeros_like(dsc_ref)
            dg_ref[...] = jnp.zeros_like(dg_ref)

        xv = x_ref[0]
        dh = dh_ref[0]
        gv = g_ref[...]
        onesc = 1.0 + sc_ref[0]
        r = lax.rsqrt(jnp.mean(xv * xv, axis=-1, keepdims=True) + EPS)
        xh = xv * r
        dsh_ref[0] += jnp.sum(dh, axis=0, keepdims=True)
        dsc_ref[0] += jnp.sum(dh * (xh * gv), axis=0, keepdims=True)
        dg_ref[0] += jnp.sum(dh * onesc * xh, axis=0, keepdims=True)
        dxh = dh * (gv * onesc)
        dx = r * (dxh - xh * jnp.mean(dxh * xh, axis=-1, keepdims=True))
        dx_total = dres_ref[0] + dx
        dx_ref[0] = dx_total
        if gate is not None:
            _gate_bwd_tile(dx_total, y_ref, gt_ref, dy_ref, dgt_ref)

    vec = jax.ShapeDtypeStruct((nb, 1, D), F32)
    vspec = pl.BlockSpec((1, 1, D), lambda b, s: (b, 0, 0))
    tile = pl.BlockSpec((1, ts, D), lambda b, s: (b, s, 0))
    gated = gate is not None
    return pl.pallas_call(
        body, name="norm_mod_bwd", grid=(nb, seq // ts),
        in_specs=[tile, tile, tile, pl.BlockSpec((1, D), lambda b, s: (0, 0)), vspec] + [tile, vspec] * gated,
        out_specs=[tile, vspec, vspec, vspec] + [tile, vspec] * gated,
        out_shape=[jax.ShapeDtypeStruct((nb, seq, D), F32), vec, vec, vec]
        + [jax.ShapeDtypeStruct((nb, seq, D), BF16), vec] * gated,
        compiler_params=_cp(("parallel", "arbitrary")),
    )(x, dh, dres, g, sc, *(gate or ()))


def _loss_head(x, tgt, g, y, gt):
    nb, seq, _ = x.shape
    ts = _seq_tile(seq)

    def body(x_ref, t_ref, g_ref, y_ref, gt_ref, dx_ref, loss_ref, dg_ref, dy_ref, dgt_ref):
        @pl.when(pl.program_id(1) == 0)
        def _():
            loss_ref[...] = jnp.zeros_like(loss_ref)
            dg_ref[...] = jnp.zeros_like(dg_ref)

        xv = x_ref[0]
        gv = g_ref[...]
        r = lax.rsqrt(jnp.mean(xv * xv, axis=-1, keepdims=True) + EPS)
        xh = xv * r
        err = xh * gv - t_ref[0]
        per_tok = jnp.mean(err * err, axis=-1, keepdims=True)
        loss_ref[0] += 0.5 * jnp.sum(per_tok, axis=0, keepdims=True)
        dy = err * (1.0 / D)
        dg_ref[0] += jnp.sum(dy * xh, axis=0, keepdims=True)
        dxh = dy * gv
        dx = r * (dxh - xh * jnp.mean(dxh * xh, axis=-1, keepdims=True))
        dx_ref[0] = dx
        _gate_bwd_tile(dx, y_ref, gt_ref, dy_ref, dgt_ref)

    tile = pl.BlockSpec((1, ts, D), lambda b, s: (b, s, 0))
    vspec = pl.BlockSpec((1, 1, D), lambda b, s: (b, 0, 0))
    vec = jax.ShapeDtypeStruct((nb, 1, D), F32)
    return pl.pallas_call(
        body, name="loss_head", grid=(nb, seq // ts),
        in_specs=[tile, tile, pl.BlockSpec((1, D), lambda b, s: (0, 0)), tile, vspec],
        out_specs=[tile, pl.BlockSpec((1, 1, 128), lambda b, s: (b, 0, 0)), vspec, tile, vspec],
        out_shape=[jax.ShapeDtypeStruct((nb, seq, D), F32), jax.ShapeDtypeStruct((nb, 1, 128), F32), vec,
                   jax.ShapeDtypeStruct((nb, seq, D), BF16), vec],
        compiler_params=_cp(("parallel", "arbitrary")),
    )(x, tgt, g, y, gt)


_GELU_C = math.sqrt(2.0 / math.pi)


def _gelu(x):
    return 0.5 * x * (1.0 + jnp.tanh(_GELU_C * (x + 0.044715 * (x * x * x))))


def _gelu_and_grad(x):
    t = jnp.tanh(_GELU_C * (x + 0.044715 * (x * x * x)))
    y = 0.5 * x * (1.0 + t)
    dy = 0.5 * (1.0 + t) + 0.5 * x * (1.0 - t * t) * (_GELU_C * (1.0 + 3.0 * 0.044715 * (x * x)))
    return y, dy


def _tril_mask():
    row = lax.broadcasted_iota(jnp.int32, (CH, CH), 0)
    col = lax.broadcasted_iota(jnp.int32, (CH, CH), 1)
    return row >= col


def _gmlp_fwd(proj, ln_g, ln_b, ws, bst):
    t = proj.shape[0]
    tm = _row_tile(t, 512)

    def body(u_ref, v_ref, lg_ref, lb_ref, ws_ref, bst_ref, o_ref):
        tril = _tril_mask()
        wm = [jnp.where(tril, ws_ref[g], 0.0).astype(BF16) for g in range(NG)]
        for ch in range(tm // CH):
            rows = slice(ch * CH, (ch + 1) * CH)
            u = _gelu(u_ref[rows, :].astype(F32))
            v = _gelu(v_ref[rows, :].astype(F32))
            mu = jnp.mean(v, axis=-1, keepdims=True)
            xc = v - mu
            rstd = lax.rsqrt(jnp.mean(xc * xc, axis=-1, keepdims=True) + EPS)
            vn = ((xc * rstd) * lg_ref[...] + lb_ref[...]).astype(BF16)
            for g in range(NG):
                cols = slice(g * CH, (g + 1) * CH)
                s = _dot(wm[g], vn[:, cols]) + bst_ref[:, g:g + 1]
                o_ref[rows, cols] = (u[:, cols] * s).astype(BF16)

    return pl.pallas_call(
        body, name="gmlp_fwd", grid=(t // tm,),
        in_specs=[pl.BlockSpec((tm, BW), lambda i: (i, 0)),
                  pl.BlockSpec((tm, BW), lambda i: (i, 1)),
                  pl.BlockSpec((1, BW), lambda i: (0, 0)),
                  pl.BlockSpec((1, BW), lambda i: (0, 0)),
                  pl.BlockSpec((NG, CH, CH), lambda i: (0, 0, 0)),
                  pl.BlockSpec((CH, NG), lambda i: (0, 0))],
        out_specs=pl.BlockSpec((tm, BW), lambda i: (i, 0)),
        out_shape=jax.ShapeDtypeStruct((t, BW), BF16),
        compiler_params=_cp(("parallel",)),
    )(proj, proj, ln_g, ln_b, ws, bst)


def _gmlp_bwd(proj, dout, ln_g, ln_b, ws, bst, dproj):
    t = proj.shape[0]
    tm = _row_tile(t, 512)

    def body(u_ref, v_ref, do_ref, lg_ref, lb_ref, ws_ref, bst_ref, buf_ref,
             dp_ref, gws_ref, gbs_ref, glg_ref, glb_ref):
        @pl.when(pl.program_id(0) == 0)
        def _():
            gws_ref[...] = jnp.zeros_like(gws_ref)
            gbs_ref[...] = jnp.zeros_like(gbs_ref)
            glg_ref[...] = jnp.zeros_like(glg_ref)
            glb_ref[...] = jnp.zeros_like(glb_ref)

        tril = _tril_mask()
        wm = [jnp.where(tril, ws_ref[g], 0.0).astype(BF16) for g in range(NG)]
        ones = jnp.ones((CH, CH), BF16)
        lg = lg_ref[...]
        for ch in range(tm // CH):
            rows = slice(ch * CH, (ch + 1) * CH)
            u, du_fac = _gelu_and_grad(u_ref[rows, :].astype(F32))
            v, dv_fac = _gelu_and_grad(v_ref[rows, :].astype(F32))
            do = do_ref[rows, :].astype(F32)
            mu = jnp.mean(v, axis=-1, keepdims=True)
            xc = v - mu
            rstd = lax.rsqrt(jnp.mean(xc * xc, axis=-1, keepdims=True) + EPS)
            xh = xc * rstd
            vn = (xh * lg + lb_ref[...]).astype(BF16)
            dvn_parts = []
            for g in range(NG):
                cols = slice(g * CH, (g + 1) * CH)
                s = _dot(wm[g], vn[:, cols]) + bst_ref[:, g:g + 1]
                dp_ref[rows, cols] = (do[:, cols] * s * du_fac[:, cols]).astype(BF16)
                ds = (do[:, cols] * u[:, cols]).astype(BF16)
                gws_ref[g] += jnp.where(tril, _dot(ds, vn[:, cols], NT), 0.0)
                gbs_ref[g] += _dot(ds, ones)
                dvn_parts.append(_dot(wm[g], ds, TN))
            dvn = jnp.concatenate(dvn_parts, axis=1)
            glb_ref[...] += jnp.sum(dvn, axis=0, keepdims=True)
            glg_ref[...] += jnp.sum(dvn * xh, axis=0, keepdims=True)
            dxh = dvn * lg
            dv = rstd * (dxh - jnp.mean(dxh, axis=-1, keepdims=True)
                         - xh * jnp.mean(dxh * xh, axis=-1, keepdims=True))
            dp_ref[rows, BW:2 * BW] = (dv * dv_fac).astype(BF16)

    small = pl.BlockSpec((NG, CH, CH), lambda i: (0, 0, 0))
    vec = pl.BlockSpec((1, BW), lambda i: (0, 0))
    return pl.pallas_call(
        body, name="gmlp_bwd", grid=(t // tm,),
        in_specs=[pl.BlockSpec((tm, BW), lambda i: (i, 0)),
                  pl.BlockSpec((tm, BW), lambda i: (i, 1)),
                  pl.BlockSpec((tm, BW), lambda i: (i, 0)),
                  vec, vec, small, pl.BlockSpec((CH, NG), lambda i: (0, 0)), _HBM],
        out_specs=[pl.BlockSpec((tm, 2 * BW), lambda i: (i, 0)), small, small, vec, vec],
        out_shape=[jax.ShapeDtypeStruct((t, IN_COLS), BF16),
                   jax.ShapeDtypeStruct((NG, CH, CH), F32), jax.ShapeDtypeStruct((NG, CH, CH), F32),
                   jax.ShapeDtypeStruct((1, BW), F32), jax.ShapeDtypeStruct((1, BW), F32)],
        input_output_aliases={7: 0},
        compiler_params=_cp(("arbitrary",)),
    )(proj, proj, dout, ln_g, ln_b, ws, bst, dproj)


def _pool_bands():
    row = lax.broadcasted_iota(jnp.int32, (CH, CH), 0)
    col = lax.broadcasted_iota(jnp.int32, (CH, CH), 1)
    cur, prev = [], []
    for w in POOL_WINDOWS:
        cur.append(jnp.where((row >= col) & (row - col < w), 1.0, 0.0).astype(BF16))
        prev.append(jnp.where(row + CH - col < w, 1.0, 0.0).astype(BF16))
    return cur, prev


def _pool_inv_count(r0, w):
    pos = r0 + lax.broadcasted_iota(jnp.int32, (CH, 1), 0)
    return 1.0 / jnp.minimum(pos + 1, w).astype(F32)


def _pool_diff(x_ref, r0, rp, has_prev, cur, prev, g):
    cols = slice(g * CH, (g + 1) * CH)
    xc = x_ref[pl.ds(r0, CH), cols]
    xp = x_ref[pl.ds(rp, CH), cols]
    ws = _dot(cur[g], xc) + has_prev * _dot(prev[g], xp)
    return ws * _pool_inv_count(r0, POOL_WINDOWS[g]) - xc.astype(F32)


def _pool_fwd(proj3, pw, pscale):
    nb, seq, _ = proj3.shape
    nch = seq // CH

    def body(x_ref, pw_ref, ps_ref, o_ref):
        cur, prev = _pool_bands()
        pwb = [pw_ref[g].astype(BF16) for g in range(NG)]

        def chunk(ch, carry):
            r0 = pl.multiple_of(ch * CH, CH)
            rp = pl.multiple_of(jnp.maximum(ch - 1, 0) * CH, CH)
            has_prev = jnp.where(ch > 0, 1.0, 0.0)
            for g in range(NG):
                cols = slice(g * CH, (g + 1) * CH)
                d = _pool_diff(x_ref, r0, rp, has_prev, cur, prev, g)
                y = _dot(d.astype(BF16), pwb[g]) * ps_ref[:, cols]
                o_ref[pl.ds(r0, CH), cols] = y.astype(BF16)
            return carry

        lax.fori_loop(0, nch, chunk, 0, unroll=2)

    return pl.pallas_call(
        body, name="pool_fwd", grid=(nb,),
        in_specs=[pl.BlockSpec((None, seq, BW), lambda b: (b, 0, 5)),
                  pl.BlockSpec((NG, CH, CH), lambda b: (0, 0, 0)),
                  pl.BlockSpec((1, BW), lambda b: (0, 0))],
        out_specs=pl.BlockSpec((None, seq, BW), lambda b: (b, 0, 0)),
        out_shape=jax.ShapeDtypeStruct((nb, seq, BW), BF16),
        compiler_params=_cp(("parallel",)),
    )(proj3, pw, pscale)


def _pool_bwd(proj3, dout3, pw, pscale, dproj3):
    nb, seq, _ = proj3.shape
    nch = seq // CH

    def body(x_ref, do_ref, pw_ref, ps_ref, buf_ref, dx_ref, gpw_ref, gps_ref, e_ref):
        @pl.when(pl.program_id(0) == 0)
        def _():
            gpw_ref[...] = jnp.zeros_like(gpw_ref)
            gps_ref[...] = jnp.zeros_like(gps_ref)

        cur, prev = _pool_bands()
        pwb = [pw_ref[g].astype(BF16) for g in range(NG)]

        def first(ch, carry):
            r0 = pl.multiple_of(ch * CH, CH)
            rp = pl.multiple_of(jnp.maximum(ch - 1, 0) * CH, CH)
            has_prev = jnp.where(ch > 0, 1.0, 0.0)
            for g in range(NG):
                cols = slice(g * CH, (g + 1) * CH)
                d = _pool_diff(x_ref, r0, rp, has_prev, cur, prev, g).astype(BF16)
                do = do_ref[pl.ds(r0, CH), cols].astype(F32)
                ypre = _dot(d, pwb[g])
                gps_ref[:, cols] += jnp.sum(do * ypre, axis=0, keepdims=True)
                dyp = (do * ps_ref[:, cols]).astype(BF16)
                gpw_ref[g] += _dot(d, dyp, TN)
                e_ref[pl.ds(r0, CH), cols] = _dot(dyp, pwb[g], NT)
            return carry

        lax.fori_loop(0, nch, first, 0, unroll=2)

        def second(ch, carry):
            r0 = pl.multiple_of(ch * CH, CH)
            rn = pl.multiple_of(jnp.minimum(ch + 1, nch - 1) * CH, CH)
            has_next = jnp.where(ch < nch - 1, 1.0, 0.0)
            for g in range(NG):
                cols = slice(g * CH, (g + 1) * CH)
                w = POOL_WINDOWS[g]
                dd = e_ref[pl.ds(r0, CH), cols]
                ec = (dd * _pool_inv_count(r0, w)).astype(BF16)
                en = (e_ref[pl.ds(rn, CH), cols] * _pool_inv_count(rn, w)).astype(BF16)
                dx = _dot(cur[g], ec, TN) + has_next * _dot(prev[g], en, TN) - dd
                dx_ref[pl.ds(r0, CH), cols] = dx.astype(BF16)
            return carry

        lax.fori_loop(0, nch, second, 0, unroll=2)

    small = pl.BlockSpec((NG, CH, CH), lambda b: (0, 0, 0))
    vec = pl.BlockSpec((1, BW), lambda b: (0, 0))
    return pl.pallas_call(
        body, name="pool_bwd", grid=(nb,),
        in_specs=[pl.BlockSpec((None, seq, BW), lambda b: (b, 0, 5)),
                  pl.BlockSpec((None, seq, BW), lambda b: (b, 0, 0)), small, vec, _HBM],
        out_specs=[pl.BlockSpec((None, seq, BW), lambda b: (b, 0, 5)), small, vec],
        out_shape=[jax.ShapeDtypeStruct((nb, seq, IN_COLS), BF16),
                   jax.ShapeDtypeStruct((NG, CH, CH), F32), jax.ShapeDtypeStruct((1, BW), F32)],
        input_output_aliases={4: 0},
        scratch_shapes=[pltpu.VMEM((seq, BW), F32)],
        compiler_params=_cp(("arbitrary",)),
    )(proj3, dout3, pw, pscale, dproj3)


SB_BQ = 256
SB_BK = 256
SB_SCALE = HD ** -0.5


SB_EXIT = -110.0


def _sb_tile(qs, k, mask):
    z = _dot(qs, k, NT)
    lb = jnp.minimum(z, 0.0) - jnp.log(1.0 + jnp.exp(-jnp.abs(z)))
    lom = lb - z
    if mask is not None:
        lom = jnp.where(mask, lom, 0.0)
    return lb, lom


def _sb_alive(c):
    top = functools.reduce(jnp.maximum, [jnp.max(state[1]) for state in c])
    return (top > SB_EXIT).astype(jnp.int32)


def _sb_past_blocks(step, c, npast):
    def cond(s):
        return jnp.logical_and(s[0] < npast, s[1] > 0)

    def body(s):
        i, _, c = s
        c = step(pl.multiple_of((npast - 1 - i) * SB_BK, SB_BK), c, None)
        return i + 1, _sb_alive(c), c

    return lax.while_loop(cond, body, (jnp.int32(0), _sb_alive(c), c))[2]


def _sb_diag_mask(bq, d):
    row = lax.broadcasted_iota(jnp.int32, (bq, SB_BK), 0)
    col = lax.broadcasted_iota(jnp.int32, (bq, SB_BK), 1)
    return col + d * SB_BK < row


def _sb_scaled(q):
    return (q.astype(F32) * SB_SCALE).astype(BF16)


def _dot_tri(a, m):
    return _dot(a.astype(BF16), m)


def _dot_tri2(a, m):
    hi = a.astype(BF16)
    lo = (a - hi.astype(F32)).astype(BF16)
    return _dot(hi, m) + _dot(lo, m)


def _sb_fwd(proj3, gather=()):
    nb, seq, _ = proj3.shape
    bq = min(SB_BQ, seq)
    nq = seq // bq
    ndiag = bq // SB_BK

    def body(q_ref, k_ref, v_ref, o_ref):
        row = lax.broadcasted_iota(jnp.int32, (SB_BK, SB_BK), 0)
        col = lax.broadcasted_iota(jnp.int32, (SB_BK, SB_BK), 1)
        upper = jnp.where(row > col, 1.0, 0.0).astype(BF16)
        heads = [slice(hh * HD, (hh + 1) * HD) for hh in range(2)]

        def qloop(qi, carry):
            q0 = pl.multiple_of(qi * bq, bq)
            qs = [_sb_scaled(q_ref[pl.ds(q0, bq), lanes]) for lanes in heads]

            def step(k0, c, mask):
                tiles = [_sb_tile(q, k_ref[pl.ds(k0, SB_BK), lanes], mask) for lanes, q in zip(heads, qs)]
                sums = [_dot_tri(lom, upper) for _, lom in tiles]
                out = []
                for lanes, (acc, cr), (lb, lom), cs in zip(heads, c, tiles, sums):
                    a = jnp.exp(lb + (cs + cr))
                    if mask is not None:
                        a = jnp.where(mask, a, 0.0)
                    rsum = cs[:, 0:1] + lom[:, 0:1]
                    out.append((acc + _dot(a.astype(BF16), v_ref[pl.ds(k0, SB_BK), lanes]), cr + rsum))
                return tuple(out)

            c = tuple((jnp.zeros((bq, HD), F32), jnp.zeros((bq, 1), F32)) for _ in heads)
            for d in reversed(range(ndiag)):
                c = step(pl.multiple_of(q0 + d * SB_BK, SB_BK), c, _sb_diag_mask(bq, d))
            c = _sb_past_blocks(step, c, qi * ndiag)
            for lanes, (acc, _) in zip(heads, c):
                o_ref[pl.ds(q0, bq), lanes] = acc
            return carry

        lax.fori_loop(0, nq, qloop, 0)

    def spec(c0):
        return pl.BlockSpec((None, seq, 128), lambda b, p: (b, 0, c0 + p))

    grid = (nb, BW // 128)
    body, ex_in, ex_out, ex_shape, ex_sems = _host_exchange(body, 3, 1, grid, gather, False)
    outs = pl.pallas_call(
        body, name="sb_fwd", grid=grid,
        in_specs=[spec(8), spec(12), spec(16)] + ex_in,
        out_specs=[spec(0)] + ex_out,
        out_shape=[jax.ShapeDtypeStruct((nb, seq, BW), F32)] + ex_shape,
        scratch_shapes=ex_sems,
        compiler_params=_cp(("arbitrary", "arbitrary")),
    )(proj3, proj3, proj3, *gather)
    return outs[0], outs[1:]


def _sb_bwd(proj3, do3, o3, scatter=()):
    nb, seq, _ = proj3.shape
    bq = min(SB_BQ, seq)
    nq = seq // bq
    ndiag = bq // SB_BK
    assert ndiag == 1

    def body(q_ref, k_ref, v_ref, do_ref, o_ref, dq_ref, dk_ref, dv_ref, dk_acc, dv_acc):
        row = lax.broadcasted_iota(jnp.int32, (SB_BK, SB_BK), 0)
        col = lax.broadcasted_iota(jnp.int32, (SB_BK, SB_BK), 1)
        upper = jnp.where(row > col, 1.0, 0.0).astype(BF16)
        later = jnp.where(row >= col, 1.0, 0.0).astype(BF16)
        heads = [slice(hh * HD, (hh + 1) * HD) for hh in range(2)]

        def qloop(qi, carry):
            q0 = pl.multiple_of(qi * bq, bq)
            qs = [_sb_scaled(q_ref[pl.ds(q0, bq), lanes]) for lanes in heads]
            dos = [do_ref[pl.ds(q0, bq), lanes] for lanes in heads]
            gtot = [jnp.sum(do.astype(F32) * o_ref[pl.ds(q0, bq), lanes], axis=1, keepdims=True)
                    for do, lanes in zip(dos, heads)]

            def step(k0, c, mask):
                ks = [k_ref[pl.ds(k0, SB_BK), lanes] for lanes in heads]
                tiles = [_sb_tile(q, k, mask) for q, k in zip(qs, ks)]
                sums = [_dot_tri(lom, upper) for _, lom in tiles]
                das = [_dot(do, v_ref[pl.ds(k0, SB_BK), lanes], NT) for do, lanes in zip(dos, heads)]
                gls, avs = [], []
                for hh, (_, cr, _) in enumerate(c):
                    a = jnp.exp(tiles[hh][0] + (sums[hh] + cr))
                    if mask is not None:
                        a = jnp.where(mask, a, 0.0)
                    ab = a.astype(BF16)
                    avs.append(ab)
                    gls.append(das[hh] * ab.astype(F32))
                tails = [_dot_tri2(gl, later) for gl in gls]
                out = []
                for hh, (dq, cr, gdone) in enumerate(c):
                    lb, lom = tiles[hh]
                    pre = gtot[hh] - gdone - tails[hh]
                    dz = gls[hh] - jnp.exp(lb) * (gls[hh] + pre)
                    if mask is not None:
                        dz = jnp.where(mask, dz, 0.0)
                    dz = dz.astype(BF16)
                    dk_new, dv_new = _dot(dz, qs[hh], TN), _dot(avs[hh], dos[hh], TN)
                    if mask is not None:
                        dk_acc[hh, pl.ds(k0, SB_BK), :] = dk_new
                        dv_acc[hh, pl.ds(k0, SB_BK), :] = dv_new
                    else:
                        dk_acc[hh, pl.ds(k0, SB_BK), :] += dk_new
                        dv_acc[hh, pl.ds(k0, SB_BK), :] += dv_new
                    rsum = sums[hh][:, 0:1] + lom[:, 0:1]
                    out.append((dq + _dot(dz, ks[hh]), cr + rsum, gdone + tails[hh][:, 0:1]))
                return tuple(out)

            c = tuple((jnp.zeros((bq, HD), F32), jnp.zeros((bq, 1), F32), jnp.zeros((bq, 1), F32))
                      for _ in heads)
            for d in reversed(range(ndiag)):
                c = step(pl.multiple_of(q0 + d * SB_BK, SB_BK), c, _sb_diag_mask(bq, d))
            c = _sb_past_blocks(step, c, qi * ndiag)
            for lanes, (dq, _, _) in zip(heads, c):
                dq_ref[pl.ds(q0, bq), lanes] = (dq * SB_SCALE).astype(BF16)
            return carry

        lax.fori_loop(0, nq, qloop, 0)
        for hh in range(2):
            lanes = slice(hh * HD, (hh + 1) * HD)
            dk_ref[:, lanes] = dk_acc[hh].astype(BF16)
            dv_ref[:, lanes] = dv_acc[hh].astype(BF16)

    def spec(c0):
        return pl.BlockSpec((None, seq, 128), lambda b, p: (b, 0, c0 + p))

    grid = (nb, BW // 128)
    body, ex_in, ex_out, ex_shape, ex_sems = _host_exchange(body, 5, 3, grid, scatter, True)
    outs = pl.pallas_call(
        body, name="sb_bwd", grid=grid,
        in_specs=[spec(8), spec(12), spec(16), spec(0), spec(0)] + ex_in,
        out_specs=[spec(0), spec(0), spec(0)] + ex_out,
        out_shape=[jax.ShapeDtypeStruct((nb, seq, BW), BF16)] * 3 + ex_shape,
        scratch_shapes=[pltpu.VMEM((2, seq, HD), F32), pltpu.VMEM((2, seq, HD), F32)] + ex_sems,
        compiler_params=_cp(("arbitrary", "arbitrary")),
    )(proj3, proj3, proj3, do3, o3, *scatter)
    return outs[:3], outs[3:]


def _merge_fwd(brs, wb, proj):
    t = proj.shape[0]
    tm = _row_tile(t, 512)
    tn = 512
    nj = D // tn

    def body(b0, b1, b2, wb_ref, l0, l1, l2, m_ref, y0, y1, y2):
        acc = None
        for br, n, lg, y_ref in ((b0, 0, l0, y0), (b1, 1, l1, y1), (b2, 2, l2, y2)):
            y = _dot(br[...].astype(BF16), wb_ref[n])
            y_ref[...] = y.astype(BF16)
            term = jax.nn.sigmoid(lg[...].astype(F32)) * y
            acc = term if acc is None else acc + term
        m_ref[...] = acc.astype(BF16)

    def lspec(n):
        return pl.BlockSpec((tm, tn), lambda i, j: (i, (3 * D + n * D) // tn + j))

    tile = pl.BlockSpec((tm, tn), lambda i, j: (i, j))
    bspec = pl.BlockSpec((tm, BW), lambda i, j: (i, 0))
    return pl.pallas_call(
        body, name="merge_fwd", grid=(t // tm, nj),
        in_specs=[bspec, bspec, bspec, pl.BlockSpec((NB, BW, tn), lambda i, j: (0, 0, j)),
                  lspec(0), lspec(1), lspec(2)],
        out_specs=[tile] * 4,
        out_shape=[jax.ShapeDtypeStruct((t, D), BF16)] * 4,
        compiler_params=_cp(("parallel", "parallel")),
    )(brs[0], brs[1], brs[2], wb, proj, proj, proj)


def _merge_bwd(dm, ys, proj):
    t = proj.shape[0]
    tm = _row_tile(t, 512)

    def body(dm_ref, y0, y1, y2, lg_ref, dp_ref, dy0, dy1, dy2):
        dmv = dm_ref[...].astype(F32)
        for n, (y_ref, dy_ref) in enumerate(((y0, dy0), (y1, dy1), (y2, dy2))):
            cols = slice(n * D, (n + 1) * D)
            g = jax.nn.sigmoid(lg_ref[:, cols].astype(F32))
            dp_ref[:, cols] = (dmv * y_ref[...].astype(F32) * g * (1.0 - g)).astype(BF16)
            dy_ref[...] = (dmv * g).astype(BF16)

    tile = pl.BlockSpec((tm, D), lambda i: (i, 0))
    gates = pl.BlockSpec((tm, NB * D), lambda i: (i, 1))
    return pl.pallas_call(
        body, name="merge_bwd", grid=(t // tm,),
        in_specs=[tile] * 4 + [gates],
        out_specs=[gates] + [tile] * 3,
        out_shape=[jax.ShapeDtypeStruct((t, IN_COLS), BF16)] + [jax.ShapeDtypeStruct((t, D), BF16)] * 3,
        compiler_params=_cp(("parallel",)),
    )(dm, ys[0], ys[1], ys[2], proj)


def _adamw_rows(rows):
    if rows <= 512:
        return rows
    return next(tr for tr in (512, 384, 352, 256, 128, 64, 32, 16, 8) if rows % tr == 0)


def _adamw_math(npart, p_ref, w_ref, m_ref, v_ref, g_ref, d_ref, mo_ref, vo_ref):
    c1 = 1.0 - ADAM_B1 ** ADAM_STEP
    c2 = 1.0 - ADAM_B2 ** ADAM_STEP
    g = p_ref[0].astype(F32)
    for p in range(1, npart):
        g = g + p_ref[p].astype(F32)
    mn = ADAM_B1 * m_ref[...] + (1.0 - ADAM_B1) * g
    vn = ADAM_B2 * v_ref[...] + (1.0 - ADAM_B2) * (g * g)
    m_hat = mn / c1
    v_hat = vn / c2
    g_ref[...] = g
    d_ref[...] = -ADAM_LR * (m_hat / (jnp.sqrt(v_hat) + ADAM_EPS) + ADAM_WD * w_ref[...])
    mo_ref[...] = mn
    vo_ref[...] = vn


def _adamw_layer(name, parts, w, m, v, layer, bufs, padded=False):
    nl, cols = w.shape[0], w.shape[-1]
    rows = int(math.prod(w.shape[1:-1]))
    npart = parts.shape[0]
    tr = _adamw_rows(rows)
    if padded:
        assert parts.shape[1] == rows // tr and parts.shape[2] >= tr
        parts_spec = pl.BlockSpec((npart, None, tr, cols), lambda i: (0, i, 0, 0))
    else:
        parts = parts.reshape(npart, rows, cols)
        parts_spec = pl.BlockSpec((npart, tr, cols), lambda i: (0, i, 0))
    if bufs is None:
        bufs = [lax.empty((nl, rows, cols), F32) for _ in range(4)]

    def body(p_ref, w_ref, m_ref, v_ref, b0, b1, b2, b3, g_ref, d_ref, mo_ref, vo_ref):
        _adamw_math(npart, p_ref, w_ref, m_ref, v_ref, g_ref, d_ref, mo_ref, vo_ref)

    slab = pl.BlockSpec((None, tr, cols), lambda i: (layer, i, 0))
    sds = jax.ShapeDtypeStruct((nl, rows, cols), F32)
    return pl.pallas_call(
        body, name=name, grid=(rows // tr,),
        in_specs=[parts_spec, slab, slab, slab] + [_HBM] * 4,
        out_specs=[slab] * 4, out_shape=[sds] * 4,
        input_output_aliases={4: 0, 5: 1, 6: 2, 7: 3},
        compiler_params=_cp(("parallel",)),
    )(parts, w.reshape(nl, rows, cols), m.reshape(nl, rows, cols), v.reshape(nl, rows, cols), *bufs)


def _adamw_reduce(name, parts, w, m, v):
    shape = w.shape
    cols = shape[-1]
    rows = int(math.prod(shape[:-1])) if len(shape) > 1 else 1
    npart = parts.shape[0]
    tr = _adamw_rows(rows)

    def body(p_ref, w_ref, m_ref, v_ref, g_ref, d_ref, mo_ref, vo_ref):
        _adamw_math(npart, p_ref, w_ref, m_ref, v_ref, g_ref, d_ref, mo_ref, vo_ref)

    tile = pl.BlockSpec((tr, cols), lambda i: (i, 0))
    sds = jax.ShapeDtypeStruct((rows, cols), F32)
    outs = pl.pallas_call(
        body, name=name, grid=(rows // tr,),
        in_specs=[pl.BlockSpec((npart, tr, cols), lambda i: (0, i, 0)), tile, tile, tile],
        out_specs=[tile] * 4, out_shape=[sds] * 4,
        compiler_params=_cp(("parallel",)),
    )(parts.reshape(npart, rows, cols), w.reshape(rows, cols), m.reshape(rows, cols), v.reshape(rows, cols))
    return tuple(o.reshape(shape) for o in outs)


def _pad_ffn_in(w):
    lead = w.shape[:-1]
    w = w.reshape(lead + (2, FF_HALF))
    w = jnp.pad(w, [(0, 0)] * len(lead) + [(0, 0), (0, FF_HALF_PAD - FF_HALF)])
    return w.reshape(lead + (FF_IN_PAD,))


def kernel(x, c, rms_g1, rms_g2, w_ada, b_ada, w_in, gm_ln_g, gm_ln_b, gm_w_spatial, gm_b_spatial, pool_w, pool_scale, w_branch, w_out, w_ffn_in, w_ffn_out, final_g, loss_target, m_rms_g1, m_rms_g2, m_w_ada, m_b_ada, m_w_in, m_gm_ln_g, m_gm_ln_b, m_gm_w_spatial, m_gm_b_spatial, m_pool_w, m_pool_scale, m_w_branch, m_w_out, m_w_ffn_in, m_w_ffn_out, m_final_g, v_rms_g1, v_rms_g2, v_w_ada, v_b_ada, v_w_in, v_gm_ln_g, v_gm_ln_b, v_gm_w_spatial, v_gm_b_spatial, v_pool_w, v_pool_scale, v_w_branch, v_w_out, v_w_ffn_in, v_w_ffn_out, v_final_g):
    nb, seq, _ = x.shape
    nl = w_in.shape[0]
    t = nb * seq
    ntot = NDEV * nb
    me = _my_index()
    assert x.shape[2] == D and w_in.shape[1:] == (D, 768) and w_ffn_in.shape[1:] == (D, FF_IN_SHARD)
    assert seq % CH == 0

    w_ffn_in_p = _pad_ffn_in(w_ffn_in).astype(BF16)
    w_ffn_out_p = jnp.pad(w_ffn_out, ((0, 0), (0, FF_HALF_PAD - FF_HALF), (0, 0))).astype(BF16)
    w_in_b = w_in.astype(BF16)
    w_branch_b = w_branch.astype(BF16)
    w_out_b = w_out.astype(BF16)
    (g_in_next,) = _exchange([w_in_b[0]], "gather_w_in0", False)

    (c_all,) = _exchange([c], "gather_c", False)
    c_all = c_all.reshape(ntot, D)
    b_blk = lax.dynamic_slice_in_dim(b_ada, me * 768, 768, axis=1).reshape(nl, 1, 768)
    mod_blk = _ada_fwd(c_all, w_ada, b_blk)
    (mod_all,) = _exchange([mod_blk], "gather_mod", False)
    mod_all = jnp.transpose(mod_all, (1, 2, 0, 3)).reshape(nl, ntot, NMOD * D)
    mod = lax.dynamic_slice_in_dim(mod_all, me * nb, nb, axis=1).reshape(nl, nb, NMOD, 1, D)

    saved = []
    gathered = []
    xc = x
    for l in range(nl):
        sh1, sc1, gt1, sh2, sc2, gt2 = [mod[l, :, i] for i in range(NMOD)]
        h = _norm_mod_fwd(xc, rms_g1[l].reshape(1, D), sc1, sh1).reshape(t, D)
        proj, (g_ffn_in_w,) = _mm_colblocked("proj_fwd", h, g_in_next, BF16, [w_ffn_in_p[l]])
        proj3 = proj.reshape(nb, seq, IN_COLS)
        br_gm = _gmlp_fwd(proj, gm_ln_g[l].reshape(1, BW), gm_ln_b[l].reshape(1, BW),
                          gm_w_spatial[l], gm_b_spatial[l].T)
        sb_o, got = _sb_fwd(proj3, [w_branch_b[l], w_out_b[l], w_ffn_out_p[l]]
                            + ([w_in_b[l + 1]] if l + 1 < nl else []))
        gw = dict(w_in=g_in_next,
                  w_branch=jnp.transpose(got[0], (1, 2, 0, 3)).reshape(NB, BW, D),
                  w_out=got[1].reshape(D, D),
                  w_ffn_in=g_ffn_in_w,
                  w_ffn_out=got[2].reshape(FFP, D))
        gathered.append(gw)
        if l + 1 < nl:
            g_in_next = got[3]
        br_pool = _pool_fwd(proj3, pool_w[l], pool_scale[l].reshape(1, BW))
        brs = [br_gm, sb_o.reshape(t, BW), br_pool.reshape(t, BW)]
        merged, y0, y1, y2 = _merge_fwd(brs, gw["w_branch"], proj)
        x_mid, mo = _mm_residual("out_fwd", merged, gw["w_out"], xc.reshape(t, D), gt1, seq)
        x_mid = x_mid.reshape(nb, seq, D)
        h2 = _norm_mod_fwd(x_mid, rms_g2[l].reshape(1, D), sc2, sh2).reshape(t, D)
        fg, fu, act, _ = _ffn_in_fwd(h2, gw["w_ffn_in"])
        x_out, fo = _mm_residual("ffn_out_fwd", act, gw["w_ffn_out"], x_mid.reshape(t, D), gt2, seq)
        saved.append(dict(x_in=xc, h=h, proj=proj, brs=brs, sb_o=sb_o, ys=(y0, y1, y2), merged=merged,
                          mo=mo, x_mid=x_mid, h2=h2, fg=fg, fu=fu, act=act, fo=fo))
        xc = x_out.reshape(nb, seq, D)

    dx, loss_part, dfinal_part, dfo, dgt2 = _loss_head(xc, loss_target, final_g.reshape(1, D),
                                                       saved[-1]["fo"].reshape(nb, seq, D), mod[nl - 1, :, 5])
    loss = lax.psum(jnp.sum(loss_part[:, 0, 0]), ("x", "y", "c"))

    big_names = ("w_in", "w_branch", "w_out", "w_ffn_in", "w_ffn_out")
    bufs = {name: None for name in big_names}
    w_ffn_in_t, m_w_ffn_in_t, v_w_ffn_in_t = [jnp.swapaxes(a, 1, 2) for a in (w_ffn_in, m_w_ffn_in, v_w_ffn_in)]
    small_parts = {k: [None] * nl for k in ("rms_g1", "rms_g2", "gm_ln_g", "gm_ln_b", "gm_w_spatial",
                                            "gm_b_spatial", "pool_w", "pool_scale")}
    dmod = [None] * nl
    for l in reversed(range(nl)):
        gw = gathered[l]
        sv = saved[l]
        sh1, sc1, gt1, sh2, sc2, gt2 = [mod[l, :, i] for i in range(NMOD)]
        dfo = dfo.reshape(t, D)
        g_ffn_out = _mm_tn("ffn_out_wgrad", sv["act"], dfo)
        dfg, dfu = _ffn_out_dgrad(dfo, gw["w_ffn_out"], sv["fg"], sv["fu"])
        dh2 = _ffn_in_dgrad(dfg, dfu, gw["w_ffn_in"])
        g_ffn_in = _ffn_in_wgrad(sv["h2"], dfg, dfu)
        dx_mid, dsh2, dsc2, dg2, dmo, dgt1 = _norm_mod_bwd(
            sv["x_mid"], dh2.reshape(nb, seq, D), dx, rms_g2[l].reshape(1, D), sc2,
            gate=(sv["mo"].reshape(nb, seq, D), gt1))
        dmo = dmo.reshape(t, D)
        dmerged = _mm_nt("out_dgrad", dmo, gw["w_out"], BF16)
        g_out = _mm_tn("out_wgrad", sv["merged"], dmo)
        dproj, *dys = _merge_bwd(dmerged, sv["ys"], sv["proj"])
        dbrs, g_br = [], []
        for n in range(NB):
            dbrs.append(_mm_nt("branch_dgrad", dys[n], gw["w_branch"], BF16, w_lead=n))
            g_br.append(_mm_tn("branch_wgrad", sv["brs"][n], dys[n]))
        proj3 = sv["proj"].reshape(nb, seq, IN_COLS)
        dproj, g_ws, g_bs, g_lg, g_lb = _gmlp_bwd(sv["proj"], dbrs[0], gm_ln_g[l].reshape(1, BW),
                                                  gm_ln_b[l].reshape(1, BW), gm_w_spatial[l], gm_b_spatial[l].T,
                                                  dproj)
        g_br_dev = jnp.transpose(jnp.stack(g_br).reshape(NB, BW, NDEV, D // NDEV), (2, 0, 1, 3))
        carried = [g_br_dev, g_out.reshape(NDEV, D // NDEV, D), g_ffn_in, g_ffn_out.reshape(NDEV, FF_HALF_PAD, D)]
        d_sb, recv = _sb_bwd(proj3, dbrs[1].reshape(nb, seq, BW), sv["sb_o"], carried)
        bufs["w_branch"] = _adamw_layer("adamw_w_branch", recv[0], w_branch, m_w_branch, v_w_branch, l,
                                        bufs["w_branch"])
        bufs["w_out"] = _adamw_layer("adamw_w_out", recv[1], w_out, m_w_out, v_w_out, l, bufs["w_out"])
        bufs["w_ffn_in"] = _adamw_layer("adamw_w_ffn_in", recv[2].reshape(NDEV, 2, FF_HALF_PAD, D), w_ffn_in_t,
                                        m_w_ffn_in_t, v_w_ffn_in_t, l, bufs["w_ffn_in"], padded=True)
        bufs["w_ffn_out"] = _adamw_layer("adamw_w_ffn_out", recv[3].reshape(NDEV, 1, FF_HALF_PAD, D), w_ffn_out,
                                         m_w_ffn_out, v_w_ffn_out, l, bufs["w_ffn_out"], padded=True)
        dproj3, g_pw, g_ps = _pool_bwd(proj3, dbrs[2].reshape(nb, seq, BW), pool_w[l], pool_scale[l].reshape(1, BW),
                                       dproj.reshape(nb, seq, IN_COLS))
        dproj = dproj3.reshape(t, IN_COLS)
        for i, piece in enumerate(d_sb):
            dproj = lax.dynamic_update_slice(dproj, piece.reshape(t, BW), (0, 2 * BW + i * BW))
        g_in = _mm_colblocked_tn("proj_wgrad", sv["h"], dproj)
        dh, (r_in,) = _mm_colblocked_nt("proj_dgrad", dproj, gw["w_in"], F32, [g_in])
        bufs["w_in"] = _adamw_layer("adamw_w_in", r_in, w_in, m_w_in, v_w_in, l, bufs["w_in"])
        dmod_tail = [dgt1, dsh2, dsc2, dgt2]
        if l > 0:
            dx, dsh1, dsc1, dg1, dfo, dgt2 = _norm_mod_bwd(
                sv["x_in"], dh.reshape(nb, seq, D), dx_mid, rms_g1[l].reshape(1, D), sc1,
                gate=(saved[l - 1]["fo"].reshape(nb, seq, D), mod[l - 1, :, 5]))
        else:
            dx, dsh1, dsc1, dg1 = _norm_mod_bwd(sv["x_in"], dh.reshape(nb, seq, D), dx_mid,
                                                rms_g1[l].reshape(1, D), sc1)

        dmod[l] = jnp.concatenate([dsh1, dsc1] + dmod_tail, axis=-1)
        small_parts["rms_g1"][l] = jnp.sum(dg1, axis=0)
        small_parts["rms_g2"][l] = jnp.sum(dg2, axis=0)
        small_parts["gm_ln_g"][l] = g_lg
        small_parts["gm_ln_b"][l] = g_lb
        small_parts["gm_w_spatial"][l] = g_ws
        small_parts["gm_b_spatial"][l] = g_bs[:, :, 0]
        small_parts["pool_w"][l] = g_pw
        small_parts["pool_scale"][l] = g_ps

    dmod_mine = jnp.stack(dmod).reshape(nl, nb, NMOD * D)
    names = list(small_parts)
    stacked = [jnp.stack(small_parts[k]).astype(BF16 if k in ("gm_w_spatial", "pool_w") else F32) for k in names]
    gathered_small = _exchange(stacked + [dfinal_part, dmod_mine], "gather_small", False)
    dmod_all = jnp.transpose(gathered_small[-1], (1, 0, 2, 3)).reshape(nl, ntot, NMOD * D)
    dfinal_all = gathered_small[-2].reshape(ntot, D)

    results = {}
    weights = dict(rms_g1=(rms_g1, m_rms_g1, v_rms_g1), rms_g2=(rms_g2, m_rms_g2, v_rms_g2),
                   gm_ln_g=(gm_ln_g, m_gm_ln_g, v_gm_ln_g), gm_ln_b=(gm_ln_b, m_gm_ln_b, v_gm_ln_b),
                   gm_w_spatial=(gm_w_spatial, m_gm_w_spatial, v_gm_w_spatial),
                   gm_b_spatial=(gm_b_spatial, m_gm_b_spatial, v_gm_b_spatial),
                   pool_w=(pool_w, m_pool_w, v_pool_w), pool_scale=(pool_scale, m_pool_scale, v_pool_scale))
    for k, parts in zip(names, gathered_small[:len(names)]):
        w, m, v = weights[k]
        results[k] = _adamw_reduce("adamw_" + k, parts.reshape((NDEV,) + w.shape), w, m, v)
    results["final_g"] = _adamw_reduce("adamw_final_g", dfinal_all, final_g, m_final_g, v_final_g)
    results["b_ada"] = _adamw_reduce("adamw_b_ada", jnp.transpose(dmod_all, (1, 0, 2)), b_ada, m_b_ada, v_b_ada)
    dmod_blk = lax.dynamic_slice_in_dim(dmod_all, me * 768, 768, axis=2)
    g_w_ada = _ada_bwd(c_all, dmod_blk)
    results["w_ada"] = _adamw_reduce("adamw_w_ada", g_w_ada[None], w_ada, m_w_ada, v_w_ada)
    stacked_w = dict(w_in=w_in, w_branch=w_branch, w_out=w_out, w_ffn_in=w_ffn_in_t, w_ffn_out=w_ffn_out)
    for name in big_names:
        results[name] = tuple(b.reshape(stacked_w[name].shape) for b in bufs[name])
    results["w_ffn_in"] = tuple(jnp.swapaxes(b, 1, 2) for b in results["w_ffn_in"])

    order = ["rms_g1", "rms_g2", "w_ada", "b_ada", "w_in", "gm_ln_g", "gm_ln_b", "gm_w_spatial", "gm_b_spatial",
             "pool_w", "pool_scale", "w_branch", "w_out", "w_ffn_in", "w_ffn_out", "final_g"]
    out = [loss, dx]
    for i in range(4):
        out.extend(results[k][i] for k in order)
    return tuple(out)
```

```python
import functools
import math

import jax
import jax.numpy as jnp
from jax import lax
from jax.experimental import pallas as pl
from jax.experimental.pallas import tpu as pltpu

F32 = jnp.float32
BF16 = jnp.bfloat16
MESH = pl.DeviceIdType.MESH

D = 1024
BW = 512
NB = 3
CH = 128
NG = 4
HD = 64
POOL_WINDOWS = (2, 4, 8, 16)
DFF = 2816
NMOD = 6
EPS = 1e-6
IN_COLS = 6 * D
NDEV = 8
FF_IN_SHARD = 2 * DFF // NDEV
FF_HALF = FF_IN_SHARD // 2
FF_HALF_PAD = 384
FF_IN_PAD = 2 * FF_HALF_PAD
FFP = NDEV // 2 * FF_IN_PAD

ADAM_LR = 0.001
ADAM_B1 = 0.9
ADAM_B2 = 0.999
ADAM_EPS = 1e-08
ADAM_WD = 0.01
ADAM_STEP = 10

VMEM_LIMIT = 48 * 1024 * 1024
BIG_ROWS = 2048
NN = (((1,), (0,)), ((), ()))
NT = (((1,), (1,)), ((), ()))
TN = (((0,), (0,)), ((), ()))


def _cp(sem=None):
    return pltpu.CompilerParams(dimension_semantics=sem, vmem_limit_bytes=VMEM_LIMIT)


def _dot(a, b, dims=NN):
    return lax.dot_general(a, b, dims, preferred_element_type=F32)


def _my_index():
    return 4 * lax.axis_index("x") + 2 * lax.axis_index("y") + lax.axis_index("c")


def _peer(k):
    x, y, c = lax.axis_index("x"), lax.axis_index("y"), lax.axis_index("c")
    px = 1 - x if k & 4 else x
    py = 1 - y if k & 2 else y
    pc = 1 - c if k & 1 else c
    return (px, py, pc), 4 * px + 2 * py + pc


def _exchange(xs, name, all_to_all):
    n = len(xs)

    def body(*refs):
        _exchange_start(refs[:n], refs[n:2 * n], refs[2 * n:], all_to_all)
        _exchange_relay(refs[:n], refs[n:2 * n], refs[2 * n:], all_to_all)
        _exchange_finish(refs[:n], refs[n:2 * n], refs[2 * n:], all_to_all)

    return pl.pallas_call(
        body, name=name, out_shape=_exchange_out_shape(xs, all_to_all),
        in_specs=[_HBM] * n, out_specs=[_HBM] * n, scratch_shapes=_exchange_sems(n),
    )(*xs)


_HBM = pl.BlockSpec(memory_space=pl.ANY)


def _exchange_out_shape(xs, all_to_all):
    if all_to_all:
        return [jax.ShapeDtypeStruct(x.shape, x.dtype) for x in xs]
    return [jax.ShapeDtypeStruct((NDEV,) + x.shape, x.dtype) for x in xs]


def _exchange_sems(n):
    return [pltpu.SemaphoreType.DMA((n * 7,)), pltpu.SemaphoreType.DMA((n * 7,)), pltpu.SemaphoreType.DMA((n,))]


def _all_to_all_copies(ins, outs, sems):
    send_sems, recv_sems, local_sems = sems
    me = _my_index()
    local, sends, recvs = [], [], []
    for a in range(len(ins)):
        local.append(pltpu.make_async_copy(ins[a].at[me], outs[a].at[me], local_sems.at[a]))
    for k in range(1, NDEV):
        dev, idx = _peer(k)
        for a in range(len(ins)):
            sem = dict(send_sem=send_sems.at[a * 7 + k - 1], recv_sem=recv_sems.at[a * 7 + k - 1],
                       device_id=dev, device_id_type=MESH)
            sends.append(pltpu.make_async_remote_copy(src_ref=ins[a].at[idx], dst_ref=outs[a].at[me], **sem))
            recvs.append(pltpu.make_async_remote_copy(src_ref=ins[a].at[idx], dst_ref=outs[a].at[idx], **sem))
    return local, sends, recvs


def _gather_copies(ins, outs, sems):
    send_sems, recv_sems, local_sems = sems
    x, y, c = lax.axis_index("x"), lax.axis_index("y"), lax.axis_index("c")
    me, other = 4 * x + 2 * y + c, 4 * x + 2 * y + (1 - c)
    other_dev = (x, y, 1 - c)
    chips = [(1 - x, y), (x, 1 - y), (1 - x, 1 - y)]
    local, own, relay, from_other = [], [], [], []
    for a in range(len(ins)):
        def copy(k, src, block, dev, a=a):
            return pltpu.make_async_remote_copy(
                src_ref=src, dst_ref=outs[a].at[block], send_sem=send_sems.at[a * 7 + k],
                recv_sem=recv_sems.at[a * 7 + k], device_id=dev, device_id_type=MESH)

        local.append(pltpu.make_async_copy(ins[a], outs[a].at[me], local_sems.at[a]))
        own.append(copy(0, ins[a], me, other_dev))
        from_other.append(copy(0, ins[a], other, other_dev))
        for j, (px, py) in enumerate(chips):
            far = 4 * px + 2 * py + c
            own.append(copy(1 + j, ins[a], me, (px, py, c)))
            relay.append((copy(1 + j, ins[a], far, (px, py, c)), copy(4 + j, outs[a].at[far], far, other_dev)))
            from_other.append(copy(4 + j, ins[a], 4 * px + 2 * py + (1 - c), other_dev))
    return local, own, relay, from_other


def _exchange_start(ins, outs, sems, all_to_all):
    local, sends = (_all_to_all_copies if all_to_all else _gather_copies)(ins, outs, sems)[:2]
    for cp in local + sends:
        cp.start()


def _exchange_relay(ins, outs, sems, all_to_all):
    if not all_to_all:
        for arrival, passing_on in _gather_copies(ins, outs, sems)[2]:
            arrival.wait_recv()
            passing_on.start()


def _exchange_finish(ins, outs, sems, all_to_all):
    if all_to_all:
        local, sends, recvs = _all_to_all_copies(ins, outs, sems)
    else:
        local, own, relay, recvs = _gather_copies(ins, outs, sems)
        sends = own + [passing_on for _, passing_on in relay]
    for cp in sends:
        cp.wait_send()
    for cp in recvs:
        cp.wait_recv()
    for cp in local:
        cp.wait()


def _host_exchange(body, n_in, n_out, grid, xs, all_to_all):
    n = len(xs)
    if n == 0:
        return body, [], [], [], []
    steps = math.prod(grid)
    half = steps * 3 // 4 if steps >= 4 else steps - 1

    def hosted(*refs):
        ins, ex_ins = refs[:n_in], refs[n_in:n_in + n]
        outs, ex_outs = refs[n_in + n:n_in + n + n_out], refs[n_in + n + n_out:n_in + 2 * n + n_out]
        scratch = refs[n_in + 2 * n + n_out:]
        own, sems = scratch[:len(scratch) - 3], scratch[len(scratch) - 3:]
        step = 0
        for a in range(len(grid)):
            step = step * grid[a] + pl.program_id(a)

        @pl.when(step == 0)
        def _():
            _exchange_start(ex_ins, ex_outs, sems, all_to_all)

        body(*ins, *outs, *own)

        @pl.when(step == half)
        def _():
            _exchange_relay(ex_ins, ex_outs, sems, all_to_all)

        @pl.when(step == steps - 1)
        def _():
            _exchange_finish(ex_ins, ex_outs, sems, all_to_all)

    return hosted, [_HBM] * n, [_HBM] * n, _exchange_out_shape(xs, all_to_all), _exchange_sems(n)


def _mm(name, a, b, grid, a_spec, b_spec, o_spec, out_sds, dims, acc_shape, carried=(), all_to_all=True):
    nk = grid[2]

    if nk == 1:
        def body(a_ref, b_ref, o_ref):
            o_ref[...] = _dot(a_ref[...].astype(BF16), b_ref[...].astype(BF16), dims).astype(o_ref.dtype)
        scratch = []
    else:
        def body(a_ref, b_ref, o_ref, acc_ref):
            k = pl.program_id(2)

            @pl.when(k == 0)
            def _():
                acc_ref[...] = jnp.zeros_like(acc_ref)

            acc_ref[...] += _dot(a_ref[...].astype(BF16), b_ref[...].astype(BF16), dims)

            @pl.when(k == nk - 1)
            def _():
                o_ref[...] = acc_ref[...].astype(o_ref.dtype)
        scratch = [pltpu.VMEM(acc_shape, F32)]

    body, ex_in, ex_out, ex_shape, ex_sems = _host_exchange(body, 2, 1, grid, carried, all_to_all)
    outs = pl.pallas_call(
        body, name=name, grid=grid, in_specs=[a_spec, b_spec] + ex_in, out_specs=[o_spec] + ex_out,
        out_shape=[out_sds] + ex_shape,
        scratch_shapes=scratch + ex_sems,
        compiler_params=_cp(("arbitrary",) * 3 if carried else ("parallel", "parallel", "arbitrary")),
    )(a, b, *carried)
    return (outs[0], outs[1:]) if carried else outs[0]


def _row_tile(t, want):
    tm = min(t, want)
    assert t % tm == 0
    return tm


def _mm_colblocked(name, a, wg, out_dtype, gather=()):
    t = a.shape[0]
    tm = _row_tile(t, BIG_ROWS)
    return _mm(name, a, wg, (t // tm, NDEV, 1),
               pl.BlockSpec((tm, D), lambda i, j, k: (i, 0)),
               pl.BlockSpec((None, D, 768), lambda i, j, k: (j, 0, 0)),
               pl.BlockSpec((tm, 768), lambda i, j, k: (i, j)),
               jax.ShapeDtypeStruct((t, NDEV * 768), out_dtype), NN, (tm, 768), gather, False)


def _mm_colblocked_nt(name, g, wg, out_dtype, scatter=()):
    t = g.shape[0]
    tm = _row_tile(t, BIG_ROWS)
    return _mm(name, g, wg, (t // tm, 1, NDEV),
               pl.BlockSpec((tm, 768), lambda i, j, k: (i, k)),
               pl.BlockSpec((None, D, 768), lambda i, j, k: (k, 0, 0)),
               pl.BlockSpec((tm, D), lambda i, j, k: (i, 0)),
               jax.ShapeDtypeStruct((t, D), out_dtype), NT, (tm, D), scatter)


_HALF = NDEV // 2


def _ffn_in_fwd(h2, wg, gather=()):
    t = h2.shape[0]
    tm = _row_tile(t, 1024)

    def body(a_ref, wg_ref, wu_ref, g_ref, u_ref, act_ref):
        a = a_ref[...]
        g = _dot(a, wg_ref[...])
        u = _dot(a, wu_ref[...])
        g_ref[...] = g.astype(BF16)
        u_ref[...] = u.astype(BF16)
        act_ref[...] = (g * jax.nn.sigmoid(g) * u).astype(BF16)

    tile = pl.BlockSpec((tm, 768), lambda i, j: (i, j))
    grid = (t // tm, _HALF)
    body, ex_in, ex_out, ex_shape, ex_sems = _host_exchange(body, 3, 3, grid, gather, False)
    outs = pl.pallas_call(
        body, name="ffn_in_fwd", grid=grid,
        in_specs=[pl.BlockSpec((tm, D), lambda i, j: (i, 0)),
                  pl.BlockSpec((None, D, 768), lambda i, j: (j, 0, 0)),
                  pl.BlockSpec((None, D, 768), lambda i, j: (j + _HALF, 0, 0))] + ex_in,
        out_specs=[tile] * 3 + ex_out, out_shape=[jax.ShapeDtypeStruct((t, FFP), BF16)] * 3 + ex_shape,
        scratch_shapes=ex_sems,
        compiler_params=_cp(("arbitrary", "arbitrary") if gather else ("parallel", "parallel")),
    )(h2, wg, wg, *gather)
    return outs[0], outs[1], outs[2], outs[3:]


def _ffn_out_dgrad(dfo, w, fg, fu):
    t = dfo.shape[0]
    tm = _row_tile(t, 1024)

    def body(a_ref, w_ref, g_ref, u_ref, dg_ref, du_ref):
        d = _dot(a_ref[...], w_ref[...], NT)
        g = g_ref[...].astype(F32)
        s = jax.nn.sigmoid(g)
        gs = g * s
        dg_ref[...] = (d * u_ref[...].astype(F32) * (s + gs * (1.0 - s))).astype(BF16)
        du_ref[...] = (d * gs).astype(BF16)

    tile = pl.BlockSpec((tm, 768), lambda i, j: (i, j))
    return pl.pallas_call(
        body, name="ffn_out_dgrad", grid=(t // tm, _HALF),
        in_specs=[pl.BlockSpec((tm, D), lambda i, j: (i, 0)),
                  pl.BlockSpec((768, D), lambda i, j: (j, 0)), tile, tile],
        out_specs=[tile] * 2, out_shape=[jax.ShapeDtypeStruct((t, FFP), BF16)] * 2,
        compiler_params=_cp(("parallel", "parallel")),
    )(dfo, w, fg, fu)


def _ffn_in_dgrad(dg, du, wg):
    t = dg.shape[0]
    tm = _row_tile(t, BIG_ROWS)

    def body(g_ref, u_ref, w_ref, o_ref, acc_ref):
        k = pl.program_id(1)

        @pl.when(k == 0)
        def _():
            acc_ref[...] = jnp.zeros_like(acc_ref)

        @pl.when(k < _HALF)
        def _():
            acc_ref[...] += _dot(g_ref[...], w_ref[...], NT)

        @pl.when(k >= _HALF)
        def _():
            acc_ref[...] += _dot(u_ref[...], w_ref[...], NT)

        @pl.when(k == NDEV - 1)
        def _():
            o_ref[...] = acc_ref[...]

    return pl.pallas_call(
        body, name="ffn_in_dgrad", grid=(t // tm, NDEV),
        in_specs=[pl.BlockSpec((tm, 768), lambda i, k: (i, jnp.minimum(k, _HALF - 1))),
                  pl.BlockSpec((tm, 768), lambda i, k: (i, jnp.maximum(k - _HALF, 0))),
                  pl.BlockSpec((None, D, 768), lambda i, k: (k, 0, 0))],
        out_specs=pl.BlockSpec((tm, D), lambda i, k: (i, 0)),
        out_shape=jax.ShapeDtypeStruct((t, D), F32),
        scratch_shapes=[pltpu.VMEM((tm, D), F32)],
        compiler_params=_cp(("parallel", "arbitrary")),
    )(dg, du, wg)


def _ffn_in_wgrad(h2, dg, du):
    t = h2.shape[0]
    tk = _row_tile(t, BIG_ROWS)
    nk = t // tk

    def body(a_ref, g_ref, u_ref, o_ref, acc_ref):
        j, k = pl.program_id(0), pl.program_id(1)

        @pl.when(k == 0)
        def _():
            acc_ref[...] = jnp.zeros_like(acc_ref)

        @pl.when(j < _HALF)
        def _():
            acc_ref[...] += _dot(g_ref[...], a_ref[...], TN)

        @pl.when(j >= _HALF)
        def _():
            acc_ref[...] += _dot(u_ref[...], a_ref[...], TN)

        @pl.when(k == nk - 1)
        def _():
            o_ref[...] = acc_ref[...].astype(BF16)

    return pl.pallas_call(
        body, name="ffn_in_wgrad", grid=(NDEV, nk),
        in_specs=[pl.BlockSpec((tk, D), lambda j, k: (k, 0)),
                  pl.BlockSpec((tk, 768), lambda j, k: (jnp.where(j < _HALF, k, 0), jnp.minimum(j, _HALF - 1))),
                  pl.BlockSpec((tk, 768), lambda j, k: (jnp.where(j < _HALF, 0, k), jnp.maximum(j - _HALF, 0)))],
        out_specs=pl.BlockSpec((None, 768, D), lambda j, k: (j, 0, 0)),
        out_shape=jax.ShapeDtypeStruct((NDEV, 768, D), BF16),
        scratch_shapes=[pltpu.VMEM((768, D), F32)],
        compiler_params=_cp(("parallel", "arbitrary")),
    )(h2, dg, du)


def _mm_colblocked_tn(name, a, g):
    t = a.shape[0]
    tk = _row_tile(t, BIG_ROWS)
    return _mm(name, a, g, (1, NDEV, t // tk),
               pl.BlockSpec((tk, D), lambda i, j, k: (k, 0)),
               pl.BlockSpec((tk, 768), lambda i, j, k: (k, j)),
               pl.BlockSpec((None, D, 768), lambda i, j, k: (j, 0, 0)),
               jax.ShapeDtypeStruct((NDEV, D, 768), BF16), TN, (D, 768))


def _mm_nt(name, a, w, out_dtype, a_col=0, w_lead=None):
    t = a.shape[0]
    if w_lead is None:
        kdim, n = w.shape
        b_spec = pl.BlockSpec((min(kdim, 1024), n), lambda i, j, k: (j, 0))
    else:
        _, kdim, n = w.shape
        b_spec = pl.BlockSpec((None, min(kdim, 1024), n), lambda i, j, k: (w_lead, j, 0))
    tn = min(kdim, 1024)
    tm = _row_tile(t, BIG_ROWS)
    return _mm(name, a, w, (t // tm, kdim // tn, 1),
               pl.BlockSpec((tm, n), lambda i, j, k: (i, a_col)),
               b_spec,
               pl.BlockSpec((tm, tn), lambda i, j, k: (i, j)),
               jax.ShapeDtypeStruct((t, kdim), out_dtype), NT, (tm, tn))


def _mm_tn(name, a, g, out_dtype=BF16, gather=()):
    t, kdim = a.shape
    n = g.shape[1]
    tk = _row_tile(t, BIG_ROWS)
    tm = min(kdim, 1024)
    tn = min(n, 1024)
    return _mm(name, a, g, (kdim // tm, n // tn, t // tk),
               pl.BlockSpec((tk, tm), lambda i, j, k: (k, i)),
               pl.BlockSpec((tk, tn), lambda i, j, k: (k, j)),
               pl.BlockSpec((tm, tn), lambda i, j, k: (i, j)),
               jax.ShapeDtypeStruct((kdim, n), out_dtype), TN, (tm, tn), gather, False)


def _mm_residual(name, a, w, x, gt, seq):
    t, kdim = a.shape
    tm = _row_tile(seq, 1024)
    tn = D
    tk = min(kdim, 1024)
    nk = kdim // tk
    per = seq // tm

    def body(a_ref, w_ref, x_ref, gt_ref, xo_ref, y_ref, acc_ref):
        k = pl.program_id(2)

        @pl.when(k == 0)
        def _():
            acc_ref[...] = jnp.zeros_like(acc_ref)

        acc_ref[...] += _dot(a_ref[...], w_ref[...])

        @pl.when(k == nk - 1)
        def _():
            y = acc_ref[...]
            xo_ref[...] = x_ref[...] + gt_ref[0] * y
            y_ref[...] = y.astype(BF16)

    return pl.pallas_call(
        body, name=name, grid=(t // tm, D // tn, nk),
        in_specs=[pl.BlockSpec((tm, tk), lambda i, j, k: (i, k)),
                  pl.BlockSpec((tk, tn), lambda i, j, k: (k, j)),
                  pl.BlockSpec((tm, tn), lambda i, j, k: (i, j)),
                  pl.BlockSpec((1, 1, tn), lambda i, j, k: (i // per, 0, j))],
        out_specs=[pl.BlockSpec((tm, tn), lambda i, j, k: (i, j)),
                   pl.BlockSpec((tm, tn), lambda i, j, k: (i, j))],
        out_shape=[jax.ShapeDtypeStruct((t, D), F32), jax.ShapeDtypeStruct((t, D), BF16)],
        scratch_shapes=[pltpu.VMEM((tm, tn), F32)],
        compiler_params=_cp(("parallel", "parallel", "arbitrary")),
    )(a, w, x, gt)


def _ada_fwd(c_all, w_ada, b_blk):
    nl = w_ada.shape[0]
    nb = c_all.shape[0]

    def body(c_ref, w_ref, b_ref, o_ref):
        c = c_ref[...]
        ca = (c * jax.nn.sigmoid(c)).astype(BF16)
        o_ref[...] = _dot(ca, w_ref[...].astype(BF16)) + b_ref[...]

    return pl.pallas_call(
        body, name="ada_fwd", grid=(nl,),
        in_specs=[pl.BlockSpec((nb, D), lambda l: (0, 0)),
                  pl.BlockSpec((None, D, 768), lambda l: (l, 0, 0)),
                  pl.BlockSpec((None, 1, 768), lambda l: (l, 0, 0))],
        out_specs=pl.BlockSpec((None, nb, 768), lambda l: (l, 0, 0)),
        out_shape=jax.ShapeDtypeStruct((nl, nb, 768), F32),
        compiler_params=_cp(("parallel",)),
    )(c_all, w_ada, b_blk)


def _ada_bwd(c_all, dmod_blk):
    nl = dmod_blk.shape[0]
    nb = c_all.shape[0]

    def body(c_ref, d_ref, o_ref):
        c = c_ref[...]
        ca = (c * jax.nn.sigmoid(c)).astype(BF16)
        o_ref[...] = _dot(ca, d_ref[...].astype(BF16), TN)

    return pl.pallas_call(
        body, name="ada_bwd", grid=(nl,),
        in_specs=[pl.BlockSpec((nb, D), lambda l: (0, 0)),
                  pl.BlockSpec((None, nb, 768), lambda l: (l, 0, 0))],
        out_specs=pl.BlockSpec((None, D, 768), lambda l: (l, 0, 0)),
        out_shape=jax.ShapeDtypeStruct((nl, D, 768), F32),
        compiler_params=_cp(("parallel",)),
    )(c_all, dmod_blk)


def _seq_tile(seq):
    return _row_tile(seq, 512)


def _norm_mod_fwd(x, g, sc, sh):
    nb, seq, _ = x.shape
    ts = _seq_tile(seq)

    def body(x_ref, g_ref, sc_ref, sh_ref, h_ref):
        xv = x_ref[0]
        r = lax.rsqrt(jnp.mean(xv * xv, axis=-1, keepdims=True) + EPS)
        h_ref[0] = ((xv * r) * g_ref[...] * (1.0 + sc_ref[0]) + sh_ref[0]).astype(BF16)

    return pl.pallas_call(
        body, name="norm_mod_fwd", grid=(nb, seq // ts),
        in_specs=[pl.BlockSpec((1, ts, D), lambda b, s: (b, s, 0)),
                  pl.BlockSpec((1, D), lambda b, s: (0, 0)),
                  pl.BlockSpec((1, 1, D), lambda b, s: (b, 0, 0)),
                  pl.BlockSpec((1, 1, D), lambda b, s: (b, 0, 0))],
        out_specs=pl.BlockSpec((1, ts, D), lambda b, s: (b, s, 0)),
        out_shape=jax.ShapeDtypeStruct((nb, seq, D), BF16),
        compiler_params=_cp(("parallel", "parallel")),
    )(x, g, sc, sh)


def _gate_bwd_tile(d, y_ref, gt_ref, dy_ref, dgt_ref):
    @pl.when(pl.program_id(1) == 0)
    def _():
        dgt_ref[...] = jnp.zeros_like(dgt_ref)

    dy_ref[0] = (gt_ref[0] * d).astype(BF16)
    dgt_ref[0] += jnp.sum(d * y_ref[0].astype(F32), axis=0, keepdims=True)


def _norm_mod_bwd(x, dh, dres, g, sc, gate=None):
    nb, seq, _ = x.shape
    ts = _seq_tile(seq)

    def body(x_ref, dh_ref, dres_ref, g_ref, sc_ref, *rest):
        if gate is None:
            dx_ref, dsh_ref, dsc_ref, dg_ref = rest
        else:
            y_ref, gt_ref, dx_ref, dsh_ref, dsc_ref, dg_ref, dy_ref, dgt_ref = rest

        @pl.when(pl.program_id(1) == 0)
        def _():
            dsh_ref[...] = jnp.zeros_like(dsh_ref)
            dsc_ref[...] = jnp.zeros_like(dsc_ref)
            dg_ref[...] = jnp.zeros_like(dg_ref)

        xv = x_ref[0]
        dh = dh_ref[0]
        gv = g_ref[...]
        onesc = 1.0 + sc_ref[0]
        r = lax.rsqrt(jnp.mean(xv * xv, axis=-1, keepdims=True) + EPS)
        xh = xv * r
        dsh_ref[0] += jnp.sum(dh, axis=0, keepdims=True)
        dsc_ref[0] += jnp.sum(dh * (xh * gv), axis=0, keepdims=True)
        dg_ref[0] += jnp.sum(dh * onesc * xh, axis=0, keepdims=True)
        dxh = dh * (gv * onesc)
        dx = r * (dxh - xh * jnp.mean(dxh * xh, axis=-1, keepdims=True))
        dx_total = dres_ref[0] + dx
        dx_ref[0] = dx_total
        if gate is not None:
            _gate_bwd_tile(dx_total, y_ref, gt_ref, dy_ref, dgt_ref)

    vec = jax.ShapeDtypeStruct((nb, 1, D), F32)
    vspec = pl.BlockSpec((1, 1, D), lambda b, s: (b, 0, 0))
    tile = pl.BlockSpec((1, ts, D), lambda b, s: (b, s, 0))
    gated = gate is not None
    return pl.pallas_call(
        body, name="norm_mod_bwd", grid=(nb, seq // ts),
        in_specs=[tile, tile, tile, pl.BlockSpec((1, D), lambda b, s: (0, 0)), vspec] + [tile, vspec] * gated,
        out_specs=[tile, vspec, vspec, vspec] + [tile, vspec] * gated,
        out_shape=[jax.ShapeDtypeStruct((nb, seq, D), F32), vec, vec, vec]
        + [jax.ShapeDtypeStruct((nb, seq, D), BF16), vec] * gated,
        compiler_params=_cp(("parallel", "arbitrary")),
    )(x, dh, dres, g, sc, *(gate or ()))


def _loss_head(x, tgt, g, y, gt):
    nb, seq, _ = x.shape
    ts = _seq_tile(seq)

    def body(x_ref, t_ref, g_ref, y_ref, gt_ref, dx_ref, loss_ref, dg_ref, dy_ref, dgt_ref):
        @pl.when(pl.program_id(1) == 0)
        def _():
            loss_ref[...] = jnp.zeros_like(loss_ref)
            dg_ref[...] = jnp.zeros_like(dg_ref)

        xv = x_ref[0]
        gv = g_ref[...]
        r = lax.rsqrt(jnp.mean(xv * xv, axis=-1, keepdims=True) + EPS)
        xh = xv * r
        err = xh * gv - t_ref[0]
        per_tok = jnp.mean(err * err, axis=-1, keepdims=True)
        loss_ref[0] += 0.5 * jnp.sum(per_tok, axis=0, keepdims=True)
        dy = err * (1.0 / D)
        dg_ref[0] += jnp.sum(dy * xh, axis=0, keepdims=True)
        dxh = dy * gv
        dx = r * (dxh - xh * jnp.mean(dxh * xh, axis=-1, keepdims=True))
        dx_ref[0] = dx
        _gate_bwd_tile(dx, y_ref, gt_ref, dy_ref, dgt_ref)

    tile = pl.BlockSpec((1, ts, D), lambda b, s: (b, s, 0))
    vspec = pl.BlockSpec((1, 1, D), lambda b, s: (b, 0, 0))
    vec = jax.ShapeDtypeStruct((nb, 1, D), F32)
    return pl.pallas_call(
        body, name="loss_head", grid=(nb, seq // ts),
        in_specs=[tile, tile, pl.BlockSpec((1, D), lambda b, s: (0, 0)), tile, vspec],
        out_specs=[tile, pl.BlockSpec((1, 1, 128), lambda b, s: (b, 0, 0)), vspec, tile, vspec],
        out_shape=[jax.ShapeDtypeStruct((nb, seq, D), F32), jax.ShapeDtypeStruct((nb, 1, 128), F32), vec,
                   jax.ShapeDtypeStruct((nb, seq, D), BF16), vec],
        compiler_params=_cp(("parallel", "arbitrary")),
    )(x, tgt, g, y, gt)


_GELU_C = math.sqrt(2.0 / math.pi)


def _gelu(x):
    return 0.5 * x * (1.0 + jnp.tanh(_GELU_C * (x + 0.044715 * (x * x * x))))


def _gelu_and_grad(x):
    t = jnp.tanh(_GELU_C * (x + 0.044715 * (x * x * x)))
    y = 0.5 * x * (1.0 + t)
    dy = 0.5 * (1.0 + t) + 0.5 * x * (1.0 - t * t) * (_GELU_C * (1.0 + 3.0 * 0.044715 * (x * x)))
    return y, dy


def _tril_mask():
    row = lax.broadcasted_iota(jnp.int32, (CH, CH), 0)
    col = lax.broadcasted_iota(jnp.int32, (CH, CH), 1)
    return row >= col


def _gmlp_fwd(proj, ln_g, ln_b, ws, bst):
    t = proj.shape[0]
    tm = _row_tile(t, 512)

    def body(u_ref, v_ref, lg_ref, lb_ref, ws_ref, bst_ref, o_ref):
        tril = _tril_mask()
        wm = [jnp.where(tril, ws_ref[g], 0.0).astype(BF16) for g in range(NG)]
        for ch in range(tm // CH):
            rows = slice(ch * CH, (ch + 1) * CH)
            u = _gelu(u_ref[rows, :].astype(F32))
            v = _gelu(v_ref[rows, :].astype(F32))
            mu = jnp.mean(v, axis=-1, keepdims=True)
            xc = v - mu
            rstd = lax.rsqrt(jnp.mean(xc * xc, axis=-1, keepdims=True) + EPS)
            vn = ((xc * rstd) * lg_ref[...] + lb_ref[...]).astype(BF16)
            for g in range(NG):
                cols = slice(g * CH, (g + 1) * CH)
                s = _dot(wm[g], vn[:, cols]) + bst_ref[:, g:g + 1]
                o_ref[rows, cols] = (u[:, cols] * s).astype(BF16)

    return pl.pallas_call(
        body, name="gmlp_fwd", grid=(t // tm,),
        in_specs=[pl.BlockSpec((tm, BW), lambda i: (i, 0)),
                  pl.BlockSpec((tm, BW), lambda i: (i, 1)),
                  pl.BlockSpec((1, BW), lambda i: (0, 0)),
                  pl.BlockSpec((1, BW), lambda i: (0, 0)),
                  pl.BlockSpec((NG, CH, CH), lambda i: (0, 0, 0)),
                  pl.BlockSpec((CH, NG), lambda i: (0, 0))],
        out_specs=pl.BlockSpec((tm, BW), lambda i: (i, 0)),
        out_shape=jax.ShapeDtypeStruct((t, BW), BF16),
        compiler_params=_cp(("parallel",)),
    )(proj, proj, ln_g, ln_b, ws, bst)


def _gmlp_bwd(proj, dout, ln_g, ln_b, ws, bst, dproj):
    t = proj.shape[0]
    tm = _row_tile(t, 512)

    def body(u_ref, v_ref, do_ref, lg_ref, lb_ref, ws_ref, bst_ref, buf_ref,
             dp_ref, gws_ref, gbs_ref, glg_ref, glb_ref):
        @pl.when(pl.program_id(0) == 0)
        def _():
            gws_ref[...] = jnp.zeros_like(gws_ref)
            gbs_ref[...] = jnp.zeros_like(gbs_ref)
            glg_ref[...] = jnp.zeros_like(glg_ref)
            glb_ref[...] = jnp.zeros_like(glb_ref)

        tril = _tril_mask()
        wm = [jnp.where(tril, ws_ref[g], 0.0).astype(BF16) for g in range(NG)]
        ones = jnp.ones((CH, CH), BF16)
        lg = lg_ref[...]
        for ch in range(tm // CH):
            rows = slice(ch * CH, (ch + 1) * CH)
            u, du_fac = _gelu_and_grad(u_ref[rows, :].astype(F32))
            v, dv_fac = _gelu_and_grad(v_ref[rows, :].astype(F32))
            do = do_ref[rows, :].astype(F32)
            mu = jnp.mean(v, axis=-1, keepdims=True)
            xc = v - mu
            rstd = lax.rsqrt(jnp.mean(xc * xc, axis=-1, keepdims=True) + EPS)
            xh = xc * rstd
            vn = (xh * lg + lb_ref[...]).astype(BF16)
            dvn_parts = []
            for g in range(NG):
                cols = slice(g * CH, (g + 1) * CH)
                s = _dot(wm[g], vn[:, cols]) + bst_ref[:, g:g + 1]
                dp_ref[rows, cols] = (do[:, cols] * s * du_fac[:, cols]).astype(BF16)
                ds = (do[:, cols] * u[:, cols]).astype(BF16)
                gws_ref[g] += jnp.where(tril, _dot(ds, vn[:, cols], NT), 0.0)
                gbs_ref[g] += _dot(ds, ones)
                dvn_parts.append(_dot(wm[g], ds, TN))
            dvn = jnp.concatenate(dvn_parts, axis=1)
            glb_ref[...] += jnp.sum(dvn, axis=0, keepdims=True)
            glg_ref[...] += jnp.sum(dvn * xh, axis=0, keepdims=True)
            dxh = dvn * lg
            dv = rstd * (dxh - jnp.mean(dxh, axis=-1, keepdims=True)
                         - xh * jnp.mean(dxh * xh, axis=-1, keepdims=True))
            dp_ref[rows, BW:2 * BW] = (dv * dv_fac).astype(BF16)

    small = pl.BlockSpec((NG, CH, CH), lambda i: (0, 0, 0))
    vec = pl.BlockSpec((1, BW), lambda i: (0, 0))
    return pl.pallas_call(
        body, name="gmlp_bwd", grid=(t // tm,),
        in_specs=[pl.BlockSpec((tm, BW), lambda i: (i, 0)),
                  pl.BlockSpec((tm, BW), lambda i: (i, 1)),
                  pl.BlockSpec((tm, BW), lambda i: (i, 0)),
                  vec, vec, small, pl.BlockSpec((CH, NG), lambda i: (0, 0)), _HBM],
        out_specs=[pl.BlockSpec((tm, 2 * BW), lambda i: (i, 0)), small, small, vec, vec],
        out_shape=[jax.ShapeDtypeStruct((t, IN_COLS), BF16),
                   jax.ShapeDtypeStruct((NG, CH, CH), F32), jax.ShapeDtypeStruct((NG, CH, CH), F32),
                   jax.ShapeDtypeStruct((1, BW), F32), jax.ShapeDtypeStruct((1, BW), F32)],
        input_output_aliases={7: 0},
        compiler_params=_cp(("arbitrary",)),
    )(proj, proj, dout, ln_g, ln_b, ws, bst, dproj)


def _pool_bands():
    row = lax.broadcasted_iota(jnp.int32, (CH, CH), 0)
    col = lax.broadcasted_iota(jnp.int32, (CH, CH), 1)
    cur, prev = [], []
    for w in POOL_WINDOWS:
        cur.append(jnp.where((row >= col) & (row - col < w), 1.0, 0.0).astype(BF16))
        prev.append(jnp.where(row + CH - col < w, 1.0, 0.0).astype(BF16))
    return cur, prev


def _pool_inv_count(r0, w):
    pos = r0 + lax.broadcasted_iota(jnp.int32, (CH, 1), 0)
    return 1.0 / jnp.minimum(pos + 1, w).astype(F32)


def _pool_diff(x_ref, r0, rp, has_prev, cur, prev, g):
    cols = slice(g * CH, (g + 1) * CH)
    xc = x_ref[pl.ds(r0, CH), cols]
    xp = x_ref[pl.ds(rp, CH), cols]
    ws = _dot(cur[g], xc) + has_prev * _dot(prev[g], xp)
    return ws * _pool_inv_count(r0, POOL_WINDOWS[g]) - xc.astype(F32)


def _pool_fwd(proj3, pw, pscale):
    nb, seq, _ = proj3.shape
    nch = seq // CH

    def body(x_ref, pw_ref, ps_ref, o_ref):
        cur, prev = _pool_bands()
        pwb = [pw_ref[g].astype(BF16) for g in range(NG)]

        def chunk(ch, carry):
            r0 = pl.multiple_of(ch * CH, CH)
            rp = pl.multiple_of(jnp.maximum(ch - 1, 0) * CH, CH)
            has_prev = jnp.where(ch > 0, 1.0, 0.0)
            for g in range(NG):
                cols = slice(g * CH, (g + 1) * CH)
                d = _pool_diff(x_ref, r0, rp, has_prev, cur, prev, g)
                y = _dot(d.astype(BF16), pwb[g]) * ps_ref[:, cols]
                o_ref[pl.ds(r0, CH), cols] = y.astype(BF16)
            return carry

        lax.fori_loop(0, nch, chunk, 0, unroll=2)

    return pl.pallas_call(
        body, name="pool_fwd", grid=(nb,),
        in_specs=[pl.BlockSpec((None, seq, BW), lambda b: (b, 0, 5)),
                  pl.BlockSpec((NG, CH, CH), lambda b: (0, 0, 0)),
                  pl.BlockSpec((1, BW), lambda b: (0, 0))],
        out_specs=pl.BlockSpec((None, seq, BW), lambda b: (b, 0, 0)),
        out_shape=jax.ShapeDtypeStruct((nb, seq, BW), BF16),
        compiler_params=_cp(("parallel",)),
    )(proj3, pw, pscale)


def _pool_bwd(proj3, dout3, pw, pscale, dproj3):
    nb, seq, _ = proj3.shape
    nch = seq // CH

    def body(x_ref, do_ref, pw_ref, ps_ref, buf_ref, dx_ref, gpw_ref, gps_ref, e_ref):
        @pl.when(pl.program_id(0) == 0)
        def _():
            gpw_ref[...] = jnp.zeros_like(gpw_ref)
            gps_ref[...] = jnp.zeros_like(gps_ref)

        cur, prev = _pool_bands()
        pwb = [pw_ref[g].astype(BF16) for g in range(NG)]

        def first(ch, carry):
            r0 = pl.multiple_of(ch * CH, CH)
            rp = pl.multiple_of(jnp.maximum(ch - 1, 0) * CH, CH)
            has_prev = jnp.where(ch > 0, 1.0, 0.0)
            for g in range(NG):
                cols = slice(g * CH, (g + 1) * CH)
                d = _pool_diff(x_ref, r0, rp, has_prev, cur, prev, g).astype(BF16)
                do = do_ref[pl.ds(r0, CH), cols].astype(F32)
                ypre = _dot(d, pwb[g])
                gps_ref[:, cols] += jnp.sum(do * ypre, axis=0, keepdims=True)
                dyp = (do * ps_ref[:, cols]).astype(BF16)
                gpw_ref[g] += _dot(d, dyp, TN)
                e_ref[pl.ds(r0, CH), cols] = _dot(dyp, pwb[g], NT)
            return carry

        lax.fori_loop(0, nch, first, 0, unroll=2)

        def second(ch, carry):
            r0 = pl.multiple_of(ch * CH, CH)
            rn = pl.multiple_of(jnp.minimum(ch + 1, nch - 1) * CH, CH)
            has_next = jnp.where(ch < nch - 1, 1.0, 0.0)
            for g in range(NG):
                cols = slice(g * CH, (g + 1) * CH)
                w = POOL_WINDOWS[g]
                dd = e_ref[pl.ds(r0, CH), cols]
                ec = (dd * _pool_inv_count(r0, w)).astype(BF16)
                en = (e_ref[pl.ds(rn, CH), cols] * _pool_inv_count(rn, w)).astype(BF16)
                dx = _dot(cur[g], ec, TN) + has_next * _dot(prev[g], en, TN) - dd
                dx_ref[pl.ds(r0, CH), cols] = dx.astype(BF16)
            return carry

        lax.fori_loop(0, nch, second, 0, unroll=2)

    small = pl.BlockSpec((NG, CH, CH), lambda b: (0, 0, 0))
    vec = pl.BlockSpec((1, BW), lambda b: (0, 0))
    return pl.pallas_call(
        body, name="pool_bwd", grid=(nb,),
        in_specs=[pl.BlockSpec((None, seq, BW), lambda b: (b, 0, 5)),
                  pl.BlockSpec((None, seq, BW), lambda b: (b, 0, 0)), small, vec, _HBM],
        out_specs=[pl.BlockSpec((None, seq, BW), lambda b: (b, 0, 5)), small, vec],
        out_shape=[jax.ShapeDtypeStruct((nb, seq, IN_COLS), BF16),
                   jax.ShapeDtypeStruct((NG, CH, CH), F32), jax.ShapeDtypeStruct((1, BW), F32)],
        input_output_aliases={4: 0},
        scratch_shapes=[pltpu.VMEM((seq, BW), F32)],
        compiler_params=_cp(("arbitrary",)),
    )(proj3, dout3, pw, pscale, dproj3)


SB_BQ = 256
SB_BK = 256
SB_SCALE = HD ** -0.5


SB_EXIT = -110.0


def _sb_tile(qs, k, mask):
    z = _dot(qs, k, NT)
    lb = jnp.minimum(z, 0.0) - jnp.log(1.0 + jnp.exp(-jnp.abs(z)))
    lom = lb - z
    if mask is not None:
        lom = jnp.where(mask, lom, 0.0)
    return lb, lom


def _sb_alive(c):
    top = functools.reduce(jnp.maximum, [jnp.max(state[1]) for state in c])
    return (top > SB_EXIT).astype(jnp.int32)


def _sb_past_blocks(step, c, npast):
    def cond(s):
        return jnp.logical_and(s[0] < npast, s[1] > 0)

    def body(s):
        i, _, c = s
        c = step(pl.multiple_of((npast - 1 - i) * SB_BK, SB_BK), c, None)
        return i + 1, _sb_alive(c), c

    return lax.while_loop(cond, body, (jnp.int32(0), _sb_alive(c), c))[2]


def _sb_diag_mask(bq, d):
    row = lax.broadcasted_iota(jnp.int32, (bq, SB_BK), 0)
    col = lax.broadcasted_iota(jnp.int32, (bq, SB_BK), 1)
    return col + d * SB_BK < row


def _sb_scaled(q):
    return (q.astype(F32) * SB_SCALE).astype(BF16)


def _dot_tri(a, m):
    return _dot(a.astype(BF16), m)


def _dot_tri2(a, m):
    hi = a.astype(BF16)
    lo = (a - hi.astype(F32)).astype(BF16)
    return _dot(hi, m) + _dot(lo, m)


def _sb_fwd(proj3, gather=()):
    nb, seq, _ = proj3.shape
    bq = min(SB_BQ, seq)
    nq = seq // bq
    ndiag = bq // SB_BK

    def body(q_ref, k_ref, v_ref, o_ref):
        row = lax.broadcasted_iota(jnp.int32, (SB_BK, SB_BK), 0)
        col = lax.broadcasted_iota(jnp.int32, (SB_BK, SB_BK), 1)
        upper = jnp.where(row > col, 1.0, 0.0).astype(BF16)
        heads = [slice(hh * HD, (hh + 1) * HD) for hh in range(2)]

        def qloop(qi, carry):
            q0 = pl.multiple_of(qi * bq, bq)
            qs = [_sb_scaled(q_ref[pl.ds(q0, bq), lanes]) for lanes in heads]

            def step(k0, c, mask):
                tiles = [_sb_tile(q, k_ref[pl.ds(k0, SB_BK), lanes], mask) for lanes, q in zip(heads, qs)]
                sums = [_dot_tri(lom, upper) for _, lom in tiles]
                out = []
                for lanes, (acc, cr), (lb, lom), cs in zip(heads, c, tiles, sums):
                    a = jnp.exp(lb + (cs + cr))
                    if mask is not None:
                        a = jnp.where(mask, a, 0.0)
                    rsum = cs[:, 0:1] + lom[:, 0:1]
                    out.append((acc + _dot(a.astype(BF16), v_ref[pl.ds(k0, SB_BK), lanes]), cr + rsum))
                return tuple(out)

            c = tuple((jnp.zeros((bq, HD), F32), jnp.zeros((bq, 1), F32)) for _ in heads)
            for d in reversed(range(ndiag)):
                c = step(pl.multiple_of(q0 + d * SB_BK, SB_BK), c, _sb_diag_mask(bq, d))
            c = _sb_past_blocks(step, c, qi * ndiag)
            for lanes, (acc, _) in zip(heads, c):
                o_ref[pl.ds(q0, bq), lanes] = acc
            return carry

        lax.fori_loop(0, nq, qloop, 0)

    def spec(c0):
        return pl.BlockSpec((None, seq, 128), lambda b, p: (b, 0, c0 + p))

    grid = (nb, BW // 128)
    body, ex_in, ex_out, ex_shape, ex_sems = _host_exchange(body, 3, 1, grid, gather, False)
    outs = pl.pallas_call(
        body, name="sb_fwd", grid=grid,
        in_specs=[spec(8), spec(12), spec(16)] + ex_in,
        out_specs=[spec(0)] + ex_out,
        out_shape=[jax.ShapeDtypeStruct((nb, seq, BW), F32)] + ex_shape,
        scratch_shapes=ex_sems,
        compiler_params=_cp(("arbitrary", "arbitrary")),
    )(proj3, proj3, proj3, *gather)
    return outs[0], outs[1:]


def _sb_bwd(proj3, do3, o3, scatter=()):
    nb, seq, _ = proj3.shape
    bq = min(SB_BQ, seq)
    nq = seq // bq
    ndiag = bq // SB_BK
    assert ndiag == 1

    def body(q_ref, k_ref, v_ref, do_ref, o_ref, dq_ref, dk_ref, dv_ref, dk_acc, dv_acc):
        row = lax.broadcasted_iota(jnp.int32, (SB_BK, SB_BK), 0)
        col = lax.broadcasted_iota(jnp.int32, (SB_BK, SB_BK), 1)
        upper = jnp.where(row > col, 1.0, 0.0).astype(BF16)
        later = jnp.where(row >= col, 1.0, 0.0).astype(BF16)
        heads = [slice(hh * HD, (hh + 1) * HD) for hh in range(2)]

        def qloop(qi, carry):
            q0 = pl.multiple_of(qi * bq, bq)
            qs = [_sb_scaled(q_ref[pl.ds(q0, bq), lanes]) for lanes in heads]
            dos = [do_ref[pl.ds(q0, bq), lanes] for lanes in heads]
            gtot = [jnp.sum(do.astype(F32) * o_ref[pl.ds(q0, bq), lanes], axis=1, keepdims=True)
                    for do, lanes in zip(dos, heads)]

            def step(k0, c, mask):
                ks = [k_ref[pl.ds(k0, SB_BK), lanes] for lanes in heads]
                tiles = [_sb_tile(q, k, mask) for q, k in zip(qs, ks)]
                sums = [_dot_tri(lom, upper) for _, lom in tiles]
                das = [_dot(do, v_ref[pl.ds(k0, SB_BK), lanes], NT) for do, lanes in zip(dos, heads)]
                gls, avs = [], []
                for hh, (_, cr, _) in enumerate(c):
                    a = jnp.exp(tiles[hh][0] + (sums[hh] + cr))
                    if mask is not None:
                        a = jnp.where(mask, a, 0.0)
                    ab = a.astype(BF16)
                    avs.append(ab)
                    gls.append(das[hh] * ab.astype(F32))
                tails = [_dot_tri2(gl, later) for gl in gls]
                out = []
                for hh, (dq, cr, gdone) in enumerate(c):
                    lb, lom = tiles[hh]
                    pre = gtot[hh] - gdone - tails[hh]
                    dz = gls[hh] - jnp.exp(lb) * (gls[hh] + pre)
                    if mask is not None:
                        dz = jnp.where(mask, dz, 0.0)
                    dz = dz.astype(BF16)
                    dk_new, dv_new = _dot(dz, qs[hh], TN), _dot(avs[hh], dos[hh], TN)
                    if mask is not None:
                        dk_acc[hh, pl.ds(k0, SB_BK), :] = dk_new
                        dv_acc[hh, pl.ds(k0, SB_BK), :] = dv_new
                    else:
                        dk_acc[hh, pl.ds(k0, SB_BK), :] += dk_new
                        dv_acc[hh, pl.ds(k0, SB_BK), :] += dv_new
                    rsum = sums[hh][:, 0:1] + lom[:, 0:1]
                    out.append((dq + _dot(dz, ks[hh]), cr + rsum, gdone + tails[hh][:, 0:1]))
                return tuple(out)

            c = tuple((jnp.zeros((bq, HD), F32), jnp.zeros((bq, 1), F32), jnp.zeros((bq, 1), F32))
                      for _ in heads)
            for d in reversed(range(ndiag)):
                c = step(pl.multiple_of(q0 + d * SB_BK, SB_BK), c, _sb_diag_mask(bq, d))
            c = _sb_past_blocks(step, c, qi * ndiag)
            for lanes, (dq, _, _) in zip(heads, c):
                dq_ref[pl.ds(q0, bq), lanes] = (dq * SB_SCALE).astype(BF16)
            return carry

        lax.fori_loop(0, nq, qloop, 0)
        for hh in range(2):
            lanes = slice(hh * HD, (hh + 1) * HD)
            dk_ref[:, lanes] = dk_acc[hh].astype(BF16)
            dv_ref[:, lanes] = dv_acc[hh].astype(BF16)

    def spec(c0):
        return pl.BlockSpec((None, seq, 128), lambda b, p: (b, 0, c0 + p))

    grid = (nb, BW // 128)
    body, ex_in, ex_out, ex_shape, ex_sems = _host_exchange(body, 5, 3, grid, scatter, True)
    outs = pl.pallas_call(
        body, name="sb_bwd", grid=grid,
        in_specs=[spec(8), spec(12), spec(16), spec(0), spec(0)] + ex_in,
        out_specs=[spec(0), spec(0), spec(0)] + ex_out,
        out_shape=[jax.ShapeDtypeStruct((nb, seq, BW), BF16)] * 3 + ex_shape,
        scratch_shapes=[pltpu.VMEM((2, seq, HD), F32), pltpu.VMEM((2, seq, HD), F32)] + ex_sems,
        compiler_params=_cp(("arbitrary", "arbitrary")),
    )(proj3, proj3, proj3, do3, o3, *scatter)
    return outs[:3], outs[3:]


def _merge_fwd(brs, wb, proj):
    t = proj.shape[0]
    tm = _row_tile(t, 512)
    tn = 512
    nj = D // tn

    def body(b0, b1, b2, wb_ref, l0, l1, l2, m_ref, y0, y1, y2):
        acc = None
        for br, n, lg, y_ref in ((b0, 0, l0, y0), (b1, 1, l1, y1), (b2, 2, l2, y2)):
            y = _dot(br[...].astype(BF16), wb_ref[n])
            y_ref[...] = y.astype(BF16)
            term = jax.nn.sigmoid(lg[...].astype(F32)) * y
            acc = term if acc is None else acc + term
        m_ref[...] = acc.astype(BF16)

    def lspec(n):
        return pl.BlockSpec((tm, tn), lambda i, j: (i, (3 * D + n * D) // tn + j))

    tile = pl.BlockSpec((tm, tn), lambda i, j: (i, j))
    bspec = pl.BlockSpec((tm, BW), lambda i, j: (i, 0))
    return pl.pallas_call(
        body, name="merge_fwd", grid=(t // tm, nj),
        in_specs=[bspec, bspec, bspec, pl.BlockSpec((NB, BW, tn), lambda i, j: (0, 0, j)),
                  lspec(0), lspec(1), lspec(2)],
        out_specs=[tile] * 4,
        out_shape=[jax.ShapeDtypeStruct((t, D), BF16)] * 4,
        compiler_params=_cp(("parallel", "parallel")),
    )(brs[0], brs[1], brs[2], wb, proj, proj, proj)


def _merge_bwd(dm, ys, proj):
    t = proj.shape[0]
    tm = _row_tile(t, 512)

    def body(dm_ref, y0, y1, y2, lg_ref, dp_ref, dy0, dy1, dy2):
        dmv = dm_ref[...].astype(F32)
        for n, (y_ref, dy_ref) in enumerate(((y0, dy0), (y1, dy1), (y2, dy2))):
            cols = slice(n * D, (n + 1) * D)
            g = jax.nn.sigmoid(lg_ref[:, cols].astype(F32))
            dp_ref[:, cols] = (dmv * y_ref[...].astype(F32) * g * (1.0 - g)).astype(BF16)
            dy_ref[...] = (dmv * g).astype(BF16)

    tile = pl.BlockSpec((tm, D), lambda i: (i, 0))
    gates = pl.BlockSpec((tm, NB * D), lambda i: (i, 1))
    return pl.pallas_call(
        body, name="merge_bwd", grid=(t // tm,),
        in_specs=[tile] * 4 + [gates],
        out_specs=[gates] + [tile] * 3,
        out_shape=[jax.ShapeDtypeStruct((t, IN_COLS), BF16)] + [jax.ShapeDtypeStruct((t, D), BF16)] * 3,
        compiler_params=_cp(("parallel",)),
    )(dm, ys[0], ys[1], ys[2], proj)


def _adamw_rows(rows):
    if rows <= 512:
        return rows
    return next(tr for tr in (512, 384, 352, 256, 128, 64, 32, 16, 8) if rows % tr == 0)


def _adamw_math(npart, p_ref, w_ref, m_ref, v_ref, g_ref, d_ref, mo_ref, vo_ref):
    c1 = 1.0 - ADAM_B1 ** ADAM_STEP
    c2 = 1.0 - ADAM_B2 ** ADAM_STEP
    g = p_ref[0].astype(F32)
    for p in range(1, npart):
        g = g + p_ref[p].astype(F32)
    mn = ADAM_B1 * m_ref[...] + (1.0 - ADAM_B1) * g
    vn = ADAM_B2 * v_ref[...] + (1.0 - ADAM_B2) * (g * g)
    m_hat = mn / c1
    v_hat = vn / c2
    g_ref[...] = g
    d_ref[...] = -ADAM_LR * (m_hat / (jnp.sqrt(v_hat) + ADAM_EPS) + ADAM_WD * w_ref[...])
    mo_ref[...] = mn
    vo_ref[...] = vn


def _adamw_layer(name, parts, w, m, v, layer, bufs, padded=False):
    nl, cols = w.shape[0], w.shape[-1]
    rows = int(math.prod(w.shape[1:-1]))
    npart = parts.shape[0]
    tr = _adamw_rows(rows)
    if padded:
        assert parts.shape[1] == rows // tr and parts.shape[2] >= tr
        parts_spec = pl.BlockSpec((npart, None, tr, cols), lambda i: (0, i, 0, 0))
    else:
        parts = parts.reshape(npart, rows, cols)
        parts_spec = pl.BlockSpec((npart, tr, cols), lambda i: (0, i, 0))
    if bufs is None:
        bufs = [lax.empty((nl, rows, cols), F32) for _ in range(4)]

    def body(p_ref, w_ref, m_ref, v_ref, b0, b1, b2, b3, g_ref, d_ref, mo_ref, vo_ref):
        _adamw_math(npart, p_ref, w_ref, m_ref, v_ref, g_ref, d_ref, mo_ref, vo_ref)

    slab = pl.BlockSpec((None, tr, cols), lambda i: (layer, i, 0))
    sds = jax.ShapeDtypeStruct((nl, rows, cols), F32)
    return pl.pallas_call(
        body, name=name, grid=(rows // tr,),
        in_specs=[parts_spec, slab, slab, slab] + [_HBM] * 4,
        out_specs=[slab] * 4, out_shape=[sds] * 4,
        input_output_aliases={4: 0, 5: 1, 6: 2, 7: 3},
        compiler_params=_cp(("parallel",)),
    )(parts, w.reshape(nl, rows, cols), m.reshape(nl, rows, cols), v.reshape(nl, rows, cols), *bufs)


def _adamw_reduce(name, parts, w, m, v):
    shape = w.shape
    cols = shape[-1]
    rows = int(math.prod(shape[:-1])) if len(shape) > 1 else 1
    npart = parts.shape[0]
    tr = _adamw_rows(rows)

    def body(p_ref, w_ref, m_ref, v_ref, g_ref, d_ref, mo_ref, vo_ref):
        _adamw_math(npart, p_ref, w_ref, m_ref, v_ref, g_ref, d_ref, mo_ref, vo_ref)

    tile = pl.BlockSpec((tr, cols), lambda i: (i, 0))
    sds = jax.ShapeDtypeStruct((rows, cols), F32)
    outs = pl.pallas_call(
        body, name=name, grid=(rows // tr,),
        in_specs=[pl.BlockSpec((npart, tr, cols), lambda i: (0, i, 0)), tile, tile, tile],
        out_specs=[tile] * 4, out_shape=[sds] * 4,
        compiler_params=_cp(("parallel",)),
    )(parts.reshape(npart, rows, cols), w.reshape(rows, cols), m.reshape(rows, cols), v.reshape(rows, cols))
    return tuple(o.reshape(shape) for o in outs)


def _pad_ffn_in(w):
    lead = w.shape[:-1]
    w = w.reshape(lead + (2, FF_HALF))
    w = jnp.pad(w, [(0, 0)] * len(lead) + [(0, 0), (0, FF_HALF_PAD - FF_HALF)])
    return w.reshape(lead + (FF_IN_PAD,))


def kernel(x, c, rms_g1, rms_g2, w_ada, b_ada, w_in, gm_ln_g, gm_ln_b, gm_w_spatial, gm_b_spatial, pool_w, pool_scale, w_branch, w_out, w_ffn_in, w_ffn_out, final_g, loss_target, m_rms_g1, m_rms_g2, m_w_ada, m_b_ada, m_w_in, m_gm_ln_g, m_gm_ln_b, m_gm_w_spatial, m_gm_b_spatial, m_pool_w, m_pool_scale, m_w_branch, m_w_out, m_w_ffn_in, m_w_ffn_out, m_final_g, v_rms_g1, v_rms_g2, v_w_ada, v_b_ada, v_w_in, v_gm_ln_g, v_gm_ln_b, v_gm_w_spatial, v_gm_b_spatial, v_pool_w, v_pool_scale, v_w_branch, v_w_out, v_w_ffn_in, v_w_ffn_out, v_final_g):
    nb, seq, _ = x.shape
    nl = w_in.shape[0]
    t = nb * seq
    ntot = NDEV * nb
    me = _my_index()
    assert x.shape[2] == D and w_in.shape[1:] == (D, 768) and w_ffn_in.shape[1:] == (D, FF_IN_SHARD)
    assert seq % CH == 0

    w_ffn_in_p = _pad_ffn_in(w_ffn_in).astype(BF16)
    w_ffn_out_p = jnp.pad(w_ffn_out, ((0, 0), (0, FF_HALF_PAD - FF_HALF), (0, 0))).astype(BF16)
    w_in_b = w_in.astype(BF16)
    w_branch_b = w_branch.astype(BF16)
    w_out_b = w_out.astype(BF16)
    (g_in_next,) = _exchange([w_in_b[0]], "gather_w_in0", False)

    (c_all,) = _exchange([c], "gather_c", False)
    c_all = c_all.reshape(ntot, D)
    b_blk = lax.dynamic_slice_in_dim(b_ada, me * 768, 768, axis=1).reshape(nl, 1, 768)
    mod_blk = _ada_fwd(c_all, w_ada, b_blk)
    (mod_all,) = _exchange([mod_blk], "gather_mod", False)
    mod_all = jnp.transpose(mod_all, (1, 2, 0, 3)).reshape(nl, ntot, NMOD * D)
    mod = lax.dynamic_slice_in_dim(mod_all, me * nb, nb, axis=1).reshape(nl, nb, NMOD, 1, D)

    saved = []
    gathered = []
    xc = x
    for l in range(nl):
        sh1, sc1, gt1, sh2, sc2, gt2 = [mod[l, :, i] for i in range(NMOD)]
        h = _norm_mod_fwd(xc, rms_g1[l].reshape(1, D), sc1, sh1).reshape(t, D)
        proj, (g_ffn_in_w,) = _mm_colblocked("proj_fwd", h, g_in_next, BF16, [w_ffn_in_p[l]])
        proj3 = proj.reshape(nb, seq, IN_COLS)
        br_gm = _gmlp_fwd(proj, gm_ln_g[l].reshape(1, BW), gm_ln_b[l].reshape(1, BW),
                          gm_w_spatial[l], gm_b_spatial[l].T)
        sb_o, got = _sb_fwd(proj3, [w_branch_b[l], w_out_b[l], w_ffn_out_p[l]]
                            + ([w_in_b[l + 1]] if l + 1 < nl else []))
        gw = dict(w_in=g_in_next,
                  w_branch=jnp.transpose(got[0], (1, 2, 0, 3)).reshape(NB, BW, D),
                  w_out=got[1].reshape(D, D),
                  w_ffn_in=g_ffn_in_w,
                  w_ffn_out=got[2].reshape(FFP, D))
        gathered.append(gw)
        if l + 1 < nl:
            g_in_next = got[3]
        br_pool = _pool_fwd(proj3, pool_w[l], pool_scale[l].reshape(1, BW))
        brs = [br_gm, sb_o.reshape(t, BW), br_pool.reshape(t, BW)]
        merged, y0, y1, y2 = _merge_fwd(brs, gw["w_branch"], proj)
        x_mid, mo = _mm_residual("out_fwd", merged, gw["w_out"], xc.reshape(t, D), gt1, seq)
        x_mid = x_mid.reshape(nb, seq, D)
        h2 = _norm_mod_fwd(x_mid, rms_g2[l].reshape(1, D), sc2, sh2).reshape(t, D)
        fg, fu, act, _ = _ffn_in_fwd(h2, gw["w_ffn_in"])
        x_out, fo = _mm_residual("ffn_out_fwd", act, gw["w_ffn_out"], x_mid.reshape(t, D), gt2, seq)
        saved.append(dict(x_in=xc, h=h, proj=proj, brs=brs, sb_o=sb_o, ys=(y0, y1, y2), merged=merged,
                          mo=mo, x_mid=x_mid, h2=h2, fg=fg, fu=fu, act=act, fo=fo))
        xc = x_out.reshape(nb, seq, D)

    dx, loss_part, dfinal_part, dfo, dgt2 = _loss_head(xc, loss_target, final_g.reshape(1, D),
                                                       saved[-1]["fo"].reshape(nb, seq, D), mod[nl - 1, :, 5])
    loss = lax.psum(jnp.sum(loss_part[:, 0, 0]), ("x", "y", "c"))

    big_names = ("w_in", "w_branch", "w_out", "w_ffn_in", "w_ffn_out")
    bufs = {name: None for name in big_names}
    w_ffn_in_t, m_w_ffn_in_t, v_w_ffn_in_t = [jnp.swapaxes(a, 1, 2) for a in (w_ffn_in, m_w_ffn_in, v_w_ffn_in)]
    small_parts = {k: [None] * nl for k in ("rms_g1", "rms_g2", "gm_ln_g", "gm_ln_b", "gm_w_spatial",
                                            "gm_b_spatial", "pool_w", "pool_scale")}
    dmod = [None] * nl
    names = list(small_parts)

    def stack_small(lo, hi):
        return [jnp.stack(small_parts[k][lo:hi]).astype(BF16 if k in ("gm_w_spatial", "pool_w") else F32)
                for k in names]

    for l in reversed(range(nl)):
        gw = gathered[l]
        sv = saved[l]
        sh1, sc1, gt1, sh2, sc2, gt2 = [mod[l, :, i] for i in range(NMOD)]
        dfo = dfo.reshape(t, D)
        if l == 0 and nl > 1:
            g_ffn_out, early_small = _mm_tn("ffn_out_wgrad", sv["act"], dfo, gather=stack_small(1, nl))
        else:
            g_ffn_out = _mm_tn("ffn_out_wgrad", sv["act"], dfo)
        dfg, dfu = _ffn_out_dgrad(dfo, gw["w_ffn_out"], sv["fg"], sv["fu"])
        dh2 = _ffn_in_dgrad(dfg, dfu, gw["w_ffn_in"])
        g_ffn_in = _ffn_in_wgrad(sv["h2"], dfg, dfu)
        dx_mid, dsh2, dsc2, dg2, dmo, dgt1 = _norm_mod_bwd(
            sv["x_mid"], dh2.reshape(nb, seq, D), dx, rms_g2[l].reshape(1, D), sc2,
            gate=(sv["mo"].reshape(nb, seq, D), gt1))
        dmo = dmo.reshape(t, D)
        dmerged = _mm_nt("out_dgrad", dmo, gw["w_out"], BF16)
        g_out = _mm_tn("out_wgrad", sv["merged"], dmo)
        dproj, *dys = _merge_bwd(dmerged, sv["ys"], sv["proj"])
        dbrs, g_br = [], []
        for n in range(NB):
            dbrs.append(_mm_nt("branch_dgrad", dys[n], gw["w_branch"], BF16, w_lead=n))
            g_br.append(_mm_tn("branch_wgrad", sv["brs"][n], dys[n]))
        proj3 = sv["proj"].reshape(nb, seq, IN_COLS)
        dproj, g_ws, g_bs, g_lg, g_lb = _gmlp_bwd(sv["proj"], dbrs[0], gm_ln_g[l].reshape(1, BW),
                                                  gm_ln_b[l].reshape(1, BW), gm_w_spatial[l], gm_b_spatial[l].T,
                                                  dproj)
        g_br_dev = jnp.transpose(jnp.stack(g_br).reshape(NB, BW, NDEV, D // NDEV), (2, 0, 1, 3))
        carried = [g_br_dev, g_out.reshape(NDEV, D // NDEV, D), g_ffn_in, g_ffn_out.reshape(NDEV, FF_HALF_PAD, D)]
        d_sb, recv = _sb_bwd(proj3, dbrs[1].reshape(nb, seq, BW), sv["sb_o"], carried)
        bufs["w_branch"] = _adamw_layer("adamw_w_branch", recv[0], w_branch, m_w_branch, v_w_branch, l,
                                        bufs["w_branch"])
        bufs["w_out"] = _adamw_layer("adamw_w_out", recv[1], w_out, m_w_out, v_w_out, l, bufs["w_out"])
        bufs["w_ffn_in"] = _adamw_layer("adamw_w_ffn_in", recv[2].reshape(NDEV, 2, FF_HALF_PAD, D), w_ffn_in_t,
                                        m_w_ffn_in_t, v_w_ffn_in_t, l, bufs["w_ffn_in"], padded=True)
        bufs["w_ffn_out"] = _adamw_layer("adamw_w_ffn_out", recv[3].reshape(NDEV, 1, FF_HALF_PAD, D), w_ffn_out,
                                         m_w_ffn_out, v_w_ffn_out, l, bufs["w_ffn_out"], padded=True)
        dproj3, g_pw, g_ps = _pool_bwd(proj3, dbrs[2].reshape(nb, seq, BW), pool_w[l], pool_scale[l].reshape(1, BW),
                                       dproj.reshape(nb, seq, IN_COLS))
        dproj = dproj3.reshape(t, IN_COLS)
        for i, piece in enumerate(d_sb):
            dproj = lax.dynamic_update_slice(dproj, piece.reshape(t, BW), (0, 2 * BW + i * BW))
        g_in = _mm_colblocked_tn("proj_wgrad", sv["h"], dproj)
        dh, (r_in,) = _mm_colblocked_nt("proj_dgrad", dproj, gw["w_in"], F32, [g_in])
        bufs["w_in"] = _adamw_layer("adamw_w_in", r_in, w_in, m_w_in, v_w_in, l, bufs["w_in"])
        dmod_tail = [dgt1, dsh2, dsc2, dgt2]
        if l > 0:
            dx, dsh1, dsc1, dg1, dfo, dgt2 = _norm_mod_bwd(
                sv["x_in"], dh.reshape(nb, seq, D), dx_mid, rms_g1[l].reshape(1, D), sc1,
                gate=(saved[l - 1]["fo"].reshape(nb, seq, D), mod[l - 1, :, 5]))
        else:
            dx, dsh1, dsc1, dg1 = _norm_mod_bwd(sv["x_in"], dh.reshape(nb, seq, D), dx_mid,
                                                rms_g1[l].reshape(1, D), sc1)

        dmod[l] = jnp.concatenate([dsh1, dsc1] + dmod_tail, axis=-1)
        small_parts["rms_g1"][l] = jnp.sum(dg1, axis=0)
        small_parts["rms_g2"][l] = jnp.sum(dg2, axis=0)
        small_parts["gm_ln_g"][l] = g_lg
        small_parts["gm_ln_b"][l] = g_lb
        small_parts["gm_w_spatial"][l] = g_ws
        small_parts["gm_b_spatial"][l] = g_bs[:, :, 0]
        small_parts["pool_w"][l] = g_pw
        small_parts["pool_scale"][l] = g_ps

    dmod_mine = jnp.stack(dmod).reshape(nl, nb, NMOD * D)
    gathered_small = _exchange(stack_small(0, 1) + [dfinal_part, dmod_mine], "gather_small", False)
    if nl > 1:
        gathered_small = [jnp.concatenate([late, early], axis=1)
                          for late, early in zip(gathered_small, early_small)] + list(gathered_small[-2:])
    dmod_all = jnp.transpose(gathered_small[-1], (1, 0, 2, 3)).reshape(nl, ntot, NMOD * D)
    dfinal_all = gathered_small[-2].reshape(ntot, D)

    results = {}
    weights = dict(rms_g1=(rms_g1, m_rms_g1, v_rms_g1), rms_g2=(rms_g2, m_rms_g2, v_rms_g2),
                   gm_ln_g=(gm_ln_g, m_gm_ln_g, v_gm_ln_g), gm_ln_b=(gm_ln_b, m_gm_ln_b, v_gm_ln_b),
                   gm_w_spatial=(gm_w_spatial, m_gm_w_spatial, v_gm_w_spatial),
                   gm_b_spatial=(gm_b_spatial, m_gm_b_spatial, v_gm_b_spatial),
                   pool_w=(pool_w, m_pool_w, v_pool_w), pool_scale=(pool_scale, m_pool_scale, v_pool_scale))
    for k, parts in zip(names, gathered_small[:len(names)]):
        w, m, v = weights[k]
        results[k] = _adamw_reduce("adamw_" + k, parts.reshape((NDEV,) + w.shape), w, m, v)
    results["final_g"] = _adamw_reduce("adamw_final_g", dfinal_all, final_g, m_final_g, v_final_g)
    results["b_ada"] = _adamw_reduce("adamw_b_ada", jnp.transpose(dmod_all, (1, 0, 2)), b_ada, m_b_ada, v_b_ada)
    dmod_blk = lax.dynamic_slice_in_dim(dmod_all, me * 768, 768, axis=2)
    g_w_ada = _ada_bwd(c_all, dmod_blk)
    results["w_ada"] = _adamw_reduce("adamw_w_ada", g_w_ada[None], w_ada, m_w_ada, v_w_ada)
    stacked_w = dict(w_in=w_in, w_branch=w_branch, w_out=w_out, w_ffn_in=w_ffn_in_t, w_ffn_out=w_ffn_out)
    for name in big_names:
        results[name] = tuple(b.reshape(stacked_w[name].shape) for b in bufs[name])
    results["w_ffn_in"] = tuple(jnp.swapaxes(b, 1, 2) for b in results["w_ffn_in"])

    order = ["rms_g1", "rms_g2", "w_ada", "b_ada", "w_in", "gm_ln_g", "gm_ln_b", "gm_w_spatial", "gm_b_spatial",
             "pool_w", "pool_scale", "w_branch", "w_out", "w_ffn_in", "w_ffn_out", "final_g"]
    out = [loss, dx]
    for i in range(4):
        out.extend(results[k][i] for k in order)
    return tuple(out)
```

```python
import functools
import math

import jax
import jax.numpy as jnp
from jax import lax
from jax.experimental import pallas as pl
from jax.experimental.pallas import tpu as pltpu

F32 = jnp.float32
BF16 = jnp.bfloat16
MESH = pl.DeviceIdType.MESH

D = 1024
BW = 512
NB = 3
CH = 128
NG = 4
HD = 64
POOL_WINDOWS = (2, 4, 8, 16)
DFF = 2816
NMOD = 6
EPS = 1e-6
IN_COLS = 6 * D
NDEV = 8
FF_IN_SHARD = 2 * DFF // NDEV
FF_HALF = FF_IN_SHARD // 2
FF_HALF_PAD = 384
FF_IN_PAD = 2 * FF_HALF_PAD
FFP = NDEV // 2 * FF_IN_PAD

ADAM_LR = 0.001
ADAM_B1 = 0.9
ADAM_B2 = 0.999
ADAM_EPS = 1e-08
ADAM_WD = 0.01
ADAM_STEP = 10

VMEM_LIMIT = 48 * 1024 * 1024
BIG_ROWS = 2048
NN = (((1,), (0,)), ((), ()))
NT = (((1,), (1,)), ((), ()))
TN = (((0,), (0,)), ((), ()))


def _cp(sem=None):
    return pltpu.CompilerParams(dimension_semantics=sem, vmem_limit_bytes=VMEM_LIMIT)


def _dot(a, b, dims=NN):
    return lax.dot_general(a, b, dims, preferred_element_type=F32)


def _my_index():
    return 4 * lax.axis_index("x") + 2 * lax.axis_index("y") + lax.axis_index("c")


def _peer(k):
    x, y, c = lax.axis_index("x"), lax.axis_index("y"), lax.axis_index("c")
    px = 1 - x if k & 4 else x
    py = 1 - y if k & 2 else y
    pc = 1 - c if k & 1 else c
    return (px, py, pc), 4 * px + 2 * py + pc


def _exchange(xs, name, all_to_all):
    n = len(xs)

    def body(*refs):
        _exchange_start(refs[:n], refs[n:2 * n], refs[2 * n:], all_to_all)
        _exchange_relay(refs[:n], refs[n:2 * n], refs[2 * n:], all_to_all)
        _exchange_finish(refs[:n], refs[n:2 * n], refs[2 * n:], all_to_all)

    return pl.pallas_call(
        body, name=name, out_shape=_exchange_out_shape(xs, all_to_all),
        in_specs=[_HBM] * n, out_specs=[_HBM] * n, scratch_shapes=_exchange_sems(n),
    )(*xs)


_HBM = pl.BlockSpec(memory_space=pl.ANY)


def _exchange_out_shape(xs, all_to_all):
    if all_to_all:
        return [jax.ShapeDtypeStruct(x.shape, x.dtype) for x in xs]
    return [jax.ShapeDtypeStruct((NDEV,) + x.shape, x.dtype) for x in xs]


def _exchange_sems(n):
    return [pltpu.SemaphoreType.DMA((n * 7,)), pltpu.SemaphoreType.DMA((n * 7,)), pltpu.SemaphoreType.DMA((n,))]


def _all_to_all_copies(ins, outs, sems):
    send_sems, recv_sems, local_sems = sems
    me = _my_index()
    local, sends, recvs = [], [], []
    for a in range(len(ins)):
        local.append(pltpu.make_async_copy(ins[a].at[me], outs[a].at[me], local_sems.at[a]))
    for k in range(1, NDEV):
        dev, idx = _peer(k)
        for a in range(len(ins)):
            sem = dict(send_sem=send_sems.at[a * 7 + k - 1], recv_sem=recv_sems.at[a * 7 + k - 1],
                       device_id=dev, device_id_type=MESH)
            sends.append(pltpu.make_async_remote_copy(src_ref=ins[a].at[idx], dst_ref=outs[a].at[me], **sem))
            recvs.append(pltpu.make_async_remote_copy(src_ref=ins[a].at[idx], dst_ref=outs[a].at[idx], **sem))
    return local, sends, recvs


def _gather_copies(ins, outs, sems):
    send_sems, recv_sems, local_sems = sems
    x, y, c = lax.axis_index("x"), lax.axis_index("y"), lax.axis_index("c")
    me, other = 4 * x + 2 * y + c, 4 * x + 2 * y + (1 - c)
    other_dev = (x, y, 1 - c)
    chips = [(1 - x, y), (x, 1 - y), (1 - x, 1 - y)]
    local, own, relay, from_other = [], [], [], []
    for a in range(len(ins)):
        def copy(k, src, block, dev, a=a):
            return pltpu.make_async_remote_copy(
                src_ref=src, dst_ref=outs[a].at[block], send_sem=send_sems.at[a * 7 + k],
                recv_sem=recv_sems.at[a * 7 + k], device_id=dev, device_id_type=MESH)

        local.append(pltpu.make_async_copy(ins[a], outs[a].at[me], local_sems.at[a]))
        own.append(copy(0, ins[a], me, other_dev))
        from_other.append(copy(0, ins[a], other, other_dev))
        for j, (px, py) in enumerate(chips):
            far = 4 * px + 2 * py + c
            own.append(copy(1 + j, ins[a], me, (px, py, c)))
            relay.append((copy(1 + j, ins[a], far, (px, py, c)), copy(4 + j, outs[a].at[far], far, other_dev)))
            from_other.append(copy(4 + j, ins[a], 4 * px + 2 * py + (1 - c), other_dev))
    return local, own, relay, from_other


def _exchange_start(ins, outs, sems, all_to_all):
    local, sends = (_all_to_all_copies if all_to_all else _gather_copies)(ins, outs, sems)[:2]
    for cp in local + sends:
        cp.start()


def _exchange_relay(ins, outs, sems, all_to_all):
    if not all_to_all:
        for arrival, passing_on in _gather_copies(ins, outs, sems)[2]:
            arrival.wait_recv()
            passing_on.start()


def _exchange_finish(ins, outs, sems, all_to_all):
    if all_to_all:
        local, sends, recvs = _all_to_all_copies(ins, outs, sems)
    else:
        local, own, relay, recvs = _gather_copies(ins, outs, sems)
        sends = own + [passing_on for _, passing_on in relay]
    for cp in sends:
        cp.wait_send()
    for cp in recvs:
        cp.wait_recv()
    for cp in local:
        cp.wait()


def _host_exchange(body, n_in, n_out, grid, xs, all_to_all):
    n = len(xs)
    if n == 0:
        return body, [], [], [], []
    steps = math.prod(grid)
    half = steps * 3 // 4 if steps >= 4 else steps - 1

    def hosted(*refs):
        ins, ex_ins = refs[:n_in], refs[n_in:n_in + n]
        outs, ex_outs = refs[n_in + n:n_in + n + n_out], refs[n_in + n + n_out:n_in + 2 * n + n_out]
        scratch = refs[n_in + 2 * n + n_out:]
        own, sems = scratch[:len(scratch) - 3], scratch[len(scratch) - 3:]
        step = 0
        for a in range(len(grid)):
            step = step * grid[a] + pl.program_id(a)

        @pl.when(step == 0)
        def _():
            _exchange_start(ex_ins, ex_outs, sems, all_to_all)

        body(*ins, *outs, *own)

        @pl.when(step == half)
        def _():
            _exchange_relay(ex_ins, ex_outs, sems, all_to_all)

        @pl.when(step == steps - 1)
        def _():
            _exchange_finish(ex_ins, ex_outs, sems, all_to_all)

    return hosted, [_HBM] * n, [_HBM] * n, _exchange_out_shape(xs, all_to_all), _exchange_sems(n)


def _mm(name, a, b, grid, a_spec, b_spec, o_spec, out_sds, dims, acc_shape, carried=(), all_to_all=True):
    nk = grid[2]

    if nk == 1:
        def body(a_ref, b_ref, o_ref):
            o_ref[...] = _dot(a_ref[...].astype(BF16), b_ref[...].astype(BF16), dims).astype(o_ref.dtype)
        scratch = []
    else:
        def body(a_ref, b_ref, o_ref, acc_ref):
            k = pl.program_id(2)

            @pl.when(k == 0)
            def _():
                acc_ref[...] = jnp.zeros_like(acc_ref)

            acc_ref[...] += _dot(a_ref[...].astype(BF16), b_ref[...].astype(BF16), dims)

            @pl.when(k == nk - 1)
            def _():
                o_ref[...] = acc_ref[...].astype(o_ref.dtype)
        scratch = [pltpu.VMEM(acc_shape, F32)]

    body, ex_in, ex_out, ex_shape, ex_sems = _host_exchange(body, 2, 1, grid, carried, all_to_all)
    outs = pl.pallas_call(
        body, name=name, grid=grid, in_specs=[a_spec, b_spec] + ex_in, out_specs=[o_spec] + ex_out,
        out_shape=[out_sds] + ex_shape,
        scratch_shapes=scratch + ex_sems,
        compiler_params=_cp(("arbitrary",) * 3 if carried else ("parallel", "parallel", "arbitrary")),
    )(a, b, *carried)
    return (outs[0], outs[1:]) if carried else outs[0]


def _row_tile(t, want):
    tm = min(t, want)
    assert t % tm == 0
    return tm


def _mm_colblocked(name, a, wg, out_dtype, gather=()):
    t = a.shape[0]
    tm = _row_tile(t, BIG_ROWS)
    return _mm(name, a, wg, (t // tm, NDEV, 1),
               pl.BlockSpec((tm, D), lambda i, j, k: (i, 0)),
               pl.BlockSpec((None, D, 768), lambda i, j, k: (j, 0, 0)),
               pl.BlockSpec((tm, 768), lambda i, j, k: (i, j)),
               jax.ShapeDtypeStruct((t, NDEV * 768), out_dtype), NN, (tm, 768), gather, False)


def _mm_colblocked_nt(name, g, wg, out_dtype, scatter=()):
    t = g.shape[0]
    tm = _row_tile(t, BIG_ROWS)
    return _mm(name, g, wg, (t // tm, 1, NDEV),
               pl.BlockSpec((tm, 768), lambda i, j, k: (i, k)),
               pl.BlockSpec((None, D, 768), lambda i, j, k: (k, 0, 0)),
               pl.BlockSpec((tm, D), lambda i, j, k: (i, 0)),
               jax.ShapeDtypeStruct((t, D), out_dtype), NT, (tm, D), scatter)


_HALF = NDEV // 2


def _ffn_in_fwd(h2, wg, gather=()):
    t = h2.shape[0]
    tm = _row_tile(t, 1024)

    def body(a_ref, wg_ref, wu_ref, g_ref, u_ref, act_ref):
        a = a_ref[...]
        g = _dot(a, wg_ref[...])
        u = _dot(a, wu_ref[...])
        g_ref[...] = g.astype(BF16)
        u_ref[...] = u.astype(BF16)
        act_ref[...] = (g * jax.nn.sigmoid(g) * u).astype(BF16)

    tile = pl.BlockSpec((tm, 768), lambda i, j: (i, j))
    grid = (t // tm, _HALF)
    body, ex_in, ex_out, ex_shape, ex_sems = _host_exchange(body, 3, 3, grid, gather, False)
    outs = pl.pallas_call(
        body, name="ffn_in_fwd", grid=grid,
        in_specs=[pl.BlockSpec((tm, D), lambda i, j: (i, 0)),
                  pl.BlockSpec((None, D, 768), lambda i, j: (j, 0, 0)),
                  pl.BlockSpec((None, D, 768), lambda i, j: (j + _HALF, 0, 0))] + ex_in,
        out_specs=[tile] * 3 + ex_out, out_shape=[jax.ShapeDtypeStruct((t, FFP), BF16)] * 3 + ex_shape,
        scratch_shapes=ex_sems,
        compiler_params=_cp(("arbitrary", "arbitrary") if gather else ("parallel", "parallel")),
    )(h2, wg, wg, *gather)
    return outs[0], outs[1], outs[2], outs[3:]


def _ffn_out_dgrad(dfo, w, fg, fu):
    t = dfo.shape[0]
    tm = _row_tile(t, 1024)

    def body(a_ref, w_ref, g_ref, u_ref, dg_ref, du_ref):
        d = _dot(a_ref[...], w_ref[...], NT)
        g = g_ref[...].astype(F32)
        s = jax.nn.sigmoid(g)
        gs = g * s
        dg_ref[...] = (d * u_ref[...].astype(F32) * (s + gs * (1.0 - s))).astype(BF16)
        du_ref[...] = (d * gs).astype(BF16)

    tile = pl.BlockSpec((tm, 768), lambda i, j: (i, j))
    return pl.pallas_call(
        body, name="ffn_out_dgrad", grid=(t // tm, _HALF),
        in_specs=[pl.BlockSpec((tm, D), lambda i, j: (i, 0)),
                  pl.BlockSpec((768, D), lambda i, j: (j, 0)), tile, tile],
        out_specs=[tile] * 2, out_shape=[jax.ShapeDtypeStruct((t, FFP), BF16)] * 2,
        compiler_params=_cp(("parallel", "parallel")),
    )(dfo, w, fg, fu)


def _ffn_in_dgrad(dg, du, wg):
    t = dg.shape[0]
    tm = _row_tile(t, BIG_ROWS)

    def body(g_ref, u_ref, w_ref, o_ref, acc_ref):
        k = pl.program_id(1)

        @pl.when(k == 0)
        def _():
            acc_ref[...] = jnp.zeros_like(acc_ref)

        @pl.when(k < _HALF)
        def _():
            acc_ref[...] += _dot(g_ref[...], w_ref[...], NT)

        @pl.when(k >= _HALF)
        def _():
            acc_ref[...] += _dot(u_ref[...], w_ref[...], NT)

        @pl.when(k == NDEV - 1)
        def _():
            o_ref[...] = acc_ref[...]

    return pl.pallas_call(
        body, name="ffn_in_dgrad", grid=(t // tm, NDEV),
        in_specs=[pl.BlockSpec((tm, 768), lambda i, k: (i, jnp.minimum(k, _HALF - 1))),
                  pl.BlockSpec((tm, 768), lambda i, k: (i, jnp.maximum(k - _HALF, 0))),
                  pl.BlockSpec((None, D, 768), lambda i, k: (k, 0, 0))],
        out_specs=pl.BlockSpec((tm, D), lambda i, k: (i, 0)),
        out_shape=jax.ShapeDtypeStruct((t, D), F32),
        scratch_shapes=[pltpu.VMEM((tm, D), F32)],
        compiler_params=_cp(("parallel", "arbitrary")),
    )(dg, du, wg)


def _ffn_in_wgrad(h2, dg, du):
    t = h2.shape[0]
    tk = _row_tile(t, BIG_ROWS)
    nk = t // tk

    def body(a_ref, g_ref, u_ref, o_ref, acc_ref):
        j, k = pl.program_id(0), pl.program_id(1)

        @pl.when(k == 0)
        def _():
            acc_ref[...] = jnp.zeros_like(acc_ref)

        @pl.when(j < _HALF)
        def _():
            acc_ref[...] += _dot(g_ref[...], a_ref[...], TN)

        @pl.when(j >= _HALF)
        def _():
            acc_ref[...] += _dot(u_ref[...], a_ref[...], TN)

        @pl.when(k == nk - 1)
        def _():
            o_ref[...] = acc_ref[...].astype(BF16)

    return pl.pallas_call(
        body, name="ffn_in_wgrad", grid=(NDEV, nk),
        in_specs=[pl.BlockSpec((tk, D), lambda j, k: (k, 0)),
                  pl.BlockSpec((tk, 768), lambda j, k: (jnp.where(j < _HALF, k, 0), jnp.minimum(j, _HALF - 1))),
                  pl.BlockSpec((tk, 768), lambda j, k: (jnp.where(j < _HALF, 0, k), jnp.maximum(j - _HALF, 0)))],
        out_specs=pl.BlockSpec((None, 768, D), lambda j, k: (j, 0, 0)),
        out_shape=jax.ShapeDtypeStruct((NDEV, 768, D), BF16),
        scratch_shapes=[pltpu.VMEM((768, D), F32)],
        compiler_params=_cp(("parallel", "arbitrary")),
    )(h2, dg, du)


def _mm_colblocked_tn(name, a, g):
    t = a.shape[0]
    tk = _row_tile(t, BIG_ROWS)
    return _mm(name, a, g, (1, NDEV, t // tk),
               pl.BlockSpec((tk, D), lambda i, j, k: (k, 0)),
               pl.BlockSpec((tk, 768), lambda i, j, k: (k, j)),
               pl.BlockSpec((None, D, 768), lambda i, j, k: (j, 0, 0)),
               jax.ShapeDtypeStruct((NDEV, D, 768), BF16), TN, (D, 768))


def _mm_nt(name, a, w, out_dtype, a_col=0, w_lead=None):
    t = a.shape[0]
    if w_lead is None:
        kdim, n = w.shape
        b_spec = pl.BlockSpec((min(kdim, 1024), n), lambda i, j, k: (j, 0))
    else:
        _, kdim, n = w.shape
        b_spec = pl.BlockSpec((None, min(kdim, 1024), n), lambda i, j, k: (w_lead, j, 0))
    tn = min(kdim, 1024)
    tm = _row_tile(t, BIG_ROWS)
    return _mm(name, a, w, (t // tm, kdim // tn, 1),
               pl.BlockSpec((tm, n), lambda i, j, k: (i, a_col)),
               b_spec,
               pl.BlockSpec((tm, tn), lambda i, j, k: (i, j)),
               jax.ShapeDtypeStruct((t, kdim), out_dtype), NT, (tm, tn))


def _mm_tn(name, a, g, out_dtype=BF16, gather=()):
    t, kdim = a.shape
    n = g.shape[1]
    tk = _row_tile(t, BIG_ROWS)
    tm = min(kdim, 1024)
    tn = min(n, 1024)
    return _mm(name, a, g, (kdim // tm, n // tn, t // tk),
               pl.BlockSpec((tk, tm), lambda i, j, k: (k, i)),
               pl.BlockSpec((tk, tn), lambda i, j, k: (k, j)),
               pl.BlockSpec((tm, tn), lambda i, j, k: (i, j)),
               jax.ShapeDtypeStruct((kdim, n), out_dtype), TN, (tm, tn), gather, False)


def _mm_residual(name, a, w, x, gt, seq):
    t, kdim = a.shape
    tm = _row_tile(seq, 1024)
    tn = D if kdim <= 1024 else D // 2
    per = seq // tm

    def body(a_ref, w_ref, x_ref, gt_ref, xo_ref, y_ref):
        y = _dot(a_ref[...], w_ref[...])
        xo_ref[...] = x_ref[...] + gt_ref[0] * y
        y_ref[...] = y.astype(BF16)

    tile = pl.BlockSpec((tm, tn), lambda i, j: (i, j))
    return pl.pallas_call(
        body, name=name, grid=(t // tm, D // tn),
        in_specs=[pl.BlockSpec((tm, kdim), lambda i, j: (i, 0)),
                  pl.BlockSpec((kdim, tn), lambda i, j: (0, j)),
                  tile,
                  pl.BlockSpec((1, 1, tn), lambda i, j: (i // per, 0, j))],
        out_specs=[tile, tile],
        out_shape=[jax.ShapeDtypeStruct((t, D), F32), jax.ShapeDtypeStruct((t, D), BF16)],
        compiler_params=_cp(("parallel", "parallel")),
    )(a, w, x, gt)


def _ada_fwd(c_all, w_ada, b_blk):
    nl = w_ada.shape[0]
    nb = c_all.shape[0]

    def body(c_ref, w_ref, b_ref, o_ref):
        c = c_ref[...]
        ca = (c * jax.nn.sigmoid(c)).astype(BF16)
        o_ref[...] = _dot(ca, w_ref[...].astype(BF16)) + b_ref[...]

    return pl.pallas_call(
        body, name="ada_fwd", grid=(nl,),
        in_specs=[pl.BlockSpec((nb, D), lambda l: (0, 0)),
                  pl.BlockSpec((None, D, 768), lambda l: (l, 0, 0)),
                  pl.BlockSpec((None, 1, 768), lambda l: (l, 0, 0))],
        out_specs=pl.BlockSpec((None, nb, 768), lambda l: (l, 0, 0)),
        out_shape=jax.ShapeDtypeStruct((nl, nb, 768), F32),
        compiler_params=_cp(("parallel",)),
    )(c_all, w_ada, b_blk)


def _ada_bwd(c_all, dmod_blk):
    nl = dmod_blk.shape[0]
    nb = c_all.shape[0]

    def body(c_ref, d_ref, o_ref):
        c = c_ref[...]
        ca = (c * jax.nn.sigmoid(c)).astype(BF16)
        o_ref[...] = _dot(ca, d_ref[...].astype(BF16), TN)

    return pl.pallas_call(
        body, name="ada_bwd", grid=(nl,),
        in_specs=[pl.BlockSpec((nb, D), lambda l: (0, 0)),
                  pl.BlockSpec((None, nb, 768), lambda l: (l, 0, 0))],
        out_specs=pl.BlockSpec((None, D, 768), lambda l: (l, 0, 0)),
        out_shape=jax.ShapeDtypeStruct((nl, D, 768), F32),
        compiler_params=_cp(("parallel",)),
    )(c_all, dmod_blk)


def _seq_tile(seq):
    return _row_tile(seq, 512)


def _norm_mod_fwd(x, g, sc, sh):
    nb, seq, _ = x.shape
    ts = _seq_tile(seq)

    def body(x_ref, g_ref, sc_ref, sh_ref, h_ref):
        xv = x_ref[0]
        r = lax.rsqrt(jnp.mean(xv * xv, axis=-1, keepdims=True) + EPS)
        h_ref[0] = ((xv * r) * g_ref[...] * (1.0 + sc_ref[0]) + sh_ref[0]).astype(BF16)

    return pl.pallas_call(
        body, name="norm_mod_fwd", grid=(nb, seq // ts),
        in_specs=[pl.BlockSpec((1, ts, D), lambda b, s: (b, s, 0)),
                  pl.BlockSpec((1, D), lambda b, s: (0, 0)),
                  pl.BlockSpec((1, 1, D), lambda b, s: (b, 0, 0)),
                  pl.BlockSpec((1, 1, D), lambda b, s: (b, 0, 0))],
        out_specs=pl.BlockSpec((1, ts, D), lambda b, s: (b, s, 0)),
        out_shape=jax.ShapeDtypeStruct((nb, seq, D), BF16),
        compiler_params=_cp(("parallel", "parallel")),
    )(x, g, sc, sh)


def _gate_bwd_tile(d, y_ref, gt_ref, dy_ref, dgt_ref):
    @pl.when(pl.program_id(1) == 0)
    def _():
        dgt_ref[...] = jnp.zeros_like(dgt_ref)

    dy_ref[0] = (gt_ref[0] * d).astype(BF16)
    dgt_ref[0] += jnp.sum(d * y_ref[0].astype(F32), axis=0, keepdims=True)


def _norm_mod_bwd(x, dh, dres, g, sc, gate=None):
    nb, seq, _ = x.shape
    ts = _seq_tile(seq)

    def body(x_ref, dh_ref, dres_ref, g_ref, sc_ref, *rest):
        if gate is None:
            dx_ref, dsh_ref, dsc_ref, dg_ref = rest
        else:
            y_ref, gt_ref, dx_ref, dsh_ref, dsc_ref, dg_ref, dy_ref, dgt_ref = rest

        @pl.when(pl.program_id(1) == 0)
        def _():
            dsh_ref[...] = jnp.zeros_like(dsh_ref)
            dsc_ref[...] = jnp.zeros_like(dsc_ref)
            dg_ref[...] = jnp.zeros_like(dg_ref)

        xv = x_ref[0]
        dh = dh_ref[0]
        gv = g_ref[...]
        onesc = 1.0 + sc_ref[0]
        r = lax.rsqrt(jnp.mean(xv * xv, axis=-1, keepdims=True) + EPS)
        xh = xv * r
        dsh_ref[0] += jnp.sum(dh, axis=0, keepdims=True)
        dsc_ref[0] += jnp.sum(dh * (xh * gv), axis=0, keepdims=True)
        dg_ref[0] += jnp.sum(dh * onesc * xh, axis=0, keepdims=True)
        dxh = dh * (gv * onesc)
        dx = r * (dxh - xh * jnp.mean(dxh * xh, axis=-1, keepdims=True))
        dx_total = dres_ref[0] + dx
        dx_ref[0] = dx_total
        if gate is not None:
            _gate_bwd_tile(dx_total, y_ref, gt_ref, dy_ref, dgt_ref)

    vec = jax.ShapeDtypeStruct((nb, 1, D), F32)
    vspec = pl.BlockSpec((1, 1, D), lambda b, s: (b, 0, 0))
    tile = pl.BlockSpec((1, ts, D), lambda b, s: (b, s, 0))
    gated = gate is not None
    return pl.pallas_call(
        body, name="norm_mod_bwd", grid=(nb, seq // ts),
        in_specs=[tile, tile, tile, pl.BlockSpec((1, D), lambda b, s: (0, 0)), vspec] + [tile, vspec] * gated,
        out_specs=[tile, vspec, vspec, vspec] + [tile, vspec] * gated,
        out_shape=[jax.ShapeDtypeStruct((nb, seq, D), F32), vec, vec, vec]
        + [jax.ShapeDtypeStruct((nb, seq, D), BF16), vec] * gated,
        compiler_params=_cp(("parallel", "arbitrary")),
    )(x, dh, dres, g, sc, *(gate or ()))


def _loss_head(x, tgt, g, y, gt):
    nb, seq, _ = x.shape
    ts = _seq_tile(seq)

    def body(x_ref, t_ref, g_ref, y_ref, gt_ref, dx_ref, loss_ref, dg_ref, dy_ref, dgt_ref):
        @pl.when(pl.program_id(1) == 0)
        def _():
            loss_ref[...] = jnp.zeros_like(loss_ref)
            dg_ref[...] = jnp.zeros_like(dg_ref)

        xv = x_ref[0]
        gv = g_ref[...]
        r = lax.rsqrt(jnp.mean(xv * xv, axis=-1, keepdims=True) + EPS)
        xh = xv * r
        err = xh * gv - t_ref[0]
        per_tok = jnp.mean(err * err, axis=-1, keepdims=True)
        loss_ref[0] += 0.5 * jnp.sum(per_tok, axis=0, keepdims=True)
        dy = err * (1.0 / D)
        dg_ref[0] += jnp.sum(dy * xh, axis=0, keepdims=True)
        dxh = dy * gv
        dx = r * (dxh - xh * jnp.mean(dxh * xh, axis=-1, keepdims=True))
        dx_ref[0] = dx
        _gate_bwd_tile(dx, y_ref, gt_ref, dy_ref, dgt_ref)

    tile = pl.BlockSpec((1, ts, D), lambda b, s: (b, s, 0))
    vspec = pl.BlockSpec((1, 1, D), lambda b, s: (b, 0, 0))
    vec = jax.ShapeDtypeStruct((nb, 1, D), F32)
    return pl.pallas_call(
        body, name="loss_head", grid=(nb, seq // ts),
        in_specs=[tile, tile, pl.BlockSpec((1, D), lambda b, s: (0, 0)), tile, vspec],
        out_specs=[tile, pl.BlockSpec((1, 1, 128), lambda b, s: (b, 0, 0)), vspec, tile, vspec],
        out_shape=[jax.ShapeDtypeStruct((nb, seq, D), F32), jax.ShapeDtypeStruct((nb, 1, 128), F32), vec,
                   jax.ShapeDtypeStruct((nb, seq, D), BF16), vec],
        compiler_params=_cp(("parallel", "arbitrary")),
    )(x, tgt, g, y, gt)


_GELU_C = math.sqrt(2.0 / math.pi)


def _gelu(x):
    return 0.5 * x * (1.0 + jnp.tanh(_GELU_C * (x + 0.044715 * (x * x * x))))


def _gelu_and_grad(x):
    t = jnp.tanh(_GELU_C * (x + 0.044715 * (x * x * x)))
    y = 0.5 * x * (1.0 + t)
    dy = 0.5 * (1.0 + t) + 0.5 * x * (1.0 - t * t) * (_GELU_C * (1.0 + 3.0 * 0.044715 * (x * x)))
    return y, dy


def _tril_mask():
    row = lax.broadcasted_iota(jnp.int32, (CH, CH), 0)
    col = lax.broadcasted_iota(jnp.int32, (CH, CH), 1)
    return row >= col


def _gmlp_fwd(proj, ln_g, ln_b, ws, bst):
    t = proj.shape[0]
    tm = _row_tile(t, 512)

    def body(u_ref, v_ref, lg_ref, lb_ref, ws_ref, bst_ref, o_ref):
        tril = _tril_mask()
        wm = [jnp.where(tril, ws_ref[g], 0.0).astype(BF16) for g in range(NG)]
        for ch in range(tm // CH):
            rows = slice(ch * CH, (ch + 1) * CH)
            u = _gelu(u_ref[rows, :].astype(F32))
            v = _gelu(v_ref[rows, :].astype(F32))
            mu = jnp.mean(v, axis=-1, keepdims=True)
            xc = v - mu
            rstd = lax.rsqrt(jnp.mean(xc * xc, axis=-1, keepdims=True) + EPS)
            vn = ((xc * rstd) * lg_ref[...] + lb_ref[...]).astype(BF16)
            for g in range(NG):
                cols = slice(g * CH, (g + 1) * CH)
                s = _dot(wm[g], vn[:, cols]) + bst_ref[:, g:g + 1]
                o_ref[rows, cols] = (u[:, cols] * s).astype(BF16)

    return pl.pallas_call(
        body, name="gmlp_fwd", grid=(t // tm,),
        in_specs=[pl.BlockSpec((tm, BW), lambda i: (i, 0)),
                  pl.BlockSpec((tm, BW), lambda i: (i, 1)),
                  pl.BlockSpec((1, BW), lambda i: (0, 0)),
                  pl.BlockSpec((1, BW), lambda i: (0, 0)),
                  pl.BlockSpec((NG, CH, CH), lambda i: (0, 0, 0)),
                  pl.BlockSpec((CH, NG), lambda i: (0, 0))],
        out_specs=pl.BlockSpec((tm, BW), lambda i: (i, 0)),
        out_shape=jax.ShapeDtypeStruct((t, BW), BF16),
        compiler_params=_cp(("parallel",)),
    )(proj, proj, ln_g, ln_b, ws, bst)


def _gmlp_bwd(proj, dout, ln_g, ln_b, ws, bst, dproj):
    t = proj.shape[0]
    tm = _row_tile(t, 512)

    def body(u_ref, v_ref, do_ref, lg_ref, lb_ref, ws_ref, bst_ref, buf_ref,
             dp_ref, gws_ref, gbs_ref, glg_ref, glb_ref):
        @pl.when(pl.program_id(0) == 0)
        def _():
            gws_ref[...] = jnp.zeros_like(gws_ref)
            gbs_ref[...] = jnp.zeros_like(gbs_ref)
            glg_ref[...] = jnp.zeros_like(glg_ref)
            glb_ref[...] = jnp.zeros_like(glb_ref)

        tril = _tril_mask()
        wm = [jnp.where(tril, ws_ref[g], 0.0).astype(BF16) for g in range(NG)]
        ones = jnp.ones((CH, CH), BF16)
        lg = lg_ref[...]
        for ch in range(tm // CH):
            rows = slice(ch * CH, (ch + 1) * CH)
            u, du_fac = _gelu_and_grad(u_ref[rows, :].astype(F32))
            v, dv_fac = _gelu_and_grad(v_ref[rows, :].astype(F32))
            do = do_ref[rows, :].astype(F32)
            mu = jnp.mean(v, axis=-1, keepdims=True)
            xc = v - mu
            rstd = lax.rsqrt(jnp.mean(xc * xc, axis=-1, keepdims=True) + EPS)
            xh = xc * rstd
            vn = (xh * lg + lb_ref[...]).astype(BF16)
            dvn_parts = []
            for g in range(NG):
                cols = slice(g * CH, (g + 1) * CH)
                s = _dot(wm[g], vn[:, cols]) + bst_ref[:, g:g + 1]
                dp_ref[rows, cols] = (do[:, cols] * s * du_fac[:, cols]).astype(BF16)
                ds = (do[:, cols] * u[:, cols]).astype(BF16)
                gws_ref[g] += jnp.where(tril, _dot(ds, vn[:, cols], NT), 0.0)
                gbs_ref[g] += _dot(ds, ones)
                dvn_parts.append(_dot(wm[g], ds, TN))
            dvn = jnp.concatenate(dvn_parts, axis=1)
            glb_ref[...] += jnp.sum(dvn, axis=0, keepdims=True)
            glg_ref[...] += jnp.sum(dvn * xh, axis=0, keepdims=True)
            dxh = dvn * lg
            dv = rstd * (dxh - jnp.mean(dxh, axis=-1, keepdims=True)
                         - xh * jnp.mean(dxh * xh, axis=-1, keepdims=True))
            dp_ref[rows, BW:2 * BW] = (dv * dv_fac).astype(BF16)

    small = pl.BlockSpec((NG, CH, CH), lambda i: (0, 0, 0))
    vec = pl.BlockSpec((1, BW), lambda i: (0, 0))
    return pl.pallas_call(
        body, name="gmlp_bwd", grid=(t // tm,),
        in_specs=[pl.BlockSpec((tm, BW), lambda i: (i, 0)),
                  pl.BlockSpec((tm, BW), lambda i: (i, 1)),
                  pl.BlockSpec((tm, BW), lambda i: (i, 0)),
                  vec, vec, small, pl.BlockSpec((CH, NG), lambda i: (0, 0)), _HBM],
        out_specs=[pl.BlockSpec((tm, 2 * BW), lambda i: (i, 0)), small, small, vec, vec],
        out_shape=[jax.ShapeDtypeStruct((t, IN_COLS), BF16),
                   jax.ShapeDtypeStruct((NG, CH, CH), F32), jax.ShapeDtypeStruct((NG, CH, CH), F32),
                   jax.ShapeDtypeStruct((1, BW), F32), jax.ShapeDtypeStruct((1, BW), F32)],
        input_output_aliases={7: 0},
        compiler_params=_cp(("arbitrary",)),
    )(proj, proj, dout, ln_g, ln_b, ws, bst, dproj)


def _pool_bands():
    row = lax.broadcasted_iota(jnp.int32, (CH, CH), 0)
    col = lax.broadcasted_iota(jnp.int32, (CH, CH), 1)
    cur, prev = [], []
    for w in POOL_WINDOWS:
        cur.append(jnp.where((row >= col) & (row - col < w), 1.0, 0.0).astype(BF16))
        prev.append(jnp.where(row + CH - col < w, 1.0, 0.0).astype(BF16))
    return cur, prev


def _pool_inv_count(r0, w):
    pos = r0 + lax.broadcasted_iota(jnp.int32, (CH, 1), 0)
    return 1.0 / jnp.minimum(pos + 1, w).astype(F32)


def _pool_diff(x_ref, r0, rp, has_prev, cur, prev, g):
    cols = slice(g * CH, (g + 1) * CH)
    xc = x_ref[pl.ds(r0, CH), cols]
    xp = x_ref[pl.ds(rp, CH), cols]
    ws = _dot(cur[g], xc) + has_prev * _dot(prev[g], xp)
    return ws * _pool_inv_count(r0, POOL_WINDOWS[g]) - xc.astype(F32)


def _pool_fwd(proj3, pw, pscale):
    nb, seq, _ = proj3.shape
    nch = seq // CH

    def body(x_ref, pw_ref, ps_ref, o_ref):
        cur, prev = _pool_bands()
        pwb = [pw_ref[g].astype(BF16) for g in range(NG)]

        def chunk(ch, carry):
            r0 = pl.multiple_of(ch * CH, CH)
            rp = pl.multiple_of(jnp.maximum(ch - 1, 0) * CH, CH)
            has_prev = jnp.where(ch > 0, 1.0, 0.0)
            for g in range(NG):
                cols = slice(g * CH, (g + 1) * CH)
                d = _pool_diff(x_ref, r0, rp, has_prev, cur, prev, g)
                y = _dot(d.astype(BF16), pwb[g]) * ps_ref[:, cols]
                o_ref[pl.ds(r0, CH), cols] = y.astype(BF16)
            return carry

        lax.fori_loop(0, nch, chunk, 0, unroll=2)

    return pl.pallas_call(
        body, name="pool_fwd", grid=(nb,),
        in_specs=[pl.BlockSpec((None, seq, BW), lambda b: (b, 0, 5)),
                  pl.BlockSpec((NG, CH, CH), lambda b: (0, 0, 0)),
                  pl.BlockSpec((1, BW), lambda b: (0, 0))],
        out_specs=pl.BlockSpec((None, seq, BW), lambda b: (b, 0, 0)),
        out_shape=jax.ShapeDtypeStruct((nb, seq, BW), BF16),
        compiler_params=_cp(("parallel",)),
    )(proj3, pw, pscale)


def _pool_bwd(proj3, dout3, pw, pscale, dproj3):
    nb, seq, _ = proj3.shape
    nch = seq // CH

    def body(x_ref, do_ref, pw_ref, ps_ref, buf_ref, dx_ref, gpw_ref, gps_ref, e_ref):
        @pl.when(pl.program_id(0) == 0)
        def _():
            gpw_ref[...] = jnp.zeros_like(gpw_ref)
            gps_ref[...] = jnp.zeros_like(gps_ref)

        cur, prev = _pool_bands()
        pwb = [pw_ref[g].astype(BF16) for g in range(NG)]

        def first(ch, carry):
            r0 = pl.multiple_of(ch * CH, CH)
            rp = pl.multiple_of(jnp.maximum(ch - 1, 0) * CH, CH)
            has_prev = jnp.where(ch > 0, 1.0, 0.0)
            for g in range(NG):
                cols = slice(g * CH, (g + 1) * CH)
                d = _pool_diff(x_ref, r0, rp, has_prev, cur, prev, g).astype(BF16)
                do = do_ref[pl.ds(r0, CH), cols].astype(F32)
                ypre = _dot(d, pwb[g])
                gps_ref[:, cols] += jnp.sum(do * ypre, axis=0, keepdims=True)
                dyp = (do * ps_ref[:, cols]).astype(BF16)
                gpw_ref[g] += _dot(d, dyp, TN)
                e_ref[pl.ds(r0, CH), cols] = _dot(dyp, pwb[g], NT)
            return carry

        lax.fori_loop(0, nch, first, 0, unroll=2)

        def second(ch, carry):
            r0 = pl.multiple_of(ch * CH, CH)
            rn = pl.multiple_of(jnp.minimum(ch + 1, nch - 1) * CH, CH)
            has_next = jnp.where(ch < nch - 1, 1.0, 0.0)
            for g in range(NG):
                cols = slice(g * CH, (g + 1) * CH)
                w = POOL_WINDOWS[g]
                dd = e_ref[pl.ds(r0, CH), cols]
                ec = (dd * _pool_inv_count(r0, w)).astype(BF16)
                en = (e_ref[pl.ds(rn, CH), cols] * _pool_inv_count(rn, w)).astype(BF16)
                dx = _dot(cur[g], ec, TN) + has_next * _dot(prev[g], en, TN) - dd
                dx_ref[pl.ds(r0, CH), cols] = dx.astype(BF16)
            return carry

        lax.fori_loop(0, nch, second, 0, unroll=2)

    small = pl.BlockSpec((NG, CH, CH), lambda b: (0, 0, 0))
    vec = pl.BlockSpec((1, BW), lambda b: (0, 0))
    return pl.pallas_call(
        body, name="pool_bwd", grid=(nb,),
        in_specs=[pl.BlockSpec((None, seq, BW), lambda b: (b, 0, 5)),
                  pl.BlockSpec((None, seq, BW), lambda b: (b, 0, 0)), small, vec, _HBM],
        out_specs=[pl.BlockSpec((None, seq, BW), lambda b: (b, 0, 5)), small, vec],
        out_shape=[jax.ShapeDtypeStruct((nb, seq, IN_COLS), BF16),
                   jax.ShapeDtypeStruct((NG, CH, CH), F32), jax.ShapeDtypeStruct((1, BW), F32)],
        input_output_aliases={4: 0},
        scratch_shapes=[pltpu.VMEM((seq, BW), F32)],
        compiler_params=_cp(("arbitrary",)),
    )(proj3, dout3, pw, pscale, dproj3)


SB_BQ = 256
SB_BK = 256
SB_SCALE = HD ** -0.5


SB_EXIT = -110.0


def _sb_tile(qs, k, mask):
    z = _dot(qs, k, NT)
    lb = jnp.minimum(z, 0.0) - jnp.log(1.0 + jnp.exp(-jnp.abs(z)))
    lom = lb - z
    if mask is not None:
        lom = jnp.where(mask, lom, 0.0)
    return lb, lom


def _sb_alive(c):
    top = functools.reduce(jnp.maximum, [jnp.max(state[1]) for state in c])
    return (top > SB_EXIT).astype(jnp.int32)


def _sb_past_blocks(step, c, npast):
    def cond(s):
        return jnp.logical_and(s[0] < npast, s[1] > 0)

    def body(s):
        i, _, c = s
        c = step(pl.multiple_of((npast - 1 - i) * SB_BK, SB_BK), c, None)
        return i + 1, _sb_alive(c), c

    return lax.while_loop(cond, body, (jnp.int32(0), _sb_alive(c), c))[2]


def _sb_diag_mask(bq, d):
    row = lax.broadcasted_iota(jnp.int32, (bq, SB_BK), 0)
    col = lax.broadcasted_iota(jnp.int32, (bq, SB_BK), 1)
    return col + d * SB_BK < row


def _sb_scaled(q):
    return (q.astype(F32) * SB_SCALE).astype(BF16)


def _dot_tri(a, m):
    return _dot(a.astype(BF16), m)


def _dot_tri2(a, m):
    hi = a.astype(BF16)
    lo = (a - hi.astype(F32)).astype(BF16)
    return _dot(hi, m) + _dot(lo, m)


def _sb_fwd(proj3, gather=()):
    nb, seq, _ = proj3.shape
    bq = min(SB_BQ, seq)
    nq = seq // bq
    ndiag = bq // SB_BK

    def body(q_ref, k_ref, v_ref, o_ref):
        row = lax.broadcasted_iota(jnp.int32, (SB_BK, SB_BK), 0)
        col = lax.broadcasted_iota(jnp.int32, (SB_BK, SB_BK), 1)
        upper = jnp.where(row > col, 1.0, 0.0).astype(BF16)
        heads = [slice(hh * HD, (hh + 1) * HD) for hh in range(2)]

        def qloop(qi, carry):
            q0 = pl.multiple_of(qi * bq, bq)
            qs = [_sb_scaled(q_ref[pl.ds(q0, bq), lanes]) for lanes in heads]

            def step(k0, c, mask):
                tiles = [_sb_tile(q, k_ref[pl.ds(k0, SB_BK), lanes], mask) for lanes, q in zip(heads, qs)]
                sums = [_dot_tri(lom, upper) for _, lom in tiles]
                out = []
                for lanes, (acc, cr), (lb, lom), cs in zip(heads, c, tiles, sums):
                    a = jnp.exp(lb + (cs + cr))
                    if mask is not None:
                        a = jnp.where(mask, a, 0.0)
                    rsum = cs[:, 0:1] + lom[:, 0:1]
                    out.append((acc + _dot(a.astype(BF16), v_ref[pl.ds(k0, SB_BK), lanes]), cr + rsum))
                return tuple(out)

            c = tuple((jnp.zeros((bq, HD), F32), jnp.zeros((bq, 1), F32)) for _ in heads)
            for d in reversed(range(ndiag)):
                c = step(pl.multiple_of(q0 + d * SB_BK, SB_BK), c, _sb_diag_mask(bq, d))
            c = _sb_past_blocks(step, c, qi * ndiag)
            for lanes, (acc, _) in zip(heads, c):
                o_ref[pl.ds(q0, bq), lanes] = acc
            return carry

        lax.fori_loop(0, nq, qloop, 0)

    def spec(c0):
        return pl.BlockSpec((None, seq, 128), lambda b, p: (b, 0, c0 + p))

    grid = (nb, BW // 128)
    body, ex_in, ex_out, ex_shape, ex_sems = _host_exchange(body, 3, 1, grid, gather, False)
    outs = pl.pallas_call(
        body, name="sb_fwd", grid=grid,
        in_specs=[spec(8), spec(12), spec(16)] + ex_in,
        out_specs=[spec(0)] + ex_out,
        out_shape=[jax.ShapeDtypeStruct((nb, seq, BW), F32)] + ex_shape,
        scratch_shapes=ex_sems,
        compiler_params=_cp(("arbitrary", "arbitrary")),
    )(proj3, proj3, proj3, *gather)
    return outs[0], outs[1:]


def _sb_bwd(proj3, do3, o3, scatter=()):
    nb, seq, _ = proj3.shape
    bq = min(SB_BQ, seq)
    nq = seq // bq
    ndiag = bq // SB_BK
    assert ndiag == 1

    def body(q_ref, k_ref, v_ref, do_ref, o_ref, dq_ref, dk_ref, dv_ref, dk_acc, dv_acc):
        row = lax.broadcasted_iota(jnp.int32, (SB_BK, SB_BK), 0)
        col = lax.broadcasted_iota(jnp.int32, (SB_BK, SB_BK), 1)
        upper = jnp.where(row > col, 1.0, 0.0).astype(BF16)
        later = jnp.where(row >= col, 1.0, 0.0).astype(BF16)
        heads = [slice(hh * HD, (hh + 1) * HD) for hh in range(2)]

        def qloop(qi, carry):
            q0 = pl.multiple_of(qi * bq, bq)
            qs = [_sb_scaled(q_ref[pl.ds(q0, bq), lanes]) for lanes in heads]
            dos = [do_ref[pl.ds(q0, bq), lanes] for lanes in heads]
            gtot = [jnp.sum(do.astype(F32) * o_ref[pl.ds(q0, bq), lanes], axis=1, keepdims=True)
                    for do, lanes in zip(dos, heads)]

            def step(k0, c, mask):
                ks = [k_ref[pl.ds(k0, SB_BK), lanes] for lanes in heads]
                tiles = [_sb_tile(q, k, mask) for q, k in zip(qs, ks)]
                sums = [_dot_tri(lom, upper) for _, lom in tiles]
                das = [_dot(do, v_ref[pl.ds(k0, SB_BK), lanes], NT) for do, lanes in zip(dos, heads)]
                gls, avs = [], []
                for hh, (_, cr, _) in enumerate(c):
                    a = jnp.exp(tiles[hh][0] + (sums[hh] + cr))
                    if mask is not None:
                        a = jnp.where(mask, a, 0.0)
                    ab = a.astype(BF16)
                    avs.append(ab)
                    gls.append(das[hh] * ab.astype(F32))
                tails = [_dot_tri2(gl, later) for gl in gls]
                out = []
                for hh, (dq, cr, gdone) in enumerate(c):
                    lb, lom = tiles[hh]
                    pre = gtot[hh] - gdone - tails[hh]
                    dz = gls[hh] - jnp.exp(lb) * (gls[hh] + pre)
                    if mask is not None:
                        dz = jnp.where(mask, dz, 0.0)
                    dz = dz.astype(BF16)
                    dk_new, dv_new = _dot(dz, qs[hh], TN), _dot(avs[hh], dos[hh], TN)
                    if mask is not None:
                        dk_acc[hh, pl.ds(k0, SB_BK), :] = dk_new
                        dv_acc[hh, pl.ds(k0, SB_BK), :] = dv_new
                    else:
                        dk_acc[hh, pl.ds(k0, SB_BK), :] += dk_new
                        dv_acc[hh, pl.ds(k0, SB_BK), :] += dv_new
                    rsum = sums[hh][:, 0:1] + lom[:, 0:1]
                    out.append((dq + _dot(dz, ks[hh]), cr + rsum, gdone + tails[hh][:, 0:1]))
                return tuple(out)

            c = tuple((jnp.zeros((bq, HD), F32), jnp.zeros((bq, 1), F32), jnp.zeros((bq, 1), F32))
                      for _ in heads)
            for d in reversed(range(ndiag)):
                c = step(pl.multiple_of(q0 + d * SB_BK, SB_BK), c, _sb_diag_mask(bq, d))
            c = _sb_past_blocks(step, c, qi * ndiag)
            for lanes, (dq, _, _) in zip(heads, c):
                dq_ref[pl.ds(q0, bq), lanes] = (dq * SB_SCALE).astype(BF16)
            return carry

        lax.fori_loop(0, nq, qloop, 0)
        for hh in range(2):
            lanes = slice(hh * HD, (hh + 1) * HD)
            dk_ref[:, lanes] = dk_acc[hh].astype(BF16)
            dv_ref[:, lanes] = dv_acc[hh].astype(BF16)

    def spec(c0):
        return pl.BlockSpec((None, seq, 128), lambda b, p: (b, 0, c0 + p))

    grid = (nb, BW // 128)
    body, ex_in, ex_out, ex_shape, ex_sems = _host_exchange(body, 5, 3, grid, scatter, True)
    outs = pl.pallas_call(
        body, name="sb_bwd", grid=grid,
        in_specs=[spec(8), spec(12), spec(16), spec(0), spec(0)] + ex_in,
        out_specs=[spec(0), spec(0), spec(0)] + ex_out,
        out_shape=[jax.ShapeDtypeStruct((nb, seq, BW), BF16)] * 3 + ex_shape,
        scratch_shapes=[pltpu.VMEM((2, seq, HD), F32), pltpu.VMEM((2, seq, HD), F32)] + ex_sems,
        compiler_params=_cp(("arbitrary", "arbitrary")),
    )(proj3, proj3, proj3, do3, o3, *scatter)
    return outs[:3], outs[3:]


def _merge_fwd(brs, wb, proj):
    t = proj.shape[0]
    tm = _row_tile(t, 512)
    tn = 512
    nj = D // tn

    def body(b0, b1, b2, wb_ref, l0, l1, l2, m_ref, y0, y1, y2):
        acc = None
        for br, n, lg, y_ref in ((b0, 0, l0, y0), (b1, 1, l1, y1), (b2, 2, l2, y2)):
            y = _dot(br[...].astype(BF16), wb_ref[n])
            y_ref[...] = y.astype(BF16)
            term = jax.nn.sigmoid(lg[...].astype(F32)) * y
            acc = term if acc is None else acc + term
        m_ref[...] = acc.astype(BF16)

    def lspec(n):
        return pl.BlockSpec((tm, tn), lambda i, j: (i, (3 * D + n * D) // tn + j))

    tile = pl.BlockSpec((tm, tn), lambda i, j: (i, j))
    bspec = pl.BlockSpec((tm, BW), lambda i, j: (i, 0))
    return pl.pallas_call(
        body, name="merge_fwd", grid=(t // tm, nj),
        in_specs=[bspec, bspec, bspec, pl.BlockSpec((NB, BW, tn), lambda i, j: (0, 0, j)),
                  lspec(0), lspec(1), lspec(2)],
        out_specs=[tile] * 4,
        out_shape=[jax.ShapeDtypeStruct((t, D), BF16)] * 4,
        compiler_params=_cp(("parallel", "parallel")),
    )(brs[0], brs[1], brs[2], wb, proj, proj, proj)


def _merge_bwd(dm, ys, proj):
    t = proj.shape[0]
    tm = _row_tile(t, 512)

    def body(dm_ref, y0, y1, y2, lg_ref, dp_ref, dy0, dy1, dy2):
        dmv = dm_ref[...].astype(F32)
        for n, (y_ref, dy_ref) in enumerate(((y0, dy0), (y1, dy1), (y2, dy2))):
            cols = slice(n * D, (n + 1) * D)
            g = jax.nn.sigmoid(lg_ref[:, cols].astype(F32))
            dp_ref[:, cols] = (dmv * y_ref[...].astype(F32) * g * (1.0 - g)).astype(BF16)
            dy_ref[...] = (dmv * g).astype(BF16)

    tile = pl.BlockSpec((tm, D), lambda i: (i, 0))
    gates = pl.BlockSpec((tm, NB * D), lambda i: (i, 1))
    return pl.pallas_call(
        body, name="merge_bwd", grid=(t // tm,),
        in_specs=[tile] * 4 + [gates],
        out_specs=[gates] + [tile] * 3,
        out_shape=[jax.ShapeDtypeStruct((t, IN_COLS), BF16)] + [jax.ShapeDtypeStruct((t, D), BF16)] * 3,
        compiler_params=_cp(("parallel",)),
    )(dm, ys[0], ys[1], ys[2], proj)


def _adamw_rows(rows):
    if rows <= 512:
        return rows
    return next(tr for tr in (512, 384, 352, 256, 128, 64, 32, 16, 8) if rows % tr == 0)


def _adamw_math(npart, p_ref, w_ref, m_ref, v_ref, g_ref, d_ref, mo_ref, vo_ref):
    c1 = 1.0 - ADAM_B1 ** ADAM_STEP
    c2 = 1.0 - ADAM_B2 ** ADAM_STEP
    g = p_ref[0].astype(F32)
    for p in range(1, npart):
        g = g + p_ref[p].astype(F32)
    mn = ADAM_B1 * m_ref[...] + (1.0 - ADAM_B1) * g
    vn = ADAM_B2 * v_ref[...] + (1.0 - ADAM_B2) * (g * g)
    m_hat = mn / c1
    v_hat = vn / c2
    g_ref[...] = g
    d_ref[...] = -ADAM_LR * (m_hat / (jnp.sqrt(v_hat) + ADAM_EPS) + ADAM_WD * w_ref[...])
    mo_ref[...] = mn
    vo_ref[...] = vn


def _adamw_layer(name, parts, w, m, v, layer, bufs, padded=False):
    nl, cols = w.shape[0], w.shape[-1]
    rows = int(math.prod(w.shape[1:-1]))
    npart = parts.shape[0]
    tr = _adamw_rows(rows)
    if padded:
        assert parts.shape[1] == rows // tr and parts.shape[2] >= tr
        parts_spec = pl.BlockSpec((npart, None, tr, cols), lambda i: (0, i, 0, 0))
    else:
        parts = parts.reshape(npart, rows, cols)
        parts_spec = pl.BlockSpec((npart, tr, cols), lambda i: (0, i, 0))
    if bufs is None:
        bufs = [lax.empty((nl, rows, cols), F32) for _ in range(4)]

    def body(p_ref, w_ref, m_ref, v_ref, b0, b1, b2, b3, g_ref, d_ref, mo_ref, vo_ref):
        _adamw_math(npart, p_ref, w_ref, m_ref, v_ref, g_ref, d_ref, mo_ref, vo_ref)

    slab = pl.BlockSpec((None, tr, cols), lambda i: (layer, i, 0))
    sds = jax.ShapeDtypeStruct((nl, rows, cols), F32)
    return pl.pallas_call(
        body, name=name, grid=(rows // tr,),
        in_specs=[parts_spec, slab, slab, slab] + [_HBM] * 4,
        out_specs=[slab] * 4, out_shape=[sds] * 4,
        input_output_aliases={4: 0, 5: 1, 6: 2, 7: 3},
        compiler_params=_cp(("parallel",)),
    )(parts, w.reshape(nl, rows, cols), m.reshape(nl, rows, cols), v.reshape(nl, rows, cols), *bufs)


def _adamw_reduce(name, parts, w, m, v):
    shape = w.shape
    cols = shape[-1]
    rows = int(math.prod(shape[:-1])) if len(shape) > 1 else 1
    npart = parts.shape[0]
    tr = _adamw_rows(rows)

    def body(p_ref, w_ref, m_ref, v_ref, g_ref, d_ref, mo_ref, vo_ref):
        _adamw_math(npart, p_ref, w_ref, m_ref, v_ref, g_ref, d_ref, mo_ref, vo_ref)

    tile = pl.BlockSpec((tr, cols), lambda i: (i, 0))
    sds = jax.ShapeDtypeStruct((rows, cols), F32)
    outs = pl.pallas_call(
        body, name=name, grid=(rows // tr,),
        in_specs=[pl.BlockSpec((npart, tr, cols), lambda i: (0, i, 0)), tile, tile, tile],
        out_specs=[tile] * 4, out_shape=[sds] * 4,
        compiler_params=_cp(("parallel",)),
    )(parts.reshape(npart, rows, cols), w.reshape(rows, cols), m.reshape(rows, cols), v.reshape(rows, cols))
    return tuple(o.reshape(shape) for o in outs)


def _pad_ffn_in(w):
    lead = w.shape[:-1]
    w = w.reshape(lead + (2, FF_HALF))
    w = jnp.pad(w, [(0, 0)] * len(lead) + [(0, 0), (0, FF_HALF_PAD - FF_HALF)])
    return w.reshape(lead + (FF_IN_PAD,))


def kernel(x, c, rms_g1, rms_g2, w_ada, b_ada, w_in, gm_ln_g, gm_ln_b, gm_w_spatial, gm_b_spatial, pool_w, pool_scale, w_branch, w_out, w_ffn_in, w_ffn_out, final_g, loss_target, m_rms_g1, m_rms_g2, m_w_ada, m_b_ada, m_w_in, m_gm_ln_g, m_gm_ln_b, m_gm_w_spatial, m_gm_b_spatial, m_pool_w, m_pool_scale, m_w_branch, m_w_out, m_w_ffn_in, m_w_ffn_out, m_final_g, v_rms_g1, v_rms_g2, v_w_ada, v_b_ada, v_w_in, v_gm_ln_g, v_gm_ln_b, v_gm_w_spatial, v_gm_b_spatial, v_pool_w, v_pool_scale, v_w_branch, v_w_out, v_w_ffn_in, v_w_ffn_out, v_final_g):
    nb, seq, _ = x.shape
    nl = w_in.shape[0]
    t = nb * seq
    ntot = NDEV * nb
    me = _my_index()
    assert x.shape[2] == D and w_in.shape[1:] == (D, 768) and w_ffn_in.shape[1:] == (D, FF_IN_SHARD)
    assert seq % CH == 0

    w_ffn_in_p = _pad_ffn_in(w_ffn_in).astype(BF16)
    w_ffn_out_p = jnp.pad(w_ffn_out, ((0, 0), (0, FF_HALF_PAD - FF_HALF), (0, 0))).astype(BF16)
    w_in_b = w_in.astype(BF16)
    w_branch_b = w_branch.astype(BF16)
    w_out_b = w_out.astype(BF16)
    (g_in_next,) = _exchange([w_in_b[0]], "gather_w_in0", False)

    (c_all,) = _exchange([c], "gather_c", False)
    c_all = c_all.reshape(ntot, D)
    b_blk = lax.dynamic_slice_in_dim(b_ada, me * 768, 768, axis=1).reshape(nl, 1, 768)
    mod_blk = _ada_fwd(c_all, w_ada, b_blk)
    (mod_all,) = _exchange([mod_blk], "gather_mod", False)
    mod_all = jnp.transpose(mod_all, (1, 2, 0, 3)).reshape(nl, ntot, NMOD * D)
    mod = lax.dynamic_slice_in_dim(mod_all, me * nb, nb, axis=1).reshape(nl, nb, NMOD, 1, D)

    saved = []
    gathered = []
    xc = x
    for l in range(nl):
        sh1, sc1, gt1, sh2, sc2, gt2 = [mod[l, :, i] for i in range(NMOD)]
        h = _norm_mod_fwd(xc, rms_g1[l].reshape(1, D), sc1, sh1).reshape(t, D)
        proj, (g_ffn_in_w,) = _mm_colblocked("proj_fwd", h, g_in_next, BF16, [w_ffn_in_p[l]])
        proj3 = proj.reshape(nb, seq, IN_COLS)
        br_gm = _gmlp_fwd(proj, gm_ln_g[l].reshape(1, BW), gm_ln_b[l].reshape(1, BW),
                          gm_w_spatial[l], gm_b_spatial[l].T)
        sb_o, got = _sb_fwd(proj3, [w_branch_b[l], w_out_b[l], w_ffn_out_p[l]]
                            + ([w_in_b[l + 1]] if l + 1 < nl else []))
        gw = dict(w_in=g_in_next,
                  w_branch=jnp.transpose(got[0], (1, 2, 0, 3)).reshape(NB, BW, D),
                  w_out=got[1].reshape(D, D),
                  w_ffn_in=g_ffn_in_w,
                  w_ffn_out=got[2].reshape(FFP, D))
        gathered.append(gw)
        if l + 1 < nl:
            g_in_next = got[3]
        br_pool = _pool_fwd(proj3, pool_w[l], pool_scale[l].reshape(1, BW))
        brs = [br_gm, sb_o.reshape(t, BW), br_pool.reshape(t, BW)]
        merged, y0, y1, y2 = _merge_fwd(brs, gw["w_branch"], proj)
        x_mid, mo = _mm_residual("out_fwd", merged, gw["w_out"], xc.reshape(t, D), gt1, seq)
        x_mid = x_mid.reshape(nb, seq, D)
        h2 = _norm_mod_fwd(x_mid, rms_g2[l].reshape(1, D), sc2, sh2).reshape(t, D)
        fg, fu, act, _ = _ffn_in_fwd(h2, gw["w_ffn_in"])
        x_out, fo = _mm_residual("ffn_out_fwd", act, gw["w_ffn_out"], x_mid.reshape(t, D), gt2, seq)
        saved.append(dict(x_in=xc, h=h, proj=proj, brs=brs, sb_o=sb_o, ys=(y0, y1, y2), merged=merged,
                          mo=mo, x_mid=x_mid, h2=h2, fg=fg, fu=fu, act=act, fo=fo))
        xc = x_out.reshape(nb, seq, D)

    dx, loss_part, dfinal_part, dfo, dgt2 = _loss_head(xc, loss_target, final_g.reshape(1, D),
                                                       saved[-1]["fo"].reshape(nb, seq, D), mod[nl - 1, :, 5])
    loss = lax.psum(jnp.sum(loss_part[:, 0, 0]), ("x", "y", "c"))

    big_names = ("w_in", "w_branch", "w_out", "w_ffn_in", "w_ffn_out")
    bufs = {name: None for name in big_names}
    w_ffn_in_t, m_w_ffn_in_t, v_w_ffn_in_t = [jnp.swapaxes(a, 1, 2) for a in (w_ffn_in, m_w_ffn_in, v_w_ffn_in)]
    small_parts = {k: [None] * nl for k in ("rms_g1", "rms_g2", "gm_ln_g", "gm_ln_b", "gm_w_spatial",
                                            "gm_b_spatial", "pool_w", "pool_scale")}
    dmod = [None] * nl
    names = list(small_parts)

    def stack_small(lo, hi):
        return [jnp.stack(small_parts[k][lo:hi]).astype(BF16 if k in ("gm_w_spatial", "pool_w") else F32)
                for k in names]

    for l in reversed(range(nl)):
        gw = gathered[l]
        sv = saved[l]
        sh1, sc1, gt1, sh2, sc2, gt2 = [mod[l, :, i] for i in range(NMOD)]
        dfo = dfo.reshape(t, D)
        if l == 0 and nl > 1:
            g_ffn_out, early_small = _mm_tn("ffn_out_wgrad", sv["act"], dfo, gather=stack_small(1, nl))
        else:
            g_ffn_out = _mm_tn("ffn_out_wgrad", sv["act"], dfo)
        dfg, dfu = _ffn_out_dgrad(dfo, gw["w_ffn_out"], sv["fg"], sv["fu"])
        dh2 = _ffn_in_dgrad(dfg, dfu, gw["w_ffn_in"])
        g_ffn_in = _ffn_in_wgrad(sv["h2"], dfg, dfu)
        dx_mid, dsh2, dsc2, dg2, dmo, dgt1 = _norm_mod_bwd(
            sv["x_mid"], dh2.reshape(nb, seq, D), dx, rms_g2[l].reshape(1, D), sc2,
            gate=(sv["mo"].reshape(nb, seq, D), gt1))
        dmo = dmo.reshape(t, D)
        dmerged = _mm_nt("out_dgrad", dmo, gw["w_out"], BF16)
        g_out = _mm_tn("out_wgrad", sv["merged"], dmo)
        dproj, *dys = _merge_bwd(dmerged, sv["ys"], sv["proj"])
        dbrs, g_br = [], []
        for n in range(NB):
            dbrs.append(_mm_nt("branch_dgrad", dys[n], gw["w_branch"], BF16, w_lead=n))
            g_br.append(_mm_tn("branch_wgrad", sv["brs"][n], dys[n]))
        proj3 = sv["proj"].reshape(nb, seq, IN_COLS)
        dproj, g_ws, g_bs, g_lg, g_lb = _gmlp_bwd(sv["proj"], dbrs[0], gm_ln_g[l].reshape(1, BW),
                                                  gm_ln_b[l].reshape(1, BW), gm_w_spatial[l], gm_b_spatial[l].T,
                                                  dproj)
        g_br_dev = jnp.transpose(jnp.stack(g_br).reshape(NB, BW, NDEV, D // NDEV), (2, 0, 1, 3))
        carried = [g_br_dev, g_out.reshape(NDEV, D // NDEV, D), g_ffn_in, g_ffn_out.reshape(NDEV, FF_HALF_PAD, D)]
        d_sb, recv = _sb_bwd(proj3, dbrs[1].reshape(nb, seq, BW), sv["sb_o"], carried)
        bufs["w_branch"] = _adamw_layer("adamw_w_branch", recv[0], w_branch, m_w_branch, v_w_branch, l,
                                        bufs["w_branch"])
        bufs["w_out"] = _adamw_layer("adamw_w_out", recv[1], w_out, m_w_out, v_w_out, l, bufs["w_out"])
        bufs["w_ffn_in"] = _adamw_layer("adamw_w_ffn_in", recv[2].reshape(NDEV, 2, FF_HALF_PAD, D), w_ffn_in_t,
                                        m_w_ffn_in_t, v_w_ffn_in_t, l, bufs["w_ffn_in"], padded=True)
        bufs["w_ffn_out"] = _adamw_layer("adamw_w_ffn_out", recv[3].reshape(NDEV, 1, FF_HALF_PAD, D), w_ffn_out,
                                         m_w_ffn_out, v_w_ffn_out, l, bufs["w_ffn_out"], padded=True)
        dproj3, g_pw, g_ps = _pool_bwd(proj3, dbrs[2].reshape(nb, seq, BW), pool_w[l], pool_scale[l].reshape(1, BW),
                                       dproj.reshape(nb, seq, IN_COLS))
        dproj = dproj3.reshape(t, IN_COLS)
        for i, piece in enumerate(d_sb):
            dproj = lax.dynamic_update_slice(dproj, piece.reshape(t, BW), (0, 2 * BW + i * BW))
        g_in = _mm_colblocked_tn("proj_wgrad", sv["h"], dproj)
        dh, (r_in,) = _mm_colblocked_nt("proj_dgrad", dproj, gw["w_in"], F32, [g_in])
        bufs["w_in"] = _adamw_layer("adamw_w_in", r_in, w_in, m_w_in, v_w_in, l, bufs["w_in"])
        dmod_tail = [dgt1, dsh2, dsc2, dgt2]
        if l > 0:
            dx, dsh1, dsc1, dg1, dfo, dgt2 = _norm_mod_bwd(
                sv["x_in"], dh.reshape(nb, seq, D), dx_mid, rms_g1[l].reshape(1, D), sc1,
                gate=(saved[l - 1]["fo"].reshape(nb, seq, D), mod[l - 1, :, 5]))
        else:
            dx, dsh1, dsc1, dg1 = _norm_mod_bwd(sv["x_in"], dh.reshape(nb, seq, D), dx_mid,
                                                rms_g1[l].reshape(1, D), sc1)

        dmod[l] = jnp.concatenate([dsh1, dsc1] + dmod_tail, axis=-1)
        small_parts["rms_g1"][l] = jnp.sum(dg1, axis=0)
        small_parts["rms_g2"][l] = jnp.sum(dg2, axis=0)
        small_parts["gm_ln_g"][l] = g_lg
        small_parts["gm_ln_b"][l] = g_lb
        small_parts["gm_w_spatial"][l] = g_ws
        small_parts["gm_b_spatial"][l] = g_bs[:, :, 0]
        small_parts["pool_w"][l] = g_pw
        small_parts["pool_scale"][l] = g_ps

    dmod_mine = jnp.stack(dmod).reshape(nl, nb, NMOD * D)
    gathered_small = _exchange(stack_small(0, 1) + [dfinal_part, dmod_mine], "gather_small", False)
    if nl > 1:
        gathered_small = [jnp.concatenate([late, early], axis=1)
                          for late, early in zip(gathered_small, early_small)] + list(gathered_small[-2:])
    dmod_all = jnp.transpose(gathered_small[-1], (1, 0, 2, 3)).reshape(nl, ntot, NMOD * D)
    dfinal_all = gathered_small[-2].reshape(ntot, D)

    results = {}
    weights = dict(rms_g1=(rms_g1, m_rms_g1, v_rms_g1), rms_g2=(rms_g2, m_rms_g2, v_rms_g2),
                   gm_ln_g=(gm_ln_g, m_gm_ln_g, v_gm_ln_g), gm_ln_b=(gm_ln_b, m_gm_ln_b, v_gm_ln_b),
                   gm_w_spatial=(gm_w_spatial, m_gm_w_spatial, v_gm_w_spatial),
                   gm_b_spatial=(gm_b_spatial, m_gm_b_spatial, v_gm_b_spatial),
                   pool_w=(pool_w, m_pool_w, v_pool_w), pool_scale=(pool_scale, m_pool_scale, v_pool_scale))
    for k, parts in zip(names, gathered_small[:len(names)]):
        w, m, v = weights[k]
        results[k] = _adamw_reduce("adamw_" + k, parts.reshape((NDEV,) + w.shape), w, m, v)
    results["final_g"] = _adamw_reduce("adamw_final_g", dfinal_all, final_g, m_final_g, v_final_g)
    results["b_ada"] = _adamw_reduce("adamw_b_ada", jnp.transpose(dmod_all, (1, 0, 2)), b_ada, m_b_ada, v_b_ada)
    dmod_blk = lax.dynamic_slice_in_dim(dmod_all, me * 768, 768, axis=2)
    g_w_ada = _ada_bwd(c_all, dmod_blk)
    results["w_ada"] = _adamw_reduce("adamw_w_ada", g_w_ada[None], w_ada, m_w_ada, v_w_ada)
    stacked_w = dict(w_in=w_in, w_branch=w_branch, w_out=w_out, w_ffn_in=w_ffn_in_t, w_ffn_out=w_ffn_out)
    for name in big_names:
        results[name] = tuple(b.reshape(stacked_w[name].shape) for b in bufs[name])
    results["w_ffn_in"] = tuple(jnp.swapaxes(b, 1, 2) for b in results["w_ffn_in"])

    order = ["rms_g1", "rms_g2", "w_ada", "b_ada", "w_in", "gm_ln_g", "gm_ln_b", "gm_w_spatial", "gm_b_spatial",
             "pool_w", "pool_scale", "w_branch", "w_out", "w_ffn_in", "w_ffn_out", "final_g"]
    out = [loss, dx]
    for i in range(4):
        out.extend(results[k][i] for k in order)
    return tuple(out)
```
